```python
import math
import jax, jax.numpy as jnp
from jax import lax
import numpy as np


D_MODEL = 1024
BATCH = 2
SEQ = 8192
DEPTH = 1

MEM_LEN = 256
CONV_CH = 512
CONV_WIDTH = 31
N_HEADS = 8
N_KV_HEADS = 2
HEAD_DIM = 64
GQ = N_HEADS // N_KV_HEADS
ATTN_W = N_HEADS * HEAD_DIM
KV_W = N_KV_HEADS * HEAD_DIM
WINDOW = 128
BLOCK = 128
D_MIX = CONV_CH + ATTN_W
D_IN = 2 * CONV_CH + ATTN_W + 2 * KV_W
MEM_HEADS = 4
MEM_HEAD_DIM = D_MODEL // MEM_HEADS
N_GROUPS = 4
EXPERTS_PER_GROUP = 4
N_EXPERTS = N_GROUPS * EXPERTS_PER_GROUP
TOP_K = 2
D_EXPERT = D_MODEL // 2
ALPHA = (2.0 * DEPTH) ** 0.25
BETA = (8.0 * DEPTH) ** -0.25
LN_EPS = 1e-5

kernel_name = 'hybrid_conformer_swa_sink_memory_hmoe'


def layer_norm(x, g, b):
    xf = x.astype(jnp.float32)
    mu = jnp.mean(xf, axis=-1, keepdims=True)
    var = jnp.mean(jnp.square(xf - mu), axis=-1, keepdims=True)
    return ((xf - mu) * lax.rsqrt(var + LN_EPS) * g + b).astype(x.dtype)


def conformer_conv(u, w_dw, b_dw, g_cn, b_cn):
    a, gate = jnp.split(u, 2, axis=-1)
    h = a * jax.nn.sigmoid(gate)
    h = lax.conv_general_dilated(
        h, w_dw[:, None, :].astype(h.dtype), window_strides=(1,),
        padding=[(CONV_WIDTH - 1, 0)],
        dimension_numbers=('NWC', 'WIO', 'NWC'),
        feature_group_count=h.shape[-1]) + b_dw
    h = layer_norm(h, g_cn, b_cn)
    return jax.nn.silu(h)


def sliding_window_sink_attention(q, k, v, sinks):
    B, S, _ = q.shape
    nb = S // BLOCK
    qb = q.reshape(B, nb, BLOCK, N_KV_HEADS, GQ, HEAD_DIM)

    def band(t):
        t = t.reshape(B, S, N_KV_HEADS, HEAD_DIM)
        t = jnp.pad(t, ((0, 0), (BLOCK, 0), (0, 0), (0, 0)))
        t = t.reshape(B, nb + 1, BLOCK, N_KV_HEADS, HEAD_DIM)
        return jnp.concatenate([t[:, :-1], t[:, 1:]], axis=2)

    kb, vb = band(k), band(v)
    s = jnp.einsum('bnqhgd,bnkhd->bnhgqk', qb, kb).astype(jnp.float32) * (HEAD_DIM ** -0.5)
    qi = jnp.arange(BLOCK)[:, None]
    kj = jnp.arange(2 * BLOCK)[None, :]
    dist = qi + BLOCK - kj
    blk = jnp.arange(nb)[:, None, None]
    valid = (dist >= 0) & (dist < WINDOW) & (blk * BLOCK - BLOCK + kj >= 0)
    s = jnp.where(valid[None, :, None, None], s, -jnp.inf)
    sink = jnp.broadcast_to(sinks.astype(jnp.float32).reshape(1, 1, N_KV_HEADS, GQ, 1, 1),
                            s.shape[:-1] + (1,))
    p = jax.nn.softmax(jnp.concatenate([s, sink], axis=-1), axis=-1)[..., :-1]
    o = jnp.einsum('bnhgqk,bnkhd->bnqhgd', p.astype(vb.dtype), vb)
    return o.reshape(B, S, ATTN_W)


def memory_cross_attention(x, mem, w_q, w_kv, w_o):
    B, S, _ = x.shape
    q = (x @ w_q).reshape(B, S, MEM_HEADS, MEM_HEAD_DIM)
    k, v = jnp.split(mem @ w_kv, 2, axis=-1)
    k = k.reshape(B, -1, MEM_HEADS, MEM_HEAD_DIM)
    v = v.reshape(B, -1, MEM_HEADS, MEM_HEAD_DIM)
    s = jnp.einsum('bshd,bmhd->bhsm', q, k).astype(jnp.float32) * (MEM_HEAD_DIM ** -0.5)
    p = jax.nn.softmax(s, axis=-1).astype(v.dtype)
    o = jnp.einsum('bhsm,bmhd->bshd', p, v).reshape(B, S, D_MODEL)
    return o @ w_o


def hierarchical_moe(x, w_group, b_group, w_router, b_router, w_gate, w_up, w_down):
    B, S, D = x.shape
    t = x.reshape(B * S, D)
    g_prob = jax.nn.softmax((t @ w_group).astype(jnp.float32) + b_group, axis=-1)
    g_p, g_idx = lax.top_k(g_prob, 1)
    e_logits = jnp.einsum('td,gde->tge', t, w_router).astype(jnp.float32) + b_router
    e_logits = jnp.take_along_axis(e_logits, g_idx[:, :, None], axis=1)[:, 0]
    e_top, e_idx = lax.top_k(e_logits, TOP_K)
    gate = g_p * jax.nn.softmax(e_top, axis=-1)
    expert_id = g_idx * EXPERTS_PER_GROUP + e_idx
    combine = jnp.sum(jax.nn.one_hot(expert_id, N_EXPERTS, dtype=jnp.float32) * gate[..., None], axis=1)
    h = jax.nn.silu(jnp.einsum('td,edf->tef', t, w_gate)) * jnp.einsum('td,edf->tef', t, w_up)
    h = h * combine[..., None].astype(h.dtype)
    y = jnp.einsum('tef,efd->td', h, w_down)
    return y.reshape(B, S, D)


def setup_inputs(seed: int = 0) -> dict:
    key = jax.random.key(seed)
    ks = jax.random.split(key, 26)
    L = DEPTH

    def nrm(k, shape, scale):
        return jax.random.normal(k, shape, jnp.float32) * scale

    return {
        'x': nrm(ks[0], (BATCH, SEQ, D_MODEL), 1.0),
        'mem': nrm(ks[1], (BATCH, MEM_LEN, D_MODEL), 1.0),
        'w_in': nrm(ks[2], (L, D_MODEL, D_IN), D_MODEL ** -0.5),
        'b_in': nrm(ks[3], (L, D_IN), 0.01),
        'w_dw': nrm(ks[4], (L, CONV_WIDTH, CONV_CH), CONV_WIDTH ** -0.5),
        'b_dw': nrm(ks[5], (L, CONV_CH), 0.01),
        'g_conv_norm': 1.0 + nrm(ks[6], (L, CONV_CH), 0.02),
        'b_conv_norm': nrm(ks[7], (L, CONV_CH), 0.01),
        'attn_sinks': nrm(ks[8], (L, N_HEADS), 0.5),
        'w_out': nrm(ks[9], (L, D_MIX, D_MODEL), BETA * D_MIX ** -0.5),
        'g_ln1': 1.0 + nrm(ks[10], (L, D_MODEL), 0.02),
        'b_ln1': nrm(ks[11], (L, D_MODEL), 0.01),
        'w_mq': nrm(ks[12], (L, D_MODEL, D_MODEL), D_MODEL ** -0.5),
        'w_mkv': nrm(ks[13], (L, D_MODEL, 2 * D_MODEL), D_MODEL ** -0.5),
        'w_mo': nrm(ks[14], (L, D_MODEL, D_MODEL), BETA * D_MODEL ** -0.5),
        'g_ln2': 1.0 + nrm(ks[15], (L, D_MODEL), 0.02),
        'b_ln2': nrm(ks[16], (L, D_MODEL), 0.01),
        'w_group': nrm(ks[17], (L, D_MODEL, N_GROUPS), D_MODEL ** -0.5),
        'b_group': nrm(ks[18], (L, N_GROUPS), 0.01),
        'w_router': nrm(ks[19], (L, N_GROUPS, D_MODEL, EXPERTS_PER_GROUP), D_MODEL ** -0.5),
        'b_router': nrm(ks[20], (L, N_GROUPS, EXPERTS_PER_GROUP), 0.01),
        'w_gate': nrm(ks[21], (L, N_EXPERTS, D_MODEL, D_EXPERT), D_MODEL ** -0.5),
        'w_up': nrm(ks[22], (L, N_EXPERTS, D_MODEL, D_EXPERT), D_MODEL ** -0.5),
        'w_down': nrm(ks[23], (L, N_EXPERTS, D_EXPERT, D_MODEL), BETA * D_EXPERT ** -0.5),
        'g_ln3': 1.0 + nrm(ks[24], (L, D_MODEL), 0.02),
        'b_ln3': nrm(ks[25], (L, D_MODEL), 0.01),
    }


def reference(x, mem, w_in, b_in, w_dw, b_dw, g_conv_norm, b_conv_norm, attn_sinks, w_out,
              g_ln1, b_ln1, w_mq, w_mkv, w_mo, g_ln2, b_ln2, w_group, b_group, w_router,
              b_router, w_gate, w_up, w_down, g_ln3, b_ln3):
    splits = [2 * CONV_CH, 2 * CONV_CH + ATTN_W, 2 * CONV_CH + ATTN_W + KV_W]
    for l in range(DEPTH):
        u = x @ w_in[l] + b_in[l]
        u_conv, q, k, v = jnp.split(u, splits, axis=-1)
        y_conv = conformer_conv(u_conv, w_dw[l], b_dw[l], g_conv_norm[l], b_conv_norm[l])
        y_attn = sliding_window_sink_attention(q, k, v, attn_sinks[l])
        mix = jnp.concatenate([y_conv, y_attn], axis=-1) @ w_out[l]
        x = layer_norm(ALPHA * x + mix, g_ln1[l], b_ln1[l])
        x = layer_norm(ALPHA * x + memory_cross_attention(x, mem, w_mq[l], w_mkv[l], w_mo[l]),
                       g_ln2[l], b_ln2[l])
        x = layer_norm(ALPHA * x + hierarchical_moe(x, w_group[l], b_group[l], w_router[l], b_router[l],
                                                    w_gate[l], w_up[l], w_down[l]),
                       g_ln3[l], b_ln3[l])
    return x
```

```python
import functools

import jax
import jax.numpy as jnp
from jax import lax
from jax.experimental import pallas as pl
from jax.experimental.pallas import tpu as pltpu

D_MODEL = 1024
MEM_LEN = 256
CONV_CH = 512
CONV_WIDTH = 31
N_HEADS = 8
N_KV_HEADS = 2
HEAD_DIM = 64
GQ = N_HEADS // N_KV_HEADS
ATTN_W = N_HEADS * HEAD_DIM
KV_W = N_KV_HEADS * HEAD_DIM
WINDOW = 128
D_MIX = CONV_CH + ATTN_W
D_IN = 2 * CONV_CH + ATTN_W + 2 * KV_W
MEM_HEADS = 4
MEM_HEAD_DIM = D_MODEL // MEM_HEADS
N_GROUPS = 4
EXPERTS_PER_GROUP = 4
N_EXPERTS = N_GROUPS * EXPERTS_PER_GROUP
D_EXPERT = D_MODEL // 2
DEPTH = 1
ALPHA = (2.0 * DEPTH) ** 0.25
LN_EPS = 1e-5

LANES = 128
MASK_VALUE = -1e30
CONV_HALO = 32
SEQ_TILE = 512
MOE_TILE = 1024
ROUTE_OFF = N_GROUPS
VMEM_LIMIT = 56 * 1024 * 1024

BF16 = jnp.bfloat16
F32 = jnp.float32


def _layer_norm(x, g, b):
    mu = jnp.mean(x, axis=-1, keepdims=True)
    xc = x - mu
    var = jnp.mean(xc * xc, axis=-1, keepdims=True)
    return xc * lax.rsqrt(var + LN_EPS) * g + b


def _dot_nt(a, b):
    return lax.dot_general(a, b, (((1,), (1,)), ((), ())), preferred_element_type=F32)


def _mixer_kernel(sinks_ref, x_ref, w_in_ref, b_in_ref, w_dw_ref, b_dw_ref, g_cn_ref, b_cn_ref,
                  w_out_ref, g1_ref, b1_ref, o_ref, hbuf, qbuf, kbuf, vbuf, ymix):
    i = pl.program_id(1)
    ts = SEQ_TILE

    @pl.when(i == 0)
    def _():
        hbuf[0:CONV_HALO, :] = jnp.zeros((CONV_HALO, CONV_CH), F32)
        kbuf[0:WINDOW, :] = jnp.zeros((WINDOW, KV_W), BF16)
        vbuf[0:WINDOW, :] = jnp.zeros((WINDOW, KV_W), BF16)

    x = x_ref[0]
    u = jnp.dot(x.astype(BF16), w_in_ref[...], preferred_element_type=F32) + b_in_ref[...]
    a = u[:, 0:CONV_CH]
    gate = u[:, CONV_CH:2 * CONV_CH]
    hbuf[CONV_HALO:CONV_HALO + ts, :] = a * jax.nn.sigmoid(gate)
    qbuf[...] = (u[:, 2 * CONV_CH:2 * CONV_CH + ATTN_W] * (HEAD_DIM ** -0.5)).astype(BF16)
    kbuf[WINDOW:WINDOW + ts, :] = u[:, 2 * CONV_CH + ATTN_W:2 * CONV_CH + ATTN_W + KV_W].astype(BF16)
    vbuf[WINDOW:WINDOW + ts, :] = u[:, 2 * CONV_CH + ATTN_W + KV_W:D_IN].astype(BF16)

    rc = 64
    base = CONV_HALO - (CONV_WIDTH - 1)

    for c in range(ts // rc):
        r0 = c * rc
        acc = jnp.zeros((rc, CONV_CH), F32)
        for j in range(CONV_WIDTH):
            acc = acc + hbuf[r0 + base + j:r0 + base + j + rc, :] * w_dw_ref[j:j + 1, :]
        y = _layer_norm(acc + b_dw_ref[...], g_cn_ref[...], b_cn_ref[...])
        y = y * jax.nn.sigmoid(y)
        ymix[r0:r0 + rc, 0:CONV_CH] = y.astype(BF16)

    qi = lax.broadcasted_iota(jnp.int32, (WINDOW, 2 * WINDOW), 0)
    kj = lax.broadcasted_iota(jnp.int32, (WINDOW, 2 * WINDOW), 1)
    dist = qi + WINDOW - kj
    band = (dist >= 0) & (dist < WINDOW)

    def attn_block(jb, carry):
        r0 = pl.multiple_of(jb * WINDOW, WINDOW)
        first = jnp.logical_and(i == 0, jb == 0)
        valid = band & jnp.logical_or(jnp.logical_not(first), kj >= WINDOW)
        kk = kbuf[pl.ds(r0, 2 * WINDOW), :]
        vv = vbuf[pl.ds(r0, 2 * WINDOW), :]
        qq = qbuf[pl.ds(r0, WINDOW), :]
        outs = []
        for hd in range(N_HEADS):
            kvh = hd // GQ
            qh = qq[:, hd * HEAD_DIM:(hd + 1) * HEAD_DIM]
            kh = kk[:, kvh * HEAD_DIM:(kvh + 1) * HEAD_DIM]
            vh = vv[:, kvh * HEAD_DIM:(kvh + 1) * HEAD_DIM]
            s = jnp.where(valid, _dot_nt(qh, kh), MASK_VALUE)
            sink = sinks_ref[hd]
            m = jnp.maximum(jnp.max(s, axis=-1, keepdims=True), sink)
            p = jnp.exp(s - m)
            denom = jnp.sum(p, axis=-1, keepdims=True) + jnp.exp(sink - m)
            o = jnp.dot(p.astype(BF16), vh, preferred_element_type=F32)
            outs.append((o / denom).astype(BF16))
        ymix[pl.ds(r0, WINDOW), CONV_CH:D_MIX] = jnp.concatenate(outs, axis=-1)
        return carry

    lax.fori_loop(0, ts // WINDOW, attn_block, 0)

    mix = jnp.dot(ymix[...], w_out_ref[...], preferred_element_type=F32)
    o_ref[0] = _layer_norm(ALPHA * x + mix, g1_ref[...], b1_ref[...])

    hbuf[0:CONV_HALO, :] = hbuf[ts:ts + CONV_HALO, :]
    kbuf[0:WINDOW, :] = kbuf[ts:ts + WINDOW, :]
    vbuf[0:WINDOW, :] = vbuf[ts:ts + WINDOW, :]


def _const_spec(shape):
    nd = len(shape)
    return pl.BlockSpec(shape, lambda *_: (0,) * nd)


def _mixer(x, sinks, w_in, b_in, w_dw, b_dw, g_cn, b_cn, w_out, g1, b1):
    B, S, D = x.shape
    ts = SEQ_TILE
    tile = pl.BlockSpec((1, ts, D), lambda b, i: (b, i, 0))
    return pl.pallas_call(
        _mixer_kernel,
        grid=(B, S // ts),
        in_specs=[
            pl.BlockSpec(memory_space=pltpu.SMEM),
            tile,
            _const_spec((D, D_IN)), _const_spec((1, D_IN)),
            _const_spec((CONV_HALO, CONV_CH)), _const_spec((1, CONV_CH)),
            _const_spec((1, CONV_CH)), _const_spec((1, CONV_CH)),
            _const_spec((D_MIX, D)), _const_spec((1, D)), _const_spec((1, D)),
        ],
        out_specs=tile,
        out_shape=jax.ShapeDtypeStruct((B, S, D), F32),
        scratch_shapes=[
            pltpu.VMEM((CONV_HALO + ts, CONV_CH), F32),
            pltpu.VMEM((ts, ATTN_W), BF16),
            pltpu.VMEM((WINDOW + ts, KV_W), BF16),
            pltpu.VMEM((WINDOW + ts, KV_W), BF16),
            pltpu.VMEM((ts, D_MIX), BF16),
        ],
        compiler_params=pltpu.CompilerParams(
            dimension_semantics=("arbitrary", "arbitrary"), vmem_limit_bytes=VMEM_LIMIT),
        name="mixer",
    )(sinks, x, w_in, b_in, w_dw, b_dw, g_cn, b_cn, w_out, g1, b1)


def _memkv_kernel(mem_ref, w_ref, o_ref):
    o_ref[...] = jnp.dot(mem_ref[...].astype(BF16), w_ref[...], preferred_element_type=F32).astype(BF16)


def _memkv(mem2d, w_mkv):
    M, D = mem2d.shape
    N = w_mkv.shape[1]
    tn = 512
    return pl.pallas_call(
        _memkv_kernel,
        grid=(N // tn,),
        in_specs=[pl.BlockSpec((M, D), lambda j: (0, 0)), pl.BlockSpec((D, tn), lambda j: (0, j))],
        out_specs=pl.BlockSpec((M, tn), lambda j: (0, j)),
        out_shape=jax.ShapeDtypeStruct((M, N), BF16),
        compiler_params=pltpu.CompilerParams(dimension_semantics=("arbitrary",)),
        name="memkv",
    )(mem2d, w_mkv)


def _route(logits):
    lane = lax.broadcasted_iota(jnp.int32, logits.shape, 1)
    big = jnp.int32(LANES)
    gl = jnp.where(lane < N_GROUPS, logits, MASK_VALUE)
    gmax = jnp.max(gl, axis=-1, keepdims=True)
    g_p = 1.0 / jnp.sum(jnp.exp(gl - gmax), axis=-1, keepdims=True)
    g_idx = jnp.min(jnp.where(gl == gmax, lane, big), axis=-1, keepdims=True)
    lo = ROUTE_OFF + g_idx * EXPERTS_PER_GROUP
    sel = (lane >= lo) & (lane < lo + EXPERTS_PER_GROUP)
    rl = jnp.where(sel, logits, MASK_VALUE)
    m1 = jnp.max(rl, axis=-1, keepdims=True)
    i1 = jnp.min(jnp.where(rl == m1, lane, big), axis=-1, keepdims=True)
    rl2 = jnp.where(lane == i1, MASK_VALUE, rl)
    m2 = jnp.max(rl2, axis=-1, keepdims=True)
    i2 = jnp.min(jnp.where(rl2 == m2, lane, big), axis=-1, keepdims=True)
    e2 = jnp.exp(m2 - m1)
    w1 = 1.0 / (1.0 + e2)
    w2 = e2 * w1
    return g_p * (jnp.where(lane == i1, w1, 0.0) + jnp.where(lane == i2, w2, 0.0))


def _memattn_kernel(x_ref, wq_ref, k_ref, v_ref, wo_ref, g2_ref, b2_ref, wr_ref, br_ref,
                    o_ref, ob_ref, comb_ref):
    x = x_ref[0]
    q = jnp.dot(x.astype(BF16), wq_ref[...], preferred_element_type=F32)
    q = (q * (MEM_HEAD_DIM ** -0.5)).astype(BF16)
    outs = []
    for h in range(MEM_HEADS):
        sl = slice(h * MEM_HEAD_DIM, (h + 1) * MEM_HEAD_DIM)
        s = _dot_nt(q[:, sl], k_ref[0, :, sl])
        m = jnp.max(s, axis=-1, keepdims=True)
        p = jnp.exp(s - m)
        denom = jnp.sum(p, axis=-1, keepdims=True)
        o = jnp.dot(p.astype(BF16), v_ref[0, :, sl], preferred_element_type=F32)
        outs.append((o / denom).astype(BF16))
    o = jnp.dot(jnp.concatenate(outs, axis=-1), wo_ref[...], preferred_element_type=F32)
    x2 = _layer_norm(ALPHA * x + o, g2_ref[...], b2_ref[...])
    o_ref[0] = x2
    ob_ref[0] = x2.astype(BF16)
    logits = jnp.dot(x2, wr_ref[...], preferred_element_type=F32,
                     precision=lax.Precision.HIGHEST) + br_ref[...]
    comb_ref[0] = _route(logits)


def _memattn(x1, wq, kmem, vmem, wo, g2, b2, wr, br):
    B, S, D = x1.shape
    ts = SEQ_TILE
    tile = pl.BlockSpec((1, ts, D), lambda b, i: (b, i, 0))
    kv = pl.BlockSpec((1, MEM_LEN, D), lambda b, i: (b, 0, 0))
    return pl.pallas_call(
        _memattn_kernel,
        grid=(B, S // ts),
        in_specs=[tile, _const_spec((D, D)), kv, kv, _const_spec((D, D)),
                  _const_spec((1, D)), _const_spec((1, D)),
                  _const_spec((D, LANES)), _const_spec((1, LANES))],
        out_specs=[tile, tile, pl.BlockSpec((1, ts, LANES), lambda b, i: (b, i, 0))],
        out_shape=[jax.ShapeDtypeStruct((B, S, D), F32),
                   jax.ShapeDtypeStruct((B, S, D), BF16),
                   jax.ShapeDtypeStruct((B, S, LANES), F32)],
        compiler_params=pltpu.CompilerParams(
            dimension_semantics=("arbitrary", "arbitrary"), vmem_limit_bytes=VMEM_LIMIT),
        name="memattn",
    )(x1, wq, kmem, vmem, wo, g2, b2, wr, br)


def _moe_kernel(x_ref, xb_ref, comb_ref, wg_ref, wu_ref, wd_ref, g3_ref, b3_ref, o_ref, acc):
    e = pl.program_id(1)

    @pl.when(e == 0)
    def _():
        acc[...] = jnp.zeros_like(acc)

    xb = xb_ref[...]
    comb = comb_ref[...]
    lane = lax.broadcasted_iota(jnp.int32, comb.shape, 1)
    c = jnp.sum(jnp.where(lane == e + ROUTE_OFF, comb, 0.0), axis=-1, keepdims=True)
    g = jnp.dot(xb, wg_ref[0], preferred_element_type=F32)
    u = jnp.dot(xb, wu_ref[0], preferred_element_type=F32)
    h = (g * jax.nn.sigmoid(g)) * u * c
    acc[...] += jnp.dot(h.astype(BF16), wd_ref[0], preferred_element_type=F32)

    @pl.when(e == N_EXPERTS - 1)
    def _():
        o_ref[...] = _layer_norm(ALPHA * x_ref[...] + acc[...], g3_ref[...], b3_ref[...])


def _moe(x2, x2b, comb, wg, wu, wd, g3, b3):
    T, D = x2.shape
    tm = MOE_TILE
    return pl.pallas_call(
        _moe_kernel,
        grid=(T // tm, N_EXPERTS),
        in_specs=[
            pl.BlockSpec((tm, D), lambda i, e: (i, 0)),
            pl.BlockSpec((tm, D), lambda i, e: (i, 0)),
            pl.BlockSpec((tm, LANES), lambda i, e: (i, 0)),
            pl.BlockSpec((1, D, D_EXPERT), lambda i, e: (e, 0, 0)),
            pl.BlockSpec((1, D, D_EXPERT), lambda i, e: (e, 0, 0)),
            pl.BlockSpec((1, D_EXPERT, D), lambda i, e: (e, 0, 0)),
            _const_spec((1, D)), _const_spec((1, D)),
        ],
        out_specs=pl.BlockSpec((tm, D), lambda i, e: (i, 0)),
        out_shape=jax.ShapeDtypeStruct((T, D), F32),
        scratch_shapes=[pltpu.VMEM((tm, D), F32)],
        compiler_params=pltpu.CompilerParams(
            dimension_semantics=("arbitrary", "arbitrary"), vmem_limit_bytes=VMEM_LIMIT),
        name="moe",
    )(x2, x2b, comb, wg, wu, wd, g3, b3)


def _row(v):
    return v.reshape(1, -1).astype(F32)


def kernel(x, mem, w_in, b_in, w_dw, b_dw, g_conv_norm, b_conv_norm, attn_sinks, w_out, g_ln1, b_ln1,
           w_mq, w_mkv, w_mo, g_ln2, b_ln2, w_group, b_group, w_router, b_router, w_gate, w_up, w_down,
           g_ln3, b_ln3):
    B, S, D = x.shape
    for l in range(DEPTH):
        w_dw_p = jnp.zeros((CONV_HALO, CONV_CH), F32).at[:CONV_WIDTH].set(w_dw[l])
        x1 = _mixer(x, attn_sinks[l].astype(F32), w_in[l].astype(BF16), _row(b_in[l]), w_dw_p,
                    _row(b_dw[l]), _row(g_conv_norm[l]), _row(b_conv_norm[l]),
                    w_out[l].astype(BF16), _row(g_ln1[l]), _row(b_ln1[l]))

        kvm = _memkv(mem.reshape(B * MEM_LEN, D), w_mkv[l].astype(BF16)).reshape(B, MEM_LEN, 2 * D)
        kmem, vmem = kvm[:, :, :D], kvm[:, :, D:]

        wr = jnp.concatenate(
            [w_group[l], jnp.transpose(w_router[l], (1, 0, 2)).reshape(D, N_EXPERTS)], axis=1)
        wr = jnp.pad(wr, ((0, 0), (0, LANES - wr.shape[1])))
        br = jnp.pad(jnp.concatenate([b_group[l], b_router[l].reshape(-1)]), (0, LANES - N_GROUPS - N_EXPERTS))
        x2, x2b, comb = _memattn(x1, w_mq[l].astype(BF16), kmem, vmem, w_mo[l].astype(BF16),
                                 _row(g_ln2[l]), _row(b_ln2[l]), wr.astype(F32), _row(br))

        T = B * S
        y = _moe(x2.reshape(T, D), x2b.reshape(T, D), comb.reshape(T, LANES),
                 w_gate[l].astype(BF16), w_up[l].astype(BF16), w_down[l].astype(BF16),
                 _row(g_ln3[l]), _row(b_ln3[l]))
        x = y.reshape(B, S, D)
    return x
```

```python
import functools

import jax
import jax.numpy as jnp
from jax import lax
from jax.experimental import pallas as pl
from jax.experimental.pallas import tpu as pltpu

D_MODEL = 1024
MEM_LEN = 256
CONV_CH = 512
CONV_WIDTH = 31
N_HEADS = 8
N_KV_HEADS = 2
HEAD_DIM = 64
GQ = N_HEADS // N_KV_HEADS
ATTN_W = N_HEADS * HEAD_DIM
KV_W = N_KV_HEADS * HEAD_DIM
WINDOW = 128
D_MIX = CONV_CH + ATTN_W
D_IN = 2 * CONV_CH + ATTN_W + 2 * KV_W
MEM_HEADS = 4
MEM_HEAD_DIM = D_MODEL // MEM_HEADS
N_GROUPS = 4
EXPERTS_PER_GROUP = 4
N_EXPERTS = N_GROUPS * EXPERTS_PER_GROUP
D_EXPERT = D_MODEL // 2
DEPTH = 1
ALPHA = (2.0 * DEPTH) ** 0.25
LN_EPS = 1e-5

LANES = 128
MASK_VALUE = -1e30
CONV_HALO = 32
SEQ_TILE = 512
MOE_TILE = 512
CHUNK = 16
GMM_ROWS = 256
TOP_K = 2
SORT_ROWS = -(-(MOE_TILE * TOP_K + N_EXPERTS * (CHUNK - 1)) // 256) * 256
ROUTE_OFF = N_GROUPS
VMEM_LIMIT = 56 * 1024 * 1024

BF16 = jnp.bfloat16
F32 = jnp.float32


def _layer_norm(x, g, b):
    mu = jnp.mean(x, axis=-1, keepdims=True)
    xc = x - mu
    var = jnp.mean(xc * xc, axis=-1, keepdims=True)
    return xc * lax.rsqrt(var + LN_EPS) * g + b


def _dot_nt(a, b):
    return lax.dot_general(a, b, (((1,), (1,)), ((), ())), preferred_element_type=F32)


def _mixer_kernel(sinks_ref, x_ref, w_in_ref, b_in_ref, w_dw_ref, b_dw_ref, g_cn_ref, b_cn_ref,
                  w_out_ref, g1_ref, b1_ref, o_ref, hbuf, qbuf, kbuf, vbuf, ymix):
    i = pl.program_id(1)
    ts = SEQ_TILE

    @pl.when(i == 0)
    def _():
        hbuf[0:CONV_HALO, :] = jnp.zeros((CONV_HALO, CONV_CH), F32)
        kbuf[0:WINDOW, :] = jnp.zeros((WINDOW, KV_W), BF16)
        vbuf[0:WINDOW, :] = jnp.zeros((WINDOW, KV_W), BF16)

    x = x_ref[0]
    u = jnp.dot(x.astype(BF16), w_in_ref[...], preferred_element_type=F32) + b_in_ref[...]
    a = u[:, 0:CONV_CH]
    gate = u[:, CONV_CH:2 * CONV_CH]
    hbuf[CONV_HALO:CONV_HALO + ts, :] = a * jax.nn.sigmoid(gate)
    qbuf[...] = (u[:, 2 * CONV_CH:2 * CONV_CH + ATTN_W] * (HEAD_DIM ** -0.5)).astype(BF16)
    kbuf[WINDOW:WINDOW + ts, :] = u[:, 2 * CONV_CH + ATTN_W:2 * CONV_CH + ATTN_W + KV_W].astype(BF16)
    vbuf[WINDOW:WINDOW + ts, :] = u[:, 2 * CONV_CH + ATTN_W + KV_W:D_IN].astype(BF16)

    rc = 64
    base = CONV_HALO - (CONV_WIDTH - 1)

    for c in range(ts // rc):
        r0 = c * rc
        acc = jnp.zeros((rc, CONV_CH), F32)
        for j in range(CONV_WIDTH):
            acc = acc + hbuf[r0 + base + j:r0 + base + j + rc, :] * w_dw_ref[j:j + 1, :]
        y = _layer_norm(acc + b_dw_ref[...], g_cn_ref[...], b_cn_ref[...])
        y = y * jax.nn.sigmoid(y)
        ymix[r0:r0 + rc, 0:CONV_CH] = y.astype(BF16)

    qi = lax.broadcasted_iota(jnp.int32, (WINDOW, 2 * WINDOW), 0)
    kj = lax.broadcasted_iota(jnp.int32, (WINDOW, 2 * WINDOW), 1)
    dist = qi + WINDOW - kj
    band = (dist >= 0) & (dist < WINDOW)

    def attn_block(jb, carry):
        r0 = pl.multiple_of(jb * WINDOW, WINDOW)
        first = jnp.logical_and(i == 0, jb == 0)
        valid = band & jnp.logical_or(jnp.logical_not(first), kj >= WINDOW)
        kk = kbuf[pl.ds(r0, 2 * WINDOW), :]
        vv = vbuf[pl.ds(r0, 2 * WINDOW), :]
        qq = qbuf[pl.ds(r0, WINDOW), :]
        outs = []
        for hd in range(N_HEADS):
            kvh = hd // GQ
            qh = qq[:, hd * HEAD_DIM:(hd + 1) * HEAD_DIM]
            kh = kk[:, kvh * HEAD_DIM:(kvh + 1) * HEAD_DIM]
            vh = vv[:, kvh * HEAD_DIM:(kvh + 1) * HEAD_DIM]
            s = jnp.where(valid, _dot_nt(qh, kh), MASK_VALUE)
            sink = sinks_ref[hd]
            m = jnp.maximum(jnp.max(s, axis=-1, keepdims=True), sink)
            p = jnp.exp(s - m)
            denom = jnp.sum(p, axis=-1, keepdims=True) + jnp.exp(sink - m)
            o = jnp.dot(p.astype(BF16), vh, preferred_element_type=F32)
            outs.append((o / denom).astype(BF16))
        ymix[pl.ds(r0, WINDOW), CONV_CH:D_MIX] = jnp.concatenate(outs, axis=-1)
        return carry

    lax.fori_loop(0, ts // WINDOW, attn_block, 0)

    mix = jnp.dot(ymix[...], w_out_ref[...], preferred_element_type=F32)
    o_ref[0] = _layer_norm(ALPHA * x + mix, g1_ref[...], b1_ref[...])

    hbuf[0:CONV_HALO, :] = hbuf[ts:ts + CONV_HALO, :]
    kbuf[0:WINDOW, :] = kbuf[ts:ts + WINDOW, :]
    vbuf[0:WINDOW, :] = vbuf[ts:ts + WINDOW, :]


def _const_spec(shape):
    nd = len(shape)
    return pl.BlockSpec(shape, lambda *_: (0,) * nd)


def _mixer(x, sinks, w_in, b_in, w_dw, b_dw, g_cn, b_cn, w_out, g1, b1):
    B, S, D = x.shape
    ts = SEQ_TILE
    tile = pl.BlockSpec((1, ts, D), lambda b, i: (b, i, 0))
    return pl.pallas_call(
        _mixer_kernel,
        grid=(B, S // ts),
        in_specs=[
            pl.BlockSpec(memory_space=pltpu.SMEM),
            tile,
            _const_spec((D, D_IN)), _const_spec((1, D_IN)),
            _const_spec((CONV_HALO, CONV_CH)), _const_spec((1, CONV_CH)),
            _const_spec((1, CONV_CH)), _const_spec((1, CONV_CH)),
            _const_spec((D_MIX, D)), _const_spec((1, D)), _const_spec((1, D)),
        ],
        out_specs=tile,
        out_shape=jax.ShapeDtypeStruct((B, S, D), F32),
        scratch_shapes=[
            pltpu.VMEM((CONV_HALO + ts, CONV_CH), F32),
            pltpu.VMEM((ts, ATTN_W), BF16),
            pltpu.VMEM((WINDOW + ts, KV_W), BF16),
            pltpu.VMEM((WINDOW + ts, KV_W), BF16),
            pltpu.VMEM((ts, D_MIX), BF16),
        ],
        compiler_params=pltpu.CompilerParams(
            dimension_semantics=("arbitrary", "arbitrary"), vmem_limit_bytes=VMEM_LIMIT),
        name="mixer",
    )(sinks, x, w_in, b_in, w_dw, b_dw, g_cn, b_cn, w_out, g1, b1)


def _memkv_kernel(mem_ref, w_ref, o_ref):
    o_ref[...] = jnp.dot(mem_ref[...].astype(BF16), w_ref[...], preferred_element_type=F32).astype(BF16)


def _memkv(mem2d, w_mkv):
    M, D = mem2d.shape
    N = w_mkv.shape[1]
    tn = 512
    return pl.pallas_call(
        _memkv_kernel,
        grid=(N // tn,),
        in_specs=[pl.BlockSpec((M, D), lambda j: (0, 0)), pl.BlockSpec((D, tn), lambda j: (0, j))],
        out_specs=pl.BlockSpec((M, tn), lambda j: (0, j)),
        out_shape=jax.ShapeDtypeStruct((M, N), BF16),
        compiler_params=pltpu.CompilerParams(dimension_semantics=("arbitrary",)),
        name="memkv",
    )(mem2d, w_mkv)


def _route(logits):
    lane = lax.broadcasted_iota(jnp.int32, logits.shape, 1)
    big = jnp.int32(LANES)
    gl = jnp.where(lane < N_GROUPS, logits, MASK_VALUE)
    gmax = jnp.max(gl, axis=-1, keepdims=True)
    g_p = 1.0 / jnp.sum(jnp.exp(gl - gmax), axis=-1, keepdims=True)
    g_idx = jnp.min(jnp.where(gl == gmax, lane, big), axis=-1, keepdims=True)
    lo = ROUTE_OFF + g_idx * EXPERTS_PER_GROUP
    sel = (lane >= lo) & (lane < lo + EXPERTS_PER_GROUP)
    rl = jnp.where(sel, logits, MASK_VALUE)
    m1 = jnp.max(rl, axis=-1, keepdims=True)
    i1 = jnp.min(jnp.where(rl == m1, lane, big), axis=-1, keepdims=True)
    rl2 = jnp.where(lane == i1, MASK_VALUE, rl)
    m2 = jnp.max(rl2, axis=-1, keepdims=True)
    i2 = jnp.min(jnp.where(rl2 == m2, lane, big), axis=-1, keepdims=True)
    e2 = jnp.exp(m2 - m1)
    w1 = 1.0 / (1.0 + e2)
    w2 = e2 * w1
    id1 = (i1 - ROUTE_OFF).astype(F32)
    id2 = (i2 - ROUTE_OFF).astype(F32)
    return jnp.where(lane == 0, id1, jnp.where(lane == 1, id2,
                     jnp.where(lane == 2, g_p * w1, jnp.where(lane == 3, g_p * w2, 0.0))))


def _memattn_kernel(x_ref, wq_ref, k_ref, v_ref, wo_ref, g2_ref, b2_ref, wr_ref, br_ref,
                    o_ref, ob_ref, comb_ref):
    x = x_ref[0]
    q = jnp.dot(x.astype(BF16), wq_ref[...], preferred_element_type=F32)
    q = (q * (MEM_HEAD_DIM ** -0.5)).astype(BF16)
    outs = []
    for h in range(MEM_HEADS):
        sl = slice(h * MEM_HEAD_DIM, (h + 1) * MEM_HEAD_DIM)
        s = _dot_nt(q[:, sl], k_ref[0, :, sl])
        m = jnp.max(s, axis=-1, keepdims=True)
        p = jnp.exp(s - m)
        denom = jnp.sum(p, axis=-1, keepdims=True)
        o = jnp.dot(p.astype(BF16), v_ref[0, :, sl], preferred_element_type=F32)
        outs.append((o / denom).astype(BF16))
    o = jnp.dot(jnp.concatenate(outs, axis=-1), wo_ref[...], preferred_element_type=F32)
    x2 = _layer_norm(ALPHA * x + o, g2_ref[...], b2_ref[...])
    o_ref[0] = x2
    ob_ref[0] = x2.astype(BF16)
    logits = jnp.dot(x2, wr_ref[...], preferred_element_type=F32,
                     precision=lax.Precision.HIGHEST) + br_ref[...]
    comb_ref[0] = _route(logits)


def _memattn(x1, wq, kmem, vmem, wo, g2, b2, wr, br):
    B, S, D = x1.shape
    ts = SEQ_TILE
    tile = pl.BlockSpec((1, ts, D), lambda b, i: (b, i, 0))
    kv = pl.BlockSpec((1, MEM_LEN, D), lambda b, i: (b, 0, 0))
    return pl.pallas_call(
        _memattn_kernel,
        grid=(B, S // ts),
        in_specs=[tile, _const_spec((D, D)), kv, kv, _const_spec((D, D)),
                  _const_spec((1, D)), _const_spec((1, D)),
                  _const_spec((D, LANES)), _const_spec((1, LANES))],
        out_specs=[tile, tile, pl.BlockSpec((1, ts, LANES), lambda b, i: (b, i, 0))],
        out_shape=[jax.ShapeDtypeStruct((B, S, D), F32),
                   jax.ShapeDtypeStruct((B, S, D), BF16),
                   jax.ShapeDtypeStruct((B, S, LANES), F32)],
        compiler_params=pltpu.CompilerParams(
            dimension_semantics=("arbitrary", "arbitrary"), vmem_limit_bytes=VMEM_LIMIT),
        name="memattn",
    )(x1, wq, kmem, vmem, wo, g2, b2, wr, br)


def _gmm_blocks(n_tokens):
    rows = (n_tokens * TOP_K + (n_tokens // MOE_TILE) * N_EXPERTS * (CHUNK - 1)
            + N_EXPERTS * (GMM_ROWS - CHUNK))
    return -(-rows // GMM_ROWS)


def _plan(e1, e2):
    T = e1.shape[0]
    nt = T // MOE_TILE
    oh = (jax.nn.one_hot(e1, N_EXPERTS, dtype=jnp.int32)
          + jax.nn.one_hot(e2, N_EXPERTS, dtype=jnp.int32)).reshape(nt, MOE_TILE, N_EXPERTS)
    csum = jnp.cumsum(oh, axis=1)
    rank = csum - oh
    n16 = (csum[:, -1, :] + CHUNK - 1) // CHUNK * CHUNK
    off16 = jnp.cumsum(n16, axis=1) - n16
    pos = off16[:, None, :] + rank
    lp0 = jnp.take_along_axis(pos, e1.reshape(nt, MOE_TILE, 1), axis=2)[..., 0]
    lp1 = jnp.take_along_axis(pos, e2.reshape(nt, MOE_TILE, 1), axis=2)[..., 0]
    n_e = jnp.sum(n16, axis=0)
    reg = (n_e + GMM_ROWS - 1) // GMM_ROWS * GMM_ROWS
    gend = jnp.cumsum(reg)
    gbase = gend - reg
    dst = gbase[None, :] + jnp.cumsum(n16, axis=0) - n16
    blk_row = jnp.arange(_gmm_blocks(T), dtype=jnp.int32)[:, None] * GMM_ROWS
    blk_expert = jnp.sum(blk_row >= gend[None, :], axis=1)
    blk_expert = jnp.minimum(blk_expert, N_EXPERTS - 1).astype(jnp.int32)
    n_used = (gend[-1] // GMM_ROWS).astype(jnp.int32).reshape(1)
    i32 = lambda a: a.astype(jnp.int32)
    fill_start = jnp.concatenate([gbase + n_e, gend[-1:]])
    fill_n = jnp.concatenate([reg - n_e, _gmm_blocks(T) * GMM_ROWS - gend[-1:]]) // CHUNK
    return dict(lp0=i32(lp0), lp1=i32(lp1), src=i32(off16).reshape(-1), dst=i32(dst).reshape(-1),
                nch=i32(n16 // CHUNK).reshape(-1), fill_start=i32(fill_start),
                fill_n=i32(fill_n), blk_expert=blk_expert, n_used=n_used)


def _chunk_copy(src_ref, src_row, dst_ref, dst_row, sem):
    return pltpu.make_async_copy(
        src_ref.at[pl.ds(pl.multiple_of(src_row, CHUNK), CHUNK), :],
        dst_ref.at[pl.ds(pl.multiple_of(dst_row, CHUNK), CHUNK), :], sem)


def _dispatch_kernel(src_ref, dst_ref, nch_ref, fstart_ref, fn_ref, lp_ref, x_ref, xs_hbm, xt, zbuf, sem):
    t = pl.program_id(0)
    nt = pl.num_programs(0)
    lp = lp_ref[0]
    r = lax.broadcasted_iota(jnp.int32, (SORT_ROWS, MOE_TILE), 0)
    hit = jnp.logical_or(lp[0:1, :] == r, lp[1:2, :] == r)
    p = jnp.where(hit, 1.0, 0.0).astype(BF16)
    xt[...] = jnp.dot(p, x_ref[...], preferred_element_type=F32).astype(BF16)

    def per_expert(e, total):
        k = t * N_EXPERTS + e
        n = nch_ref[k]

        def issue(c, carry):
            _chunk_copy(xt, src_ref[k] + c * CHUNK, xs_hbm, dst_ref[k] + c * CHUNK, sem).start()
            return carry

        lax.fori_loop(0, n, issue, 0)
        return total + n

    total = lax.fori_loop(0, N_EXPERTS, per_expert, 0)

    def fill_tail(_):
        zbuf[...] = jnp.zeros_like(zbuf)

        def per_expert_fill(e, tot):
            n = fn_ref[e]

            def issue(c, carry):
                _chunk_copy(zbuf, 0, xs_hbm, fstart_ref[e] + c * CHUNK, sem).start()
                return carry

            lax.fori_loop(0, n, issue, 0)
            return tot + n

        return lax.fori_loop(0, N_EXPERTS + 1, per_expert_fill, 0)

    total = total + lax.cond(t == nt - 1, fill_tail, lambda _: 0, 0)

    def drain(c, carry):
        _chunk_copy(xt, 0, xs_hbm, 0, sem).wait()
        return carry

    lax.fori_loop(0, total, drain, 0)


def _dispatch(plan, x2b):
    T, D = x2b.shape
    nt = T // MOE_TILE
    lp = jnp.stack([plan["lp0"], plan["lp1"]], axis=1)
    grid_spec = pltpu.PrefetchScalarGridSpec(
        num_scalar_prefetch=5,
        grid=(nt,),
        in_specs=[pl.BlockSpec((1, 2, MOE_TILE), lambda t, *_: (t, 0, 0)),
                  pl.BlockSpec((MOE_TILE, D), lambda t, *_: (t, 0))],
        out_specs=pl.BlockSpec(memory_space=pl.ANY),
        scratch_shapes=[pltpu.VMEM((SORT_ROWS, D), BF16), pltpu.VMEM((CHUNK, D), BF16),
                        pltpu.SemaphoreType.DMA(())],
    )
    return pl.pallas_call(
        _dispatch_kernel,
        grid_spec=grid_spec,
        out_shape=jax.ShapeDtypeStruct((_gmm_blocks(T) * GMM_ROWS, D), BF16),
        compiler_params=pltpu.CompilerParams(
            dimension_semantics=("arbitrary",), vmem_limit_bytes=VMEM_LIMIT),
        name="moe_dispatch",
    )(plan["src"], plan["dst"], plan["nch"], plan["fill_start"], plan["fill_n"], lp, x2b)


def _gmm_kernel(be_ref, nu_ref, x_ref, wg_ref, wu_ref, wd_ref, o_ref):
    @pl.when(pl.program_id(0) < nu_ref[0])
    def _():
        xb = x_ref[...]
        g = jnp.dot(xb, wg_ref[0], preferred_element_type=F32)
        u = jnp.dot(xb, wu_ref[0], preferred_element_type=F32)
        h = (g * jax.nn.sigmoid(g)) * u
        o_ref[...] = jnp.dot(h.astype(BF16), wd_ref[0], preferred_element_type=F32).astype(BF16)

    @pl.when(pl.program_id(0) >= nu_ref[0])
    def _():
        o_ref[...] = jnp.zeros_like(o_ref)


def _gmm(plan, xs, wg, wu, wd):
    R, D = xs.shape
    rows = pl.BlockSpec((GMM_ROWS, D), lambda b, be, nu: (jnp.minimum(b, nu[0] - 1), 0))
    grid_spec = pltpu.PrefetchScalarGridSpec(
        num_scalar_prefetch=2,
        grid=(R // GMM_ROWS,),
        in_specs=[rows,
                  pl.BlockSpec((1, D, D_EXPERT), lambda b, be, nu: (be[b], 0, 0)),
                  pl.BlockSpec((1, D, D_EXPERT), lambda b, be, nu: (be[b], 0, 0)),
                  pl.BlockSpec((1, D_EXPERT, D), lambda b, be, nu: (be[b], 0, 0))],
        out_specs=pl.BlockSpec((GMM_ROWS, D), lambda b, be, nu: (b, 0)),
    )
    return pl.pallas_call(
        _gmm_kernel,
        grid_spec=grid_spec,
        out_shape=jax.ShapeDtypeStruct((R, D), BF16),
        compiler_params=pltpu.CompilerParams(
            dimension_semantics=("arbitrary",), vmem_limit_bytes=VMEM_LIMIT),
        name="moe_gmm",
    )(plan["blk_expert"], plan["n_used"], xs, wg, wu, wd)


def _combine_kernel(src_ref, dst_ref, nch_ref, cm_ref, x_ref, ys_hbm, g3_ref, b3_ref, o_ref, yt, sem):
    t = pl.program_id(0)

    @pl.when(t == 0)
    def _():
        yt[...] = jnp.zeros_like(yt)

    def per_expert(e, total):
        k = t * N_EXPERTS + e
        n = nch_ref[k]

        def issue(c, carry):
            _chunk_copy(ys_hbm, dst_ref[k] + c * CHUNK, yt, src_ref[k] + c * CHUNK, sem).start()
            return carry

        lax.fori_loop(0, n, issue, 0)
        return total + n

    total = lax.fori_loop(0, N_EXPERTS, per_expert, 0)

    cm = cm_ref[...]
    col = lax.broadcasted_iota(jnp.int32, (MOE_TILE, SORT_ROWS), 1)
    lp0 = cm[:, 0:1].astype(jnp.int32)
    lp1 = cm[:, 1:2].astype(jnp.int32)
    w = jnp.where(col == lp0, cm[:, 2:3], 0.0) + jnp.where(col == lp1, cm[:, 3:4], 0.0)
    w = w.astype(BF16)

    def drain(c, carry):
        _chunk_copy(ys_hbm, 0, yt, 0, sem).wait()
        return carry

    lax.fori_loop(0, total, drain, 0)

    y = jnp.dot(w, yt[...], preferred_element_type=F32)
    o_ref[...] = _layer_norm(ALPHA * x_ref[...] + y, g3_ref[...], b3_ref[...])


def _combine(plan, cm, x2, ys, g3, b3):
    T, D = x2.shape
    nt = T // MOE_TILE
    grid_spec = pltpu.PrefetchScalarGridSpec(
        num_scalar_prefetch=3,
        grid=(nt,),
        in_specs=[pl.BlockSpec((MOE_TILE, 4), lambda t, *_: (t, 0)),
                  pl.BlockSpec((MOE_TILE, D), lambda t, *_: (t, 0)),
                  pl.BlockSpec(memory_space=pl.ANY),
                  pl.BlockSpec((1, D), lambda t, *_: (0, 0)),
                  pl.BlockSpec((1, D), lambda t, *_: (0, 0))],
        out_specs=pl.BlockSpec((MOE_TILE, D), lambda t, *_: (t, 0)),
        scratch_shapes=[pltpu.VMEM((SORT_ROWS, D), BF16), pltpu.SemaphoreType.DMA(())],
    )
    return pl.pallas_call(
        _combine_kernel,
        grid_spec=grid_spec,
        out_shape=jax.ShapeDtypeStruct((T, D), F32),
        compiler_params=pltpu.CompilerParams(
            dimension_semantics=("arbitrary",), vmem_limit_bytes=VMEM_LIMIT),
        name="moe_combine",
    )(plan["src"], plan["dst"], plan["nch"], cm, x2, ys, g3, b3)


def _moe(x2, x2b, route, wg, wu, wd, g3, b3):
    T, D = x2.shape
    e1 = route[:, 0].astype(jnp.int32)
    e2 = route[:, 1].astype(jnp.int32)
    plan = _plan(e1, e2)
    cm = jnp.stack([plan["lp0"].reshape(T).astype(F32), plan["lp1"].reshape(T).astype(F32),
                    route[:, 2], route[:, 3]], axis=1)
    xs = _dispatch(plan, x2b)
    ys = _gmm(plan, xs, wg, wu, wd)
    return _combine(plan, cm, x2, ys, g3, b3)


def _row(v):
    return v.reshape(1, -1).astype(F32)


def kernel(x, mem, w_in, b_in, w_dw, b_dw, g_conv_norm, b_conv_norm, attn_sinks, w_out, g_ln1, b_ln1,
           w_mq, w_mkv, w_mo, g_ln2, b_ln2, w_group, b_group, w_router, b_router, w_gate, w_up, w_down,
           g_ln3, b_ln3):
    B, S, D = x.shape
    for l in range(DEPTH):
        w_dw_p = jnp.zeros((CONV_HALO, CONV_CH), F32).at[:CONV_WIDTH].set(w_dw[l])
        x1 = _mixer(x, attn_sinks[l].astype(F32), w_in[l].astype(BF16), _row(b_in[l]), w_dw_p,
                    _row(b_dw[l]), _row(g_conv_norm[l]), _row(b_conv_norm[l]),
                    w_out[l].astype(BF16), _row(g_ln1[l]), _row(b_ln1[l]))

        kvm = _memkv(mem.reshape(B * MEM_LEN, D), w_mkv[l].astype(BF16)).reshape(B, MEM_LEN, 2 * D)
        kmem, vmem = kvm[:, :, :D], kvm[:, :, D:]

        wr = jnp.concatenate(
            [w_group[l], jnp.transpose(w_router[l], (1, 0, 2)).reshape(D, N_EXPERTS)], axis=1)
        wr = jnp.pad(wr, ((0, 0), (0, LANES - wr.shape[1])))
        br = jnp.pad(jnp.concatenate([b_group[l], b_router[l].reshape(-1)]), (0, LANES - N_GROUPS - N_EXPERTS))
        x2, x2b, route = _memattn(x1, w_mq[l].astype(BF16), kmem, vmem, w_mo[l].astype(BF16),
                                 _row(g_ln2[l]), _row(b_ln2[l]), wr.astype(F32), _row(br))

        T = B * S
        y = _moe(x2.reshape(T, D), x2b.reshape(T, D), route.reshape(T, LANES),
                 w_gate[l].astype(BF16), w_up[l].astype(BF16), w_down[l].astype(BF16),
                 _row(g_ln3[l]), _row(b_ln3[l]))
        x = y.reshape(B, S, D)
    return x
```

```python
import functools

import jax
import jax.numpy as jnp
from jax import lax
from jax.experimental import pallas as pl
from jax.experimental.pallas import tpu as pltpu

D_MODEL = 1024
MEM_LEN = 256
CONV_CH = 512
CONV_WIDTH = 31
N_HEADS = 8
N_KV_HEADS = 2
HEAD_DIM = 64
GQ = N_HEADS // N_KV_HEADS
ATTN_W = N_HEADS * HEAD_DIM
KV_W = N_KV_HEADS * HEAD_DIM
WINDOW = 128
D_MIX = CONV_CH + ATTN_W
D_IN = 2 * CONV_CH + ATTN_W + 2 * KV_W
MEM_HEADS = 4
MEM_HEAD_DIM = D_MODEL // MEM_HEADS
N_GROUPS = 4
EXPERTS_PER_GROUP = 4
N_EXPERTS = N_GROUPS * EXPERTS_PER_GROUP
D_EXPERT = D_MODEL // 2
DEPTH = 1
ALPHA = (2.0 * DEPTH) ** 0.25
LN_EPS = 1e-5

LANES = 128
SUBLANES = 8
CONV_ROWS = 128
LN_ROWS = 64
MASK_VALUE = -1e30
CONV_HALO = 32
SEQ_TILE = 512
MOE_TILE = 512
CHUNK = 16
GMM_ROWS = 256
TOP_K = 2
SORT_ROWS = -(-(MOE_TILE * TOP_K + N_EXPERTS * (CHUNK - 1)) // 256) * 256
ROUTE_OFF = N_GROUPS
VMEM_LIMIT = 56 * 1024 * 1024

BF16 = jnp.bfloat16
F32 = jnp.float32


def _layer_norm(x, g, b):
    mu = jnp.mean(x, axis=-1, keepdims=True)
    xc = x - mu
    var = jnp.mean(xc * xc, axis=-1, keepdims=True)
    return xc * lax.rsqrt(var + LN_EPS) * g + b


def _dot_nt(a, b):
    return lax.dot_general(a, b, (((1,), (1,)), ((), ())), preferred_element_type=F32)


def _mixer_kernel(sinks_ref, x_ref, w_in_ref, b_in_ref, w_dw_ref, b_dw_ref, g_cn_ref, b_cn_ref,
                  w_out_ref, g1_ref, b1_ref, o_ref, hbuf, hshift, cbuf, qbuf, kbuf, vbuf, ymix):
    i = pl.program_id(1)
    ts = SEQ_TILE

    @pl.when(i == 0)
    def _():
        hbuf[0:CONV_HALO, :] = jnp.zeros((CONV_HALO, CONV_CH), F32)
        kbuf[:, 0:WINDOW, :] = jnp.zeros((2 * N_KV_HEADS, WINDOW, KV_W), BF16)
        vbuf[:, 0:WINDOW, :] = jnp.zeros((2, WINDOW, KV_W), BF16)

    x = x_ref[0]
    u = jnp.dot(x.astype(BF16), w_in_ref[...], preferred_element_type=F32) + b_in_ref[...]
    a = u[:, 0:CONV_CH]
    gate = u[:, CONV_CH:2 * CONV_CH]
    hbuf[CONV_HALO:CONV_HALO + ts, :] = a * jax.nn.sigmoid(gate)
    qbuf[...] = (u[:, 2 * CONV_CH:2 * CONV_CH + ATTN_W] * (HEAD_DIM ** -0.5)).astype(BF16)
    kf = u[:, 2 * CONV_CH + ATTN_W:2 * CONV_CH + ATTN_W + KV_W]
    vf = u[:, 2 * CONV_CH + ATTN_W + KV_W:D_IN]
    kr = pltpu.roll(kf, HEAD_DIM, axis=1)
    vr = pltpu.roll(vf, HEAD_DIM, axis=1)
    lo = lax.broadcasted_iota(jnp.int32, (ts, KV_W), 1) < HEAD_DIM
    rows = slice(WINDOW, WINDOW + ts)
    kbuf[0, rows, :] = jnp.where(lo, kf, 0.0).astype(BF16)
    kbuf[1, rows, :] = jnp.where(lo, 0.0, kr).astype(BF16)
    kbuf[2, rows, :] = jnp.where(lo, kr, 0.0).astype(BF16)
    kbuf[3, rows, :] = jnp.where(lo, 0.0, kf).astype(BF16)
    vbuf[0, rows, :] = vf.astype(BF16)
    vbuf[1, rows, :] = vr.astype(BF16)

    base = CONV_HALO - (CONV_WIDTH - 1)
    n_shift = ts + CONV_HALO - SUBLANES
    for b in range(1, SUBLANES):
        hshift[b - 1, 0:n_shift, :] = hbuf[b:b + n_shift, :]
    rc = CONV_ROWS
    for c in range(ts // rc):
        r0 = c * rc
        for l in range(CONV_CH // LANES):
            ls = slice(l * LANES, (l + 1) * LANES)
            acc = jnp.zeros((rc, LANES), F32)
            for j in range(CONV_WIDTH):
                a8, b = divmod(j + base, SUBLANES)
                rs = slice(r0 + SUBLANES * a8, r0 + SUBLANES * a8 + rc)
                tap = hbuf[rs, ls] if b == 0 else hshift[b - 1, rs, ls]
                acc = acc + tap * w_dw_ref[j:j + 1, ls]
            cbuf[r0:r0 + rc, ls] = acc
    for c in range(ts // LN_ROWS):
        rs = slice(c * LN_ROWS, (c + 1) * LN_ROWS)
        y = _layer_norm(cbuf[rs, :] + b_dw_ref[...], g_cn_ref[...], b_cn_ref[...])
        y = y * jax.nn.sigmoid(y)
        ymix[rs, 0:CONV_CH] = y.astype(BF16)

    qi = lax.broadcasted_iota(jnp.int32, (2 * WINDOW, 2 * WINDOW), 0) % WINDOW
    kj = lax.broadcasted_iota(jnp.int32, (2 * WINDOW, 2 * WINDOW), 1)
    dist = qi + WINDOW - kj
    band = (dist >= 0) & (dist < WINDOW)
    top = lax.broadcasted_iota(jnp.int32, (2 * WINDOW, 1), 0) < WINDOW
    lo_out = lax.broadcasted_iota(jnp.int32, (WINDOW, 2 * HEAD_DIM), 1) < HEAD_DIM

    def attn_block(jb, carry):
        r0 = pl.multiple_of(jb * WINDOW, WINDOW)
        first = jnp.logical_and(i == 0, jb == 0)
        valid = band & jnp.logical_or(jnp.logical_not(first), kj >= WINDOW)
        for kvh in range(N_KV_HEADS):
            h0 = kvh * GQ
            c0 = h0 * HEAD_DIM
            qs = jnp.concatenate([qbuf[pl.ds(r0, WINDOW), c0:c0 + 2 * HEAD_DIM],
                                  qbuf[pl.ds(r0, WINDOW), c0 + 2 * HEAD_DIM:c0 + 4 * HEAD_DIM]], axis=0)
            pv = []
            for par in range(2):
                kk = kbuf[2 * kvh + par, pl.ds(r0, 2 * WINDOW), :]
                vv = vbuf[(kvh + par) % 2, pl.ds(r0, 2 * WINDOW), :]
                s = jnp.where(valid, _dot_nt(qs, kk), MASK_VALUE)
                sink = jnp.where(top, sinks_ref[h0 + par], sinks_ref[h0 + 2 + par])
                m = jnp.maximum(jnp.max(s, axis=-1, keepdims=True), sink)
                p = jnp.exp(s - m)
                denom = jnp.sum(p, axis=-1, keepdims=True) + jnp.exp(sink - m)
                pv.append(jnp.dot(p.astype(BF16), vv, preferred_element_type=F32) / denom)
            for pair in range(2):
                rs = slice(pair * WINDOW, (pair + 1) * WINDOW)
                o = jnp.where(lo_out, pv[0][rs], pv[1][rs])
                cs = CONV_CH + c0 + pair * 2 * HEAD_DIM
                ymix[pl.ds(r0, WINDOW), cs:cs + 2 * HEAD_DIM] = o.astype(BF16)
        return carry

    lax.fori_loop(0, ts // WINDOW, attn_block, 0)

    mix = jnp.dot(ymix[...], w_out_ref[...], preferred_element_type=F32)
    o_ref[0] = _layer_norm(ALPHA * x + mix, g1_ref[...], b1_ref[...])

    hbuf[0:CONV_HALO, :] = hbuf[ts:ts + CONV_HALO, :]
    kbuf[:, 0:WINDOW, :] = kbuf[:, ts:ts + WINDOW, :]
    vbuf[:, 0:WINDOW, :] = vbuf[:, ts:ts + WINDOW, :]


def _const_spec(shape):
    nd = len(shape)
    return pl.BlockSpec(shape, lambda *_: (0,) * nd)


def _mixer(x, sinks, w_in, b_in, w_dw, b_dw, g_cn, b_cn, w_out, g1, b1):
    B, S, D = x.shape
    ts = SEQ_TILE
    tile = pl.BlockSpec((1, ts, D), lambda b, i: (b, i, 0))
    return pl.pallas_call(
        _mixer_kernel,
        grid=(B, S // ts),
        in_specs=[
            pl.BlockSpec(memory_space=pltpu.SMEM),
            tile,
            _const_spec((D, D_IN)), _const_spec((1, D_IN)),
            _const_spec((CONV_HALO, CONV_CH)), _const_spec((1, CONV_CH)),
            _const_spec((1, CONV_CH)), _const_spec((1, CONV_CH)),
            _const_spec((D_MIX, D)), _const_spec((1, D)), _const_spec((1, D)),
        ],
        out_specs=tile,
        out_shape=jax.ShapeDtypeStruct((B, S, D), F32),
        scratch_shapes=[
            pltpu.VMEM((CONV_HALO + ts, CONV_CH), F32),
            pltpu.VMEM((SUBLANES - 1, CONV_HALO + ts, CONV_CH), F32),
            pltpu.VMEM((ts, CONV_CH), F32),
            pltpu.VMEM((ts, ATTN_W), BF16),
            pltpu.VMEM((2 * N_KV_HEADS, WINDOW + ts, KV_W), BF16),
            pltpu.VMEM((2, WINDOW + ts, KV_W), BF16),
            pltpu.VMEM((ts, D_MIX), BF16),
        ],
        compiler_params=pltpu.CompilerParams(
            dimension_semantics=("arbitrary", "arbitrary"), vmem_limit_bytes=VMEM_LIMIT),
        name="mixer",
    )(sinks, x, w_in, b_in, w_dw, b_dw, g_cn, b_cn, w_out, g1, b1)


def _memkv_kernel(mem_ref, w_ref, o_ref):
    o_ref[...] = jnp.dot(mem_ref[...].astype(BF16), w_ref[...], preferred_element_type=F32).astype(BF16)


def _memkv(mem2d, w_mkv):
    M, D = mem2d.shape
    N = w_mkv.shape[1]
    tn = 512
    return pl.pallas_call(
        _memkv_kernel,
        grid=(N // tn,),
        in_specs=[pl.BlockSpec((M, D), lambda j: (0, 0)), pl.BlockSpec((D, tn), lambda j: (0, j))],
        out_specs=pl.BlockSpec((M, tn), lambda j: (0, j)),
        out_shape=jax.ShapeDtypeStruct((M, N), BF16),
        compiler_params=pltpu.CompilerParams(dimension_semantics=("arbitrary",)),
        name="memkv",
    )(mem2d, w_mkv)


def _route(logits):
    lane = lax.broadcasted_iota(jnp.int32, logits.shape, 1)
    big = jnp.int32(LANES)
    gl = jnp.where(lane < N_GROUPS, logits, MASK_VALUE)
    gmax = jnp.max(gl, axis=-1, keepdims=True)
    g_p = 1.0 / jnp.sum(jnp.exp(gl - gmax), axis=-1, keepdims=True)
    g_idx = jnp.min(jnp.where(gl == gmax, lane, big), axis=-1, keepdims=True)
    lo = ROUTE_OFF + g_idx * EXPERTS_PER_GROUP
    sel = (lane >= lo) & (lane < lo + EXPERTS_PER_GROUP)
    rl = jnp.where(sel, logits, MASK_VALUE)
    m1 = jnp.max(rl, axis=-1, keepdims=True)
    i1 = jnp.min(jnp.where(rl == m1, lane, big), axis=-1, keepdims=True)
    rl2 = jnp.where(lane == i1, MASK_VALUE, rl)
    m2 = jnp.max(rl2, axis=-1, keepdims=True)
    i2 = jnp.min(jnp.where(rl2 == m2, lane, big), axis=-1, keepdims=True)
    e2 = jnp.exp(m2 - m1)
    w1 = 1.0 / (1.0 + e2)
    w2 = e2 * w1
    id1 = (i1 - ROUTE_OFF).astype(F32)
    id2 = (i2 - ROUTE_OFF).astype(F32)
    return jnp.where(lane == 0, id1, jnp.where(lane == 1, id2,
                     jnp.where(lane == 2, g_p * w1, jnp.where(lane == 3, g_p * w2, 0.0))))


def _memattn_kernel(x_ref, wq_ref, k_ref, v_ref, wo_ref, g2_ref, b2_ref, wr_ref, br_ref,
                    o_ref, ob_ref, comb_ref):
    x = x_ref[0]
    q = jnp.dot(x.astype(BF16), wq_ref[...], preferred_element_type=F32)
    q = (q * (MEM_HEAD_DIM ** -0.5)).astype(BF16)
    outs = []
    for h in range(MEM_HEADS):
        sl = slice(h * MEM_HEAD_DIM, (h + 1) * MEM_HEAD_DIM)
        s = _dot_nt(q[:, sl], k_ref[0, :, sl])
        m = jnp.max(s, axis=-1, keepdims=True)
        p = jnp.exp(s - m)
        denom = jnp.sum(p, axis=-1, keepdims=True)
        o = jnp.dot(p.astype(BF16), v_ref[0, :, sl], preferred_element_type=F32)
        outs.append((o / denom).astype(BF16))
    o = jnp.dot(jnp.concatenate(outs, axis=-1), wo_ref[...], preferred_element_type=F32)
    x2 = _layer_norm(ALPHA * x + o, g2_ref[...], b2_ref[...])
    o_ref[0] = x2
    ob_ref[0] = x2.astype(BF16)
    logits = jnp.dot(x2, wr_ref[...], preferred_element_type=F32,
                     precision=lax.Precision.HIGHEST) + br_ref[...]
    comb_ref[0] = _route(logits)


def _memattn(x1, wq, kmem, vmem, wo, g2, b2, wr, br):
    B, S, D = x1.shape
    ts = SEQ_TILE
    tile = pl.BlockSpec((1, ts, D), lambda b, i: (b, i, 0))
    kv = pl.BlockSpec((1, MEM_LEN, D), lambda b, i: (b, 0, 0))
    return pl.pallas_call(
        _memattn_kernel,
        grid=(B, S // ts),
        in_specs=[tile, _const_spec((D, D)), kv, kv, _const_spec((D, D)),
                  _const_spec((1, D)), _const_spec((1, D)),
                  _const_spec((D, LANES)), _const_spec((1, LANES))],
        out_specs=[tile, tile, pl.BlockSpec((1, ts, LANES), lambda b, i: (b, i, 0))],
        out_shape=[jax.ShapeDtypeStruct((B, S, D), F32),
                   jax.ShapeDtypeStruct((B, S, D), BF16),
                   jax.ShapeDtypeStruct((B, S, LANES), F32)],
        compiler_params=pltpu.CompilerParams(
            dimension_semantics=("arbitrary", "arbitrary"), vmem_limit_bytes=VMEM_LIMIT),
        name="memattn",
    )(x1, wq, kmem, vmem, wo, g2, b2, wr, br)


def _gmm_blocks(n_tokens):
    rows = (n_tokens * TOP_K + (n_tokens // MOE_TILE) * N_EXPERTS * (CHUNK - 1)
            + N_EXPERTS * (GMM_ROWS - CHUNK))
    return -(-rows // GMM_ROWS)


def _plan(e1, e2):
    T = e1.shape[0]
    nt = T // MOE_TILE
    oh = (jax.nn.one_hot(e1, N_EXPERTS, dtype=jnp.int32)
          + jax.nn.one_hot(e2, N_EXPERTS, dtype=jnp.int32)).reshape(nt, MOE_TILE, N_EXPERTS)
    csum = jnp.cumsum(oh, axis=1)
    rank = csum - oh
    n16 = (csum[:, -1, :] + CHUNK - 1) // CHUNK * CHUNK
    off16 = jnp.cumsum(n16, axis=1) - n16
    pos = off16[:, None, :] + rank
    lp0 = jnp.take_along_axis(pos, e1.reshape(nt, MOE_TILE, 1), axis=2)[..., 0]
    lp1 = jnp.take_along_axis(pos, e2.reshape(nt, MOE_TILE, 1), axis=2)[..., 0]
    n_e = jnp.sum(n16, axis=0)
    reg = (n_e + GMM_ROWS - 1) // GMM_ROWS * GMM_ROWS
    gend = jnp.cumsum(reg)
    gbase = gend - reg
    dst = gbase[None, :] + jnp.cumsum(n16, axis=0) - n16
    blk_row = jnp.arange(_gmm_blocks(T), dtype=jnp.int32)[:, None] * GMM_ROWS
    blk_expert = jnp.sum(blk_row >= gend[None, :], axis=1)
    blk_expert = jnp.minimum(blk_expert, N_EXPERTS - 1).astype(jnp.int32)
    n_used = (gend[-1] // GMM_ROWS).astype(jnp.int32).reshape(1)
    i32 = lambda a: a.astype(jnp.int32)
    fill_start = jnp.concatenate([gbase + n_e, gend[-1:]])
    fill_n = jnp.concatenate([reg - n_e, _gmm_blocks(T) * GMM_ROWS - gend[-1:]]) // CHUNK
    return dict(lp0=i32(lp0), lp1=i32(lp1), src=i32(off16).reshape(-1), dst=i32(dst).reshape(-1),
                nch=i32(n16 // CHUNK).reshape(-1), fill_start=i32(fill_start),
                fill_n=i32(fill_n), blk_expert=blk_expert, n_used=n_used)


def _chunk_copy(src_ref, src_row, dst_ref, dst_row, sem):
    return pltpu.make_async_copy(
        src_ref.at[pl.ds(pl.multiple_of(src_row, CHUNK), CHUNK), :],
        dst_ref.at[pl.ds(pl.multiple_of(dst_row, CHUNK), CHUNK), :], sem)


def _dispatch_kernel(src_ref, dst_ref, nch_ref, fstart_ref, fn_ref, lp_ref, x_ref, xs_hbm, xt, zbuf, sem):
    t = pl.program_id(0)
    nt = pl.num_programs(0)
    lp = lp_ref[0]
    r = lax.broadcasted_iota(jnp.int32, (SORT_ROWS, MOE_TILE), 0)
    hit = jnp.logical_or(lp[0:1, :] == r, lp[1:2, :] == r)
    p = jnp.where(hit, 1.0, 0.0).astype(BF16)
    xt[...] = jnp.dot(p, x_ref[...], preferred_element_type=F32).astype(BF16)

    def per_expert(e, total):
        k = t * N_EXPERTS + e
        n = nch_ref[k]

        def issue(c, carry):
            _chunk_copy(xt, src_ref[k] + c * CHUNK, xs_hbm, dst_ref[k] + c * CHUNK, sem).start()
            return carry

        lax.fori_loop(0, n, issue, 0)
        return total + n

    total = lax.fori_loop(0, N_EXPERTS, per_expert, 0)

    def fill_tail(_):
        zbuf[...] = jnp.zeros_like(zbuf)

        def per_expert_fill(e, tot):
            n = fn_ref[e]

            def issue(c, carry):
                _chunk_copy(zbuf, 0, xs_hbm, fstart_ref[e] + c * CHUNK, sem).start()
                return carry

            lax.fori_loop(0, n, issue, 0)
            return tot + n

        return lax.fori_loop(0, N_EXPERTS + 1, per_expert_fill, 0)

    total = total + lax.cond(t == nt - 1, fill_tail, lambda _: 0, 0)

    def drain(c, carry):
        _chunk_copy(xt, 0, xs_hbm, 0, sem).wait()
        return carry

    lax.fori_loop(0, total, drain, 0)


def _dispatch(plan, x2b):
    T, D = x2b.shape
    nt = T // MOE_TILE
    lp = jnp.stack([plan["lp0"], plan["lp1"]], axis=1)
    grid_spec = pltpu.PrefetchScalarGridSpec(
        num_scalar_prefetch=5,
        grid=(nt,),
        in_specs=[pl.BlockSpec((1, 2, MOE_TILE), lambda t, *_: (t, 0, 0)),
                  pl.BlockSpec((MOE_TILE, D), lambda t, *_: (t, 0))],
        out_specs=pl.BlockSpec(memory_space=pl.ANY),
        scratch_shapes=[pltpu.VMEM((SORT_ROWS, D), BF16), pltpu.VMEM((CHUNK, D), BF16),
                        pltpu.SemaphoreType.DMA(())],
    )
    return pl.pallas_call(
        _dispatch_kernel,
        grid_spec=grid_spec,
        out_shape=jax.ShapeDtypeStruct((_gmm_blocks(T) * GMM_ROWS, D), BF16),
        compiler_params=pltpu.CompilerParams(
            dimension_semantics=("arbitrary",), vmem_limit_bytes=VMEM_LIMIT),
        name="moe_dispatch",
    )(plan["src"], plan["dst"], plan["nch"], plan["fill_start"], plan["fill_n"], lp, x2b)


def _gmm_kernel(be_ref, nu_ref, x_ref, wg_ref, wu_ref, wd_ref, o_ref):
    @pl.when(pl.program_id(0) < nu_ref[0])
    def _():
        xb = x_ref[...]
        g = jnp.dot(xb, wg_ref[0], preferred_element_type=F32)
        u = jnp.dot(xb, wu_ref[0], preferred_element_type=F32)
        h = (g * jax.nn.sigmoid(g)) * u
        o_ref[...] = jnp.dot(h.astype(BF16), wd_ref[0], preferred_element_type=F32).astype(BF16)

    @pl.when(pl.program_id(0) >= nu_ref[0])
    def _():
        o_ref[...] = jnp.zeros_like(o_ref)


def _gmm(plan, xs, wg, wu, wd):
    R, D = xs.shape
    rows = pl.BlockSpec((GMM_ROWS, D), lambda b, be, nu: (jnp.minimum(b, nu[0] - 1), 0))
    grid_spec = pltpu.PrefetchScalarGridSpec(
        num_scalar_prefetch=2,
        grid=(R // GMM_ROWS,),
        in_specs=[rows,
                  pl.BlockSpec((1, D, D_EXPERT), lambda b, be, nu: (be[b], 0, 0)),
                  pl.BlockSpec((1, D, D_EXPERT), lambda b, be, nu: (be[b], 0, 0)),
                  pl.BlockSpec((1, D_EXPERT, D), lambda b, be, nu: (be[b], 0, 0))],
        out_specs=pl.BlockSpec((GMM_ROWS, D), lambda b, be, nu: (b, 0)),
    )
    return pl.pallas_call(
        _gmm_kernel,
        grid_spec=grid_spec,
        out_shape=jax.ShapeDtypeStruct((R, D), BF16),
        compiler_params=pltpu.CompilerParams(
            dimension_semantics=("arbitrary",), vmem_limit_bytes=VMEM_LIMIT),
        name="moe_gmm",
    )(plan["blk_expert"], plan["n_used"], xs, wg, wu, wd)


def _combine_kernel(src_ref, dst_ref, nch_ref, cm_ref, x_ref, ys_hbm, g3_ref, b3_ref, o_ref, yt, sem):
    t = pl.program_id(0)

    @pl.when(t == 0)
    def _():
        yt[...] = jnp.zeros_like(yt)

    def per_expert(e, total):
        k = t * N_EXPERTS + e
        n = nch_ref[k]

        def issue(c, carry):
            _chunk_copy(ys_hbm, dst_ref[k] + c * CHUNK, yt, src_ref[k] + c * CHUNK, sem).start()
            return carry

        lax.fori_loop(0, n, issue, 0)
        return total + n

    total = lax.fori_loop(0, N_EXPERTS, per_expert, 0)

    cm = cm_ref[...]
    col = lax.broadcasted_iota(jnp.int32, (MOE_TILE, SORT_ROWS), 1)
    lp0 = cm[:, 0:1].astype(jnp.int32)
    lp1 = cm[:, 1:2].astype(jnp.int32)
    w = jnp.where(col == lp0, cm[:, 2:3], 0.0) + jnp.where(col == lp1, cm[:, 3:4], 0.0)
    w = w.astype(BF16)

    def drain(c, carry):
        _chunk_copy(ys_hbm, 0, yt, 0, sem).wait()
        return carry

    lax.fori_loop(0, total, drain, 0)

    y = jnp.dot(w, yt[...], preferred_element_type=F32)
    o_ref[...] = _layer_norm(ALPHA * x_ref[...] + y, g3_ref[...], b3_ref[...])


def _combine(plan, cm, x2, ys, g3, b3):
    T, D = x2.shape
    nt = T // MOE_TILE
    grid_spec = pltpu.PrefetchScalarGridSpec(
        num_scalar_prefetch=3,
        grid=(nt,),
        in_specs=[pl.BlockSpec((MOE_TILE, 4), lambda t, *_: (t, 0)),
                  pl.BlockSpec((MOE_TILE, D), lambda t, *_: (t, 0)),
                  pl.BlockSpec(memory_space=pl.ANY),
                  pl.BlockSpec((1, D), lambda t, *_: (0, 0)),
                  pl.BlockSpec((1, D), lambda t, *_: (0, 0))],
        out_specs=pl.BlockSpec((MOE_TILE, D), lambda t, *_: (t, 0)),
        scratch_shapes=[pltpu.VMEM((SORT_ROWS, D), BF16), pltpu.SemaphoreType.DMA(())],
    )
    return pl.pallas_call(
        _combine_kernel,
        grid_spec=grid_spec,
        out_shape=jax.ShapeDtypeStruct((T, D), F32),
        compiler_params=pltpu.CompilerParams(
            dimension_semantics=("arbitrary",), vmem_limit_bytes=VMEM_LIMIT),
        name="moe_combine",
    )(plan["src"], plan["dst"], plan["nch"], cm, x2, ys, g3, b3)


def _moe(x2, x2b, route, wg, wu, wd, g3, b3):
    T, D = x2.shape
    e1 = route[:, 0].astype(jnp.int32)
    e2 = route[:, 1].astype(jnp.int32)
    plan = _plan(e1, e2)
    cm = jnp.stack([plan["lp0"].reshape(T).astype(F32), plan["lp1"].reshape(T).astype(F32),
                    route[:, 2], route[:, 3]], axis=1)
    xs = _dispatch(plan, x2b)
    ys = _gmm(plan, xs, wg, wu, wd)
    return _combine(plan, cm, x2, ys, g3, b3)


def _row(v):
    return v.reshape(1, -1).astype(F32)


def kernel(x, mem, w_in, b_in, w_dw, b_dw, g_conv_norm, b_conv_norm, attn_sinks, w_out, g_ln1, b_ln1,
           w_mq, w_mkv, w_mo, g_ln2, b_ln2, w_group, b_group, w_router, b_router, w_gate, w_up, w_down,
           g_ln3, b_ln3):
    B, S, D = x.shape
    for l in range(DEPTH):
        w_dw_p = jnp.zeros((CONV_HALO, CONV_CH), F32).at[:CONV_WIDTH].set(w_dw[l])
        x1 = _mixer(x, attn_sinks[l].astype(F32), w_in[l].astype(BF16), _row(b_in[l]), w_dw_p,
                    _row(b_dw[l]), _row(g_conv_norm[l]), _row(b_conv_norm[l]),
                    w_out[l].astype(BF16), _row(g_ln1[l]), _row(b_ln1[l]))

        kvm = _memkv(mem.reshape(B * MEM_LEN, D), w_mkv[l].astype(BF16)).reshape(B, MEM_LEN, 2 * D)
        kmem, vmem = kvm[:, :, :D], kvm[:, :, D:]

        wr = jnp.concatenate(
            [w_group[l], jnp.transpose(w_router[l], (1, 0, 2)).reshape(D, N_EXPERTS)], axis=1)
        wr = jnp.pad(wr, ((0, 0), (0, LANES - wr.shape[1])))
        br = jnp.pad(jnp.concatenate([b_group[l], b_router[l].reshape(-1)]), (0, LANES - N_GROUPS - N_EXPERTS))
        x2, x2b, route = _memattn(x1, w_mq[l].astype(BF16), kmem, vmem, w_mo[l].astype(BF16),
                                 _row(g_ln2[l]), _row(b_ln2[l]), wr.astype(F32), _row(br))

        T = B * S
        y = _moe(x2.reshape(T, D), x2b.reshape(T, D), route.reshape(T, LANES),
                 w_gate[l].astype(BF16), w_up[l].astype(BF16), w_down[l].astype(BF16),
                 _row(g_ln3[l]), _row(b_ln3[l]))
        x = y.reshape(B, S, D)
    return x
```

```python
import functools

import jax
import jax.numpy as jnp
from jax import lax
from jax.experimental import pallas as pl
from jax.experimental.pallas import tpu as pltpu

D_MODEL = 1024
MEM_LEN = 256
CONV_CH = 512
CONV_WIDTH = 31
N_HEADS = 8
N_KV_HEADS = 2
HEAD_DIM = 64
GQ = N_HEADS // N_KV_HEADS
ATTN_W = N_HEADS * HEAD_DIM
KV_W = N_KV_HEADS * HEAD_DIM
WINDOW = 128
D_MIX = CONV_CH + ATTN_W
D_IN = 2 * CONV_CH + ATTN_W + 2 * KV_W
MEM_HEADS = 4
MEM_HEAD_DIM = D_MODEL // MEM_HEADS
N_GROUPS = 4
EXPERTS_PER_GROUP = 4
N_EXPERTS = N_GROUPS * EXPERTS_PER_GROUP
D_EXPERT = D_MODEL // 2
DEPTH = 1
ALPHA = (2.0 * DEPTH) ** 0.25
LN_EPS = 1e-5

LANES = 128
SUBLANES = 8
CONV_ROWS = 128
LN_ROWS = 64
MASK_VALUE = -1e30
CONV_HALO = 32
SEQ_TILE = 512
MOE_TILE = 512
CHUNK = 16
GMM_ROWS = 256
TOP_K = 2
SORT_ROWS = -(-(MOE_TILE * TOP_K + N_EXPERTS * (CHUNK - 1)) // 256) * 256
ROUTE_OFF = N_GROUPS
VMEM_LIMIT = 56 * 1024 * 1024

BF16 = jnp.bfloat16
F32 = jnp.float32


def _layer_norm(x, g, b):
    mu = jnp.mean(x, axis=-1, keepdims=True)
    xc = x - mu
    var = jnp.mean(xc * xc, axis=-1, keepdims=True)
    return xc * lax.rsqrt(var + LN_EPS) * g + b


def _dot_nt(a, b):
    return lax.dot_general(a, b, (((1,), (1,)), ((), ())), preferred_element_type=F32)


def _mixer_kernel(sinks_ref, x_ref, w_in_ref, b_in_ref, w_dw_ref, b_dw_ref, g_cn_ref, b_cn_ref,
                  w_out_ref, g1_ref, b1_ref, o_ref, hbuf, hshift, cbuf, qbuf, kbuf, vbuf, ymix):
    i = pl.program_id(1)
    ts = SEQ_TILE

    @pl.when(i == 0)
    def _():
        hbuf[0:CONV_HALO, :] = jnp.zeros((CONV_HALO, CONV_CH), F32)
        kbuf[:, 0:WINDOW, :] = jnp.zeros((2 * N_KV_HEADS, WINDOW, KV_W), BF16)
        vbuf[:, 0:WINDOW, :] = jnp.zeros((2, WINDOW, KV_W), BF16)

    x = x_ref[0]
    u = jnp.dot(x.astype(BF16), w_in_ref[...], preferred_element_type=F32) + b_in_ref[...]
    a = u[:, 0:CONV_CH]
    gate = u[:, CONV_CH:2 * CONV_CH]
    hbuf[CONV_HALO:CONV_HALO + ts, :] = a * jax.nn.sigmoid(gate)
    qbuf[...] = (u[:, 2 * CONV_CH:2 * CONV_CH + ATTN_W] * (HEAD_DIM ** -0.5)).astype(BF16)
    kf = u[:, 2 * CONV_CH + ATTN_W:2 * CONV_CH + ATTN_W + KV_W]
    vf = u[:, 2 * CONV_CH + ATTN_W + KV_W:D_IN]
    kr = pltpu.roll(kf, HEAD_DIM, axis=1)
    vr = pltpu.roll(vf, HEAD_DIM, axis=1)
    lo = lax.broadcasted_iota(jnp.int32, (ts, KV_W), 1) < HEAD_DIM
    rows = slice(WINDOW, WINDOW + ts)
    kbuf[0, rows, :] = jnp.where(lo, kf, 0.0).astype(BF16)
    kbuf[1, rows, :] = jnp.where(lo, 0.0, kr).astype(BF16)
    kbuf[2, rows, :] = jnp.where(lo, kr, 0.0).astype(BF16)
    kbuf[3, rows, :] = jnp.where(lo, 0.0, kf).astype(BF16)
    vbuf[0, rows, :] = vf.astype(BF16)
    vbuf[1, rows, :] = vr.astype(BF16)

    base = CONV_HALO - (CONV_WIDTH - 1)
    n_shift = ts + CONV_HALO - SUBLANES
    for b in range(1, SUBLANES):
        hshift[b - 1, 0:n_shift, :] = hbuf[b:b + n_shift, :]
    rc = CONV_ROWS
    for c in range(ts // rc):
        r0 = c * rc
        for l in range(CONV_CH // LANES):
            ls = slice(l * LANES, (l + 1) * LANES)
            acc = jnp.zeros((rc, LANES), F32)
            for j in range(CONV_WIDTH):
                a8, b = divmod(j + base, SUBLANES)
                rs = slice(r0 + SUBLANES * a8, r0 + SUBLANES * a8 + rc)
                tap = hbuf[rs, ls] if b == 0 else hshift[b - 1, rs, ls]
                acc = acc + tap * w_dw_ref[j:j + 1, ls]
            cbuf[r0:r0 + rc, ls] = acc
    for c in range(ts // LN_ROWS):
        rs = slice(c * LN_ROWS, (c + 1) * LN_ROWS)
        y = _layer_norm(cbuf[rs, :] + b_dw_ref[...], g_cn_ref[...], b_cn_ref[...])
        y = y * jax.nn.sigmoid(y)
        ymix[rs, 0:CONV_CH] = y.astype(BF16)

    qi = lax.broadcasted_iota(jnp.int32, (2 * WINDOW, 2 * WINDOW), 0) % WINDOW
    kj = lax.broadcasted_iota(jnp.int32, (2 * WINDOW, 2 * WINDOW), 1)
    dist = qi + WINDOW - kj
    band = (dist >= 0) & (dist < WINDOW)
    top = lax.broadcasted_iota(jnp.int32, (2 * WINDOW, 1), 0) < WINDOW
    lo_out = lax.broadcasted_iota(jnp.int32, (WINDOW, 2 * HEAD_DIM), 1) < HEAD_DIM

    def attn_block(jb, carry):
        r0 = pl.multiple_of(jb * WINDOW, WINDOW)
        first = jnp.logical_and(i == 0, jb == 0)
        valid = band & jnp.logical_or(jnp.logical_not(first), kj >= WINDOW)
        for kvh in range(N_KV_HEADS):
            h0 = kvh * GQ
            c0 = h0 * HEAD_DIM
            qs = jnp.concatenate([qbuf[pl.ds(r0, WINDOW), c0:c0 + 2 * HEAD_DIM],
                                  qbuf[pl.ds(r0, WINDOW), c0 + 2 * HEAD_DIM:c0 + 4 * HEAD_DIM]], axis=0)
            pv = []
            for par in range(2):
                kk = kbuf[2 * kvh + par, pl.ds(r0, 2 * WINDOW), :]
                vv = vbuf[(kvh + par) % 2, pl.ds(r0, 2 * WINDOW), :]
                s = jnp.where(valid, _dot_nt(qs, kk), MASK_VALUE)
                sink = jnp.where(top, sinks_ref[h0 + par], sinks_ref[h0 + 2 + par])
                m = jnp.maximum(jnp.max(s, axis=-1, keepdims=True), sink)
                p = jnp.exp(s - m)
                denom = jnp.sum(p, axis=-1, keepdims=True) + jnp.exp(sink - m)
                pv.append(jnp.dot(p.astype(BF16), vv, preferred_element_type=F32) / denom)
            for pair in range(2):
                rs = slice(pair * WINDOW, (pair + 1) * WINDOW)
                o = jnp.where(lo_out, pv[0][rs], pv[1][rs])
                cs = CONV_CH + c0 + pair * 2 * HEAD_DIM
                ymix[pl.ds(r0, WINDOW), cs:cs + 2 * HEAD_DIM] = o.astype(BF16)
        return carry

    lax.fori_loop(0, ts // WINDOW, attn_block, 0)

    mix = jnp.dot(ymix[...], w_out_ref[...], preferred_element_type=F32)
    o_ref[0] = _layer_norm(ALPHA * x + mix, g1_ref[...], b1_ref[...])

    hbuf[0:CONV_HALO, :] = hbuf[ts:ts + CONV_HALO, :]
    kbuf[:, 0:WINDOW, :] = kbuf[:, ts:ts + WINDOW, :]
    vbuf[:, 0:WINDOW, :] = vbuf[:, ts:ts + WINDOW, :]


def _const_spec(shape):
    nd = len(shape)
    return pl.BlockSpec(shape, lambda *_: (0,) * nd)


def _mixer(x, sinks, w_in, b_in, w_dw, b_dw, g_cn, b_cn, w_out, g1, b1):
    B, S, D = x.shape
    ts = SEQ_TILE
    tile = pl.BlockSpec((1, ts, D), lambda b, i: (b, i, 0))
    return pl.pallas_call(
        _mixer_kernel,
        grid=(B, S // ts),
        in_specs=[
            pl.BlockSpec(memory_space=pltpu.SMEM),
            tile,
            _const_spec((D, D_IN)), _const_spec((1, D_IN)),
            _const_spec((CONV_HALO, CONV_CH)), _const_spec((1, CONV_CH)),
            _const_spec((1, CONV_CH)), _const_spec((1, CONV_CH)),
            _const_spec((D_MIX, D)), _const_spec((1, D)), _const_spec((1, D)),
        ],
        out_specs=tile,
        out_shape=jax.ShapeDtypeStruct((B, S, D), F32),
        scratch_shapes=[
            pltpu.VMEM((CONV_HALO + ts, CONV_CH), F32),
            pltpu.VMEM((SUBLANES - 1, CONV_HALO + ts, CONV_CH), F32),
            pltpu.VMEM((ts, CONV_CH), F32),
            pltpu.VMEM((ts, ATTN_W), BF16),
            pltpu.VMEM((2 * N_KV_HEADS, WINDOW + ts, KV_W), BF16),
            pltpu.VMEM((2, WINDOW + ts, KV_W), BF16),
            pltpu.VMEM((ts, D_MIX), BF16),
        ],
        compiler_params=pltpu.CompilerParams(
            dimension_semantics=("arbitrary", "arbitrary"), vmem_limit_bytes=VMEM_LIMIT),
        name="mixer",
    )(sinks, x, w_in, b_in, w_dw, b_dw, g_cn, b_cn, w_out, g1, b1)


def _memkv_kernel(mem_ref, w_ref, o_ref):
    o_ref[...] = jnp.dot(mem_ref[...].astype(BF16), w_ref[...], preferred_element_type=F32).astype(BF16)


def _memkv(mem2d, w_mkv):
    M, D = mem2d.shape
    N = w_mkv.shape[1]
    tn = 512
    return pl.pallas_call(
        _memkv_kernel,
        grid=(N // tn,),
        in_specs=[pl.BlockSpec((M, D), lambda j: (0, 0)), pl.BlockSpec((D, tn), lambda j: (0, j))],
        out_specs=pl.BlockSpec((M, tn), lambda j: (0, j)),
        out_shape=jax.ShapeDtypeStruct((M, N), BF16),
        compiler_params=pltpu.CompilerParams(dimension_semantics=("arbitrary",)),
        name="memkv",
    )(mem2d, w_mkv)


def _route_plan(logits):
    tile = logits.shape[0]
    lane = lax.broadcasted_iota(jnp.int32, logits.shape, 1)
    big = jnp.int32(LANES)
    gl = jnp.where(lane < N_GROUPS, logits, MASK_VALUE)
    gmax = jnp.max(gl, axis=-1, keepdims=True)
    g_p = 1.0 / jnp.sum(jnp.exp(gl - gmax), axis=-1, keepdims=True)
    g_idx = jnp.min(jnp.where(gl == gmax, lane, big), axis=-1, keepdims=True)
    lo = ROUTE_OFF + g_idx * EXPERTS_PER_GROUP
    sel = (lane >= lo) & (lane < lo + EXPERTS_PER_GROUP)
    rl = jnp.where(sel, logits, MASK_VALUE)
    m1 = jnp.max(rl, axis=-1, keepdims=True)
    i1 = jnp.min(jnp.where(rl == m1, lane, big), axis=-1, keepdims=True)
    rl2 = jnp.where(lane == i1, MASK_VALUE, rl)
    m2 = jnp.max(rl2, axis=-1, keepdims=True)
    i2 = jnp.min(jnp.where(rl2 == m2, lane, big), axis=-1, keepdims=True)
    ex = jnp.exp(m2 - m1)
    w1 = 1.0 / (1.0 + ex)
    w2 = ex * w1

    hit1 = lane == i1 - ROUTE_OFF
    hit2 = lane == i2 - ROUTE_OFF
    oh = jnp.where(jnp.logical_or(hit1, hit2), 1.0, 0.0)
    r = lax.broadcasted_iota(jnp.int32, (tile, tile), 0)
    c = lax.broadcasted_iota(jnp.int32, (tile, tile), 1)
    tri = jnp.where(c <= r, 1.0, 0.0).astype(BF16)
    csum = jnp.dot(tri, oh.astype(BF16), preferred_element_type=F32)
    counts = csum[tile - 1:tile, :].astype(jnp.int32)
    nch = jnp.right_shift(counts + (CHUNK - 1), CHUNK.bit_length() - 1)
    er = lax.broadcasted_iota(jnp.int32, (LANES, LANES), 0)
    ec = lax.broadcasted_iota(jnp.int32, (LANES, LANES), 1)
    upper = jnp.where(er < ec, 1.0, 0.0).astype(BF16)
    nch8 = jnp.broadcast_to(nch.astype(F32), (SUBLANES, LANES)).astype(BF16)
    off = jnp.dot(nch8, upper, preferred_element_type=F32)[0:1, :] * CHUNK
    pos = off + csum - oh
    lp1 = jnp.sum(jnp.where(hit1, pos, 0.0), axis=-1, keepdims=True)
    lp2 = jnp.sum(jnp.where(hit2, pos, 0.0), axis=-1, keepdims=True)
    route = jnp.where(lane == 0, lp1, jnp.where(lane == 1, lp2,
                      jnp.where(lane == 2, g_p * w1, jnp.where(lane == 3, g_p * w2, 0.0))))
    row = lax.broadcasted_iota(jnp.int32, (SUBLANES, LANES), 0)
    meta = jnp.where(row == 0, nch, jnp.where(row == 1, off.astype(jnp.int32), 0))
    return route, meta


def _memattn_kernel(x_ref, wq_ref, k_ref, v_ref, wo_ref, g2_ref, b2_ref, wr2_ref, wrh_ref, br_ref,
                    o_ref, ob_ref, route_ref, routet_ref, meta_ref):
    x = x_ref[0]
    q = jnp.dot(x.astype(BF16), wq_ref[...], preferred_element_type=F32)
    q = (q * (MEM_HEAD_DIM ** -0.5)).astype(BF16)
    outs = []
    for h in range(MEM_HEADS):
        sl = slice(h * MEM_HEAD_DIM, (h + 1) * MEM_HEAD_DIM)
        s = _dot_nt(q[:, sl], k_ref[0, :, sl])
        m = jnp.max(s, axis=-1, keepdims=True)
        p = jnp.exp(s - m)
        denom = jnp.sum(p, axis=-1, keepdims=True)
        o = jnp.dot(p.astype(BF16), v_ref[0, :, sl], preferred_element_type=F32)
        outs.append((o / denom).astype(BF16))
    o = jnp.dot(jnp.concatenate(outs, axis=-1), wo_ref[...], preferred_element_type=F32)
    x2 = _layer_norm(ALPHA * x + o, g2_ref[...], b2_ref[...])
    o_ref[0] = x2
    x2h = x2.astype(BF16)
    ob_ref[0] = x2h
    x2l = (x2 - x2h.astype(F32)).astype(BF16)
    hh = jnp.dot(x2h, wr2_ref[...], preferred_element_type=F32)
    logits = (hh[:, 0:LANES] + hh[:, LANES:2 * LANES]
              + jnp.dot(x2l, wrh_ref[...], preferred_element_type=F32) + br_ref[...])
    route, meta = _route_plan(logits)
    route_ref[0] = route
    routet_ref[0] = jnp.transpose(route)[0:SUBLANES, :]
    meta_ref[0] = meta


def _memattn(x1, wq, kvm, wo, g2, b2, wr, br):
    B, S, D = x1.shape
    ts = SEQ_TILE
    assert ts == MOE_TILE
    nt = S // ts
    wrh = wr.astype(BF16)
    wr2 = jnp.concatenate([wrh, (wr - wrh.astype(F32)).astype(BF16)], axis=1)
    tile = pl.BlockSpec((1, ts, D), lambda b, i: (b, i, 0))
    kspec = pl.BlockSpec((1, MEM_LEN, D), lambda b, i: (b, 0, 0))
    vspec = pl.BlockSpec((1, MEM_LEN, D), lambda b, i: (b, 0, 1))
    return pl.pallas_call(
        _memattn_kernel,
        grid=(B, nt),
        in_specs=[tile, _const_spec((D, D)), kspec, vspec, _const_spec((D, D)),
                  _const_spec((1, D)), _const_spec((1, D)),
                  _const_spec((D, 2 * LANES)), _const_spec((D, LANES)), _const_spec((1, LANES))],
        out_specs=[tile, tile, pl.BlockSpec((1, ts, LANES), lambda b, i: (b, i, 0)),
                   pl.BlockSpec((1, SUBLANES, ts), lambda b, i: (b * nt + i, 0, 0)),
                   pl.BlockSpec((1, SUBLANES, LANES), lambda b, i: (b * nt + i, 0, 0))],
        out_shape=[jax.ShapeDtypeStruct((B, S, D), F32),
                   jax.ShapeDtypeStruct((B, S, D), BF16),
                   jax.ShapeDtypeStruct((B, S, LANES), F32),
                   jax.ShapeDtypeStruct((B * nt, SUBLANES, ts), F32),
                   jax.ShapeDtypeStruct((B * nt, SUBLANES, LANES), jnp.int32)],
        compiler_params=pltpu.CompilerParams(
            dimension_semantics=("arbitrary", "arbitrary"), vmem_limit_bytes=VMEM_LIMIT),
        name="memattn",
    )(x1, wq, kvm, kvm, wo, g2, b2, wr2, wrh, br)


def _gmm_blocks(n_tokens):
    rows = (n_tokens * TOP_K + (n_tokens // MOE_TILE) * N_EXPERTS * (CHUNK - 1)
            + N_EXPERTS * (GMM_ROWS - CHUNK))
    return -(-rows // GMM_ROWS)


def _plan(meta, T):
    nch = meta[:, 0, :N_EXPERTS]
    off16 = meta[:, 1, :N_EXPERTS]
    n16 = nch * CHUNK
    n_e = jnp.sum(n16, axis=0)
    reg = (n_e + GMM_ROWS - 1) // GMM_ROWS * GMM_ROWS
    gend = jnp.cumsum(reg)
    gbase = gend - reg
    dst = gbase[None, :] + jnp.cumsum(n16, axis=0) - n16
    blk_row = jnp.arange(_gmm_blocks(T), dtype=jnp.int32)[:, None] * GMM_ROWS
    blk_expert = jnp.sum(blk_row >= gend[None, :], axis=1)
    blk_expert = jnp.minimum(blk_expert, N_EXPERTS - 1).astype(jnp.int32)
    n_used = (gend[-1] // GMM_ROWS).astype(jnp.int32).reshape(1)
    i32 = lambda a: a.astype(jnp.int32)
    fill_start = jnp.concatenate([gbase + n_e, gend[-1:]])
    fill_n = jnp.concatenate([reg - n_e, _gmm_blocks(T) * GMM_ROWS - gend[-1:]]) // CHUNK
    return dict(src=i32(off16).reshape(-1), dst=i32(dst).reshape(-1), nch=i32(nch).reshape(-1),
                ntot=i32(jnp.sum(nch, axis=1)), fill_start=i32(fill_start), fill_n=i32(fill_n),
                fill_tot=i32(jnp.sum(fill_n)).reshape(1), blk_expert=blk_expert, n_used=n_used)


def _chunk_copy(src_ref, src_row, dst_ref, dst_row, sem):
    return pltpu.make_async_copy(
        src_ref.at[pl.ds(pl.multiple_of(src_row, CHUNK), CHUNK), :],
        dst_ref.at[pl.ds(pl.multiple_of(dst_row, CHUNK), CHUNK), :], sem)


def _wait_chunks(n, src_ref, dst_ref, sem):
    def body(c, carry):
        _chunk_copy(src_ref, 0, dst_ref, 0, sem).wait()
        return carry

    lax.fori_loop(0, n, body, 0)


def _dispatch_kernel(src_ref, dst_ref, nch_ref, ntot_ref, fstart_ref, fn_ref, ftot_ref,
                     lp_ref, x_ref, xs_hbm, xt, zbuf, sems):
    t = pl.program_id(0)
    nt = pl.num_programs(0)
    slot = t % 2
    xt_s = xt.at[slot]
    sem = sems.at[slot]

    @pl.when(t >= 2)
    def _():
        _wait_chunks(ntot_ref[t - 2], xt_s, xs_hbm, sem)

    lp = lp_ref[0].astype(jnp.int32)
    r = lax.broadcasted_iota(jnp.int32, (SORT_ROWS, MOE_TILE), 0)
    hit = jnp.logical_or(lp[0:1, :] == r, lp[1:2, :] == r)
    p = jnp.where(hit, 1.0, 0.0).astype(BF16)
    xt_s[...] = jnp.dot(p, x_ref[...], preferred_element_type=F32).astype(BF16)

    def per_expert(e, carry):
        k = t * N_EXPERTS + e

        def issue(c, carry2):
            _chunk_copy(xt_s, src_ref[k] + c * CHUNK, xs_hbm, dst_ref[k] + c * CHUNK, sem).start()
            return carry2

        return lax.fori_loop(0, nch_ref[k], issue, carry)

    lax.fori_loop(0, N_EXPERTS, per_expert, 0)

    @pl.when(t == nt - 1)
    def _():
        zbuf[...] = jnp.zeros_like(zbuf)

        def per_range(e, carry):
            def issue(c, carry2):
                _chunk_copy(zbuf, 0, xs_hbm, fstart_ref[e] + c * CHUNK, sem).start()
                return carry2

            return lax.fori_loop(0, fn_ref[e], issue, carry)

        lax.fori_loop(0, N_EXPERTS + 1, per_range, 0)
        _wait_chunks(ntot_ref[t - 1], xt.at[1 - slot], xs_hbm, sems.at[1 - slot])
        _wait_chunks(ntot_ref[t] + ftot_ref[0], xt_s, xs_hbm, sem)


def _dispatch(plan, lpt, x2b):
    T, D = x2b.shape
    nt = T // MOE_TILE
    assert nt >= 2
    grid_spec = pltpu.PrefetchScalarGridSpec(
        num_scalar_prefetch=7,
        grid=(nt,),
        in_specs=[pl.BlockSpec((1, SUBLANES, MOE_TILE), lambda t, *_: (t, 0, 0)),
                  pl.BlockSpec((MOE_TILE, D), lambda t, *_: (t, 0))],
        out_specs=pl.BlockSpec(memory_space=pl.ANY),
        scratch_shapes=[pltpu.VMEM((2, SORT_ROWS, D), BF16), pltpu.VMEM((CHUNK, D), BF16),
                        pltpu.SemaphoreType.DMA((2,))],
    )
    return pl.pallas_call(
        _dispatch_kernel,
        grid_spec=grid_spec,
        out_shape=jax.ShapeDtypeStruct((_gmm_blocks(T) * GMM_ROWS, D), BF16),
        compiler_params=pltpu.CompilerParams(
            dimension_semantics=("arbitrary",), vmem_limit_bytes=VMEM_LIMIT),
        name="moe_dispatch",
    )(plan["src"], plan["dst"], plan["nch"], plan["ntot"], plan["fill_start"], plan["fill_n"],
      plan["fill_tot"], lpt, x2b)


def _gmm_kernel(be_ref, nu_ref, x_ref, wg_ref, wu_ref, wd_ref, o_ref):
    @pl.when(pl.program_id(0) < nu_ref[0])
    def _():
        xb = x_ref[...]
        g = jnp.dot(xb, wg_ref[0], preferred_element_type=F32)
        u = jnp.dot(xb, wu_ref[0], preferred_element_type=F32)
        h = (g * jax.nn.sigmoid(g)) * u
        o_ref[...] = jnp.dot(h.astype(BF16), wd_ref[0], preferred_element_type=F32).astype(BF16)

    @pl.when(pl.program_id(0) >= nu_ref[0])
    def _():
        o_ref[...] = jnp.zeros_like(o_ref)


def _gmm(plan, xs, wg, wu, wd):
    R, D = xs.shape
    rows = pl.BlockSpec((GMM_ROWS, D), lambda b, be, nu: (jnp.minimum(b, nu[0] - 1), 0))
    grid_spec = pltpu.PrefetchScalarGridSpec(
        num_scalar_prefetch=2,
        grid=(R // GMM_ROWS,),
        in_specs=[rows,
                  pl.BlockSpec((1, D, D_EXPERT), lambda b, be, nu: (be[b], 0, 0)),
                  pl.BlockSpec((1, D, D_EXPERT), lambda b, be, nu: (be[b], 0, 0)),
                  pl.BlockSpec((1, D_EXPERT, D), lambda b, be, nu: (be[b], 0, 0))],
        out_specs=pl.BlockSpec((GMM_ROWS, D), lambda b, be, nu: (b, 0)),
    )
    return pl.pallas_call(
        _gmm_kernel,
        grid_spec=grid_spec,
        out_shape=jax.ShapeDtypeStruct((R, D), BF16),
        compiler_params=pltpu.CompilerParams(
            dimension_semantics=("arbitrary",), vmem_limit_bytes=VMEM_LIMIT),
        name="moe_gmm",
    )(plan["blk_expert"], plan["n_used"], xs, wg, wu, wd)


def _combine_kernel(src_ref, dst_ref, nch_ref, ntot_ref, cm_ref, x_ref, ys_hbm, g3_ref, b3_ref, o_ref,
                    yt, sems):
    t = pl.program_id(0)
    nt = pl.num_programs(0)
    slot = t % 2

    def fetch(tile, s):
        def per_expert(e, carry):
            k = tile * N_EXPERTS + e

            def issue(c, carry2):
                _chunk_copy(ys_hbm, dst_ref[k] + c * CHUNK, yt.at[s], src_ref[k] + c * CHUNK,
                            sems.at[s]).start()
                return carry2

            return lax.fori_loop(0, nch_ref[k], issue, carry)

        lax.fori_loop(0, N_EXPERTS, per_expert, 0)

    @pl.when(t == 0)
    def _():
        yt[...] = jnp.zeros_like(yt)
        fetch(0, 0)

    @pl.when(t + 1 < nt)
    def _():
        fetch(t + 1, 1 - slot)

    cm = cm_ref[...]
    col = lax.broadcasted_iota(jnp.int32, (MOE_TILE, SORT_ROWS), 1)
    lp0 = cm[:, 0:1].astype(jnp.int32)
    lp1 = cm[:, 1:2].astype(jnp.int32)
    w = jnp.where(col == lp0, cm[:, 2:3], 0.0) + jnp.where(col == lp1, cm[:, 3:4], 0.0)
    w = w.astype(BF16)

    _wait_chunks(ntot_ref[t], ys_hbm, yt.at[slot], sems.at[slot])
    y = jnp.dot(w, yt[slot], preferred_element_type=F32)
    o_ref[...] = _layer_norm(ALPHA * x_ref[...] + y, g3_ref[...], b3_ref[...])


def _combine(plan, route, x2, ys, g3, b3):
    T, D = x2.shape
    nt = T // MOE_TILE
    grid_spec = pltpu.PrefetchScalarGridSpec(
        num_scalar_prefetch=4,
        grid=(nt,),
        in_specs=[pl.BlockSpec((MOE_TILE, LANES), lambda t, *_: (t, 0)),
                  pl.BlockSpec((MOE_TILE, D), lambda t, *_: (t, 0)),
                  pl.BlockSpec(memory_space=pl.ANY),
                  pl.BlockSpec((1, D), lambda t, *_: (0, 0)),
                  pl.BlockSpec((1, D), lambda t, *_: (0, 0))],
        out_specs=pl.BlockSpec((MOE_TILE, D), lambda t, *_: (t, 0)),
        scratch_shapes=[pltpu.VMEM((2, SORT_ROWS, D), BF16), pltpu.SemaphoreType.DMA((2,))],
    )
    return pl.pallas_call(
        _combine_kernel,
        grid_spec=grid_spec,
        out_shape=jax.ShapeDtypeStruct((T, D), F32),
        compiler_params=pltpu.CompilerParams(
            dimension_semantics=("arbitrary",), vmem_limit_bytes=VMEM_LIMIT),
        name="moe_combine",
    )(plan["src"], plan["dst"], plan["nch"], plan["ntot"], route, x2, ys, g3, b3)


def _moe(x2, x2b, route, lpt, meta, wg, wu, wd, g3, b3):
    plan = _plan(meta, x2.shape[0])
    xs = _dispatch(plan, lpt, x2b)
    ys = _gmm(plan, xs, wg, wu, wd)
    return _combine(plan, route, x2, ys, g3, b3)


def _row(v):
    return v.reshape(1, -1).astype(F32)


def kernel(x, mem, w_in, b_in, w_dw, b_dw, g_conv_norm, b_conv_norm, attn_sinks, w_out, g_ln1, b_ln1,
           w_mq, w_mkv, w_mo, g_ln2, b_ln2, w_group, b_group, w_router, b_router, w_gate, w_up, w_down,
           g_ln3, b_ln3):
    B, S, D = x.shape
    for l in range(DEPTH):
        w_dw_p = jnp.zeros((CONV_HALO, CONV_CH), F32).at[:CONV_WIDTH].set(w_dw[l])
        x1 = _mixer(x, attn_sinks[l].astype(F32), w_in[l].astype(BF16), _row(b_in[l]), w_dw_p,
                    _row(b_dw[l]), _row(g_conv_norm[l]), _row(b_conv_norm[l]),
                    w_out[l].astype(BF16), _row(g_ln1[l]), _row(b_ln1[l]))

        kvm = _memkv(mem.reshape(B * MEM_LEN, D), w_mkv[l].astype(BF16)).reshape(B, MEM_LEN, 2 * D)

        wr = jnp.concatenate(
            [w_group[l], jnp.transpose(w_router[l], (1, 0, 2)).reshape(D, N_EXPERTS)], axis=1)
        wr = jnp.pad(wr, ((0, 0), (0, LANES - wr.shape[1])))
        br = jnp.pad(jnp.concatenate([b_group[l], b_router[l].reshape(-1)]), (0, LANES - N_GROUPS - N_EXPERTS))
        x2, x2b, route, lpt, meta = _memattn(x1, w_mq[l].astype(BF16), kvm, w_mo[l].astype(BF16),
                                 _row(g_ln2[l]), _row(b_ln2[l]), wr.astype(F32), _row(br))

        T = B * S
        y = _moe(x2.reshape(T, D), x2b.reshape(T, D), route.reshape(T, LANES), lpt, meta,
                 w_gate[l].astype(BF16), w_up[l].astype(BF16), w_down[l].astype(BF16),
                 _row(g_ln3[l]), _row(b_ln3[l]))
        x = y.reshape(B, S, D)
    return x
```

```python
import functools

import jax
import jax.numpy as jnp
from jax import lax
from jax.experimental import pallas as pl
from jax.experimental.pallas import tpu as pltpu

D_MODEL = 1024
MEM_LEN = 256
CONV_CH = 512
CONV_WIDTH = 31
N_HEADS = 8
N_KV_HEADS = 2
HEAD_DIM = 64
GQ = N_HEADS // N_KV_HEADS
ATTN_W = N_HEADS * HEAD_DIM
KV_W = N_KV_HEADS * HEAD_DIM
WINDOW = 128
D_MIX = CONV_CH + ATTN_W
D_IN = 2 * CONV_CH + ATTN_W + 2 * KV_W
MEM_HEADS = 4
MEM_HEAD_DIM = D_MODEL // MEM_HEADS
N_GROUPS = 4
EXPERTS_PER_GROUP = 4
N_EXPERTS = N_GROUPS * EXPERTS_PER_GROUP
D_EXPERT = D_MODEL // 2
DEPTH = 1
ALPHA = (2.0 * DEPTH) ** 0.25
LN_EPS = 1e-5

LANES = 128
SUBLANES = 8
CONV_ROWS = 128
LN_ROWS = 64
MASK_VALUE = -1e30
CONV_HALO = 32
SEQ_TILE = 512
MOE_TILE = 512
CHUNK = 16
GMM_ROWS = 512
TOP_K = 2
SORT_ROWS = -(-(MOE_TILE * TOP_K + N_EXPERTS * (CHUNK - 1)) // 256) * 256
ROUTE_OFF = N_GROUPS
VMEM_LIMIT = 56 * 1024 * 1024

BF16 = jnp.bfloat16
F32 = jnp.float32


def _layer_norm(x, g, b):
    mu = jnp.mean(x, axis=-1, keepdims=True)
    xc = x - mu
    var = jnp.mean(xc * xc, axis=-1, keepdims=True)
    return xc * lax.rsqrt(var + LN_EPS) * g + b


def _cast_bf16(dst_ref, src_ref):
    rows = 256
    for r0 in range(0, src_ref.shape[0], rows):
        dst_ref[r0:r0 + rows, :] = src_ref[r0:r0 + rows, :].astype(BF16)


def _dot_nt(a, b):
    return lax.dot_general(a, b, (((1,), (1,)), ((), ())), preferred_element_type=F32)


def _mixer_kernel(sinks_ref, x_ref, w_in_ref, b_in_ref, w_dw_ref, b_dw_ref, g_cn_ref, b_cn_ref,
                  w_out_ref, g1_ref, b1_ref, o_ref, w_in_b, w_out_b, hbuf, hshift, cbuf, qbuf, kbuf, vbuf, ymix):
    i = pl.program_id(1)
    ts = SEQ_TILE

    @pl.when(jnp.logical_and(pl.program_id(0) == 0, i == 0))
    def _():
        _cast_bf16(w_in_b, w_in_ref)
        _cast_bf16(w_out_b, w_out_ref)

    @pl.when(i == 0)
    def _():
        hbuf[0:CONV_HALO, :] = jnp.zeros((CONV_HALO, CONV_CH), F32)
        kbuf[:, 0:WINDOW, :] = jnp.zeros((2 * N_KV_HEADS, WINDOW, KV_W), BF16)
        vbuf[:, 0:WINDOW, :] = jnp.zeros((2, WINDOW, KV_W), BF16)

    x = x_ref[0]
    u = jnp.dot(x.astype(BF16), w_in_b[...], preferred_element_type=F32) + b_in_ref[...]
    a = u[:, 0:CONV_CH]
    gate = u[:, CONV_CH:2 * CONV_CH]
    hbuf[CONV_HALO:CONV_HALO + ts, :] = a * jax.nn.sigmoid(gate)
    qbuf[...] = (u[:, 2 * CONV_CH:2 * CONV_CH + ATTN_W] * (HEAD_DIM ** -0.5)).astype(BF16)
    kf = u[:, 2 * CONV_CH + ATTN_W:2 * CONV_CH + ATTN_W + KV_W]
    vf = u[:, 2 * CONV_CH + ATTN_W + KV_W:D_IN]
    kr = pltpu.roll(kf, HEAD_DIM, axis=1)
    vr = pltpu.roll(vf, HEAD_DIM, axis=1)
    lo = lax.broadcasted_iota(jnp.int32, (ts, KV_W), 1) < HEAD_DIM
    rows = slice(WINDOW, WINDOW + ts)
    kbuf[0, rows, :] = jnp.where(lo, kf, 0.0).astype(BF16)
    kbuf[1, rows, :] = jnp.where(lo, 0.0, kr).astype(BF16)
    kbuf[2, rows, :] = jnp.where(lo, kr, 0.0).astype(BF16)
    kbuf[3, rows, :] = jnp.where(lo, 0.0, kf).astype(BF16)
    vbuf[0, rows, :] = vf.astype(BF16)
    vbuf[1, rows, :] = vr.astype(BF16)

    base = CONV_HALO - (CONV_WIDTH - 1)
    n_shift = ts + CONV_HALO - SUBLANES
    for b in range(1, SUBLANES):
        hshift[b - 1, 0:n_shift, :] = hbuf[b:b + n_shift, :]
    rc = CONV_ROWS
    for c in range(ts // rc):
        r0 = c * rc
        for l in range(CONV_CH // LANES):
            ls = slice(l * LANES, (l + 1) * LANES)
            acc = jnp.zeros((rc, LANES), F32)
            for j in range(CONV_WIDTH):
                a8, b = divmod(j + base, SUBLANES)
                rs = slice(r0 + SUBLANES * a8, r0 + SUBLANES * a8 + rc)
                tap = hbuf[rs, ls] if b == 0 else hshift[b - 1, rs, ls]
                acc = acc + tap * w_dw_ref[j:j + 1, ls]
            cbuf[r0:r0 + rc, ls] = acc
    for c in range(ts // LN_ROWS):
        rs = slice(c * LN_ROWS, (c + 1) * LN_ROWS)
        y = _layer_norm(cbuf[rs, :] + b_dw_ref[...], g_cn_ref[...], b_cn_ref[...])
        y = y * jax.nn.sigmoid(y)
        ymix[rs, 0:CONV_CH] = y.astype(BF16)

    qi = lax.broadcasted_iota(jnp.int32, (2 * WINDOW, 2 * WINDOW), 0) % WINDOW
    kj = lax.broadcasted_iota(jnp.int32, (2 * WINDOW, 2 * WINDOW), 1)
    dist = qi + WINDOW - kj
    band = (dist >= 0) & (dist < WINDOW)
    top = lax.broadcasted_iota(jnp.int32, (2 * WINDOW, 1), 0) < WINDOW
    lo_out = lax.broadcasted_iota(jnp.int32, (WINDOW, 2 * HEAD_DIM), 1) < HEAD_DIM

    def attn_block(jb, carry):
        r0 = pl.multiple_of(jb * WINDOW, WINDOW)
        first = jnp.logical_and(i == 0, jb == 0)
        valid = band & jnp.logical_or(jnp.logical_not(first), kj >= WINDOW)
        for kvh in range(N_KV_HEADS):
            h0 = kvh * GQ
            c0 = h0 * HEAD_DIM
            qs = jnp.concatenate([qbuf[pl.ds(r0, WINDOW), c0:c0 + 2 * HEAD_DIM],
                                  qbuf[pl.ds(r0, WINDOW), c0 + 2 * HEAD_DIM:c0 + 4 * HEAD_DIM]], axis=0)
            pv = []
            for par in range(2):
                kk = kbuf[2 * kvh + par, pl.ds(r0, 2 * WINDOW), :]
                vv = vbuf[(kvh + par) % 2, pl.ds(r0, 2 * WINDOW), :]
                s = jnp.where(valid, _dot_nt(qs, kk), MASK_VALUE)
                sink = jnp.where(top, sinks_ref[h0 + par], sinks_ref[h0 + 2 + par])
                m = jnp.maximum(jnp.max(s, axis=-1, keepdims=True), sink)
                p = jnp.exp(s - m)
                denom = jnp.sum(p, axis=-1, keepdims=True) + jnp.exp(sink - m)
                pv.append(jnp.dot(p.astype(BF16), vv, preferred_element_type=F32) / denom)
            for pair in range(2):
                rs = slice(pair * WINDOW, (pair + 1) * WINDOW)
                o = jnp.where(lo_out, pv[0][rs], pv[1][rs])
                cs = CONV_CH + c0 + pair * 2 * HEAD_DIM
                ymix[pl.ds(r0, WINDOW), cs:cs + 2 * HEAD_DIM] = o.astype(BF16)
        return carry

    lax.fori_loop(0, ts // WINDOW, attn_block, 0)

    mix = jnp.dot(ymix[...], w_out_b[...], preferred_element_type=F32)
    o_ref[0] = _layer_norm(ALPHA * x + mix, g1_ref[...], b1_ref[...])

    hbuf[0:CONV_HALO, :] = hbuf[ts:ts + CONV_HALO, :]
    kbuf[:, 0:WINDOW, :] = kbuf[:, ts:ts + WINDOW, :]
    vbuf[:, 0:WINDOW, :] = vbuf[:, ts:ts + WINDOW, :]


def _const_spec(shape):
    nd = len(shape)
    return pl.BlockSpec(shape, lambda *_: (0,) * nd)


def _resident_spec(shape):
    nd = len(shape)
    return pl.BlockSpec(shape, lambda *_: (0,) * nd, pipeline_mode=pl.Buffered(1))


def _mixer(x, sinks, w_in, b_in, w_dw, b_dw, g_cn, b_cn, w_out, g1, b1):
    B, S, D = x.shape
    ts = SEQ_TILE
    tile = pl.BlockSpec((1, ts, D), lambda b, i: (b, i, 0))
    return pl.pallas_call(
        _mixer_kernel,
        grid=(B, S // ts),
        in_specs=[
            pl.BlockSpec(memory_space=pltpu.SMEM),
            tile,
            _resident_spec((D, D_IN)), _const_spec((1, D_IN)),
            _const_spec((CONV_HALO, CONV_CH)), _const_spec((1, CONV_CH)),
            _const_spec((1, CONV_CH)), _const_spec((1, CONV_CH)),
            _resident_spec((D_MIX, D)), _const_spec((1, D)), _const_spec((1, D)),
        ],
        out_specs=tile,
        out_shape=jax.ShapeDtypeStruct((B, S, D), F32),
        scratch_shapes=[
            pltpu.VMEM((D, D_IN), BF16),
            pltpu.VMEM((D_MIX, D), BF16),
            pltpu.VMEM((CONV_HALO + ts, CONV_CH), F32),
            pltpu.VMEM((SUBLANES - 1, CONV_HALO + ts, CONV_CH), F32),
            pltpu.VMEM((ts, CONV_CH), F32),
            pltpu.VMEM((ts, ATTN_W), BF16),
            pltpu.VMEM((2 * N_KV_HEADS, WINDOW + ts, KV_W), BF16),
            pltpu.VMEM((2, WINDOW + ts, KV_W), BF16),
            pltpu.VMEM((ts, D_MIX), BF16),
        ],
        compiler_params=pltpu.CompilerParams(
            dimension_semantics=("arbitrary", "arbitrary"), vmem_limit_bytes=VMEM_LIMIT),
        name="mixer",
    )(sinks, x, w_in, b_in, w_dw, b_dw, g_cn, b_cn, w_out, g1, b1)


def _memkv_kernel(mem_ref, w_ref, o_ref):
    o_ref[...] = jnp.dot(mem_ref[...].astype(BF16), w_ref[...].astype(BF16),
                         preferred_element_type=F32).astype(BF16)


def _memkv(mem2d, w_mkv):
    M, D = mem2d.shape
    N = w_mkv.shape[1]
    tn = 512
    return pl.pallas_call(
        _memkv_kernel,
        grid=(N // tn,),
        in_specs=[pl.BlockSpec((M, D), lambda j: (0, 0)), pl.BlockSpec((D, tn), lambda j: (0, j))],
        out_specs=pl.BlockSpec((M, tn), lambda j: (0, j)),
        out_shape=jax.ShapeDtypeStruct((M, N), BF16),
        compiler_params=pltpu.CompilerParams(dimension_semantics=("arbitrary",)),
        name="memkv",
    )(mem2d, w_mkv)


def _route_plan(logits):
    tile = logits.shape[0]
    lane = lax.broadcasted_iota(jnp.int32, logits.shape, 1)
    big = jnp.int32(LANES)
    gl = jnp.where(lane < N_GROUPS, logits, MASK_VALUE)
    gmax = jnp.max(gl, axis=-1, keepdims=True)
    g_p = 1.0 / jnp.sum(jnp.exp(gl - gmax), axis=-1, keepdims=True)
    g_idx = jnp.min(jnp.where(gl == gmax, lane, big), axis=-1, keepdims=True)
    lo = ROUTE_OFF + g_idx * EXPERTS_PER_GROUP
    sel = (lane >= lo) & (lane < lo + EXPERTS_PER_GROUP)
    rl = jnp.where(sel, logits, MASK_VALUE)
    m1 = jnp.max(rl, axis=-1, keepdims=True)
    i1 = jnp.min(jnp.where(rl == m1, lane, big), axis=-1, keepdims=True)
    rl2 = jnp.where(lane == i1, MASK_VALUE, rl)
    m2 = jnp.max(rl2, axis=-1, keepdims=True)
    i2 = jnp.min(jnp.where(rl2 == m2, lane, big), axis=-1, keepdims=True)
    ex = jnp.exp(m2 - m1)
    w1 = 1.0 / (1.0 + ex)
    w2 = ex * w1

    hit1 = lane == i1 - ROUTE_OFF
    hit2 = lane == i2 - ROUTE_OFF
    oh = jnp.where(jnp.logical_or(hit1, hit2), 1.0, 0.0)
    r = lax.broadcasted_iota(jnp.int32, (tile, tile), 0)
    c = lax.broadcasted_iota(jnp.int32, (tile, tile), 1)
    tri = jnp.where(c <= r, 1.0, 0.0).astype(BF16)
    csum = jnp.dot(tri, oh.astype(BF16), preferred_element_type=F32)
    counts = csum[tile - 1:tile, :].astype(jnp.int32)
    nch = jnp.right_shift(counts + (CHUNK - 1), CHUNK.bit_length() - 1)
    er = lax.broadcasted_iota(jnp.int32, (LANES, LANES), 0)
    ec = lax.broadcasted_iota(jnp.int32, (LANES, LANES), 1)
    upper = jnp.where(er < ec, 1.0, 0.0).astype(BF16)
    nch8 = jnp.broadcast_to(nch.astype(F32), (SUBLANES, LANES)).astype(BF16)
    off = jnp.dot(nch8, upper, preferred_element_type=F32)[0:1, :] * CHUNK
    pos = off + csum - oh
    lp1 = jnp.sum(jnp.where(hit1, pos, 0.0), axis=-1, keepdims=True)
    lp2 = jnp.sum(jnp.where(hit2, pos, 0.0), axis=-1, keepdims=True)
    route = jnp.where(lane == 0, lp1, jnp.where(lane == 1, lp2,
                      jnp.where(lane == 2, g_p * w1, jnp.where(lane == 3, g_p * w2, 0.0))))
    row = lax.broadcasted_iota(jnp.int32, (SUBLANES, LANES), 0)
    meta = jnp.where(row == 0, nch, jnp.where(row == 1, off.astype(jnp.int32), 0))
    return route, meta


def _memattn_kernel(x_ref, wq_ref, k_ref, v_ref, wo_ref, g2_ref, b2_ref, wr2_ref, wrh_ref, br_ref,
                    o_ref, ob_ref, route_ref, routet_ref, meta_ref, wq_b, wo_b):
    @pl.when(jnp.logical_and(pl.program_id(0) == 0, pl.program_id(1) == 0))
    def _():
        _cast_bf16(wq_b, wq_ref)
        _cast_bf16(wo_b, wo_ref)

    x = x_ref[0]
    q = jnp.dot(x.astype(BF16), wq_b[...], preferred_element_type=F32)
    q = (q * (MEM_HEAD_DIM ** -0.5)).astype(BF16)
    outs = []
    for h in range(MEM_HEADS):
        sl = slice(h * MEM_HEAD_DIM, (h + 1) * MEM_HEAD_DIM)
        s = _dot_nt(q[:, sl], k_ref[0, :, sl])
        m = jnp.max(s, axis=-1, keepdims=True)
        p = jnp.exp(s - m)
        denom = jnp.sum(p, axis=-1, keepdims=True)
        o = jnp.dot(p.astype(BF16), v_ref[0, :, sl], preferred_element_type=F32)
        outs.append((o / denom).astype(BF16))
    o = jnp.dot(jnp.concatenate(outs, axis=-1), wo_b[...], preferred_element_type=F32)
    x2 = _layer_norm(ALPHA * x + o, g2_ref[...], b2_ref[...])
    o_ref[0] = x2
    x2h = x2.astype(BF16)
    ob_ref[0] = x2h
    x2l = (x2 - x2h.astype(F32)).astype(BF16)
    hh = jnp.dot(x2h, wr2_ref[...], preferred_element_type=F32)
    logits = (hh[:, 0:LANES] + hh[:, LANES:2 * LANES]
              + jnp.dot(x2l, wrh_ref[...], preferred_element_type=F32) + br_ref[...])
    route, meta = _route_plan(logits)
    route_ref[0] = route
    routet_ref[0] = jnp.transpose(route)[0:SUBLANES, :]
    meta_ref[0] = meta


def _memattn(x1, wq, kvm, wo, g2, b2, wr, br):
    B, S, D = x1.shape
    ts = SEQ_TILE
    assert ts == MOE_TILE
    nt = S // ts
    c = wr * (2.0 ** 16 + 1.0)
    w_high = c - (c - wr)
    wrh = w_high.astype(BF16)
    wr2 = jnp.concatenate([wrh, (wr - w_high).astype(BF16)], axis=1)
    tile = pl.BlockSpec((1, ts, D), lambda b, i: (b, i, 0))
    kspec = pl.BlockSpec((1, MEM_LEN, D), lambda b, i: (b, 0, 0))
    vspec = pl.BlockSpec((1, MEM_LEN, D), lambda b, i: (b, 0, 1))
    return pl.pallas_call(
        _memattn_kernel,
        grid=(B, nt),
        in_specs=[tile, _resident_spec((D, D)), kspec, vspec, _resident_spec((D, D)),
                  _const_spec((1, D)), _const_spec((1, D)),
                  _const_spec((D, 2 * LANES)), _const_spec((D, LANES)), _const_spec((1, LANES))],
        out_specs=[tile, tile, pl.BlockSpec((1, ts, LANES), lambda b, i: (b, i, 0)),
                   pl.BlockSpec((1, SUBLANES, ts), lambda b, i: (b * nt + i, 0, 0)),
                   pl.BlockSpec((1, SUBLANES, LANES), lambda b, i: (b * nt + i, 0, 0))],
        out_shape=[jax.ShapeDtypeStruct((B, S, D), F32),
                   jax.ShapeDtypeStruct((B, S, D), BF16),
                   jax.ShapeDtypeStruct((B, S, LANES), F32),
                   jax.ShapeDtypeStruct((B * nt, SUBLANES, ts), F32),
                   jax.ShapeDtypeStruct((B * nt, SUBLANES, LANES), jnp.int32)],
        scratch_shapes=[pltpu.VMEM((D, D), BF16), pltpu.VMEM((D, D), BF16)],
        compiler_params=pltpu.CompilerParams(
            dimension_semantics=("arbitrary", "arbitrary"), vmem_limit_bytes=VMEM_LIMIT),
        name="memattn",
    )(x1, wq, kvm, kvm, wo, g2, b2, wr2, wrh, br)


def _gmm_blocks(n_tokens):
    rows = (n_tokens * TOP_K + (n_tokens // MOE_TILE) * N_EXPERTS * (CHUNK - 1)
            + N_EXPERTS * (GMM_ROWS - CHUNK))
    return -(-rows // GMM_ROWS)


def _plan(meta, T):
    nch = meta[:, 0, :N_EXPERTS]
    off16 = meta[:, 1, :N_EXPERTS]
    n16 = nch * CHUNK
    n_e = jnp.sum(n16, axis=0)
    reg = (n_e + GMM_ROWS - 1) // GMM_ROWS * GMM_ROWS
    gend = jnp.cumsum(reg)
    gbase = gend - reg
    dst = gbase[None, :] + jnp.cumsum(n16, axis=0) - n16
    blk_row = jnp.arange(_gmm_blocks(T), dtype=jnp.int32)[:, None] * GMM_ROWS
    blk_expert = jnp.sum(blk_row >= gend[None, :], axis=1)
    blk_expert = jnp.minimum(blk_expert, N_EXPERTS - 1).astype(jnp.int32)
    n_used = (gend[-1] // GMM_ROWS).astype(jnp.int32).reshape(1)
    i32 = lambda a: a.astype(jnp.int32)
    fill_start = jnp.concatenate([gbase + n_e, gend[-1:]])
    fill_n = jnp.concatenate([reg - n_e, _gmm_blocks(T) * GMM_ROWS - gend[-1:]]) // CHUNK
    return dict(src=i32(off16).reshape(-1), dst=i32(dst).reshape(-1), nch=i32(nch).reshape(-1),
                ntot=i32(jnp.sum(nch, axis=1)), fill_start=i32(fill_start), fill_n=i32(fill_n),
                fill_tot=i32(jnp.sum(fill_n)).reshape(1), blk_expert=blk_expert, n_used=n_used)


def _chunk_copy(src_ref, src_row, dst_ref, dst_row, sem):
    return pltpu.make_async_copy(
        src_ref.at[pl.ds(pl.multiple_of(src_row, CHUNK), CHUNK), :],
        dst_ref.at[pl.ds(pl.multiple_of(dst_row, CHUNK), CHUNK), :], sem)


def _wait_chunks(n, src_ref, dst_ref, sem):
    def body(c, carry):
        _chunk_copy(src_ref, 0, dst_ref, 0, sem).wait()
        return carry

    lax.fori_loop(0, n, body, 0)


def _dispatch_kernel(src_ref, dst_ref, nch_ref, ntot_ref, fstart_ref, fn_ref, ftot_ref,
                     lp_ref, x_ref, xs_hbm, xt, zbuf, sems):
    t = pl.program_id(0)
    nt = pl.num_programs(0)
    slot = t % 2
    xt_s = xt.at[slot]
    sem = sems.at[slot]

    @pl.when(t >= 2)
    def _():
        _wait_chunks(ntot_ref[t - 2], xt_s, xs_hbm, sem)

    lp = lp_ref[0].astype(jnp.int32)
    r = lax.broadcasted_iota(jnp.int32, (SORT_ROWS, MOE_TILE), 0)
    hit = jnp.logical_or(lp[0:1, :] == r, lp[1:2, :] == r)
    p = jnp.where(hit, 1.0, 0.0).astype(BF16)
    xt_s[...] = jnp.dot(p, x_ref[...], preferred_element_type=F32).astype(BF16)

    def per_expert(e, carry):
        k = t * N_EXPERTS + e

        def issue(c, carry2):
            _chunk_copy(xt_s, src_ref[k] + c * CHUNK, xs_hbm, dst_ref[k] + c * CHUNK, sem).start()
            return carry2

        return lax.fori_loop(0, nch_ref[k], issue, carry)

    lax.fori_loop(0, N_EXPERTS, per_expert, 0)

    @pl.when(t == nt - 1)
    def _():
        zbuf[...] = jnp.zeros_like(zbuf)

        def per_range(e, carry):
            def issue(c, carry2):
                _chunk_copy(zbuf, 0, xs_hbm, fstart_ref[e] + c * CHUNK, sem).start()
                return carry2

            return lax.fori_loop(0, fn_ref[e], issue, carry)

        lax.fori_loop(0, N_EXPERTS + 1, per_range, 0)
        _wait_chunks(ntot_ref[t - 1], xt.at[1 - slot], xs_hbm, sems.at[1 - slot])
        _wait_chunks(ntot_ref[t] + ftot_ref[0], xt_s, xs_hbm, sem)


def _dispatch(plan, lpt, x2b):
    T, D = x2b.shape
    nt = T // MOE_TILE
    assert nt >= 2
    grid_spec = pltpu.PrefetchScalarGridSpec(
        num_scalar_prefetch=7,
        grid=(nt,),
        in_specs=[pl.BlockSpec((1, SUBLANES, MOE_TILE), lambda t, *_: (t, 0, 0)),
                  pl.BlockSpec((MOE_TILE, D), lambda t, *_: (t, 0))],
        out_specs=pl.BlockSpec(memory_space=pl.ANY),
        scratch_shapes=[pltpu.VMEM((2, SORT_ROWS, D), BF16), pltpu.VMEM((CHUNK, D), BF16),
                        pltpu.SemaphoreType.DMA((2,))],
    )
    return pl.pallas_call(
        _dispatch_kernel,
        grid_spec=grid_spec,
        out_shape=jax.ShapeDtypeStruct((_gmm_blocks(T) * GMM_ROWS, D), BF16),
        compiler_params=pltpu.CompilerParams(
            dimension_semantics=("arbitrary",), vmem_limit_bytes=VMEM_LIMIT),
        name="moe_dispatch",
    )(plan["src"], plan["dst"], plan["nch"], plan["ntot"], plan["fill_start"], plan["fill_n"],
      plan["fill_tot"], lpt, x2b)


def _gmm_kernel(be_ref, nu_ref, x_ref, wg_ref, wu_ref, wd_ref, o_ref, wg_b, wu_b, wd_b):
    b = pl.program_id(0)

    @pl.when(jnp.logical_or(b == 0, be_ref[b] != be_ref[jnp.maximum(b - 1, 0)]))
    def _():
        _cast_bf16(wg_b, wg_ref.at[0])
        _cast_bf16(wu_b, wu_ref.at[0])
        _cast_bf16(wd_b, wd_ref.at[0])

    @pl.when(b < nu_ref[0])
    def _():
        xb = x_ref[...]
        g = jnp.dot(xb, wg_b[...], preferred_element_type=F32)
        u = jnp.dot(xb, wu_b[...], preferred_element_type=F32)
        h = (g * jax.nn.sigmoid(g)) * u
        o_ref[...] = jnp.dot(h.astype(BF16), wd_b[...], preferred_element_type=F32).astype(BF16)

    @pl.when(pl.program_id(0) >= nu_ref[0])
    def _():
        o_ref[...] = jnp.zeros_like(o_ref)


def _gmm(plan, xs, wg, wu, wd):
    R, D = xs.shape
    rows = pl.BlockSpec((GMM_ROWS, D), lambda b, be, nu: (jnp.minimum(b, nu[0] - 1), 0))
    grid_spec = pltpu.PrefetchScalarGridSpec(
        num_scalar_prefetch=2,
        grid=(R // GMM_ROWS,),
        in_specs=[rows,
                  pl.BlockSpec((1, D, D_EXPERT), lambda b, be, nu: (be[b], 0, 0)),
                  pl.BlockSpec((1, D, D_EXPERT), lambda b, be, nu: (be[b], 0, 0)),
                  pl.BlockSpec((1, D_EXPERT, D), lambda b, be, nu: (be[b], 0, 0))],
        out_specs=pl.BlockSpec((GMM_ROWS, D), lambda b, be, nu: (b, 0)),
        scratch_shapes=[pltpu.VMEM((D, D_EXPERT), BF16), pltpu.VMEM((D, D_EXPERT), BF16),
                        pltpu.VMEM((D_EXPERT, D), BF16)],
    )
    return pl.pallas_call(
        _gmm_kernel,
        grid_spec=grid_spec,
        out_shape=jax.ShapeDtypeStruct((R, D), BF16),
        compiler_params=pltpu.CompilerParams(
            dimension_semantics=("arbitrary",), vmem_limit_bytes=VMEM_LIMIT),
        name="moe_gmm",
    )(plan["blk_expert"], plan["n_used"], xs, wg, wu, wd)


def _combine_kernel(src_ref, dst_ref, nch_ref, ntot_ref, cm_ref, x_ref, ys_hbm, g3_ref, b3_ref, o_ref,
                    yt, sems):
    t = pl.program_id(0)
    nt = pl.num_programs(0)
    slot = t % 2

    def fetch(tile, s):
        def per_expert(e, carry):
            k = tile * N_EXPERTS + e

            def issue(c, carry2):
                _chunk_copy(ys_hbm, dst_ref[k] + c * CHUNK, yt.at[s], src_ref[k] + c * CHUNK,
                            sems.at[s]).start()
                return carry2

            return lax.fori_loop(0, nch_ref[k], issue, carry)

        lax.fori_loop(0, N_EXPERTS, per_expert, 0)

    @pl.when(t == 0)
    def _():
        yt[...] = jnp.zeros_like(yt)
        fetch(0, 0)

    @pl.when(t + 1 < nt)
    def _():
        fetch(t + 1, 1 - slot)

    cm = cm_ref[...]
    col = lax.broadcasted_iota(jnp.int32, (MOE_TILE, SORT_ROWS), 1)
    lp0 = cm[:, 0:1].astype(jnp.int32)
    lp1 = cm[:, 1:2].astype(jnp.int32)
    w = jnp.where(col == lp0, cm[:, 2:3], 0.0) + jnp.where(col == lp1, cm[:, 3:4], 0.0)
    w = w.astype(BF16)

    _wait_chunks(ntot_ref[t], ys_hbm, yt.at[slot], sems.at[slot])
    y = jnp.dot(w, yt[slot], preferred_element_type=F32)
    o_ref[...] = _layer_norm(ALPHA * x_ref[...] + y, g3_ref[...], b3_ref[...])


def _combine(plan, route, x2, ys, g3, b3):
    T, D = x2.shape
    nt = T // MOE_TILE
    grid_spec = pltpu.PrefetchScalarGridSpec(
        num_scalar_prefetch=4,
        grid=(nt,),
        in_specs=[pl.BlockSpec((MOE_TILE, LANES), lambda t, *_: (t, 0)),
                  pl.BlockSpec((MOE_TILE, D), lambda t, *_: (t, 0)),
                  pl.BlockSpec(memory_space=pl.ANY),
                  pl.BlockSpec((1, D), lambda t, *_: (0, 0)),
                  pl.BlockSpec((1, D), lambda t, *_: (0, 0))],
        out_specs=pl.BlockSpec((MOE_TILE, D), lambda t, *_: (t, 0)),
        scratch_shapes=[pltpu.VMEM((2, SORT_ROWS, D), BF16), pltpu.SemaphoreType.DMA((2,))],
    )
    return pl.pallas_call(
        _combine_kernel,
        grid_spec=grid_spec,
        out_shape=jax.ShapeDtypeStruct((T, D), F32),
        compiler_params=pltpu.CompilerParams(
            dimension_semantics=("arbitrary",), vmem_limit_bytes=VMEM_LIMIT),
        name="moe_combine",
    )(plan["src"], plan["dst"], plan["nch"], plan["ntot"], route, x2, ys, g3, b3)


def _moe(x2, x2b, route, lpt, meta, wg, wu, wd, g3, b3):
    plan = _plan(meta, x2.shape[0])
    xs = _dispatch(plan, lpt, x2b)
    ys = _gmm(plan, xs, wg, wu, wd)
    return _combine(plan, route, x2, ys, g3, b3)


def _row(v):
    return v.reshape(1, -1).astype(F32)


def kernel(x, mem, w_in, b_in, w_dw, b_dw, g_conv_norm, b_conv_norm, attn_sinks, w_out, g_ln1, b_ln1,
           w_mq, w_mkv, w_mo, g_ln2, b_ln2, w_group, b_group, w_router, b_router, w_gate, w_up, w_down,
           g_ln3, b_ln3):
    B, S, D = x.shape
    for l in range(DEPTH):
        w_dw_p = jnp.zeros((CONV_HALO, CONV_CH), F32).at[:CONV_WIDTH].set(w_dw[l])
        x1 = _mixer(x, attn_sinks[l].astype(F32), w_in[l], _row(b_in[l]), w_dw_p,
                    _row(b_dw[l]), _row(g_conv_norm[l]), _row(b_conv_norm[l]),
                    w_out[l], _row(g_ln1[l]), _row(b_ln1[l]))

        kvm = _memkv(mem.reshape(B * MEM_LEN, D), w_mkv[l]).reshape(B, MEM_LEN, 2 * D)

        wr = jnp.concatenate(
            [w_group[l], jnp.transpose(w_router[l], (1, 0, 2)).reshape(D, N_EXPERTS)], axis=1)
        wr = jnp.pad(wr, ((0, 0), (0, LANES - wr.shape[1])))
        br = jnp.pad(jnp.concatenate([b_group[l], b_router[l].reshape(-1)]), (0, LANES - N_GROUPS - N_EXPERTS))
        x2, x2b, route, lpt, meta = _memattn(x1, w_mq[l], kvm, w_mo[l],
                                 _row(g_ln2[l]), _row(b_ln2[l]), wr.astype(F32), _row(br))

        T = B * S
        y = _moe(x2.reshape(T, D), x2b.reshape(T, D), route.reshape(T, LANES), lpt, meta,
                 w_gate[l], w_up[l], w_down[l],
                 _row(g_ln3[l]), _row(b_ln3[l]))
        x = y.reshape(B, S, D)
    return x
```

```python
import functools

import jax
import jax.numpy as jnp
from jax import lax
from jax.experimental import pallas as pl
from jax.experimental.pallas import tpu as pltpu

D_MODEL = 1024
MEM_LEN = 256
CONV_CH = 512
CONV_WIDTH = 31
N_HEADS = 8
N_KV_HEADS = 2
HEAD_DIM = 64
GQ = N_HEADS // N_KV_HEADS
ATTN_W = N_HEADS * HEAD_DIM
KV_W = N_KV_HEADS * HEAD_DIM
WINDOW = 128
D_MIX = CONV_CH + ATTN_W
D_IN = 2 * CONV_CH + ATTN_W + 2 * KV_W
MEM_HEADS = 4
MEM_HEAD_DIM = D_MODEL // MEM_HEADS
N_GROUPS = 4
EXPERTS_PER_GROUP = 4
N_EXPERTS = N_GROUPS * EXPERTS_PER_GROUP
D_EXPERT = D_MODEL // 2
DEPTH = 1
ALPHA = (2.0 * DEPTH) ** 0.25
LN_EPS = 1e-5

LANES = 128
SUBLANES = 8
CONV_ROWS = 128
LN_ROWS = 64
MASK_VALUE = -1e30
CONV_HALO = 32
SEQ_TILE = 512
MOE_TILE = 512
CHUNK = 16
GMM_ROWS = 512
TOP_K = 2
SORT_ROWS = -(-(MOE_TILE * TOP_K + N_EXPERTS * (CHUNK - 1)) // 256) * 256
ROUTE_OFF = N_GROUPS
VMEM_LIMIT = 56 * 1024 * 1024

BF16 = jnp.bfloat16
F32 = jnp.float32


def _layer_norm(x, g, b):
    mu = jnp.mean(x, axis=-1, keepdims=True)
    xc = x - mu
    var = jnp.mean(xc * xc, axis=-1, keepdims=True)
    return xc * lax.rsqrt(var + LN_EPS) * g + b


def _cast_bf16(dst_ref, src_ref):
    rows = 256
    for r0 in range(0, src_ref.shape[0], rows):
        dst_ref[r0:r0 + rows, :] = src_ref[r0:r0 + rows, :].astype(BF16)


def _dot_nt(a, b):
    return lax.dot_general(a, b, (((1,), (1,)), ((), ())), preferred_element_type=F32)


def _mixer_kernel(sinks_ref, x_ref, w_in_ref, b_in_ref, w_dw_ref, b_dw_ref, g_cn_ref, b_cn_ref,
                  w_out_ref, g1_ref, b1_ref, o_ref, w_in_b, w_out_b, hbuf, hshift, cbuf, qbuf, kbuf, vbuf, ymix):
    i = pl.program_id(1)
    ts = SEQ_TILE

    @pl.when(jnp.logical_and(pl.program_id(0) == 0, i == 0))
    def _():
        _cast_bf16(w_in_b, w_in_ref)
        _cast_bf16(w_out_b, w_out_ref)

    @pl.when(i == 0)
    def _():
        hbuf[0:CONV_HALO, :] = jnp.zeros((CONV_HALO, CONV_CH), F32)
        kbuf[:, 0:WINDOW, :] = jnp.zeros((2 * N_KV_HEADS, WINDOW, KV_W), BF16)
        vbuf[:, 0:WINDOW, :] = jnp.zeros((2, WINDOW, KV_W), BF16)

    x = x_ref[0]
    u = jnp.dot(x.astype(BF16), w_in_b[...], preferred_element_type=F32) + b_in_ref[...]
    a = u[:, 0:CONV_CH]
    gate = u[:, CONV_CH:2 * CONV_CH]
    hbuf[CONV_HALO:CONV_HALO + ts, :] = a * jax.nn.sigmoid(gate)
    qbuf[...] = (u[:, 2 * CONV_CH:2 * CONV_CH + ATTN_W] * (HEAD_DIM ** -0.5)).astype(BF16)
    kf = u[:, 2 * CONV_CH + ATTN_W:2 * CONV_CH + ATTN_W + KV_W]
    vf = u[:, 2 * CONV_CH + ATTN_W + KV_W:D_IN]
    kr = pltpu.roll(kf, HEAD_DIM, axis=1)
    vr = pltpu.roll(vf, HEAD_DIM, axis=1)
    lo = lax.broadcasted_iota(jnp.int32, (ts, KV_W), 1) < HEAD_DIM
    rows = slice(WINDOW, WINDOW + ts)
    kbuf[0, rows, :] = jnp.where(lo, kf, 0.0).astype(BF16)
    kbuf[1, rows, :] = jnp.where(lo, 0.0, kr).astype(BF16)
    kbuf[2, rows, :] = jnp.where(lo, kr, 0.0).astype(BF16)
    kbuf[3, rows, :] = jnp.where(lo, 0.0, kf).astype(BF16)
    vbuf[0, rows, :] = vf.astype(BF16)
    vbuf[1, rows, :] = vr.astype(BF16)

    base = CONV_HALO - (CONV_WIDTH - 1)
    n_shift = ts + CONV_HALO - SUBLANES
    for b in range(1, SUBLANES):
        hshift[b - 1, 0:n_shift, :] = hbuf[b:b + n_shift, :]
    rc = CONV_ROWS

    def conv_chunk(c):
        r0 = c * rc
        for l in range(CONV_CH // LANES):
            ls = slice(l * LANES, (l + 1) * LANES)
            acc = jnp.zeros((rc, LANES), F32)
            for j in range(CONV_WIDTH):
                a8, b = divmod(j + base, SUBLANES)
                rs = slice(r0 + SUBLANES * a8, r0 + SUBLANES * a8 + rc)
                tap = hbuf[rs, ls] if b == 0 else hshift[b - 1, rs, ls]
                acc = acc + tap * w_dw_ref[j:j + 1, ls]
            cbuf[r0:r0 + rc, ls] = acc
        for r1 in range(r0, r0 + rc, LN_ROWS):
            rs = slice(r1, r1 + LN_ROWS)
            y = _layer_norm(cbuf[rs, :] + b_dw_ref[...], g_cn_ref[...], b_cn_ref[...])
            y = y * jax.nn.sigmoid(y)
            ymix[rs, 0:CONV_CH] = y.astype(BF16)

    qi = lax.broadcasted_iota(jnp.int32, (2 * WINDOW, 2 * WINDOW), 0) % WINDOW
    kj = lax.broadcasted_iota(jnp.int32, (2 * WINDOW, 2 * WINDOW), 1)
    dist = qi + WINDOW - kj
    band = (dist >= 0) & (dist < WINDOW)
    top = lax.broadcasted_iota(jnp.int32, (2 * WINDOW, 1), 0) < WINDOW
    lo_out = lax.broadcasted_iota(jnp.int32, (WINDOW, 2 * HEAD_DIM), 1) < HEAD_DIM

    def attn_block(jb):
        r0 = jb * WINDOW
        valid = band & jnp.logical_or(i != 0, kj >= WINDOW) if jb == 0 else band
        for kvh in range(N_KV_HEADS):
            h0 = kvh * GQ
            c0 = h0 * HEAD_DIM
            qs = jnp.concatenate([qbuf[r0:r0 + WINDOW, c0:c0 + 2 * HEAD_DIM],
                                  qbuf[r0:r0 + WINDOW, c0 + 2 * HEAD_DIM:c0 + 4 * HEAD_DIM]], axis=0)
            pv = []
            for par in range(2):
                kk = kbuf[2 * kvh + par, r0:r0 + 2 * WINDOW, :]
                vv = vbuf[(kvh + par) % 2, r0:r0 + 2 * WINDOW, :]
                s = jnp.where(valid, _dot_nt(qs, kk), MASK_VALUE)
                sink = jnp.where(top, sinks_ref[h0 + par], sinks_ref[h0 + 2 + par])
                m = jnp.maximum(jnp.max(s, axis=-1, keepdims=True), sink)
                p = jnp.exp(s - m)
                denom = jnp.sum(p, axis=-1, keepdims=True) + jnp.exp(sink - m)
                pv.append(jnp.dot(p.astype(BF16), vv, preferred_element_type=F32) / denom)
            for pair in range(2):
                rs = slice(pair * WINDOW, (pair + 1) * WINDOW)
                o = jnp.where(lo_out, pv[0][rs], pv[1][rs])
                cs = CONV_CH + c0 + pair * 2 * HEAD_DIM
                ymix[r0:r0 + WINDOW, cs:cs + 2 * HEAD_DIM] = o.astype(BF16)

    assert ts // rc == ts // WINDOW
    for c in range(ts // rc):
        attn_block(c)
        conv_chunk(c)

    mix = jnp.dot(ymix[...], w_out_b[...], preferred_element_type=F32)
    o_ref[0] = _layer_norm(ALPHA * x + mix, g1_ref[...], b1_ref[...])

    hbuf[0:CONV_HALO, :] = hbuf[ts:ts + CONV_HALO, :]
    kbuf[:, 0:WINDOW, :] = kbuf[:, ts:ts + WINDOW, :]
    vbuf[:, 0:WINDOW, :] = vbuf[:, ts:ts + WINDOW, :]


def _const_spec(shape):
    nd = len(shape)
    return pl.BlockSpec(shape, lambda *_: (0,) * nd)


def _resident_spec(shape):
    nd = len(shape)
    return pl.BlockSpec(shape, lambda *_: (0,) * nd, pipeline_mode=pl.Buffered(1))


def _mixer(x, sinks, w_in, b_in, w_dw, b_dw, g_cn, b_cn, w_out, g1, b1):
    B, S, D = x.shape
    ts = SEQ_TILE
    tile = pl.BlockSpec((1, ts, D), lambda b, i: (b, i, 0))
    return pl.pallas_call(
        _mixer_kernel,
        grid=(B, S // ts),
        in_specs=[
            pl.BlockSpec(memory_space=pltpu.SMEM),
            tile,
            _resident_spec((D, D_IN)), _const_spec((1, D_IN)),
            _const_spec((CONV_HALO, CONV_CH)), _const_spec((1, CONV_CH)),
            _const_spec((1, CONV_CH)), _const_spec((1, CONV_CH)),
            _resident_spec((D_MIX, D)), _const_spec((1, D)), _const_spec((1, D)),
        ],
        out_specs=tile,
        out_shape=jax.ShapeDtypeStruct((B, S, D), F32),
        scratch_shapes=[
            pltpu.VMEM((D, D_IN), BF16),
            pltpu.VMEM((D_MIX, D), BF16),
            pltpu.VMEM((CONV_HALO + ts, CONV_CH), F32),
            pltpu.VMEM((SUBLANES - 1, CONV_HALO + ts, CONV_CH), F32),
            pltpu.VMEM((ts, CONV_CH), F32),
            pltpu.VMEM((ts, ATTN_W), BF16),
            pltpu.VMEM((2 * N_KV_HEADS, WINDOW + ts, KV_W), BF16),
            pltpu.VMEM((2, WINDOW + ts, KV_W), BF16),
            pltpu.VMEM((ts, D_MIX), BF16),
        ],
        compiler_params=pltpu.CompilerParams(
            dimension_semantics=("arbitrary", "arbitrary"), vmem_limit_bytes=VMEM_LIMIT),
        name="mixer",
    )(sinks, x, w_in, b_in, w_dw, b_dw, g_cn, b_cn, w_out, g1, b1)


def _memkv_kernel(mem_ref, w_ref, o_ref):
    o_ref[...] = jnp.dot(mem_ref[...].astype(BF16), w_ref[...].astype(BF16),
                         preferred_element_type=F32).astype(BF16)


def _memkv(mem2d, w_mkv):
    M, D = mem2d.shape
    N = w_mkv.shape[1]
    tn = 512
    return pl.pallas_call(
        _memkv_kernel,
        grid=(N // tn,),
        in_specs=[pl.BlockSpec((M, D), lambda j: (0, 0)), pl.BlockSpec((D, tn), lambda j: (0, j))],
        out_specs=pl.BlockSpec((M, tn), lambda j: (0, j)),
        out_shape=jax.ShapeDtypeStruct((M, N), BF16),
        compiler_params=pltpu.CompilerParams(dimension_semantics=("arbitrary",)),
        name="memkv",
    )(mem2d, w_mkv)


def _route_plan(logits):
    tile = logits.shape[0]
    lane = lax.broadcasted_iota(jnp.int32, logits.shape, 1)
    big = jnp.int32(LANES)
    gl = jnp.where(lane < N_GROUPS, logits, MASK_VALUE)
    gmax = jnp.max(gl, axis=-1, keepdims=True)
    g_p = 1.0 / jnp.sum(jnp.exp(gl - gmax), axis=-1, keepdims=True)
    g_idx = jnp.min(jnp.where(gl == gmax, lane, big), axis=-1, keepdims=True)
    lo = ROUTE_OFF + g_idx * EXPERTS_PER_GROUP
    sel = (lane >= lo) & (lane < lo + EXPERTS_PER_GROUP)
    rl = jnp.where(sel, logits, MASK_VALUE)
    m1 = jnp.max(rl, axis=-1, keepdims=True)
    i1 = jnp.min(jnp.where(rl == m1, lane, big), axis=-1, keepdims=True)
    rl2 = jnp.where(lane == i1, MASK_VALUE, rl)
    m2 = jnp.max(rl2, axis=-1, keepdims=True)
    i2 = jnp.min(jnp.where(rl2 == m2, lane, big), axis=-1, keepdims=True)
    ex = jnp.exp(m2 - m1)
    w1 = 1.0 / (1.0 + ex)
    w2 = ex * w1

    hit1 = lane == i1 - ROUTE_OFF
    hit2 = lane == i2 - ROUTE_OFF
    oh = jnp.where(jnp.logical_or(hit1, hit2), 1.0, 0.0)
    r = lax.broadcasted_iota(jnp.int32, (tile, tile), 0)
    c = lax.broadcasted_iota(jnp.int32, (tile, tile), 1)
    tri = jnp.where(c <= r, 1.0, 0.0).astype(BF16)
    csum = jnp.dot(tri, oh.astype(BF16), preferred_element_type=F32)
    counts = csum[tile - 1:tile, :].astype(jnp.int32)
    nch = jnp.right_shift(counts + (CHUNK - 1), CHUNK.bit_length() - 1)
    er = lax.broadcasted_iota(jnp.int32, (LANES, LANES), 0)
    ec = lax.broadcasted_iota(jnp.int32, (LANES, LANES), 1)
    upper = jnp.where(er < ec, 1.0, 0.0).astype(BF16)
    nch8 = jnp.broadcast_to(nch.astype(F32), (SUBLANES, LANES)).astype(BF16)
    off = jnp.dot(nch8, upper, preferred_element_type=F32)[0:1, :] * CHUNK
    pos = off + csum - oh
    lp1 = jnp.sum(jnp.where(hit1, pos, 0.0), axis=-1, keepdims=True)
    lp2 = jnp.sum(jnp.where(hit2, pos, 0.0), axis=-1, keepdims=True)
    route = jnp.where(lane == 0, lp1, jnp.where(lane == 1, lp2,
                      jnp.where(lane == 2, g_p * w1, jnp.where(lane == 3, g_p * w2, 0.0))))
    row = lax.broadcasted_iota(jnp.int32, (SUBLANES, LANES), 0)
    meta = jnp.where(row == 0, nch, jnp.where(row == 1, off.astype(jnp.int32), 0))
    return route, meta


def _memattn_kernel(x_ref, wq_ref, k_ref, v_ref, wo_ref, g2_ref, b2_ref, wr2_ref, wrh_ref, br_ref,
                    o_ref, ob_ref, route_ref, routet_ref, meta_ref, wq_b, wo_b):
    @pl.when(jnp.logical_and(pl.program_id(0) == 0, pl.program_id(1) == 0))
    def _():
        _cast_bf16(wq_b, wq_ref)
        _cast_bf16(wo_b, wo_ref)

    x = x_ref[0]
    q = jnp.dot(x.astype(BF16), wq_b[...], preferred_element_type=F32)
    q = (q * (MEM_HEAD_DIM ** -0.5)).astype(BF16)
    outs = []
    for h in range(MEM_HEADS):
        sl = slice(h * MEM_HEAD_DIM, (h + 1) * MEM_HEAD_DIM)
        s = _dot_nt(q[:, sl], k_ref[0, :, sl])
        m = jnp.max(s, axis=-1, keepdims=True)
        p = jnp.exp(s - m)
        denom = jnp.sum(p, axis=-1, keepdims=True)
        o = jnp.dot(p.astype(BF16), v_ref[0, :, sl], preferred_element_type=F32)
        outs.append((o / denom).astype(BF16))
    o = jnp.dot(jnp.concatenate(outs, axis=-1), wo_b[...], preferred_element_type=F32)
    x2 = _layer_norm(ALPHA * x + o, g2_ref[...], b2_ref[...])
    o_ref[0] = x2
    x2h = x2.astype(BF16)
    ob_ref[0] = x2h
    x2l = (x2 - x2h.astype(F32)).astype(BF16)
    hh = jnp.dot(x2h, wr2_ref[...], preferred_element_type=F32)
    logits = (hh[:, 0:LANES] + hh[:, LANES:2 * LANES]
              + jnp.dot(x2l, wrh_ref[...], preferred_element_type=F32) + br_ref[...])
    route, meta = _route_plan(logits)
    route_ref[0] = route
    routet_ref[0] = jnp.transpose(route)[0:SUBLANES, :]
    meta_ref[0] = meta


def _memattn(x1, wq, kvm, wo, g2, b2, wr, br):
    B, S, D = x1.shape
    ts = SEQ_TILE
    assert ts == MOE_TILE
    nt = S // ts
    c = wr * (2.0 ** 16 + 1.0)
    w_high = c - (c - wr)
    wrh = w_high.astype(BF16)
    wr2 = jnp.concatenate([wrh, (wr - w_high).astype(BF16)], axis=1)
    tile = pl.BlockSpec((1, ts, D), lambda b, i: (b, i, 0))
    kspec = pl.BlockSpec((1, MEM_LEN, D), lambda b, i: (b, 0, 0))
    vspec = pl.BlockSpec((1, MEM_LEN, D), lambda b, i: (b, 0, 1))
    return pl.pallas_call(
        _memattn_kernel,
        grid=(B, nt),
        in_specs=[tile, _resident_spec((D, D)), kspec, vspec, _resident_spec((D, D)),
                  _const_spec((1, D)), _const_spec((1, D)),
                  _const_spec((D, 2 * LANES)), _const_spec((D, LANES)), _const_spec((1, LANES))],
        out_specs=[tile, tile, pl.BlockSpec((1, ts, LANES), lambda b, i: (b, i, 0)),
                   pl.BlockSpec((1, SUBLANES, ts), lambda b, i: (b * nt + i, 0, 0)),
                   pl.BlockSpec((1, SUBLANES, LANES), lambda b, i: (b * nt + i, 0, 0))],
        out_shape=[jax.ShapeDtypeStruct((B, S, D), F32),
                   jax.ShapeDtypeStruct((B, S, D), BF16),
                   jax.ShapeDtypeStruct((B, S, LANES), F32),
                   jax.ShapeDtypeStruct((B * nt, SUBLANES, ts), F32),
                   jax.ShapeDtypeStruct((B * nt, SUBLANES, LANES), jnp.int32)],
        scratch_shapes=[pltpu.VMEM((D, D), BF16), pltpu.VMEM((D, D), BF16)],
        compiler_params=pltpu.CompilerParams(
            dimension_semantics=("arbitrary", "arbitrary"), vmem_limit_bytes=VMEM_LIMIT),
        name="memattn",
    )(x1, wq, kvm, kvm, wo, g2, b2, wr2, wrh, br)


def _gmm_blocks(n_tokens):
    rows = (n_tokens * TOP_K + (n_tokens // MOE_TILE) * N_EXPERTS * (CHUNK - 1)
            + N_EXPERTS * (GMM_ROWS - CHUNK))
    return -(-rows // GMM_ROWS)


def _plan(meta, T):
    nch = meta[:, 0, :N_EXPERTS]
    off16 = meta[:, 1, :N_EXPERTS]
    n16 = nch * CHUNK
    n_e = jnp.sum(n16, axis=0)
    reg = (n_e + GMM_ROWS - 1) // GMM_ROWS * GMM_ROWS
    gend = jnp.cumsum(reg)
    gbase = gend - reg
    dst = gbase[None, :] + jnp.cumsum(n16, axis=0) - n16
    blk_row = jnp.arange(_gmm_blocks(T), dtype=jnp.int32)[:, None] * GMM_ROWS
    blk_expert = jnp.sum(blk_row >= gend[None, :], axis=1)
    blk_expert = jnp.minimum(blk_expert, N_EXPERTS - 1).astype(jnp.int32)
    n_used = (gend[-1] // GMM_ROWS).astype(jnp.int32).reshape(1)
    i32 = lambda a: a.astype(jnp.int32)
    fill_start = jnp.concatenate([gbase + n_e, gend[-1:]])
    fill_n = jnp.concatenate([reg - n_e, _gmm_blocks(T) * GMM_ROWS - gend[-1:]]) // CHUNK
    return dict(src=i32(off16).reshape(-1), dst=i32(dst).reshape(-1), nch=i32(nch).reshape(-1),
                ntot=i32(jnp.sum(nch, axis=1)), fill_start=i32(fill_start), fill_n=i32(fill_n),
                fill_tot=i32(jnp.sum(fill_n)).reshape(1), blk_expert=blk_expert, n_used=n_used)


def _chunk_copy(src_ref, src_row, dst_ref, dst_row, sem):
    return pltpu.make_async_copy(
        src_ref.at[pl.ds(pl.multiple_of(src_row, CHUNK), CHUNK), :],
        dst_ref.at[pl.ds(pl.multiple_of(dst_row, CHUNK), CHUNK), :], sem)


def _wait_chunks(n, src_ref, dst_ref, sem):
    def body(c, carry):
        _chunk_copy(src_ref, 0, dst_ref, 0, sem).wait()
        return carry

    lax.fori_loop(0, n, body, 0)


def _dispatch_kernel(src_ref, dst_ref, nch_ref, ntot_ref, fstart_ref, fn_ref, ftot_ref,
                     lp_ref, x_ref, xs_hbm, xt, zbuf, sems):
    t = pl.program_id(0)
    nt = pl.num_programs(0)
    slot = t % 2
    xt_s = xt.at[slot]
    sem = sems.at[slot]

    @pl.when(t >= 2)
    def _():
        _wait_chunks(ntot_ref[t - 2], xt_s, xs_hbm, sem)

    lp = lp_ref[0].astype(jnp.int32)
    r = lax.broadcasted_iota(jnp.int32, (SORT_ROWS, MOE_TILE), 0)
    hit = jnp.logical_or(lp[0:1, :] == r, lp[1:2, :] == r)
    p = jnp.where(hit, 1.0, 0.0).astype(BF16)
    xt_s[...] = jnp.dot(p, x_ref[...], preferred_element_type=F32).astype(BF16)

    def per_expert(e, carry):
        k = t * N_EXPERTS + e

        def issue(c, carry2):
            _chunk_copy(xt_s, src_ref[k] + c * CHUNK, xs_hbm, dst_ref[k] + c * CHUNK, sem).start()
            return carry2

        return lax.fori_loop(0, nch_ref[k], issue, carry)

    lax.fori_loop(0, N_EXPERTS, per_expert, 0)

    @pl.when(t == nt - 1)
    def _():
        zbuf[...] = jnp.zeros_like(zbuf)

        def per_range(e, carry):
            def issue(c, carry2):
                _chunk_copy(zbuf, 0, xs_hbm, fstart_ref[e] + c * CHUNK, sem).start()
                return carry2

            return lax.fori_loop(0, fn_ref[e], issue, carry)

        lax.fori_loop(0, N_EXPERTS + 1, per_range, 0)
        _wait_chunks(ntot_ref[t - 1], xt.at[1 - slot], xs_hbm, sems.at[1 - slot])
        _wait_chunks(ntot_ref[t] + ftot_ref[0], xt_s, xs_hbm, sem)


def _dispatch(plan, lpt, x2b):
    T, D = x2b.shape
    nt = T // MOE_TILE
    assert nt >= 2
    grid_spec = pltpu.PrefetchScalarGridSpec(
        num_scalar_prefetch=7,
        grid=(nt,),
        in_specs=[pl.BlockSpec((1, SUBLANES, MOE_TILE), lambda t, *_: (t, 0, 0)),
                  pl.BlockSpec((MOE_TILE, D), lambda t, *_: (t, 0))],
        out_specs=pl.BlockSpec(memory_space=pl.ANY),
        scratch_shapes=[pltpu.VMEM((2, SORT_ROWS, D), BF16), pltpu.VMEM((CHUNK, D), BF16),
                        pltpu.SemaphoreType.DMA((2,))],
    )
    return pl.pallas_call(
        _dispatch_kernel,
        grid_spec=grid_spec,
        out_shape=jax.ShapeDtypeStruct((_gmm_blocks(T) * GMM_ROWS, D), BF16),
        compiler_params=pltpu.CompilerParams(
            dimension_semantics=("arbitrary",), vmem_limit_bytes=VMEM_LIMIT),
        name="moe_dispatch",
    )(plan["src"], plan["dst"], plan["nch"], plan["ntot"], plan["fill_start"], plan["fill_n"],
      plan["fill_tot"], lpt, x2b)


def _gmm_kernel(be_ref, nu_ref, x_ref, wg_ref, wu_ref, wd_ref, o_ref, wg_b, wu_b, wd_b):
    b = pl.program_id(0)

    @pl.when(jnp.logical_or(b == 0, be_ref[b] != be_ref[jnp.maximum(b - 1, 0)]))
    def _():
        _cast_bf16(wg_b, wg_ref.at[0])
        _cast_bf16(wu_b, wu_ref.at[0])
        _cast_bf16(wd_b, wd_ref.at[0])

    @pl.when(b < nu_ref[0])
    def _():
        xb = x_ref[...]
        g = jnp.dot(xb, wg_b[...], preferred_element_type=F32)
        u = jnp.dot(xb, wu_b[...], preferred_element_type=F32)
        h = (g * jax.nn.sigmoid(g)) * u
        o_ref[...] = jnp.dot(h.astype(BF16), wd_b[...], preferred_element_type=F32).astype(BF16)

    @pl.when(pl.program_id(0) >= nu_ref[0])
    def _():
        o_ref[...] = jnp.zeros_like(o_ref)


def _gmm(plan, xs, wg, wu, wd):
    R, D = xs.shape
    rows = pl.BlockSpec((GMM_ROWS, D), lambda b, be, nu: (jnp.minimum(b, nu[0] - 1), 0))
    grid_spec = pltpu.PrefetchScalarGridSpec(
        num_scalar_prefetch=2,
        grid=(R // GMM_ROWS,),
        in_specs=[rows,
                  pl.BlockSpec((1, D, D_EXPERT), lambda b, be, nu: (be[b], 0, 0)),
                  pl.BlockSpec((1, D, D_EXPERT), lambda b, be, nu: (be[b], 0, 0)),
                  pl.BlockSpec((1, D_EXPERT, D), lambda b, be, nu: (be[b], 0, 0))],
        out_specs=pl.BlockSpec((GMM_ROWS, D), lambda b, be, nu: (b, 0)),
        scratch_shapes=[pltpu.VMEM((D, D_EXPERT), BF16), pltpu.VMEM((D, D_EXPERT), BF16),
                        pltpu.VMEM((D_EXPERT, D), BF16)],
    )
    return pl.pallas_call(
        _gmm_kernel,
        grid_spec=grid_spec,
        out_shape=jax.ShapeDtypeStruct((R, D), BF16),
        compiler_params=pltpu.CompilerParams(
            dimension_semantics=("arbitrary",), vmem_limit_bytes=VMEM_LIMIT),
        name="moe_gmm",
    )(plan["blk_expert"], plan["n_used"], xs, wg, wu, wd)


def _combine_kernel(src_ref, dst_ref, nch_ref, ntot_ref, cm_ref, x_ref, ys_hbm, g3_ref, b3_ref, o_ref,
                    yt, sems):
    t = pl.program_id(0)
    nt = pl.num_programs(0)
    slot = t % 2

    def fetch(tile, s):
        def per_expert(e, carry):
            k = tile * N_EXPERTS + e

            def issue(c, carry2):
                _chunk_copy(ys_hbm, dst_ref[k] + c * CHUNK, yt.at[s], src_ref[k] + c * CHUNK,
                            sems.at[s]).start()
                return carry2

            return lax.fori_loop(0, nch_ref[k], issue, carry)

        lax.fori_loop(0, N_EXPERTS, per_expert, 0)

    @pl.when(t == 0)
    def _():
        yt[...] = jnp.zeros_like(yt)
        fetch(0, 0)

    @pl.when(t + 1 < nt)
    def _():
        fetch(t + 1, 1 - slot)

    cm = cm_ref[...]
    col = lax.broadcasted_iota(jnp.int32, (MOE_TILE, SORT_ROWS), 1)
    lp0 = cm[:, 0:1].astype(jnp.int32)
    lp1 = cm[:, 1:2].astype(jnp.int32)
    w = jnp.where(col == lp0, cm[:, 2:3], 0.0) + jnp.where(col == lp1, cm[:, 3:4], 0.0)
    w = w.astype(BF16)

    _wait_chunks(ntot_ref[t], ys_hbm, yt.at[slot], sems.at[slot])
    y = jnp.dot(w, yt[slot], preferred_element_type=F32)
    o_ref[...] = _layer_norm(ALPHA * x_ref[...] + y, g3_ref[...], b3_ref[...])


def _combine(plan, route, x2, ys, g3, b3):
    T, D = x2.shape
    nt = T // MOE_TILE
    grid_spec = pltpu.PrefetchScalarGridSpec(
        num_scalar_prefetch=4,
        grid=(nt,),
        in_specs=[pl.BlockSpec((MOE_TILE, LANES), lambda t, *_: (t, 0)),
                  pl.BlockSpec((MOE_TILE, D), lambda t, *_: (t, 0)),
                  pl.BlockSpec(memory_space=pl.ANY),
                  pl.BlockSpec((1, D), lambda t, *_: (0, 0)),
                  pl.BlockSpec((1, D), lambda t, *_: (0, 0))],
        out_specs=pl.BlockSpec((MOE_TILE, D), lambda t, *_: (t, 0)),
        scratch_shapes=[pltpu.VMEM((2, SORT_ROWS, D), BF16), pltpu.SemaphoreType.DMA((2,))],
    )
    return pl.pallas_call(
        _combine_kernel,
        grid_spec=grid_spec,
        out_shape=jax.ShapeDtypeStruct((T, D), F32),
        compiler_params=pltpu.CompilerParams(
            dimension_semantics=("arbitrary",), vmem_limit_bytes=VMEM_LIMIT),
        name="moe_combine",
    )(plan["src"], plan["dst"], plan["nch"], plan["ntot"], route, x2, ys, g3, b3)


def _moe(x2, x2b, route, lpt, meta, wg, wu, wd, g3, b3):
    plan = _plan(meta, x2.shape[0])
    xs = _dispatch(plan, lpt, x2b)
    ys = _gmm(plan, xs, wg, wu, wd)
    return _combine(plan, route, x2, ys, g3, b3)


def _row(v):
    return v.reshape(1, -1).astype(F32)


def kernel(x, mem, w_in, b_in, w_dw, b_dw, g_conv_norm, b_conv_norm, attn_sinks, w_out, g_ln1, b_ln1,
           w_mq, w_mkv, w_mo, g_ln2, b_ln2, w_group, b_group, w_router, b_router, w_gate, w_up, w_down,
           g_ln3, b_ln3):
    B, S, D = x.shape
    for l in range(DEPTH):
        w_dw_p = jnp.zeros((CONV_HALO, CONV_CH), F32).at[:CONV_WIDTH].set(w_dw[l])
        x1 = _mixer(x, attn_sinks[l].astype(F32), w_in[l], _row(b_in[l]), w_dw_p,
                    _row(b_dw[l]), _row(g_conv_norm[l]), _row(b_conv_norm[l]),
                    w_out[l], _row(g_ln1[l]), _row(b_ln1[l]))

        kvm = _memkv(mem.reshape(B * MEM_LEN, D), w_mkv[l]).reshape(B, MEM_LEN, 2 * D)

        wr = jnp.concatenate(
            [w_group[l], jnp.transpose(w_router[l], (1, 0, 2)).reshape(D, N_EXPERTS)], axis=1)
        wr = jnp.pad(wr, ((0, 0), (0, LANES - wr.shape[1])))
        br = jnp.pad(jnp.concatenate([b_group[l], b_router[l].reshape(-1)]), (0, LANES - N_GROUPS - N_EXPERTS))
        x2, x2b, route, lpt, meta = _memattn(x1, w_mq[l], kvm, w_mo[l],
                                 _row(g_ln2[l]), _row(b_ln2[l]), wr.astype(F32), _row(br))

        T = B * S
        y = _moe(x2.reshape(T, D), x2b.reshape(T, D), route.reshape(T, LANES), lpt, meta,
                 w_gate[l], w_up[l], w_down[l],
                 _row(g_ln3[l]), _row(b_ln3[l]))
        x = y.reshape(B, S, D)
    return x
```

```python
import functools

import jax
import jax.numpy as jnp
from jax import lax
from jax.experimental import pallas as pl
from jax.experimental.pallas import tpu as pltpu

D_MODEL = 1024
MEM_LEN = 256
CONV_CH = 512
CONV_WIDTH = 31
N_HEADS = 8
N_KV_HEADS = 2
HEAD_DIM = 64
GQ = N_HEADS // N_KV_HEADS
ATTN_W = N_HEADS * HEAD_DIM
KV_W = N_KV_HEADS * HEAD_DIM
WINDOW = 128
D_MIX = CONV_CH + ATTN_W
D_IN = 2 * CONV_CH + ATTN_W + 2 * KV_W
MEM_HEADS = 4
MEM_HEAD_DIM = D_MODEL // MEM_HEADS
N_GROUPS = 4
EXPERTS_PER_GROUP = 4
N_EXPERTS = N_GROUPS * EXPERTS_PER_GROUP
D_EXPERT = D_MODEL // 2
DEPTH = 1
ALPHA = (2.0 * DEPTH) ** 0.25
LN_EPS = 1e-5

LANES = 128
SUBLANES = 8
CONV_ROWS = 128
LN_ROWS = 64
MEM_ROWS = 512
MASK_VALUE = -1e30
CONV_HALO = 32
SEQ_TILE = 512
MOE_TILE = 512
CHUNK = 16
GMM_ROWS = 512
TOP_K = 2
SORT_ROWS = -(-(MOE_TILE * TOP_K + N_EXPERTS * (CHUNK - 1)) // 256) * 256
ROUTE_OFF = N_GROUPS
VMEM_LIMIT = 56 * 1024 * 1024

BF16 = jnp.bfloat16
F32 = jnp.float32


def _layer_norm(x, g, b):
    mu = jnp.mean(x, axis=-1, keepdims=True)
    xc = x - mu
    var = jnp.mean(xc * xc, axis=-1, keepdims=True)
    return xc * lax.rsqrt(var + LN_EPS) * g + b


def _cast_bf16(dst_ref, src_ref):
    rows = 256
    for r0 in range(0, src_ref.shape[0], rows):
        dst_ref[r0:r0 + rows, :] = src_ref[r0:r0 + rows, :].astype(BF16)


def _dot_nt(a, b):
    return lax.dot_general(a, b, (((1,), (1,)), ((), ())), preferred_element_type=F32)


def _mixer_kernel(sinks_ref, x_ref, w_in_ref, b_in_ref, w_dw_ref, b_dw_ref, g_cn_ref, b_cn_ref,
                  w_out_ref, g1_ref, b1_ref, o_ref, w_in_b, w_out_b, hbuf, hshift, cbuf, qbuf, kbuf, vbuf, ymix):
    i = pl.program_id(1)
    ts = SEQ_TILE

    @pl.when(jnp.logical_and(pl.program_id(0) == 0, i == 0))
    def _():
        _cast_bf16(w_in_b, w_in_ref)
        _cast_bf16(w_out_b, w_out_ref)

    @pl.when(i == 0)
    def _():
        hbuf[0:CONV_HALO, :] = jnp.zeros((CONV_HALO, CONV_CH), F32)
        kbuf[:, 0:WINDOW, :] = jnp.zeros((2 * N_KV_HEADS, WINDOW, KV_W), BF16)
        vbuf[:, 0:WINDOW, :] = jnp.zeros((2, WINDOW, KV_W), BF16)

    x = x_ref[0]
    u = jnp.dot(x.astype(BF16), w_in_b[...], preferred_element_type=F32) + b_in_ref[...]
    a = u[:, 0:CONV_CH]
    gate = u[:, CONV_CH:2 * CONV_CH]
    hbuf[CONV_HALO:CONV_HALO + ts, :] = a * jax.nn.sigmoid(gate)
    qbuf[...] = (u[:, 2 * CONV_CH:2 * CONV_CH + ATTN_W] * (HEAD_DIM ** -0.5)).astype(BF16)
    kf = u[:, 2 * CONV_CH + ATTN_W:2 * CONV_CH + ATTN_W + KV_W]
    vf = u[:, 2 * CONV_CH + ATTN_W + KV_W:D_IN]
    kr = pltpu.roll(kf, HEAD_DIM, axis=1)
    vr = pltpu.roll(vf, HEAD_DIM, axis=1)
    lo = lax.broadcasted_iota(jnp.int32, (ts, KV_W), 1) < HEAD_DIM
    rows = slice(WINDOW, WINDOW + ts)
    kbuf[0, rows, :] = jnp.where(lo, kf, 0.0).astype(BF16)
    kbuf[1, rows, :] = jnp.where(lo, 0.0, kr).astype(BF16)
    kbuf[2, rows, :] = jnp.where(lo, kr, 0.0).astype(BF16)
    kbuf[3, rows, :] = jnp.where(lo, 0.0, kf).astype(BF16)
    vbuf[0, rows, :] = vf.astype(BF16)
    vbuf[1, rows, :] = vr.astype(BF16)

    base = CONV_HALO - (CONV_WIDTH - 1)
    n_shift = ts + CONV_HALO - SUBLANES
    for b in range(1, SUBLANES):
        hshift[b - 1, 0:n_shift, :] = hbuf[b:b + n_shift, :]
    rc = CONV_ROWS

    def conv_chunk(c):
        r0 = c * rc
        for l in range(CONV_CH // LANES):
            ls = slice(l * LANES, (l + 1) * LANES)
            acc = jnp.zeros((rc, LANES), F32)
            for j in range(CONV_WIDTH):
                a8, b = divmod(j + base, SUBLANES)
                rs = slice(r0 + SUBLANES * a8, r0 + SUBLANES * a8 + rc)
                tap = hbuf[rs, ls] if b == 0 else hshift[b - 1, rs, ls]
                acc = acc + tap * w_dw_ref[j:j + 1, ls]
            cbuf[r0:r0 + rc, ls] = acc
        for r1 in range(r0, r0 + rc, LN_ROWS):
            rs = slice(r1, r1 + LN_ROWS)
            y = _layer_norm(cbuf[rs, :] + b_dw_ref[...], g_cn_ref[...], b_cn_ref[...])
            y = y * jax.nn.sigmoid(y)
            ymix[rs, 0:CONV_CH] = y.astype(BF16)

    qi = lax.broadcasted_iota(jnp.int32, (2 * WINDOW, 2 * WINDOW), 0) % WINDOW
    kj = lax.broadcasted_iota(jnp.int32, (2 * WINDOW, 2 * WINDOW), 1)
    dist = qi + WINDOW - kj
    band = (dist >= 0) & (dist < WINDOW)
    top = lax.broadcasted_iota(jnp.int32, (2 * WINDOW, 1), 0) < WINDOW
    lo_out = lax.broadcasted_iota(jnp.int32, (WINDOW, 2 * HEAD_DIM), 1) < HEAD_DIM

    def attn_block(jb):
        r0 = jb * WINDOW
        valid = band & jnp.logical_or(i != 0, kj >= WINDOW) if jb == 0 else band
        for kvh in range(N_KV_HEADS):
            h0 = kvh * GQ
            c0 = h0 * HEAD_DIM
            qs = jnp.concatenate([qbuf[r0:r0 + WINDOW, c0:c0 + 2 * HEAD_DIM],
                                  qbuf[r0:r0 + WINDOW, c0 + 2 * HEAD_DIM:c0 + 4 * HEAD_DIM]], axis=0)
            pv = []
            for par in range(2):
                kk = kbuf[2 * kvh + par, r0:r0 + 2 * WINDOW, :]
                vv = vbuf[(kvh + par) % 2, r0:r0 + 2 * WINDOW, :]
                s = jnp.where(valid, _dot_nt(qs, kk), MASK_VALUE)
                sink = jnp.where(top, sinks_ref[h0 + par], sinks_ref[h0 + 2 + par])
                m = jnp.maximum(jnp.max(s, axis=-1, keepdims=True), sink)
                p = jnp.exp(s - m)
                denom = jnp.sum(p, axis=-1, keepdims=True) + jnp.exp(sink - m)
                pv.append(jnp.dot(p.astype(BF16), vv, preferred_element_type=F32) / denom)
            for pair in range(2):
                rs = slice(pair * WINDOW, (pair + 1) * WINDOW)
                o = jnp.where(lo_out, pv[0][rs], pv[1][rs])
                cs = CONV_CH + c0 + pair * 2 * HEAD_DIM
                ymix[r0:r0 + WINDOW, cs:cs + 2 * HEAD_DIM] = o.astype(BF16)

    assert ts // rc == ts // WINDOW
    for c in range(ts // rc):
        attn_block(c)
        conv_chunk(c)

    mix = jnp.dot(ymix[...], w_out_b[...], preferred_element_type=F32)
    o_ref[0] = _layer_norm(ALPHA * x + mix, g1_ref[...], b1_ref[...])

    hbuf[0:CONV_HALO, :] = hbuf[ts:ts + CONV_HALO, :]
    kbuf[:, 0:WINDOW, :] = kbuf[:, ts:ts + WINDOW, :]
    vbuf[:, 0:WINDOW, :] = vbuf[:, ts:ts + WINDOW, :]


def _const_spec(shape):
    nd = len(shape)
    return pl.BlockSpec(shape, lambda *_: (0,) * nd)


def _resident_spec(shape):
    nd = len(shape)
    return pl.BlockSpec(shape, lambda *_: (0,) * nd, pipeline_mode=pl.Buffered(1))


def _mixer(x, sinks, w_in, b_in, w_dw, b_dw, g_cn, b_cn, w_out, g1, b1):
    B, S, D = x.shape
    ts = SEQ_TILE
    tile = pl.BlockSpec((1, ts, D), lambda b, i: (b, i, 0))
    return pl.pallas_call(
        _mixer_kernel,
        grid=(B, S // ts),
        in_specs=[
            pl.BlockSpec(memory_space=pltpu.SMEM),
            tile,
            _resident_spec((D, D_IN)), _const_spec((1, D_IN)),
            _const_spec((CONV_HALO, CONV_CH)), _const_spec((1, CONV_CH)),
            _const_spec((1, CONV_CH)), _const_spec((1, CONV_CH)),
            _resident_spec((D_MIX, D)), _const_spec((1, D)), _const_spec((1, D)),
        ],
        out_specs=tile,
        out_shape=jax.ShapeDtypeStruct((B, S, D), F32),
        scratch_shapes=[
            pltpu.VMEM((D, D_IN), BF16),
            pltpu.VMEM((D_MIX, D), BF16),
            pltpu.VMEM((CONV_HALO + ts, CONV_CH), F32),
            pltpu.VMEM((SUBLANES - 1, CONV_HALO + ts, CONV_CH), F32),
            pltpu.VMEM((ts, CONV_CH), F32),
            pltpu.VMEM((ts, ATTN_W), BF16),
            pltpu.VMEM((2 * N_KV_HEADS, WINDOW + ts, KV_W), BF16),
            pltpu.VMEM((2, WINDOW + ts, KV_W), BF16),
            pltpu.VMEM((ts, D_MIX), BF16),
        ],
        compiler_params=pltpu.CompilerParams(
            dimension_semantics=("arbitrary", "arbitrary"), vmem_limit_bytes=VMEM_LIMIT),
        name="mixer",
    )(sinks, x, w_in, b_in, w_dw, b_dw, g_cn, b_cn, w_out, g1, b1)


def _memkv_kernel(mem_ref, w_ref, o_ref):
    o_ref[...] = jnp.dot(mem_ref[...].astype(BF16), w_ref[...].astype(BF16),
                         preferred_element_type=F32).astype(BF16)


def _memkv(mem2d, w_mkv):
    M, D = mem2d.shape
    N = w_mkv.shape[1]
    tn = 512
    return pl.pallas_call(
        _memkv_kernel,
        grid=(N // tn,),
        in_specs=[pl.BlockSpec((M, D), lambda j: (0, 0)), pl.BlockSpec((D, tn), lambda j: (0, j))],
        out_specs=pl.BlockSpec((M, tn), lambda j: (0, j)),
        out_shape=jax.ShapeDtypeStruct((M, N), BF16),
        compiler_params=pltpu.CompilerParams(dimension_semantics=("arbitrary",)),
        name="memkv",
    )(mem2d, w_mkv)


def _first_max(rows):
    best = rows[0]
    for r in rows[1:]:
        best = jnp.maximum(best, r)
    idx = jnp.full(best.shape, len(rows) - 1, jnp.int32)
    for k in range(len(rows) - 2, -1, -1):
        idx = jnp.where(rows[k] == best, k, idx)
    return best, idx


def _route_plan(logits_t):
    tile = logits_t.shape[1]
    row = lambda k: logits_t[k:k + 1, :]
    gmax, g_idx = _first_max([row(g) for g in range(N_GROUPS)])
    gsum = jnp.exp(row(0) - gmax)
    for g in range(1, N_GROUPS):
        gsum = gsum + jnp.exp(row(g) - gmax)
    g_p = 1.0 / gsum
    rl = []
    for e in range(EXPERTS_PER_GROUP):
        v = row(ROUTE_OFF + (N_GROUPS - 1) * EXPERTS_PER_GROUP + e)
        for g in range(N_GROUPS - 2, -1, -1):
            v = jnp.where(g_idx == g, row(ROUTE_OFF + g * EXPERTS_PER_GROUP + e), v)
        rl.append(v)
    m1, i1 = _first_max(rl)
    m2, i2 = _first_max([jnp.where(i1 == e, MASK_VALUE, rl[e]) for e in range(EXPERTS_PER_GROUP)])
    ex = jnp.exp(m2 - m1)
    w1 = 1.0 / (1.0 + ex)
    w2 = ex * w1
    e1 = g_idx * EXPERTS_PER_GROUP + i1
    e2 = g_idx * EXPERTS_PER_GROUP + i2

    eid = lax.broadcasted_iota(jnp.int32, (N_EXPERTS, tile), 0)
    hit1 = eid == e1
    hit2 = eid == e2
    oh = jnp.where(jnp.logical_or(hit1, hit2), 1.0, 0.0)
    r = lax.broadcasted_iota(jnp.int32, (tile, tile), 0)
    c = lax.broadcasted_iota(jnp.int32, (tile, tile), 1)
    tri = jnp.where(r <= c, 1.0, 0.0).astype(BF16)
    csum = jnp.dot(oh.astype(BF16), tri, preferred_element_type=F32)
    counts = jnp.broadcast_to(csum[:, tile - 1:tile], (N_EXPERTS, tile)).astype(jnp.int32)
    nch = jnp.right_shift(counts + (CHUNK - 1), CHUNK.bit_length() - 1)
    er = lax.broadcasted_iota(jnp.int32, (N_EXPERTS, N_EXPERTS), 0)
    ec = lax.broadcasted_iota(jnp.int32, (N_EXPERTS, N_EXPERTS), 1)
    lower = jnp.where(ec < er, 1.0, 0.0).astype(BF16)
    off = jnp.dot(lower, nch.astype(F32).astype(BF16), preferred_element_type=F32) * CHUNK
    pos = off + csum - oh
    lp1 = jnp.sum(jnp.where(hit1, pos, 0.0), axis=0, keepdims=True)
    lp2 = jnp.sum(jnp.where(hit2, pos, 0.0), axis=0, keepdims=True)
    zero = jnp.zeros_like(lp1)
    route_t = jnp.concatenate([lp1, lp2, g_p * w1, g_p * w2, zero, zero, zero, zero], axis=0)
    meta = jnp.concatenate([nch[:, 0:LANES], off[:, 0:LANES].astype(jnp.int32)], axis=0)
    return route_t, meta


def _memattn_kernel(x_ref, wq_ref, k_ref, v_ref, wo_ref, g2_ref, b2_ref, wr2_ref, wrh_ref, br_ref,
                    o_ref, ob_ref, route_ref, routet_ref, meta_ref, wq_b, wo_b):
    @pl.when(jnp.logical_and(pl.program_id(0) == 0, pl.program_id(1) == 0))
    def _():
        _cast_bf16(wq_b, wq_ref)
        _cast_bf16(wo_b, wo_ref)

    def rows_logits(rs):
        x = x_ref[0, rs, :]
        q = jnp.dot(x.astype(BF16), wq_b[...], preferred_element_type=F32)
        q = (q * (MEM_HEAD_DIM ** -0.5)).astype(BF16)
        outs = []
        for h in range(MEM_HEADS):
            sl = slice(h * MEM_HEAD_DIM, (h + 1) * MEM_HEAD_DIM)
            s = _dot_nt(q[:, sl], k_ref[0, :, sl])
            m = jnp.max(s, axis=-1, keepdims=True)
            p = jnp.exp(s - m)
            denom = jnp.sum(p, axis=-1, keepdims=True)
            o = jnp.dot(p.astype(BF16), v_ref[0, :, sl], preferred_element_type=F32)
            outs.append((o / denom).astype(BF16))
        o = jnp.dot(jnp.concatenate(outs, axis=-1), wo_b[...], preferred_element_type=F32)
        x2 = _layer_norm(ALPHA * x + o, g2_ref[...], b2_ref[...])
        o_ref[0, rs, :] = x2
        x2h = x2.astype(BF16)
        ob_ref[0, rs, :] = x2h
        x2l = (x2 - x2h.astype(F32)).astype(BF16)
        hh = jnp.dot(x2h, wr2_ref[...], preferred_element_type=F32)
        return (hh[:, 0:LANES] + hh[:, LANES:2 * LANES]
                + jnp.dot(x2l, wrh_ref[...], preferred_element_type=F32) + br_ref[...])

    ts = x_ref.shape[1]
    logits = jnp.concatenate(
        [rows_logits(slice(r0, r0 + MEM_ROWS)) for r0 in range(0, ts, MEM_ROWS)], axis=0)
    route_t, meta = _route_plan(jnp.transpose(logits))
    routet_ref[0] = route_t
    pad = jnp.zeros((LANES - SUBLANES, ts), F32)
    route_ref[0] = jnp.transpose(jnp.concatenate([route_t, pad], axis=0))
    meta_ref[0] = meta


def _memattn(x1, wq, kvm, wo, g2, b2, wr, br):
    B, S, D = x1.shape
    ts = SEQ_TILE
    assert ts == MOE_TILE
    nt = S // ts
    c = wr * (2.0 ** 16 + 1.0)
    w_high = c - (c - wr)
    wrh = w_high.astype(BF16)
    wr2 = jnp.concatenate([wrh, (wr - w_high).astype(BF16)], axis=1)
    tile = pl.BlockSpec((1, ts, D), lambda b, i: (b, i, 0))
    kspec = pl.BlockSpec((1, MEM_LEN, D), lambda b, i: (b, 0, 0))
    vspec = pl.BlockSpec((1, MEM_LEN, D), lambda b, i: (b, 0, 1))
    return pl.pallas_call(
        _memattn_kernel,
        grid=(B, nt),
        in_specs=[tile, _resident_spec((D, D)), kspec, vspec, _resident_spec((D, D)),
                  _const_spec((1, D)), _const_spec((1, D)),
                  _const_spec((D, 2 * LANES)), _const_spec((D, LANES)), _const_spec((1, LANES))],
        out_specs=[tile, tile, pl.BlockSpec((1, ts, LANES), lambda b, i: (b, i, 0)),
                   pl.BlockSpec((1, SUBLANES, ts), lambda b, i: (b * nt + i, 0, 0)),
                   pl.BlockSpec((1, 2 * N_EXPERTS, LANES), lambda b, i: (b * nt + i, 0, 0))],
        out_shape=[jax.ShapeDtypeStruct((B, S, D), F32),
                   jax.ShapeDtypeStruct((B, S, D), BF16),
                   jax.ShapeDtypeStruct((B, S, LANES), F32),
                   jax.ShapeDtypeStruct((B * nt, SUBLANES, ts), F32),
                   jax.ShapeDtypeStruct((B * nt, 2 * N_EXPERTS, LANES), jnp.int32)],
        scratch_shapes=[pltpu.VMEM((D, D), BF16), pltpu.VMEM((D, D), BF16)],
        compiler_params=pltpu.CompilerParams(
            dimension_semantics=("arbitrary", "arbitrary"), vmem_limit_bytes=VMEM_LIMIT),
        name="memattn",
    )(x1, wq, kvm, kvm, wo, g2, b2, wr2, wrh, br)


def _gmm_blocks(n_tokens):
    rows = (n_tokens * TOP_K + (n_tokens // MOE_TILE) * N_EXPERTS * (CHUNK - 1)
            + N_EXPERTS * (GMM_ROWS - CHUNK))
    return -(-rows // GMM_ROWS)


def _plan(meta, T):
    nch = meta[:, :N_EXPERTS, 0]
    off16 = meta[:, N_EXPERTS:, 0]
    n16 = nch * CHUNK
    n_e = jnp.sum(n16, axis=0)
    reg = (n_e + GMM_ROWS - 1) // GMM_ROWS * GMM_ROWS
    gend = jnp.cumsum(reg)
    gbase = gend - reg
    dst = gbase[None, :] + jnp.cumsum(n16, axis=0) - n16
    blk_row = jnp.arange(_gmm_blocks(T), dtype=jnp.int32)[:, None] * GMM_ROWS
    blk_expert = jnp.sum(blk_row >= gend[None, :], axis=1)
    blk_expert = jnp.minimum(blk_expert, N_EXPERTS - 1).astype(jnp.int32)
    n_used = (gend[-1] // GMM_ROWS).astype(jnp.int32).reshape(1)
    i32 = lambda a: a.astype(jnp.int32)
    fill_start = jnp.concatenate([gbase + n_e, gend[-1:]])
    fill_n = jnp.concatenate([reg - n_e, _gmm_blocks(T) * GMM_ROWS - gend[-1:]]) // CHUNK
    return dict(src=i32(off16).reshape(-1), dst=i32(dst).reshape(-1), nch=i32(nch).reshape(-1),
                ntot=i32(jnp.sum(nch, axis=1)), fill_start=i32(fill_start), fill_n=i32(fill_n),
                fill_tot=i32(jnp.sum(fill_n)).reshape(1), blk_expert=blk_expert, n_used=n_used)


def _chunk_copy(src_ref, src_row, dst_ref, dst_row, sem):
    return pltpu.make_async_copy(
        src_ref.at[pl.ds(pl.multiple_of(src_row, CHUNK), CHUNK), :],
        dst_ref.at[pl.ds(pl.multiple_of(dst_row, CHUNK), CHUNK), :], sem)


def _wait_chunks(n, src_ref, dst_ref, sem):
    def body(c, carry):
        _chunk_copy(src_ref, 0, dst_ref, 0, sem).wait()
        return carry

    lax.fori_loop(0, n, body, 0)


def _dispatch_kernel(src_ref, dst_ref, nch_ref, ntot_ref, fstart_ref, fn_ref, ftot_ref,
                     lp_ref, x_ref, xs_hbm, xt, zbuf, sems):
    t = pl.program_id(0)
    nt = pl.num_programs(0)
    slot = t % 2
    xt_s = xt.at[slot]
    sem = sems.at[slot]

    @pl.when(t >= 2)
    def _():
        _wait_chunks(ntot_ref[t - 2], xt_s, xs_hbm, sem)

    lp = lp_ref[0].astype(jnp.int32)
    r = lax.broadcasted_iota(jnp.int32, (SORT_ROWS, MOE_TILE), 0)
    hit = jnp.logical_or(lp[0:1, :] == r, lp[1:2, :] == r)
    p = jnp.where(hit, 1.0, 0.0).astype(BF16)
    xt_s[...] = jnp.dot(p, x_ref[...], preferred_element_type=F32).astype(BF16)

    def per_expert(e, carry):
        k = t * N_EXPERTS + e

        def issue(c, carry2):
            _chunk_copy(xt_s, src_ref[k] + c * CHUNK, xs_hbm, dst_ref[k] + c * CHUNK, sem).start()
            return carry2

        return lax.fori_loop(0, nch_ref[k], issue, carry)

    lax.fori_loop(0, N_EXPERTS, per_expert, 0)

    @pl.when(t == nt - 1)
    def _():
        zbuf[...] = jnp.zeros_like(zbuf)

        def per_range(e, carry):
            def issue(c, carry2):
                _chunk_copy(zbuf, 0, xs_hbm, fstart_ref[e] + c * CHUNK, sem).start()
                return carry2

            return lax.fori_loop(0, fn_ref[e], issue, carry)

        lax.fori_loop(0, N_EXPERTS + 1, per_range, 0)
        _wait_chunks(ntot_ref[t - 1], xt.at[1 - slot], xs_hbm, sems.at[1 - slot])
        _wait_chunks(ntot_ref[t] + ftot_ref[0], xt_s, xs_hbm, sem)


def _dispatch(plan, lpt, x2b):
    T, D = x2b.shape
    nt = T // MOE_TILE
    assert nt >= 2
    grid_spec = pltpu.PrefetchScalarGridSpec(
        num_scalar_prefetch=7,
        grid=(nt,),
        in_specs=[pl.BlockSpec((1, SUBLANES, MOE_TILE), lambda t, *_: (t, 0, 0)),
                  pl.BlockSpec((MOE_TILE, D), lambda t, *_: (t, 0))],
        out_specs=pl.BlockSpec(memory_space=pl.ANY),
        scratch_shapes=[pltpu.VMEM((2, SORT_ROWS, D), BF16), pltpu.VMEM((CHUNK, D), BF16),
                        pltpu.SemaphoreType.DMA((2,))],
    )
    return pl.pallas_call(
        _dispatch_kernel,
        grid_spec=grid_spec,
        out_shape=jax.ShapeDtypeStruct((_gmm_blocks(T) * GMM_ROWS, D), BF16),
        compiler_params=pltpu.CompilerParams(
            dimension_semantics=("arbitrary",), vmem_limit_bytes=VMEM_LIMIT),
        name="moe_dispatch",
    )(plan["src"], plan["dst"], plan["nch"], plan["ntot"], plan["fill_start"], plan["fill_n"],
      plan["fill_tot"], lpt, x2b)


def _gmm_kernel(be_ref, nu_ref, x_ref, wg_ref, wu_ref, wd_ref, o_ref, wg_b, wu_b, wd_b):
    b = pl.program_id(0)

    @pl.when(jnp.logical_or(b == 0, be_ref[b] != be_ref[jnp.maximum(b - 1, 0)]))
    def _():
        _cast_bf16(wg_b, wg_ref.at[0])
        _cast_bf16(wu_b, wu_ref.at[0])
        _cast_bf16(wd_b, wd_ref.at[0])

    @pl.when(b < nu_ref[0])
    def _():
        xb = x_ref[...]
        g = jnp.dot(xb, wg_b[...], preferred_element_type=F32)
        u = jnp.dot(xb, wu_b[...], preferred_element_type=F32)
        h = (g * jax.nn.sigmoid(g)) * u
        o_ref[...] = jnp.dot(h.astype(BF16), wd_b[...], preferred_element_type=F32).astype(BF16)

    @pl.when(pl.program_id(0) >= nu_ref[0])
    def _():
        o_ref[...] = jnp.zeros_like(o_ref)


def _gmm(plan, xs, wg, wu, wd):
    R, D = xs.shape
    rows = pl.BlockSpec((GMM_ROWS, D), lambda b, be, nu: (jnp.minimum(b, nu[0] - 1), 0))
    grid_spec = pltpu.PrefetchScalarGridSpec(
        num_scalar_prefetch=2,
        grid=(R // GMM_ROWS,),
        in_specs=[rows,
                  pl.BlockSpec((1, D, D_EXPERT), lambda b, be, nu: (be[b], 0, 0)),
                  pl.BlockSpec((1, D, D_EXPERT), lambda b, be, nu: (be[b], 0, 0)),
                  pl.BlockSpec((1, D_EXPERT, D), lambda b, be, nu: (be[b], 0, 0))],
        out_specs=pl.BlockSpec((GMM_ROWS, D), lambda b, be, nu: (b, 0)),
        scratch_shapes=[pltpu.VMEM((D, D_EXPERT), BF16), pltpu.VMEM((D, D_EXPERT), BF16),
                        pltpu.VMEM((D_EXPERT, D), BF16)],
    )
    return pl.pallas_call(
        _gmm_kernel,
        grid_spec=grid_spec,
        out_shape=jax.ShapeDtypeStruct((R, D), BF16),
        compiler_params=pltpu.CompilerParams(
            dimension_semantics=("arbitrary",), vmem_limit_bytes=VMEM_LIMIT),
        name="moe_gmm",
    )(plan["blk_expert"], plan["n_used"], xs, wg, wu, wd)


def _combine_kernel(src_ref, dst_ref, nch_ref, ntot_ref, cm_ref, x_ref, ys_hbm, g3_ref, b3_ref, o_ref,
                    yt, sems):
    t = pl.program_id(0)
    nt = pl.num_programs(0)
    slot = t % 2

    def fetch(tile, s):
        def per_expert(e, carry):
            k = tile * N_EXPERTS + e

            def issue(c, carry2):
                _chunk_copy(ys_hbm, dst_ref[k] + c * CHUNK, yt.at[s], src_ref[k] + c * CHUNK,
                            sems.at[s]).start()
                return carry2

            return lax.fori_loop(0, nch_ref[k], issue, carry)

        lax.fori_loop(0, N_EXPERTS, per_expert, 0)

    @pl.when(t == 0)
    def _():
        yt[...] = jnp.zeros_like(yt)
        fetch(0, 0)

    @pl.when(t + 1 < nt)
    def _():
        fetch(t + 1, 1 - slot)

    cm = cm_ref[...]
    col = lax.broadcasted_iota(jnp.int32, (MOE_TILE, SORT_ROWS), 1)
    lp0 = cm[:, 0:1].astype(jnp.int32)
    lp1 = cm[:, 1:2].astype(jnp.int32)
    w = jnp.where(col == lp0, cm[:, 2:3], 0.0) + jnp.where(col == lp1, cm[:, 3:4], 0.0)
    w = w.astype(BF16)

    _wait_chunks(ntot_ref[t], ys_hbm, yt.at[slot], sems.at[slot])
    y = jnp.dot(w, yt[slot], preferred_element_type=F32)
    o_ref[...] = _layer_norm(ALPHA * x_ref[...] + y, g3_ref[...], b3_ref[...])


def _combine(plan, route, x2, ys, g3, b3):
    T, D = x2.shape
    nt = T // MOE_TILE
    grid_spec = pltpu.PrefetchScalarGridSpec(
        num_scalar_prefetch=4,
        grid=(nt,),
        in_specs=[pl.BlockSpec((MOE_TILE, LANES), lambda t, *_: (t, 0)),
                  pl.BlockSpec((MOE_TILE, D), lambda t, *_: (t, 0)),
                  pl.BlockSpec(memory_space=pl.ANY),
                  pl.BlockSpec((1, D), lambda t, *_: (0, 0)),
                  pl.BlockSpec((1, D), lambda t, *_: (0, 0))],
        out_specs=pl.BlockSpec((MOE_TILE, D), lambda t, *_: (t, 0)),
        scratch_shapes=[pltpu.VMEM((2, SORT_ROWS, D), BF16), pltpu.SemaphoreType.DMA((2,))],
    )
    return pl.pallas_call(
        _combine_kernel,
        grid_spec=grid_spec,
        out_shape=jax.ShapeDtypeStruct((T, D), F32),
        compiler_params=pltpu.CompilerParams(
            dimension_semantics=("arbitrary",), vmem_limit_bytes=VMEM_LIMIT),
        name="moe_combine",
    )(plan["src"], plan["dst"], plan["nch"], plan["ntot"], route, x2, ys, g3, b3)


def _moe(x2, x2b, route, lpt, meta, wg, wu, wd, g3, b3):
    plan = _plan(meta, x2.shape[0])
    xs = _dispatch(plan, lpt, x2b)
    ys = _gmm(plan, xs, wg, wu, wd)
    return _combine(plan, route, x2, ys, g3, b3)


def _row(v):
    return v.reshape(1, -1).astype(F32)


def kernel(x, mem, w_in, b_in, w_dw, b_dw, g_conv_norm, b_conv_norm, attn_sinks, w_out, g_ln1, b_ln1,
           w_mq, w_mkv, w_mo, g_ln2, b_ln2, w_group, b_group, w_router, b_router, w_gate, w_up, w_down,
           g_ln3, b_ln3):
    B, S, D = x.shape
    for l in range(DEPTH):
        w_dw_p = jnp.zeros((CONV_HALO, CONV_CH), F32).at[:CONV_WIDTH].set(w_dw[l])
        x1 = _mixer(x, attn_sinks[l].astype(F32), w_in[l], _row(b_in[l]), w_dw_p,
                    _row(b_dw[l]), _row(g_conv_norm[l]), _row(b_conv_norm[l]),
                    w_out[l], _row(g_ln1[l]), _row(b_ln1[l]))

        kvm = _memkv(mem.reshape(B * MEM_LEN, D), w_mkv[l]).reshape(B, MEM_LEN, 2 * D)

        wr = jnp.concatenate(
            [w_group[l], jnp.transpose(w_router[l], (1, 0, 2)).reshape(D, N_EXPERTS)], axis=1)
        wr = jnp.pad(wr, ((0, 0), (0, LANES - wr.shape[1])))
        br = jnp.pad(jnp.concatenate([b_group[l], b_router[l].reshape(-1)]), (0, LANES - N_GROUPS - N_EXPERTS))
        x2, x2b, route, lpt, meta = _memattn(x1, w_mq[l], kvm, w_mo[l],
                                 _row(g_ln2[l]), _row(b_ln2[l]), wr.astype(F32), _row(br))

        T = B * S
        y = _moe(x2.reshape(T, D), x2b.reshape(T, D), route.reshape(T, LANES), lpt, meta,
                 w_gate[l], w_up[l], w_down[l],
                 _row(g_ln3[l]), _row(b_ln3[l]))
        x = y.reshape(B, S, D)
    return x
```

```python
import functools

import jax
import jax.numpy as jnp
from jax import lax
from jax.experimental import pallas as pl
from jax.experimental.pallas import tpu as pltpu

D_MODEL = 1024
MEM_LEN = 256
CONV_CH = 512
CONV_WIDTH = 31
N_HEADS = 8
N_KV_HEADS = 2
HEAD_DIM = 64
GQ = N_HEADS // N_KV_HEADS
ATTN_W = N_HEADS * HEAD_DIM
KV_W = N_KV_HEADS * HEAD_DIM
WINDOW = 128
D_MIX = CONV_CH + ATTN_W
D_IN = 2 * CONV_CH + ATTN_W + 2 * KV_W
MEM_HEADS = 4
MEM_HEAD_DIM = D_MODEL // MEM_HEADS
N_GROUPS = 4
EXPERTS_PER_GROUP = 4
N_EXPERTS = N_GROUPS * EXPERTS_PER_GROUP
D_EXPERT = D_MODEL // 2
DEPTH = 1
ALPHA = (2.0 * DEPTH) ** 0.25
LN_EPS = 1e-5

LANES = 128
SUBLANES = 8
CONV_ROWS = 128
LN_ROWS = 64
MEM_ROWS = 512
MASK_VALUE = -1e30
CONV_HALO = 32
SEQ_TILE = 512
MOE_TILE = 512
CHUNK = 16
GMM_ROWS = 512
TOP_K = 2
SORT_ROWS = -(-(MOE_TILE * TOP_K + N_EXPERTS * (CHUNK - 1)) // 256) * 256
MAX_CHUNKS = SORT_ROWS // CHUNK
WAIT_GROUP = 8
ROUTE_OFF = N_GROUPS
VMEM_LIMIT = 56 * 1024 * 1024

BF16 = jnp.bfloat16
F32 = jnp.float32


def _layer_norm(x, g, b):
    mu = jnp.mean(x, axis=-1, keepdims=True)
    xc = x - mu
    var = jnp.mean(xc * xc, axis=-1, keepdims=True)
    return xc * lax.rsqrt(var + LN_EPS) * g + b


def _cast_bf16(dst_ref, src_ref):
    rows = 256
    for r0 in range(0, src_ref.shape[0], rows):
        dst_ref[r0:r0 + rows, :] = src_ref[r0:r0 + rows, :].astype(BF16)


def _dot_nt(a, b):
    return lax.dot_general(a, b, (((1,), (1,)), ((), ())), preferred_element_type=F32)


def _mixer_kernel(sinks_ref, x_ref, w_in_ref, b_in_ref, w_dw_ref, b_dw_ref, g_cn_ref, b_cn_ref,
                  w_out_ref, g1_ref, b1_ref, o_ref, w_in_b, w_out_b, hbuf, hshift, cbuf, qbuf, kbuf, vbuf, ymix):
    i = pl.program_id(1)
    ts = SEQ_TILE

    @pl.when(jnp.logical_and(pl.program_id(0) == 0, i == 0))
    def _():
        _cast_bf16(w_in_b, w_in_ref)
        _cast_bf16(w_out_b, w_out_ref)

    @pl.when(i == 0)
    def _():
        hbuf[0:CONV_HALO, :] = jnp.zeros((CONV_HALO, CONV_CH), F32)
        kbuf[:, 0:WINDOW, :] = jnp.zeros((2 * N_KV_HEADS, WINDOW, KV_W), BF16)
        vbuf[:, 0:WINDOW, :] = jnp.zeros((2, WINDOW, KV_W), BF16)

    x = x_ref[0]
    u = jnp.dot(x.astype(BF16), w_in_b[...], preferred_element_type=F32) + b_in_ref[...]
    a = u[:, 0:CONV_CH]
    gate = u[:, CONV_CH:2 * CONV_CH]
    hbuf[CONV_HALO:CONV_HALO + ts, :] = a * jax.nn.sigmoid(gate)
    qbuf[...] = (u[:, 2 * CONV_CH:2 * CONV_CH + ATTN_W] * (HEAD_DIM ** -0.5)).astype(BF16)
    kf = u[:, 2 * CONV_CH + ATTN_W:2 * CONV_CH + ATTN_W + KV_W]
    vf = u[:, 2 * CONV_CH + ATTN_W + KV_W:D_IN]
    kr = pltpu.roll(kf, HEAD_DIM, axis=1)
    vr = pltpu.roll(vf, HEAD_DIM, axis=1)
    lo = lax.broadcasted_iota(jnp.int32, (ts, KV_W), 1) < HEAD_DIM
    rows = slice(WINDOW, WINDOW + ts)
    kbuf[0, rows, :] = jnp.where(lo, kf, 0.0).astype(BF16)
    kbuf[1, rows, :] = jnp.where(lo, 0.0, kr).astype(BF16)
    kbuf[2, rows, :] = jnp.where(lo, kr, 0.0).astype(BF16)
    kbuf[3, rows, :] = jnp.where(lo, 0.0, kf).astype(BF16)
    vbuf[0, rows, :] = vf.astype(BF16)
    vbuf[1, rows, :] = vr.astype(BF16)

    base = CONV_HALO - (CONV_WIDTH - 1)
    n_shift = ts + CONV_HALO - SUBLANES
    for b in range(1, SUBLANES):
        hshift[b - 1, 0:n_shift, :] = hbuf[b:b + n_shift, :]
    rc = CONV_ROWS

    def conv_chunk(c):
        r0 = c * rc
        for l in range(CONV_CH // LANES):
            ls = slice(l * LANES, (l + 1) * LANES)
            acc = jnp.zeros((rc, LANES), F32)
            for j in range(CONV_WIDTH):
                a8, b = divmod(j + base, SUBLANES)
                rs = slice(r0 + SUBLANES * a8, r0 + SUBLANES * a8 + rc)
                tap = hbuf[rs, ls] if b == 0 else hshift[b - 1, rs, ls]
                acc = acc + tap * w_dw_ref[j:j + 1, ls]
            cbuf[r0:r0 + rc, ls] = acc
        for r1 in range(r0, r0 + rc, LN_ROWS):
            rs = slice(r1, r1 + LN_ROWS)
            y = _layer_norm(cbuf[rs, :] + b_dw_ref[...], g_cn_ref[...], b_cn_ref[...])
            y = y * jax.nn.sigmoid(y)
            ymix[rs, 0:CONV_CH] = y.astype(BF16)

    qi = lax.broadcasted_iota(jnp.int32, (2 * WINDOW, 2 * WINDOW), 0) % WINDOW
    kj = lax.broadcasted_iota(jnp.int32, (2 * WINDOW, 2 * WINDOW), 1)
    dist = qi + WINDOW - kj
    band = (dist >= 0) & (dist < WINDOW)
    top = lax.broadcasted_iota(jnp.int32, (2 * WINDOW, 1), 0) < WINDOW
    lo_out = lax.broadcasted_iota(jnp.int32, (WINDOW, 2 * HEAD_DIM), 1) < HEAD_DIM

    def attn_block(jb):
        r0 = jb * WINDOW
        valid = band & jnp.logical_or(i != 0, kj >= WINDOW) if jb == 0 else band
        for kvh in range(N_KV_HEADS):
            h0 = kvh * GQ
            c0 = h0 * HEAD_DIM
            qs = jnp.concatenate([qbuf[r0:r0 + WINDOW, c0:c0 + 2 * HEAD_DIM],
                                  qbuf[r0:r0 + WINDOW, c0 + 2 * HEAD_DIM:c0 + 4 * HEAD_DIM]], axis=0)
            pv = []
            for par in range(2):
                kk = kbuf[2 * kvh + par, r0:r0 + 2 * WINDOW, :]
                vv = vbuf[(kvh + par) % 2, r0:r0 + 2 * WINDOW, :]
                s = jnp.where(valid, _dot_nt(qs, kk), MASK_VALUE)
                sink = jnp.where(top, sinks_ref[h0 + par], sinks_ref[h0 + 2 + par])
                m = jnp.maximum(jnp.max(s, axis=-1, keepdims=True), sink)
                p = jnp.exp(s - m)
                denom = jnp.sum(p, axis=-1, keepdims=True) + jnp.exp(sink - m)
                pv.append(jnp.dot(p.astype(BF16), vv, preferred_element_type=F32) / denom)
            for pair in range(2):
                rs = slice(pair * WINDOW, (pair + 1) * WINDOW)
                o = jnp.where(lo_out, pv[0][rs], pv[1][rs])
                cs = CONV_CH + c0 + pair * 2 * HEAD_DIM
                ymix[r0:r0 + WINDOW, cs:cs + 2 * HEAD_DIM] = o.astype(BF16)

    assert ts // rc == ts // WINDOW
    for c in range(ts // rc):
        attn_block(c)
        conv_chunk(c)

    mix = jnp.dot(ymix[...], w_out_b[...], preferred_element_type=F32)
    o_ref[0] = _layer_norm(ALPHA * x + mix, g1_ref[...], b1_ref[...])

    hbuf[0:CONV_HALO, :] = hbuf[ts:ts + CONV_HALO, :]
    kbuf[:, 0:WINDOW, :] = kbuf[:, ts:ts + WINDOW, :]
    vbuf[:, 0:WINDOW, :] = vbuf[:, ts:ts + WINDOW, :]


def _const_spec(shape):
    nd = len(shape)
    return pl.BlockSpec(shape, lambda *_: (0,) * nd)


def _resident_spec(shape):
    nd = len(shape)
    return pl.BlockSpec(shape, lambda *_: (0,) * nd, pipeline_mode=pl.Buffered(1))


def _mixer(x, sinks, w_in, b_in, w_dw, b_dw, g_cn, b_cn, w_out, g1, b1):
    B, S, D = x.shape
    ts = SEQ_TILE
    tile = pl.BlockSpec((1, ts, D), lambda b, i: (b, i, 0))
    return pl.pallas_call(
        _mixer_kernel,
        grid=(B, S // ts),
        in_specs=[
            pl.BlockSpec(memory_space=pltpu.SMEM),
            tile,
            _resident_spec((D, D_IN)), _const_spec((1, D_IN)),
            _const_spec((CONV_HALO, CONV_CH)), _const_spec((1, CONV_CH)),
            _const_spec((1, CONV_CH)), _const_spec((1, CONV_CH)),
            _resident_spec((D_MIX, D)), _const_spec((1, D)), _const_spec((1, D)),
        ],
        out_specs=tile,
        out_shape=jax.ShapeDtypeStruct((B, S, D), F32),
        scratch_shapes=[
            pltpu.VMEM((D, D_IN), BF16),
            pltpu.VMEM((D_MIX, D), BF16),
            pltpu.VMEM((CONV_HALO + ts, CONV_CH), F32),
            pltpu.VMEM((SUBLANES - 1, CONV_HALO + ts, CONV_CH), F32),
            pltpu.VMEM((ts, CONV_CH), F32),
            pltpu.VMEM((ts, ATTN_W), BF16),
            pltpu.VMEM((2 * N_KV_HEADS, WINDOW + ts, KV_W), BF16),
            pltpu.VMEM((2, WINDOW + ts, KV_W), BF16),
            pltpu.VMEM((ts, D_MIX), BF16),
        ],
        compiler_params=pltpu.CompilerParams(
            dimension_semantics=("arbitrary", "arbitrary"), vmem_limit_bytes=VMEM_LIMIT),
        name="mixer",
    )(sinks, x, w_in, b_in, w_dw, b_dw, g_cn, b_cn, w_out, g1, b1)


def _memkv_kernel(mem_ref, w_ref, o_ref):
    o_ref[...] = jnp.dot(mem_ref[...].astype(BF16), w_ref[...].astype(BF16),
                         preferred_element_type=F32).astype(BF16)


def _memkv(mem2d, w_mkv):
    M, D = mem2d.shape
    N = w_mkv.shape[1]
    tn = 512
    return pl.pallas_call(
        _memkv_kernel,
        grid=(N // tn,),
        in_specs=[pl.BlockSpec((M, D), lambda j: (0, 0)), pl.BlockSpec((D, tn), lambda j: (0, j))],
        out_specs=pl.BlockSpec((M, tn), lambda j: (0, j)),
        out_shape=jax.ShapeDtypeStruct((M, N), BF16),
        compiler_params=pltpu.CompilerParams(dimension_semantics=("arbitrary",)),
        name="memkv",
    )(mem2d, w_mkv)


def _first_max(rows):
    best = rows[0]
    for r in rows[1:]:
        best = jnp.maximum(best, r)
    idx = jnp.full(best.shape, len(rows) - 1, jnp.int32)
    for k in range(len(rows) - 2, -1, -1):
        idx = jnp.where(rows[k] == best, k, idx)
    return best, idx


def _route_plan(logits_t):
    tile = logits_t.shape[1]
    row = lambda k: logits_t[k:k + 1, :]
    gmax, g_idx = _first_max([row(g) for g in range(N_GROUPS)])
    gsum = jnp.exp(row(0) - gmax)
    for g in range(1, N_GROUPS):
        gsum = gsum + jnp.exp(row(g) - gmax)
    g_p = 1.0 / gsum
    rl = []
    for e in range(EXPERTS_PER_GROUP):
        v = row(ROUTE_OFF + (N_GROUPS - 1) * EXPERTS_PER_GROUP + e)
        for g in range(N_GROUPS - 2, -1, -1):
            v = jnp.where(g_idx == g, row(ROUTE_OFF + g * EXPERTS_PER_GROUP + e), v)
        rl.append(v)
    m1, i1 = _first_max(rl)
    m2, i2 = _first_max([jnp.where(i1 == e, MASK_VALUE, rl[e]) for e in range(EXPERTS_PER_GROUP)])
    ex = jnp.exp(m2 - m1)
    w1 = 1.0 / (1.0 + ex)
    w2 = ex * w1
    e1 = g_idx * EXPERTS_PER_GROUP + i1
    e2 = g_idx * EXPERTS_PER_GROUP + i2

    eid = lax.broadcasted_iota(jnp.int32, (N_EXPERTS, tile), 0)
    hit1 = eid == e1
    hit2 = eid == e2
    oh = jnp.where(jnp.logical_or(hit1, hit2), 1.0, 0.0)
    r = lax.broadcasted_iota(jnp.int32, (tile, tile), 0)
    c = lax.broadcasted_iota(jnp.int32, (tile, tile), 1)
    tri = jnp.where(r <= c, 1.0, 0.0).astype(BF16)
    csum = jnp.dot(oh.astype(BF16), tri, preferred_element_type=F32)
    counts = jnp.broadcast_to(csum[:, tile - 1:tile], (N_EXPERTS, tile)).astype(jnp.int32)
    nch = jnp.right_shift(counts + (CHUNK - 1), CHUNK.bit_length() - 1)
    er = lax.broadcasted_iota(jnp.int32, (N_EXPERTS, N_EXPERTS), 0)
    ec = lax.broadcasted_iota(jnp.int32, (N_EXPERTS, N_EXPERTS), 1)
    lower = jnp.where(ec < er, 1.0, 0.0).astype(BF16)
    off = jnp.dot(lower, nch.astype(F32).astype(BF16), preferred_element_type=F32) * CHUNK
    pos = off + csum - oh
    lp1 = jnp.sum(jnp.where(hit1, pos, 0.0), axis=0, keepdims=True)
    lp2 = jnp.sum(jnp.where(hit2, pos, 0.0), axis=0, keepdims=True)
    zero = jnp.zeros_like(lp1)
    route_t = jnp.concatenate([lp1, lp2, g_p * w1, g_p * w2, zero, zero, zero, zero], axis=0)
    meta = jnp.concatenate([nch[:, 0:LANES], off[:, 0:LANES].astype(jnp.int32)], axis=0)
    return route_t, meta


def _memattn_kernel(x_ref, wq_ref, k_ref, v_ref, wo_ref, g2_ref, b2_ref, wr2_ref, wrh_ref, br_ref,
                    o_ref, ob_ref, route_ref, routet_ref, meta_ref, wq_b, wo_b):
    @pl.when(jnp.logical_and(pl.program_id(0) == 0, pl.program_id(1) == 0))
    def _():
        _cast_bf16(wq_b, wq_ref)
        _cast_bf16(wo_b, wo_ref)

    def rows_logits(rs):
        x = x_ref[0, rs, :]
        q = jnp.dot(x.astype(BF16), wq_b[...], preferred_element_type=F32)
        q = (q * (MEM_HEAD_DIM ** -0.5)).astype(BF16)
        outs = []
        for h in range(MEM_HEADS):
            sl = slice(h * MEM_HEAD_DIM, (h + 1) * MEM_HEAD_DIM)
            s = _dot_nt(q[:, sl], k_ref[0, :, sl])
            m = jnp.max(s, axis=-1, keepdims=True)
            p = jnp.exp(s - m)
            denom = jnp.sum(p, axis=-1, keepdims=True)
            o = jnp.dot(p.astype(BF16), v_ref[0, :, sl], preferred_element_type=F32)
            outs.append((o / denom).astype(BF16))
        o = jnp.dot(jnp.concatenate(outs, axis=-1), wo_b[...], preferred_element_type=F32)
        x2 = _layer_norm(ALPHA * x + o, g2_ref[...], b2_ref[...])
        o_ref[0, rs, :] = x2
        x2h = x2.astype(BF16)
        ob_ref[0, rs, :] = x2h
        x2l = (x2 - x2h.astype(F32)).astype(BF16)
        hh = jnp.dot(x2h, wr2_ref[...], preferred_element_type=F32)
        return (hh[:, 0:LANES] + hh[:, LANES:2 * LANES]
                + jnp.dot(x2l, wrh_ref[...], preferred_element_type=F32) + br_ref[...])

    ts = x_ref.shape[1]
    logits = jnp.concatenate(
        [rows_logits(slice(r0, r0 + MEM_ROWS)) for r0 in range(0, ts, MEM_ROWS)], axis=0)
    route_t, meta = _route_plan(jnp.transpose(logits))
    routet_ref[0] = route_t
    pad = jnp.zeros((LANES - SUBLANES, ts), F32)
    route_ref[0] = jnp.transpose(jnp.concatenate([route_t, pad], axis=0))
    meta_ref[0] = meta


def _memattn(x1, wq, kvm, wo, g2, b2, wr, br):
    B, S, D = x1.shape
    ts = SEQ_TILE
    assert ts == MOE_TILE
    nt = S // ts
    c = wr * (2.0 ** 16 + 1.0)
    w_high = c - (c - wr)
    wrh = w_high.astype(BF16)
    wr2 = jnp.concatenate([wrh, (wr - w_high).astype(BF16)], axis=1)
    tile = pl.BlockSpec((1, ts, D), lambda b, i: (b, i, 0))
    kspec = pl.BlockSpec((1, MEM_LEN, D), lambda b, i: (b, 0, 0))
    vspec = pl.BlockSpec((1, MEM_LEN, D), lambda b, i: (b, 0, 1))
    return pl.pallas_call(
        _memattn_kernel,
        grid=(B, nt),
        in_specs=[tile, _resident_spec((D, D)), kspec, vspec, _resident_spec((D, D)),
                  _const_spec((1, D)), _const_spec((1, D)),
                  _const_spec((D, 2 * LANES)), _const_spec((D, LANES)), _const_spec((1, LANES))],
        out_specs=[tile, tile, pl.BlockSpec((1, ts, LANES), lambda b, i: (b, i, 0)),
                   pl.BlockSpec((1, SUBLANES, ts), lambda b, i: (b * nt + i, 0, 0)),
                   pl.BlockSpec((1, 2 * N_EXPERTS, LANES), lambda b, i: (b * nt + i, 0, 0))],
        out_shape=[jax.ShapeDtypeStruct((B, S, D), F32),
                   jax.ShapeDtypeStruct((B, S, D), BF16),
                   jax.ShapeDtypeStruct((B, S, LANES), F32),
                   jax.ShapeDtypeStruct((B * nt, SUBLANES, ts), F32),
                   jax.ShapeDtypeStruct((B * nt, 2 * N_EXPERTS, LANES), jnp.int32)],
        scratch_shapes=[pltpu.VMEM((D, D), BF16), pltpu.VMEM((D, D), BF16)],
        compiler_params=pltpu.CompilerParams(
            dimension_semantics=("arbitrary", "arbitrary"), vmem_limit_bytes=VMEM_LIMIT),
        name="memattn",
    )(x1, wq, kvm, kvm, wo, g2, b2, wr2, wrh, br)


def _gmm_blocks(n_tokens):
    rows = (n_tokens * TOP_K + (n_tokens // MOE_TILE) * N_EXPERTS * (CHUNK - 1)
            + N_EXPERTS * (GMM_ROWS - CHUNK))
    return -(-rows // GMM_ROWS)


def _plan(meta, T):
    nch = meta[:, :N_EXPERTS, 0]
    n16 = nch * CHUNK
    n_e = jnp.sum(n16, axis=0)
    reg = (n_e + GMM_ROWS - 1) // GMM_ROWS * GMM_ROWS
    gend = jnp.cumsum(reg)
    gbase = gend - reg
    dst = gbase[None, :] + jnp.cumsum(n16, axis=0) - n16
    blk_row = jnp.arange(_gmm_blocks(T), dtype=jnp.int32)[:, None] * GMM_ROWS
    blk_expert = jnp.sum(blk_row >= gend[None, :], axis=1)
    blk_expert = jnp.minimum(blk_expert, N_EXPERTS - 1).astype(jnp.int32)
    n_used = (gend[-1] // GMM_ROWS).astype(jnp.int32).reshape(1)
    cum = jnp.cumsum(nch, axis=1)
    c = jnp.arange(MAX_CHUNKS, dtype=jnp.int32)
    e_of_c = jnp.minimum(jnp.sum(c[None, :, None] >= cum[:, None, :], axis=2), N_EXPERTS - 1)
    first_c = jnp.take_along_axis(cum - nch, e_of_c, axis=1)
    dstc = jnp.take_along_axis(dst, e_of_c, axis=1) + (c[None, :] - first_c) * CHUNK
    i32 = lambda a: a.astype(jnp.int32)
    fill_start = jnp.concatenate([gbase + n_e, gend[-1:]])
    fill_n = jnp.concatenate([reg - n_e, _gmm_blocks(T) * GMM_ROWS - gend[-1:]]) // CHUNK
    return dict(dstc=i32(dstc).reshape(-1), ntot=i32(cum[:, -1]), fill_start=i32(fill_start), fill_n=i32(fill_n),
                fill_tot=i32(jnp.sum(fill_n)).reshape(1), blk_expert=blk_expert, n_used=n_used)


def _chunk_copy(src_ref, src_row, dst_ref, dst_row, sem):
    return pltpu.make_async_copy(
        src_ref.at[pl.ds(pl.multiple_of(src_row, CHUNK), CHUNK), :],
        dst_ref.at[pl.ds(pl.multiple_of(dst_row, CHUNK), CHUNK), :], sem)


def _wait_chunks(n, src_ref, dst_ref, sem):
    def wait_rows(rows):
        def body(c, carry):
            pltpu.make_async_copy(src_ref.at[pl.ds(0, rows), :], dst_ref.at[pl.ds(0, rows), :], sem).wait()
            return carry
        return body

    lax.fori_loop(0, n // WAIT_GROUP, wait_rows(WAIT_GROUP * CHUNK), 0)
    lax.fori_loop(0, n % WAIT_GROUP, wait_rows(CHUNK), 0)


def _dispatch_kernel(dstc_ref, ntot_ref, fstart_ref, fn_ref, ftot_ref,
                     lp_ref, x_ref, xs_hbm, xt, zbuf, sems):
    t = pl.program_id(0)
    nt = pl.num_programs(0)
    slot = t % 2
    xt_s = xt.at[slot]
    sem = sems.at[slot]

    @pl.when(t >= 2)
    def _():
        _wait_chunks(ntot_ref[t - 2], xt_s, xs_hbm, sem)

    lp = lp_ref[0].astype(jnp.int32)
    r = lax.broadcasted_iota(jnp.int32, (SORT_ROWS, MOE_TILE), 0)
    hit = jnp.logical_or(lp[0:1, :] == r, lp[1:2, :] == r)
    p = jnp.where(hit, 1.0, 0.0).astype(BF16)
    xt_s[...] = jnp.dot(p, x_ref[...], preferred_element_type=F32).astype(BF16)

    def issue(c, carry):
        _chunk_copy(xt_s, c * CHUNK, xs_hbm, dstc_ref[t * MAX_CHUNKS + c], sem).start()
        return carry

    lax.fori_loop(0, ntot_ref[t], issue, 0)

    @pl.when(t == nt - 1)
    def _():
        zbuf[...] = jnp.zeros_like(zbuf)

        def per_range(e, carry):
            def issue(c, carry2):
                _chunk_copy(zbuf, 0, xs_hbm, fstart_ref[e] + c * CHUNK, sem).start()
                return carry2

            return lax.fori_loop(0, fn_ref[e], issue, carry)

        lax.fori_loop(0, N_EXPERTS + 1, per_range, 0)
        _wait_chunks(ntot_ref[t - 1], xt.at[1 - slot], xs_hbm, sems.at[1 - slot])
        _wait_chunks(ntot_ref[t] + ftot_ref[0], xt_s, xs_hbm, sem)


def _dispatch(plan, lpt, x2b):
    T, D = x2b.shape
    nt = T // MOE_TILE
    assert nt >= 2
    grid_spec = pltpu.PrefetchScalarGridSpec(
        num_scalar_prefetch=5,
        grid=(nt,),
        in_specs=[pl.BlockSpec((1, SUBLANES, MOE_TILE), lambda t, *_: (t, 0, 0)),
                  pl.BlockSpec((MOE_TILE, D), lambda t, *_: (t, 0))],
        out_specs=pl.BlockSpec(memory_space=pl.ANY),
        scratch_shapes=[pltpu.VMEM((2, SORT_ROWS, D), BF16), pltpu.VMEM((CHUNK, D), BF16),
                        pltpu.SemaphoreType.DMA((2,))],
    )
    return pl.pallas_call(
        _dispatch_kernel,
        grid_spec=grid_spec,
        out_shape=jax.ShapeDtypeStruct((_gmm_blocks(T) * GMM_ROWS, D), BF16),
        compiler_params=pltpu.CompilerParams(
            dimension_semantics=("arbitrary",), vmem_limit_bytes=VMEM_LIMIT),
        name="moe_dispatch",
    )(plan["dstc"], plan["ntot"], plan["fill_start"], plan["fill_n"], plan["fill_tot"], lpt, x2b)


def _gmm_kernel(be_ref, nu_ref, x_ref, wg_ref, wu_ref, wd_ref, o_ref, wg_b, wu_b, wd_b):
    b = pl.program_id(0)

    @pl.when(jnp.logical_or(b == 0, be_ref[b] != be_ref[jnp.maximum(b - 1, 0)]))
    def _():
        _cast_bf16(wg_b, wg_ref.at[0])
        _cast_bf16(wu_b, wu_ref.at[0])
        _cast_bf16(wd_b, wd_ref.at[0])

    @pl.when(b < nu_ref[0])
    def _():
        xb = x_ref[...]
        g = jnp.dot(xb, wg_b[...], preferred_element_type=F32)
        u = jnp.dot(xb, wu_b[...], preferred_element_type=F32)
        h = (g * jax.nn.sigmoid(g)) * u
        o_ref[...] = jnp.dot(h.astype(BF16), wd_b[...], preferred_element_type=F32).astype(BF16)

    @pl.when(pl.program_id(0) >= nu_ref[0])
    def _():
        o_ref[...] = jnp.zeros_like(o_ref)


def _gmm(plan, xs, wg, wu, wd):
    R, D = xs.shape
    rows = pl.BlockSpec((GMM_ROWS, D), lambda b, be, nu: (jnp.minimum(b, nu[0] - 1), 0))
    grid_spec = pltpu.PrefetchScalarGridSpec(
        num_scalar_prefetch=2,
        grid=(R // GMM_ROWS,),
        in_specs=[rows,
                  pl.BlockSpec((1, D, D_EXPERT), lambda b, be, nu: (be[b], 0, 0)),
                  pl.BlockSpec((1, D, D_EXPERT), lambda b, be, nu: (be[b], 0, 0)),
                  pl.BlockSpec((1, D_EXPERT, D), lambda b, be, nu: (be[b], 0, 0))],
        out_specs=pl.BlockSpec((GMM_ROWS, D), lambda b, be, nu: (b, 0)),
        scratch_shapes=[pltpu.VMEM((D, D_EXPERT), BF16), pltpu.VMEM((D, D_EXPERT), BF16),
                        pltpu.VMEM((D_EXPERT, D), BF16)],
    )
    return pl.pallas_call(
        _gmm_kernel,
        grid_spec=grid_spec,
        out_shape=jax.ShapeDtypeStruct((R, D), BF16),
        compiler_params=pltpu.CompilerParams(
            dimension_semantics=("arbitrary",), vmem_limit_bytes=VMEM_LIMIT),
        name="moe_gmm",
    )(plan["blk_expert"], plan["n_used"], xs, wg, wu, wd)


def _combine_kernel(dstc_ref, ntot_ref, cm_ref, x_ref, ys_hbm, g3_ref, b3_ref, o_ref, yt, sems):
    t = pl.program_id(0)
    nt = pl.num_programs(0)
    slot = t % 2

    def fetch(tile, s):
        def issue(c, carry):
            _chunk_copy(ys_hbm, dstc_ref[tile * MAX_CHUNKS + c], yt.at[s], c * CHUNK, sems.at[s]).start()
            return carry

        lax.fori_loop(0, ntot_ref[tile], issue, 0)

    @pl.when(t == 0)
    def _():
        yt[...] = jnp.zeros_like(yt)
        fetch(0, 0)

    @pl.when(t + 1 < nt)
    def _():
        fetch(t + 1, 1 - slot)

    cm = cm_ref[...]
    col = lax.broadcasted_iota(jnp.int32, (MOE_TILE, SORT_ROWS), 1)
    lp0 = cm[:, 0:1].astype(jnp.int32)
    lp1 = cm[:, 1:2].astype(jnp.int32)
    w = jnp.where(col == lp0, cm[:, 2:3], 0.0) + jnp.where(col == lp1, cm[:, 3:4], 0.0)
    w = w.astype(BF16)

    _wait_chunks(ntot_ref[t], ys_hbm, yt.at[slot], sems.at[slot])
    y = jnp.dot(w, yt[slot], preferred_element_type=F32)
    o_ref[...] = _layer_norm(ALPHA * x_ref[...] + y, g3_ref[...], b3_ref[...])


def _combine(plan, route, x2, ys, g3, b3):
    T, D = x2.shape
    nt = T // MOE_TILE
    grid_spec = pltpu.PrefetchScalarGridSpec(
        num_scalar_prefetch=2,
        grid=(nt,),
        in_specs=[pl.BlockSpec((MOE_TILE, LANES), lambda t, *_: (t, 0)),
                  pl.BlockSpec((MOE_TILE, D), lambda t, *_: (t, 0)),
                  pl.BlockSpec(memory_space=pl.ANY),
                  pl.BlockSpec((1, D), lambda t, *_: (0, 0)),
                  pl.BlockSpec((1, D), lambda t, *_: (0, 0))],
        out_specs=pl.BlockSpec((MOE_TILE, D), lambda t, *_: (t, 0)),
        scratch_shapes=[pltpu.VMEM((2, SORT_ROWS, D), BF16), pltpu.SemaphoreType.DMA((2,))],
    )
    return pl.pallas_call(
        _combine_kernel,
        grid_spec=grid_spec,
        out_shape=jax.ShapeDtypeStruct((T, D), F32),
        compiler_params=pltpu.CompilerParams(
            dimension_semantics=("arbitrary",), vmem_limit_bytes=VMEM_LIMIT),
        name="moe_combine",
    )(plan["dstc"], plan["ntot"], route, x2, ys, g3, b3)


def _moe(x2, x2b, route, lpt, meta, wg, wu, wd, g3, b3):
    plan = _plan(meta, x2.shape[0])
    xs = _dispatch(plan, lpt, x2b)
    ys = _gmm(plan, xs, wg, wu, wd)
    return _combine(plan, route, x2, ys, g3, b3)


def _row(v):
    return v.reshape(1, -1).astype(F32)


def kernel(x, mem, w_in, b_in, w_dw, b_dw, g_conv_norm, b_conv_norm, attn_sinks, w_out, g_ln1, b_ln1,
           w_mq, w_mkv, w_mo, g_ln2, b_ln2, w_group, b_group, w_router, b_router, w_gate, w_up, w_down,
           g_ln3, b_ln3):
    B, S, D = x.shape
    for l in range(DEPTH):
        w_dw_p = jnp.zeros((CONV_HALO, CONV_CH), F32).at[:CONV_WIDTH].set(w_dw[l])
        x1 = _mixer(x, attn_sinks[l].astype(F32), w_in[l], _row(b_in[l]), w_dw_p,
                    _row(b_dw[l]), _row(g_conv_norm[l]), _row(b_conv_norm[l]),
                    w_out[l], _row(g_ln1[l]), _row(b_ln1[l]))

        kvm = _memkv(mem.reshape(B * MEM_LEN, D), w_mkv[l]).reshape(B, MEM_LEN, 2 * D)

        wr = jnp.concatenate(
            [w_group[l], jnp.transpose(w_router[l], (1, 0, 2)).reshape(D, N_EXPERTS)], axis=1)
        wr = jnp.pad(wr, ((0, 0), (0, LANES - wr.shape[1])))
        br = jnp.pad(jnp.concatenate([b_group[l], b_router[l].reshape(-1)]), (0, LANES - N_GROUPS - N_EXPERTS))
        x2, x2b, route, lpt, meta = _memattn(x1, w_mq[l], kvm, w_mo[l],
                                 _row(g_ln2[l]), _row(b_ln2[l]), wr.astype(F32), _row(br))

        T = B * S
        y = _moe(x2.reshape(T, D), x2b.reshape(T, D), route.reshape(T, LANES), lpt, meta,
                 w_gate[l], w_up[l], w_down[l],
                 _row(g_ln3[l]), _row(b_ln3[l]))
        x = y.reshape(B, S, D)
    return x
```

```python
import functools

import jax
import jax.numpy as jnp
from jax import lax
from jax.experimental import pallas as pl
from jax.experimental.pallas import tpu as pltpu

D_MODEL = 1024
MEM_LEN = 256
CONV_CH = 512
CONV_WIDTH = 31
N_HEADS = 8
N_KV_HEADS = 2
HEAD_DIM = 64
GQ = N_HEADS // N_KV_HEADS
ATTN_W = N_HEADS * HEAD_DIM
KV_W = N_KV_HEADS * HEAD_DIM
WINDOW = 128
D_MIX = CONV_CH + ATTN_W
D_IN = 2 * CONV_CH + ATTN_W + 2 * KV_W
MEM_HEADS = 4
MEM_HEAD_DIM = D_MODEL // MEM_HEADS
N_GROUPS = 4
EXPERTS_PER_GROUP = 4
N_EXPERTS = N_GROUPS * EXPERTS_PER_GROUP
D_EXPERT = D_MODEL // 2
DEPTH = 1
ALPHA = (2.0 * DEPTH) ** 0.25
LN_EPS = 1e-5

LANES = 128
SUBLANES = 8
CONV_ROWS = 128
LN_ROWS = 64
MEM_ROWS = 512
MASK_VALUE = -1e30
CONV_HALO = 32
SEQ_TILE = 512
MOE_TILE = 512
CHUNK = 16
GMM_ROWS = 512
TOP_K = 2
SORT_ROWS = -(-(MOE_TILE * TOP_K + N_EXPERTS * (CHUNK - 1)) // 256) * 256
MAX_CHUNKS = SORT_ROWS // CHUNK
WAIT_GROUP = 8
ROUTE_OFF = N_GROUPS
VMEM_LIMIT = 56 * 1024 * 1024

BF16 = jnp.bfloat16
F32 = jnp.float32


def _layer_norm(x, g, b):
    mu = jnp.mean(x, axis=-1, keepdims=True)
    xc = x - mu
    var = jnp.mean(xc * xc, axis=-1, keepdims=True)
    return xc * lax.rsqrt(var + LN_EPS) * g + b


def _cast_bf16(dst_ref, src_ref):
    rows = 256
    for r0 in range(0, src_ref.shape[0], rows):
        dst_ref[r0:r0 + rows, :] = src_ref[r0:r0 + rows, :].astype(BF16)


def _dot_nt(a, b):
    return lax.dot_general(a, b, (((1,), (1,)), ((), ())), preferred_element_type=F32)


def _mixer_kernel(sinks_ref, x_ref, w_in_ref, b_in_ref, w_dw_ref, b_dw_ref, g_cn_ref, b_cn_ref,
                  w_out_ref, g1_ref, b1_ref, o_ref, w_in_b, w_out_b, hbuf, hshift, cbuf, qbuf, kbuf, vbuf, ymix):
    i = pl.program_id(1)
    ts = SEQ_TILE

    @pl.when(jnp.logical_and(pl.program_id(0) == 0, i == 0))
    def _():
        _cast_bf16(w_in_b, w_in_ref)
        _cast_bf16(w_out_b, w_out_ref)

    @pl.when(i == 0)
    def _():
        hbuf[0:CONV_HALO, :] = jnp.zeros((CONV_HALO, CONV_CH), F32)
        kbuf[:, 0:WINDOW, :] = jnp.zeros((2 * N_KV_HEADS, WINDOW, KV_W), BF16)
        vbuf[:, 0:WINDOW, :] = jnp.zeros((2, WINDOW, KV_W), BF16)

    x = x_ref[0]
    u = jnp.dot(x.astype(BF16), w_in_b[...], preferred_element_type=F32) + b_in_ref[...]
    a = u[:, 0:CONV_CH]
    gate = u[:, CONV_CH:2 * CONV_CH]
    hbuf[CONV_HALO:CONV_HALO + ts, :] = a * jax.nn.sigmoid(gate)
    qbuf[...] = (u[:, 2 * CONV_CH:2 * CONV_CH + ATTN_W] * (HEAD_DIM ** -0.5)).astype(BF16)
    kf = u[:, 2 * CONV_CH + ATTN_W:2 * CONV_CH + ATTN_W + KV_W]
    vf = u[:, 2 * CONV_CH + ATTN_W + KV_W:D_IN]
    kr = pltpu.roll(kf, HEAD_DIM, axis=1)
    vr = pltpu.roll(vf, HEAD_DIM, axis=1)
    lo = lax.broadcasted_iota(jnp.int32, (ts, KV_W), 1) < HEAD_DIM
    rows = slice(WINDOW, WINDOW + ts)
    kbuf[0, rows, :] = jnp.where(lo, kf, 0.0).astype(BF16)
    kbuf[1, rows, :] = jnp.where(lo, 0.0, kr).astype(BF16)
    kbuf[2, rows, :] = jnp.where(lo, kr, 0.0).astype(BF16)
    kbuf[3, rows, :] = jnp.where(lo, 0.0, kf).astype(BF16)
    vbuf[0, rows, :] = vf.astype(BF16)
    vbuf[1, rows, :] = vr.astype(BF16)

    base = CONV_HALO - (CONV_WIDTH - 1)
    n_shift = ts + CONV_HALO - SUBLANES
    for b in range(1, SUBLANES):
        hshift[b - 1, 0:n_shift, :] = hbuf[b:b + n_shift, :]
    rc = CONV_ROWS

    def conv_chunk(c):
        r0 = c * rc
        for l in range(CONV_CH // LANES):
            ls = slice(l * LANES, (l + 1) * LANES)
            acc = jnp.zeros((rc, LANES), F32)
            for j in range(CONV_WIDTH):
                a8, b = divmod(j + base, SUBLANES)
                rs = slice(r0 + SUBLANES * a8, r0 + SUBLANES * a8 + rc)
                tap = hbuf[rs, ls] if b == 0 else hshift[b - 1, rs, ls]
                acc = acc + tap * w_dw_ref[j:j + 1, ls]
            cbuf[r0:r0 + rc, ls] = acc
        for r1 in range(r0, r0 + rc, LN_ROWS):
            rs = slice(r1, r1 + LN_ROWS)
            y = _layer_norm(cbuf[rs, :] + b_dw_ref[...], g_cn_ref[...], b_cn_ref[...])
            y = y * jax.nn.sigmoid(y)
            ymix[rs, 0:CONV_CH] = y.astype(BF16)

    qi = lax.broadcasted_iota(jnp.int32, (2 * WINDOW, 2 * WINDOW), 0) % WINDOW
    kj = lax.broadcasted_iota(jnp.int32, (2 * WINDOW, 2 * WINDOW), 1)
    dist = qi + WINDOW - kj
    band = (dist >= 0) & (dist < WINDOW)
    top = lax.broadcasted_iota(jnp.int32, (2 * WINDOW, 1), 0) < WINDOW
    lo_out = lax.broadcasted_iota(jnp.int32, (WINDOW, 2 * HEAD_DIM), 1) < HEAD_DIM

    def attn_block(jb):
        r0 = jb * WINDOW
        valid = band & jnp.logical_or(i != 0, kj >= WINDOW) if jb == 0 else band
        for kvh in range(N_KV_HEADS):
            h0 = kvh * GQ
            c0 = h0 * HEAD_DIM
            qs = jnp.concatenate([qbuf[r0:r0 + WINDOW, c0:c0 + 2 * HEAD_DIM],
                                  qbuf[r0:r0 + WINDOW, c0 + 2 * HEAD_DIM:c0 + 4 * HEAD_DIM]], axis=0)
            pv = []
            for par in range(2):
                kk = kbuf[2 * kvh + par, r0:r0 + 2 * WINDOW, :]
                vv = vbuf[(kvh + par) % 2, r0:r0 + 2 * WINDOW, :]
                s = jnp.where(valid, _dot_nt(qs, kk), MASK_VALUE)
                sink = jnp.where(top, sinks_ref[h0 + par], sinks_ref[h0 + 2 + par])
                m = jnp.maximum(jnp.max(s, axis=-1, keepdims=True), sink)
                p = jnp.exp(s - m)
                denom = jnp.sum(p, axis=-1, keepdims=True) + jnp.exp(sink - m)
                pv.append(jnp.dot(p.astype(BF16), vv, preferred_element_type=F32) / denom)
            for pair in range(2):
                rs = slice(pair * WINDOW, (pair + 1) * WINDOW)
                o = jnp.where(lo_out, pv[0][rs], pv[1][rs])
                cs = CONV_CH + c0 + pair * 2 * HEAD_DIM
                ymix[r0:r0 + WINDOW, cs:cs + 2 * HEAD_DIM] = o.astype(BF16)

    assert ts // rc == ts // WINDOW
    for c in range(ts // rc):
        attn_block(c)
        conv_chunk(c)

    mix = jnp.dot(ymix[...], w_out_b[...], preferred_element_type=F32)
    o_ref[0] = _layer_norm(ALPHA * x + mix, g1_ref[...], b1_ref[...])

    hbuf[0:CONV_HALO, :] = hbuf[ts:ts + CONV_HALO, :]
    kbuf[:, 0:WINDOW, :] = kbuf[:, ts:ts + WINDOW, :]
    vbuf[:, 0:WINDOW, :] = vbuf[:, ts:ts + WINDOW, :]


def _const_spec(shape):
    nd = len(shape)
    return pl.BlockSpec(shape, lambda *_: (0,) * nd)


def _resident_spec(shape):
    nd = len(shape)
    return pl.BlockSpec(shape, lambda *_: (0,) * nd, pipeline_mode=pl.Buffered(1))


def _mixer(x, sinks, w_in, b_in, w_dw, b_dw, g_cn, b_cn, w_out, g1, b1):
    B, S, D = x.shape
    ts = SEQ_TILE
    tile = pl.BlockSpec((1, ts, D), lambda b, i: (b, i, 0))
    return pl.pallas_call(
        _mixer_kernel,
        grid=(B, S // ts),
        in_specs=[
            pl.BlockSpec(memory_space=pltpu.SMEM),
            tile,
            _resident_spec((D, D_IN)), _const_spec((1, D_IN)),
            _const_spec((CONV_HALO, CONV_CH)), _const_spec((1, CONV_CH)),
            _const_spec((1, CONV_CH)), _const_spec((1, CONV_CH)),
            _resident_spec((D_MIX, D)), _const_spec((1, D)), _const_spec((1, D)),
        ],
        out_specs=tile,
        out_shape=jax.ShapeDtypeStruct((B, S, D), F32),
        scratch_shapes=[
            pltpu.VMEM((D, D_IN), BF16),
            pltpu.VMEM((D_MIX, D), BF16),
            pltpu.VMEM((CONV_HALO + ts, CONV_CH), F32),
            pltpu.VMEM((SUBLANES - 1, CONV_HALO + ts, CONV_CH), F32),
            pltpu.VMEM((ts, CONV_CH), F32),
            pltpu.VMEM((ts, ATTN_W), BF16),
            pltpu.VMEM((2 * N_KV_HEADS, WINDOW + ts, KV_W), BF16),
            pltpu.VMEM((2, WINDOW + ts, KV_W), BF16),
            pltpu.VMEM((ts, D_MIX), BF16),
        ],
        compiler_params=pltpu.CompilerParams(
            dimension_semantics=("arbitrary", "arbitrary"), vmem_limit_bytes=VMEM_LIMIT),
        name="mixer",
    )(sinks, x, w_in, b_in, w_dw, b_dw, g_cn, b_cn, w_out, g1, b1)


def _memkv_kernel(mem_ref, w_ref, o_ref):
    o_ref[...] = jnp.dot(mem_ref[...].astype(BF16), w_ref[...].astype(BF16),
                         preferred_element_type=F32).astype(BF16)


def _memkv(mem2d, w_mkv):
    M, D = mem2d.shape
    N = w_mkv.shape[1]
    tn = 512
    return pl.pallas_call(
        _memkv_kernel,
        grid=(N // tn,),
        in_specs=[pl.BlockSpec((M, D), lambda j: (0, 0)), pl.BlockSpec((D, tn), lambda j: (0, j))],
        out_specs=pl.BlockSpec((M, tn), lambda j: (0, j)),
        out_shape=jax.ShapeDtypeStruct((M, N), BF16),
        compiler_params=pltpu.CompilerParams(dimension_semantics=("arbitrary",)),
        name="memkv",
    )(mem2d, w_mkv)


def _first_max(rows):
    best = rows[0]
    for r in rows[1:]:
        best = jnp.maximum(best, r)
    idx = jnp.full(best.shape, len(rows) - 1, jnp.int32)
    for k in range(len(rows) - 2, -1, -1):
        idx = jnp.where(rows[k] == best, k, idx)
    return best, idx


def _route_plan(logits_t):
    tile = logits_t.shape[1]
    row = lambda k: logits_t[k:k + 1, :]
    gmax, g_idx = _first_max([row(g) for g in range(N_GROUPS)])
    gsum = jnp.exp(row(0) - gmax)
    for g in range(1, N_GROUPS):
        gsum = gsum + jnp.exp(row(g) - gmax)
    g_p = 1.0 / gsum
    rl = []
    for e in range(EXPERTS_PER_GROUP):
        v = row(ROUTE_OFF + (N_GROUPS - 1) * EXPERTS_PER_GROUP + e)
        for g in range(N_GROUPS - 2, -1, -1):
            v = jnp.where(g_idx == g, row(ROUTE_OFF + g * EXPERTS_PER_GROUP + e), v)
        rl.append(v)
    m1, i1 = _first_max(rl)
    m2, i2 = _first_max([jnp.where(i1 == e, MASK_VALUE, rl[e]) for e in range(EXPERTS_PER_GROUP)])
    ex = jnp.exp(m2 - m1)
    w1 = 1.0 / (1.0 + ex)
    w2 = ex * w1
    e1 = g_idx * EXPERTS_PER_GROUP + i1
    e2 = g_idx * EXPERTS_PER_GROUP + i2

    eid = lax.broadcasted_iota(jnp.int32, (N_EXPERTS, tile), 0)
    hit1 = eid == e1
    hit2 = eid == e2
    oh = jnp.where(jnp.logical_or(hit1, hit2), 1.0, 0.0)
    r = lax.broadcasted_iota(jnp.int32, (tile, tile), 0)
    c = lax.broadcasted_iota(jnp.int32, (tile, tile), 1)
    tri = jnp.where(r <= c, 1.0, 0.0).astype(BF16)
    csum = jnp.dot(oh.astype(BF16), tri, preferred_element_type=F32)
    counts = jnp.broadcast_to(csum[:, tile - 1:tile], (N_EXPERTS, tile)).astype(jnp.int32)
    nch = jnp.right_shift(counts + (CHUNK - 1), CHUNK.bit_length() - 1)
    er = lax.broadcasted_iota(jnp.int32, (N_EXPERTS, N_EXPERTS), 0)
    ec = lax.broadcasted_iota(jnp.int32, (N_EXPERTS, N_EXPERTS), 1)
    lower = jnp.where(ec < er, 1.0, 0.0).astype(BF16)
    off = jnp.dot(lower, nch.astype(F32).astype(BF16), preferred_element_type=F32) * CHUNK
    pos = off + csum - oh
    lp1 = jnp.sum(jnp.where(hit1, pos, 0.0), axis=0, keepdims=True)
    lp2 = jnp.sum(jnp.where(hit2, pos, 0.0), axis=0, keepdims=True)
    zero = jnp.zeros_like(lp1)
    route_t = jnp.concatenate([lp1, lp2, g_p * w1, g_p * w2, zero, zero, zero, zero], axis=0)
    meta = jnp.concatenate([nch[:, 0:LANES], off[:, 0:LANES].astype(jnp.int32)], axis=0)
    return route_t, meta


def _memattn_kernel(x_ref, wq_ref, k_ref, v_ref, wo_ref, g2_ref, b2_ref, wr2_ref, wrh_ref, br_ref,
                    o_ref, ob_ref, route_ref, routet_ref, meta_ref, wq_b, wo_b):
    @pl.when(jnp.logical_and(pl.program_id(0) == 0, pl.program_id(1) == 0))
    def _():
        _cast_bf16(wq_b, wq_ref)
        _cast_bf16(wo_b, wo_ref)

    def rows_logits(rs):
        x = x_ref[0, rs, :]
        q = jnp.dot(x.astype(BF16), wq_b[...], preferred_element_type=F32)
        q = (q * (MEM_HEAD_DIM ** -0.5)).astype(BF16)
        outs = []
        for h in range(MEM_HEADS):
            sl = slice(h * MEM_HEAD_DIM, (h + 1) * MEM_HEAD_DIM)
            s = _dot_nt(q[:, sl], k_ref[0, :, sl])
            m = jnp.max(s, axis=-1, keepdims=True)
            p = jnp.exp(s - m)
            denom = jnp.sum(p, axis=-1, keepdims=True)
            o = jnp.dot(p.astype(BF16), v_ref[0, :, sl], preferred_element_type=F32)
            outs.append((o / denom).astype(BF16))
        o = jnp.dot(jnp.concatenate(outs, axis=-1), wo_b[...], preferred_element_type=F32)
        x2 = _layer_norm(ALPHA * x + o, g2_ref[...], b2_ref[...])
        o_ref[0, rs, :] = x2
        x2h = x2.astype(BF16)
        ob_ref[0, rs, :] = x2h
        x2l = (x2 - x2h.astype(F32)).astype(BF16)
        hh = jnp.dot(x2h, wr2_ref[...], preferred_element_type=F32)
        return (hh[:, 0:LANES] + hh[:, LANES:2 * LANES]
                + jnp.dot(x2l, wrh_ref[...], preferred_element_type=F32) + br_ref[...])

    ts = x_ref.shape[1]
    logits = jnp.concatenate(
        [rows_logits(slice(r0, r0 + MEM_ROWS)) for r0 in range(0, ts, MEM_ROWS)], axis=0)
    route_t, meta = _route_plan(jnp.transpose(logits))
    routet_ref[0] = route_t
    pad = jnp.zeros((LANES - SUBLANES, ts), F32)
    route_ref[0] = jnp.transpose(jnp.concatenate([route_t, pad], axis=0))
    meta_ref[0] = meta


def _memattn(x1, wq, kvm, wo, g2, b2, wr, br):
    B, S, D = x1.shape
    ts = SEQ_TILE
    assert ts == MOE_TILE
    nt = S // ts
    c = wr * (2.0 ** 16 + 1.0)
    w_high = c - (c - wr)
    wrh = w_high.astype(BF16)
    wr2 = jnp.concatenate([wrh, (wr - w_high).astype(BF16)], axis=1)
    tile = pl.BlockSpec((1, ts, D), lambda b, i: (b, i, 0))
    kspec = pl.BlockSpec((1, MEM_LEN, D), lambda b, i: (b, 0, 0))
    vspec = pl.BlockSpec((1, MEM_LEN, D), lambda b, i: (b, 0, 1))
    return pl.pallas_call(
        _memattn_kernel,
        grid=(B, nt),
        in_specs=[tile, _resident_spec((D, D)), kspec, vspec, _resident_spec((D, D)),
                  _const_spec((1, D)), _const_spec((1, D)),
                  _const_spec((D, 2 * LANES)), _const_spec((D, LANES)), _const_spec((1, LANES))],
        out_specs=[tile, tile, pl.BlockSpec((1, ts, LANES), lambda b, i: (b, i, 0)),
                   pl.BlockSpec((1, SUBLANES, ts), lambda b, i: (b * nt + i, 0, 0)),
                   pl.BlockSpec((1, 2 * N_EXPERTS, LANES), lambda b, i: (b * nt + i, 0, 0))],
        out_shape=[jax.ShapeDtypeStruct((B, S, D), F32),
                   jax.ShapeDtypeStruct((B, S, D), BF16),
                   jax.ShapeDtypeStruct((B, S, LANES), F32),
                   jax.ShapeDtypeStruct((B * nt, SUBLANES, ts), F32),
                   jax.ShapeDtypeStruct((B * nt, 2 * N_EXPERTS, LANES), jnp.int32)],
        scratch_shapes=[pltpu.VMEM((D, D), BF16), pltpu.VMEM((D, D), BF16)],
        compiler_params=pltpu.CompilerParams(
            dimension_semantics=("arbitrary", "arbitrary"), vmem_limit_bytes=VMEM_LIMIT),
        name="memattn",
    )(x1, wq, kvm, kvm, wo, g2, b2, wr2, wrh, br)


def _gmm_blocks(n_tokens):
    rows = (n_tokens * TOP_K + (n_tokens // MOE_TILE) * N_EXPERTS * (CHUNK - 1)
            + N_EXPERTS * (GMM_ROWS - CHUNK))
    return -(-rows // GMM_ROWS)


def _plan(meta, T):
    nch = meta[:, :N_EXPERTS, 0]
    n16 = nch * CHUNK
    n_e = jnp.sum(n16, axis=0)
    reg = (n_e + GMM_ROWS - 1) // GMM_ROWS * GMM_ROWS
    gend = jnp.cumsum(reg)
    gbase = gend - reg
    dst = gbase[None, :] + jnp.cumsum(n16, axis=0) - n16
    blk_row = jnp.arange(_gmm_blocks(T), dtype=jnp.int32)[:, None] * GMM_ROWS
    blk_expert = jnp.sum(blk_row >= gend[None, :], axis=1)
    blk_expert = jnp.minimum(blk_expert, N_EXPERTS - 1).astype(jnp.int32)
    n_used = (gend[-1] // GMM_ROWS).astype(jnp.int32).reshape(1)
    data_end = jnp.sum(jnp.where(blk_expert[:, None] == jnp.arange(N_EXPERTS)[None, :],
                                 (gbase + n_e)[None, :], 0), axis=1)
    blk_rows = jnp.where(blk_row[:, 0] < gend[-1], jnp.clip(data_end - blk_row[:, 0], 0, GMM_ROWS), 0)
    cum = jnp.cumsum(nch, axis=1)
    c = jnp.arange(MAX_CHUNKS, dtype=jnp.int32)
    first_c = (cum - nch)[:, None, :]
    in_seg = (c[None, :, None] >= first_c) & (c[None, :, None] < cum[:, None, :])
    dstc = jnp.sum(jnp.where(in_seg, dst[:, None, :] + (c[None, :, None] - first_c) * CHUNK, 0), axis=2)
    i32 = lambda a: a.astype(jnp.int32)
    fill_start = jnp.concatenate([gbase + n_e, gend[-1:]])
    fill_n = jnp.concatenate([reg - n_e, _gmm_blocks(T) * GMM_ROWS - gend[-1:]]) // CHUNK
    return dict(dstc=i32(dstc).reshape(-1), ntot=i32(cum[:, -1]), fill_start=i32(fill_start), fill_n=i32(fill_n),
                fill_tot=i32(jnp.sum(fill_n)).reshape(1), blk_expert=blk_expert, n_used=n_used,
                blk_rows=i32(blk_rows))


def _chunk_copy(src_ref, src_row, dst_ref, dst_row, sem):
    return pltpu.make_async_copy(
        src_ref.at[pl.ds(pl.multiple_of(src_row, CHUNK), CHUNK), :],
        dst_ref.at[pl.ds(pl.multiple_of(dst_row, CHUNK), CHUNK), :], sem)


def _wait_chunks(n, src_ref, dst_ref, sem):
    def wait_rows(rows):
        def body(c, carry):
            pltpu.make_async_copy(src_ref.at[pl.ds(0, rows), :], dst_ref.at[pl.ds(0, rows), :], sem).wait()
            return carry
        return body

    lax.fori_loop(0, n // WAIT_GROUP, wait_rows(WAIT_GROUP * CHUNK), 0)
    lax.fori_loop(0, n % WAIT_GROUP, wait_rows(CHUNK), 0)


def _dispatch_kernel(dstc_ref, ntot_ref, fstart_ref, fn_ref, ftot_ref,
                     lp_ref, x_ref, xs_hbm, xt, zbuf, sems):
    t = pl.program_id(0)
    nt = pl.num_programs(0)
    slot = t % 2
    xt_s = xt.at[slot]
    sem = sems.at[slot]

    @pl.when(t >= 2)
    def _():
        _wait_chunks(ntot_ref[t - 2], xt_s, xs_hbm, sem)

    lp = lp_ref[0].astype(jnp.int32)
    r = lax.broadcasted_iota(jnp.int32, (SORT_ROWS, MOE_TILE), 0)
    hit = jnp.logical_or(lp[0:1, :] == r, lp[1:2, :] == r)
    p = jnp.where(hit, 1.0, 0.0).astype(BF16)
    xt_s[...] = jnp.dot(p, x_ref[...], preferred_element_type=F32).astype(BF16)

    def issue(c, carry):
        _chunk_copy(xt_s, c * CHUNK, xs_hbm, dstc_ref[t * MAX_CHUNKS + c], sem).start()
        return carry

    lax.fori_loop(0, ntot_ref[t], issue, 0)

    @pl.when(t == nt - 1)
    def _():
        zbuf[...] = jnp.zeros_like(zbuf)

        def per_range(e, carry):
            def issue(c, carry2):
                _chunk_copy(zbuf, 0, xs_hbm, fstart_ref[e] + c * CHUNK, sem).start()
                return carry2

            return lax.fori_loop(0, fn_ref[e], issue, carry)

        lax.fori_loop(0, N_EXPERTS + 1, per_range, 0)
        _wait_chunks(ntot_ref[t - 1], xt.at[1 - slot], xs_hbm, sems.at[1 - slot])
        _wait_chunks(ntot_ref[t] + ftot_ref[0], xt_s, xs_hbm, sem)


def _dispatch(plan, lpt, x2b):
    T, D = x2b.shape
    nt = T // MOE_TILE
    assert nt >= 2
    grid_spec = pltpu.PrefetchScalarGridSpec(
        num_scalar_prefetch=5,
        grid=(nt,),
        in_specs=[pl.BlockSpec((1, SUBLANES, MOE_TILE), lambda t, *_: (t, 0, 0)),
                  pl.BlockSpec((MOE_TILE, D), lambda t, *_: (t, 0))],
        out_specs=pl.BlockSpec(memory_space=pl.ANY),
        scratch_shapes=[pltpu.VMEM((2, SORT_ROWS, D), BF16), pltpu.VMEM((CHUNK, D), BF16),
                        pltpu.SemaphoreType.DMA((2,))],
    )
    return pl.pallas_call(
        _dispatch_kernel,
        grid_spec=grid_spec,
        out_shape=jax.ShapeDtypeStruct((_gmm_blocks(T) * GMM_ROWS, D), BF16),
        compiler_params=pltpu.CompilerParams(
            dimension_semantics=("arbitrary",), vmem_limit_bytes=VMEM_LIMIT),
        name="moe_dispatch",
    )(plan["dstc"], plan["ntot"], plan["fill_start"], plan["fill_n"], plan["fill_tot"], lpt, x2b)


def _expert_ffn(xb, wg_b, wu_b, wd_b):
    g = jnp.dot(xb, wg_b[...], preferred_element_type=F32)
    u = jnp.dot(xb, wu_b[...], preferred_element_type=F32)
    h = (g * jax.nn.sigmoid(g)) * u
    return jnp.dot(h.astype(BF16), wd_b[...], preferred_element_type=F32).astype(BF16)


def _gmm_kernel(be_ref, nu_ref, br_ref, x_ref, wg_ref, wu_ref, wd_ref, o_ref, wg_b, wu_b, wd_b):
    b = pl.program_id(0)
    rows = br_ref[b]
    half = GMM_ROWS // 2

    @pl.when(jnp.logical_or(b == 0, be_ref[b] != be_ref[jnp.maximum(b - 1, 0)]))
    def _():
        _cast_bf16(wg_b, wg_ref.at[0])
        _cast_bf16(wu_b, wu_ref.at[0])
        _cast_bf16(wd_b, wd_ref.at[0])

    @pl.when(rows > half)
    def _():
        o_ref[...] = _expert_ffn(x_ref[...], wg_b, wu_b, wd_b)

    @pl.when(jnp.logical_and(rows > 0, rows <= half))
    def _():
        o_ref[0:half, :] = _expert_ffn(x_ref[0:half, :], wg_b, wu_b, wd_b)
        o_ref[half:GMM_ROWS, :] = jnp.zeros((GMM_ROWS - half, o_ref.shape[1]), o_ref.dtype)

    @pl.when(rows == 0)
    def _():
        o_ref[...] = jnp.zeros_like(o_ref)


def _gmm(plan, xs, wg, wu, wd):
    R, D = xs.shape
    rows = pl.BlockSpec((GMM_ROWS, D), lambda b, be, nu, br: (jnp.minimum(b, nu[0] - 1), 0))
    grid_spec = pltpu.PrefetchScalarGridSpec(
        num_scalar_prefetch=3,
        grid=(R // GMM_ROWS,),
        in_specs=[rows,
                  pl.BlockSpec((1, D, D_EXPERT), lambda b, be, nu, br: (be[b], 0, 0)),
                  pl.BlockSpec((1, D, D_EXPERT), lambda b, be, nu, br: (be[b], 0, 0)),
                  pl.BlockSpec((1, D_EXPERT, D), lambda b, be, nu, br: (be[b], 0, 0))],
        out_specs=pl.BlockSpec((GMM_ROWS, D), lambda b, be, nu, br: (b, 0)),
        scratch_shapes=[pltpu.VMEM((D, D_EXPERT), BF16), pltpu.VMEM((D, D_EXPERT), BF16),
                        pltpu.VMEM((D_EXPERT, D), BF16)],
    )
    return pl.pallas_call(
        _gmm_kernel,
        grid_spec=grid_spec,
        out_shape=jax.ShapeDtypeStruct((R, D), BF16),
        compiler_params=pltpu.CompilerParams(
            dimension_semantics=("arbitrary",), vmem_limit_bytes=VMEM_LIMIT),
        name="moe_gmm",
    )(plan["blk_expert"], plan["n_used"], plan["blk_rows"], xs, wg, wu, wd)


def _combine_kernel(dstc_ref, ntot_ref, cm_ref, x_ref, ys_hbm, g3_ref, b3_ref, o_ref, yt, sems):
    t = pl.program_id(0)
    nt = pl.num_programs(0)
    slot = t % 2

    def fetch(tile, s):
        def issue(c, carry):
            _chunk_copy(ys_hbm, dstc_ref[tile * MAX_CHUNKS + c], yt.at[s], c * CHUNK, sems.at[s]).start()
            return carry

        lax.fori_loop(0, ntot_ref[tile], issue, 0)

    @pl.when(t == 0)
    def _():
        yt[...] = jnp.zeros_like(yt)
        fetch(0, 0)

    @pl.when(t + 1 < nt)
    def _():
        fetch(t + 1, 1 - slot)

    cm = cm_ref[...]
    col = lax.broadcasted_iota(jnp.int32, (MOE_TILE, SORT_ROWS), 1)
    lp0 = cm[:, 0:1].astype(jnp.int32)
    lp1 = cm[:, 1:2].astype(jnp.int32)
    w = jnp.where(col == lp0, cm[:, 2:3], 0.0) + jnp.where(col == lp1, cm[:, 3:4], 0.0)
    w = w.astype(BF16)

    _wait_chunks(ntot_ref[t], ys_hbm, yt.at[slot], sems.at[slot])
    y = jnp.dot(w, yt[slot], preferred_element_type=F32)
    o_ref[...] = _layer_norm(ALPHA * x_ref[...] + y, g3_ref[...], b3_ref[...])


def _combine(plan, route, x2, ys, g3, b3):
    T, D = x2.shape
    nt = T // MOE_TILE
    grid_spec = pltpu.PrefetchScalarGridSpec(
        num_scalar_prefetch=2,
        grid=(nt,),
        in_specs=[pl.BlockSpec((MOE_TILE, LANES), lambda t, *_: (t, 0)),
                  pl.BlockSpec((MOE_TILE, D), lambda t, *_: (t, 0)),
                  pl.BlockSpec(memory_space=pl.ANY),
                  pl.BlockSpec((1, D), lambda t, *_: (0, 0)),
                  pl.BlockSpec((1, D), lambda t, *_: (0, 0))],
        out_specs=pl.BlockSpec((MOE_TILE, D), lambda t, *_: (t, 0)),
        scratch_shapes=[pltpu.VMEM((2, SORT_ROWS, D), BF16), pltpu.SemaphoreType.DMA((2,))],
    )
    return pl.pallas_call(
        _combine_kernel,
        grid_spec=grid_spec,
        out_shape=jax.ShapeDtypeStruct((T, D), F32),
        compiler_params=pltpu.CompilerParams(
            dimension_semantics=("arbitrary",), vmem_limit_bytes=VMEM_LIMIT),
        name="moe_combine",
    )(plan["dstc"], plan["ntot"], route, x2, ys, g3, b3)


def _moe(x2, x2b, route, lpt, meta, wg, wu, wd, g3, b3):
    plan = _plan(meta, x2.shape[0])
    xs = _dispatch(plan, lpt, x2b)
    ys = _gmm(plan, xs, wg, wu, wd)
    return _combine(plan, route, x2, ys, g3, b3)


def _row(v):
    return v.reshape(1, -1).astype(F32)


def kernel(x, mem, w_in, b_in, w_dw, b_dw, g_conv_norm, b_conv_norm, attn_sinks, w_out, g_ln1, b_ln1,
           w_mq, w_mkv, w_mo, g_ln2, b_ln2, w_group, b_group, w_router, b_router, w_gate, w_up, w_down,
           g_ln3, b_ln3):
    B, S, D = x.shape
    for l in range(DEPTH):
        w_dw_p = jnp.zeros((CONV_HALO, CONV_CH), F32).at[:CONV_WIDTH].set(w_dw[l])
        x1 = _mixer(x, attn_sinks[l].astype(F32), w_in[l], _row(b_in[l]), w_dw_p,
                    _row(b_dw[l]), _row(g_conv_norm[l]), _row(b_conv_norm[l]),
                    w_out[l], _row(g_ln1[l]), _row(b_ln1[l]))

        kvm = _memkv(mem.reshape(B * MEM_LEN, D), w_mkv[l]).reshape(B, MEM_LEN, 2 * D)

        wr = jnp.concatenate(
            [w_group[l], jnp.transpose(w_router[l], (1, 0, 2)).reshape(D, N_EXPERTS)], axis=1)
        wr = jnp.pad(wr, ((0, 0), (0, LANES - wr.shape[1])))
        br = jnp.pad(jnp.concatenate([b_group[l], b_router[l].reshape(-1)]), (0, LANES - N_GROUPS - N_EXPERTS))
        x2, x2b, route, lpt, meta = _memattn(x1, w_mq[l], kvm, w_mo[l],
                                 _row(g_ln2[l]), _row(b_ln2[l]), wr.astype(F32), _row(br))

        T = B * S
        y = _moe(x2.reshape(T, D), x2b.reshape(T, D), route.reshape(T, LANES), lpt, meta,
                 w_gate[l], w_up[l], w_down[l],
                 _row(g_ln3[l]), _row(b_ln3[l]))
        x = y.reshape(B, S, D)
    return x
```

```python
import functools

import jax
import jax.numpy as jnp
from jax import lax
from jax.experimental import pallas as pl
from jax.experimental.pallas import tpu as pltpu

D_MODEL = 1024
MEM_LEN = 256
CONV_CH = 512
CONV_WIDTH = 31
N_HEADS = 8
N_KV_HEADS = 2
HEAD_DIM = 64
GQ = N_HEADS // N_KV_HEADS
ATTN_W = N_HEADS * HEAD_DIM
KV_W = N_KV_HEADS * HEAD_DIM
WINDOW = 128
D_MIX = CONV_CH + ATTN_W
D_IN = 2 * CONV_CH + ATTN_W + 2 * KV_W
MEM_HEADS = 4
MEM_HEAD_DIM = D_MODEL // MEM_HEADS
N_GROUPS = 4
EXPERTS_PER_GROUP = 4
N_EXPERTS = N_GROUPS * EXPERTS_PER_GROUP
D_EXPERT = D_MODEL // 2
DEPTH = 1
ALPHA = (2.0 * DEPTH) ** 0.25
LN_EPS = 1e-5

LANES = 128
SUBLANES = 8
CONV_ROWS = 128
LN_ROWS = 64
MEM_ROWS = 512
MASK_VALUE = -1e30
CONV_HALO = 32
SEQ_TILE = 512
MOE_TILE = 512
CHUNK = 16
GMM_ROWS = 512
TOP_K = 2
SORT_ROWS = -(-(MOE_TILE * TOP_K + N_EXPERTS * (CHUNK - 1)) // 256) * 256
BIG = 4
MAX_BIG = SORT_ROWS // (BIG * CHUNK)
MAX_SMALL = N_EXPERTS * (BIG - 1)
WAIT_GROUP = 8
ROUTE_OFF = N_GROUPS
VMEM_LIMIT = 56 * 1024 * 1024

BF16 = jnp.bfloat16
F32 = jnp.float32


def _layer_norm(x, g, b):
    mu = jnp.mean(x, axis=-1, keepdims=True)
    xc = x - mu
    var = jnp.mean(xc * xc, axis=-1, keepdims=True)
    return xc * lax.rsqrt(var + LN_EPS) * g + b


def _cast_bf16(dst_ref, src_ref):
    rows = 256
    for r0 in range(0, src_ref.shape[0], rows):
        dst_ref[r0:r0 + rows, :] = src_ref[r0:r0 + rows, :].astype(BF16)


def _dot_nt(a, b):
    return lax.dot_general(a, b, (((1,), (1,)), ((), ())), preferred_element_type=F32)


def _mixer_kernel(sinks_ref, x_ref, w_in_ref, b_in_ref, w_dw_ref, b_dw_ref, g_cn_ref, b_cn_ref,
                  w_out_ref, g1_ref, b1_ref, o_ref, w_in_b, w_out_b, hbuf, hshift, cbuf, qbuf, kbuf, vbuf, ymix):
    i = pl.program_id(1)
    ts = SEQ_TILE

    @pl.when(jnp.logical_and(pl.program_id(0) == 0, i == 0))
    def _():
        _cast_bf16(w_in_b, w_in_ref)
        _cast_bf16(w_out_b, w_out_ref)

    @pl.when(i == 0)
    def _():
        hbuf[0:CONV_HALO, :] = jnp.zeros((CONV_HALO, CONV_CH), F32)
        kbuf[:, 0:WINDOW, :] = jnp.zeros((2 * N_KV_HEADS, WINDOW, KV_W), BF16)
        vbuf[:, 0:WINDOW, :] = jnp.zeros((2, WINDOW, KV_W), BF16)

    x = x_ref[0]
    u = jnp.dot(x.astype(BF16), w_in_b[...], preferred_element_type=F32) + b_in_ref[...]
    a = u[:, 0:CONV_CH]
    gate = u[:, CONV_CH:2 * CONV_CH]
    hbuf[CONV_HALO:CONV_HALO + ts, :] = a * jax.nn.sigmoid(gate)
    qbuf[...] = (u[:, 2 * CONV_CH:2 * CONV_CH + ATTN_W] * (HEAD_DIM ** -0.5)).astype(BF16)
    kf = u[:, 2 * CONV_CH + ATTN_W:2 * CONV_CH + ATTN_W + KV_W]
    vf = u[:, 2 * CONV_CH + ATTN_W + KV_W:D_IN]
    kr = pltpu.roll(kf, HEAD_DIM, axis=1)
    vr = pltpu.roll(vf, HEAD_DIM, axis=1)
    lo = lax.broadcasted_iota(jnp.int32, (ts, KV_W), 1) < HEAD_DIM
    rows = slice(WINDOW, WINDOW + ts)
    kbuf[0, rows, :] = jnp.where(lo, kf, 0.0).astype(BF16)
    kbuf[1, rows, :] = jnp.where(lo, 0.0, kr).astype(BF16)
    kbuf[2, rows, :] = jnp.where(lo, kr, 0.0).astype(BF16)
    kbuf[3, rows, :] = jnp.where(lo, 0.0, kf).astype(BF16)
    vbuf[0, rows, :] = vf.astype(BF16)
    vbuf[1, rows, :] = vr.astype(BF16)

    base = CONV_HALO - (CONV_WIDTH - 1)
    n_shift = ts + CONV_HALO - SUBLANES
    for b in range(1, SUBLANES):
        hshift[b - 1, 0:n_shift, :] = hbuf[b:b + n_shift, :]
    rc = CONV_ROWS

    def conv_chunk(c):
        r0 = c * rc
        for l in range(CONV_CH // LANES):
            ls = slice(l * LANES, (l + 1) * LANES)
            acc = jnp.zeros((rc, LANES), F32)
            for j in range(CONV_WIDTH):
                a8, b = divmod(j + base, SUBLANES)
                rs = slice(r0 + SUBLANES * a8, r0 + SUBLANES * a8 + rc)
                tap = hbuf[rs, ls] if b == 0 else hshift[b - 1, rs, ls]
                acc = acc + tap * w_dw_ref[j:j + 1, ls]
            cbuf[r0:r0 + rc, ls] = acc
        for r1 in range(r0, r0 + rc, LN_ROWS):
            rs = slice(r1, r1 + LN_ROWS)
            y = _layer_norm(cbuf[rs, :] + b_dw_ref[...], g_cn_ref[...], b_cn_ref[...])
            y = y * jax.nn.sigmoid(y)
            ymix[rs, 0:CONV_CH] = y.astype(BF16)

    qi = lax.broadcasted_iota(jnp.int32, (2 * WINDOW, 2 * WINDOW), 0) % WINDOW
    kj = lax.broadcasted_iota(jnp.int32, (2 * WINDOW, 2 * WINDOW), 1)
    dist = qi + WINDOW - kj
    band = (dist >= 0) & (dist < WINDOW)
    top = lax.broadcasted_iota(jnp.int32, (2 * WINDOW, 1), 0) < WINDOW
    lo_out = lax.broadcasted_iota(jnp.int32, (WINDOW, 2 * HEAD_DIM), 1) < HEAD_DIM

    def attn_block(jb):
        r0 = jb * WINDOW
        valid = band & jnp.logical_or(i != 0, kj >= WINDOW) if jb == 0 else band
        for kvh in range(N_KV_HEADS):
            h0 = kvh * GQ
            c0 = h0 * HEAD_DIM
            qs = jnp.concatenate([qbuf[r0:r0 + WINDOW, c0:c0 + 2 * HEAD_DIM],
                                  qbuf[r0:r0 + WINDOW, c0 + 2 * HEAD_DIM:c0 + 4 * HEAD_DIM]], axis=0)
            pv = []
            for par in range(2):
                kk = kbuf[2 * kvh + par, r0:r0 + 2 * WINDOW, :]
                vv = vbuf[(kvh + par) % 2, r0:r0 + 2 * WINDOW, :]
                s = jnp.where(valid, _dot_nt(qs, kk), MASK_VALUE)
                sink = jnp.where(top, sinks_ref[h0 + par], sinks_ref[h0 + 2 + par])
                m = jnp.maximum(jnp.max(s, axis=-1, keepdims=True), sink)
                p = jnp.exp(s - m)
                denom = jnp.sum(p, axis=-1, keepdims=True) + jnp.exp(sink - m)
                pv.append(jnp.dot(p.astype(BF16), vv, preferred_element_type=F32) / denom)
            for pair in range(2):
                rs = slice(pair * WINDOW, (pair + 1) * WINDOW)
                o = jnp.where(lo_out, pv[0][rs], pv[1][rs])
                cs = CONV_CH + c0 + pair * 2 * HEAD_DIM
                ymix[r0:r0 + WINDOW, cs:cs + 2 * HEAD_DIM] = o.astype(BF16)

    assert ts // rc == ts // WINDOW
    for c in range(ts // rc):
        attn_block(c)
        conv_chunk(c)

    mix = jnp.dot(ymix[...], w_out_b[...], preferred_element_type=F32)
    o_ref[0] = _layer_norm(ALPHA * x + mix, g1_ref[...], b1_ref[...])

    hbuf[0:CONV_HALO, :] = hbuf[ts:ts + CONV_HALO, :]
    kbuf[:, 0:WINDOW, :] = kbuf[:, ts:ts + WINDOW, :]
    vbuf[:, 0:WINDOW, :] = vbuf[:, ts:ts + WINDOW, :]


def _const_spec(shape):
    nd = len(shape)
    return pl.BlockSpec(shape, lambda *_: (0,) * nd)


def _resident_spec(shape):
    nd = len(shape)
    return pl.BlockSpec(shape, lambda *_: (0,) * nd, pipeline_mode=pl.Buffered(1))


def _mixer(x, sinks, w_in, b_in, w_dw, b_dw, g_cn, b_cn, w_out, g1, b1):
    B, S, D = x.shape
    ts = SEQ_TILE
    tile = pl.BlockSpec((1, ts, D), lambda b, i: (b, i, 0))
    return pl.pallas_call(
        _mixer_kernel,
        grid=(B, S // ts),
        in_specs=[
            pl.BlockSpec(memory_space=pltpu.SMEM),
            tile,
            _resident_spec((D, D_IN)), _const_spec((1, D_IN)),
            _const_spec((CONV_HALO, CONV_CH)), _const_spec((1, CONV_CH)),
            _const_spec((1, CONV_CH)), _const_spec((1, CONV_CH)),
            _resident_spec((D_MIX, D)), _const_spec((1, D)), _const_spec((1, D)),
        ],
        out_specs=tile,
        out_shape=jax.ShapeDtypeStruct((B, S, D), F32),
        scratch_shapes=[
            pltpu.VMEM((D, D_IN), BF16),
            pltpu.VMEM((D_MIX, D), BF16),
            pltpu.VMEM((CONV_HALO + ts, CONV_CH), F32),
            pltpu.VMEM((SUBLANES - 1, CONV_HALO + ts, CONV_CH), F32),
            pltpu.VMEM((ts, CONV_CH), F32),
            pltpu.VMEM((ts, ATTN_W), BF16),
            pltpu.VMEM((2 * N_KV_HEADS, WINDOW + ts, KV_W), BF16),
            pltpu.VMEM((2, WINDOW + ts, KV_W), BF16),
            pltpu.VMEM((ts, D_MIX), BF16),
        ],
        compiler_params=pltpu.CompilerParams(
            dimension_semantics=("arbitrary", "arbitrary"), vmem_limit_bytes=VMEM_LIMIT),
        name="mixer",
    )(sinks, x, w_in, b_in, w_dw, b_dw, g_cn, b_cn, w_out, g1, b1)


def _memkv_kernel(mem_ref, w_ref, o_ref):
    o_ref[...] = jnp.dot(mem_ref[...].astype(BF16), w_ref[...].astype(BF16),
                         preferred_element_type=F32).astype(BF16)


def _memkv(mem2d, w_mkv):
    M, D = mem2d.shape
    N = w_mkv.shape[1]
    tn = 512
    return pl.pallas_call(
        _memkv_kernel,
        grid=(N // tn,),
        in_specs=[pl.BlockSpec((M, D), lambda j: (0, 0)), pl.BlockSpec((D, tn), lambda j: (0, j))],
        out_specs=pl.BlockSpec((M, tn), lambda j: (0, j)),
        out_shape=jax.ShapeDtypeStruct((M, N), BF16),
        compiler_params=pltpu.CompilerParams(dimension_semantics=("arbitrary",)),
        name="memkv",
    )(mem2d, w_mkv)


def _first_max(rows):
    best = rows[0]
    for r in rows[1:]:
        best = jnp.maximum(best, r)
    idx = jnp.full(best.shape, len(rows) - 1, jnp.int32)
    for k in range(len(rows) - 2, -1, -1):
        idx = jnp.where(rows[k] == best, k, idx)
    return best, idx


def _route_plan(logits_t):
    tile = logits_t.shape[1]
    row = lambda k: logits_t[k:k + 1, :]
    gmax, g_idx = _first_max([row(g) for g in range(N_GROUPS)])
    gsum = jnp.exp(row(0) - gmax)
    for g in range(1, N_GROUPS):
        gsum = gsum + jnp.exp(row(g) - gmax)
    g_p = 1.0 / gsum
    rl = []
    for e in range(EXPERTS_PER_GROUP):
        v = row(ROUTE_OFF + (N_GROUPS - 1) * EXPERTS_PER_GROUP + e)
        for g in range(N_GROUPS - 2, -1, -1):
            v = jnp.where(g_idx == g, row(ROUTE_OFF + g * EXPERTS_PER_GROUP + e), v)
        rl.append(v)
    m1, i1 = _first_max(rl)
    m2, i2 = _first_max([jnp.where(i1 == e, MASK_VALUE, rl[e]) for e in range(EXPERTS_PER_GROUP)])
    ex = jnp.exp(m2 - m1)
    w1 = 1.0 / (1.0 + ex)
    w2 = ex * w1
    e1 = g_idx * EXPERTS_PER_GROUP + i1
    e2 = g_idx * EXPERTS_PER_GROUP + i2

    eid = lax.broadcasted_iota(jnp.int32, (N_EXPERTS, tile), 0)
    hit1 = eid == e1
    hit2 = eid == e2
    oh = jnp.where(jnp.logical_or(hit1, hit2), 1.0, 0.0)
    r = lax.broadcasted_iota(jnp.int32, (tile, tile), 0)
    c = lax.broadcasted_iota(jnp.int32, (tile, tile), 1)
    tri = jnp.where(r <= c, 1.0, 0.0).astype(BF16)
    csum = jnp.dot(oh.astype(BF16), tri, preferred_element_type=F32)
    counts = jnp.broadcast_to(csum[:, tile - 1:tile], (N_EXPERTS, tile)).astype(jnp.int32)
    nch = jnp.right_shift(counts + (CHUNK - 1), CHUNK.bit_length() - 1)
    er = lax.broadcasted_iota(jnp.int32, (N_EXPERTS, N_EXPERTS), 0)
    ec = lax.broadcasted_iota(jnp.int32, (N_EXPERTS, N_EXPERTS), 1)
    lower = jnp.where(ec < er, 1.0, 0.0).astype(BF16)
    off = jnp.dot(lower, nch.astype(F32).astype(BF16), preferred_element_type=F32) * CHUNK
    pos = off + csum - oh
    lp1 = jnp.sum(jnp.where(hit1, pos, 0.0), axis=0, keepdims=True)
    lp2 = jnp.sum(jnp.where(hit2, pos, 0.0), axis=0, keepdims=True)
    zero = jnp.zeros_like(lp1)
    route_t = jnp.concatenate([lp1, lp2, g_p * w1, g_p * w2, zero, zero, zero, zero], axis=0)
    meta = jnp.concatenate([nch[:, 0:LANES], off[:, 0:LANES].astype(jnp.int32)], axis=0)
    return route_t, meta


def _memattn_kernel(x_ref, wq_ref, k_ref, v_ref, wo_ref, g2_ref, b2_ref, wr2_ref, wrh_ref, br_ref,
                    o_ref, ob_ref, route_ref, routet_ref, meta_ref, wq_b, wo_b):
    @pl.when(jnp.logical_and(pl.program_id(0) == 0, pl.program_id(1) == 0))
    def _():
        _cast_bf16(wq_b, wq_ref)
        _cast_bf16(wo_b, wo_ref)

    def rows_logits(rs):
        x = x_ref[0, rs, :]
        q = jnp.dot(x.astype(BF16), wq_b[...], preferred_element_type=F32)
        q = (q * (MEM_HEAD_DIM ** -0.5)).astype(BF16)
        outs = []
        for h in range(MEM_HEADS):
            sl = slice(h * MEM_HEAD_DIM, (h + 1) * MEM_HEAD_DIM)
            s = _dot_nt(q[:, sl], k_ref[0, :, sl])
            m = jnp.max(s, axis=-1, keepdims=True)
            p = jnp.exp(s - m)
            denom = jnp.sum(p, axis=-1, keepdims=True)
            o = jnp.dot(p.astype(BF16), v_ref[0, :, sl], preferred_element_type=F32)
            outs.append((o / denom).astype(BF16))
        o = jnp.dot(jnp.concatenate(outs, axis=-1), wo_b[...], preferred_element_type=F32)
        x2 = _layer_norm(ALPHA * x + o, g2_ref[...], b2_ref[...])
        o_ref[0, rs, :] = x2
        x2h = x2.astype(BF16)
        ob_ref[0, rs, :] = x2h
        x2l = (x2 - x2h.astype(F32)).astype(BF16)
        hh = jnp.dot(x2h, wr2_ref[...], preferred_element_type=F32)
        return (hh[:, 0:LANES] + hh[:, LANES:2 * LANES]
                + jnp.dot(x2l, wrh_ref[...], preferred_element_type=F32) + br_ref[...])

    ts = x_ref.shape[1]
    logits = jnp.concatenate(
        [rows_logits(slice(r0, r0 + MEM_ROWS)) for r0 in range(0, ts, MEM_ROWS)], axis=0)
    route_t, meta = _route_plan(jnp.transpose(logits))
    routet_ref[0] = route_t
    pad = jnp.zeros((LANES - SUBLANES, ts), F32)
    route_ref[0] = jnp.transpose(jnp.concatenate([route_t, pad], axis=0))
    meta_ref[0] = meta


def _memattn(x1, wq, kvm, wo, g2, b2, wr, br):
    B, S, D = x1.shape
    ts = SEQ_TILE
    assert ts == MOE_TILE
    nt = S // ts
    c = wr * (2.0 ** 16 + 1.0)
    w_high = c - (c - wr)
    wrh = w_high.astype(BF16)
    wr2 = jnp.concatenate([wrh, (wr - w_high).astype(BF16)], axis=1)
    tile = pl.BlockSpec((1, ts, D), lambda b, i: (b, i, 0))
    kspec = pl.BlockSpec((1, MEM_LEN, D), lambda b, i: (b, 0, 0))
    vspec = pl.BlockSpec((1, MEM_LEN, D), lambda b, i: (b, 0, 1))
    return pl.pallas_call(
        _memattn_kernel,
        grid=(B, nt),
        in_specs=[tile, _resident_spec((D, D)), kspec, vspec, _resident_spec((D, D)),
                  _const_spec((1, D)), _const_spec((1, D)),
                  _const_spec((D, 2 * LANES)), _const_spec((D, LANES)), _const_spec((1, LANES))],
        out_specs=[tile, tile, pl.BlockSpec((1, ts, LANES), lambda b, i: (b, i, 0)),
                   pl.BlockSpec((1, SUBLANES, ts), lambda b, i: (b * nt + i, 0, 0)),
                   pl.BlockSpec((1, 2 * N_EXPERTS, LANES), lambda b, i: (b * nt + i, 0, 0))],
        out_shape=[jax.ShapeDtypeStruct((B, S, D), F32),
                   jax.ShapeDtypeStruct((B, S, D), BF16),
                   jax.ShapeDtypeStruct((B, S, LANES), F32),
                   jax.ShapeDtypeStruct((B * nt, SUBLANES, ts), F32),
                   jax.ShapeDtypeStruct((B * nt, 2 * N_EXPERTS, LANES), jnp.int32)],
        scratch_shapes=[pltpu.VMEM((D, D), BF16), pltpu.VMEM((D, D), BF16)],
        compiler_params=pltpu.CompilerParams(
            dimension_semantics=("arbitrary", "arbitrary"), vmem_limit_bytes=VMEM_LIMIT),
        name="memattn",
    )(x1, wq, kvm, kvm, wo, g2, b2, wr2, wrh, br)


def _gmm_blocks(n_tokens):
    rows = (n_tokens * TOP_K + (n_tokens // MOE_TILE) * N_EXPERTS * (CHUNK - 1)
            + N_EXPERTS * (GMM_ROWS - CHUNK))
    return -(-rows // GMM_ROWS)


def _plan(meta, T):
    nch = meta[:, :N_EXPERTS, 0]
    n16 = nch * CHUNK
    n_e = jnp.sum(n16, axis=0)
    reg = (n_e + GMM_ROWS - 1) // GMM_ROWS * GMM_ROWS
    gend = jnp.cumsum(reg)
    gbase = gend - reg
    dst = gbase[None, :] + jnp.cumsum(n16, axis=0) - n16
    blk_row = jnp.arange(_gmm_blocks(T), dtype=jnp.int32)[:, None] * GMM_ROWS
    blk_expert = jnp.sum(blk_row >= gend[None, :], axis=1)
    blk_expert = jnp.minimum(blk_expert, N_EXPERTS - 1).astype(jnp.int32)
    n_used = (gend[-1] // GMM_ROWS).astype(jnp.int32).reshape(1)
    data_end = jnp.sum(jnp.where(blk_expert[:, None] == jnp.arange(N_EXPERTS)[None, :],
                                 (gbase + n_e)[None, :], 0), axis=1)
    blk_rows = jnp.where(blk_row[:, 0] < gend[-1], jnp.clip(data_end - blk_row[:, 0], 0, GMM_ROWS), 0)
    src = (jnp.cumsum(nch, axis=1) - nch) * CHUNK
    n_big = nch // BIG

    def copy_list(count, src0, dst0, rows, length):
        cum = jnp.cumsum(count, axis=1)
        first = (cum - count)[:, None, :]
        k = jnp.arange(length, dtype=jnp.int32)[None, :, None]
        mine = (k >= first) & (k < cum[:, None, :])
        step = (k - first) * rows
        pick = lambda base: jnp.sum(jnp.where(mine, base[:, None, :] + step, 0), axis=2)
        return pick(src0), pick(dst0), cum[:, -1]

    big_src, big_dst, big_n = copy_list(n_big, src, dst, BIG * CHUNK, MAX_BIG)
    rest = n_big * (BIG * CHUNK)
    small_src, small_dst, small_n = copy_list(nch - n_big * BIG, src + rest, dst + rest, CHUNK, MAX_SMALL)
    i32 = lambda a: a.astype(jnp.int32)
    fill_start = jnp.concatenate([gbase + n_e, gend[-1:]])
    fill_n = jnp.concatenate([reg - n_e, _gmm_blocks(T) * GMM_ROWS - gend[-1:]]) // CHUNK
    return dict(big_src=i32(big_src).reshape(-1), big_dst=i32(big_dst).reshape(-1), big_n=i32(big_n),
                small_src=i32(small_src).reshape(-1), small_dst=i32(small_dst).reshape(-1), small_n=i32(small_n),
                fill_start=i32(fill_start), fill_n=i32(fill_n),
                fill_tot=i32(jnp.sum(fill_n)).reshape(1), blk_expert=blk_expert, n_used=n_used,
                blk_rows=i32(blk_rows))


def _rows_copy(src_ref, src_row, dst_ref, dst_row, rows, sem):
    return pltpu.make_async_copy(
        src_ref.at[pl.ds(pl.multiple_of(src_row, CHUNK), rows), :],
        dst_ref.at[pl.ds(pl.multiple_of(dst_row, CHUNK), rows), :], sem)


def _chunk_copy(src_ref, src_row, dst_ref, dst_row, sem):
    return _rows_copy(src_ref, src_row, dst_ref, dst_row, CHUNK, sem)


def _issue_copies(lists, tile, tile_ref, tile_is_src, hbm_ref, sem):
    big_src, big_dst, big_n, small_src, small_dst, small_n = lists
    for src_l, dst_l, n_l, length, rows in ((big_src, big_dst, big_n, MAX_BIG, BIG * CHUNK),
                                            (small_src, small_dst, small_n, MAX_SMALL, CHUNK)):
        def issue(k, carry, src_l=src_l, dst_l=dst_l, length=length, rows=rows):
            local, remote = src_l[tile * length + k], dst_l[tile * length + k]
            if tile_is_src:
                _rows_copy(tile_ref, local, hbm_ref, remote, rows, sem).start()
            else:
                _rows_copy(hbm_ref, remote, tile_ref, local, rows, sem).start()
            return carry

        lax.fori_loop(0, n_l[tile], issue, 0)


def _wait_rows(n, rows, src_ref, dst_ref, sem):
    def body(c, carry):
        pltpu.make_async_copy(src_ref.at[pl.ds(0, rows), :], dst_ref.at[pl.ds(0, rows), :], sem).wait()
        return carry

    lax.fori_loop(0, n, body, 0)


def _wait_copies(lists, tile, src_ref, dst_ref, sem, extra_chunks=0):
    _wait_rows(lists[2][tile], BIG * CHUNK, src_ref, dst_ref, sem)
    n = lists[5][tile] + extra_chunks
    _wait_rows(n // WAIT_GROUP, WAIT_GROUP * CHUNK, src_ref, dst_ref, sem)
    _wait_rows(n % WAIT_GROUP, CHUNK, src_ref, dst_ref, sem)


def _dispatch_kernel(bs_ref, bd_ref, bn_ref, ss_ref, sd_ref, sn_ref, fstart_ref, fn_ref, ftot_ref,
                     lp_ref, x_ref, xs_hbm, xt, zbuf, sems):
    lists = (bs_ref, bd_ref, bn_ref, ss_ref, sd_ref, sn_ref)
    t = pl.program_id(0)
    nt = pl.num_programs(0)
    slot = t % 2
    xt_s = xt.at[slot]
    sem = sems.at[slot]

    @pl.when(t >= 2)
    def _():
        _wait_copies(lists, t - 2, xt_s, xs_hbm, sem)

    lp = lp_ref[0].astype(jnp.int32)
    r = lax.broadcasted_iota(jnp.int32, (SORT_ROWS, MOE_TILE), 0)
    hit = jnp.logical_or(lp[0:1, :] == r, lp[1:2, :] == r)
    p = jnp.where(hit, 1.0, 0.0).astype(BF16)
    xt_s[...] = jnp.dot(p, x_ref[...], preferred_element_type=F32).astype(BF16)

    _issue_copies(lists, t, xt_s, True, xs_hbm, sem)

    @pl.when(t == nt - 1)
    def _():
        zbuf[...] = jnp.zeros_like(zbuf)

        def per_range(e, carry):
            def issue(c, carry2):
                _chunk_copy(zbuf, 0, xs_hbm, fstart_ref[e] + c * CHUNK, sem).start()
                return carry2

            return lax.fori_loop(0, fn_ref[e], issue, carry)

        lax.fori_loop(0, N_EXPERTS + 1, per_range, 0)
        _wait_copies(lists, t - 1, xt.at[1 - slot], xs_hbm, sems.at[1 - slot])
        _wait_copies(lists, t, xt_s, xs_hbm, sem, extra_chunks=ftot_ref[0])


def _copy_lists(plan):
    return tuple(plan[k] for k in ("big_src", "big_dst", "big_n", "small_src", "small_dst", "small_n"))


def _dispatch(plan, lpt, x2b):
    T, D = x2b.shape
    nt = T // MOE_TILE
    assert nt >= 2
    grid_spec = pltpu.PrefetchScalarGridSpec(
        num_scalar_prefetch=9,
        grid=(nt,),
        in_specs=[pl.BlockSpec((1, SUBLANES, MOE_TILE), lambda t, *_: (t, 0, 0)),
                  pl.BlockSpec((MOE_TILE, D), lambda t, *_: (t, 0))],
        out_specs=pl.BlockSpec(memory_space=pl.ANY),
        scratch_shapes=[pltpu.VMEM((2, SORT_ROWS, D), BF16), pltpu.VMEM((CHUNK, D), BF16),
                        pltpu.SemaphoreType.DMA((2,))],
    )
    return pl.pallas_call(
        _dispatch_kernel,
        grid_spec=grid_spec,
        out_shape=jax.ShapeDtypeStruct((_gmm_blocks(T) * GMM_ROWS, D), BF16),
        compiler_params=pltpu.CompilerParams(
            dimension_semantics=("arbitrary",), vmem_limit_bytes=VMEM_LIMIT),
        name="moe_dispatch",
    )(*_copy_lists(plan), plan["fill_start"], plan["fill_n"], plan["fill_tot"], lpt, x2b)


def _expert_ffn(xb, wg_b, wu_b, wd_b):
    g = jnp.dot(xb, wg_b[...], preferred_element_type=F32)
    u = jnp.dot(xb, wu_b[...], preferred_element_type=F32)
    h = (g * jax.nn.sigmoid(g)) * u
    return jnp.dot(h.astype(BF16), wd_b[...], preferred_element_type=F32).astype(BF16)


def _gmm_kernel(be_ref, nu_ref, br_ref, x_ref, wg_ref, wu_ref, wd_ref, o_ref, wg_b, wu_b, wd_b):
    b = pl.program_id(0)
    rows = br_ref[b]
    half = GMM_ROWS // 2

    @pl.when(jnp.logical_or(b == 0, be_ref[b] != be_ref[jnp.maximum(b - 1, 0)]))
    def _():
        _cast_bf16(wg_b, wg_ref.at[0])
        _cast_bf16(wu_b, wu_ref.at[0])
        _cast_bf16(wd_b, wd_ref.at[0])

    @pl.when(rows > half)
    def _():
        o_ref[...] = _expert_ffn(x_ref[...], wg_b, wu_b, wd_b)

    @pl.when(jnp.logical_and(rows > 0, rows <= half))
    def _():
        o_ref[0:half, :] = _expert_ffn(x_ref[0:half, :], wg_b, wu_b, wd_b)
        o_ref[half:GMM_ROWS, :] = jnp.zeros((GMM_ROWS - half, o_ref.shape[1]), o_ref.dtype)

    @pl.when(rows == 0)
    def _():
        o_ref[...] = jnp.zeros_like(o_ref)


def _gmm(plan, xs, wg, wu, wd):
    R, D = xs.shape
    rows = pl.BlockSpec((GMM_ROWS, D), lambda b, be, nu, br: (jnp.minimum(b, nu[0] - 1), 0))
    grid_spec = pltpu.PrefetchScalarGridSpec(
        num_scalar_prefetch=3,
        grid=(R // GMM_ROWS,),
        in_specs=[rows,
                  pl.BlockSpec((1, D, D_EXPERT), lambda b, be, nu, br: (be[b], 0, 0)),
                  pl.BlockSpec((1, D, D_EXPERT), lambda b, be, nu, br: (be[b], 0, 0)),
                  pl.BlockSpec((1, D_EXPERT, D), lambda b, be, nu, br: (be[b], 0, 0))],
        out_specs=pl.BlockSpec((GMM_ROWS, D), lambda b, be, nu, br: (b, 0)),
        scratch_shapes=[pltpu.VMEM((D, D_EXPERT), BF16), pltpu.VMEM((D, D_EXPERT), BF16),
                        pltpu.VMEM((D_EXPERT, D), BF16)],
    )
    return pl.pallas_call(
        _gmm_kernel,
        grid_spec=grid_spec,
        out_shape=jax.ShapeDtypeStruct((R, D), BF16),
        compiler_params=pltpu.CompilerParams(
            dimension_semantics=("arbitrary",), vmem_limit_bytes=VMEM_LIMIT),
        name="moe_gmm",
    )(plan["blk_expert"], plan["n_used"], plan["blk_rows"], xs, wg, wu, wd)


def _combine_kernel(bs_ref, bd_ref, bn_ref, ss_ref, sd_ref, sn_ref, cm_ref, x_ref, ys_hbm, g3_ref, b3_ref,
                    o_ref, yt, sems):
    lists = (bs_ref, bd_ref, bn_ref, ss_ref, sd_ref, sn_ref)
    t = pl.program_id(0)
    nt = pl.num_programs(0)
    slot = t % 2

    def fetch(tile, s):
        _issue_copies(lists, tile, yt.at[s], False, ys_hbm, sems.at[s])

    @pl.when(t == 0)
    def _():
        yt[...] = jnp.zeros_like(yt)
        fetch(0, 0)

    @pl.when(t + 1 < nt)
    def _():
        fetch(t + 1, 1 - slot)

    cm = cm_ref[...]
    col = lax.broadcasted_iota(jnp.int32, (MOE_TILE, SORT_ROWS), 1)
    lp0 = cm[:, 0:1].astype(jnp.int32)
    lp1 = cm[:, 1:2].astype(jnp.int32)
    w = jnp.where(col == lp0, cm[:, 2:3], 0.0) + jnp.where(col == lp1, cm[:, 3:4], 0.0)
    w = w.astype(BF16)

    _wait_copies(lists, t, ys_hbm, yt.at[slot], sems.at[slot])
    y = jnp.dot(w, yt[slot], preferred_element_type=F32)
    o_ref[...] = _layer_norm(ALPHA * x_ref[...] + y, g3_ref[...], b3_ref[...])


def _combine(plan, route, x2, ys, g3, b3):
    T, D = x2.shape
    nt = T // MOE_TILE
    grid_spec = pltpu.PrefetchScalarGridSpec(
        num_scalar_prefetch=6,
        grid=(nt,),
        in_specs=[pl.BlockSpec((MOE_TILE, LANES), lambda t, *_: (t, 0)),
                  pl.BlockSpec((MOE_TILE, D), lambda t, *_: (t, 0)),
                  pl.BlockSpec(memory_space=pl.ANY),
                  pl.BlockSpec((1, D), lambda t, *_: (0, 0)),
                  pl.BlockSpec((1, D), lambda t, *_: (0, 0))],
        out_specs=pl.BlockSpec((MOE_TILE, D), lambda t, *_: (t, 0)),
        scratch_shapes=[pltpu.VMEM((2, SORT_ROWS, D), BF16), pltpu.SemaphoreType.DMA((2,))],
    )
    return pl.pallas_call(
        _combine_kernel,
        grid_spec=grid_spec,
        out_shape=jax.ShapeDtypeStruct((T, D), F32),
        compiler_params=pltpu.CompilerParams(
            dimension_semantics=("arbitrary",), vmem_limit_bytes=VMEM_LIMIT),
        name="moe_combine",
    )(*_copy_lists(plan), route, x2, ys, g3, b3)


def _moe(x2, x2b, route, lpt, meta, wg, wu, wd, g3, b3):
    plan = _plan(meta, x2.shape[0])
    xs = _dispatch(plan, lpt, x2b)
    ys = _gmm(plan, xs, wg, wu, wd)
    return _combine(plan, route, x2, ys, g3, b3)


def _row(v):
    return v.reshape(1, -1).astype(F32)


def kernel(x, mem, w_in, b_in, w_dw, b_dw, g_conv_norm, b_conv_norm, attn_sinks, w_out, g_ln1, b_ln1,
           w_mq, w_mkv, w_mo, g_ln2, b_ln2, w_group, b_group, w_router, b_router, w_gate, w_up, w_down,
           g_ln3, b_ln3):
    B, S, D = x.shape
    for l in range(DEPTH):
        w_dw_p = jnp.zeros((CONV_HALO, CONV_CH), F32).at[:CONV_WIDTH].set(w_dw[l])
        x1 = _mixer(x, attn_sinks[l].astype(F32), w_in[l], _row(b_in[l]), w_dw_p,
                    _row(b_dw[l]), _row(g_conv_norm[l]), _row(b_conv_norm[l]),
                    w_out[l], _row(g_ln1[l]), _row(b_ln1[l]))

        kvm = _memkv(mem.reshape(B * MEM_LEN, D), w_mkv[l]).reshape(B, MEM_LEN, 2 * D)

        wr = jnp.concatenate(
            [w_group[l], jnp.transpose(w_router[l], (1, 0, 2)).reshape(D, N_EXPERTS)], axis=1)
        wr = jnp.pad(wr, ((0, 0), (0, LANES - wr.shape[1])))
        br = jnp.pad(jnp.concatenate([b_group[l], b_router[l].reshape(-1)]), (0, LANES - N_GROUPS - N_EXPERTS))
        x2, x2b, route, lpt, meta = _memattn(x1, w_mq[l], kvm, w_mo[l],
                                 _row(g_ln2[l]), _row(b_ln2[l]), wr.astype(F32), _row(br))

        T = B * S
        y = _moe(x2.reshape(T, D), x2b.reshape(T, D), route.reshape(T, LANES), lpt, meta,
                 w_gate[l], w_up[l], w_down[l],
                 _row(g_ln3[l]), _row(b_ln3[l]))
        x = y.reshape(B, S, D)
    return x
```

```python
import functools

import jax
import jax.numpy as jnp
from jax import lax
from jax.experimental import pallas as pl
from jax.experimental.pallas import tpu as pltpu

D_MODEL = 1024
MEM_LEN = 256
CONV_CH = 512
CONV_WIDTH = 31
N_HEADS = 8
N_KV_HEADS = 2
HEAD_DIM = 64
GQ = N_HEADS // N_KV_HEADS
ATTN_W = N_HEADS * HEAD_DIM
KV_W = N_KV_HEADS * HEAD_DIM
WINDOW = 128
D_MIX = CONV_CH + ATTN_W
D_IN = 2 * CONV_CH + ATTN_W + 2 * KV_W
MEM_HEADS = 4
MEM_HEAD_DIM = D_MODEL // MEM_HEADS
N_GROUPS = 4
EXPERTS_PER_GROUP = 4
N_EXPERTS = N_GROUPS * EXPERTS_PER_GROUP
D_EXPERT = D_MODEL // 2
DEPTH = 1
ALPHA = (2.0 * DEPTH) ** 0.25
LN_EPS = 1e-5

LANES = 128
SUBLANES = 8
CONV_ROWS = 128
LN_ROWS = 64
MEM_TILE = 1024
MASK_VALUE = -1e30
CONV_HALO = 32
SEQ_TILE = 512
MOE_TILE = 512
CHUNK = 16
GMM_ROWS = 512
TOP_K = 2
SORT_ROWS = -(-(MOE_TILE * TOP_K + N_EXPERTS * (CHUNK - 1)) // 256) * 256
BIG = 4
MAX_BIG = SORT_ROWS // (BIG * CHUNK)
MAX_SMALL = N_EXPERTS * (BIG - 1)
WAIT_GROUP = 8
ROUTE_OFF = N_GROUPS
VMEM_LIMIT = 56 * 1024 * 1024

BF16 = jnp.bfloat16
F32 = jnp.float32


def _layer_norm(x, g, b):
    mu = jnp.mean(x, axis=-1, keepdims=True)
    xc = x - mu
    var = jnp.mean(xc * xc, axis=-1, keepdims=True)
    return xc * lax.rsqrt(var + LN_EPS) * g + b


def _cast_bf16(dst_ref, src_ref):
    rows = 256
    for r0 in range(0, src_ref.shape[0], rows):
        dst_ref[r0:r0 + rows, :] = src_ref[r0:r0 + rows, :].astype(BF16)


def _dot_nt(a, b):
    return lax.dot_general(a, b, (((1,), (1,)), ((), ())), preferred_element_type=F32)


def _mixer_kernel(sinks_ref, x_ref, w_in_ref, b_in_ref, w_dw_ref, b_dw_ref, g_cn_ref, b_cn_ref,
                  w_out_ref, g1_ref, b1_ref, o_ref, w_in_b, w_out_b, hbuf, hshift, cbuf, qbuf, kbuf, vbuf, ymix):
    i = pl.program_id(1)
    ts = SEQ_TILE

    @pl.when(jnp.logical_and(pl.program_id(0) == 0, i == 0))
    def _():
        _cast_bf16(w_in_b, w_in_ref)
        _cast_bf16(w_out_b, w_out_ref)

    @pl.when(i == 0)
    def _():
        hbuf[0:CONV_HALO, :] = jnp.zeros((CONV_HALO, CONV_CH), F32)
        kbuf[:, 0:WINDOW, :] = jnp.zeros((2 * N_KV_HEADS, WINDOW, KV_W), BF16)
        vbuf[:, 0:WINDOW, :] = jnp.zeros((2, WINDOW, KV_W), BF16)

    x = x_ref[0]
    u = jnp.dot(x.astype(BF16), w_in_b[...], preferred_element_type=F32) + b_in_ref[...]
    a = u[:, 0:CONV_CH]
    gate = u[:, CONV_CH:2 * CONV_CH]
    hbuf[CONV_HALO:CONV_HALO + ts, :] = a * jax.nn.sigmoid(gate)
    qbuf[...] = (u[:, 2 * CONV_CH:2 * CONV_CH + ATTN_W] * (HEAD_DIM ** -0.5)).astype(BF16)
    kf = u[:, 2 * CONV_CH + ATTN_W:2 * CONV_CH + ATTN_W + KV_W]
    vf = u[:, 2 * CONV_CH + ATTN_W + KV_W:D_IN]
    kr = pltpu.roll(kf, HEAD_DIM, axis=1)
    vr = pltpu.roll(vf, HEAD_DIM, axis=1)
    lo = lax.broadcasted_iota(jnp.int32, (ts, KV_W), 1) < HEAD_DIM
    rows = slice(WINDOW, WINDOW + ts)
    kbuf[0, rows, :] = jnp.where(lo, kf, 0.0).astype(BF16)
    kbuf[1, rows, :] = jnp.where(lo, 0.0, kr).astype(BF16)
    kbuf[2, rows, :] = jnp.where(lo, kr, 0.0).astype(BF16)
    kbuf[3, rows, :] = jnp.where(lo, 0.0, kf).astype(BF16)
    vbuf[0, rows, :] = vf.astype(BF16)
    vbuf[1, rows, :] = vr.astype(BF16)

    base = CONV_HALO - (CONV_WIDTH - 1)
    n_shift = ts + CONV_HALO - SUBLANES
    for b in range(1, SUBLANES):
        hshift[b - 1, 0:n_shift, :] = hbuf[b:b + n_shift, :]
    rc = CONV_ROWS

    def conv_chunk(c):
        r0 = c * rc
        for l in range(CONV_CH // LANES):
            ls = slice(l * LANES, (l + 1) * LANES)
            acc = jnp.zeros((rc, LANES), F32)
            for j in range(CONV_WIDTH):
                a8, b = divmod(j + base, SUBLANES)
                rs = slice(r0 + SUBLANES * a8, r0 + SUBLANES * a8 + rc)
                tap = hbuf[rs, ls] if b == 0 else hshift[b - 1, rs, ls]
                acc = acc + tap * w_dw_ref[j:j + 1, ls]
            cbuf[r0:r0 + rc, ls] = acc
        for r1 in range(r0, r0 + rc, LN_ROWS):
            rs = slice(r1, r1 + LN_ROWS)
            y = _layer_norm(cbuf[rs, :] + b_dw_ref[...], g_cn_ref[...], b_cn_ref[...])
            y = y * jax.nn.sigmoid(y)
            ymix[rs, 0:CONV_CH] = y.astype(BF16)

    qi = lax.broadcasted_iota(jnp.int32, (2 * WINDOW, 2 * WINDOW), 0) % WINDOW
    kj = lax.broadcasted_iota(jnp.int32, (2 * WINDOW, 2 * WINDOW), 1)
    dist = qi + WINDOW - kj
    band = (dist >= 0) & (dist < WINDOW)
    top = lax.broadcasted_iota(jnp.int32, (2 * WINDOW, 1), 0) < WINDOW
    lo_out = lax.broadcasted_iota(jnp.int32, (WINDOW, 2 * HEAD_DIM), 1) < HEAD_DIM

    def attn_block(jb):
        r0 = jb * WINDOW
        valid = band & jnp.logical_or(i != 0, kj >= WINDOW) if jb == 0 else band
        for kvh in range(N_KV_HEADS):
            h0 = kvh * GQ
            c0 = h0 * HEAD_DIM
            qs = jnp.concatenate([qbuf[r0:r0 + WINDOW, c0:c0 + 2 * HEAD_DIM],
                                  qbuf[r0:r0 + WINDOW, c0 + 2 * HEAD_DIM:c0 + 4 * HEAD_DIM]], axis=0)
            pv = []
            for par in range(2):
                kk = kbuf[2 * kvh + par, r0:r0 + 2 * WINDOW, :]
                vv = vbuf[(kvh + par) % 2, r0:r0 + 2 * WINDOW, :]
                s = jnp.where(valid, _dot_nt(qs, kk), MASK_VALUE)
                sink = jnp.where(top, sinks_ref[h0 + par], sinks_ref[h0 + 2 + par])
                m = jnp.maximum(jnp.max(s, axis=-1, keepdims=True), sink)
                p = jnp.exp(s - m)
                denom = jnp.sum(p, axis=-1, keepdims=True) + jnp.exp(sink - m)
                pv.append(jnp.dot(p.astype(BF16), vv, preferred_element_type=F32) / denom)
            for pair in range(2):
                rs = slice(pair * WINDOW, (pair + 1) * WINDOW)
                o = jnp.where(lo_out, pv[0][rs], pv[1][rs])
                cs = CONV_CH + c0 + pair * 2 * HEAD_DIM
                ymix[r0:r0 + WINDOW, cs:cs + 2 * HEAD_DIM] = o.astype(BF16)

    assert ts // rc == ts // WINDOW
    for c in range(ts // rc):
        attn_block(c)
        conv_chunk(c)

    mix = jnp.dot(ymix[...], w_out_b[...], preferred_element_type=F32)
    o_ref[0] = _layer_norm(ALPHA * x + mix, g1_ref[...], b1_ref[...])

    hbuf[0:CONV_HALO, :] = hbuf[ts:ts + CONV_HALO, :]
    kbuf[:, 0:WINDOW, :] = kbuf[:, ts:ts + WINDOW, :]
    vbuf[:, 0:WINDOW, :] = vbuf[:, ts:ts + WINDOW, :]


def _const_spec(shape):
    nd = len(shape)
    return pl.BlockSpec(shape, lambda *_: (0,) * nd)


def _resident_spec(shape):
    nd = len(shape)
    return pl.BlockSpec(shape, lambda *_: (0,) * nd, pipeline_mode=pl.Buffered(1))


def _mixer(x, sinks, w_in, b_in, w_dw, b_dw, g_cn, b_cn, w_out, g1, b1):
    B, S, D = x.shape
    ts = SEQ_TILE
    tile = pl.BlockSpec((1, ts, D), lambda b, i: (b, i, 0))
    return pl.pallas_call(
        _mixer_kernel,
        grid=(B, S // ts),
        in_specs=[
            pl.BlockSpec(memory_space=pltpu.SMEM),
            tile,
            _resident_spec((D, D_IN)), _const_spec((1, D_IN)),
            _const_spec((CONV_HALO, CONV_CH)), _const_spec((1, CONV_CH)),
            _const_spec((1, CONV_CH)), _const_spec((1, CONV_CH)),
            _resident_spec((D_MIX, D)), _const_spec((1, D)), _const_spec((1, D)),
        ],
        out_specs=tile,
        out_shape=jax.ShapeDtypeStruct((B, S, D), F32),
        scratch_shapes=[
            pltpu.VMEM((D, D_IN), BF16),
            pltpu.VMEM((D_MIX, D), BF16),
            pltpu.VMEM((CONV_HALO + ts, CONV_CH), F32),
            pltpu.VMEM((SUBLANES - 1, CONV_HALO + ts, CONV_CH), F32),
            pltpu.VMEM((ts, CONV_CH), F32),
            pltpu.VMEM((ts, ATTN_W), BF16),
            pltpu.VMEM((2 * N_KV_HEADS, WINDOW + ts, KV_W), BF16),
            pltpu.VMEM((2, WINDOW + ts, KV_W), BF16),
            pltpu.VMEM((ts, D_MIX), BF16),
        ],
        compiler_params=pltpu.CompilerParams(
            dimension_semantics=("arbitrary", "arbitrary"), vmem_limit_bytes=VMEM_LIMIT),
        name="mixer",
    )(sinks, x, w_in, b_in, w_dw, b_dw, g_cn, b_cn, w_out, g1, b1)


def _memkv_kernel(mem_ref, w_ref, o_ref):
    o_ref[...] = jnp.dot(mem_ref[...].astype(BF16), w_ref[...].astype(BF16),
                         preferred_element_type=F32).astype(BF16)


def _memkv(mem2d, w_mkv):
    M, D = mem2d.shape
    N = w_mkv.shape[1]
    tn = 512
    return pl.pallas_call(
        _memkv_kernel,
        grid=(N // tn,),
        in_specs=[pl.BlockSpec((M, D), lambda j: (0, 0)), pl.BlockSpec((D, tn), lambda j: (0, j))],
        out_specs=pl.BlockSpec((M, tn), lambda j: (0, j)),
        out_shape=jax.ShapeDtypeStruct((M, N), BF16),
        compiler_params=pltpu.CompilerParams(dimension_semantics=("arbitrary",)),
        name="memkv",
    )(mem2d, w_mkv)


def _first_max(rows):
    best = rows[0]
    for r in rows[1:]:
        best = jnp.maximum(best, r)
    idx = jnp.full(best.shape, len(rows) - 1, jnp.int32)
    for k in range(len(rows) - 2, -1, -1):
        idx = jnp.where(rows[k] == best, k, idx)
    return best, idx


def _route_plan(logits_t):
    tile = logits_t.shape[1]
    row = lambda k: logits_t[k:k + 1, :]
    gmax, g_idx = _first_max([row(g) for g in range(N_GROUPS)])
    gsum = jnp.exp(row(0) - gmax)
    for g in range(1, N_GROUPS):
        gsum = gsum + jnp.exp(row(g) - gmax)
    g_p = 1.0 / gsum
    rl = []
    for e in range(EXPERTS_PER_GROUP):
        v = row(ROUTE_OFF + (N_GROUPS - 1) * EXPERTS_PER_GROUP + e)
        for g in range(N_GROUPS - 2, -1, -1):
            v = jnp.where(g_idx == g, row(ROUTE_OFF + g * EXPERTS_PER_GROUP + e), v)
        rl.append(v)
    m1, i1 = _first_max(rl)
    m2, i2 = _first_max([jnp.where(i1 == e, MASK_VALUE, rl[e]) for e in range(EXPERTS_PER_GROUP)])
    ex = jnp.exp(m2 - m1)
    w1 = 1.0 / (1.0 + ex)
    w2 = ex * w1
    e1 = g_idx * EXPERTS_PER_GROUP + i1
    e2 = g_idx * EXPERTS_PER_GROUP + i2

    eid = lax.broadcasted_iota(jnp.int32, (N_EXPERTS, tile), 0)
    hit1 = eid == e1
    hit2 = eid == e2
    oh = jnp.where(jnp.logical_or(hit1, hit2), 1.0, 0.0)
    r = lax.broadcasted_iota(jnp.int32, (tile, tile), 0)
    c = lax.broadcasted_iota(jnp.int32, (tile, tile), 1)
    tri = jnp.where(r <= c, 1.0, 0.0).astype(BF16)
    csum = jnp.dot(oh.astype(BF16), tri, preferred_element_type=F32)
    counts = jnp.broadcast_to(csum[:, tile - 1:tile], (N_EXPERTS, tile)).astype(jnp.int32)
    nch = jnp.right_shift(counts + (CHUNK - 1), CHUNK.bit_length() - 1)
    er = lax.broadcasted_iota(jnp.int32, (N_EXPERTS, N_EXPERTS), 0)
    ec = lax.broadcasted_iota(jnp.int32, (N_EXPERTS, N_EXPERTS), 1)
    lower = jnp.where(ec < er, 1.0, 0.0).astype(BF16)
    off = jnp.dot(lower, nch.astype(F32).astype(BF16), preferred_element_type=F32) * CHUNK
    pos = off + csum - oh
    lp1 = jnp.sum(jnp.where(hit1, pos, 0.0), axis=0, keepdims=True)
    lp2 = jnp.sum(jnp.where(hit2, pos, 0.0), axis=0, keepdims=True)
    zero = jnp.zeros_like(lp1)
    route_t = jnp.concatenate([lp1, lp2, g_p * w1, g_p * w2, zero, zero, zero, zero], axis=0)
    meta = jnp.concatenate([nch[:, 0:LANES], off[:, 0:LANES].astype(jnp.int32)], axis=0)
    return route_t, meta


def _memattn_kernel(x_ref, wq_ref, k_ref, v_ref, wo_ref, g2_ref, b2_ref, wr2_ref, wrh_ref, br_ref,
                    o_ref, ob_ref, route_ref, routet_ref, meta_ref, wq_b, wo_b):
    @pl.when(jnp.logical_and(pl.program_id(0) == 0, pl.program_id(1) == 0))
    def _():
        _cast_bf16(wq_b, wq_ref)
        _cast_bf16(wo_b, wo_ref)

    def rows_logits(rs):
        x = x_ref[0, rs, :]
        q = jnp.dot(x.astype(BF16), wq_b[...], preferred_element_type=F32)
        q = (q * (MEM_HEAD_DIM ** -0.5)).astype(BF16)
        outs = []
        for h in range(MEM_HEADS):
            sl = slice(h * MEM_HEAD_DIM, (h + 1) * MEM_HEAD_DIM)
            s = _dot_nt(q[:, sl], k_ref[0, :, sl])
            m = jnp.max(s, axis=-1, keepdims=True)
            p = jnp.exp(s - m)
            denom = jnp.sum(p, axis=-1, keepdims=True)
            o = jnp.dot(p.astype(BF16), v_ref[0, :, sl], preferred_element_type=F32)
            outs.append((o / denom).astype(BF16))
        o = jnp.dot(jnp.concatenate(outs, axis=-1), wo_b[...], preferred_element_type=F32)
        x2 = _layer_norm(ALPHA * x + o, g2_ref[...], b2_ref[...])
        o_ref[0, rs, :] = x2
        x2h = x2.astype(BF16)
        ob_ref[0, rs, :] = x2h
        x2l = (x2 - x2h.astype(F32)).astype(BF16)
        hh = jnp.dot(x2h, wr2_ref[...], preferred_element_type=F32)
        return (hh[:, 0:LANES] + hh[:, LANES:2 * LANES]
                + jnp.dot(x2l, wrh_ref[...], preferred_element_type=F32) + br_ref[...])

    ts = x_ref.shape[1]
    groups = [slice(r0, r0 + MOE_TILE) for r0 in range(0, ts, MOE_TILE)]
    logits = [rows_logits(rs) for rs in groups]
    pad = jnp.zeros((LANES - SUBLANES, MOE_TILE), F32)
    for k, rs in enumerate(groups):
        route_t, meta = _route_plan(jnp.transpose(logits[k]))
        routet_ref[k] = route_t
        route_ref[0, rs, :] = jnp.transpose(jnp.concatenate([route_t, pad], axis=0))
        meta_ref[k] = meta


def _memattn(x1, wq, kvm, wo, g2, b2, wr, br):
    B, S, D = x1.shape
    ts = MEM_TILE
    per = ts // MOE_TILE
    nt = S // ts
    c = wr * (2.0 ** 16 + 1.0)
    w_high = c - (c - wr)
    wrh = w_high.astype(BF16)
    wr2 = jnp.concatenate([wrh, (wr - w_high).astype(BF16)], axis=1)
    tile = pl.BlockSpec((1, ts, D), lambda b, i: (b, i, 0))
    kspec = pl.BlockSpec((1, MEM_LEN, D), lambda b, i: (b, 0, 0))
    vspec = pl.BlockSpec((1, MEM_LEN, D), lambda b, i: (b, 0, 1))
    return pl.pallas_call(
        _memattn_kernel,
        grid=(B, nt),
        in_specs=[tile, _resident_spec((D, D)), kspec, vspec, _resident_spec((D, D)),
                  _const_spec((1, D)), _const_spec((1, D)),
                  _const_spec((D, 2 * LANES)), _const_spec((D, LANES)), _const_spec((1, LANES))],
        out_specs=[tile, tile, pl.BlockSpec((1, ts, LANES), lambda b, i: (b, i, 0)),
                   pl.BlockSpec((per, SUBLANES, MOE_TILE), lambda b, i: (b * nt + i, 0, 0)),
                   pl.BlockSpec((per, 2 * N_EXPERTS, LANES), lambda b, i: (b * nt + i, 0, 0))],
        out_shape=[jax.ShapeDtypeStruct((B, S, D), F32),
                   jax.ShapeDtypeStruct((B, S, D), BF16),
                   jax.ShapeDtypeStruct((B, S, LANES), F32),
                   jax.ShapeDtypeStruct((B * nt * per, SUBLANES, MOE_TILE), F32),
                   jax.ShapeDtypeStruct((B * nt * per, 2 * N_EXPERTS, LANES), jnp.int32)],
        scratch_shapes=[pltpu.VMEM((D, D), BF16), pltpu.VMEM((D, D), BF16)],
        compiler_params=pltpu.CompilerParams(
            dimension_semantics=("arbitrary", "arbitrary"), vmem_limit_bytes=VMEM_LIMIT),
        name="memattn",
    )(x1, wq, kvm, kvm, wo, g2, b2, wr2, wrh, br)


def _gmm_blocks(n_tokens):
    rows = (n_tokens * TOP_K + (n_tokens // MOE_TILE) * N_EXPERTS * (CHUNK - 1)
            + N_EXPERTS * (GMM_ROWS - CHUNK))
    return -(-rows // GMM_ROWS)


def _plan(meta, T):
    nch = meta[:, :N_EXPERTS, 0]
    n16 = nch * CHUNK
    n_e = jnp.sum(n16, axis=0)
    reg = (n_e + GMM_ROWS - 1) // GMM_ROWS * GMM_ROWS
    gend = jnp.cumsum(reg)
    gbase = gend - reg
    dst = gbase[None, :] + jnp.cumsum(n16, axis=0) - n16
    blk_row = jnp.arange(_gmm_blocks(T), dtype=jnp.int32)[:, None] * GMM_ROWS
    blk_expert = jnp.sum(blk_row >= gend[None, :], axis=1)
    blk_expert = jnp.minimum(blk_expert, N_EXPERTS - 1).astype(jnp.int32)
    n_used = (gend[-1] // GMM_ROWS).astype(jnp.int32).reshape(1)
    data_end = jnp.sum(jnp.where(blk_expert[:, None] == jnp.arange(N_EXPERTS)[None, :],
                                 (gbase + n_e)[None, :], 0), axis=1)
    blk_rows = jnp.where(blk_row[:, 0] < gend[-1], jnp.clip(data_end - blk_row[:, 0], 0, GMM_ROWS), 0)
    src = (jnp.cumsum(nch, axis=1) - nch) * CHUNK
    n_big = nch // BIG

    def copy_list(count, src0, dst0, rows, length):
        cum = jnp.cumsum(count, axis=1)
        first = (cum - count)[:, None, :]
        k = jnp.arange(length, dtype=jnp.int32)[None, :, None]
        mine = (k >= first) & (k < cum[:, None, :])
        step = (k - first) * rows
        pick = lambda base: jnp.sum(jnp.where(mine, base[:, None, :] + step, 0), axis=2)
        return pick(src0), pick(dst0), cum[:, -1]

    big_src, big_dst, big_n = copy_list(n_big, src, dst, BIG * CHUNK, MAX_BIG)
    rest = n_big * (BIG * CHUNK)
    small_src, small_dst, small_n = copy_list(nch - n_big * BIG, src + rest, dst + rest, CHUNK, MAX_SMALL)
    i32 = lambda a: a.astype(jnp.int32)
    fill_start = jnp.concatenate([gbase + n_e, gend[-1:]])
    fill_n = jnp.concatenate([reg - n_e, _gmm_blocks(T) * GMM_ROWS - gend[-1:]]) // CHUNK
    return dict(big_src=i32(big_src).reshape(-1), big_dst=i32(big_dst).reshape(-1), big_n=i32(big_n),
                small_src=i32(small_src).reshape(-1), small_dst=i32(small_dst).reshape(-1), small_n=i32(small_n),
                fill_start=i32(fill_start), fill_n=i32(fill_n),
                fill_tot=i32(jnp.sum(fill_n)).reshape(1), blk_expert=blk_expert, n_used=n_used,
                blk_rows=i32(blk_rows))


def _rows_copy(src_ref, src_row, dst_ref, dst_row, rows, sem):
    return pltpu.make_async_copy(
        src_ref.at[pl.ds(pl.multiple_of(src_row, CHUNK), rows), :],
        dst_ref.at[pl.ds(pl.multiple_of(dst_row, CHUNK), rows), :], sem)


def _chunk_copy(src_ref, src_row, dst_ref, dst_row, sem):
    return _rows_copy(src_ref, src_row, dst_ref, dst_row, CHUNK, sem)


def _issue_copies(lists, tile, tile_ref, tile_is_src, hbm_ref, sem):
    big_src, big_dst, big_n, small_src, small_dst, small_n = lists
    for src_l, dst_l, n_l, length, rows in ((big_src, big_dst, big_n, MAX_BIG, BIG * CHUNK),
                                            (small_src, small_dst, small_n, MAX_SMALL, CHUNK)):
        def issue(k, carry, src_l=src_l, dst_l=dst_l, length=length, rows=rows):
            local, remote = src_l[tile * length + k], dst_l[tile * length + k]
            if tile_is_src:
                _rows_copy(tile_ref, local, hbm_ref, remote, rows, sem).start()
            else:
                _rows_copy(hbm_ref, remote, tile_ref, local, rows, sem).start()
            return carry

        lax.fori_loop(0, n_l[tile], issue, 0)


def _wait_rows(n, rows, src_ref, dst_ref, sem):
    def body(c, carry):
        pltpu.make_async_copy(src_ref.at[pl.ds(0, rows), :], dst_ref.at[pl.ds(0, rows), :], sem).wait()
        return carry

    lax.fori_loop(0, n, body, 0)


def _wait_copies(lists, tile, src_ref, dst_ref, sem, extra_chunks=0):
    _wait_rows(lists[2][tile], BIG * CHUNK, src_ref, dst_ref, sem)
    n = lists[5][tile] + extra_chunks
    _wait_rows(n // WAIT_GROUP, WAIT_GROUP * CHUNK, src_ref, dst_ref, sem)
    _wait_rows(n % WAIT_GROUP, CHUNK, src_ref, dst_ref, sem)


def _dispatch_kernel(bs_ref, bd_ref, bn_ref, ss_ref, sd_ref, sn_ref, fstart_ref, fn_ref, ftot_ref,
                     lp_ref, x_ref, xs_hbm, xt, zbuf, sems):
    lists = (bs_ref, bd_ref, bn_ref, ss_ref, sd_ref, sn_ref)
    t = pl.program_id(0)
    nt = pl.num_programs(0)
    slot = t % 2
    xt_s = xt.at[slot]
    sem = sems.at[slot]

    @pl.when(t >= 2)
    def _():
        _wait_copies(lists, t - 2, xt_s, xs_hbm, sem)

    lp = lp_ref[0].astype(jnp.int32)
    r = lax.broadcasted_iota(jnp.int32, (SORT_ROWS, MOE_TILE), 0)
    hit = jnp.logical_or(lp[0:1, :] == r, lp[1:2, :] == r)
    p = jnp.where(hit, 1.0, 0.0).astype(BF16)
    xt_s[...] = jnp.dot(p, x_ref[...], preferred_element_type=F32).astype(BF16)

    _issue_copies(lists, t, xt_s, True, xs_hbm, sem)

    @pl.when(t == nt - 1)
    def _():
        zbuf[...] = jnp.zeros_like(zbuf)

        def per_range(e, carry):
            def issue(c, carry2):
                _chunk_copy(zbuf, 0, xs_hbm, fstart_ref[e] + c * CHUNK, sem).start()
                return carry2

            return lax.fori_loop(0, fn_ref[e], issue, carry)

        lax.fori_loop(0, N_EXPERTS + 1, per_range, 0)
        _wait_copies(lists, t - 1, xt.at[1 - slot], xs_hbm, sems.at[1 - slot])
        _wait_copies(lists, t, xt_s, xs_hbm, sem, extra_chunks=ftot_ref[0])


def _copy_lists(plan):
    return tuple(plan[k] for k in ("big_src", "big_dst", "big_n", "small_src", "small_dst", "small_n"))


def _dispatch(plan, lpt, x2b):
    T, D = x2b.shape
    nt = T // MOE_TILE
    assert nt >= 2
    grid_spec = pltpu.PrefetchScalarGridSpec(
        num_scalar_prefetch=9,
        grid=(nt,),
        in_specs=[pl.BlockSpec((1, SUBLANES, MOE_TILE), lambda t, *_: (t, 0, 0)),
                  pl.BlockSpec((MOE_TILE, D), lambda t, *_: (t, 0))],
        out_specs=pl.BlockSpec(memory_space=pl.ANY),
        scratch_shapes=[pltpu.VMEM((2, SORT_ROWS, D), BF16), pltpu.VMEM((CHUNK, D), BF16),
                        pltpu.SemaphoreType.DMA((2,))],
    )
    return pl.pallas_call(
        _dispatch_kernel,
        grid_spec=grid_spec,
        out_shape=jax.ShapeDtypeStruct((_gmm_blocks(T) * GMM_ROWS, D), BF16),
        compiler_params=pltpu.CompilerParams(
            dimension_semantics=("arbitrary",), vmem_limit_bytes=VMEM_LIMIT),
        name="moe_dispatch",
    )(*_copy_lists(plan), plan["fill_start"], plan["fill_n"], plan["fill_tot"], lpt, x2b)


def _expert_ffn(xb, wg_b, wu_b, wd_b):
    g = jnp.dot(xb, wg_b[...], preferred_element_type=F32)
    u = jnp.dot(xb, wu_b[...], preferred_element_type=F32)
    h = (g * jax.nn.sigmoid(g)) * u
    return jnp.dot(h.astype(BF16), wd_b[...], preferred_element_type=F32).astype(BF16)


def _gmm_kernel(be_ref, nu_ref, br_ref, x_ref, wg_ref, wu_ref, wd_ref, o_ref, wg_b, wu_b, wd_b):
    b = pl.program_id(0)
    rows = br_ref[b]
    half = GMM_ROWS // 2

    @pl.when(jnp.logical_or(b == 0, be_ref[b] != be_ref[jnp.maximum(b - 1, 0)]))
    def _():
        _cast_bf16(wg_b, wg_ref.at[0])
        _cast_bf16(wu_b, wu_ref.at[0])
        _cast_bf16(wd_b, wd_ref.at[0])

    @pl.when(rows > half)
    def _():
        o_ref[...] = _expert_ffn(x_ref[...], wg_b, wu_b, wd_b)

    @pl.when(jnp.logical_and(rows > 0, rows <= half))
    def _():
        o_ref[0:half, :] = _expert_ffn(x_ref[0:half, :], wg_b, wu_b, wd_b)
        o_ref[half:GMM_ROWS, :] = jnp.zeros((GMM_ROWS - half, o_ref.shape[1]), o_ref.dtype)

    @pl.when(rows == 0)
    def _():
        o_ref[...] = jnp.zeros_like(o_ref)


def _gmm(plan, xs, wg, wu, wd):
    R, D = xs.shape
    rows = pl.BlockSpec((GMM_ROWS, D), lambda b, be, nu, br: (jnp.minimum(b, nu[0] - 1), 0))
    grid_spec = pltpu.PrefetchScalarGridSpec(
        num_scalar_prefetch=3,
        grid=(R // GMM_ROWS,),
        in_specs=[rows,
                  pl.BlockSpec((1, D, D_EXPERT), lambda b, be, nu, br: (be[b], 0, 0)),
                  pl.BlockSpec((1, D, D_EXPERT), lambda b, be, nu, br: (be[b], 0, 0)),
                  pl.BlockSpec((1, D_EXPERT, D), lambda b, be, nu, br: (be[b], 0, 0))],
        out_specs=pl.BlockSpec((GMM_ROWS, D), lambda b, be, nu, br: (b, 0)),
        scratch_shapes=[pltpu.VMEM((D, D_EXPERT), BF16), pltpu.VMEM((D, D_EXPERT), BF16),
                        pltpu.VMEM((D_EXPERT, D), BF16)],
    )
    return pl.pallas_call(
        _gmm_kernel,
        grid_spec=grid_spec,
        out_shape=jax.ShapeDtypeStruct((R, D), BF16),
        compiler_params=pltpu.CompilerParams(
            dimension_semantics=("arbitrary",), vmem_limit_bytes=VMEM_LIMIT),
        name="moe_gmm",
    )(plan["blk_expert"], plan["n_used"], plan["blk_rows"], xs, wg, wu, wd)


def _combine_kernel(bs_ref, bd_ref, bn_ref, ss_ref, sd_ref, sn_ref, cm_ref, x_ref, ys_hbm, g3_ref, b3_ref,
                    o_ref, yt, sems):
    lists = (bs_ref, bd_ref, bn_ref, ss_ref, sd_ref, sn_ref)
    t = pl.program_id(0)
    nt = pl.num_programs(0)
    slot = t % 2

    def fetch(tile, s):
        _issue_copies(lists, tile, yt.at[s], False, ys_hbm, sems.at[s])

    @pl.when(t == 0)
    def _():
        yt[...] = jnp.zeros_like(yt)
        fetch(0, 0)

    @pl.when(t + 1 < nt)
    def _():
        fetch(t + 1, 1 - slot)

    cm = cm_ref[...]
    col = lax.broadcasted_iota(jnp.int32, (MOE_TILE, SORT_ROWS), 1)
    lp0 = cm[:, 0:1].astype(jnp.int32)
    lp1 = cm[:, 1:2].astype(jnp.int32)
    w = jnp.where(col == lp0, cm[:, 2:3], jnp.where(col == lp1, cm[:, 3:4], 0.0)).astype(BF16)

    _wait_copies(lists, t, ys_hbm, yt.at[slot], sems.at[slot])
    y = jnp.dot(w, yt[slot], preferred_element_type=F32)
    o_ref[...] = _layer_norm(ALPHA * x_ref[...] + y, g3_ref[...], b3_ref[...])


def _combine(plan, route, x2, ys, g3, b3):
    T, D = x2.shape
    nt = T // MOE_TILE
    grid_spec = pltpu.PrefetchScalarGridSpec(
        num_scalar_prefetch=6,
        grid=(nt,),
        in_specs=[pl.BlockSpec((MOE_TILE, LANES), lambda t, *_: (t, 0)),
                  pl.BlockSpec((MOE_TILE, D), lambda t, *_: (t, 0)),
                  pl.BlockSpec(memory_space=pl.ANY),
                  pl.BlockSpec((1, D), lambda t, *_: (0, 0)),
                  pl.BlockSpec((1, D), lambda t, *_: (0, 0))],
        out_specs=pl.BlockSpec((MOE_TILE, D), lambda t, *_: (t, 0)),
        scratch_shapes=[pltpu.VMEM((2, SORT_ROWS, D), BF16), pltpu.SemaphoreType.DMA((2,))],
    )
    return pl.pallas_call(
        _combine_kernel,
        grid_spec=grid_spec,
        out_shape=jax.ShapeDtypeStruct((T, D), F32),
        compiler_params=pltpu.CompilerParams(
            dimension_semantics=("arbitrary",), vmem_limit_bytes=VMEM_LIMIT),
        name="moe_combine",
    )(*_copy_lists(plan), route, x2, ys, g3, b3)


def _moe(x2, x2b, route, lpt, meta, wg, wu, wd, g3, b3):
    plan = _plan(meta, x2.shape[0])
    xs = _dispatch(plan, lpt, x2b)
    ys = _gmm(plan, xs, wg, wu, wd)
    return _combine(plan, route, x2, ys, g3, b3)


def _row(v):
    return v.reshape(1, -1).astype(F32)


def kernel(x, mem, w_in, b_in, w_dw, b_dw, g_conv_norm, b_conv_norm, attn_sinks, w_out, g_ln1, b_ln1,
           w_mq, w_mkv, w_mo, g_ln2, b_ln2, w_group, b_group, w_router, b_router, w_gate, w_up, w_down,
           g_ln3, b_ln3):
    B, S, D = x.shape
    for l in range(DEPTH):
        w_dw_p = jnp.zeros((CONV_HALO, CONV_CH), F32).at[:CONV_WIDTH].set(w_dw[l])
        x1 = _mixer(x, attn_sinks[l].astype(F32), w_in[l], _row(b_in[l]), w_dw_p,
                    _row(b_dw[l]), _row(g_conv_norm[l]), _row(b_conv_norm[l]),
                    w_out[l], _row(g_ln1[l]), _row(b_ln1[l]))

        kvm = _memkv(mem.reshape(B * MEM_LEN, D), w_mkv[l]).reshape(B, MEM_LEN, 2 * D)

        wr = jnp.concatenate(
            [w_group[l], jnp.transpose(w_router[l], (1, 0, 2)).reshape(D, N_EXPERTS)], axis=1)
        wr = jnp.pad(wr, ((0, 0), (0, LANES - wr.shape[1])))
        br = jnp.pad(jnp.concatenate([b_group[l], b_router[l].reshape(-1)]), (0, LANES - N_GROUPS - N_EXPERTS))
        x2, x2b, route, lpt, meta = _memattn(x1, w_mq[l], kvm, w_mo[l],
                                 _row(g_ln2[l]), _row(b_ln2[l]), wr.astype(F32), _row(br))

        T = B * S
        y = _moe(x2.reshape(T, D), x2b.reshape(T, D), route.reshape(T, LANES), lpt, meta,
                 w_gate[l], w_up[l], w_down[l],
                 _row(g_ln3[l]), _row(b_ln3[l]))
        x = y.reshape(B, S, D)
    return x
```

```python
import functools

import jax
import jax.numpy as jnp
from jax import lax
from jax.experimental import pallas as pl
from jax.experimental.pallas import tpu as pltpu

D_MODEL = 1024
MEM_LEN = 256
CONV_CH = 512
CONV_WIDTH = 31
N_HEADS = 8
N_KV_HEADS = 2
HEAD_DIM = 64
GQ = N_HEADS // N_KV_HEADS
ATTN_W = N_HEADS * HEAD_DIM
KV_W = N_KV_HEADS * HEAD_DIM
WINDOW = 128
D_MIX = CONV_CH + ATTN_W
D_IN = 2 * CONV_CH + ATTN_W + 2 * KV_W
MEM_HEADS = 4
MEM_HEAD_DIM = D_MODEL // MEM_HEADS
N_GROUPS = 4
EXPERTS_PER_GROUP = 4
N_EXPERTS = N_GROUPS * EXPERTS_PER_GROUP
D_EXPERT = D_MODEL // 2
DEPTH = 1
ALPHA = (2.0 * DEPTH) ** 0.25
LN_EPS = 1e-5

LANES = 128
SUBLANES = 8
CONV_ROWS = 128
LN_ROWS = 64
MEM_TILE = 1024
MASK_VALUE = -1e30
CONV_HALO = 32
SEQ_TILE = 512
MOE_TILE = 512
CHUNK = 16
GMM_ROWS = 512
TOP_K = 2
SORT_ROWS = -(-(MOE_TILE * TOP_K + N_EXPERTS * (CHUNK - 1)) // 256) * 256
BIG = 4
MAX_BIG = SORT_ROWS // (BIG * CHUNK)
MAX_SMALL = N_EXPERTS * (BIG - 1)
PER_STEP = 2
WAIT_GROUP = 8
ROUTE_OFF = N_GROUPS
VMEM_LIMIT = 56 * 1024 * 1024

BF16 = jnp.bfloat16
F32 = jnp.float32


def _layer_norm(x, g, b):
    mu = jnp.mean(x, axis=-1, keepdims=True)
    xc = x - mu
    var = jnp.mean(xc * xc, axis=-1, keepdims=True)
    return xc * lax.rsqrt(var + LN_EPS) * g + b


def _cast_bf16(dst_ref, src_ref):
    rows = 256
    for r0 in range(0, src_ref.shape[0], rows):
        dst_ref[r0:r0 + rows, :] = src_ref[r0:r0 + rows, :].astype(BF16)


def _dot_nt(a, b):
    return lax.dot_general(a, b, (((1,), (1,)), ((), ())), preferred_element_type=F32)


def _mixer_kernel(sinks_ref, x_ref, w_in_ref, b_in_ref, w_dw_ref, b_dw_ref, g_cn_ref, b_cn_ref,
                  w_out_ref, g1_ref, b1_ref, o_ref, w_in_b, w_out_b, hbuf, hshift, cbuf, qbuf, kbuf, vbuf, ymix):
    i = pl.program_id(1)
    ts = SEQ_TILE

    @pl.when(jnp.logical_and(pl.program_id(0) == 0, i == 0))
    def _():
        _cast_bf16(w_in_b, w_in_ref)
        _cast_bf16(w_out_b, w_out_ref)

    @pl.when(i == 0)
    def _():
        hbuf[0:CONV_HALO, :] = jnp.zeros((CONV_HALO, CONV_CH), F32)
        kbuf[:, 0:WINDOW, :] = jnp.zeros((2 * N_KV_HEADS, WINDOW, KV_W), BF16)
        vbuf[:, 0:WINDOW, :] = jnp.zeros((2, WINDOW, KV_W), BF16)

    x = x_ref[0]
    u = jnp.dot(x.astype(BF16), w_in_b[...], preferred_element_type=F32) + b_in_ref[...]
    a = u[:, 0:CONV_CH]
    gate = u[:, CONV_CH:2 * CONV_CH]
    hbuf[CONV_HALO:CONV_HALO + ts, :] = a * jax.nn.sigmoid(gate)
    qbuf[...] = (u[:, 2 * CONV_CH:2 * CONV_CH + ATTN_W] * (HEAD_DIM ** -0.5)).astype(BF16)
    kf = u[:, 2 * CONV_CH + ATTN_W:2 * CONV_CH + ATTN_W + KV_W]
    vf = u[:, 2 * CONV_CH + ATTN_W + KV_W:D_IN]
    kr = pltpu.roll(kf, HEAD_DIM, axis=1)
    vr = pltpu.roll(vf, HEAD_DIM, axis=1)
    lo = lax.broadcasted_iota(jnp.int32, (ts, KV_W), 1) < HEAD_DIM
    rows = slice(WINDOW, WINDOW + ts)
    kbuf[0, rows, :] = jnp.where(lo, kf, 0.0).astype(BF16)
    kbuf[1, rows, :] = jnp.where(lo, 0.0, kr).astype(BF16)
    kbuf[2, rows, :] = jnp.where(lo, kr, 0.0).astype(BF16)
    kbuf[3, rows, :] = jnp.where(lo, 0.0, kf).astype(BF16)
    vbuf[0, rows, :] = vf.astype(BF16)
    vbuf[1, rows, :] = vr.astype(BF16)

    base = CONV_HALO - (CONV_WIDTH - 1)
    n_shift = ts + CONV_HALO - SUBLANES
    for b in range(1, SUBLANES):
        hshift[b - 1, 0:n_shift, :] = hbuf[b:b + n_shift, :]
    rc = CONV_ROWS

    def conv_chunk(c):
        r0 = c * rc
        for l in range(CONV_CH // LANES):
            ls = slice(l * LANES, (l + 1) * LANES)
            acc = jnp.zeros((rc, LANES), F32)
            for j in range(CONV_WIDTH):
                a8, b = divmod(j + base, SUBLANES)
                rs = slice(r0 + SUBLANES * a8, r0 + SUBLANES * a8 + rc)
                tap = hbuf[rs, ls] if b == 0 else hshift[b - 1, rs, ls]
                acc = acc + tap * w_dw_ref[j:j + 1, ls]
            cbuf[r0:r0 + rc, ls] = acc
        for r1 in range(r0, r0 + rc, LN_ROWS):
            rs = slice(r1, r1 + LN_ROWS)
            y = _layer_norm(cbuf[rs, :] + b_dw_ref[...], g_cn_ref[...], b_cn_ref[...])
            y = y * jax.nn.sigmoid(y)
            ymix[rs, 0:CONV_CH] = y.astype(BF16)

    qi = lax.broadcasted_iota(jnp.int32, (2 * WINDOW, 2 * WINDOW), 0) % WINDOW
    kj = lax.broadcasted_iota(jnp.int32, (2 * WINDOW, 2 * WINDOW), 1)
    dist = qi + WINDOW - kj
    band = (dist >= 0) & (dist < WINDOW)
    top = lax.broadcasted_iota(jnp.int32, (2 * WINDOW, 1), 0) < WINDOW
    lo_out = lax.broadcasted_iota(jnp.int32, (WINDOW, 2 * HEAD_DIM), 1) < HEAD_DIM

    def attn_block(jb):
        r0 = jb * WINDOW
        valid = band & jnp.logical_or(i != 0, kj >= WINDOW) if jb == 0 else band
        for kvh in range(N_KV_HEADS):
            h0 = kvh * GQ
            c0 = h0 * HEAD_DIM
            qs = jnp.concatenate([qbuf[r0:r0 + WINDOW, c0:c0 + 2 * HEAD_DIM],
                                  qbuf[r0:r0 + WINDOW, c0 + 2 * HEAD_DIM:c0 + 4 * HEAD_DIM]], axis=0)
            pv = []
            for par in range(2):
                kk = kbuf[2 * kvh + par, r0:r0 + 2 * WINDOW, :]
                vv = vbuf[(kvh + par) % 2, r0:r0 + 2 * WINDOW, :]
                s = jnp.where(valid, _dot_nt(qs, kk), MASK_VALUE)
                sink = jnp.where(top, sinks_ref[h0 + par], sinks_ref[h0 + 2 + par])
                m = jnp.maximum(jnp.max(s, axis=-1, keepdims=True), sink)
                p = jnp.exp(s - m)
                denom = jnp.sum(p, axis=-1, keepdims=True) + jnp.exp(sink - m)
                pv.append(jnp.dot(p.astype(BF16), vv, preferred_element_type=F32) / denom)
            for pair in range(2):
                rs = slice(pair * WINDOW, (pair + 1) * WINDOW)
                o = jnp.where(lo_out, pv[0][rs], pv[1][rs])
                cs = CONV_CH + c0 + pair * 2 * HEAD_DIM
                ymix[r0:r0 + WINDOW, cs:cs + 2 * HEAD_DIM] = o.astype(BF16)

    assert ts // rc == ts // WINDOW
    for c in range(ts // rc):
        attn_block(c)
        conv_chunk(c)

    mix = jnp.dot(ymix[...], w_out_b[...], preferred_element_type=F32)
    o_ref[0] = _layer_norm(ALPHA * x + mix, g1_ref[...], b1_ref[...])

    hbuf[0:CONV_HALO, :] = hbuf[ts:ts + CONV_HALO, :]
    kbuf[:, 0:WINDOW, :] = kbuf[:, ts:ts + WINDOW, :]
    vbuf[:, 0:WINDOW, :] = vbuf[:, ts:ts + WINDOW, :]


def _const_spec(shape):
    nd = len(shape)
    return pl.BlockSpec(shape, lambda *_: (0,) * nd)


def _resident_spec(shape):
    nd = len(shape)
    return pl.BlockSpec(shape, lambda *_: (0,) * nd, pipeline_mode=pl.Buffered(1))


def _mixer(x, sinks, w_in, b_in, w_dw, b_dw, g_cn, b_cn, w_out, g1, b1):
    B, S, D = x.shape
    ts = SEQ_TILE
    tile = pl.BlockSpec((1, ts, D), lambda b, i: (b, i, 0))
    return pl.pallas_call(
        _mixer_kernel,
        grid=(B, S // ts),
        in_specs=[
            pl.BlockSpec(memory_space=pltpu.SMEM),
            tile,
            _resident_spec((D, D_IN)), _const_spec((1, D_IN)),
            _const_spec((CONV_HALO, CONV_CH)), _const_spec((1, CONV_CH)),
            _const_spec((1, CONV_CH)), _const_spec((1, CONV_CH)),
            _resident_spec((D_MIX, D)), _const_spec((1, D)), _const_spec((1, D)),
        ],
        out_specs=tile,
        out_shape=jax.ShapeDtypeStruct((B, S, D), F32),
        scratch_shapes=[
            pltpu.VMEM((D, D_IN), BF16),
            pltpu.VMEM((D_MIX, D), BF16),
            pltpu.VMEM((CONV_HALO + ts, CONV_CH), F32),
            pltpu.VMEM((SUBLANES - 1, CONV_HALO + ts, CONV_CH), F32),
            pltpu.VMEM((ts, CONV_CH), F32),
            pltpu.VMEM((ts, ATTN_W), BF16),
            pltpu.VMEM((2 * N_KV_HEADS, WINDOW + ts, KV_W), BF16),
            pltpu.VMEM((2, WINDOW + ts, KV_W), BF16),
            pltpu.VMEM((ts, D_MIX), BF16),
        ],
        compiler_params=pltpu.CompilerParams(
            dimension_semantics=("arbitrary", "arbitrary"), vmem_limit_bytes=VMEM_LIMIT),
        name="mixer",
    )(sinks, x, w_in, b_in, w_dw, b_dw, g_cn, b_cn, w_out, g1, b1)


def _memkv_kernel(mem_ref, w_ref, o_ref):
    o_ref[...] = jnp.dot(mem_ref[...].astype(BF16), w_ref[...].astype(BF16),
                         preferred_element_type=F32).astype(BF16)


def _memkv(mem2d, w_mkv):
    M, D = mem2d.shape
    N = w_mkv.shape[1]
    tn = 512
    return pl.pallas_call(
        _memkv_kernel,
        grid=(N // tn,),
        in_specs=[pl.BlockSpec((M, D), lambda j: (0, 0)), pl.BlockSpec((D, tn), lambda j: (0, j))],
        out_specs=pl.BlockSpec((M, tn), lambda j: (0, j)),
        out_shape=jax.ShapeDtypeStruct((M, N), BF16),
        compiler_params=pltpu.CompilerParams(dimension_semantics=("arbitrary",)),
        name="memkv",
    )(mem2d, w_mkv)


def _first_max(rows):
    best = rows[0]
    for r in rows[1:]:
        best = jnp.maximum(best, r)
    idx = jnp.full(best.shape, len(rows) - 1, jnp.int32)
    for k in range(len(rows) - 2, -1, -1):
        idx = jnp.where(rows[k] == best, k, idx)
    return best, idx


def _route_plan(logits_t):
    tile = logits_t.shape[1]
    row = lambda k: logits_t[k:k + 1, :]
    gmax, g_idx = _first_max([row(g) for g in range(N_GROUPS)])
    gsum = jnp.exp(row(0) - gmax)
    for g in range(1, N_GROUPS):
        gsum = gsum + jnp.exp(row(g) - gmax)
    g_p = 1.0 / gsum
    rl = []
    for e in range(EXPERTS_PER_GROUP):
        v = row(ROUTE_OFF + (N_GROUPS - 1) * EXPERTS_PER_GROUP + e)
        for g in range(N_GROUPS - 2, -1, -1):
            v = jnp.where(g_idx == g, row(ROUTE_OFF + g * EXPERTS_PER_GROUP + e), v)
        rl.append(v)
    m1, i1 = _first_max(rl)
    m2, i2 = _first_max([jnp.where(i1 == e, MASK_VALUE, rl[e]) for e in range(EXPERTS_PER_GROUP)])
    ex = jnp.exp(m2 - m1)
    w1 = 1.0 / (1.0 + ex)
    w2 = ex * w1
    e1 = g_idx * EXPERTS_PER_GROUP + i1
    e2 = g_idx * EXPERTS_PER_GROUP + i2

    eid = lax.broadcasted_iota(jnp.int32, (N_EXPERTS, tile), 0)
    hit1 = eid == e1
    hit2 = eid == e2
    oh = jnp.where(jnp.logical_or(hit1, hit2), 1.0, 0.0)
    r = lax.broadcasted_iota(jnp.int32, (tile, tile), 0)
    c = lax.broadcasted_iota(jnp.int32, (tile, tile), 1)
    tri = jnp.where(r <= c, 1.0, 0.0).astype(BF16)
    csum = jnp.dot(oh.astype(BF16), tri, preferred_element_type=F32)
    counts = jnp.broadcast_to(csum[:, tile - 1:tile], (N_EXPERTS, tile)).astype(jnp.int32)
    nch = jnp.right_shift(counts + (CHUNK - 1), CHUNK.bit_length() - 1)
    er = lax.broadcasted_iota(jnp.int32, (N_EXPERTS, N_EXPERTS), 0)
    ec = lax.broadcasted_iota(jnp.int32, (N_EXPERTS, N_EXPERTS), 1)
    lower = jnp.where(ec < er, 1.0, 0.0).astype(BF16)
    off = jnp.dot(lower, nch.astype(F32).astype(BF16), preferred_element_type=F32) * CHUNK
    pos = off + csum - oh
    lp1 = jnp.sum(jnp.where(hit1, pos, 0.0), axis=0, keepdims=True)
    lp2 = jnp.sum(jnp.where(hit2, pos, 0.0), axis=0, keepdims=True)
    zero = jnp.zeros_like(lp1)
    route_t = jnp.concatenate([lp1, lp2, g_p * w1, g_p * w2, zero, zero, zero, zero], axis=0)
    meta = jnp.concatenate([nch[:, 0:LANES], off[:, 0:LANES].astype(jnp.int32)], axis=0)
    return route_t, meta


def _memattn_kernel(x_ref, wq_ref, k_ref, v_ref, wo_ref, g2_ref, b2_ref, wr2_ref, wrh_ref, br_ref,
                    o_ref, ob_ref, route_ref, routet_ref, meta_ref, wq_b, wo_b):
    @pl.when(jnp.logical_and(pl.program_id(0) == 0, pl.program_id(1) == 0))
    def _():
        _cast_bf16(wq_b, wq_ref)
        _cast_bf16(wo_b, wo_ref)

    def rows_logits(rs):
        x = x_ref[0, rs, :]
        q = jnp.dot(x.astype(BF16), wq_b[...], preferred_element_type=F32)
        q = (q * (MEM_HEAD_DIM ** -0.5)).astype(BF16)
        outs = []
        for h in range(MEM_HEADS):
            sl = slice(h * MEM_HEAD_DIM, (h + 1) * MEM_HEAD_DIM)
            s = _dot_nt(q[:, sl], k_ref[0, :, sl])
            m = jnp.max(s, axis=-1, keepdims=True)
            p = jnp.exp(s - m)
            denom = jnp.sum(p, axis=-1, keepdims=True)
            o = jnp.dot(p.astype(BF16), v_ref[0, :, sl], preferred_element_type=F32)
            outs.append((o / denom).astype(BF16))
        o = jnp.dot(jnp.concatenate(outs, axis=-1), wo_b[...], preferred_element_type=F32)
        x2 = _layer_norm(ALPHA * x + o, g2_ref[...], b2_ref[...])
        o_ref[0, rs, :] = x2
        x2h = x2.astype(BF16)
        ob_ref[0, rs, :] = x2h
        x2l = (x2 - x2h.astype(F32)).astype(BF16)
        hh = jnp.dot(x2h, wr2_ref[...], preferred_element_type=F32)
        return (hh[:, 0:LANES] + hh[:, LANES:2 * LANES]
                + jnp.dot(x2l, wrh_ref[...], preferred_element_type=F32) + br_ref[...])

    ts = x_ref.shape[1]
    groups = [slice(r0, r0 + MOE_TILE) for r0 in range(0, ts, MOE_TILE)]
    logits = [rows_logits(rs) for rs in groups]
    pad = jnp.zeros((LANES - SUBLANES, MOE_TILE), F32)
    for k, rs in enumerate(groups):
        route_t, meta = _route_plan(jnp.transpose(logits[k]))
        routet_ref[k] = route_t
        route_ref[0, rs, :] = jnp.transpose(jnp.concatenate([route_t, pad], axis=0))
        meta_ref[k] = meta


def _memattn(x1, wq, kvm, wo, g2, b2, wr, br):
    B, S, D = x1.shape
    ts = MEM_TILE
    per = ts // MOE_TILE
    nt = S // ts
    c = wr * (2.0 ** 16 + 1.0)
    w_high = c - (c - wr)
    wrh = w_high.astype(BF16)
    wr2 = jnp.concatenate([wrh, (wr - w_high).astype(BF16)], axis=1)
    tile = pl.BlockSpec((1, ts, D), lambda b, i: (b, i, 0))
    kspec = pl.BlockSpec((1, MEM_LEN, D), lambda b, i: (b, 0, 0))
    vspec = pl.BlockSpec((1, MEM_LEN, D), lambda b, i: (b, 0, 1))
    return pl.pallas_call(
        _memattn_kernel,
        grid=(B, nt),
        in_specs=[tile, _resident_spec((D, D)), kspec, vspec, _resident_spec((D, D)),
                  _const_spec((1, D)), _const_spec((1, D)),
                  _const_spec((D, 2 * LANES)), _const_spec((D, LANES)), _const_spec((1, LANES))],
        out_specs=[tile, tile, pl.BlockSpec((1, ts, LANES), lambda b, i: (b, i, 0)),
                   pl.BlockSpec((per, SUBLANES, MOE_TILE), lambda b, i: (b * nt + i, 0, 0)),
                   pl.BlockSpec((per, 2 * N_EXPERTS, LANES), lambda b, i: (b * nt + i, 0, 0))],
        out_shape=[jax.ShapeDtypeStruct((B, S, D), F32),
                   jax.ShapeDtypeStruct((B, S, D), BF16),
                   jax.ShapeDtypeStruct((B, S, LANES), F32),
                   jax.ShapeDtypeStruct((B * nt * per, SUBLANES, MOE_TILE), F32),
                   jax.ShapeDtypeStruct((B * nt * per, 2 * N_EXPERTS, LANES), jnp.int32)],
        scratch_shapes=[pltpu.VMEM((D, D), BF16), pltpu.VMEM((D, D), BF16)],
        compiler_params=pltpu.CompilerParams(
            dimension_semantics=("arbitrary", "arbitrary"), vmem_limit_bytes=VMEM_LIMIT),
        name="memattn",
    )(x1, wq, kvm, kvm, wo, g2, b2, wr2, wrh, br)


def _gmm_blocks(n_tokens):
    rows = (n_tokens * TOP_K + (n_tokens // MOE_TILE) * N_EXPERTS * (CHUNK - 1)
            + N_EXPERTS * (GMM_ROWS - CHUNK))
    return -(-rows // GMM_ROWS)


def _plan(meta, T):
    nch = meta[:, :N_EXPERTS, 0]
    n16 = nch * CHUNK
    n_e = jnp.sum(n16, axis=0)
    reg = (n_e + GMM_ROWS - 1) // GMM_ROWS * GMM_ROWS
    gend = jnp.cumsum(reg)
    gbase = gend - reg
    dst = gbase[None, :] + jnp.cumsum(n16, axis=0) - n16
    blk_row = jnp.arange(_gmm_blocks(T), dtype=jnp.int32)[:, None] * GMM_ROWS
    blk_expert = jnp.sum(blk_row >= gend[None, :], axis=1)
    blk_expert = jnp.minimum(blk_expert, N_EXPERTS - 1).astype(jnp.int32)
    n_used = (gend[-1] // GMM_ROWS).astype(jnp.int32).reshape(1)
    data_end = jnp.sum(jnp.where(blk_expert[:, None] == jnp.arange(N_EXPERTS)[None, :],
                                 (gbase + n_e)[None, :], 0), axis=1)
    blk_rows = jnp.where(blk_row[:, 0] < gend[-1], jnp.clip(data_end - blk_row[:, 0], 0, GMM_ROWS), 0)
    src = (jnp.cumsum(nch, axis=1) - nch) * CHUNK
    n_big = nch // BIG

    def copy_list(count, src0, dst0, rows, length):
        cum = jnp.cumsum(count, axis=1)
        first = (cum - count)[:, None, :]
        k = jnp.arange(length, dtype=jnp.int32)[None, :, None]
        mine = (k >= first) & (k < cum[:, None, :])
        step = (k - first) * rows
        pick = lambda base: jnp.sum(jnp.where(mine, base[:, None, :] + step, 0), axis=2)
        return pick(src0), pick(dst0), cum[:, -1]

    big_src, big_dst, big_n = copy_list(n_big, src, dst, BIG * CHUNK, MAX_BIG)
    rest = n_big * (BIG * CHUNK)
    small_src, small_dst, small_n = copy_list(nch - n_big * BIG, src + rest, dst + rest, CHUNK, MAX_SMALL)
    i32 = lambda a: a.astype(jnp.int32)
    fill_start = jnp.concatenate([gbase + n_e, gend[-1:]])
    fill_n = jnp.concatenate([reg - n_e, _gmm_blocks(T) * GMM_ROWS - gend[-1:]]) // CHUNK
    return dict(big_src=i32(big_src).reshape(-1), big_dst=i32(big_dst).reshape(-1), big_n=i32(big_n),
                small_src=i32(small_src).reshape(-1), small_dst=i32(small_dst).reshape(-1), small_n=i32(small_n),
                fill_start=i32(fill_start), fill_n=i32(fill_n),
                fill_tot=i32(jnp.sum(fill_n)).reshape(1), blk_expert=blk_expert, n_used=n_used,
                blk_rows=i32(blk_rows))


def _rows_copy(src_ref, src_row, dst_ref, dst_row, rows, sem):
    return pltpu.make_async_copy(
        src_ref.at[pl.ds(pl.multiple_of(src_row, CHUNK), rows), :],
        dst_ref.at[pl.ds(pl.multiple_of(dst_row, CHUNK), rows), :], sem)


def _chunk_copy(src_ref, src_row, dst_ref, dst_row, sem):
    return _rows_copy(src_ref, src_row, dst_ref, dst_row, CHUNK, sem)


def _issue_copies(lists, tile, tile_ref, tile_is_src, hbm_ref, sem):
    big_src, big_dst, big_n, small_src, small_dst, small_n = lists
    for src_l, dst_l, n_l, length, rows in ((big_src, big_dst, big_n, MAX_BIG, BIG * CHUNK),
                                            (small_src, small_dst, small_n, MAX_SMALL, CHUNK)):
        def issue(k, carry, src_l=src_l, dst_l=dst_l, length=length, rows=rows):
            local, remote = src_l[tile * length + k], dst_l[tile * length + k]
            if tile_is_src:
                _rows_copy(tile_ref, local, hbm_ref, remote, rows, sem).start()
            else:
                _rows_copy(hbm_ref, remote, tile_ref, local, rows, sem).start()
            return carry

        lax.fori_loop(0, n_l[tile], issue, 0)


def _wait_rows(n, rows, src_ref, dst_ref, sem):
    def body(c, carry):
        pltpu.make_async_copy(src_ref.at[pl.ds(0, rows), :], dst_ref.at[pl.ds(0, rows), :], sem).wait()
        return carry

    lax.fori_loop(0, n, body, 0)


def _wait_copies(lists, tile, src_ref, dst_ref, sem, extra_chunks=0):
    _wait_rows(lists[2][tile], BIG * CHUNK, src_ref, dst_ref, sem)
    n = lists[5][tile] + extra_chunks
    _wait_rows(n // WAIT_GROUP, WAIT_GROUP * CHUNK, src_ref, dst_ref, sem)
    _wait_rows(n % WAIT_GROUP, CHUNK, src_ref, dst_ref, sem)


def _dispatch_kernel(bs_ref, bd_ref, bn_ref, ss_ref, sd_ref, sn_ref, fstart_ref, fn_ref, ftot_ref,
                     lp_ref, x_ref, xs_hbm, xt, zbuf, sems):
    lists = (bs_ref, bd_ref, bn_ref, ss_ref, sd_ref, sn_ref)
    s = pl.program_id(0)
    ns = pl.num_programs(0)
    cur = (s % 2) * PER_STEP
    prv = PER_STEP - cur

    @pl.when(s >= 2)
    def _():
        for k in range(PER_STEP):
            _wait_copies(lists, (s - 2) * PER_STEP + k, xt.at[cur + k], xs_hbm, sems.at[cur + k])

    r = lax.broadcasted_iota(jnp.int32, (SORT_ROWS, MOE_TILE), 0)
    for k in range(PER_STEP):
        lp = lp_ref[k].astype(jnp.int32)
        hit = jnp.logical_or(lp[0:1, :] == r, lp[1:2, :] == r)
        p = jnp.where(hit, 1.0, 0.0).astype(BF16)
        x = x_ref[k * MOE_TILE:(k + 1) * MOE_TILE, :]
        xt[cur + k] = jnp.dot(p, x, preferred_element_type=F32).astype(BF16)

    for k in range(PER_STEP):
        _issue_copies(lists, s * PER_STEP + k, xt.at[cur + k], True, xs_hbm, sems.at[cur + k])

    @pl.when(s == ns - 1)
    def _():
        zbuf[...] = jnp.zeros_like(zbuf)
        sem = sems.at[cur]

        def per_range(e, carry):
            def issue(c, carry2):
                _chunk_copy(zbuf, 0, xs_hbm, fstart_ref[e] + c * CHUNK, sem).start()
                return carry2

            return lax.fori_loop(0, fn_ref[e], issue, carry)

        lax.fori_loop(0, N_EXPERTS + 1, per_range, 0)
        for k in range(PER_STEP):
            _wait_copies(lists, (s - 1) * PER_STEP + k, xt.at[prv + k], xs_hbm, sems.at[prv + k])
            _wait_copies(lists, s * PER_STEP + k, xt.at[cur + k], xs_hbm, sems.at[cur + k],
                         extra_chunks=ftot_ref[0] if k == 0 else 0)


def _copy_lists(plan):
    return tuple(plan[k] for k in ("big_src", "big_dst", "big_n", "small_src", "small_dst", "small_n"))


def _dispatch(plan, lpt, x2b):
    T, D = x2b.shape
    rows = PER_STEP * MOE_TILE
    assert T // rows >= 2
    grid_spec = pltpu.PrefetchScalarGridSpec(
        num_scalar_prefetch=9,
        grid=(T // rows,),
        in_specs=[pl.BlockSpec((PER_STEP, SUBLANES, MOE_TILE), lambda t, *_: (t, 0, 0)),
                  pl.BlockSpec((rows, D), lambda t, *_: (t, 0))],
        out_specs=pl.BlockSpec(memory_space=pl.ANY),
        scratch_shapes=[pltpu.VMEM((2 * PER_STEP, SORT_ROWS, D), BF16), pltpu.VMEM((CHUNK, D), BF16),
                        pltpu.SemaphoreType.DMA((2 * PER_STEP,))],
    )
    return pl.pallas_call(
        _dispatch_kernel,
        grid_spec=grid_spec,
        out_shape=jax.ShapeDtypeStruct((_gmm_blocks(T) * GMM_ROWS, D), BF16),
        compiler_params=pltpu.CompilerParams(
            dimension_semantics=("arbitrary",), vmem_limit_bytes=VMEM_LIMIT),
        name="moe_dispatch",
    )(*_copy_lists(plan), plan["fill_start"], plan["fill_n"], plan["fill_tot"], lpt, x2b)


def _expert_ffn(xb, wg_b, wu_b, wd_b):
    g = jnp.dot(xb, wg_b[...], preferred_element_type=F32)
    u = jnp.dot(xb, wu_b[...], preferred_element_type=F32)
    h = (g * jax.nn.sigmoid(g)) * u
    return jnp.dot(h.astype(BF16), wd_b[...], preferred_element_type=F32).astype(BF16)


def _gmm_kernel(be_ref, nu_ref, br_ref, x_ref, wg_ref, wu_ref, wd_ref, o_ref, wg_b, wu_b, wd_b):
    b = pl.program_id(0)
    rows = br_ref[b]
    half = GMM_ROWS // 2

    @pl.when(jnp.logical_or(b == 0, be_ref[b] != be_ref[jnp.maximum(b - 1, 0)]))
    def _():
        _cast_bf16(wg_b, wg_ref.at[0])
        _cast_bf16(wu_b, wu_ref.at[0])
        _cast_bf16(wd_b, wd_ref.at[0])

    @pl.when(rows > half)
    def _():
        o_ref[...] = _expert_ffn(x_ref[...], wg_b, wu_b, wd_b)

    @pl.when(jnp.logical_and(rows > 0, rows <= half))
    def _():
        o_ref[0:half, :] = _expert_ffn(x_ref[0:half, :], wg_b, wu_b, wd_b)
        o_ref[half:GMM_ROWS, :] = jnp.zeros((GMM_ROWS - half, o_ref.shape[1]), o_ref.dtype)

    @pl.when(rows == 0)
    def _():
        o_ref[...] = jnp.zeros_like(o_ref)


def _gmm(plan, xs, wg, wu, wd):
    R, D = xs.shape
    rows = pl.BlockSpec((GMM_ROWS, D), lambda b, be, nu, br: (jnp.minimum(b, nu[0] - 1), 0))
    grid_spec = pltpu.PrefetchScalarGridSpec(
        num_scalar_prefetch=3,
        grid=(R // GMM_ROWS,),
        in_specs=[rows,
                  pl.BlockSpec((1, D, D_EXPERT), lambda b, be, nu, br: (be[b], 0, 0)),
                  pl.BlockSpec((1, D, D_EXPERT), lambda b, be, nu, br: (be[b], 0, 0)),
                  pl.BlockSpec((1, D_EXPERT, D), lambda b, be, nu, br: (be[b], 0, 0))],
        out_specs=pl.BlockSpec((GMM_ROWS, D), lambda b, be, nu, br: (b, 0)),
        scratch_shapes=[pltpu.VMEM((D, D_EXPERT), BF16), pltpu.VMEM((D, D_EXPERT), BF16),
                        pltpu.VMEM((D_EXPERT, D), BF16)],
    )
    return pl.pallas_call(
        _gmm_kernel,
        grid_spec=grid_spec,
        out_shape=jax.ShapeDtypeStruct((R, D), BF16),
        compiler_params=pltpu.CompilerParams(
            dimension_semantics=("arbitrary",), vmem_limit_bytes=VMEM_LIMIT),
        name="moe_gmm",
    )(plan["blk_expert"], plan["n_used"], plan["blk_rows"], xs, wg, wu, wd)


def _combine_kernel(bs_ref, bd_ref, bn_ref, ss_ref, sd_ref, sn_ref, cm_ref, x_ref, ys_hbm, g3_ref, b3_ref,
                    o_ref, yt, sems):
    lists = (bs_ref, bd_ref, bn_ref, ss_ref, sd_ref, sn_ref)
    s = pl.program_id(0)
    ns = pl.num_programs(0)
    cur = (s % 2) * PER_STEP
    nxt = PER_STEP - cur

    def fetch(step, first_buf):
        for k in range(PER_STEP):
            _issue_copies(lists, step * PER_STEP + k, yt.at[first_buf + k], False, ys_hbm,
                          sems.at[first_buf + k])

    @pl.when(s == 0)
    def _():
        yt[...] = jnp.zeros_like(yt)
        fetch(0, 0)

    @pl.when(s + 1 < ns)
    def _():
        fetch(s + 1, nxt)

    for k in range(PER_STEP):
        _wait_copies(lists, s * PER_STEP + k, ys_hbm, yt.at[cur + k], sems.at[cur + k])

    col = lax.broadcasted_iota(jnp.int32, (MOE_TILE, SORT_ROWS), 1)
    for k in range(PER_STEP):
        rs = slice(k * MOE_TILE, (k + 1) * MOE_TILE)
        cm = cm_ref[rs, :]
        lp0 = cm[:, 0:1].astype(jnp.int32)
        lp1 = cm[:, 1:2].astype(jnp.int32)
        w = jnp.where(col == lp0, cm[:, 2:3], jnp.where(col == lp1, cm[:, 3:4], 0.0)).astype(BF16)
        y = jnp.dot(w, yt[cur + k], preferred_element_type=F32)
        o_ref[rs, :] = _layer_norm(ALPHA * x_ref[rs, :] + y, g3_ref[...], b3_ref[...])


def _combine(plan, route, x2, ys, g3, b3):
    T, D = x2.shape
    rows = PER_STEP * MOE_TILE
    grid_spec = pltpu.PrefetchScalarGridSpec(
        num_scalar_prefetch=6,
        grid=(T // rows,),
        in_specs=[pl.BlockSpec((rows, LANES), lambda t, *_: (t, 0)),
                  pl.BlockSpec((rows, D), lambda t, *_: (t, 0)),
                  pl.BlockSpec(memory_space=pl.ANY),
                  pl.BlockSpec((1, D), lambda t, *_: (0, 0)),
                  pl.BlockSpec((1, D), lambda t, *_: (0, 0))],
        out_specs=pl.BlockSpec((rows, D), lambda t, *_: (t, 0)),
        scratch_shapes=[pltpu.VMEM((2 * PER_STEP, SORT_ROWS, D), BF16),
                        pltpu.SemaphoreType.DMA((2 * PER_STEP,))],
    )
    return pl.pallas_call(
        _combine_kernel,
        grid_spec=grid_spec,
        out_shape=jax.ShapeDtypeStruct((T, D), F32),
        compiler_params=pltpu.CompilerParams(
            dimension_semantics=("arbitrary",), vmem_limit_bytes=VMEM_LIMIT),
        name="moe_combine",
    )(*_copy_lists(plan), route, x2, ys, g3, b3)


def _moe(x2, x2b, route, lpt, meta, wg, wu, wd, g3, b3):
    plan = _plan(meta, x2.shape[0])
    xs = _dispatch(plan, lpt, x2b)
    ys = _gmm(plan, xs, wg, wu, wd)
    return _combine(plan, route, x2, ys, g3, b3)


def _row(v):
    return v.reshape(1, -1).astype(F32)


def kernel(x, mem, w_in, b_in, w_dw, b_dw, g_conv_norm, b_conv_norm, attn_sinks, w_out, g_ln1, b_ln1,
           w_mq, w_mkv, w_mo, g_ln2, b_ln2, w_group, b_group, w_router, b_router, w_gate, w_up, w_down,
           g_ln3, b_ln3):
    B, S, D = x.shape
    for l in range(DEPTH):
        w_dw_p = jnp.zeros((CONV_HALO, CONV_CH), F32).at[:CONV_WIDTH].set(w_dw[l])
        x1 = _mixer(x, attn_sinks[l].astype(F32), w_in[l], _row(b_in[l]), w_dw_p,
                    _row(b_dw[l]), _row(g_conv_norm[l]), _row(b_conv_norm[l]),
                    w_out[l], _row(g_ln1[l]), _row(b_ln1[l]))

        kvm = _memkv(mem.reshape(B * MEM_LEN, D), w_mkv[l]).reshape(B, MEM_LEN, 2 * D)

        wr = jnp.concatenate(
            [w_group[l], jnp.transpose(w_router[l], (1, 0, 2)).reshape(D, N_EXPERTS)], axis=1)
        wr = jnp.pad(wr, ((0, 0), (0, LANES - wr.shape[1])))
        br = jnp.pad(jnp.concatenate([b_group[l], b_router[l].reshape(-1)]), (0, LANES - N_GROUPS - N_EXPERTS))
        x2, x2b, route, lpt, meta = _memattn(x1, w_mq[l], kvm, w_mo[l],
                                 _row(g_ln2[l]), _row(b_ln2[l]), wr.astype(F32), _row(br))

        T = B * S
        y = _moe(x2.reshape(T, D), x2b.reshape(T, D), route.reshape(T, LANES), lpt, meta,
                 w_gate[l], w_up[l], w_down[l],
                 _row(g_ln3[l]), _row(b_ln3[l]))
        x = y.reshape(B, S, D)
    return x
```

```python
import functools

import jax
import jax.numpy as jnp
from jax import lax
from jax.experimental import pallas as pl
from jax.experimental.pallas import tpu as pltpu

D_MODEL = 1024
MEM_LEN = 256
CONV_CH = 512
CONV_WIDTH = 31
N_HEADS = 8
N_KV_HEADS = 2
HEAD_DIM = 64
GQ = N_HEADS // N_KV_HEADS
ATTN_W = N_HEADS * HEAD_DIM
KV_W = N_KV_HEADS * HEAD_DIM
WINDOW = 128
D_MIX = CONV_CH + ATTN_W
D_IN = 2 * CONV_CH + ATTN_W + 2 * KV_W
MEM_HEADS = 4
MEM_HEAD_DIM = D_MODEL // MEM_HEADS
N_GROUPS = 4
EXPERTS_PER_GROUP = 4
N_EXPERTS = N_GROUPS * EXPERTS_PER_GROUP
D_EXPERT = D_MODEL // 2
DEPTH = 1
ALPHA = (2.0 * DEPTH) ** 0.25
LN_EPS = 1e-5

LANES = 128
SUBLANES = 8
CONV_ROWS = 128
LN_ROWS = 64
MEM_TILE = 1024
MASK_VALUE = -1e30
CONV_HALO = 32
SEQ_TILE = 512
MOE_TILE = 512
CHUNK = 16
GMM_ROWS = 512
TOP_K = 2
SORT_ROWS = -(-(MOE_TILE * TOP_K + N_EXPERTS * (CHUNK - 1)) // 256) * 256
BIG = 4
MAX_BIG = SORT_ROWS // (BIG * CHUNK)
MAX_SMALL = N_EXPERTS * (BIG - 1)
PER_STEP = 2
WAIT_GROUP = 8
ROUTE_OFF = N_GROUPS
VMEM_LIMIT = 56 * 1024 * 1024

BF16 = jnp.bfloat16
F32 = jnp.float32


def _layer_norm(x, g, b):
    mu = jnp.mean(x, axis=-1, keepdims=True)
    xc = x - mu
    var = jnp.mean(xc * xc, axis=-1, keepdims=True)
    return xc * lax.rsqrt(var + LN_EPS) * g + b


def _cast_bf16(dst_ref, src_ref):
    rows = 256
    for r0 in range(0, src_ref.shape[0], rows):
        dst_ref[r0:r0 + rows, :] = src_ref[r0:r0 + rows, :].astype(BF16)


def _dot_nt(a, b):
    return lax.dot_general(a, b, (((1,), (1,)), ((), ())), preferred_element_type=F32)


def _mixer_kernel(sinks_ref, x_ref, w_in_ref, b_in_ref, w_dw_ref, b_dw_ref, g_cn_ref, b_cn_ref,
                  w_out_ref, g1_ref, b1_ref, o_ref, w_in_b, w_out_b, hbuf, hshift, cbuf, qbuf, kbuf, vbuf, ymix):
    i = pl.program_id(1)
    ts = SEQ_TILE

    @pl.when(jnp.logical_and(pl.program_id(0) == 0, i == 0))
    def _():
        _cast_bf16(w_in_b, w_in_ref)
        _cast_bf16(w_out_b, w_out_ref)

    @pl.when(i == 0)
    def _():
        hbuf[0:CONV_HALO, :] = jnp.zeros((CONV_HALO, CONV_CH), F32)
        kbuf[:, 0:WINDOW, :] = jnp.zeros((2 * N_KV_HEADS, WINDOW, KV_W), BF16)
        vbuf[:, 0:WINDOW, :] = jnp.zeros((2, WINDOW, KV_W), BF16)

    x = x_ref[0]
    u = jnp.dot(x.astype(BF16), w_in_b[...], preferred_element_type=F32) + b_in_ref[...]
    a = u[:, 0:CONV_CH]
    gate = u[:, CONV_CH:2 * CONV_CH]
    hbuf[CONV_HALO:CONV_HALO + ts, :] = a * jax.nn.sigmoid(gate)
    qbuf[...] = (u[:, 2 * CONV_CH:2 * CONV_CH + ATTN_W] * (HEAD_DIM ** -0.5)).astype(BF16)
    kf = u[:, 2 * CONV_CH + ATTN_W:2 * CONV_CH + ATTN_W + KV_W]
    vf = u[:, 2 * CONV_CH + ATTN_W + KV_W:D_IN]
    kr = pltpu.roll(kf, HEAD_DIM, axis=1)
    vr = pltpu.roll(vf, HEAD_DIM, axis=1)
    lo = lax.broadcasted_iota(jnp.int32, (ts, KV_W), 1) < HEAD_DIM
    rows = slice(WINDOW, WINDOW + ts)
    kbuf[0, rows, :] = jnp.where(lo, kf, 0.0).astype(BF16)
    kbuf[1, rows, :] = jnp.where(lo, 0.0, kr).astype(BF16)
    kbuf[2, rows, :] = jnp.where(lo, kr, 0.0).astype(BF16)
    kbuf[3, rows, :] = jnp.where(lo, 0.0, kf).astype(BF16)
    vbuf[0, rows, :] = vf.astype(BF16)
    vbuf[1, rows, :] = vr.astype(BF16)

    base = CONV_HALO - (CONV_WIDTH - 1)
    n_shift = ts + CONV_HALO - SUBLANES
    for b in range(1, SUBLANES):
        hshift[b - 1, 0:n_shift, :] = hbuf[b:b + n_shift, :]
    rc = CONV_ROWS

    def conv_chunk(c):
        r0 = c * rc
        for l in range(CONV_CH // LANES):
            ls = slice(l * LANES, (l + 1) * LANES)
            acc = jnp.zeros((rc, LANES), F32)
            for j in range(CONV_WIDTH):
                a8, b = divmod(j + base, SUBLANES)
                rs = slice(r0 + SUBLANES * a8, r0 + SUBLANES * a8 + rc)
                tap = hbuf[rs, ls] if b == 0 else hshift[b - 1, rs, ls]
                acc = acc + tap * w_dw_ref[j:j + 1, ls]
            cbuf[r0:r0 + rc, ls] = acc
        for r1 in range(r0, r0 + rc, LN_ROWS):
            rs = slice(r1, r1 + LN_ROWS)
            y = _layer_norm(cbuf[rs, :] + b_dw_ref[...], g_cn_ref[...], b_cn_ref[...])
            y = y * jax.nn.sigmoid(y)
            ymix[rs, 0:CONV_CH] = y.astype(BF16)

    qi = lax.broadcasted_iota(jnp.int32, (2 * WINDOW, 2 * WINDOW), 0) % WINDOW
    kj = lax.broadcasted_iota(jnp.int32, (2 * WINDOW, 2 * WINDOW), 1)
    dist = qi + WINDOW - kj
    band = (dist >= 0) & (dist < WINDOW)
    top = lax.broadcasted_iota(jnp.int32, (2 * WINDOW, 1), 0) < WINDOW
    lo_out = lax.broadcasted_iota(jnp.int32, (WINDOW, 2 * HEAD_DIM), 1) < HEAD_DIM

    def attn_block(jb):
        r0 = jb * WINDOW
        valid = band & jnp.logical_or(i != 0, kj >= WINDOW) if jb == 0 else band
        for kvh in range(N_KV_HEADS):
            h0 = kvh * GQ
            c0 = h0 * HEAD_DIM
            qs = jnp.concatenate([qbuf[r0:r0 + WINDOW, c0:c0 + 2 * HEAD_DIM],
                                  qbuf[r0:r0 + WINDOW, c0 + 2 * HEAD_DIM:c0 + 4 * HEAD_DIM]], axis=0)
            pv = []
            for par in range(2):
                kk = kbuf[2 * kvh + par, r0:r0 + 2 * WINDOW, :]
                vv = vbuf[(kvh + par) % 2, r0:r0 + 2 * WINDOW, :]
                s = jnp.where(valid, _dot_nt(qs, kk), MASK_VALUE)
                sink = jnp.where(top, sinks_ref[h0 + par], sinks_ref[h0 + 2 + par])
                m = jnp.maximum(jnp.max(s, axis=-1, keepdims=True), sink)
                p = jnp.exp(s - m)
                denom = jnp.sum(p, axis=-1, keepdims=True) + jnp.exp(sink - m)
                pv.append(jnp.dot(p.astype(BF16), vv, preferred_element_type=F32) / denom)
            for pair in range(2):
                rs = slice(pair * WINDOW, (pair + 1) * WINDOW)
                o = jnp.where(lo_out, pv[0][rs], pv[1][rs])
                cs = CONV_CH + c0 + pair * 2 * HEAD_DIM
                ymix[r0:r0 + WINDOW, cs:cs + 2 * HEAD_DIM] = o.astype(BF16)

    assert ts // rc == ts // WINDOW
    for c in range(ts // rc):
        attn_block(c)
        conv_chunk(c)

    mix = jnp.dot(ymix[...], w_out_b[...], preferred_element_type=F32)
    o_ref[0] = _layer_norm(ALPHA * x + mix, g1_ref[...], b1_ref[...])

    hbuf[0:CONV_HALO, :] = hbuf[ts:ts + CONV_HALO, :]
    kbuf[:, 0:WINDOW, :] = kbuf[:, ts:ts + WINDOW, :]
    vbuf[:, 0:WINDOW, :] = vbuf[:, ts:ts + WINDOW, :]


def _const_spec(shape):
    nd = len(shape)
    return pl.BlockSpec(shape, lambda *_: (0,) * nd)


def _resident_spec(shape):
    nd = len(shape)
    return pl.BlockSpec(shape, lambda *_: (0,) * nd, pipeline_mode=pl.Buffered(1))


def _mixer(x, sinks, w_in, b_in, w_dw, b_dw, g_cn, b_cn, w_out, g1, b1):
    B, S, D = x.shape
    ts = SEQ_TILE
    tile = pl.BlockSpec((1, ts, D), lambda b, i: (b, i, 0))
    return pl.pallas_call(
        _mixer_kernel,
        grid=(B, S // ts),
        in_specs=[
            pl.BlockSpec(memory_space=pltpu.SMEM),
            tile,
            _resident_spec((D, D_IN)), _const_spec((1, D_IN)),
            _const_spec((CONV_HALO, CONV_CH)), _const_spec((1, CONV_CH)),
            _const_spec((1, CONV_CH)), _const_spec((1, CONV_CH)),
            _resident_spec((D_MIX, D)), _const_spec((1, D)), _const_spec((1, D)),
        ],
        out_specs=tile,
        out_shape=jax.ShapeDtypeStruct((B, S, D), F32),
        scratch_shapes=[
            pltpu.VMEM((D, D_IN), BF16),
            pltpu.VMEM((D_MIX, D), BF16),
            pltpu.VMEM((CONV_HALO + ts, CONV_CH), F32),
            pltpu.VMEM((SUBLANES - 1, CONV_HALO + ts, CONV_CH), F32),
            pltpu.VMEM((ts, CONV_CH), F32),
            pltpu.VMEM((ts, ATTN_W), BF16),
            pltpu.VMEM((2 * N_KV_HEADS, WINDOW + ts, KV_W), BF16),
            pltpu.VMEM((2, WINDOW + ts, KV_W), BF16),
            pltpu.VMEM((ts, D_MIX), BF16),
        ],
        compiler_params=pltpu.CompilerParams(
            dimension_semantics=("arbitrary", "arbitrary"), vmem_limit_bytes=VMEM_LIMIT),
        name="mixer",
    )(sinks, x, w_in, b_in, w_dw, b_dw, g_cn, b_cn, w_out, g1, b1)


def _memkv_kernel(mem_ref, w_ref, o_ref):
    o_ref[...] = jnp.dot(mem_ref[...].astype(BF16), w_ref[...].astype(BF16),
                         preferred_element_type=F32).astype(BF16)


def _memkv(mem2d, w_mkv):
    M, D = mem2d.shape
    N = w_mkv.shape[1]
    tn = 512
    return pl.pallas_call(
        _memkv_kernel,
        grid=(N // tn,),
        in_specs=[pl.BlockSpec((M, D), lambda j: (0, 0)), pl.BlockSpec((D, tn), lambda j: (0, j))],
        out_specs=pl.BlockSpec((M, tn), lambda j: (0, j)),
        out_shape=jax.ShapeDtypeStruct((M, N), BF16),
        compiler_params=pltpu.CompilerParams(dimension_semantics=("arbitrary",)),
        name="memkv",
    )(mem2d, w_mkv)


def _first_max(rows):
    best = rows[0]
    for r in rows[1:]:
        best = jnp.maximum(best, r)
    idx = jnp.full(best.shape, len(rows) - 1, jnp.int32)
    for k in range(len(rows) - 2, -1, -1):
        idx = jnp.where(rows[k] == best, k, idx)
    return best, idx


def _route_plan(logits_t):
    tile = logits_t.shape[1]
    row = lambda k: logits_t[k:k + 1, :]
    gmax, g_idx = _first_max([row(g) for g in range(N_GROUPS)])
    gsum = jnp.exp(row(0) - gmax)
    for g in range(1, N_GROUPS):
        gsum = gsum + jnp.exp(row(g) - gmax)
    g_p = 1.0 / gsum
    rl = []
    for e in range(EXPERTS_PER_GROUP):
        v = row(ROUTE_OFF + (N_GROUPS - 1) * EXPERTS_PER_GROUP + e)
        for g in range(N_GROUPS - 2, -1, -1):
            v = jnp.where(g_idx == g, row(ROUTE_OFF + g * EXPERTS_PER_GROUP + e), v)
        rl.append(v)
    m1, i1 = _first_max(rl)
    m2, i2 = _first_max([jnp.where(i1 == e, MASK_VALUE, rl[e]) for e in range(EXPERTS_PER_GROUP)])
    ex = jnp.exp(m2 - m1)
    w1 = 1.0 / (1.0 + ex)
    w2 = ex * w1
    e1 = g_idx * EXPERTS_PER_GROUP + i1
    e2 = g_idx * EXPERTS_PER_GROUP + i2

    eid = lax.broadcasted_iota(jnp.int32, (N_EXPERTS, tile), 0)
    hit1 = eid == e1
    hit2 = eid == e2
    oh = jnp.where(jnp.logical_or(hit1, hit2), 1.0, 0.0)
    r = lax.broadcasted_iota(jnp.int32, (tile, tile), 0)
    c = lax.broadcasted_iota(jnp.int32, (tile, tile), 1)
    tri = jnp.where(r <= c, 1.0, 0.0).astype(BF16)
    csum = jnp.dot(oh.astype(BF16), tri, preferred_element_type=F32)
    counts = jnp.broadcast_to(csum[:, tile - 1:tile], (N_EXPERTS, tile)).astype(jnp.int32)
    nch = jnp.right_shift(counts + (CHUNK - 1), CHUNK.bit_length() - 1)
    er = lax.broadcasted_iota(jnp.int32, (N_EXPERTS, N_EXPERTS), 0)
    ec = lax.broadcasted_iota(jnp.int32, (N_EXPERTS, N_EXPERTS), 1)
    lower = jnp.where(ec < er, 1.0, 0.0).astype(BF16)
    off = jnp.dot(lower, nch.astype(F32).astype(BF16), preferred_element_type=F32) * CHUNK
    pos = off + csum - oh
    lp1 = jnp.sum(jnp.where(hit1, pos, 0.0), axis=0, keepdims=True)
    lp2 = jnp.sum(jnp.where(hit2, pos, 0.0), axis=0, keepdims=True)
    zero = jnp.zeros_like(lp1)
    route_t = jnp.concatenate([lp1, lp2, g_p * w1, g_p * w2, zero, zero, zero, zero], axis=0)
    meta = jnp.concatenate([nch[:, 0:LANES], off[:, 0:LANES].astype(jnp.int32)], axis=0)
    return route_t, meta


def _memattn_kernel(x_ref, wq_ref, k_ref, v_ref, wo_ref, g2_ref, b2_ref, wr2_ref, wrh_ref, br_ref,
                    o_ref, ob_ref, route_ref, routet_ref, meta_ref, wq_b, wo_b):
    @pl.when(jnp.logical_and(pl.program_id(0) == 0, pl.program_id(1) == 0))
    def _():
        _cast_bf16(wq_b, wq_ref)
        _cast_bf16(wo_b, wo_ref)

    def rows_logits(rs):
        x = x_ref[0, rs, :]
        q = jnp.dot(x.astype(BF16), wq_b[...], preferred_element_type=F32)
        q = (q * (MEM_HEAD_DIM ** -0.5)).astype(BF16)
        outs = []
        for h in range(MEM_HEADS):
            sl = slice(h * MEM_HEAD_DIM, (h + 1) * MEM_HEAD_DIM)
            s = _dot_nt(q[:, sl], k_ref[0, :, sl])
            m = jnp.max(s, axis=-1, keepdims=True)
            p = jnp.exp(s - m)
            denom = jnp.sum(p, axis=-1, keepdims=True)
            o = jnp.dot(p.astype(BF16), v_ref[0, :, sl], preferred_element_type=F32)
            outs.append((o / denom).astype(BF16))
        o = jnp.dot(jnp.concatenate(outs, axis=-1), wo_b[...], preferred_element_type=F32)
        x2 = _layer_norm(ALPHA * x + o, g2_ref[...], b2_ref[...])
        o_ref[0, rs, :] = x2
        x2h = x2.astype(BF16)
        ob_ref[0, rs, :] = x2h
        x2l = (x2 - x2h.astype(F32)).astype(BF16)
        hh = jnp.dot(x2h, wr2_ref[...], preferred_element_type=F32)
        return (hh[:, 0:LANES] + hh[:, LANES:2 * LANES]
                + jnp.dot(x2l, wrh_ref[...], preferred_element_type=F32) + br_ref[...])

    ts = x_ref.shape[1]
    groups = [slice(r0, r0 + MOE_TILE) for r0 in range(0, ts, MOE_TILE)]
    logits = [rows_logits(rs) for rs in groups]
    pad = jnp.zeros((LANES - SUBLANES, MOE_TILE), F32)
    for k, rs in enumerate(groups):
        route_t, meta = _route_plan(jnp.transpose(logits[k]))
        routet_ref[k] = route_t
        route_ref[0, rs, :] = jnp.transpose(jnp.concatenate([route_t, pad], axis=0))
        meta_ref[k] = meta


def _memattn(x1, wq, kvm, wo, g2, b2, wr, br):
    B, S, D = x1.shape
    ts = MEM_TILE
    per = ts // MOE_TILE
    nt = S // ts
    c = wr * (2.0 ** 16 + 1.0)
    w_high = c - (c - wr)
    wrh = w_high.astype(BF16)
    wr2 = jnp.concatenate([wrh, (wr - w_high).astype(BF16)], axis=1)
    tile = pl.BlockSpec((1, ts, D), lambda b, i: (b, i, 0))
    kspec = pl.BlockSpec((1, MEM_LEN, D), lambda b, i: (b, 0, 0))
    vspec = pl.BlockSpec((1, MEM_LEN, D), lambda b, i: (b, 0, 1))
    return pl.pallas_call(
        _memattn_kernel,
        grid=(B, nt),
        in_specs=[tile, _resident_spec((D, D)), kspec, vspec, _resident_spec((D, D)),
                  _const_spec((1, D)), _const_spec((1, D)),
                  _const_spec((D, 2 * LANES)), _const_spec((D, LANES)), _const_spec((1, LANES))],
        out_specs=[tile, tile, pl.BlockSpec((1, ts, LANES), lambda b, i: (b, i, 0)),
                   pl.BlockSpec((per, SUBLANES, MOE_TILE), lambda b, i: (b * nt + i, 0, 0)),
                   pl.BlockSpec((per, 2 * N_EXPERTS, LANES), lambda b, i: (b * nt + i, 0, 0))],
        out_shape=[jax.ShapeDtypeStruct((B, S, D), F32),
                   jax.ShapeDtypeStruct((B, S, D), BF16),
                   jax.ShapeDtypeStruct((B, S, LANES), F32),
                   jax.ShapeDtypeStruct((B * nt * per, SUBLANES, MOE_TILE), F32),
                   jax.ShapeDtypeStruct((B * nt * per, 2 * N_EXPERTS, LANES), jnp.int32)],
        scratch_shapes=[pltpu.VMEM((D, D), BF16), pltpu.VMEM((D, D), BF16)],
        compiler_params=pltpu.CompilerParams(
            dimension_semantics=("arbitrary", "arbitrary"), vmem_limit_bytes=VMEM_LIMIT),
        name="memattn",
    )(x1, wq, kvm, kvm, wo, g2, b2, wr2, wrh, br)


def _gmm_blocks(n_tokens):
    rows = (n_tokens * TOP_K + (n_tokens // MOE_TILE) * N_EXPERTS * (CHUNK - 1)
            + N_EXPERTS * (GMM_ROWS - CHUNK))
    return -(-rows // GMM_ROWS)


def _plan(meta, T):
    nch = meta[:, :N_EXPERTS, 0]
    n16 = nch * CHUNK
    n_e = jnp.sum(n16, axis=0)
    reg = (n_e + GMM_ROWS - 1) // GMM_ROWS * GMM_ROWS
    gend = jnp.cumsum(reg)
    gbase = gend - reg
    dst = gbase[None, :] + jnp.cumsum(n16, axis=0) - n16
    tail_blocks = (_gmm_blocks(T) * GMM_ROWS - gend[-1:]) // GMM_ROWS
    src = (jnp.cumsum(nch, axis=1) - nch) * CHUNK
    n_big = nch // BIG

    def copy_list(count, src0, dst0, rows, length):
        cum = jnp.cumsum(count, axis=1)
        first = (cum - count)[:, None, :]
        k = jnp.arange(length, dtype=jnp.int32)[None, :, None]
        mine = (k >= first) & (k < cum[:, None, :])
        step = (k - first) * rows
        pick = lambda base: jnp.sum(jnp.where(mine, base[:, None, :] + step, 0), axis=2)
        return pick(src0), pick(dst0), cum[:, -1]

    big_src, big_dst, big_n = copy_list(n_big, src, dst, BIG * CHUNK, MAX_BIG)
    rest = n_big * (BIG * CHUNK)
    small_src, small_dst, small_n = copy_list(nch - n_big * BIG, src + rest, dst + rest, CHUNK, MAX_SMALL)
    i32 = lambda a: a.astype(jnp.int32)
    fill_start = jnp.concatenate([gbase + n_e, gend[-1:]])
    fill_n = jnp.concatenate([reg - n_e, _gmm_blocks(T) * GMM_ROWS - gend[-1:]]) // CHUNK
    return dict(big_src=i32(big_src).reshape(-1), big_dst=i32(big_dst).reshape(-1), big_n=i32(big_n),
                small_src=i32(small_src).reshape(-1), small_dst=i32(small_dst).reshape(-1), small_n=i32(small_n),
                fill_start=i32(fill_start), fill_n=i32(fill_n),
                fill_tot=i32(jnp.sum(fill_n)).reshape(1), exp_blocks=i32(reg // GMM_ROWS),
                exp_base=i32(gbase), tail_blocks=i32(tail_blocks))


def _rows_copy(src_ref, src_row, dst_ref, dst_row, rows, sem):
    return pltpu.make_async_copy(
        src_ref.at[pl.ds(pl.multiple_of(src_row, CHUNK), rows), :],
        dst_ref.at[pl.ds(pl.multiple_of(dst_row, CHUNK), rows), :], sem)


def _chunk_copy(src_ref, src_row, dst_ref, dst_row, sem):
    return _rows_copy(src_ref, src_row, dst_ref, dst_row, CHUNK, sem)


def _issue_copies(lists, tile, tile_ref, tile_is_src, hbm_ref, sem):
    big_src, big_dst, big_n, small_src, small_dst, small_n = lists
    for src_l, dst_l, n_l, length, rows in ((big_src, big_dst, big_n, MAX_BIG, BIG * CHUNK),
                                            (small_src, small_dst, small_n, MAX_SMALL, CHUNK)):
        def issue(k, carry, src_l=src_l, dst_l=dst_l, length=length, rows=rows):
            local, remote = src_l[tile * length + k], dst_l[tile * length + k]
            if tile_is_src:
                _rows_copy(tile_ref, local, hbm_ref, remote, rows, sem).start()
            else:
                _rows_copy(hbm_ref, remote, tile_ref, local, rows, sem).start()
            return carry

        lax.fori_loop(0, n_l[tile], issue, 0)


def _wait_rows(n, rows, src_ref, dst_ref, sem):
    def body(c, carry):
        pltpu.make_async_copy(src_ref.at[pl.ds(0, rows), :], dst_ref.at[pl.ds(0, rows), :], sem).wait()
        return carry

    lax.fori_loop(0, n, body, 0)


def _wait_copies(lists, tile, src_ref, dst_ref, sem, extra_chunks=0):
    _wait_rows(lists[2][tile], BIG * CHUNK, src_ref, dst_ref, sem)
    n = lists[5][tile] + extra_chunks
    _wait_rows(n // WAIT_GROUP, WAIT_GROUP * CHUNK, src_ref, dst_ref, sem)
    _wait_rows(n % WAIT_GROUP, CHUNK, src_ref, dst_ref, sem)


def _dispatch_kernel(bs_ref, bd_ref, bn_ref, ss_ref, sd_ref, sn_ref, fstart_ref, fn_ref, ftot_ref,
                     lp_ref, x_ref, xs_hbm, xt, zbuf, sems):
    lists = (bs_ref, bd_ref, bn_ref, ss_ref, sd_ref, sn_ref)
    s = pl.program_id(0)
    ns = pl.num_programs(0)
    cur = (s % 2) * PER_STEP
    prv = PER_STEP - cur

    @pl.when(s >= 2)
    def _():
        for k in range(PER_STEP):
            _wait_copies(lists, (s - 2) * PER_STEP + k, xt.at[cur + k], xs_hbm, sems.at[cur + k])

    r = lax.broadcasted_iota(jnp.int32, (SORT_ROWS, MOE_TILE), 0)
    for k in range(PER_STEP):
        lp = lp_ref[k].astype(jnp.int32)
        hit = jnp.logical_or(lp[0:1, :] == r, lp[1:2, :] == r)
        p = jnp.where(hit, 1.0, 0.0).astype(BF16)
        x = x_ref[k * MOE_TILE:(k + 1) * MOE_TILE, :]
        xt[cur + k] = jnp.dot(p, x, preferred_element_type=F32).astype(BF16)

    for k in range(PER_STEP):
        _issue_copies(lists, s * PER_STEP + k, xt.at[cur + k], True, xs_hbm, sems.at[cur + k])

    @pl.when(s == ns - 1)
    def _():
        zbuf[...] = jnp.zeros_like(zbuf)
        sem = sems.at[cur]

        def per_range(e, carry):
            def issue(c, carry2):
                _chunk_copy(zbuf, 0, xs_hbm, fstart_ref[e] + c * CHUNK, sem).start()
                return carry2

            return lax.fori_loop(0, fn_ref[e], issue, carry)

        lax.fori_loop(0, N_EXPERTS + 1, per_range, 0)
        for k in range(PER_STEP):
            _wait_copies(lists, (s - 1) * PER_STEP + k, xt.at[prv + k], xs_hbm, sems.at[prv + k])
            _wait_copies(lists, s * PER_STEP + k, xt.at[cur + k], xs_hbm, sems.at[cur + k],
                         extra_chunks=ftot_ref[0] if k == 0 else 0)


def _copy_lists(plan):
    return tuple(plan[k] for k in ("big_src", "big_dst", "big_n", "small_src", "small_dst", "small_n"))


def _dispatch(plan, lpt, x2b):
    T, D = x2b.shape
    rows = PER_STEP * MOE_TILE
    assert T // rows >= 2
    grid_spec = pltpu.PrefetchScalarGridSpec(
        num_scalar_prefetch=9,
        grid=(T // rows,),
        in_specs=[pl.BlockSpec((PER_STEP, SUBLANES, MOE_TILE), lambda t, *_: (t, 0, 0)),
                  pl.BlockSpec((rows, D), lambda t, *_: (t, 0))],
        out_specs=pl.BlockSpec(memory_space=pl.ANY),
        scratch_shapes=[pltpu.VMEM((2 * PER_STEP, SORT_ROWS, D), BF16), pltpu.VMEM((CHUNK, D), BF16),
                        pltpu.SemaphoreType.DMA((2 * PER_STEP,))],
    )
    return pl.pallas_call(
        _dispatch_kernel,
        grid_spec=grid_spec,
        out_shape=jax.ShapeDtypeStruct((_gmm_blocks(T) * GMM_ROWS, D), BF16),
        compiler_params=pltpu.CompilerParams(
            dimension_semantics=("arbitrary",), vmem_limit_bytes=VMEM_LIMIT),
        name="moe_dispatch",
    )(*_copy_lists(plan), plan["fill_start"], plan["fill_n"], plan["fill_tot"], lpt, x2b)


def _expert_ffn(xb, wg_b, wu_b, wd_b):
    g = jnp.dot(xb, wg_b[...], preferred_element_type=F32)
    u = jnp.dot(xb, wu_b[...], preferred_element_type=F32)
    h = (g * jax.nn.sigmoid(g)) * u
    return jnp.dot(h.astype(BF16), wd_b[...], preferred_element_type=F32).astype(BF16)


def _gmm_kernel(nb_ref, base_ref, nfill_ref, x_hbm, wg_ref, wu_ref, wd_ref, y_hbm,
                wg_b, wu_b, wd_b, xbuf, ybuf, xsem, ysem):
    e = pl.program_id(0)
    nb = nb_ref[e]
    base = base_ref[e]

    def rows_at(i):
        return pl.ds(pl.multiple_of(base + i * GMM_ROWS, GMM_ROWS), GMM_ROWS)

    def x_copy(i, slot):
        return pltpu.make_async_copy(x_hbm.at[rows_at(i), :], xbuf.at[slot], xsem.at[slot])

    def y_copy(i, slot):
        return pltpu.make_async_copy(ybuf.at[slot], y_hbm.at[rows_at(i), :], ysem.at[slot])

    @pl.when(nb > 0)
    def _():
        x_copy(0, 0).start()
        _cast_bf16(wg_b, wg_ref.at[0])
        _cast_bf16(wu_b, wu_ref.at[0])
        _cast_bf16(wd_b, wd_ref.at[0])

    def block(i, carry):
        slot = i % 2

        @pl.when(i + 1 < nb)
        def _():
            x_copy(i + 1, 1 - slot).start()

        x_copy(i, slot).wait()

        @pl.when(i >= 2)
        def _():
            y_copy(i - 2, slot).wait()

        ybuf[slot] = _expert_ffn(xbuf[slot], wg_b, wu_b, wd_b)
        y_copy(i, slot).start()
        return carry

    lax.fori_loop(0, nb, block, 0)

    @pl.when(nb >= 2)
    def _():
        y_copy(nb - 2, nb % 2).wait()

    @pl.when(nb >= 1)
    def _():
        y_copy(nb - 1, (nb - 1) % 2).wait()

    @pl.when(e == pl.num_programs(0) - 1)
    def _():
        ybuf[0] = jnp.zeros(ybuf.shape[1:], ybuf.dtype)
        nfill = nfill_ref[0]

        def fill_copy(k):
            return y_copy(nb + k, 0)

        lax.fori_loop(0, nfill, lambda k, c: (fill_copy(k).start(), c)[1], 0)
        lax.fori_loop(0, nfill, lambda k, c: (fill_copy(k).wait(), c)[1], 0)


def _gmm(plan, xs, wg, wu, wd):
    R, D = xs.shape
    wspec = lambda shape: pl.BlockSpec((1,) + shape, lambda e, *_: (e, 0, 0))
    grid_spec = pltpu.PrefetchScalarGridSpec(
        num_scalar_prefetch=3,
        grid=(N_EXPERTS,),
        in_specs=[pl.BlockSpec(memory_space=pl.ANY),
                  wspec((D, D_EXPERT)), wspec((D, D_EXPERT)), wspec((D_EXPERT, D))],
        out_specs=pl.BlockSpec(memory_space=pl.ANY),
        scratch_shapes=[pltpu.VMEM((D, D_EXPERT), BF16), pltpu.VMEM((D, D_EXPERT), BF16),
                        pltpu.VMEM((D_EXPERT, D), BF16),
                        pltpu.VMEM((2, GMM_ROWS, D), BF16), pltpu.VMEM((2, GMM_ROWS, D), BF16),
                        pltpu.SemaphoreType.DMA((2,)), pltpu.SemaphoreType.DMA((2,))],
    )
    return pl.pallas_call(
        _gmm_kernel,
        grid_spec=grid_spec,
        out_shape=jax.ShapeDtypeStruct((R, D), BF16),
        compiler_params=pltpu.CompilerParams(
            dimension_semantics=("arbitrary",), vmem_limit_bytes=VMEM_LIMIT),
        name="moe_gmm",
    )(plan["exp_blocks"], plan["exp_base"], plan["tail_blocks"], xs, wg, wu, wd)


def _combine_kernel(bs_ref, bd_ref, bn_ref, ss_ref, sd_ref, sn_ref, cm_ref, x_ref, ys_hbm, g3_ref, b3_ref,
                    o_ref, yt, sems):
    lists = (bs_ref, bd_ref, bn_ref, ss_ref, sd_ref, sn_ref)
    s = pl.program_id(0)
    ns = pl.num_programs(0)
    cur = (s % 2) * PER_STEP
    nxt = PER_STEP - cur

    def fetch(step, first_buf):
        for k in range(PER_STEP):
            _issue_copies(lists, step * PER_STEP + k, yt.at[first_buf + k], False, ys_hbm,
                          sems.at[first_buf + k])

    @pl.when(s == 0)
    def _():
        yt[...] = jnp.zeros_like(yt)
        fetch(0, 0)

    @pl.when(s + 1 < ns)
    def _():
        fetch(s + 1, nxt)

    for k in range(PER_STEP):
        _wait_copies(lists, s * PER_STEP + k, ys_hbm, yt.at[cur + k], sems.at[cur + k])

    col = lax.broadcasted_iota(jnp.int32, (MOE_TILE, SORT_ROWS), 1)
    for k in range(PER_STEP):
        rs = slice(k * MOE_TILE, (k + 1) * MOE_TILE)
        cm = cm_ref[rs, :]
        lp0 = cm[:, 0:1].astype(jnp.int32)
        lp1 = cm[:, 1:2].astype(jnp.int32)
        w = jnp.where(col == lp0, cm[:, 2:3], jnp.where(col == lp1, cm[:, 3:4], 0.0)).astype(BF16)
        y = jnp.dot(w, yt[cur + k], preferred_element_type=F32)
        o_ref[rs, :] = _layer_norm(ALPHA * x_ref[rs, :] + y, g3_ref[...], b3_ref[...])


def _combine(plan, route, x2, ys, g3, b3):
    T, D = x2.shape
    rows = PER_STEP * MOE_TILE
    grid_spec = pltpu.PrefetchScalarGridSpec(
        num_scalar_prefetch=6,
        grid=(T // rows,),
        in_specs=[pl.BlockSpec((rows, LANES), lambda t, *_: (t, 0)),
                  pl.BlockSpec((rows, D), lambda t, *_: (t, 0)),
                  pl.BlockSpec(memory_space=pl.ANY),
                  pl.BlockSpec((1, D), lambda t, *_: (0, 0)),
                  pl.BlockSpec((1, D), lambda t, *_: (0, 0))],
        out_specs=pl.BlockSpec((rows, D), lambda t, *_: (t, 0)),
        scratch_shapes=[pltpu.VMEM((2 * PER_STEP, SORT_ROWS, D), BF16),
                        pltpu.SemaphoreType.DMA((2 * PER_STEP,))],
    )
    return pl.pallas_call(
        _combine_kernel,
        grid_spec=grid_spec,
        out_shape=jax.ShapeDtypeStruct((T, D), F32),
        compiler_params=pltpu.CompilerParams(
            dimension_semantics=("arbitrary",), vmem_limit_bytes=VMEM_LIMIT),
        name="moe_combine",
    )(*_copy_lists(plan), route, x2, ys, g3, b3)


def _moe(x2, x2b, route, lpt, meta, wg, wu, wd, g3, b3):
    plan = _plan(meta, x2.shape[0])
    xs = _dispatch(plan, lpt, x2b)
    ys = _gmm(plan, xs, wg, wu, wd)
    return _combine(plan, route, x2, ys, g3, b3)


def _row(v):
    return v.reshape(1, -1).astype(F32)


def kernel(x, mem, w_in, b_in, w_dw, b_dw, g_conv_norm, b_conv_norm, attn_sinks, w_out, g_ln1, b_ln1,
           w_mq, w_mkv, w_mo, g_ln2, b_ln2, w_group, b_group, w_router, b_router, w_gate, w_up, w_down,
           g_ln3, b_ln3):
    B, S, D = x.shape
    for l in range(DEPTH):
        w_dw_p = jnp.zeros((CONV_HALO, CONV_CH), F32).at[:CONV_WIDTH].set(w_dw[l])
        x1 = _mixer(x, attn_sinks[l].astype(F32), w_in[l], _row(b_in[l]), w_dw_p,
                    _row(b_dw[l]), _row(g_conv_norm[l]), _row(b_conv_norm[l]),
                    w_out[l], _row(g_ln1[l]), _row(b_ln1[l]))

        kvm = _memkv(mem.reshape(B * MEM_LEN, D), w_mkv[l]).reshape(B, MEM_LEN, 2 * D)

        wr = jnp.concatenate(
            [w_group[l], jnp.transpose(w_router[l], (1, 0, 2)).reshape(D, N_EXPERTS)], axis=1)
        wr = jnp.pad(wr, ((0, 0), (0, LANES - wr.shape[1])))
        br = jnp.pad(jnp.concatenate([b_group[l], b_router[l].reshape(-1)]), (0, LANES - N_GROUPS - N_EXPERTS))
        x2, x2b, route, lpt, meta = _memattn(x1, w_mq[l], kvm, w_mo[l],
                                 _row(g_ln2[l]), _row(b_ln2[l]), wr.astype(F32), _row(br))

        T = B * S
        y = _moe(x2.reshape(T, D), x2b.reshape(T, D), route.reshape(T, LANES), lpt, meta,
                 w_gate[l], w_up[l], w_down[l],
                 _row(g_ln3[l]), _row(b_ln3[l]))
        x = y.reshape(B, S, D)
    return x
```

```python
import functools

import jax
import jax.numpy as jnp
from jax import lax
from jax.experimental import pallas as pl
from jax.experimental.pallas import tpu as pltpu

D_MODEL = 1024
MEM_LEN = 256
CONV_CH = 512
CONV_WIDTH = 31
N_HEADS = 8
N_KV_HEADS = 2
HEAD_DIM = 64
GQ = N_HEADS // N_KV_HEADS
ATTN_W = N_HEADS * HEAD_DIM
KV_W = N_KV_HEADS * HEAD_DIM
WINDOW = 128
D_MIX = CONV_CH + ATTN_W
D_IN = 2 * CONV_CH + ATTN_W + 2 * KV_W
MEM_HEADS = 4
MEM_HEAD_DIM = D_MODEL // MEM_HEADS
N_GROUPS = 4
EXPERTS_PER_GROUP = 4
N_EXPERTS = N_GROUPS * EXPERTS_PER_GROUP
D_EXPERT = D_MODEL // 2
DEPTH = 1
ALPHA = (2.0 * DEPTH) ** 0.25
LN_EPS = 1e-5

LANES = 128
SUBLANES = 8
CONV_ROWS = 128
LN_ROWS = 64
MEM_TILE = 1024
MASK_VALUE = -1e30
CONV_HALO = 32
SEQ_TILE = 512
MOE_TILE = 512
CHUNK = 16
GMM_ROWS = 512
TOP_K = 2
SORT_ROWS = -(-(MOE_TILE * TOP_K + N_EXPERTS * (CHUNK - 1)) // 256) * 256
BIG = 4
MAX_BIG = SORT_ROWS // (BIG * CHUNK)
MAX_SMALL = N_EXPERTS * (BIG - 1)
ROW_DMA_PRIORITY = 1
PER_STEP = 2
WAIT_GROUP = 8
ROUTE_OFF = N_GROUPS
VMEM_LIMIT = 56 * 1024 * 1024

BF16 = jnp.bfloat16
F32 = jnp.float32


def _layer_norm(x, g, b):
    mu = jnp.mean(x, axis=-1, keepdims=True)
    xc = x - mu
    var = jnp.mean(xc * xc, axis=-1, keepdims=True)
    return xc * lax.rsqrt(var + LN_EPS) * g + b


def _cast_bf16(dst_ref, src_ref):
    rows = 256
    for r0 in range(0, src_ref.shape[0], rows):
        dst_ref[r0:r0 + rows, :] = src_ref[r0:r0 + rows, :].astype(BF16)


def _dot_nt(a, b):
    return lax.dot_general(a, b, (((1,), (1,)), ((), ())), preferred_element_type=F32)


def _mixer_kernel(sinks_ref, x_ref, w_in_ref, b_in_ref, w_dw_ref, b_dw_ref, g_cn_ref, b_cn_ref,
                  w_out_ref, g1_ref, b1_ref, o_ref, w_in_b, w_out_b, hbuf, hshift, cbuf, qbuf, kbuf, vbuf, ymix):
    i = pl.program_id(1)
    ts = SEQ_TILE

    @pl.when(jnp.logical_and(pl.program_id(0) == 0, i == 0))
    def _():
        _cast_bf16(w_in_b, w_in_ref)
        _cast_bf16(w_out_b, w_out_ref)

    @pl.when(i == 0)
    def _():
        hbuf[0:CONV_HALO, :] = jnp.zeros((CONV_HALO, CONV_CH), F32)
        kbuf[:, 0:WINDOW, :] = jnp.zeros((2 * N_KV_HEADS, WINDOW, KV_W), BF16)
        vbuf[:, 0:WINDOW, :] = jnp.zeros((2, WINDOW, KV_W), BF16)

    x = x_ref[0]
    u = jnp.dot(x.astype(BF16), w_in_b[...], preferred_element_type=F32) + b_in_ref[...]
    a = u[:, 0:CONV_CH]
    gate = u[:, CONV_CH:2 * CONV_CH]
    hbuf[CONV_HALO:CONV_HALO + ts, :] = a * jax.nn.sigmoid(gate)
    qbuf[...] = (u[:, 2 * CONV_CH:2 * CONV_CH + ATTN_W] * (HEAD_DIM ** -0.5)).astype(BF16)
    kf = u[:, 2 * CONV_CH + ATTN_W:2 * CONV_CH + ATTN_W + KV_W]
    vf = u[:, 2 * CONV_CH + ATTN_W + KV_W:D_IN]
    kr = pltpu.roll(kf, HEAD_DIM, axis=1)
    vr = pltpu.roll(vf, HEAD_DIM, axis=1)
    lo = lax.broadcasted_iota(jnp.int32, (ts, KV_W), 1) < HEAD_DIM
    rows = slice(WINDOW, WINDOW + ts)
    kbuf[0, rows, :] = jnp.where(lo, kf, 0.0).astype(BF16)
    kbuf[1, rows, :] = jnp.where(lo, 0.0, kr).astype(BF16)
    kbuf[2, rows, :] = jnp.where(lo, kr, 0.0).astype(BF16)
    kbuf[3, rows, :] = jnp.where(lo, 0.0, kf).astype(BF16)
    vbuf[0, rows, :] = vf.astype(BF16)
    vbuf[1, rows, :] = vr.astype(BF16)

    base = CONV_HALO - (CONV_WIDTH - 1)
    n_shift = ts + CONV_HALO - SUBLANES
    for b in range(1, SUBLANES):
        hshift[b - 1, 0:n_shift, :] = hbuf[b:b + n_shift, :]
    rc = CONV_ROWS

    def conv_chunk(c):
        r0 = c * rc
        for l in range(CONV_CH // LANES):
            ls = slice(l * LANES, (l + 1) * LANES)
            acc = jnp.zeros((rc, LANES), F32)
            for j in range(CONV_WIDTH):
                a8, b = divmod(j + base, SUBLANES)
                rs = slice(r0 + SUBLANES * a8, r0 + SUBLANES * a8 + rc)
                tap = hbuf[rs, ls] if b == 0 else hshift[b - 1, rs, ls]
                acc = acc + tap * w_dw_ref[j:j + 1, ls]
            cbuf[r0:r0 + rc, ls] = acc
        for r1 in range(r0, r0 + rc, LN_ROWS):
            rs = slice(r1, r1 + LN_ROWS)
            y = _layer_norm(cbuf[rs, :] + b_dw_ref[...], g_cn_ref[...], b_cn_ref[...])
            y = y * jax.nn.sigmoid(y)
            ymix[rs, 0:CONV_CH] = y.astype(BF16)

    qi = lax.broadcasted_iota(jnp.int32, (2 * WINDOW, 2 * WINDOW), 0) % WINDOW
    kj = lax.broadcasted_iota(jnp.int32, (2 * WINDOW, 2 * WINDOW), 1)
    dist = qi + WINDOW - kj
    band = (dist >= 0) & (dist < WINDOW)
    top = lax.broadcasted_iota(jnp.int32, (2 * WINDOW, 1), 0) < WINDOW
    lo_out = lax.broadcasted_iota(jnp.int32, (WINDOW, 2 * HEAD_DIM), 1) < HEAD_DIM

    def attn_block(jb):
        r0 = jb * WINDOW
        valid = band & jnp.logical_or(i != 0, kj >= WINDOW) if jb == 0 else band
        for kvh in range(N_KV_HEADS):
            h0 = kvh * GQ
            c0 = h0 * HEAD_DIM
            qs = jnp.concatenate([qbuf[r0:r0 + WINDOW, c0:c0 + 2 * HEAD_DIM],
                                  qbuf[r0:r0 + WINDOW, c0 + 2 * HEAD_DIM:c0 + 4 * HEAD_DIM]], axis=0)
            pv = []
            for par in range(2):
                kk = kbuf[2 * kvh + par, r0:r0 + 2 * WINDOW, :]
                vv = vbuf[(kvh + par) % 2, r0:r0 + 2 * WINDOW, :]
                s = jnp.where(valid, _dot_nt(qs, kk), MASK_VALUE)
                sink = jnp.where(top, sinks_ref[h0 + par], sinks_ref[h0 + 2 + par])
                m = jnp.maximum(jnp.max(s, axis=-1, keepdims=True), sink)
                p = jnp.exp(s - m)
                denom = jnp.sum(p, axis=-1, keepdims=True) + jnp.exp(sink - m)
                pv.append(jnp.dot(p.astype(BF16), vv, preferred_element_type=F32) / denom)
            for pair in range(2):
                rs = slice(pair * WINDOW, (pair + 1) * WINDOW)
                o = jnp.where(lo_out, pv[0][rs], pv[1][rs])
                cs = CONV_CH + c0 + pair * 2 * HEAD_DIM
                ymix[r0:r0 + WINDOW, cs:cs + 2 * HEAD_DIM] = o.astype(BF16)

    assert ts // rc == ts // WINDOW
    for c in range(ts // rc):
        attn_block(c)
        conv_chunk(c)

    mix = jnp.dot(ymix[...], w_out_b[...], preferred_element_type=F32)
    o_ref[0] = _layer_norm(ALPHA * x + mix, g1_ref[...], b1_ref[...])

    hbuf[0:CONV_HALO, :] = hbuf[ts:ts + CONV_HALO, :]
    kbuf[:, 0:WINDOW, :] = kbuf[:, ts:ts + WINDOW, :]
    vbuf[:, 0:WINDOW, :] = vbuf[:, ts:ts + WINDOW, :]


def _const_spec(shape):
    nd = len(shape)
    return pl.BlockSpec(shape, lambda *_: (0,) * nd)


def _resident_spec(shape):
    nd = len(shape)
    return pl.BlockSpec(shape, lambda *_: (0,) * nd, pipeline_mode=pl.Buffered(1))


def _mixer(x, sinks, w_in, b_in, w_dw, b_dw, g_cn, b_cn, w_out, g1, b1):
    B, S, D = x.shape
    ts = SEQ_TILE
    tile = pl.BlockSpec((1, ts, D), lambda b, i: (b, i, 0))
    return pl.pallas_call(
        _mixer_kernel,
        grid=(B, S // ts),
        in_specs=[
            pl.BlockSpec(memory_space=pltpu.SMEM),
            tile,
            _resident_spec((D, D_IN)), _const_spec((1, D_IN)),
            _const_spec((CONV_HALO, CONV_CH)), _const_spec((1, CONV_CH)),
            _const_spec((1, CONV_CH)), _const_spec((1, CONV_CH)),
            _resident_spec((D_MIX, D)), _const_spec((1, D)), _const_spec((1, D)),
        ],
        out_specs=tile,
        out_shape=jax.ShapeDtypeStruct((B, S, D), F32),
        scratch_shapes=[
            pltpu.VMEM((D, D_IN), BF16),
            pltpu.VMEM((D_MIX, D), BF16),
            pltpu.VMEM((CONV_HALO + ts, CONV_CH), F32),
            pltpu.VMEM((SUBLANES - 1, CONV_HALO + ts, CONV_CH), F32),
            pltpu.VMEM((ts, CONV_CH), F32),
            pltpu.VMEM((ts, ATTN_W), BF16),
            pltpu.VMEM((2 * N_KV_HEADS, WINDOW + ts, KV_W), BF16),
            pltpu.VMEM((2, WINDOW + ts, KV_W), BF16),
            pltpu.VMEM((ts, D_MIX), BF16),
        ],
        compiler_params=pltpu.CompilerParams(
            dimension_semantics=("arbitrary", "arbitrary"), vmem_limit_bytes=VMEM_LIMIT),
        name="mixer",
    )(sinks, x, w_in, b_in, w_dw, b_dw, g_cn, b_cn, w_out, g1, b1)


def _memkv_kernel(mem_ref, w_ref, o_ref):
    o_ref[...] = jnp.dot(mem_ref[...].astype(BF16), w_ref[...].astype(BF16),
                         preferred_element_type=F32).astype(BF16)


def _memkv(mem2d, w_mkv):
    M, D = mem2d.shape
    N = w_mkv.shape[1]
    tn = 512
    return pl.pallas_call(
        _memkv_kernel,
        grid=(N // tn,),
        in_specs=[pl.BlockSpec((M, D), lambda j: (0, 0)), pl.BlockSpec((D, tn), lambda j: (0, j))],
        out_specs=pl.BlockSpec((M, tn), lambda j: (0, j)),
        out_shape=jax.ShapeDtypeStruct((M, N), BF16),
        compiler_params=pltpu.CompilerParams(dimension_semantics=("arbitrary",)),
        name="memkv",
    )(mem2d, w_mkv)


def _first_max(rows):
    best = rows[0]
    for r in rows[1:]:
        best = jnp.maximum(best, r)
    idx = jnp.full(best.shape, len(rows) - 1, jnp.int32)
    for k in range(len(rows) - 2, -1, -1):
        idx = jnp.where(rows[k] == best, k, idx)
    return best, idx


def _route_plan(logits_t):
    tile = logits_t.shape[1]
    row = lambda k: logits_t[k:k + 1, :]
    gmax, g_idx = _first_max([row(g) for g in range(N_GROUPS)])
    gsum = jnp.exp(row(0) - gmax)
    for g in range(1, N_GROUPS):
        gsum = gsum + jnp.exp(row(g) - gmax)
    g_p = 1.0 / gsum
    rl = []
    for e in range(EXPERTS_PER_GROUP):
        v = row(ROUTE_OFF + (N_GROUPS - 1) * EXPERTS_PER_GROUP + e)
        for g in range(N_GROUPS - 2, -1, -1):
            v = jnp.where(g_idx == g, row(ROUTE_OFF + g * EXPERTS_PER_GROUP + e), v)
        rl.append(v)
    m1, i1 = _first_max(rl)
    m2, i2 = _first_max([jnp.where(i1 == e, MASK_VALUE, rl[e]) for e in range(EXPERTS_PER_GROUP)])
    ex = jnp.exp(m2 - m1)
    w1 = 1.0 / (1.0 + ex)
    w2 = ex * w1
    e1 = g_idx * EXPERTS_PER_GROUP + i1
    e2 = g_idx * EXPERTS_PER_GROUP + i2

    eid = lax.broadcasted_iota(jnp.int32, (N_EXPERTS, tile), 0)
    hit1 = eid == e1
    hit2 = eid == e2
    oh = jnp.where(jnp.logical_or(hit1, hit2), 1.0, 0.0)
    r = lax.broadcasted_iota(jnp.int32, (tile, tile), 0)
    c = lax.broadcasted_iota(jnp.int32, (tile, tile), 1)
    tri = jnp.where(r <= c, 1.0, 0.0).astype(BF16)
    csum = jnp.dot(oh.astype(BF16), tri, preferred_element_type=F32)
    counts = jnp.broadcast_to(csum[:, tile - 1:tile], (N_EXPERTS, tile)).astype(jnp.int32)
    nch = jnp.right_shift(counts + (CHUNK - 1), CHUNK.bit_length() - 1)
    er = lax.broadcasted_iota(jnp.int32, (N_EXPERTS, N_EXPERTS), 0)
    ec = lax.broadcasted_iota(jnp.int32, (N_EXPERTS, N_EXPERTS), 1)
    lower = jnp.where(ec < er, 1.0, 0.0).astype(BF16)
    off = jnp.dot(lower, nch.astype(F32).astype(BF16), preferred_element_type=F32) * CHUNK
    pos = off + csum - oh
    lp1 = jnp.sum(jnp.where(hit1, pos, 0.0), axis=0, keepdims=True)
    lp2 = jnp.sum(jnp.where(hit2, pos, 0.0), axis=0, keepdims=True)
    zero = jnp.zeros_like(lp1)
    route_t = jnp.concatenate([lp1, lp2, g_p * w1, g_p * w2, zero, zero, zero, zero], axis=0)
    meta = jnp.concatenate([nch[:, 0:LANES], off[:, 0:LANES].astype(jnp.int32)], axis=0)
    return route_t, meta


def _memattn_kernel(x_ref, wq_ref, k_ref, v_ref, wo_ref, g2_ref, b2_ref, wr2_ref, wrh_ref, br_ref,
                    o_ref, ob_ref, route_ref, routet_ref, meta_ref, wq_b, wo_b):
    @pl.when(jnp.logical_and(pl.program_id(0) == 0, pl.program_id(1) == 0))
    def _():
        _cast_bf16(wq_b, wq_ref)
        _cast_bf16(wo_b, wo_ref)

    def rows_logits(rs):
        x = x_ref[0, rs, :]
        q = jnp.dot(x.astype(BF16), wq_b[...], preferred_element_type=F32)
        q = (q * (MEM_HEAD_DIM ** -0.5)).astype(BF16)
        outs = []
        for h in range(MEM_HEADS):
            sl = slice(h * MEM_HEAD_DIM, (h + 1) * MEM_HEAD_DIM)
            s = _dot_nt(q[:, sl], k_ref[0, :, sl])
            m = jnp.max(s, axis=-1, keepdims=True)
            p = jnp.exp(s - m)
            denom = jnp.sum(p, axis=-1, keepdims=True)
            o = jnp.dot(p.astype(BF16), v_ref[0, :, sl], preferred_element_type=F32)
            outs.append((o / denom).astype(BF16))
        o = jnp.dot(jnp.concatenate(outs, axis=-1), wo_b[...], preferred_element_type=F32)
        x2 = _layer_norm(ALPHA * x + o, g2_ref[...], b2_ref[...])
        o_ref[0, rs, :] = x2
        x2h = x2.astype(BF16)
        ob_ref[0, rs, :] = x2h
        x2l = (x2 - x2h.astype(F32)).astype(BF16)
        hh = jnp.dot(x2h, wr2_ref[...], preferred_element_type=F32)
        return (hh[:, 0:LANES] + hh[:, LANES:2 * LANES]
                + jnp.dot(x2l, wrh_ref[...], preferred_element_type=F32) + br_ref[...])

    ts = x_ref.shape[1]
    groups = [slice(r0, r0 + MOE_TILE) for r0 in range(0, ts, MOE_TILE)]
    logits = [rows_logits(rs) for rs in groups]
    pad = jnp.zeros((LANES - SUBLANES, MOE_TILE), F32)
    for k, rs in enumerate(groups):
        route_t, meta = _route_plan(jnp.transpose(logits[k]))
        routet_ref[k] = route_t
        route_ref[0, rs, :] = jnp.transpose(jnp.concatenate([route_t, pad], axis=0))
        meta_ref[k] = meta


def _memattn(x1, wq, kvm, wo, g2, b2, wr, br):
    B, S, D = x1.shape
    ts = MEM_TILE
    per = ts // MOE_TILE
    nt = S // ts
    c = wr * (2.0 ** 16 + 1.0)
    w_high = c - (c - wr)
    wrh = w_high.astype(BF16)
    wr2 = jnp.concatenate([wrh, (wr - w_high).astype(BF16)], axis=1)
    tile = pl.BlockSpec((1, ts, D), lambda b, i: (b, i, 0))
    kspec = pl.BlockSpec((1, MEM_LEN, D), lambda b, i: (b, 0, 0))
    vspec = pl.BlockSpec((1, MEM_LEN, D), lambda b, i: (b, 0, 1))
    return pl.pallas_call(
        _memattn_kernel,
        grid=(B, nt),
        in_specs=[tile, _resident_spec((D, D)), kspec, vspec, _resident_spec((D, D)),
                  _const_spec((1, D)), _const_spec((1, D)),
                  _const_spec((D, 2 * LANES)), _const_spec((D, LANES)), _const_spec((1, LANES))],
        out_specs=[tile, tile, pl.BlockSpec((1, ts, LANES), lambda b, i: (b, i, 0)),
                   pl.BlockSpec((per, SUBLANES, MOE_TILE), lambda b, i: (b * nt + i, 0, 0)),
                   pl.BlockSpec((per, 2 * N_EXPERTS, LANES), lambda b, i: (b * nt + i, 0, 0))],
        out_shape=[jax.ShapeDtypeStruct((B, S, D), F32),
                   jax.ShapeDtypeStruct((B, S, D), BF16),
                   jax.ShapeDtypeStruct((B, S, LANES), F32),
                   jax.ShapeDtypeStruct((B * nt * per, SUBLANES, MOE_TILE), F32),
                   jax.ShapeDtypeStruct((B * nt * per, 2 * N_EXPERTS, LANES), jnp.int32)],
        scratch_shapes=[pltpu.VMEM((D, D), BF16), pltpu.VMEM((D, D), BF16)],
        compiler_params=pltpu.CompilerParams(
            dimension_semantics=("arbitrary", "arbitrary"), vmem_limit_bytes=VMEM_LIMIT),
        name="memattn",
    )(x1, wq, kvm, kvm, wo, g2, b2, wr2, wrh, br)


def _gmm_blocks(n_tokens):
    rows = (n_tokens * TOP_K + (n_tokens // MOE_TILE) * N_EXPERTS * (CHUNK - 1)
            + N_EXPERTS * (GMM_ROWS - CHUNK))
    return -(-rows // GMM_ROWS)


def _plan(meta, T):
    nch = meta[:, :N_EXPERTS, 0]
    n16 = nch * CHUNK
    n_e = jnp.sum(n16, axis=0)
    reg = (n_e + GMM_ROWS - 1) // GMM_ROWS * GMM_ROWS
    gend = jnp.cumsum(reg)
    gbase = gend - reg
    dst = gbase[None, :] + jnp.cumsum(n16, axis=0) - n16
    tail_blocks = (_gmm_blocks(T) * GMM_ROWS - gend[-1:]) // GMM_ROWS
    src = (jnp.cumsum(nch, axis=1) - nch) * CHUNK
    n_big = nch // BIG

    def copy_list(count, src0, dst0, rows, length):
        cum = jnp.cumsum(count, axis=1)
        first = (cum - count)[:, None, :]
        k = jnp.arange(length, dtype=jnp.int32)[None, :, None]
        mine = (k >= first) & (k < cum[:, None, :])
        step = (k - first) * rows
        pick = lambda base: jnp.sum(jnp.where(mine, base[:, None, :] + step, 0), axis=2)
        return pick(src0), pick(dst0), cum[:, -1]

    big_src, big_dst, big_n = copy_list(n_big, src, dst, BIG * CHUNK, MAX_BIG)
    rest = n_big * (BIG * CHUNK)
    small_src, small_dst, small_n = copy_list(nch - n_big * BIG, src + rest, dst + rest, CHUNK, MAX_SMALL)
    i32 = lambda a: a.astype(jnp.int32)
    fill_start = jnp.concatenate([gbase + n_e, gend[-1:]])
    fill_n = jnp.concatenate([reg - n_e, _gmm_blocks(T) * GMM_ROWS - gend[-1:]]) // CHUNK
    return dict(big_src=i32(big_src).reshape(-1), big_dst=i32(big_dst).reshape(-1), big_n=i32(big_n),
                small_src=i32(small_src).reshape(-1), small_dst=i32(small_dst).reshape(-1), small_n=i32(small_n),
                fill_start=i32(fill_start), fill_n=i32(fill_n),
                fill_tot=i32(jnp.sum(fill_n)).reshape(1), exp_blocks=i32(reg // GMM_ROWS),
                exp_base=i32(gbase), tail_blocks=i32(tail_blocks))


def _rows_copy(src_ref, src_row, dst_ref, dst_row, rows, sem):
    return pltpu.make_async_copy(
        src_ref.at[pl.ds(pl.multiple_of(src_row, CHUNK), rows), :],
        dst_ref.at[pl.ds(pl.multiple_of(dst_row, CHUNK), rows), :], sem)


def _chunk_copy(src_ref, src_row, dst_ref, dst_row, sem):
    return _rows_copy(src_ref, src_row, dst_ref, dst_row, CHUNK, sem)


def _issue_copies(lists, tile, tile_ref, tile_is_src, hbm_ref, sem):
    big_src, big_dst, big_n, small_src, small_dst, small_n = lists
    for src_l, dst_l, n_l, length, rows in ((big_src, big_dst, big_n, MAX_BIG, BIG * CHUNK),
                                            (small_src, small_dst, small_n, MAX_SMALL, CHUNK)):
        def issue(k, carry, src_l=src_l, dst_l=dst_l, length=length, rows=rows):
            local, remote = src_l[tile * length + k], dst_l[tile * length + k]
            if tile_is_src:
                _rows_copy(tile_ref, local, hbm_ref, remote, rows, sem).start()
            else:
                _rows_copy(hbm_ref, remote, tile_ref, local, rows, sem).start()
            return carry

        lax.fori_loop(0, n_l[tile], issue, 0)


def _wait_rows(n, rows, src_ref, dst_ref, sem):
    def body(c, carry):
        pltpu.make_async_copy(src_ref.at[pl.ds(0, rows), :], dst_ref.at[pl.ds(0, rows), :], sem).wait()
        return carry

    lax.fori_loop(0, n, body, 0)


def _wait_copies(lists, tile, src_ref, dst_ref, sem, extra_chunks=0):
    _wait_rows(lists[2][tile], BIG * CHUNK, src_ref, dst_ref, sem)
    n = lists[5][tile] + extra_chunks
    _wait_rows(n // WAIT_GROUP, WAIT_GROUP * CHUNK, src_ref, dst_ref, sem)
    _wait_rows(n % WAIT_GROUP, CHUNK, src_ref, dst_ref, sem)


def _dispatch_kernel(bs_ref, bd_ref, bn_ref, ss_ref, sd_ref, sn_ref, fstart_ref, fn_ref, ftot_ref,
                     lp_ref, x_ref, xs_hbm, xt, zbuf, sems):
    lists = (bs_ref, bd_ref, bn_ref, ss_ref, sd_ref, sn_ref)
    s = pl.program_id(0)
    ns = pl.num_programs(0)
    cur = (s % 2) * PER_STEP
    prv = PER_STEP - cur

    @pl.when(s >= 2)
    def _():
        for k in range(PER_STEP):
            _wait_copies(lists, (s - 2) * PER_STEP + k, xt.at[cur + k], xs_hbm, sems.at[cur + k])

    r = lax.broadcasted_iota(jnp.int32, (SORT_ROWS, MOE_TILE), 0)
    for k in range(PER_STEP):
        lp = lp_ref[k].astype(jnp.int32)
        hit = jnp.logical_or(lp[0:1, :] == r, lp[1:2, :] == r)
        p = jnp.where(hit, 1.0, 0.0).astype(BF16)
        x = x_ref[k * MOE_TILE:(k + 1) * MOE_TILE, :]
        xt[cur + k] = jnp.dot(p, x, preferred_element_type=F32).astype(BF16)

    for k in range(PER_STEP):
        _issue_copies(lists, s * PER_STEP + k, xt.at[cur + k], True, xs_hbm, sems.at[cur + k])

    @pl.when(s == ns - 1)
    def _():
        zbuf[...] = jnp.zeros_like(zbuf)
        sem = sems.at[cur]

        def per_range(e, carry):
            def issue(c, carry2):
                _chunk_copy(zbuf, 0, xs_hbm, fstart_ref[e] + c * CHUNK, sem).start()
                return carry2

            return lax.fori_loop(0, fn_ref[e], issue, carry)

        lax.fori_loop(0, N_EXPERTS + 1, per_range, 0)
        for k in range(PER_STEP):
            _wait_copies(lists, (s - 1) * PER_STEP + k, xt.at[prv + k], xs_hbm, sems.at[prv + k])
            _wait_copies(lists, s * PER_STEP + k, xt.at[cur + k], xs_hbm, sems.at[cur + k],
                         extra_chunks=ftot_ref[0] if k == 0 else 0)


def _copy_lists(plan):
    return tuple(plan[k] for k in ("big_src", "big_dst", "big_n", "small_src", "small_dst", "small_n"))


def _dispatch(plan, lpt, x2b):
    T, D = x2b.shape
    rows = PER_STEP * MOE_TILE
    assert T // rows >= 2
    grid_spec = pltpu.PrefetchScalarGridSpec(
        num_scalar_prefetch=9,
        grid=(T // rows,),
        in_specs=[pl.BlockSpec((PER_STEP, SUBLANES, MOE_TILE), lambda t, *_: (t, 0, 0)),
                  pl.BlockSpec((rows, D), lambda t, *_: (t, 0))],
        out_specs=pl.BlockSpec(memory_space=pl.ANY),
        scratch_shapes=[pltpu.VMEM((2 * PER_STEP, SORT_ROWS, D), BF16), pltpu.VMEM((CHUNK, D), BF16),
                        pltpu.SemaphoreType.DMA((2 * PER_STEP,))],
    )
    return pl.pallas_call(
        _dispatch_kernel,
        grid_spec=grid_spec,
        out_shape=jax.ShapeDtypeStruct((_gmm_blocks(T) * GMM_ROWS, D), BF16),
        compiler_params=pltpu.CompilerParams(
            dimension_semantics=("arbitrary",), vmem_limit_bytes=VMEM_LIMIT),
        name="moe_dispatch",
    )(*_copy_lists(plan), plan["fill_start"], plan["fill_n"], plan["fill_tot"], lpt, x2b)


def _expert_ffn(xb, wg_b, wu_b, wd_b):
    g = jnp.dot(xb, wg_b[...], preferred_element_type=F32)
    u = jnp.dot(xb, wu_b[...], preferred_element_type=F32)
    h = (g * jax.nn.sigmoid(g)) * u
    return jnp.dot(h.astype(BF16), wd_b[...], preferred_element_type=F32).astype(BF16)


def _gmm_kernel(nb_ref, base_ref, nfill_ref, x_hbm, wg_ref, wu_ref, wd_ref, y_hbm,
                wg_b, wu_b, wd_b, xbuf, ybuf, xsem, ysem):
    e = pl.program_id(0)
    nb = nb_ref[e]
    base = base_ref[e]

    def rows_at(i):
        return pl.ds(pl.multiple_of(base + i * GMM_ROWS, GMM_ROWS), GMM_ROWS)

    def x_copy(i, slot):
        return pltpu.make_async_copy(x_hbm.at[rows_at(i), :], xbuf.at[slot], xsem.at[slot])

    def y_copy(i, slot):
        return pltpu.make_async_copy(ybuf.at[slot], y_hbm.at[rows_at(i), :], ysem.at[slot])

    @pl.when(nb > 0)
    def _():
        x_copy(0, 0).start(priority=ROW_DMA_PRIORITY)
        _cast_bf16(wg_b, wg_ref.at[0])
        _cast_bf16(wu_b, wu_ref.at[0])
        _cast_bf16(wd_b, wd_ref.at[0])

    def block(i, carry):
        slot = i % 2

        @pl.when(i + 1 < nb)
        def _():
            x_copy(i + 1, 1 - slot).start(priority=ROW_DMA_PRIORITY)

        x_copy(i, slot).wait()

        @pl.when(i >= 2)
        def _():
            y_copy(i - 2, slot).wait()

        ybuf[slot] = _expert_ffn(xbuf[slot], wg_b, wu_b, wd_b)
        y_copy(i, slot).start()
        return carry

    lax.fori_loop(0, nb, block, 0)

    @pl.when(nb >= 2)
    def _():
        y_copy(nb - 2, nb % 2).wait()

    @pl.when(nb >= 1)
    def _():
        y_copy(nb - 1, (nb - 1) % 2).wait()

    @pl.when(e == pl.num_programs(0) - 1)
    def _():
        ybuf[0] = jnp.zeros(ybuf.shape[1:], ybuf.dtype)
        nfill = nfill_ref[0]

        def fill_copy(k):
            return y_copy(nb + k, 0)

        lax.fori_loop(0, nfill, lambda k, c: (fill_copy(k).start(), c)[1], 0)
        lax.fori_loop(0, nfill, lambda k, c: (fill_copy(k).wait(), c)[1], 0)


def _gmm(plan, xs, wg, wu, wd):
    R, D = xs.shape
    wspec = lambda shape: pl.BlockSpec((1,) + shape, lambda e, *_: (e, 0, 0))
    grid_spec = pltpu.PrefetchScalarGridSpec(
        num_scalar_prefetch=3,
        grid=(N_EXPERTS,),
        in_specs=[pl.BlockSpec(memory_space=pl.ANY),
                  wspec((D, D_EXPERT)), wspec((D, D_EXPERT)), wspec((D_EXPERT, D))],
        out_specs=pl.BlockSpec(memory_space=pl.ANY),
        scratch_shapes=[pltpu.VMEM((D, D_EXPERT), BF16), pltpu.VMEM((D, D_EXPERT), BF16),
                        pltpu.VMEM((D_EXPERT, D), BF16),
                        pltpu.VMEM((2, GMM_ROWS, D), BF16), pltpu.VMEM((2, GMM_ROWS, D), BF16),
                        pltpu.SemaphoreType.DMA((2,)), pltpu.SemaphoreType.DMA((2,))],
    )
    return pl.pallas_call(
        _gmm_kernel,
        grid_spec=grid_spec,
        out_shape=jax.ShapeDtypeStruct((R, D), BF16),
        compiler_params=pltpu.CompilerParams(
            dimension_semantics=("arbitrary",), vmem_limit_bytes=VMEM_LIMIT),
        name="moe_gmm",
    )(plan["exp_blocks"], plan["exp_base"], plan["tail_blocks"], xs, wg, wu, wd)


def _combine_kernel(bs_ref, bd_ref, bn_ref, ss_ref, sd_ref, sn_ref, cm_ref, x_ref, ys_hbm, g3_ref, b3_ref,
                    o_ref, yt, sems):
    lists = (bs_ref, bd_ref, bn_ref, ss_ref, sd_ref, sn_ref)
    s = pl.program_id(0)
    ns = pl.num_programs(0)
    cur = (s % 2) * PER_STEP
    nxt = PER_STEP - cur

    def fetch(step, first_buf):
        for k in range(PER_STEP):
            _issue_copies(lists, step * PER_STEP + k, yt.at[first_buf + k], False, ys_hbm,
                          sems.at[first_buf + k])

    @pl.when(s == 0)
    def _():
        yt[...] = jnp.zeros_like(yt)
        fetch(0, 0)

    @pl.when(s + 1 < ns)
    def _():
        fetch(s + 1, nxt)

    for k in range(PER_STEP):
        _wait_copies(lists, s * PER_STEP + k, ys_hbm, yt.at[cur + k], sems.at[cur + k])

    col = lax.broadcasted_iota(jnp.int32, (MOE_TILE, SORT_ROWS), 1)
    for k in range(PER_STEP):
        rs = slice(k * MOE_TILE, (k + 1) * MOE_TILE)
        cm = cm_ref[rs, :]
        lp0 = cm[:, 0:1].astype(jnp.int32)
        lp1 = cm[:, 1:2].astype(jnp.int32)
        w = jnp.where(col == lp0, cm[:, 2:3], jnp.where(col == lp1, cm[:, 3:4], 0.0)).astype(BF16)
        y = jnp.dot(w, yt[cur + k], preferred_element_type=F32)
        o_ref[rs, :] = _layer_norm(ALPHA * x_ref[rs, :] + y, g3_ref[...], b3_ref[...])


def _combine(plan, route, x2, ys, g3, b3):
    T, D = x2.shape
    rows = PER_STEP * MOE_TILE
    grid_spec = pltpu.PrefetchScalarGridSpec(
        num_scalar_prefetch=6,
        grid=(T // rows,),
        in_specs=[pl.BlockSpec((rows, LANES), lambda t, *_: (t, 0)),
                  pl.BlockSpec((rows, D), lambda t, *_: (t, 0)),
                  pl.BlockSpec(memory_space=pl.ANY),
                  pl.BlockSpec((1, D), lambda t, *_: (0, 0)),
                  pl.BlockSpec((1, D), lambda t, *_: (0, 0))],
        out_specs=pl.BlockSpec((rows, D), lambda t, *_: (t, 0)),
        scratch_shapes=[pltpu.VMEM((2 * PER_STEP, SORT_ROWS, D), BF16),
                        pltpu.SemaphoreType.DMA((2 * PER_STEP,))],
    )
    return pl.pallas_call(
        _combine_kernel,
        grid_spec=grid_spec,
        out_shape=jax.ShapeDtypeStruct((T, D), F32),
        compiler_params=pltpu.CompilerParams(
            dimension_semantics=("arbitrary",), vmem_limit_bytes=VMEM_LIMIT),
        name="moe_combine",
    )(*_copy_lists(plan), route, x2, ys, g3, b3)


def _moe(x2, x2b, route, lpt, meta, wg, wu, wd, g3, b3):
    plan = _plan(meta, x2.shape[0])
    xs = _dispatch(plan, lpt, x2b)
    ys = _gmm(plan, xs, wg, wu, wd)
    return _combine(plan, route, x2, ys, g3, b3)


def _row(v):
    return v.reshape(1, -1).astype(F32)


def kernel(x, mem, w_in, b_in, w_dw, b_dw, g_conv_norm, b_conv_norm, attn_sinks, w_out, g_ln1, b_ln1,
           w_mq, w_mkv, w_mo, g_ln2, b_ln2, w_group, b_group, w_router, b_router, w_gate, w_up, w_down,
           g_ln3, b_ln3):
    B, S, D = x.shape
    for l in range(DEPTH):
        w_dw_p = jnp.zeros((CONV_HALO, CONV_CH), F32).at[:CONV_WIDTH].set(w_dw[l])
        x1 = _mixer(x, attn_sinks[l].astype(F32), w_in[l], _row(b_in[l]), w_dw_p,
                    _row(b_dw[l]), _row(g_conv_norm[l]), _row(b_conv_norm[l]),
                    w_out[l], _row(g_ln1[l]), _row(b_ln1[l]))

        kvm = _memkv(mem.reshape(B * MEM_LEN, D), w_mkv[l]).reshape(B, MEM_LEN, 2 * D)

        wr = jnp.concatenate(
            [w_group[l], jnp.transpose(w_router[l], (1, 0, 2)).reshape(D, N_EXPERTS)], axis=1)
        wr = jnp.pad(wr, ((0, 0), (0, LANES - wr.shape[1])))
        br = jnp.pad(jnp.concatenate([b_group[l], b_router[l].reshape(-1)]), (0, LANES - N_GROUPS - N_EXPERTS))
        x2, x2b, route, lpt, meta = _memattn(x1, w_mq[l], kvm, w_mo[l],
                                 _row(g_ln2[l]), _row(b_ln2[l]), wr.astype(F32), _row(br))

        T = B * S
        y = _moe(x2.reshape(T, D), x2b.reshape(T, D), route.reshape(T, LANES), lpt, meta,
                 w_gate[l], w_up[l], w_down[l],
                 _row(g_ln3[l]), _row(b_ln3[l]))
        x = y.reshape(B, S, D)
    return x
```

```python
import functools

import jax
import jax.numpy as jnp
from jax import lax
from jax.experimental import pallas as pl
from jax.experimental.pallas import tpu as pltpu

D_MODEL = 1024
MEM_LEN = 256
CONV_CH = 512
CONV_WIDTH = 31
N_HEADS = 8
N_KV_HEADS = 2
HEAD_DIM = 64
GQ = N_HEADS // N_KV_HEADS
ATTN_W = N_HEADS * HEAD_DIM
KV_W = N_KV_HEADS * HEAD_DIM
WINDOW = 128
D_MIX = CONV_CH + ATTN_W
D_IN = 2 * CONV_CH + ATTN_W + 2 * KV_W
MEM_HEADS = 4
MEM_HEAD_DIM = D_MODEL // MEM_HEADS
N_GROUPS = 4
EXPERTS_PER_GROUP = 4
N_EXPERTS = N_GROUPS * EXPERTS_PER_GROUP
D_EXPERT = D_MODEL // 2
DEPTH = 1
ALPHA = (2.0 * DEPTH) ** 0.25
LN_EPS = 1e-5

LANES = 128
SUBLANES = 8
CONV_ROWS = 128
LN_ROWS = 64
MEM_TILE = 1024
MASK_VALUE = -1e30
CONV_HALO = 32
SEQ_TILE = 512
MOE_TILE = 512
CHUNK = 16
GMM_ROWS = 1024
TOP_K = 2
SORT_ROWS = -(-(MOE_TILE * TOP_K + N_EXPERTS * (CHUNK - 1)) // 256) * 256
BIG = 4
MAX_BIG = SORT_ROWS // (BIG * CHUNK)
MAX_SMALL = N_EXPERTS * (BIG - 1)
ROW_DMA_PRIORITY = 1
PER_STEP = 2
WAIT_GROUP = 8
ROUTE_OFF = N_GROUPS
VMEM_LIMIT = 56 * 1024 * 1024

BF16 = jnp.bfloat16
F32 = jnp.float32


def _layer_norm(x, g, b):
    mu = jnp.mean(x, axis=-1, keepdims=True)
    xc = x - mu
    var = jnp.mean(xc * xc, axis=-1, keepdims=True)
    return xc * lax.rsqrt(var + LN_EPS) * g + b


def _cast_bf16(dst_ref, src_ref):
    rows = 256
    for r0 in range(0, src_ref.shape[0], rows):
        dst_ref[r0:r0 + rows, :] = src_ref[r0:r0 + rows, :].astype(BF16)


def _dot_nt(a, b):
    return lax.dot_general(a, b, (((1,), (1,)), ((), ())), preferred_element_type=F32)


def _mixer_kernel(sinks_ref, x_ref, w_in_ref, b_in_ref, w_dw_ref, b_dw_ref, g_cn_ref, b_cn_ref,
                  w_out_ref, g1_ref, b1_ref, o_ref, w_in_b, w_out_b, hbuf, hshift, cbuf, qbuf, kbuf, vbuf, ymix):
    i = pl.program_id(1)
    ts = SEQ_TILE

    @pl.when(jnp.logical_and(pl.program_id(0) == 0, i == 0))
    def _():
        _cast_bf16(w_in_b, w_in_ref)
        _cast_bf16(w_out_b, w_out_ref)

    @pl.when(i == 0)
    def _():
        hbuf[0:CONV_HALO, :] = jnp.zeros((CONV_HALO, CONV_CH), F32)
        kbuf[:, 0:WINDOW, :] = jnp.zeros((2 * N_KV_HEADS, WINDOW, KV_W), BF16)
        vbuf[:, 0:WINDOW, :] = jnp.zeros((2, WINDOW, KV_W), BF16)

    x = x_ref[0]
    u = jnp.dot(x.astype(BF16), w_in_b[...], preferred_element_type=F32) + b_in_ref[...]
    a = u[:, 0:CONV_CH]
    gate = u[:, CONV_CH:2 * CONV_CH]
    hbuf[CONV_HALO:CONV_HALO + ts, :] = a * jax.nn.sigmoid(gate)
    qbuf[...] = (u[:, 2 * CONV_CH:2 * CONV_CH + ATTN_W] * (HEAD_DIM ** -0.5)).astype(BF16)
    kf = u[:, 2 * CONV_CH + ATTN_W:2 * CONV_CH + ATTN_W + KV_W]
    vf = u[:, 2 * CONV_CH + ATTN_W + KV_W:D_IN]
    kr = pltpu.roll(kf, HEAD_DIM, axis=1)
    vr = pltpu.roll(vf, HEAD_DIM, axis=1)
    lo = lax.broadcasted_iota(jnp.int32, (ts, KV_W), 1) < HEAD_DIM
    rows = slice(WINDOW, WINDOW + ts)
    kbuf[0, rows, :] = jnp.where(lo, kf, 0.0).astype(BF16)
    kbuf[1, rows, :] = jnp.where(lo, 0.0, kr).astype(BF16)
    kbuf[2, rows, :] = jnp.where(lo, kr, 0.0).astype(BF16)
    kbuf[3, rows, :] = jnp.where(lo, 0.0, kf).astype(BF16)
    vbuf[0, rows, :] = vf.astype(BF16)
    vbuf[1, rows, :] = vr.astype(BF16)

    base = CONV_HALO - (CONV_WIDTH - 1)
    n_shift = ts + CONV_HALO - SUBLANES
    for b in range(1, SUBLANES):
        hshift[b - 1, 0:n_shift, :] = hbuf[b:b + n_shift, :]
    rc = CONV_ROWS

    def conv_chunk(c):
        r0 = c * rc
        for l in range(CONV_CH // LANES):
            ls = slice(l * LANES, (l + 1) * LANES)
            acc = jnp.zeros((rc, LANES), F32)
            for j in range(CONV_WIDTH):
                a8, b = divmod(j + base, SUBLANES)
                rs = slice(r0 + SUBLANES * a8, r0 + SUBLANES * a8 + rc)
                tap = hbuf[rs, ls] if b == 0 else hshift[b - 1, rs, ls]
                acc = acc + tap * w_dw_ref[j:j + 1, ls]
            cbuf[r0:r0 + rc, ls] = acc
        for r1 in range(r0, r0 + rc, LN_ROWS):
            rs = slice(r1, r1 + LN_ROWS)
            y = _layer_norm(cbuf[rs, :] + b_dw_ref[...], g_cn_ref[...], b_cn_ref[...])
            y = y * jax.nn.sigmoid(y)
            ymix[rs, 0:CONV_CH] = y.astype(BF16)

    qi = lax.broadcasted_iota(jnp.int32, (2 * WINDOW, 2 * WINDOW), 0) % WINDOW
    kj = lax.broadcasted_iota(jnp.int32, (2 * WINDOW, 2 * WINDOW), 1)
    dist = qi + WINDOW - kj
    band = (dist >= 0) & (dist < WINDOW)
    top = lax.broadcasted_iota(jnp.int32, (2 * WINDOW, 1), 0) < WINDOW
    lo_out = lax.broadcasted_iota(jnp.int32, (WINDOW, 2 * HEAD_DIM), 1) < HEAD_DIM

    def attn_block(jb):
        r0 = jb * WINDOW
        valid = band & jnp.logical_or(i != 0, kj >= WINDOW) if jb == 0 else band
        for kvh in range(N_KV_HEADS):
            h0 = kvh * GQ
            c0 = h0 * HEAD_DIM
            qs = jnp.concatenate([qbuf[r0:r0 + WINDOW, c0:c0 + 2 * HEAD_DIM],
                                  qbuf[r0:r0 + WINDOW, c0 + 2 * HEAD_DIM:c0 + 4 * HEAD_DIM]], axis=0)
            pv = []
            for par in range(2):
                kk = kbuf[2 * kvh + par, r0:r0 + 2 * WINDOW, :]
                vv = vbuf[(kvh + par) % 2, r0:r0 + 2 * WINDOW, :]
                s = jnp.where(valid, _dot_nt(qs, kk), MASK_VALUE)
                sink = jnp.where(top, sinks_ref[h0 + par], sinks_ref[h0 + 2 + par])
                m = jnp.maximum(jnp.max(s, axis=-1, keepdims=True), sink)
                p = jnp.exp(s - m)
                denom = jnp.sum(p, axis=-1, keepdims=True) + jnp.exp(sink - m)
                pv.append(jnp.dot(p.astype(BF16), vv, preferred_element_type=F32) / denom)
            for pair in range(2):
                rs = slice(pair * WINDOW, (pair + 1) * WINDOW)
                o = jnp.where(lo_out, pv[0][rs], pv[1][rs])
                cs = CONV_CH + c0 + pair * 2 * HEAD_DIM
                ymix[r0:r0 + WINDOW, cs:cs + 2 * HEAD_DIM] = o.astype(BF16)

    assert ts // rc == ts // WINDOW
    for c in range(ts // rc):
        attn_block(c)
        conv_chunk(c)

    mix = jnp.dot(ymix[...], w_out_b[...], preferred_element_type=F32)
    o_ref[0] = _layer_norm(ALPHA * x + mix, g1_ref[...], b1_ref[...])

    hbuf[0:CONV_HALO, :] = hbuf[ts:ts + CONV_HALO, :]
    kbuf[:, 0:WINDOW, :] = kbuf[:, ts:ts + WINDOW, :]
    vbuf[:, 0:WINDOW, :] = vbuf[:, ts:ts + WINDOW, :]


def _const_spec(shape):
    nd = len(shape)
    return pl.BlockSpec(shape, lambda *_: (0,) * nd)


def _resident_spec(shape):
    nd = len(shape)
    return pl.BlockSpec(shape, lambda *_: (0,) * nd, pipeline_mode=pl.Buffered(1))


def _mixer(x, sinks, w_in, b_in, w_dw, b_dw, g_cn, b_cn, w_out, g1, b1):
    B, S, D = x.shape
    ts = SEQ_TILE
    tile = pl.BlockSpec((1, ts, D), lambda b, i: (b, i, 0))
    return pl.pallas_call(
        _mixer_kernel,
        grid=(B, S // ts),
        in_specs=[
            pl.BlockSpec(memory_space=pltpu.SMEM),
            tile,
            _resident_spec((D, D_IN)), _const_spec((1, D_IN)),
            _const_spec((CONV_HALO, CONV_CH)), _const_spec((1, CONV_CH)),
            _const_spec((1, CONV_CH)), _const_spec((1, CONV_CH)),
            _resident_spec((D_MIX, D)), _const_spec((1, D)), _const_spec((1, D)),
        ],
        out_specs=tile,
        out_shape=jax.ShapeDtypeStruct((B, S, D), F32),
        scratch_shapes=[
            pltpu.VMEM((D, D_IN), BF16),
            pltpu.VMEM((D_MIX, D), BF16),
            pltpu.VMEM((CONV_HALO + ts, CONV_CH), F32),
            pltpu.VMEM((SUBLANES - 1, CONV_HALO + ts, CONV_CH), F32),
            pltpu.VMEM((ts, CONV_CH), F32),
            pltpu.VMEM((ts, ATTN_W), BF16),
            pltpu.VMEM((2 * N_KV_HEADS, WINDOW + ts, KV_W), BF16),
            pltpu.VMEM((2, WINDOW + ts, KV_W), BF16),
            pltpu.VMEM((ts, D_MIX), BF16),
        ],
        compiler_params=pltpu.CompilerParams(
            dimension_semantics=("arbitrary", "arbitrary"), vmem_limit_bytes=VMEM_LIMIT),
        name="mixer",
    )(sinks, x, w_in, b_in, w_dw, b_dw, g_cn, b_cn, w_out, g1, b1)


def _memkv_kernel(mem_ref, w_ref, o_ref):
    o_ref[...] = jnp.dot(mem_ref[...].astype(BF16), w_ref[...].astype(BF16),
                         preferred_element_type=F32).astype(BF16)


def _memkv(mem2d, w_mkv):
    M, D = mem2d.shape
    N = w_mkv.shape[1]
    tn = 512
    return pl.pallas_call(
        _memkv_kernel,
        grid=(N // tn,),
        in_specs=[pl.BlockSpec((M, D), lambda j: (0, 0)), pl.BlockSpec((D, tn), lambda j: (0, j))],
        out_specs=pl.BlockSpec((M, tn), lambda j: (0, j)),
        out_shape=jax.ShapeDtypeStruct((M, N), BF16),
        compiler_params=pltpu.CompilerParams(dimension_semantics=("arbitrary",)),
        name="memkv",
    )(mem2d, w_mkv)


def _first_max(rows):
    best = rows[0]
    for r in rows[1:]:
        best = jnp.maximum(best, r)
    idx = jnp.full(best.shape, len(rows) - 1, jnp.int32)
    for k in range(len(rows) - 2, -1, -1):
        idx = jnp.where(rows[k] == best, k, idx)
    return best, idx


def _route_plan(logits_t):
    tile = logits_t.shape[1]
    row = lambda k: logits_t[k:k + 1, :]
    gmax, g_idx = _first_max([row(g) for g in range(N_GROUPS)])
    gsum = jnp.exp(row(0) - gmax)
    for g in range(1, N_GROUPS):
        gsum = gsum + jnp.exp(row(g) - gmax)
    g_p = 1.0 / gsum
    rl = []
    for e in range(EXPERTS_PER_GROUP):
        v = row(ROUTE_OFF + (N_GROUPS - 1) * EXPERTS_PER_GROUP + e)
        for g in range(N_GROUPS - 2, -1, -1):
            v = jnp.where(g_idx == g, row(ROUTE_OFF + g * EXPERTS_PER_GROUP + e), v)
        rl.append(v)
    m1, i1 = _first_max(rl)
    m2, i2 = _first_max([jnp.where(i1 == e, MASK_VALUE, rl[e]) for e in range(EXPERTS_PER_GROUP)])
    ex = jnp.exp(m2 - m1)
    w1 = 1.0 / (1.0 + ex)
    w2 = ex * w1
    e1 = g_idx * EXPERTS_PER_GROUP + i1
    e2 = g_idx * EXPERTS_PER_GROUP + i2

    eid = lax.broadcasted_iota(jnp.int32, (N_EXPERTS, tile), 0)
    hit1 = eid == e1
    hit2 = eid == e2
    oh = jnp.where(jnp.logical_or(hit1, hit2), 1.0, 0.0)
    r = lax.broadcasted_iota(jnp.int32, (tile, tile), 0)
    c = lax.broadcasted_iota(jnp.int32, (tile, tile), 1)
    tri = jnp.where(r <= c, 1.0, 0.0).astype(BF16)
    csum = jnp.dot(oh.astype(BF16), tri, preferred_element_type=F32)
    counts = jnp.broadcast_to(csum[:, tile - 1:tile], (N_EXPERTS, tile)).astype(jnp.int32)
    nch = jnp.right_shift(counts + (CHUNK - 1), CHUNK.bit_length() - 1)
    er = lax.broadcasted_iota(jnp.int32, (N_EXPERTS, N_EXPERTS), 0)
    ec = lax.broadcasted_iota(jnp.int32, (N_EXPERTS, N_EXPERTS), 1)
    lower = jnp.where(ec < er, 1.0, 0.0).astype(BF16)
    off = jnp.dot(lower, nch.astype(F32).astype(BF16), preferred_element_type=F32) * CHUNK
    pos = off + csum - oh
    lp1 = jnp.sum(jnp.where(hit1, pos, 0.0), axis=0, keepdims=True)
    lp2 = jnp.sum(jnp.where(hit2, pos, 0.0), axis=0, keepdims=True)
    zero = jnp.zeros_like(lp1)
    route_t = jnp.concatenate([lp1, lp2, g_p * w1, g_p * w2, zero, zero, zero, zero], axis=0)
    meta = jnp.concatenate([nch[:, 0:LANES], off[:, 0:LANES].astype(jnp.int32)], axis=0)
    return route_t, meta


def _memattn_kernel(x_ref, wq_ref, k_ref, v_ref, wo_ref, g2_ref, b2_ref, wr2_ref, wrh_ref, br_ref,
                    o_ref, ob_ref, route_ref, routet_ref, meta_ref, wq_b, wo_b):
    @pl.when(jnp.logical_and(pl.program_id(0) == 0, pl.program_id(1) == 0))
    def _():
        _cast_bf16(wq_b, wq_ref)
        _cast_bf16(wo_b, wo_ref)

    def rows_logits(rs):
        x = x_ref[0, rs, :]
        q = jnp.dot(x.astype(BF16), wq_b[...], preferred_element_type=F32)
        q = (q * (MEM_HEAD_DIM ** -0.5)).astype(BF16)
        outs = []
        for h in range(MEM_HEADS):
            sl = slice(h * MEM_HEAD_DIM, (h + 1) * MEM_HEAD_DIM)
            s = _dot_nt(q[:, sl], k_ref[0, :, sl])
            m = jnp.max(s, axis=-1, keepdims=True)
            p = jnp.exp(s - m)
            denom = jnp.sum(p, axis=-1, keepdims=True)
            o = jnp.dot(p.astype(BF16), v_ref[0, :, sl], preferred_element_type=F32)
            outs.append((o / denom).astype(BF16))
        o = jnp.dot(jnp.concatenate(outs, axis=-1), wo_b[...], preferred_element_type=F32)
        x2 = _layer_norm(ALPHA * x + o, g2_ref[...], b2_ref[...])
        o_ref[0, rs, :] = x2
        x2h = x2.astype(BF16)
        ob_ref[0, rs, :] = x2h
        x2l = (x2 - x2h.astype(F32)).astype(BF16)
        hh = jnp.dot(x2h, wr2_ref[...], preferred_element_type=F32)
        return (hh[:, 0:LANES] + hh[:, LANES:2 * LANES]
                + jnp.dot(x2l, wrh_ref[...], preferred_element_type=F32) + br_ref[...])

    ts = x_ref.shape[1]
    groups = [slice(r0, r0 + MOE_TILE) for r0 in range(0, ts, MOE_TILE)]
    logits = [rows_logits(rs) for rs in groups]
    pad = jnp.zeros((LANES - SUBLANES, MOE_TILE), F32)
    for k, rs in enumerate(groups):
        route_t, meta = _route_plan(jnp.transpose(logits[k]))
        routet_ref[k] = route_t
        route_ref[0, rs, :] = jnp.transpose(jnp.concatenate([route_t, pad], axis=0))
        meta_ref[k] = meta


def _memattn(x1, wq, kvm, wo, g2, b2, wr, br):
    B, S, D = x1.shape
    ts = MEM_TILE
    per = ts // MOE_TILE
    nt = S // ts
    c = wr * (2.0 ** 16 + 1.0)
    w_high = c - (c - wr)
    wrh = w_high.astype(BF16)
    wr2 = jnp.concatenate([wrh, (wr - w_high).astype(BF16)], axis=1)
    tile = pl.BlockSpec((1, ts, D), lambda b, i: (b, i, 0))
    kspec = pl.BlockSpec((1, MEM_LEN, D), lambda b, i: (b, 0, 0))
    vspec = pl.BlockSpec((1, MEM_LEN, D), lambda b, i: (b, 0, 1))
    return pl.pallas_call(
        _memattn_kernel,
        grid=(B, nt),
        in_specs=[tile, _resident_spec((D, D)), kspec, vspec, _resident_spec((D, D)),
                  _const_spec((1, D)), _const_spec((1, D)),
                  _const_spec((D, 2 * LANES)), _const_spec((D, LANES)), _const_spec((1, LANES))],
        out_specs=[tile, tile, pl.BlockSpec((1, ts, LANES), lambda b, i: (b, i, 0)),
                   pl.BlockSpec((per, SUBLANES, MOE_TILE), lambda b, i: (b * nt + i, 0, 0)),
                   pl.BlockSpec((per, 2 * N_EXPERTS, LANES), lambda b, i: (b * nt + i, 0, 0))],
        out_shape=[jax.ShapeDtypeStruct((B, S, D), F32),
                   jax.ShapeDtypeStruct((B, S, D), BF16),
                   jax.ShapeDtypeStruct((B, S, LANES), F32),
                   jax.ShapeDtypeStruct((B * nt * per, SUBLANES, MOE_TILE), F32),
                   jax.ShapeDtypeStruct((B * nt * per, 2 * N_EXPERTS, LANES), jnp.int32)],
        scratch_shapes=[pltpu.VMEM((D, D), BF16), pltpu.VMEM((D, D), BF16)],
        compiler_params=pltpu.CompilerParams(
            dimension_semantics=("arbitrary", "arbitrary"), vmem_limit_bytes=VMEM_LIMIT),
        name="memattn",
    )(x1, wq, kvm, kvm, wo, g2, b2, wr2, wrh, br)


def _gmm_blocks(n_tokens):
    rows = (n_tokens * TOP_K + (n_tokens // MOE_TILE) * N_EXPERTS * (CHUNK - 1)
            + N_EXPERTS * (GMM_ROWS - CHUNK))
    return -(-rows // GMM_ROWS)


def _plan(meta, T):
    nch = meta[:, :N_EXPERTS, 0]
    n16 = nch * CHUNK
    n_e = jnp.sum(n16, axis=0)
    reg = (n_e + GMM_ROWS - 1) // GMM_ROWS * GMM_ROWS
    gend = jnp.cumsum(reg)
    gbase = gend - reg
    dst = gbase[None, :] + jnp.cumsum(n16, axis=0) - n16
    tail_blocks = (_gmm_blocks(T) * GMM_ROWS - gend[-1:]) // GMM_ROWS
    src = (jnp.cumsum(nch, axis=1) - nch) * CHUNK
    n_big = nch // BIG

    def copy_list(count, src0, dst0, rows, length):
        cum = jnp.cumsum(count, axis=1)
        first = (cum - count)[:, None, :]
        k = jnp.arange(length, dtype=jnp.int32)[None, :, None]
        mine = (k >= first) & (k < cum[:, None, :])
        step = (k - first) * rows
        pick = lambda base: jnp.sum(jnp.where(mine, base[:, None, :] + step, 0), axis=2)
        return pick(src0), pick(dst0), cum[:, -1]

    big_src, big_dst, big_n = copy_list(n_big, src, dst, BIG * CHUNK, MAX_BIG)
    rest = n_big * (BIG * CHUNK)
    small_src, small_dst, small_n = copy_list(nch - n_big * BIG, src + rest, dst + rest, CHUNK, MAX_SMALL)
    i32 = lambda a: a.astype(jnp.int32)
    fill_start = jnp.concatenate([gbase + n_e, gend[-1:]])
    fill_n = jnp.concatenate([reg - n_e, _gmm_blocks(T) * GMM_ROWS - gend[-1:]]) // CHUNK
    return dict(big_src=i32(big_src).reshape(-1), big_dst=i32(big_dst).reshape(-1), big_n=i32(big_n),
                small_src=i32(small_src).reshape(-1), small_dst=i32(small_dst).reshape(-1), small_n=i32(small_n),
                fill_start=i32(fill_start), fill_n=i32(fill_n),
                fill_tot=i32(jnp.sum(fill_n)).reshape(1), exp_blocks=i32(reg // GMM_ROWS),
                exp_base=i32(gbase), tail_blocks=i32(tail_blocks))


def _rows_copy(src_ref, src_row, dst_ref, dst_row, rows, sem):
    return pltpu.make_async_copy(
        src_ref.at[pl.ds(pl.multiple_of(src_row, CHUNK), rows), :],
        dst_ref.at[pl.ds(pl.multiple_of(dst_row, CHUNK), rows), :], sem)


def _chunk_copy(src_ref, src_row, dst_ref, dst_row, sem):
    return _rows_copy(src_ref, src_row, dst_ref, dst_row, CHUNK, sem)


def _issue_copies(lists, tile, tile_ref, tile_is_src, hbm_ref, sem):
    big_src, big_dst, big_n, small_src, small_dst, small_n = lists
    for src_l, dst_l, n_l, length, rows in ((big_src, big_dst, big_n, MAX_BIG, BIG * CHUNK),
                                            (small_src, small_dst, small_n, MAX_SMALL, CHUNK)):
        def issue(k, carry, src_l=src_l, dst_l=dst_l, length=length, rows=rows):
            local, remote = src_l[tile * length + k], dst_l[tile * length + k]
            if tile_is_src:
                _rows_copy(tile_ref, local, hbm_ref, remote, rows, sem).start()
            else:
                _rows_copy(hbm_ref, remote, tile_ref, local, rows, sem).start()
            return carry

        lax.fori_loop(0, n_l[tile], issue, 0)


def _wait_rows(n, rows, src_ref, dst_ref, sem):
    def body(c, carry):
        pltpu.make_async_copy(src_ref.at[pl.ds(0, rows), :], dst_ref.at[pl.ds(0, rows), :], sem).wait()
        return carry

    lax.fori_loop(0, n, body, 0)


def _wait_copies(lists, tile, src_ref, dst_ref, sem, extra_chunks=0):
    _wait_rows(lists[2][tile], BIG * CHUNK, src_ref, dst_ref, sem)
    n = lists[5][tile] + extra_chunks
    _wait_rows(n // WAIT_GROUP, WAIT_GROUP * CHUNK, src_ref, dst_ref, sem)
    _wait_rows(n % WAIT_GROUP, CHUNK, src_ref, dst_ref, sem)


def _dispatch_kernel(bs_ref, bd_ref, bn_ref, ss_ref, sd_ref, sn_ref, fstart_ref, fn_ref, ftot_ref,
                     lp_ref, x_ref, xs_hbm, xt, zbuf, sems):
    lists = (bs_ref, bd_ref, bn_ref, ss_ref, sd_ref, sn_ref)
    s = pl.program_id(0)
    ns = pl.num_programs(0)
    cur = (s % 2) * PER_STEP
    prv = PER_STEP - cur

    @pl.when(s >= 2)
    def _():
        for k in range(PER_STEP):
            _wait_copies(lists, (s - 2) * PER_STEP + k, xt.at[cur + k], xs_hbm, sems.at[cur + k])

    r = lax.broadcasted_iota(jnp.int32, (SORT_ROWS, MOE_TILE), 0)
    for k in range(PER_STEP):
        lp = lp_ref[k].astype(jnp.int32)
        hit = jnp.logical_or(lp[0:1, :] == r, lp[1:2, :] == r)
        p = jnp.where(hit, 1.0, 0.0).astype(BF16)
        x = x_ref[k * MOE_TILE:(k + 1) * MOE_TILE, :]
        xt[cur + k] = jnp.dot(p, x, preferred_element_type=F32).astype(BF16)

    for k in range(PER_STEP):
        _issue_copies(lists, s * PER_STEP + k, xt.at[cur + k], True, xs_hbm, sems.at[cur + k])

    @pl.when(s == ns - 1)
    def _():
        zbuf[...] = jnp.zeros_like(zbuf)
        sem = sems.at[cur]

        def per_range(e, carry):
            def issue(c, carry2):
                _chunk_copy(zbuf, 0, xs_hbm, fstart_ref[e] + c * CHUNK, sem).start()
                return carry2

            return lax.fori_loop(0, fn_ref[e], issue, carry)

        lax.fori_loop(0, N_EXPERTS + 1, per_range, 0)
        for k in range(PER_STEP):
            _wait_copies(lists, (s - 1) * PER_STEP + k, xt.at[prv + k], xs_hbm, sems.at[prv + k])
            _wait_copies(lists, s * PER_STEP + k, xt.at[cur + k], xs_hbm, sems.at[cur + k],
                         extra_chunks=ftot_ref[0] if k == 0 else 0)


def _copy_lists(plan):
    return tuple(plan[k] for k in ("big_src", "big_dst", "big_n", "small_src", "small_dst", "small_n"))


def _dispatch(plan, lpt, x2b):
    T, D = x2b.shape
    rows = PER_STEP * MOE_TILE
    assert T // rows >= 2
    grid_spec = pltpu.PrefetchScalarGridSpec(
        num_scalar_prefetch=9,
        grid=(T // rows,),
        in_specs=[pl.BlockSpec((PER_STEP, SUBLANES, MOE_TILE), lambda t, *_: (t, 0, 0)),
                  pl.BlockSpec((rows, D), lambda t, *_: (t, 0))],
        out_specs=pl.BlockSpec(memory_space=pl.ANY),
        scratch_shapes=[pltpu.VMEM((2 * PER_STEP, SORT_ROWS, D), BF16), pltpu.VMEM((CHUNK, D), BF16),
                        pltpu.SemaphoreType.DMA((2 * PER_STEP,))],
    )
    return pl.pallas_call(
        _dispatch_kernel,
        grid_spec=grid_spec,
        out_shape=jax.ShapeDtypeStruct((_gmm_blocks(T) * GMM_ROWS, D), BF16),
        compiler_params=pltpu.CompilerParams(
            dimension_semantics=("arbitrary",), vmem_limit_bytes=VMEM_LIMIT),
        name="moe_dispatch",
    )(*_copy_lists(plan), plan["fill_start"], plan["fill_n"], plan["fill_tot"], lpt, x2b)


def _expert_ffn(xb, wg_b, wu_b, wd_b):
    g = jnp.dot(xb, wg_b[...], preferred_element_type=F32)
    u = jnp.dot(xb, wu_b[...], preferred_element_type=F32)
    h = (g * jax.nn.sigmoid(g)) * u
    return jnp.dot(h.astype(BF16), wd_b[...], preferred_element_type=F32).astype(BF16)


def _gmm_kernel(nb_ref, base_ref, nfill_ref, x_hbm, wg_ref, wu_ref, wd_ref, y_hbm,
                wg_b, wu_b, wd_b, xbuf, ybuf, xsem, ysem):
    e = pl.program_id(0)
    nb = nb_ref[e]
    base = base_ref[e]

    def rows_at(i):
        return pl.ds(pl.multiple_of(base + i * GMM_ROWS, GMM_ROWS), GMM_ROWS)

    def x_copy(i, slot):
        return pltpu.make_async_copy(x_hbm.at[rows_at(i), :], xbuf.at[slot], xsem.at[slot])

    def y_copy(i, slot):
        return pltpu.make_async_copy(ybuf.at[slot], y_hbm.at[rows_at(i), :], ysem.at[slot])

    @pl.when(nb > 0)
    def _():
        x_copy(0, 0).start(priority=ROW_DMA_PRIORITY)
        _cast_bf16(wg_b, wg_ref.at[0])
        _cast_bf16(wu_b, wu_ref.at[0])
        _cast_bf16(wd_b, wd_ref.at[0])

    def block(i, carry):
        slot = i % 2

        @pl.when(i + 1 < nb)
        def _():
            x_copy(i + 1, 1 - slot).start(priority=ROW_DMA_PRIORITY)

        x_copy(i, slot).wait()

        @pl.when(i >= 2)
        def _():
            y_copy(i - 2, slot).wait()

        ybuf[slot] = _expert_ffn(xbuf[slot], wg_b, wu_b, wd_b)
        y_copy(i, slot).start()
        return carry

    lax.fori_loop(0, nb, block, 0)

    @pl.when(nb >= 2)
    def _():
        y_copy(nb - 2, nb % 2).wait()

    @pl.when(nb >= 1)
    def _():
        y_copy(nb - 1, (nb - 1) % 2).wait()

    @pl.when(e == pl.num_programs(0) - 1)
    def _():
        ybuf[0] = jnp.zeros(ybuf.shape[1:], ybuf.dtype)
        nfill = nfill_ref[0]

        def fill_copy(k):
            return y_copy(nb + k, 0)

        lax.fori_loop(0, nfill, lambda k, c: (fill_copy(k).start(), c)[1], 0)
        lax.fori_loop(0, nfill, lambda k, c: (fill_copy(k).wait(), c)[1], 0)


def _gmm(plan, xs, wg, wu, wd):
    R, D = xs.shape
    wspec = lambda shape: pl.BlockSpec((1,) + shape, lambda e, *_: (e, 0, 0))
    grid_spec = pltpu.PrefetchScalarGridSpec(
        num_scalar_prefetch=3,
        grid=(N_EXPERTS,),
        in_specs=[pl.BlockSpec(memory_space=pl.ANY),
                  wspec((D, D_EXPERT)), wspec((D, D_EXPERT)), wspec((D_EXPERT, D))],
        out_specs=pl.BlockSpec(memory_space=pl.ANY),
        scratch_shapes=[pltpu.VMEM((D, D_EXPERT), BF16), pltpu.VMEM((D, D_EXPERT), BF16),
                        pltpu.VMEM((D_EXPERT, D), BF16),
                        pltpu.VMEM((2, GMM_ROWS, D), BF16), pltpu.VMEM((2, GMM_ROWS, D), BF16),
                        pltpu.SemaphoreType.DMA((2,)), pltpu.SemaphoreType.DMA((2,))],
    )
    return pl.pallas_call(
        _gmm_kernel,
        grid_spec=grid_spec,
        out_shape=jax.ShapeDtypeStruct((R, D), BF16),
        compiler_params=pltpu.CompilerParams(
            dimension_semantics=("arbitrary",), vmem_limit_bytes=VMEM_LIMIT),
        name="moe_gmm",
    )(plan["exp_blocks"], plan["exp_base"], plan["tail_blocks"], xs, wg, wu, wd)


def _combine_kernel(bs_ref, bd_ref, bn_ref, ss_ref, sd_ref, sn_ref, cm_ref, x_ref, ys_hbm, g3_ref, b3_ref,
                    o_ref, yt, sems):
    lists = (bs_ref, bd_ref, bn_ref, ss_ref, sd_ref, sn_ref)
    s = pl.program_id(0)
    ns = pl.num_programs(0)
    cur = (s % 2) * PER_STEP
    nxt = PER_STEP - cur

    def fetch(step, first_buf):
        for k in range(PER_STEP):
            _issue_copies(lists, step * PER_STEP + k, yt.at[first_buf + k], False, ys_hbm,
                          sems.at[first_buf + k])

    @pl.when(s == 0)
    def _():
        yt[...] = jnp.zeros_like(yt)
        fetch(0, 0)

    @pl.when(s + 1 < ns)
    def _():
        fetch(s + 1, nxt)

    for k in range(PER_STEP):
        _wait_copies(lists, s * PER_STEP + k, ys_hbm, yt.at[cur + k], sems.at[cur + k])

    col = lax.broadcasted_iota(jnp.int32, (MOE_TILE, SORT_ROWS), 1)
    for k in range(PER_STEP):
        rs = slice(k * MOE_TILE, (k + 1) * MOE_TILE)
        cm = cm_ref[rs, :]
        lp0 = cm[:, 0:1].astype(jnp.int32)
        lp1 = cm[:, 1:2].astype(jnp.int32)
        w = jnp.where(col == lp0, cm[:, 2:3], jnp.where(col == lp1, cm[:, 3:4], 0.0)).astype(BF16)
        y = jnp.dot(w, yt[cur + k], preferred_element_type=F32)
        o_ref[rs, :] = _layer_norm(ALPHA * x_ref[rs, :] + y, g3_ref[...], b3_ref[...])


def _combine(plan, route, x2, ys, g3, b3):
    T, D = x2.shape
    rows = PER_STEP * MOE_TILE
    grid_spec = pltpu.PrefetchScalarGridSpec(
        num_scalar_prefetch=6,
        grid=(T // rows,),
        in_specs=[pl.BlockSpec((rows, LANES), lambda t, *_: (t, 0)),
                  pl.BlockSpec((rows, D), lambda t, *_: (t, 0)),
                  pl.BlockSpec(memory_space=pl.ANY),
                  pl.BlockSpec((1, D), lambda t, *_: (0, 0)),
                  pl.BlockSpec((1, D), lambda t, *_: (0, 0))],
        out_specs=pl.BlockSpec((rows, D), lambda t, *_: (t, 0)),
        scratch_shapes=[pltpu.VMEM((2 * PER_STEP, SORT_ROWS, D), BF16),
                        pltpu.SemaphoreType.DMA((2 * PER_STEP,))],
    )
    return pl.pallas_call(
        _combine_kernel,
        grid_spec=grid_spec,
        out_shape=jax.ShapeDtypeStruct((T, D), F32),
        compiler_params=pltpu.CompilerParams(
            dimension_semantics=("arbitrary",), vmem_limit_bytes=VMEM_LIMIT),
        name="moe_combine",
    )(*_copy_lists(plan), route, x2, ys, g3, b3)


def _moe(x2, x2b, route, lpt, meta, wg, wu, wd, g3, b3):
    plan = _plan(meta, x2.shape[0])
    xs = _dispatch(plan, lpt, x2b)
    ys = _gmm(plan, xs, wg, wu, wd)
    return _combine(plan, route, x2, ys, g3, b3)


def _row(v):
    return v.reshape(1, -1).astype(F32)


def kernel(x, mem, w_in, b_in, w_dw, b_dw, g_conv_norm, b_conv_norm, attn_sinks, w_out, g_ln1, b_ln1,
           w_mq, w_mkv, w_mo, g_ln2, b_ln2, w_group, b_group, w_router, b_router, w_gate, w_up, w_down,
           g_ln3, b_ln3):
    B, S, D = x.shape
    for l in range(DEPTH):
        w_dw_p = jnp.zeros((CONV_HALO, CONV_CH), F32).at[:CONV_WIDTH].set(w_dw[l])
        x1 = _mixer(x, attn_sinks[l].astype(F32), w_in[l], _row(b_in[l]), w_dw_p,
                    _row(b_dw[l]), _row(g_conv_norm[l]), _row(b_conv_norm[l]),
                    w_out[l], _row(g_ln1[l]), _row(b_ln1[l]))

        kvm = _memkv(mem.reshape(B * MEM_LEN, D), w_mkv[l]).reshape(B, MEM_LEN, 2 * D)

        wr = jnp.concatenate(
            [w_group[l], jnp.transpose(w_router[l], (1, 0, 2)).reshape(D, N_EXPERTS)], axis=1)
        wr = jnp.pad(wr, ((0, 0), (0, LANES - wr.shape[1])))
        br = jnp.pad(jnp.concatenate([b_group[l], b_router[l].reshape(-1)]), (0, LANES - N_GROUPS - N_EXPERTS))
        x2, x2b, route, lpt, meta = _memattn(x1, w_mq[l], kvm, w_mo[l],
                                 _row(g_ln2[l]), _row(b_ln2[l]), wr.astype(F32), _row(br))

        T = B * S
        y = _moe(x2.reshape(T, D), x2b.reshape(T, D), route.reshape(T, LANES), lpt, meta,
                 w_gate[l], w_up[l], w_down[l],
                 _row(g_ln3[l]), _row(b_ln3[l]))
        x = y.reshape(B, S, D)
    return x
```

```python
import functools

import jax
import jax.numpy as jnp
from jax import lax
from jax.experimental import pallas as pl
from jax.experimental.pallas import tpu as pltpu

D_MODEL = 1024
MEM_LEN = 256
CONV_CH = 512
CONV_WIDTH = 31
N_HEADS = 8
N_KV_HEADS = 2
HEAD_DIM = 64
GQ = N_HEADS // N_KV_HEADS
ATTN_W = N_HEADS * HEAD_DIM
KV_W = N_KV_HEADS * HEAD_DIM
WINDOW = 128
D_MIX = CONV_CH + ATTN_W
D_IN = 2 * CONV_CH + ATTN_W + 2 * KV_W
MEM_HEADS = 4
MEM_HEAD_DIM = D_MODEL // MEM_HEADS
N_GROUPS = 4
EXPERTS_PER_GROUP = 4
N_EXPERTS = N_GROUPS * EXPERTS_PER_GROUP
D_EXPERT = D_MODEL // 2
DEPTH = 1
ALPHA = (2.0 * DEPTH) ** 0.25
LN_EPS = 1e-5

LANES = 128
SUBLANES = 8
CONV_ROWS = 128
IN_GROUP = 256
LN_ROWS = 64
MEM_TILE = 1024
MASK_VALUE = -1e30
CONV_HALO = 32
SEQ_TILE = 512
MOE_TILE = 512
CHUNK = 16
GMM_ROWS = 512
TOP_K = 2
SORT_ROWS = -(-(MOE_TILE * TOP_K + N_EXPERTS * (CHUNK - 1)) // 256) * 256
BIG = 4
MAX_BIG = SORT_ROWS // (BIG * CHUNK)
MAX_SMALL = N_EXPERTS * (BIG - 1)
PER_STEP = 2
WAIT_GROUP = 8
ROUTE_OFF = N_GROUPS
VMEM_LIMIT = 56 * 1024 * 1024

BF16 = jnp.bfloat16
F32 = jnp.float32


def _layer_norm(x, g, b):
    mu = jnp.mean(x, axis=-1, keepdims=True)
    xc = x - mu
    var = jnp.mean(xc * xc, axis=-1, keepdims=True)
    return xc * lax.rsqrt(var + LN_EPS) * g + b


def _cast_bf16(dst_ref, src_ref):
    rows = 256
    for r0 in range(0, src_ref.shape[0], rows):
        dst_ref[r0:r0 + rows, :] = src_ref[r0:r0 + rows, :].astype(BF16)


def _dot_nt(a, b):
    return lax.dot_general(a, b, (((1,), (1,)), ((), ())), preferred_element_type=F32)


def _mixer_kernel(sinks_ref, x_ref, xp_ref, w_in_ref, b_in_ref, w_dw_ref, b_dw_ref, g_cn_ref, b_cn_ref,
                  w_out_ref, g1_ref, b1_ref, o_ref, w_in_b, w_out_b, xbuf, ubuf, hbuf, hshift, cbuf, qbuf, kbuf,
                  vbuf, ymix, *, tiles_per_seq):
    t = pl.program_id(0)
    ts = SEQ_TILE
    sa = t % 2
    sb = 1 - sa
    n_tiles = pl.num_programs(0) - 1
    first_a = jnp.minimum(t, n_tiles - 1) % tiles_per_seq == 0
    first_b = jnp.maximum(t - 1, 0) % tiles_per_seq == 0

    @pl.when(t == 0)
    def _():
        _cast_bf16(w_in_b, w_in_ref)
        _cast_bf16(w_out_b, w_out_ref)
        hbuf[...] = jnp.zeros_like(hbuf)
        qbuf[...] = jnp.zeros_like(qbuf)
        kbuf[...] = jnp.zeros_like(kbuf)
        vbuf[...] = jnp.zeros_like(vbuf)

    xbuf[...] = x_ref[0].astype(BF16)

    def in_proj_group(g):
        cs = slice(g * IN_GROUP, (g + 1) * IN_GROUP)
        ubuf[:, cs] = jnp.dot(xbuf[...], w_in_b[:, cs], preferred_element_type=F32) + b_in_ref[:, cs]

    hb = hbuf.at[sb]
    base = CONV_HALO - (CONV_WIDTH - 1)
    n_shift = ts + CONV_HALO - SUBLANES
    for b in range(1, SUBLANES):
        hshift[b - 1, 0:n_shift, :] = hb[b:b + n_shift, :]
    rc = CONV_ROWS

    def conv_chunk(c):
        r0 = c * rc
        for l in range(CONV_CH // LANES):
            ls = slice(l * LANES, (l + 1) * LANES)
            acc = jnp.zeros((rc, LANES), F32)
            for j in range(CONV_WIDTH):
                a8, b = divmod(j + base, SUBLANES)
                rs = slice(r0 + SUBLANES * a8, r0 + SUBLANES * a8 + rc)
                tap = hb[rs, ls] if b == 0 else hshift[b - 1, rs, ls]
                acc = acc + tap * w_dw_ref[j:j + 1, ls]
            cbuf[r0:r0 + rc, ls] = acc
        for r1 in range(r0, r0 + rc, LN_ROWS):
            rs = slice(r1, r1 + LN_ROWS)
            y = _layer_norm(cbuf[rs, :] + b_dw_ref[...], g_cn_ref[...], b_cn_ref[...])
            y = y * jax.nn.sigmoid(y)
            ymix[rs, 0:CONV_CH] = y.astype(BF16)

    qi = lax.broadcasted_iota(jnp.int32, (2 * WINDOW, 2 * WINDOW), 0) % WINDOW
    kj = lax.broadcasted_iota(jnp.int32, (2 * WINDOW, 2 * WINDOW), 1)
    dist = qi + WINDOW - kj
    band = (dist >= 0) & (dist < WINDOW)
    top = lax.broadcasted_iota(jnp.int32, (2 * WINDOW, 1), 0) < WINDOW
    lo_out = lax.broadcasted_iota(jnp.int32, (WINDOW, 2 * HEAD_DIM), 1) < HEAD_DIM

    def attn_block(jb):
        r0 = jb * WINDOW
        valid = band & jnp.logical_or(jnp.logical_not(first_b), kj >= WINDOW) if jb == 0 else band
        for kvh in range(N_KV_HEADS):
            h0 = kvh * GQ
            c0 = h0 * HEAD_DIM
            qs = jnp.concatenate([qbuf[sb, r0:r0 + WINDOW, c0:c0 + 2 * HEAD_DIM],
                                  qbuf[sb, r0:r0 + WINDOW, c0 + 2 * HEAD_DIM:c0 + 4 * HEAD_DIM]], axis=0)
            pv = []
            for par in range(2):
                kk = kbuf[sb, 2 * kvh + par, r0:r0 + 2 * WINDOW, :]
                vv = vbuf[sb, (kvh + par) % 2, r0:r0 + 2 * WINDOW, :]
                s = jnp.where(valid, _dot_nt(qs, kk), MASK_VALUE)
                sink = jnp.where(top, sinks_ref[h0 + par], sinks_ref[h0 + 2 + par])
                m = jnp.maximum(jnp.max(s, axis=-1, keepdims=True), sink)
                p = jnp.exp(s - m)
                denom = jnp.sum(p, axis=-1, keepdims=True) + jnp.exp(sink - m)
                pv.append(jnp.dot(p.astype(BF16), vv, preferred_element_type=F32) / denom)
            for pair in range(2):
                rs = slice(pair * WINDOW, (pair + 1) * WINDOW)
                o = jnp.where(lo_out, pv[0][rs], pv[1][rs])
                cs = CONV_CH + c0 + pair * 2 * HEAD_DIM
                ymix[r0:r0 + WINDOW, cs:cs + 2 * HEAD_DIM] = o.astype(BF16)

    assert ts // rc == ts // WINDOW
    n_groups = D_IN // IN_GROUP
    per_chunk = -(-n_groups // (ts // rc))
    for c in range(ts // rc):
        for g in range(c * per_chunk, min((c + 1) * per_chunk, n_groups)):
            in_proj_group(g)
        attn_block(c)
        conv_chunk(c)

    mix = jnp.dot(ymix[...], w_out_b[...], preferred_element_type=F32)
    o_ref[0] = _layer_norm(ALPHA * xp_ref[0] + mix, g1_ref[...], b1_ref[...])

    u = ubuf
    a = u[:, 0:CONV_CH]
    gate = u[:, CONV_CH:2 * CONV_CH]
    hbuf[sa, 0:CONV_HALO, :] = jnp.where(first_a, 0.0, hbuf[sb, ts:ts + CONV_HALO, :])
    hbuf[sa, CONV_HALO:CONV_HALO + ts, :] = a * jax.nn.sigmoid(gate)
    qbuf[sa] = (u[:, 2 * CONV_CH:2 * CONV_CH + ATTN_W] * (HEAD_DIM ** -0.5)).astype(BF16)
    kf = u[:, 2 * CONV_CH + ATTN_W:2 * CONV_CH + ATTN_W + KV_W]
    vf = u[:, 2 * CONV_CH + ATTN_W + KV_W:D_IN]
    kr = pltpu.roll(kf, HEAD_DIM, axis=1)
    vr = pltpu.roll(vf, HEAD_DIM, axis=1)
    lo = lax.broadcasted_iota(jnp.int32, (ts, KV_W), 1) < HEAD_DIM
    rows = slice(WINDOW, WINDOW + ts)
    zero_kv = jnp.zeros((), BF16)
    kbuf[sa, :, 0:WINDOW, :] = jnp.where(first_a, zero_kv, kbuf[sb, :, ts:ts + WINDOW, :])
    vbuf[sa, :, 0:WINDOW, :] = jnp.where(first_a, zero_kv, vbuf[sb, :, ts:ts + WINDOW, :])
    kbuf[sa, 0, rows, :] = jnp.where(lo, kf, 0.0).astype(BF16)
    kbuf[sa, 1, rows, :] = jnp.where(lo, 0.0, kr).astype(BF16)
    kbuf[sa, 2, rows, :] = jnp.where(lo, kr, 0.0).astype(BF16)
    kbuf[sa, 3, rows, :] = jnp.where(lo, 0.0, kf).astype(BF16)
    vbuf[sa, 0, rows, :] = vf.astype(BF16)
    vbuf[sa, 1, rows, :] = vr.astype(BF16)


def _const_spec(shape):
    nd = len(shape)
    return pl.BlockSpec(shape, lambda *_: (0,) * nd)


def _resident_spec(shape):
    nd = len(shape)
    return pl.BlockSpec(shape, lambda *_: (0,) * nd, pipeline_mode=pl.Buffered(1))


def _mixer(x, sinks, w_in, b_in, w_dw, b_dw, g_cn, b_cn, w_out, g1, b1):
    B, S, D = x.shape
    ts = SEQ_TILE
    nt = S // ts
    n_tiles = B * nt

    def tile_of(k):
        return (k // nt, k % nt, 0)

    cur = pl.BlockSpec((1, ts, D), lambda t: tile_of(jnp.minimum(t, n_tiles - 1)))
    prev = pl.BlockSpec((1, ts, D), lambda t: tile_of(jnp.maximum(t - 1, 0)))
    return pl.pallas_call(
        functools.partial(_mixer_kernel, tiles_per_seq=nt),
        grid=(n_tiles + 1,),
        in_specs=[
            pl.BlockSpec(memory_space=pltpu.SMEM),
            cur, prev,
            _resident_spec((D, D_IN)), _const_spec((1, D_IN)),
            _const_spec((CONV_HALO, CONV_CH)), _const_spec((1, CONV_CH)),
            _const_spec((1, CONV_CH)), _const_spec((1, CONV_CH)),
            _resident_spec((D_MIX, D)), _const_spec((1, D)), _const_spec((1, D)),
        ],
        out_specs=prev,
        out_shape=jax.ShapeDtypeStruct((B, S, D), F32),
        scratch_shapes=[
            pltpu.VMEM((D, D_IN), BF16),
            pltpu.VMEM((D_MIX, D), BF16),
            pltpu.VMEM((ts, D), BF16),
            pltpu.VMEM((ts, D_IN), F32),
            pltpu.VMEM((2, CONV_HALO + ts, CONV_CH), F32),
            pltpu.VMEM((SUBLANES - 1, CONV_HALO + ts, CONV_CH), F32),
            pltpu.VMEM((ts, CONV_CH), F32),
            pltpu.VMEM((2, ts, ATTN_W), BF16),
            pltpu.VMEM((2, 2 * N_KV_HEADS, WINDOW + ts, KV_W), BF16),
            pltpu.VMEM((2, 2, WINDOW + ts, KV_W), BF16),
            pltpu.VMEM((ts, D_MIX), BF16),
        ],
        compiler_params=pltpu.CompilerParams(
            dimension_semantics=("arbitrary",), vmem_limit_bytes=VMEM_LIMIT),
        name="mixer",
    )(sinks, x, x, w_in, b_in, w_dw, b_dw, g_cn, b_cn, w_out, g1, b1)


def _memkv_kernel(mem_ref, w_ref, o_ref):
    o_ref[...] = jnp.dot(mem_ref[...].astype(BF16), w_ref[...].astype(BF16),
                         preferred_element_type=F32).astype(BF16)


def _memkv(mem2d, w_mkv):
    M, D = mem2d.shape
    N = w_mkv.shape[1]
    tn = 512
    return pl.pallas_call(
        _memkv_kernel,
        grid=(N // tn,),
        in_specs=[pl.BlockSpec((M, D), lambda j: (0, 0)), pl.BlockSpec((D, tn), lambda j: (0, j))],
        out_specs=pl.BlockSpec((M, tn), lambda j: (0, j)),
        out_shape=jax.ShapeDtypeStruct((M, N), BF16),
        compiler_params=pltpu.CompilerParams(dimension_semantics=("arbitrary",)),
        name="memkv",
    )(mem2d, w_mkv)


def _first_max(rows):
    best = rows[0]
    for r in rows[1:]:
        best = jnp.maximum(best, r)
    idx = jnp.full(best.shape, len(rows) - 1, jnp.int32)
    for k in range(len(rows) - 2, -1, -1):
        idx = jnp.where(rows[k] == best, k, idx)
    return best, idx


def _route_plan(logits_t):
    tile = logits_t.shape[1]
    row = lambda k: logits_t[k:k + 1, :]
    gmax, g_idx = _first_max([row(g) for g in range(N_GROUPS)])
    gsum = jnp.exp(row(0) - gmax)
    for g in range(1, N_GROUPS):
        gsum = gsum + jnp.exp(row(g) - gmax)
    g_p = 1.0 / gsum
    rl = []
    for e in range(EXPERTS_PER_GROUP):
        v = row(ROUTE_OFF + (N_GROUPS - 1) * EXPERTS_PER_GROUP + e)
        for g in range(N_GROUPS - 2, -1, -1):
            v = jnp.where(g_idx == g, row(ROUTE_OFF + g * EXPERTS_PER_GROUP + e), v)
        rl.append(v)
    m1, i1 = _first_max(rl)
    m2, i2 = _first_max([jnp.where(i1 == e, MASK_VALUE, rl[e]) for e in range(EXPERTS_PER_GROUP)])
    ex = jnp.exp(m2 - m1)
    w1 = 1.0 / (1.0 + ex)
    w2 = ex * w1
    e1 = g_idx * EXPERTS_PER_GROUP + i1
    e2 = g_idx * EXPERTS_PER_GROUP + i2

    eid = lax.broadcasted_iota(jnp.int32, (N_EXPERTS, tile), 0)
    hit1 = eid == e1
    hit2 = eid == e2
    oh = jnp.where(jnp.logical_or(hit1, hit2), 1.0, 0.0)
    r = lax.broadcasted_iota(jnp.int32, (tile, tile), 0)
    c = lax.broadcasted_iota(jnp.int32, (tile, tile), 1)
    tri = jnp.where(r <= c, 1.0, 0.0).astype(BF16)
    csum = jnp.dot(oh.astype(BF16), tri, preferred_element_type=F32)
    counts = jnp.broadcast_to(csum[:, tile - 1:tile], (N_EXPERTS, tile)).astype(jnp.int32)
    nch = jnp.right_shift(counts + (CHUNK - 1), CHUNK.bit_length() - 1)
    er = lax.broadcasted_iota(jnp.int32, (N_EXPERTS, N_EXPERTS), 0)
    ec = lax.broadcasted_iota(jnp.int32, (N_EXPERTS, N_EXPERTS), 1)
    lower = jnp.where(ec < er, 1.0, 0.0).astype(BF16)
    off = jnp.dot(lower, nch.astype(F32).astype(BF16), preferred_element_type=F32) * CHUNK
    pos = off + csum - oh
    lp1 = jnp.sum(jnp.where(hit1, pos, 0.0), axis=0, keepdims=True)
    lp2 = jnp.sum(jnp.where(hit2, pos, 0.0), axis=0, keepdims=True)
    zero = jnp.zeros_like(lp1)
    route_t = jnp.concatenate([lp1, lp2, g_p * w1, g_p * w2, zero, zero, zero, zero], axis=0)
    meta = jnp.concatenate([nch[:, 0:LANES], off[:, 0:LANES].astype(jnp.int32)], axis=0)
    return route_t, meta


def _memattn_kernel(x_ref, wq_ref, k_ref, v_ref, wo_ref, g2_ref, b2_ref, wr2_ref, wrh_ref, br_ref,
                    o_ref, ob_ref, route_ref, routet_ref, meta_ref, wq_b, wo_b):
    @pl.when(jnp.logical_and(pl.program_id(0) == 0, pl.program_id(1) == 0))
    def _():
        _cast_bf16(wq_b, wq_ref)
        _cast_bf16(wo_b, wo_ref)

    def rows_logits(rs):
        x = x_ref[0, rs, :]
        q = jnp.dot(x.astype(BF16), wq_b[...], preferred_element_type=F32)
        q = (q * (MEM_HEAD_DIM ** -0.5)).astype(BF16)
        outs = []
        for h in range(MEM_HEADS):
            sl = slice(h * MEM_HEAD_DIM, (h + 1) * MEM_HEAD_DIM)
            s = _dot_nt(q[:, sl], k_ref[0, :, sl])
            m = jnp.max(s, axis=-1, keepdims=True)
            p = jnp.exp(s - m)
            denom = jnp.sum(p, axis=-1, keepdims=True)
            o = jnp.dot(p.astype(BF16), v_ref[0, :, sl], preferred_element_type=F32)
            outs.append((o / denom).astype(BF16))
        o = jnp.dot(jnp.concatenate(outs, axis=-1), wo_b[...], preferred_element_type=F32)
        x2 = _layer_norm(ALPHA * x + o, g2_ref[...], b2_ref[...])
        o_ref[0, rs, :] = x2
        x2h = x2.astype(BF16)
        ob_ref[0, rs, :] = x2h
        x2l = (x2 - x2h.astype(F32)).astype(BF16)
        hh = jnp.dot(x2h, wr2_ref[...], preferred_element_type=F32)
        return (hh[:, 0:LANES] + hh[:, LANES:2 * LANES]
                + jnp.dot(x2l, wrh_ref[...], preferred_element_type=F32) + br_ref[...])

    ts = x_ref.shape[1]
    groups = [slice(r0, r0 + MOE_TILE) for r0 in range(0, ts, MOE_TILE)]
    logits = [rows_logits(rs) for rs in groups]
    pad = jnp.zeros((LANES - SUBLANES, MOE_TILE), F32)
    for k, rs in enumerate(groups):
        route_t, meta = _route_plan(jnp.transpose(logits[k]))
        routet_ref[k] = route_t
        route_ref[0, rs, :] = jnp.transpose(jnp.concatenate([route_t, pad], axis=0))
        meta_ref[k] = meta


def _memattn(x1, wq, kvm, wo, g2, b2, wr, br):
    B, S, D = x1.shape
    ts = MEM_TILE
    per = ts // MOE_TILE
    nt = S // ts
    c = wr * (2.0 ** 16 + 1.0)
    w_high = c - (c - wr)
    wrh = w_high.astype(BF16)
    wr2 = jnp.concatenate([wrh, (wr - w_high).astype(BF16)], axis=1)
    tile = pl.BlockSpec((1, ts, D), lambda b, i: (b, i, 0))
    kspec = pl.BlockSpec((1, MEM_LEN, D), lambda b, i: (b, 0, 0))
    vspec = pl.BlockSpec((1, MEM_LEN, D), lambda b, i: (b, 0, 1))
    return pl.pallas_call(
        _memattn_kernel,
        grid=(B, nt),
        in_specs=[tile, _resident_spec((D, D)), kspec, vspec, _resident_spec((D, D)),
                  _const_spec((1, D)), _const_spec((1, D)),
                  _const_spec((D, 2 * LANES)), _const_spec((D, LANES)), _const_spec((1, LANES))],
        out_specs=[tile, tile, pl.BlockSpec((1, ts, LANES), lambda b, i: (b, i, 0)),
                   pl.BlockSpec((per, SUBLANES, MOE_TILE), lambda b, i: (b * nt + i, 0, 0)),
                   pl.BlockSpec((per, 2 * N_EXPERTS, LANES), lambda b, i: (b * nt + i, 0, 0))],
        out_shape=[jax.ShapeDtypeStruct((B, S, D), F32),
                   jax.ShapeDtypeStruct((B, S, D), BF16),
                   jax.ShapeDtypeStruct((B, S, LANES), F32),
                   jax.ShapeDtypeStruct((B * nt * per, SUBLANES, MOE_TILE), F32),
                   jax.ShapeDtypeStruct((B * nt * per, 2 * N_EXPERTS, LANES), jnp.int32)],
        scratch_shapes=[pltpu.VMEM((D, D), BF16), pltpu.VMEM((D, D), BF16)],
        compiler_params=pltpu.CompilerParams(
            dimension_semantics=("arbitrary", "arbitrary"), vmem_limit_bytes=VMEM_LIMIT),
        name="memattn",
    )(x1, wq, kvm, kvm, wo, g2, b2, wr2, wrh, br)


def _gmm_blocks(n_tokens):
    rows = (n_tokens * TOP_K + (n_tokens // MOE_TILE) * N_EXPERTS * (CHUNK - 1)
            + N_EXPERTS * (GMM_ROWS - CHUNK))
    return -(-rows // GMM_ROWS)


def _plan(meta, T):
    nch = meta[:, :N_EXPERTS, 0]
    n16 = nch * CHUNK
    n_e = jnp.sum(n16, axis=0)
    reg = (n_e + GMM_ROWS - 1) // GMM_ROWS * GMM_ROWS
    gend = jnp.cumsum(reg)
    gbase = gend - reg
    dst = gbase[None, :] + jnp.cumsum(n16, axis=0) - n16
    blk_row = jnp.arange(_gmm_blocks(T), dtype=jnp.int32)[:, None] * GMM_ROWS
    blk_expert = jnp.sum(blk_row >= gend[None, :], axis=1)
    blk_expert = jnp.minimum(blk_expert, N_EXPERTS - 1).astype(jnp.int32)
    n_used = (gend[-1] // GMM_ROWS).astype(jnp.int32).reshape(1)
    data_end = jnp.sum(jnp.where(blk_expert[:, None] == jnp.arange(N_EXPERTS)[None, :],
                                 (gbase + n_e)[None, :], 0), axis=1)
    blk_rows = jnp.where(blk_row[:, 0] < gend[-1], jnp.clip(data_end - blk_row[:, 0], 0, GMM_ROWS), 0)
    src = (jnp.cumsum(nch, axis=1) - nch) * CHUNK
    n_big = nch // BIG

    def copy_list(count, src0, dst0, rows, length):
        cum = jnp.cumsum(count, axis=1)
        first = (cum - count)[:, None, :]
        k = jnp.arange(length, dtype=jnp.int32)[None, :, None]
        mine = (k >= first) & (k < cum[:, None, :])
        step = (k - first) * rows
        pick = lambda base: jnp.sum(jnp.where(mine, base[:, None, :] + step, 0), axis=2)
        return pick(src0), pick(dst0), cum[:, -1]

    big_src, big_dst, big_n = copy_list(n_big, src, dst, BIG * CHUNK, MAX_BIG)
    rest = n_big * (BIG * CHUNK)
    small_src, small_dst, small_n = copy_list(nch - n_big * BIG, src + rest, dst + rest, CHUNK, MAX_SMALL)
    i32 = lambda a: a.astype(jnp.int32)
    fill_start = jnp.concatenate([gbase + n_e, gend[-1:]])
    fill_n = jnp.concatenate([reg - n_e, _gmm_blocks(T) * GMM_ROWS - gend[-1:]]) // CHUNK
    return dict(big_src=i32(big_src).reshape(-1), big_dst=i32(big_dst).reshape(-1), big_n=i32(big_n),
                small_src=i32(small_src).reshape(-1), small_dst=i32(small_dst).reshape(-1), small_n=i32(small_n),
                fill_start=i32(fill_start), fill_n=i32(fill_n),
                fill_tot=i32(jnp.sum(fill_n)).reshape(1), blk_expert=blk_expert, n_used=n_used,
                blk_rows=i32(blk_rows))


def _rows_copy(src_ref, src_row, dst_ref, dst_row, rows, sem):
    return pltpu.make_async_copy(
        src_ref.at[pl.ds(pl.multiple_of(src_row, CHUNK), rows), :],
        dst_ref.at[pl.ds(pl.multiple_of(dst_row, CHUNK), rows), :], sem)


def _chunk_copy(src_ref, src_row, dst_ref, dst_row, sem):
    return _rows_copy(src_ref, src_row, dst_ref, dst_row, CHUNK, sem)


def _issue_copies(lists, tile, tile_ref, tile_is_src, hbm_ref, sem):
    big_src, big_dst, big_n, small_src, small_dst, small_n = lists
    for src_l, dst_l, n_l, length, rows in ((big_src, big_dst, big_n, MAX_BIG, BIG * CHUNK),
                                            (small_src, small_dst, small_n, MAX_SMALL, CHUNK)):
        def issue(k, carry, src_l=src_l, dst_l=dst_l, length=length, rows=rows):
            local, remote = src_l[tile * length + k], dst_l[tile * length + k]
            if tile_is_src:
                _rows_copy(tile_ref, local, hbm_ref, remote, rows, sem).start()
            else:
                _rows_copy(hbm_ref, remote, tile_ref, local, rows, sem).start()
            return carry

        lax.fori_loop(0, n_l[tile], issue, 0)


def _wait_rows(n, rows, src_ref, dst_ref, sem):
    def body(c, carry):
        pltpu.make_async_copy(src_ref.at[pl.ds(0, rows), :], dst_ref.at[pl.ds(0, rows), :], sem).wait()
        return carry

    lax.fori_loop(0, n, body, 0)


def _wait_copies(lists, tile, src_ref, dst_ref, sem, extra_chunks=0):
    _wait_rows(lists[2][tile], BIG * CHUNK, src_ref, dst_ref, sem)
    n = lists[5][tile] + extra_chunks
    _wait_rows(n // WAIT_GROUP, WAIT_GROUP * CHUNK, src_ref, dst_ref, sem)
    _wait_rows(n % WAIT_GROUP, CHUNK, src_ref, dst_ref, sem)


def _dispatch_kernel(bs_ref, bd_ref, bn_ref, ss_ref, sd_ref, sn_ref, fstart_ref, fn_ref, ftot_ref,
                     lp_ref, x_ref, xs_hbm, xt, zbuf, sems):
    lists = (bs_ref, bd_ref, bn_ref, ss_ref, sd_ref, sn_ref)
    s = pl.program_id(0)
    ns = pl.num_programs(0)
    cur = (s % 2) * PER_STEP
    prv = PER_STEP - cur

    @pl.when(s >= 2)
    def _():
        for k in range(PER_STEP):
            _wait_copies(lists, (s - 2) * PER_STEP + k, xt.at[cur + k], xs_hbm, sems.at[cur + k])

    r = lax.broadcasted_iota(jnp.int32, (SORT_ROWS, MOE_TILE), 0)
    for k in range(PER_STEP):
        lp = lp_ref[k].astype(jnp.int32)
        hit = jnp.logical_or(lp[0:1, :] == r, lp[1:2, :] == r)
        p = jnp.where(hit, 1.0, 0.0).astype(BF16)
        x = x_ref[k * MOE_TILE:(k + 1) * MOE_TILE, :]
        xt[cur + k] = jnp.dot(p, x, preferred_element_type=F32).astype(BF16)

    for k in range(PER_STEP):
        _issue_copies(lists, s * PER_STEP + k, xt.at[cur + k], True, xs_hbm, sems.at[cur + k])

    @pl.when(s == ns - 1)
    def _():
        zbuf[...] = jnp.zeros_like(zbuf)
        sem = sems.at[cur]

        def per_range(e, carry):
            def issue(c, carry2):
                _chunk_copy(zbuf, 0, xs_hbm, fstart_ref[e] + c * CHUNK, sem).start()
                return carry2

            return lax.fori_loop(0, fn_ref[e], issue, carry)

        lax.fori_loop(0, N_EXPERTS + 1, per_range, 0)
        for k in range(PER_STEP):
            _wait_copies(lists, (s - 1) * PER_STEP + k, xt.at[prv + k], xs_hbm, sems.at[prv + k])
            _wait_copies(lists, s * PER_STEP + k, xt.at[cur + k], xs_hbm, sems.at[cur + k],
                         extra_chunks=ftot_ref[0] if k == 0 else 0)


def _copy_lists(plan):
    return tuple(plan[k] for k in ("big_src", "big_dst", "big_n", "small_src", "small_dst", "small_n"))


def _dispatch(plan, lpt, x2b):
    T, D = x2b.shape
    rows = PER_STEP * MOE_TILE
    assert T // rows >= 2
    grid_spec = pltpu.PrefetchScalarGridSpec(
        num_scalar_prefetch=9,
        grid=(T // rows,),
        in_specs=[pl.BlockSpec((PER_STEP, SUBLANES, MOE_TILE), lambda t, *_: (t, 0, 0)),
                  pl.BlockSpec((rows, D), lambda t, *_: (t, 0))],
        out_specs=pl.BlockSpec(memory_space=pl.ANY),
        scratch_shapes=[pltpu.VMEM((2 * PER_STEP, SORT_ROWS, D), BF16), pltpu.VMEM((CHUNK, D), BF16),
                        pltpu.SemaphoreType.DMA((2 * PER_STEP,))],
    )
    return pl.pallas_call(
        _dispatch_kernel,
        grid_spec=grid_spec,
        out_shape=jax.ShapeDtypeStruct((_gmm_blocks(T) * GMM_ROWS, D), BF16),
        compiler_params=pltpu.CompilerParams(
            dimension_semantics=("arbitrary",), vmem_limit_bytes=VMEM_LIMIT),
        name="moe_dispatch",
    )(*_copy_lists(plan), plan["fill_start"], plan["fill_n"], plan["fill_tot"], lpt, x2b)


def _expert_ffn(xb, wg_b, wu_b, wd_b):
    g = jnp.dot(xb, wg_b[...], preferred_element_type=F32)
    u = jnp.dot(xb, wu_b[...], preferred_element_type=F32)
    h = (g * jax.nn.sigmoid(g)) * u
    return jnp.dot(h.astype(BF16), wd_b[...], preferred_element_type=F32).astype(BF16)


def _gmm_kernel(be_ref, nu_ref, br_ref, x_ref, wg_ref, wu_ref, wd_ref, o_ref, wg_b, wu_b, wd_b):
    b = pl.program_id(0)
    rows = br_ref[b]
    half = GMM_ROWS // 2

    @pl.when(jnp.logical_or(b == 0, be_ref[b] != be_ref[jnp.maximum(b - 1, 0)]))
    def _():
        _cast_bf16(wg_b, wg_ref.at[0])
        _cast_bf16(wu_b, wu_ref.at[0])
        _cast_bf16(wd_b, wd_ref.at[0])

    @pl.when(rows > half)
    def _():
        o_ref[...] = _expert_ffn(x_ref[...], wg_b, wu_b, wd_b)

    @pl.when(jnp.logical_and(rows > 0, rows <= half))
    def _():
        o_ref[0:half, :] = _expert_ffn(x_ref[0:half, :], wg_b, wu_b, wd_b)
        o_ref[half:GMM_ROWS, :] = jnp.zeros((GMM_ROWS - half, o_ref.shape[1]), o_ref.dtype)

    @pl.when(rows == 0)
    def _():
        o_ref[...] = jnp.zeros_like(o_ref)


def _gmm(plan, xs, wg, wu, wd):
    R, D = xs.shape
    rows = pl.BlockSpec((GMM_ROWS, D), lambda b, be, nu, br: (jnp.minimum(b, nu[0] - 1), 0))
    grid_spec = pltpu.PrefetchScalarGridSpec(
        num_scalar_prefetch=3,
        grid=(R // GMM_ROWS,),
        in_specs=[rows,
                  pl.BlockSpec((1, D, D_EXPERT), lambda b, be, nu, br: (be[b], 0, 0)),
                  pl.BlockSpec((1, D, D_EXPERT), lambda b, be, nu, br: (be[b], 0, 0)),
                  pl.BlockSpec((1, D_EXPERT, D), lambda b, be, nu, br: (be[b], 0, 0))],
        out_specs=pl.BlockSpec((GMM_ROWS, D), lambda b, be, nu, br: (b, 0)),
        scratch_shapes=[pltpu.VMEM((D, D_EXPERT), BF16), pltpu.VMEM((D, D_EXPERT), BF16),
                        pltpu.VMEM((D_EXPERT, D), BF16)],
    )
    return pl.pallas_call(
        _gmm_kernel,
        grid_spec=grid_spec,
        out_shape=jax.ShapeDtypeStruct((R, D), BF16),
        compiler_params=pltpu.CompilerParams(
            dimension_semantics=("arbitrary",), vmem_limit_bytes=VMEM_LIMIT),
        name="moe_gmm",
    )(plan["blk_expert"], plan["n_used"], plan["blk_rows"], xs, wg, wu, wd)


def _combine_kernel(bs_ref, bd_ref, bn_ref, ss_ref, sd_ref, sn_ref, cm_ref, x_ref, ys_hbm, g3_ref, b3_ref,
                    o_ref, yt, sems):
    lists = (bs_ref, bd_ref, bn_ref, ss_ref, sd_ref, sn_ref)
    s = pl.program_id(0)
    ns = pl.num_programs(0)
    cur = (s % 2) * PER_STEP
    nxt = PER_STEP - cur

    def fetch(step, first_buf):
        for k in range(PER_STEP):
            _issue_copies(lists, step * PER_STEP + k, yt.at[first_buf + k], False, ys_hbm,
                          sems.at[first_buf + k])

    @pl.when(s == 0)
    def _():
        yt[...] = jnp.zeros_like(yt)
        fetch(0, 0)

    @pl.when(s + 1 < ns)
    def _():
        fetch(s + 1, nxt)

    for k in range(PER_STEP):
        _wait_copies(lists, s * PER_STEP + k, ys_hbm, yt.at[cur + k], sems.at[cur + k])

    col = lax.broadcasted_iota(jnp.int32, (MOE_TILE, SORT_ROWS), 1)
    for k in range(PER_STEP):
        rs = slice(k * MOE_TILE, (k + 1) * MOE_TILE)
        cm = cm_ref[rs, :]
        lp0 = cm[:, 0:1].astype(jnp.int32)
        lp1 = cm[:, 1:2].astype(jnp.int32)
        w = jnp.where(col == lp0, cm[:, 2:3], jnp.where(col == lp1, cm[:, 3:4], 0.0)).astype(BF16)
        y = jnp.dot(w, yt[cur + k], preferred_element_type=F32)
        o_ref[rs, :] = _layer_norm(ALPHA * x_ref[rs, :] + y, g3_ref[...], b3_ref[...])


def _combine(plan, route, x2, ys, g3, b3):
    T, D = x2.shape
    rows = PER_STEP * MOE_TILE
    grid_spec = pltpu.PrefetchScalarGridSpec(
        num_scalar_prefetch=6,
        grid=(T // rows,),
        in_specs=[pl.BlockSpec((rows, LANES), lambda t, *_: (t, 0)),
                  pl.BlockSpec((rows, D), lambda t, *_: (t, 0)),
                  pl.BlockSpec(memory_space=pl.ANY),
                  pl.BlockSpec((1, D), lambda t, *_: (0, 0)),
                  pl.BlockSpec((1, D), lambda t, *_: (0, 0))],
        out_specs=pl.BlockSpec((rows, D), lambda t, *_: (t, 0)),
        scratch_shapes=[pltpu.VMEM((2 * PER_STEP, SORT_ROWS, D), BF16),
                        pltpu.SemaphoreType.DMA((2 * PER_STEP,))],
    )
    return pl.pallas_call(
        _combine_kernel,
        grid_spec=grid_spec,
        out_shape=jax.ShapeDtypeStruct((T, D), F32),
        compiler_params=pltpu.CompilerParams(
            dimension_semantics=("arbitrary",), vmem_limit_bytes=VMEM_LIMIT),
        name="moe_combine",
    )(*_copy_lists(plan), route, x2, ys, g3, b3)


def _moe(x2, x2b, route, lpt, meta, wg, wu, wd, g3, b3):
    plan = _plan(meta, x2.shape[0])
    xs = _dispatch(plan, lpt, x2b)
    ys = _gmm(plan, xs, wg, wu, wd)
    return _combine(plan, route, x2, ys, g3, b3)


def _row(v):
    return v.reshape(1, -1).astype(F32)


def kernel(x, mem, w_in, b_in, w_dw, b_dw, g_conv_norm, b_conv_norm, attn_sinks, w_out, g_ln1, b_ln1,
           w_mq, w_mkv, w_mo, g_ln2, b_ln2, w_group, b_group, w_router, b_router, w_gate, w_up, w_down,
           g_ln3, b_ln3):
    B, S, D = x.shape
    for l in range(DEPTH):
        w_dw_p = jnp.zeros((CONV_HALO, CONV_CH), F32).at[:CONV_WIDTH].set(w_dw[l])
        x1 = _mixer(x, attn_sinks[l].astype(F32), w_in[l], _row(b_in[l]), w_dw_p,
                    _row(b_dw[l]), _row(g_conv_norm[l]), _row(b_conv_norm[l]),
                    w_out[l], _row(g_ln1[l]), _row(b_ln1[l]))

        kvm = _memkv(mem.reshape(B * MEM_LEN, D), w_mkv[l]).reshape(B, MEM_LEN, 2 * D)

        wr = jnp.concatenate(
            [w_group[l], jnp.transpose(w_router[l], (1, 0, 2)).reshape(D, N_EXPERTS)], axis=1)
        wr = jnp.pad(wr, ((0, 0), (0, LANES - wr.shape[1])))
        br = jnp.pad(jnp.concatenate([b_group[l], b_router[l].reshape(-1)]), (0, LANES - N_GROUPS - N_EXPERTS))
        x2, x2b, route, lpt, meta = _memattn(x1, w_mq[l], kvm, w_mo[l],
                                 _row(g_ln2[l]), _row(b_ln2[l]), wr.astype(F32), _row(br))

        T = B * S
        y = _moe(x2.reshape(T, D), x2b.reshape(T, D), route.reshape(T, LANES), lpt, meta,
                 w_gate[l], w_up[l], w_down[l],
                 _row(g_ln3[l]), _row(b_ln3[l]))
        x = y.reshape(B, S, D)
    return x
```

```python
import functools

import jax
import jax.numpy as jnp
from jax import lax
from jax.experimental import pallas as pl
from jax.experimental.pallas import tpu as pltpu

D_MODEL = 1024
MEM_LEN = 256
CONV_CH = 512
CONV_WIDTH = 31
N_HEADS = 8
N_KV_HEADS = 2
HEAD_DIM = 64
GQ = N_HEADS // N_KV_HEADS
ATTN_W = N_HEADS * HEAD_DIM
KV_W = N_KV_HEADS * HEAD_DIM
WINDOW = 128
D_MIX = CONV_CH + ATTN_W
D_IN = 2 * CONV_CH + ATTN_W + 2 * KV_W
MEM_HEADS = 4
MEM_HEAD_DIM = D_MODEL // MEM_HEADS
N_GROUPS = 4
EXPERTS_PER_GROUP = 4
N_EXPERTS = N_GROUPS * EXPERTS_PER_GROUP
D_EXPERT = D_MODEL // 2
DEPTH = 1
ALPHA = (2.0 * DEPTH) ** 0.25
LN_EPS = 1e-5

LANES = 128
SUBLANES = 8
CONV_ROWS = 128
LN_ROWS = 64
MEM_TILE = 1024
MASK_VALUE = -1e30
CONV_HALO = 32
SEQ_TILE = 512
MOE_TILE = 512
CHUNK = 16
GMM_ROWS = 512
TOP_K = 2
SORT_ROWS = -(-(MOE_TILE * TOP_K + N_EXPERTS * (CHUNK - 1)) // 256) * 256
BIG = 4
MAX_BIG = SORT_ROWS // (BIG * CHUNK)
MAX_SMALL = N_EXPERTS * (BIG - 1)
PER_STEP = 2
WAIT_GROUP = 8
ROUTE_OFF = N_GROUPS
VMEM_LIMIT = 56 * 1024 * 1024

BF16 = jnp.bfloat16
F32 = jnp.float32


def _layer_norm(x, g, b):
    mu = jnp.mean(x, axis=-1, keepdims=True)
    xc = x - mu
    var = jnp.mean(xc * xc, axis=-1, keepdims=True)
    return xc * lax.rsqrt(var + LN_EPS) * g + b


def _cast_bf16(dst_ref, src_ref):
    rows = 256
    for r0 in range(0, src_ref.shape[0], rows):
        dst_ref[r0:r0 + rows, :] = src_ref[r0:r0 + rows, :].astype(BF16)


def _dot_nt(a, b):
    return lax.dot_general(a, b, (((1,), (1,)), ((), ())), preferred_element_type=F32)


def _mixer_kernel(sinks_ref, x_ref, w_in_ref, b_in_ref, w_dw_ref, b_dw_ref, g_cn_ref, b_cn_ref,
                  w_out_ref, g1_ref, b1_ref, o_ref, w_in_b, w_out_b, hbuf, hshift, cbuf, qbuf, kbuf, vbuf, ymix):
    i = pl.program_id(1)
    ts = SEQ_TILE

    @pl.when(jnp.logical_and(pl.program_id(0) == 0, i == 0))
    def _():
        _cast_bf16(w_in_b, w_in_ref)
        _cast_bf16(w_out_b, w_out_ref)

    @pl.when(i == 0)
    def _():
        hbuf[0:CONV_HALO, :] = jnp.zeros((CONV_HALO, CONV_CH), F32)
        kbuf[:, 0:WINDOW, :] = jnp.zeros((2 * N_KV_HEADS, WINDOW, KV_W), BF16)
        vbuf[:, 0:WINDOW, :] = jnp.zeros((2, WINDOW, KV_W), BF16)

    x = x_ref[0]
    u = jnp.dot(x.astype(BF16), w_in_b[...], preferred_element_type=F32) + b_in_ref[...]
    a = u[:, 0:CONV_CH]
    gate = u[:, CONV_CH:2 * CONV_CH]
    hbuf[CONV_HALO:CONV_HALO + ts, :] = a * jax.nn.sigmoid(gate)
    qbuf[...] = (u[:, 2 * CONV_CH:2 * CONV_CH + ATTN_W] * (HEAD_DIM ** -0.5)).astype(BF16)
    kf = u[:, 2 * CONV_CH + ATTN_W:2 * CONV_CH + ATTN_W + KV_W]
    vf = u[:, 2 * CONV_CH + ATTN_W + KV_W:D_IN]
    kr = pltpu.roll(kf, HEAD_DIM, axis=1)
    vr = pltpu.roll(vf, HEAD_DIM, axis=1)
    lo = lax.broadcasted_iota(jnp.int32, (ts, KV_W), 1) < HEAD_DIM
    rows = slice(WINDOW, WINDOW + ts)
    kbuf[0, rows, :] = jnp.where(lo, kf, 0.0).astype(BF16)
    kbuf[1, rows, :] = jnp.where(lo, 0.0, kr).astype(BF16)
    kbuf[2, rows, :] = jnp.where(lo, kr, 0.0).astype(BF16)
    kbuf[3, rows, :] = jnp.where(lo, 0.0, kf).astype(BF16)
    vbuf[0, rows, :] = vf.astype(BF16)
    vbuf[1, rows, :] = vr.astype(BF16)

    base = CONV_HALO - (CONV_WIDTH - 1)
    n_shift = ts + CONV_HALO - SUBLANES
    for b in range(1, SUBLANES):
        hshift[b - 1, 0:n_shift, :] = hbuf[b:b + n_shift, :]
    rc = CONV_ROWS

    def conv_chunk(c):
        r0 = c * rc
        for l in range(CONV_CH // LANES):
            ls = slice(l * LANES, (l + 1) * LANES)
            acc = jnp.zeros((rc, LANES), F32)
            for j in range(CONV_WIDTH):
                a8, b = divmod(j + base, SUBLANES)
                rs = slice(r0 + SUBLANES * a8, r0 + SUBLANES * a8 + rc)
                tap = hbuf[rs, ls] if b == 0 else hshift[b - 1, rs, ls]
                acc = acc + tap * w_dw_ref[j:j + 1, ls]
            cbuf[r0:r0 + rc, ls] = acc
        for r1 in range(r0, r0 + rc, LN_ROWS):
            rs = slice(r1, r1 + LN_ROWS)
            y = _layer_norm(cbuf[rs, :] + b_dw_ref[...], g_cn_ref[...], b_cn_ref[...])
            y = y * jax.nn.sigmoid(y)
            ymix[rs, 0:CONV_CH] = y.astype(BF16)

    qi = lax.broadcasted_iota(jnp.int32, (2 * WINDOW, 2 * WINDOW), 0) % WINDOW
    kj = lax.broadcasted_iota(jnp.int32, (2 * WINDOW, 2 * WINDOW), 1)
    dist = qi + WINDOW - kj
    band = (dist >= 0) & (dist < WINDOW)
    top = lax.broadcasted_iota(jnp.int32, (2 * WINDOW, 1), 0) < WINDOW
    lo_out = lax.broadcasted_iota(jnp.int32, (WINDOW, 2 * HEAD_DIM), 1) < HEAD_DIM

    def attn_block(jb):
        r0 = jb * WINDOW
        valid = band & jnp.logical_or(i != 0, kj >= WINDOW) if jb == 0 else band
        for kvh in range(N_KV_HEADS):
            h0 = kvh * GQ
            c0 = h0 * HEAD_DIM
            qs = jnp.concatenate([qbuf[r0:r0 + WINDOW, c0:c0 + 2 * HEAD_DIM],
                                  qbuf[r0:r0 + WINDOW, c0 + 2 * HEAD_DIM:c0 + 4 * HEAD_DIM]], axis=0)
            pv = []
            for par in range(2):
                kk = kbuf[2 * kvh + par, r0:r0 + 2 * WINDOW, :]
                vv = vbuf[(kvh + par) % 2, r0:r0 + 2 * WINDOW, :]
                s = jnp.where(valid, _dot_nt(qs, kk), MASK_VALUE)
                sink = jnp.where(top, sinks_ref[h0 + par], sinks_ref[h0 + 2 + par])
                m = jnp.maximum(jnp.max(s, axis=-1, keepdims=True), sink)
                p = jnp.exp(s - m)
                denom = jnp.sum(p, axis=-1, keepdims=True) + jnp.exp(sink - m)
                pv.append(jnp.dot(p.astype(BF16), vv, preferred_element_type=F32) / denom)
            for pair in range(2):
                rs = slice(pair * WINDOW, (pair + 1) * WINDOW)
                o = jnp.where(lo_out, pv[0][rs], pv[1][rs])
                cs = CONV_CH + c0 + pair * 2 * HEAD_DIM
                ymix[r0:r0 + WINDOW, cs:cs + 2 * HEAD_DIM] = o.astype(BF16)

    assert ts // rc == ts // WINDOW
    for c in range(ts // rc):
        attn_block(c)
        conv_chunk(c)

    mix = jnp.dot(ymix[...], w_out_b[...], preferred_element_type=F32)
    o_ref[0] = _layer_norm(ALPHA * x + mix, g1_ref[...], b1_ref[...])

    hbuf[0:CONV_HALO, :] = hbuf[ts:ts + CONV_HALO, :]
    kbuf[:, 0:WINDOW, :] = kbuf[:, ts:ts + WINDOW, :]
    vbuf[:, 0:WINDOW, :] = vbuf[:, ts:ts + WINDOW, :]


def _const_spec(shape):
    nd = len(shape)
    return pl.BlockSpec(shape, lambda *_: (0,) * nd)


def _resident_spec(shape):
    nd = len(shape)
    return pl.BlockSpec(shape, lambda *_: (0,) * nd, pipeline_mode=pl.Buffered(1))


def _mixer(x, sinks, w_in, b_in, w_dw, b_dw, g_cn, b_cn, w_out, g1, b1):
    B, S, D = x.shape
    ts = SEQ_TILE
    tile = pl.BlockSpec((1, ts, D), lambda b, i: (b, i, 0))
    return pl.pallas_call(
        _mixer_kernel,
        grid=(B, S // ts),
        in_specs=[
            pl.BlockSpec(memory_space=pltpu.SMEM),
            tile,
            _resident_spec((D, D_IN)), _const_spec((1, D_IN)),
            _const_spec((CONV_HALO, CONV_CH)), _const_spec((1, CONV_CH)),
            _const_spec((1, CONV_CH)), _const_spec((1, CONV_CH)),
            _resident_spec((D_MIX, D)), _const_spec((1, D)), _const_spec((1, D)),
        ],
        out_specs=tile,
        out_shape=jax.ShapeDtypeStruct((B, S, D), F32),
        scratch_shapes=[
            pltpu.VMEM((D, D_IN), BF16),
            pltpu.VMEM((D_MIX, D), BF16),
            pltpu.VMEM((CONV_HALO + ts, CONV_CH), F32),
            pltpu.VMEM((SUBLANES - 1, CONV_HALO + ts, CONV_CH), F32),
            pltpu.VMEM((ts, CONV_CH), F32),
            pltpu.VMEM((ts, ATTN_W), BF16),
            pltpu.VMEM((2 * N_KV_HEADS, WINDOW + ts, KV_W), BF16),
            pltpu.VMEM((2, WINDOW + ts, KV_W), BF16),
            pltpu.VMEM((ts, D_MIX), BF16),
        ],
        compiler_params=pltpu.CompilerParams(
            dimension_semantics=("arbitrary", "arbitrary"), vmem_limit_bytes=VMEM_LIMIT),
        name="mixer",
    )(sinks, x, w_in, b_in, w_dw, b_dw, g_cn, b_cn, w_out, g1, b1)


def _memkv_kernel(mem_ref, w_ref, o_ref):
    o_ref[...] = jnp.dot(mem_ref[...].astype(BF16), w_ref[...].astype(BF16),
                         preferred_element_type=F32).astype(BF16)


def _memkv(mem2d, w_mkv):
    M, D = mem2d.shape
    N = w_mkv.shape[1]
    tn = 512
    return pl.pallas_call(
        _memkv_kernel,
        grid=(N // tn,),
        in_specs=[pl.BlockSpec((M, D), lambda j: (0, 0)), pl.BlockSpec((D, tn), lambda j: (0, j))],
        out_specs=pl.BlockSpec((M, tn), lambda j: (0, j)),
        out_shape=jax.ShapeDtypeStruct((M, N), BF16),
        compiler_params=pltpu.CompilerParams(dimension_semantics=("arbitrary",)),
        name="memkv",
    )(mem2d, w_mkv)


def _first_max(rows):
    best = rows[0]
    for r in rows[1:]:
        best = jnp.maximum(best, r)
    idx = jnp.full(best.shape, len(rows) - 1, jnp.int32)
    for k in range(len(rows) - 2, -1, -1):
        idx = jnp.where(rows[k] == best, k, idx)
    return best, idx


def _route_plan(logits_t):
    tile = logits_t.shape[1]
    row = lambda k: logits_t[k:k + 1, :]
    gmax, g_idx = _first_max([row(g) for g in range(N_GROUPS)])
    gsum = jnp.exp(row(0) - gmax)
    for g in range(1, N_GROUPS):
        gsum = gsum + jnp.exp(row(g) - gmax)
    g_p = 1.0 / gsum
    rl = []
    for e in range(EXPERTS_PER_GROUP):
        v = row(ROUTE_OFF + (N_GROUPS - 1) * EXPERTS_PER_GROUP + e)
        for g in range(N_GROUPS - 2, -1, -1):
            v = jnp.where(g_idx == g, row(ROUTE_OFF + g * EXPERTS_PER_GROUP + e), v)
        rl.append(v)
    m1, i1 = _first_max(rl)
    m2, i2 = _first_max([jnp.where(i1 == e, MASK_VALUE, rl[e]) for e in range(EXPERTS_PER_GROUP)])
    ex = jnp.exp(m2 - m1)
    w1 = 1.0 / (1.0 + ex)
    w2 = ex * w1
    e1 = g_idx * EXPERTS_PER_GROUP + i1
    e2 = g_idx * EXPERTS_PER_GROUP + i2

    eid = lax.broadcasted_iota(jnp.int32, (N_EXPERTS, tile), 0)
    hit1 = eid == e1
    hit2 = eid == e2
    oh = jnp.where(jnp.logical_or(hit1, hit2), 1.0, 0.0)
    r = lax.broadcasted_iota(jnp.int32, (tile, tile), 0)
    c = lax.broadcasted_iota(jnp.int32, (tile, tile), 1)
    tri = jnp.where(r <= c, 1.0, 0.0).astype(BF16)
    csum = jnp.dot(oh.astype(BF16), tri, preferred_element_type=F32)
    counts = jnp.broadcast_to(csum[:, tile - 1:tile], (N_EXPERTS, tile)).astype(jnp.int32)
    nch = jnp.right_shift(counts + (CHUNK - 1), CHUNK.bit_length() - 1)
    er = lax.broadcasted_iota(jnp.int32, (N_EXPERTS, N_EXPERTS), 0)
    ec = lax.broadcasted_iota(jnp.int32, (N_EXPERTS, N_EXPERTS), 1)
    lower = jnp.where(ec < er, 1.0, 0.0).astype(BF16)
    off = jnp.dot(lower, nch.astype(F32).astype(BF16), preferred_element_type=F32) * CHUNK
    pos = off + csum - oh
    lp1 = jnp.sum(jnp.where(hit1, pos, 0.0), axis=0, keepdims=True)
    lp2 = jnp.sum(jnp.where(hit2, pos, 0.0), axis=0, keepdims=True)
    zero = jnp.zeros_like(lp1)
    route_t = jnp.concatenate([lp1, lp2, g_p * w1, g_p * w2, zero, zero, zero, zero], axis=0)
    meta = jnp.concatenate([nch[:, 0:LANES], off[:, 0:LANES].astype(jnp.int32)], axis=0)
    return route_t, meta


def _memattn_kernel(x_ref, wq_ref, k_ref, v_ref, wo_ref, g2_ref, b2_ref, wr2_ref, wrh_ref, br_ref,
                    o_ref, ob_ref, route_ref, routet_ref, meta_ref, wq_b, wo_b):
    @pl.when(jnp.logical_and(pl.program_id(0) == 0, pl.program_id(1) == 0))
    def _():
        _cast_bf16(wq_b, wq_ref)
        _cast_bf16(wo_b, wo_ref)

    def rows_logits(rs):
        x = x_ref[0, rs, :]
        q = jnp.dot(x.astype(BF16), wq_b[...], preferred_element_type=F32)
        q = (q * (MEM_HEAD_DIM ** -0.5)).astype(BF16)
        outs = []
        for h in range(MEM_HEADS):
            sl = slice(h * MEM_HEAD_DIM, (h + 1) * MEM_HEAD_DIM)
            s = _dot_nt(q[:, sl], k_ref[0, :, sl])
            m = jnp.max(s, axis=-1, keepdims=True)
            p = jnp.exp(s - m)
            denom = jnp.sum(p, axis=-1, keepdims=True)
            o = jnp.dot(p.astype(BF16), v_ref[0, :, sl], preferred_element_type=F32)
            outs.append((o / denom).astype(BF16))
        o = jnp.dot(jnp.concatenate(outs, axis=-1), wo_b[...], preferred_element_type=F32)
        x2 = _layer_norm(ALPHA * x + o, g2_ref[...], b2_ref[...])
        o_ref[0, rs, :] = x2
        x2h = x2.astype(BF16)
        ob_ref[0, rs, :] = x2h
        x2l = (x2 - x2h.astype(F32)).astype(BF16)
        hh = jnp.dot(x2h, wr2_ref[...], preferred_element_type=F32)
        return (hh[:, 0:LANES] + hh[:, LANES:2 * LANES]
                + jnp.dot(x2l, wrh_ref[...], preferred_element_type=F32) + br_ref[...])

    ts = x_ref.shape[1]
    groups = [slice(r0, r0 + MOE_TILE) for r0 in range(0, ts, MOE_TILE)]
    logits = [rows_logits(rs) for rs in groups]
    pad = jnp.zeros((LANES - SUBLANES, MOE_TILE), F32)
    for k, rs in enumerate(groups):
        route_t, meta = _route_plan(jnp.transpose(logits[k]))
        routet_ref[k] = route_t
        route_ref[0, rs, :] = jnp.transpose(jnp.concatenate([route_t, pad], axis=0))
        meta_ref[k] = meta


def _memattn(x1, wq, kvm, wo, g2, b2, wr, br):
    B, S, D = x1.shape
    ts = MEM_TILE
    per = ts // MOE_TILE
    nt = S // ts
    c = wr * (2.0 ** 16 + 1.0)
    w_high = c - (c - wr)
    wrh = w_high.astype(BF16)
    wr2 = jnp.concatenate([wrh, (wr - w_high).astype(BF16)], axis=1)
    tile = pl.BlockSpec((1, ts, D), lambda b, i: (b, i, 0))
    kspec = pl.BlockSpec((1, MEM_LEN, D), lambda b, i: (b, 0, 0))
    vspec = pl.BlockSpec((1, MEM_LEN, D), lambda b, i: (b, 0, 1))
    return pl.pallas_call(
        _memattn_kernel,
        grid=(B, nt),
        in_specs=[tile, _resident_spec((D, D)), kspec, vspec, _resident_spec((D, D)),
                  _const_spec((1, D)), _const_spec((1, D)),
                  _const_spec((D, 2 * LANES)), _const_spec((D, LANES)), _const_spec((1, LANES))],
        out_specs=[tile, tile, pl.BlockSpec((1, ts, LANES), lambda b, i: (b, i, 0)),
                   pl.BlockSpec((per, SUBLANES, MOE_TILE), lambda b, i: (b * nt + i, 0, 0)),
                   pl.BlockSpec((per, 2 * N_EXPERTS, LANES), lambda b, i: (b * nt + i, 0, 0))],
        out_shape=[jax.ShapeDtypeStruct((B, S, D), F32),
                   jax.ShapeDtypeStruct((B, S, D), BF16),
                   jax.ShapeDtypeStruct((B, S, LANES), F32),
                   jax.ShapeDtypeStruct((B * nt * per, SUBLANES, MOE_TILE), F32),
                   jax.ShapeDtypeStruct((B * nt * per, 2 * N_EXPERTS, LANES), jnp.int32)],
        scratch_shapes=[pltpu.VMEM((D, D), BF16), pltpu.VMEM((D, D), BF16)],
        compiler_params=pltpu.CompilerParams(
            dimension_semantics=("arbitrary", "arbitrary"), vmem_limit_bytes=VMEM_LIMIT),
        name="memattn",
    )(x1, wq, kvm, kvm, wo, g2, b2, wr2, wrh, br)


def _gmm_blocks(n_tokens):
    rows = (n_tokens * TOP_K + (n_tokens // MOE_TILE) * N_EXPERTS * (CHUNK - 1)
            + N_EXPERTS * (GMM_ROWS - CHUNK))
    return -(-rows // GMM_ROWS)


def _plan(meta, T):
    nch = meta[:, :N_EXPERTS, 0]
    n16 = nch * CHUNK
    n_e = jnp.sum(n16, axis=0)
    reg = (n_e + GMM_ROWS - 1) // GMM_ROWS * GMM_ROWS
    gend = jnp.cumsum(reg)
    gbase = gend - reg
    dst = gbase[None, :] + jnp.cumsum(n16, axis=0) - n16
    blk_row = jnp.arange(_gmm_blocks(T), dtype=jnp.int32)[:, None] * GMM_ROWS
    blk_expert = jnp.sum(blk_row >= gend[None, :], axis=1)
    blk_expert = jnp.minimum(blk_expert, N_EXPERTS - 1).astype(jnp.int32)
    n_used = (gend[-1] // GMM_ROWS).astype(jnp.int32).reshape(1)
    data_end = jnp.sum(jnp.where(blk_expert[:, None] == jnp.arange(N_EXPERTS)[None, :],
                                 (gbase + n_e)[None, :], 0), axis=1)
    blk_rows = jnp.where(blk_row[:, 0] < gend[-1], jnp.clip(data_end - blk_row[:, 0], 0, GMM_ROWS), 0)
    src = (jnp.cumsum(nch, axis=1) - nch) * CHUNK
    n_big = nch // BIG

    def copy_list(count, src0, dst0, rows, length):
        cum = jnp.cumsum(count, axis=1)
        first = (cum - count)[:, None, :]
        k = jnp.arange(length, dtype=jnp.int32)[None, :, None]
        mine = (k >= first) & (k < cum[:, None, :])
        step = (k - first) * rows
        pick = lambda base: jnp.sum(jnp.where(mine, base[:, None, :] + step, 0), axis=2)
        return pick(src0), pick(dst0), cum[:, -1]

    big_src, big_dst, big_n = copy_list(n_big, src, dst, BIG * CHUNK, MAX_BIG)
    rest = n_big * (BIG * CHUNK)
    small_src, small_dst, small_n = copy_list(nch - n_big * BIG, src + rest, dst + rest, CHUNK, MAX_SMALL)
    i32 = lambda a: a.astype(jnp.int32)
    fill_start = jnp.concatenate([gbase + n_e, gend[-1:]])
    fill_n = jnp.concatenate([reg - n_e, _gmm_blocks(T) * GMM_ROWS - gend[-1:]]) // CHUNK
    return dict(big_src=i32(big_src).reshape(-1), big_dst=i32(big_dst).reshape(-1), big_n=i32(big_n),
                small_src=i32(small_src).reshape(-1), small_dst=i32(small_dst).reshape(-1), small_n=i32(small_n),
                fill_start=i32(fill_start), fill_n=i32(fill_n),
                fill_tot=i32(jnp.sum(fill_n)).reshape(1), blk_expert=blk_expert, n_used=n_used,
                blk_rows=i32(blk_rows))


def _rows_copy(src_ref, src_row, dst_ref, dst_row, rows, sem):
    return pltpu.make_async_copy(
        src_ref.at[pl.ds(pl.multiple_of(src_row, CHUNK), rows), :],
        dst_ref.at[pl.ds(pl.multiple_of(dst_row, CHUNK), rows), :], sem)


def _chunk_copy(src_ref, src_row, dst_ref, dst_row, sem):
    return _rows_copy(src_ref, src_row, dst_ref, dst_row, CHUNK, sem)


def _issue_copies(lists, tile, tile_ref, tile_is_src, hbm_ref, sem):
    big_src, big_dst, big_n, small_src, small_dst, small_n = lists
    for src_l, dst_l, n_l, length, rows in ((big_src, big_dst, big_n, MAX_BIG, BIG * CHUNK),
                                            (small_src, small_dst, small_n, MAX_SMALL, CHUNK)):
        def issue(k, carry, src_l=src_l, dst_l=dst_l, length=length, rows=rows):
            local, remote = src_l[tile * length + k], dst_l[tile * length + k]
            if tile_is_src:
                _rows_copy(tile_ref, local, hbm_ref, remote, rows, sem).start()
            else:
                _rows_copy(hbm_ref, remote, tile_ref, local, rows, sem).start()
            return carry

        lax.fori_loop(0, n_l[tile], issue, 0)


def _wait_rows(n, rows, src_ref, dst_ref, sem):
    def body(c, carry):
        pltpu.make_async_copy(src_ref.at[pl.ds(0, rows), :], dst_ref.at[pl.ds(0, rows), :], sem).wait()
        return carry

    lax.fori_loop(0, n, body, 0)


def _wait_copies(lists, tile, src_ref, dst_ref, sem, extra_chunks=0):
    _wait_rows(lists[2][tile], BIG * CHUNK, src_ref, dst_ref, sem)
    n = lists[5][tile] + extra_chunks
    _wait_rows(n // WAIT_GROUP, WAIT_GROUP * CHUNK, src_ref, dst_ref, sem)
    _wait_rows(n % WAIT_GROUP, CHUNK, src_ref, dst_ref, sem)


def _dispatch_kernel(bs_ref, bd_ref, bn_ref, ss_ref, sd_ref, sn_ref, fstart_ref, fn_ref, ftot_ref,
                     lp_ref, x_ref, xs_hbm, xt, zbuf, sems):
    lists = (bs_ref, bd_ref, bn_ref, ss_ref, sd_ref, sn_ref)
    s = pl.program_id(0)
    ns = pl.num_programs(0)
    cur = (s % 2) * PER_STEP
    prv = PER_STEP - cur

    @pl.when(s >= 2)
    def _():
        for k in range(PER_STEP):
            _wait_copies(lists, (s - 2) * PER_STEP + k, xt.at[cur + k], xs_hbm, sems.at[cur + k])

    r = lax.broadcasted_iota(jnp.int32, (SORT_ROWS, MOE_TILE), 0)
    for k in range(PER_STEP):
        lp = lp_ref[k].astype(jnp.int32)
        hit = jnp.logical_or(lp[0:1, :] == r, lp[1:2, :] == r)
        p = jnp.where(hit, 1.0, 0.0).astype(BF16)
        x = x_ref[k * MOE_TILE:(k + 1) * MOE_TILE, :]
        xt[cur + k] = jnp.dot(p, x, preferred_element_type=F32).astype(BF16)

    for k in range(PER_STEP):
        _issue_copies(lists, s * PER_STEP + k, xt.at[cur + k], True, xs_hbm, sems.at[cur + k])

    @pl.when(s == ns - 1)
    def _():
        zbuf[...] = jnp.zeros_like(zbuf)
        sem = sems.at[cur]

        def per_range(e, carry):
            def issue(c, carry2):
                _chunk_copy(zbuf, 0, xs_hbm, fstart_ref[e] + c * CHUNK, sem).start()
                return carry2

            return lax.fori_loop(0, fn_ref[e], issue, carry)

        lax.fori_loop(0, N_EXPERTS + 1, per_range, 0)
        for k in range(PER_STEP):
            _wait_copies(lists, (s - 1) * PER_STEP + k, xt.at[prv + k], xs_hbm, sems.at[prv + k])
            _wait_copies(lists, s * PER_STEP + k, xt.at[cur + k], xs_hbm, sems.at[cur + k],
                         extra_chunks=ftot_ref[0] if k == 0 else 0)


def _copy_lists(plan):
    return tuple(plan[k] for k in ("big_src", "big_dst", "big_n", "small_src", "small_dst", "small_n"))


def _dispatch(plan, lpt, x2b):
    T, D = x2b.shape
    rows = PER_STEP * MOE_TILE
    assert T // rows >= 2
    grid_spec = pltpu.PrefetchScalarGridSpec(
        num_scalar_prefetch=9,
        grid=(T // rows,),
        in_specs=[pl.BlockSpec((PER_STEP, SUBLANES, MOE_TILE), lambda t, *_: (t, 0, 0)),
                  pl.BlockSpec((rows, D), lambda t, *_: (t, 0))],
        out_specs=pl.BlockSpec(memory_space=pl.ANY),
        scratch_shapes=[pltpu.VMEM((2 * PER_STEP, SORT_ROWS, D), BF16), pltpu.VMEM((CHUNK, D), BF16),
                        pltpu.SemaphoreType.DMA((2 * PER_STEP,))],
    )
    return pl.pallas_call(
        _dispatch_kernel,
        grid_spec=grid_spec,
        out_shape=jax.ShapeDtypeStruct((_gmm_blocks(T) * GMM_ROWS, D), BF16),
        compiler_params=pltpu.CompilerParams(
            dimension_semantics=("arbitrary",), vmem_limit_bytes=VMEM_LIMIT),
        name="moe_dispatch",
    )(*_copy_lists(plan), plan["fill_start"], plan["fill_n"], plan["fill_tot"], lpt, x2b)


def _expert_ffn(xb, wg_b, wu_b, wd_b):
    g = jnp.dot(xb, wg_b[...], preferred_element_type=F32)
    u = jnp.dot(xb, wu_b[...], preferred_element_type=F32)
    h = (g * jax.nn.sigmoid(g)) * u
    return jnp.dot(h.astype(BF16), wd_b[...], preferred_element_type=F32).astype(BF16)


def _gmm_kernel(be_ref, nu_ref, br_ref, x_ref, wg_ref, wu_ref, wd_ref, o_ref, wg_b, wu_b, wd_b):
    b = pl.program_id(0)
    rows = br_ref[b]
    half = GMM_ROWS // 2

    @pl.when(jnp.logical_or(b == 0, be_ref[b] != be_ref[jnp.maximum(b - 1, 0)]))
    def _():
        _cast_bf16(wg_b, wg_ref.at[0])
        _cast_bf16(wu_b, wu_ref.at[0])
        _cast_bf16(wd_b, wd_ref.at[0])

    @pl.when(rows > half)
    def _():
        o_ref[...] = _expert_ffn(x_ref[...], wg_b, wu_b, wd_b)

    @pl.when(jnp.logical_and(rows > 0, rows <= half))
    def _():
        o_ref[0:half, :] = _expert_ffn(x_ref[0:half, :], wg_b, wu_b, wd_b)
        o_ref[half:GMM_ROWS, :] = jnp.zeros((GMM_ROWS - half, o_ref.shape[1]), o_ref.dtype)

    @pl.when(rows == 0)
    def _():
        o_ref[...] = jnp.zeros_like(o_ref)


def _gmm(plan, xs, wg, wu, wd):
    R, D = xs.shape
    rows = pl.BlockSpec((GMM_ROWS, D), lambda b, be, nu, br: (jnp.minimum(b, nu[0] - 1), 0))
    grid_spec = pltpu.PrefetchScalarGridSpec(
        num_scalar_prefetch=3,
        grid=(R // GMM_ROWS,),
        in_specs=[rows,
                  pl.BlockSpec((1, D, D_EXPERT), lambda b, be, nu, br: (be[b], 0, 0)),
                  pl.BlockSpec((1, D, D_EXPERT), lambda b, be, nu, br: (be[b], 0, 0)),
                  pl.BlockSpec((1, D_EXPERT, D), lambda b, be, nu, br: (be[b], 0, 0))],
        out_specs=pl.BlockSpec((GMM_ROWS, D), lambda b, be, nu, br: (b, 0)),
        scratch_shapes=[pltpu.VMEM((D, D_EXPERT), BF16), pltpu.VMEM((D, D_EXPERT), BF16),
                        pltpu.VMEM((D_EXPERT, D), BF16)],
    )
    return pl.pallas_call(
        _gmm_kernel,
        grid_spec=grid_spec,
        out_shape=jax.ShapeDtypeStruct((R, D), BF16),
        compiler_params=pltpu.CompilerParams(
            dimension_semantics=("arbitrary",), vmem_limit_bytes=VMEM_LIMIT),
        name="moe_gmm",
    )(plan["blk_expert"], plan["n_used"], plan["blk_rows"], xs, wg, wu, wd)


def _combine_kernel(bs_ref, bd_ref, bn_ref, ss_ref, sd_ref, sn_ref, cm_ref, x_ref, ys_hbm, g3_ref, b3_ref,
                    o_ref, yt, sems):
    lists = (bs_ref, bd_ref, bn_ref, ss_ref, sd_ref, sn_ref)
    s = pl.program_id(0)
    ns = pl.num_programs(0)
    cur = (s % 2) * PER_STEP
    nxt = PER_STEP - cur

    def fetch(step, first_buf):
        for k in range(PER_STEP):
            _issue_copies(lists, step * PER_STEP + k, yt.at[first_buf + k], False, ys_hbm,
                          sems.at[first_buf + k])

    @pl.when(s == 0)
    def _():
        yt[...] = jnp.zeros_like(yt)
        fetch(0, 0)

    @pl.when(s + 1 < ns)
    def _():
        fetch(s + 1, nxt)

    for k in range(PER_STEP):
        _wait_copies(lists, s * PER_STEP + k, ys_hbm, yt.at[cur + k], sems.at[cur + k])

    col = lax.broadcasted_iota(jnp.int32, (MOE_TILE, SORT_ROWS), 1)
    for k in range(PER_STEP):
        rs = slice(k * MOE_TILE, (k + 1) * MOE_TILE)
        cm = cm_ref[rs, :]
        lp0 = cm[:, 0:1].astype(jnp.int32)
        lp1 = cm[:, 1:2].astype(jnp.int32)
        w = jnp.where(col == lp0, cm[:, 2:3], jnp.where(col == lp1, cm[:, 3:4], 0.0)).astype(BF16)
        y = jnp.dot(w, yt[cur + k], preferred_element_type=F32)
        o_ref[rs, :] = _layer_norm(ALPHA * x_ref[rs, :] + y, g3_ref[...], b3_ref[...])


def _combine(plan, route, x2, ys, g3, b3):
    T, D = x2.shape
    rows = PER_STEP * MOE_TILE
    grid_spec = pltpu.PrefetchScalarGridSpec(
        num_scalar_prefetch=6,
        grid=(T // rows,),
        in_specs=[pl.BlockSpec((rows, LANES), lambda t, *_: (t, 0)),
                  pl.BlockSpec((rows, D), lambda t, *_: (t, 0)),
                  pl.BlockSpec(memory_space=pl.ANY),
                  pl.BlockSpec((1, D), lambda t, *_: (0, 0)),
                  pl.BlockSpec((1, D), lambda t, *_: (0, 0))],
        out_specs=pl.BlockSpec((rows, D), lambda t, *_: (t, 0)),
        scratch_shapes=[pltpu.VMEM((2 * PER_STEP, SORT_ROWS, D), BF16),
                        pltpu.SemaphoreType.DMA((2 * PER_STEP,))],
    )
    return pl.pallas_call(
        _combine_kernel,
        grid_spec=grid_spec,
        out_shape=jax.ShapeDtypeStruct((T, D), F32),
        compiler_params=pltpu.CompilerParams(
            dimension_semantics=("arbitrary",), vmem_limit_bytes=VMEM_LIMIT),
        name="moe_combine",
    )(*_copy_lists(plan), route, x2, ys, g3, b3)


def _moe(x2, x2b, route, lpt, meta, wg, wu, wd, g3, b3):
    plan = _plan(meta, x2.shape[0])
    xs = _dispatch(plan, lpt, x2b)
    ys = _gmm(plan, xs, wg, wu, wd)
    return _combine(plan, route, x2, ys, g3, b3)


def _row(v):
    return v.reshape(1, -1).astype(F32)


def kernel(x, mem, w_in, b_in, w_dw, b_dw, g_conv_norm, b_conv_norm, attn_sinks, w_out, g_ln1, b_ln1,
           w_mq, w_mkv, w_mo, g_ln2, b_ln2, w_group, b_group, w_router, b_router, w_gate, w_up, w_down,
           g_ln3, b_ln3):
    B, S, D = x.shape
    for l in range(DEPTH):
        w_dw_p = jnp.zeros((CONV_HALO, CONV_CH), F32).at[:CONV_WIDTH].set(w_dw[l])
        x1 = _mixer(x, attn_sinks[l].astype(F32), w_in[l], _row(b_in[l]), w_dw_p,
                    _row(b_dw[l]), _row(g_conv_norm[l]), _row(b_conv_norm[l]),
                    w_out[l], _row(g_ln1[l]), _row(b_ln1[l]))

        kvm = _memkv(mem.reshape(B * MEM_LEN, D), w_mkv[l]).reshape(B, MEM_LEN, 2 * D)

        wr = jnp.concatenate(
            [w_group[l], jnp.transpose(w_router[l], (1, 0, 2)).reshape(D, N_EXPERTS)], axis=1)
        wr = jnp.pad(wr, ((0, 0), (0, LANES - wr.shape[1])))
        br = jnp.pad(jnp.concatenate([b_group[l], b_router[l].reshape(-1)]), (0, LANES - N_GROUPS - N_EXPERTS))
        x2, x2b, route, lpt, meta = _memattn(x1, w_mq[l], kvm, w_mo[l],
                                 _row(g_ln2[l]), _row(b_ln2[l]), wr.astype(F32), _row(br))

        T = B * S
        y = _moe(x2.reshape(T, D), x2b.reshape(T, D), route.reshape(T, LANES), lpt, meta,
                 w_gate[l].astype(BF16), w_up[l].astype(BF16), w_down[l].astype(BF16),
                 _row(g_ln3[l]), _row(b_ln3[l]))
        x = y.reshape(B, S, D)
    return x
```

```python
import functools

import jax
import jax.numpy as jnp
from jax import lax
from jax.experimental import pallas as pl
from jax.experimental.pallas import tpu as pltpu

D_MODEL = 1024
MEM_LEN = 256
CONV_CH = 512
CONV_WIDTH = 31
N_HEADS = 8
N_KV_HEADS = 2
HEAD_DIM = 64
GQ = N_HEADS // N_KV_HEADS
ATTN_W = N_HEADS * HEAD_DIM
KV_W = N_KV_HEADS * HEAD_DIM
WINDOW = 128
D_MIX = CONV_CH + ATTN_W
D_IN = 2 * CONV_CH + ATTN_W + 2 * KV_W
MEM_HEADS = 4
MEM_HEAD_DIM = D_MODEL // MEM_HEADS
N_GROUPS = 4
EXPERTS_PER_GROUP = 4
N_EXPERTS = N_GROUPS * EXPERTS_PER_GROUP
D_EXPERT = D_MODEL // 2
DEPTH = 1
ALPHA = (2.0 * DEPTH) ** 0.25
LN_EPS = 1e-5

LANES = 128
SUBLANES = 8
CONV_ROWS = 128
LN_ROWS = 64
MEM_TILE = 1024
MASK_VALUE = -1e30
CONV_HALO = 32
SEQ_TILE = 512
MOE_TILE = 512
CHUNK = 16
GMM_ROWS = 1024
TOP_K = 2
SORT_ROWS = -(-(MOE_TILE * TOP_K + N_EXPERTS * (CHUNK - 1)) // 256) * 256
BIG = 4
MAX_BIG = SORT_ROWS // (BIG * CHUNK)
MAX_SMALL = N_EXPERTS * (BIG - 1)
PER_STEP = 2
WAIT_GROUP = 8
ROUTE_OFF = N_GROUPS
VMEM_LIMIT = 56 * 1024 * 1024

BF16 = jnp.bfloat16
F32 = jnp.float32


def _layer_norm(x, g, b):
    mu = jnp.mean(x, axis=-1, keepdims=True)
    xc = x - mu
    var = jnp.mean(xc * xc, axis=-1, keepdims=True)
    return xc * lax.rsqrt(var + LN_EPS) * g + b


def _cast_bf16(dst_ref, src_ref):
    rows = 256
    for r0 in range(0, src_ref.shape[0], rows):
        dst_ref[r0:r0 + rows, :] = src_ref[r0:r0 + rows, :].astype(BF16)


def _dot_nt(a, b):
    return lax.dot_general(a, b, (((1,), (1,)), ((), ())), preferred_element_type=F32)


def _mixer_kernel(sinks_ref, x_ref, w_in_ref, b_in_ref, w_dw_ref, b_dw_ref, g_cn_ref, b_cn_ref,
                  w_out_ref, g1_ref, b1_ref, o_ref, w_in_b, w_out_b, hbuf, hshift, cbuf, qbuf, kbuf, vbuf, ymix):
    i = pl.program_id(1)
    ts = SEQ_TILE

    @pl.when(jnp.logical_and(pl.program_id(0) == 0, i == 0))
    def _():
        _cast_bf16(w_in_b, w_in_ref)
        _cast_bf16(w_out_b, w_out_ref)

    @pl.when(i == 0)
    def _():
        hbuf[0:CONV_HALO, :] = jnp.zeros((CONV_HALO, CONV_CH), F32)
        kbuf[:, 0:WINDOW, :] = jnp.zeros((2 * N_KV_HEADS, WINDOW, KV_W), BF16)
        vbuf[:, 0:WINDOW, :] = jnp.zeros((2, WINDOW, KV_W), BF16)

    x = x_ref[0]
    u = jnp.dot(x.astype(BF16), w_in_b[...], preferred_element_type=F32) + b_in_ref[...]
    a = u[:, 0:CONV_CH]
    gate = u[:, CONV_CH:2 * CONV_CH]
    hbuf[CONV_HALO:CONV_HALO + ts, :] = a * jax.nn.sigmoid(gate)
    qbuf[...] = (u[:, 2 * CONV_CH:2 * CONV_CH + ATTN_W] * (HEAD_DIM ** -0.5)).astype(BF16)
    kf = u[:, 2 * CONV_CH + ATTN_W:2 * CONV_CH + ATTN_W + KV_W]
    vf = u[:, 2 * CONV_CH + ATTN_W + KV_W:D_IN]
    kr = pltpu.roll(kf, HEAD_DIM, axis=1)
    vr = pltpu.roll(vf, HEAD_DIM, axis=1)
    lo = lax.broadcasted_iota(jnp.int32, (ts, KV_W), 1) < HEAD_DIM
    rows = slice(WINDOW, WINDOW + ts)
    kbuf[0, rows, :] = jnp.where(lo, kf, 0.0).astype(BF16)
    kbuf[1, rows, :] = jnp.where(lo, 0.0, kr).astype(BF16)
    kbuf[2, rows, :] = jnp.where(lo, kr, 0.0).astype(BF16)
    kbuf[3, rows, :] = jnp.where(lo, 0.0, kf).astype(BF16)
    vbuf[0, rows, :] = vf.astype(BF16)
    vbuf[1, rows, :] = vr.astype(BF16)

    base = CONV_HALO - (CONV_WIDTH - 1)
    n_shift = ts + CONV_HALO - SUBLANES
    for b in range(1, SUBLANES):
        hshift[b - 1, 0:n_shift, :] = hbuf[b:b + n_shift, :]
    rc = CONV_ROWS

    def conv_chunk(c):
        r0 = c * rc
        for l in range(CONV_CH // LANES):
            ls = slice(l * LANES, (l + 1) * LANES)
            acc = jnp.zeros((rc, LANES), F32)
            for j in range(CONV_WIDTH):
                a8, b = divmod(j + base, SUBLANES)
                rs = slice(r0 + SUBLANES * a8, r0 + SUBLANES * a8 + rc)
                tap = hbuf[rs, ls] if b == 0 else hshift[b - 1, rs, ls]
                acc = acc + tap * w_dw_ref[j:j + 1, ls]
            cbuf[r0:r0 + rc, ls] = acc
        for r1 in range(r0, r0 + rc, LN_ROWS):
            rs = slice(r1, r1 + LN_ROWS)
            y = _layer_norm(cbuf[rs, :] + b_dw_ref[...], g_cn_ref[...], b_cn_ref[...])
            y = y * jax.nn.sigmoid(y)
            ymix[rs, 0:CONV_CH] = y.astype(BF16)

    qi = lax.broadcasted_iota(jnp.int32, (2 * WINDOW, 2 * WINDOW), 0) % WINDOW
    kj = lax.broadcasted_iota(jnp.int32, (2 * WINDOW, 2 * WINDOW), 1)
    dist = qi + WINDOW - kj
    band = (dist >= 0) & (dist < WINDOW)
    top = lax.broadcasted_iota(jnp.int32, (2 * WINDOW, 1), 0) < WINDOW
    lo_out = lax.broadcasted_iota(jnp.int32, (WINDOW, 2 * HEAD_DIM), 1) < HEAD_DIM

    def attn_block(jb):
        r0 = jb * WINDOW
        valid = band & jnp.logical_or(i != 0, kj >= WINDOW) if jb == 0 else band
        for kvh in range(N_KV_HEADS):
            h0 = kvh * GQ
            c0 = h0 * HEAD_DIM
            qs = jnp.concatenate([qbuf[r0:r0 + WINDOW, c0:c0 + 2 * HEAD_DIM],
                                  qbuf[r0:r0 + WINDOW, c0 + 2 * HEAD_DIM:c0 + 4 * HEAD_DIM]], axis=0)
            pv = []
            for par in range(2):
                kk = kbuf[2 * kvh + par, r0:r0 + 2 * WINDOW, :]
                vv = vbuf[(kvh + par) % 2, r0:r0 + 2 * WINDOW, :]
                s = jnp.where(valid, _dot_nt(qs, kk), MASK_VALUE)
                sink = jnp.where(top, sinks_ref[h0 + par], sinks_ref[h0 + 2 + par])
                m = jnp.maximum(jnp.max(s, axis=-1, keepdims=True), sink)
                p = jnp.exp(s - m)
                denom = jnp.sum(p, axis=-1, keepdims=True) + jnp.exp(sink - m)
                pv.append(jnp.dot(p.astype(BF16), vv, preferred_element_type=F32) / denom)
            for pair in range(2):
                rs = slice(pair * WINDOW, (pair + 1) * WINDOW)
                o = jnp.where(lo_out, pv[0][rs], pv[1][rs])
                cs = CONV_CH + c0 + pair * 2 * HEAD_DIM
                ymix[r0:r0 + WINDOW, cs:cs + 2 * HEAD_DIM] = o.astype(BF16)

    assert ts // rc == ts // WINDOW
    for c in range(ts // rc):
        attn_block(c)
        conv_chunk(c)

    mix = jnp.dot(ymix[...], w_out_b[...], preferred_element_type=F32)
    o_ref[0] = _layer_norm(ALPHA * x + mix, g1_ref[...], b1_ref[...])

    hbuf[0:CONV_HALO, :] = hbuf[ts:ts + CONV_HALO, :]
    kbuf[:, 0:WINDOW, :] = kbuf[:, ts:ts + WINDOW, :]
    vbuf[:, 0:WINDOW, :] = vbuf[:, ts:ts + WINDOW, :]


def _const_spec(shape):
    nd = len(shape)
    return pl.BlockSpec(shape, lambda *_: (0,) * nd)


def _resident_spec(shape):
    nd = len(shape)
    return pl.BlockSpec(shape, lambda *_: (0,) * nd, pipeline_mode=pl.Buffered(1))


def _mixer(x, sinks, w_in, b_in, w_dw, b_dw, g_cn, b_cn, w_out, g1, b1):
    B, S, D = x.shape
    ts = SEQ_TILE
    tile = pl.BlockSpec((1, ts, D), lambda b, i: (b, i, 0))
    return pl.pallas_call(
        _mixer_kernel,
        grid=(B, S // ts),
        in_specs=[
            pl.BlockSpec(memory_space=pltpu.SMEM),
            tile,
            _resident_spec((D, D_IN)), _const_spec((1, D_IN)),
            _const_spec((CONV_HALO, CONV_CH)), _const_spec((1, CONV_CH)),
            _const_spec((1, CONV_CH)), _const_spec((1, CONV_CH)),
            _resident_spec((D_MIX, D)), _const_spec((1, D)), _const_spec((1, D)),
        ],
        out_specs=tile,
        out_shape=jax.ShapeDtypeStruct((B, S, D), F32),
        scratch_shapes=[
            pltpu.VMEM((D, D_IN), BF16),
            pltpu.VMEM((D_MIX, D), BF16),
            pltpu.VMEM((CONV_HALO + ts, CONV_CH), F32),
            pltpu.VMEM((SUBLANES - 1, CONV_HALO + ts, CONV_CH), F32),
            pltpu.VMEM((ts, CONV_CH), F32),
            pltpu.VMEM((ts, ATTN_W), BF16),
            pltpu.VMEM((2 * N_KV_HEADS, WINDOW + ts, KV_W), BF16),
            pltpu.VMEM((2, WINDOW + ts, KV_W), BF16),
            pltpu.VMEM((ts, D_MIX), BF16),
        ],
        compiler_params=pltpu.CompilerParams(
            dimension_semantics=("arbitrary", "arbitrary"), vmem_limit_bytes=VMEM_LIMIT),
        name="mixer",
    )(sinks, x, w_in, b_in, w_dw, b_dw, g_cn, b_cn, w_out, g1, b1)


def _memkv_kernel(mem_ref, w_ref, o_ref):
    o_ref[...] = jnp.dot(mem_ref[...].astype(BF16), w_ref[...].astype(BF16),
                         preferred_element_type=F32).astype(BF16)


def _memkv(mem2d, w_mkv):
    M, D = mem2d.shape
    N = w_mkv.shape[1]
    tn = 512
    return pl.pallas_call(
        _memkv_kernel,
        grid=(N // tn,),
        in_specs=[pl.BlockSpec((M, D), lambda j: (0, 0)), pl.BlockSpec((D, tn), lambda j: (0, j))],
        out_specs=pl.BlockSpec((M, tn), lambda j: (0, j)),
        out_shape=jax.ShapeDtypeStruct((M, N), BF16),
        compiler_params=pltpu.CompilerParams(dimension_semantics=("arbitrary",)),
        name="memkv",
    )(mem2d, w_mkv)


def _first_max(rows):
    best = rows[0]
    for r in rows[1:]:
        best = jnp.maximum(best, r)
    idx = jnp.full(best.shape, len(rows) - 1, jnp.int32)
    for k in range(len(rows) - 2, -1, -1):
        idx = jnp.where(rows[k] == best, k, idx)
    return best, idx


def _route_plan(logits_t):
    tile = logits_t.shape[1]
    row = lambda k: logits_t[k:k + 1, :]
    gmax, g_idx = _first_max([row(g) for g in range(N_GROUPS)])
    gsum = jnp.exp(row(0) - gmax)
    for g in range(1, N_GROUPS):
        gsum = gsum + jnp.exp(row(g) - gmax)
    g_p = 1.0 / gsum
    rl = []
    for e in range(EXPERTS_PER_GROUP):
        v = row(ROUTE_OFF + (N_GROUPS - 1) * EXPERTS_PER_GROUP + e)
        for g in range(N_GROUPS - 2, -1, -1):
            v = jnp.where(g_idx == g, row(ROUTE_OFF + g * EXPERTS_PER_GROUP + e), v)
        rl.append(v)
    m1, i1 = _first_max(rl)
    m2, i2 = _first_max([jnp.where(i1 == e, MASK_VALUE, rl[e]) for e in range(EXPERTS_PER_GROUP)])
    ex = jnp.exp(m2 - m1)
    w1 = 1.0 / (1.0 + ex)
    w2 = ex * w1
    e1 = g_idx * EXPERTS_PER_GROUP + i1
    e2 = g_idx * EXPERTS_PER_GROUP + i2

    eid = lax.broadcasted_iota(jnp.int32, (N_EXPERTS, tile), 0)
    hit1 = eid == e1
    hit2 = eid == e2
    oh = jnp.where(jnp.logical_or(hit1, hit2), 1.0, 0.0)
    r = lax.broadcasted_iota(jnp.int32, (tile, tile), 0)
    c = lax.broadcasted_iota(jnp.int32, (tile, tile), 1)
    tri = jnp.where(r <= c, 1.0, 0.0).astype(BF16)
    csum = jnp.dot(oh.astype(BF16), tri, preferred_element_type=F32)
    counts = jnp.broadcast_to(csum[:, tile - 1:tile], (N_EXPERTS, tile)).astype(jnp.int32)
    nch = jnp.right_shift(counts + (CHUNK - 1), CHUNK.bit_length() - 1)
    er = lax.broadcasted_iota(jnp.int32, (N_EXPERTS, N_EXPERTS), 0)
    ec = lax.broadcasted_iota(jnp.int32, (N_EXPERTS, N_EXPERTS), 1)
    lower = jnp.where(ec < er, 1.0, 0.0).astype(BF16)
    off = jnp.dot(lower, nch.astype(F32).astype(BF16), preferred_element_type=F32) * CHUNK
    pos = off + csum - oh
    lp1 = jnp.sum(jnp.where(hit1, pos, 0.0), axis=0, keepdims=True)
    lp2 = jnp.sum(jnp.where(hit2, pos, 0.0), axis=0, keepdims=True)
    zero = jnp.zeros_like(lp1)
    route_t = jnp.concatenate([lp1, lp2, g_p * w1, g_p * w2, zero, zero, zero, zero], axis=0)
    meta = jnp.concatenate([nch[:, 0:LANES], off[:, 0:LANES].astype(jnp.int32)], axis=0)
    return route_t, meta


def _memattn_kernel(x_ref, wq_ref, k_ref, v_ref, wo_ref, g2_ref, b2_ref, wr2_ref, wrh_ref, br_ref,
                    o_ref, ob_ref, route_ref, routet_ref, meta_ref, wq_b, wo_b):
    @pl.when(jnp.logical_and(pl.program_id(0) == 0, pl.program_id(1) == 0))
    def _():
        _cast_bf16(wq_b, wq_ref)
        _cast_bf16(wo_b, wo_ref)

    def rows_logits(rs):
        x = x_ref[0, rs, :]
        q = jnp.dot(x.astype(BF16), wq_b[...], preferred_element_type=F32)
        q = (q * (MEM_HEAD_DIM ** -0.5)).astype(BF16)
        outs = []
        for h in range(MEM_HEADS):
            sl = slice(h * MEM_HEAD_DIM, (h + 1) * MEM_HEAD_DIM)
            s = _dot_nt(q[:, sl], k_ref[0, :, sl])
            m = jnp.max(s, axis=-1, keepdims=True)
            p = jnp.exp(s - m)
            denom = jnp.sum(p, axis=-1, keepdims=True)
            o = jnp.dot(p.astype(BF16), v_ref[0, :, sl], preferred_element_type=F32)
            outs.append((o / denom).astype(BF16))
        o = jnp.dot(jnp.concatenate(outs, axis=-1), wo_b[...], preferred_element_type=F32)
        x2 = _layer_norm(ALPHA * x + o, g2_ref[...], b2_ref[...])
        o_ref[0, rs, :] = x2
        x2h = x2.astype(BF16)
        ob_ref[0, rs, :] = x2h
        x2l = (x2 - x2h.astype(F32)).astype(BF16)
        hh = jnp.dot(x2h, wr2_ref[...], preferred_element_type=F32)
        return (hh[:, 0:LANES] + hh[:, LANES:2 * LANES]
                + jnp.dot(x2l, wrh_ref[...], preferred_element_type=F32) + br_ref[...])

    ts = x_ref.shape[1]
    groups = [slice(r0, r0 + MOE_TILE) for r0 in range(0, ts, MOE_TILE)]
    logits = [rows_logits(rs) for rs in groups]
    pad = jnp.zeros((LANES - SUBLANES, MOE_TILE), F32)
    for k, rs in enumerate(groups):
        route_t, meta = _route_plan(jnp.transpose(logits[k]))
        routet_ref[k] = route_t
        route_ref[0, rs, :] = jnp.transpose(jnp.concatenate([route_t, pad], axis=0))
        meta_ref[k] = meta


def _memattn(x1, wq, kvm, wo, g2, b2, wr, br):
    B, S, D = x1.shape
    ts = MEM_TILE
    per = ts // MOE_TILE
    nt = S // ts
    c = wr * (2.0 ** 16 + 1.0)
    w_high = c - (c - wr)
    wrh = w_high.astype(BF16)
    wr2 = jnp.concatenate([wrh, (wr - w_high).astype(BF16)], axis=1)
    tile = pl.BlockSpec((1, ts, D), lambda b, i: (b, i, 0))
    kspec = pl.BlockSpec((1, MEM_LEN, D), lambda b, i: (b, 0, 0))
    vspec = pl.BlockSpec((1, MEM_LEN, D), lambda b, i: (b, 0, 1))
    return pl.pallas_call(
        _memattn_kernel,
        grid=(B, nt),
        in_specs=[tile, _resident_spec((D, D)), kspec, vspec, _resident_spec((D, D)),
                  _const_spec((1, D)), _const_spec((1, D)),
                  _const_spec((D, 2 * LANES)), _const_spec((D, LANES)), _const_spec((1, LANES))],
        out_specs=[tile, tile, pl.BlockSpec((1, ts, LANES), lambda b, i: (b, i, 0)),
                   pl.BlockSpec((per, SUBLANES, MOE_TILE), lambda b, i: (b * nt + i, 0, 0)),
                   pl.BlockSpec((per, 2 * N_EXPERTS, LANES), lambda b, i: (b * nt + i, 0, 0))],
        out_shape=[jax.ShapeDtypeStruct((B, S, D), F32),
                   jax.ShapeDtypeStruct((B, S, D), BF16),
                   jax.ShapeDtypeStruct((B, S, LANES), F32),
                   jax.ShapeDtypeStruct((B * nt * per, SUBLANES, MOE_TILE), F32),
                   jax.ShapeDtypeStruct((B * nt * per, 2 * N_EXPERTS, LANES), jnp.int32)],
        scratch_shapes=[pltpu.VMEM((D, D), BF16), pltpu.VMEM((D, D), BF16)],
        compiler_params=pltpu.CompilerParams(
            dimension_semantics=("arbitrary", "arbitrary"), vmem_limit_bytes=VMEM_LIMIT),
        name="memattn",
    )(x1, wq, kvm, kvm, wo, g2, b2, wr2, wrh, br)


def _gmm_blocks(n_tokens):
    rows = (n_tokens * TOP_K + (n_tokens // MOE_TILE) * N_EXPERTS * (CHUNK - 1)
            + N_EXPERTS * (GMM_ROWS - CHUNK))
    return -(-rows // GMM_ROWS)


def _plan(meta, T):
    nch = meta[:, :N_EXPERTS, 0]
    n16 = nch * CHUNK
    n_e = jnp.sum(n16, axis=0)
    reg = (n_e + GMM_ROWS - 1) // GMM_ROWS * GMM_ROWS
    gend = jnp.cumsum(reg)
    gbase = gend - reg
    dst = gbase[None, :] + jnp.cumsum(n16, axis=0) - n16
    blk_row = jnp.arange(_gmm_blocks(T), dtype=jnp.int32)[:, None] * GMM_ROWS
    blk_expert = jnp.sum(blk_row >= gend[None, :], axis=1)
    blk_expert = jnp.minimum(blk_expert, N_EXPERTS - 1).astype(jnp.int32)
    n_used = (gend[-1] // GMM_ROWS).astype(jnp.int32).reshape(1)
    data_end = jnp.sum(jnp.where(blk_expert[:, None] == jnp.arange(N_EXPERTS)[None, :],
                                 (gbase + n_e)[None, :], 0), axis=1)
    blk_rows = jnp.where(blk_row[:, 0] < gend[-1], jnp.clip(data_end - blk_row[:, 0], 0, GMM_ROWS), 0)
    src = (jnp.cumsum(nch, axis=1) - nch) * CHUNK
    n_big = nch // BIG

    def copy_list(count, src0, dst0, rows, length):
        cum = jnp.cumsum(count, axis=1)
        first = (cum - count)[:, None, :]
        k = jnp.arange(length, dtype=jnp.int32)[None, :, None]
        mine = (k >= first) & (k < cum[:, None, :])
        step = (k - first) * rows
        pick = lambda base: jnp.sum(jnp.where(mine, base[:, None, :] + step, 0), axis=2)
        return pick(src0), pick(dst0), cum[:, -1]

    big_src, big_dst, big_n = copy_list(n_big, src, dst, BIG * CHUNK, MAX_BIG)
    rest = n_big * (BIG * CHUNK)
    small_src, small_dst, small_n = copy_list(nch - n_big * BIG, src + rest, dst + rest, CHUNK, MAX_SMALL)
    i32 = lambda a: a.astype(jnp.int32)
    fill_start = jnp.concatenate([gbase + n_e, gend[-1:]])
    fill_n = jnp.concatenate([reg - n_e, _gmm_blocks(T) * GMM_ROWS - gend[-1:]]) // CHUNK
    return dict(big_src=i32(big_src).reshape(-1), big_dst=i32(big_dst).reshape(-1), big_n=i32(big_n),
                small_src=i32(small_src).reshape(-1), small_dst=i32(small_dst).reshape(-1), small_n=i32(small_n),
                fill_start=i32(fill_start), fill_n=i32(fill_n),
                fill_tot=i32(jnp.sum(fill_n)).reshape(1), blk_expert=blk_expert, n_used=n_used,
                blk_rows=i32(blk_rows))


def _rows_copy(src_ref, src_row, dst_ref, dst_row, rows, sem):
    return pltpu.make_async_copy(
        src_ref.at[pl.ds(pl.multiple_of(src_row, CHUNK), rows), :],
        dst_ref.at[pl.ds(pl.multiple_of(dst_row, CHUNK), rows), :], sem)


def _chunk_copy(src_ref, src_row, dst_ref, dst_row, sem):
    return _rows_copy(src_ref, src_row, dst_ref, dst_row, CHUNK, sem)


def _issue_copies(lists, tile, tile_ref, tile_is_src, hbm_ref, sem):
    big_src, big_dst, big_n, small_src, small_dst, small_n = lists
    for src_l, dst_l, n_l, length, rows in ((big_src, big_dst, big_n, MAX_BIG, BIG * CHUNK),
                                            (small_src, small_dst, small_n, MAX_SMALL, CHUNK)):
        def issue(k, carry, src_l=src_l, dst_l=dst_l, length=length, rows=rows):
            local, remote = src_l[tile * length + k], dst_l[tile * length + k]
            if tile_is_src:
                _rows_copy(tile_ref, local, hbm_ref, remote, rows, sem).start()
            else:
                _rows_copy(hbm_ref, remote, tile_ref, local, rows, sem).start()
            return carry

        lax.fori_loop(0, n_l[tile], issue, 0)


def _wait_rows(n, rows, src_ref, dst_ref, sem):
    def body(c, carry):
        pltpu.make_async_copy(src_ref.at[pl.ds(0, rows), :], dst_ref.at[pl.ds(0, rows), :], sem).wait()
        return carry

    lax.fori_loop(0, n, body, 0)


def _wait_copies(lists, tile, src_ref, dst_ref, sem, extra_chunks=0):
    _wait_rows(lists[2][tile], BIG * CHUNK, src_ref, dst_ref, sem)
    n = lists[5][tile] + extra_chunks
    _wait_rows(n // WAIT_GROUP, WAIT_GROUP * CHUNK, src_ref, dst_ref, sem)
    _wait_rows(n % WAIT_GROUP, CHUNK, src_ref, dst_ref, sem)


def _dispatch_kernel(bs_ref, bd_ref, bn_ref, ss_ref, sd_ref, sn_ref, fstart_ref, fn_ref, ftot_ref,
                     lp_ref, x_ref, xs_hbm, xt, zbuf, sems):
    lists = (bs_ref, bd_ref, bn_ref, ss_ref, sd_ref, sn_ref)
    s = pl.program_id(0)
    ns = pl.num_programs(0)
    cur = (s % 2) * PER_STEP
    prv = PER_STEP - cur

    @pl.when(s >= 2)
    def _():
        for k in range(PER_STEP):
            _wait_copies(lists, (s - 2) * PER_STEP + k, xt.at[cur + k], xs_hbm, sems.at[cur + k])

    r = lax.broadcasted_iota(jnp.int32, (SORT_ROWS, MOE_TILE), 0)
    for k in range(PER_STEP):
        lp = lp_ref[k].astype(jnp.int32)
        hit = jnp.logical_or(lp[0:1, :] == r, lp[1:2, :] == r)
        p = jnp.where(hit, 1.0, 0.0).astype(BF16)
        x = x_ref[k * MOE_TILE:(k + 1) * MOE_TILE, :]
        xt[cur + k] = jnp.dot(p, x, preferred_element_type=F32).astype(BF16)

    for k in range(PER_STEP):
        _issue_copies(lists, s * PER_STEP + k, xt.at[cur + k], True, xs_hbm, sems.at[cur + k])

    @pl.when(s == ns - 1)
    def _():
        zbuf[...] = jnp.zeros_like(zbuf)
        sem = sems.at[cur]

        def per_range(e, carry):
            def issue(c, carry2):
                _chunk_copy(zbuf, 0, xs_hbm, fstart_ref[e] + c * CHUNK, sem).start()
                return carry2

            return lax.fori_loop(0, fn_ref[e], issue, carry)

        lax.fori_loop(0, N_EXPERTS + 1, per_range, 0)
        for k in range(PER_STEP):
            _wait_copies(lists, (s - 1) * PER_STEP + k, xt.at[prv + k], xs_hbm, sems.at[prv + k])
            _wait_copies(lists, s * PER_STEP + k, xt.at[cur + k], xs_hbm, sems.at[cur + k],
                         extra_chunks=ftot_ref[0] if k == 0 else 0)


def _copy_lists(plan):
    return tuple(plan[k] for k in ("big_src", "big_dst", "big_n", "small_src", "small_dst", "small_n"))


def _dispatch(plan, lpt, x2b):
    T, D = x2b.shape
    rows = PER_STEP * MOE_TILE
    assert T // rows >= 2
    grid_spec = pltpu.PrefetchScalarGridSpec(
        num_scalar_prefetch=9,
        grid=(T // rows,),
        in_specs=[pl.BlockSpec((PER_STEP, SUBLANES, MOE_TILE), lambda t, *_: (t, 0, 0)),
                  pl.BlockSpec((rows, D), lambda t, *_: (t, 0))],
        out_specs=pl.BlockSpec(memory_space=pl.ANY),
        scratch_shapes=[pltpu.VMEM((2 * PER_STEP, SORT_ROWS, D), BF16), pltpu.VMEM((CHUNK, D), BF16),
                        pltpu.SemaphoreType.DMA((2 * PER_STEP,))],
    )
    return pl.pallas_call(
        _dispatch_kernel,
        grid_spec=grid_spec,
        out_shape=jax.ShapeDtypeStruct((_gmm_blocks(T) * GMM_ROWS, D), BF16),
        compiler_params=pltpu.CompilerParams(
            dimension_semantics=("arbitrary",), vmem_limit_bytes=VMEM_LIMIT),
        name="moe_dispatch",
    )(*_copy_lists(plan), plan["fill_start"], plan["fill_n"], plan["fill_tot"], lpt, x2b)


def _expert_ffn(xb, wg_b, wu_b, wd_b):
    g = jnp.dot(xb, wg_b[...], preferred_element_type=F32)
    u = jnp.dot(xb, wu_b[...], preferred_element_type=F32)
    h = (g * jax.nn.sigmoid(g)) * u
    return jnp.dot(h.astype(BF16), wd_b[...], preferred_element_type=F32).astype(BF16)


def _gmm_kernel(be_ref, nu_ref, br_ref, x_ref, wg_ref, wu_ref, wd_ref, o_ref, wg_b, wu_b, wd_b):
    b = pl.program_id(0)
    rows = br_ref[b]
    half = GMM_ROWS // 2

    @pl.when(jnp.logical_or(b == 0, be_ref[b] != be_ref[jnp.maximum(b - 1, 0)]))
    def _():
        _cast_bf16(wg_b, wg_ref.at[0])
        _cast_bf16(wu_b, wu_ref.at[0])
        _cast_bf16(wd_b, wd_ref.at[0])

    @pl.when(rows > half)
    def _():
        o_ref[...] = _expert_ffn(x_ref[...], wg_b, wu_b, wd_b)

    @pl.when(jnp.logical_and(rows > 0, rows <= half))
    def _():
        o_ref[0:half, :] = _expert_ffn(x_ref[0:half, :], wg_b, wu_b, wd_b)
        o_ref[half:GMM_ROWS, :] = jnp.zeros((GMM_ROWS - half, o_ref.shape[1]), o_ref.dtype)

    @pl.when(rows == 0)
    def _():
        o_ref[...] = jnp.zeros_like(o_ref)


def _gmm(plan, xs, wg, wu, wd):
    R, D = xs.shape
    rows = pl.BlockSpec((GMM_ROWS, D), lambda b, be, nu, br: (jnp.minimum(b, nu[0] - 1), 0))
    grid_spec = pltpu.PrefetchScalarGridSpec(
        num_scalar_prefetch=3,
        grid=(R // GMM_ROWS,),
        in_specs=[rows,
                  pl.BlockSpec((1, D, D_EXPERT), lambda b, be, nu, br: (be[b], 0, 0)),
                  pl.BlockSpec((1, D, D_EXPERT), lambda b, be, nu, br: (be[b], 0, 0)),
                  pl.BlockSpec((1, D_EXPERT, D), lambda b, be, nu, br: (be[b], 0, 0))],
        out_specs=pl.BlockSpec((GMM_ROWS, D), lambda b, be, nu, br: (b, 0)),
        scratch_shapes=[pltpu.VMEM((D, D_EXPERT), BF16), pltpu.VMEM((D, D_EXPERT), BF16),
                        pltpu.VMEM((D_EXPERT, D), BF16)],
    )
    return pl.pallas_call(
        _gmm_kernel,
        grid_spec=grid_spec,
        out_shape=jax.ShapeDtypeStruct((R, D), BF16),
        compiler_params=pltpu.CompilerParams(
            dimension_semantics=("arbitrary",), vmem_limit_bytes=VMEM_LIMIT),
        name="moe_gmm",
    )(plan["blk_expert"], plan["n_used"], plan["blk_rows"], xs, wg, wu, wd)


def _combine_kernel(bs_ref, bd_ref, bn_ref, ss_ref, sd_ref, sn_ref, cm_ref, x_ref, ys_hbm, g3_ref, b3_ref,
                    o_ref, yt, sems):
    lists = (bs_ref, bd_ref, bn_ref, ss_ref, sd_ref, sn_ref)
    s = pl.program_id(0)
    ns = pl.num_programs(0)
    cur = (s % 2) * PER_STEP
    nxt = PER_STEP - cur

    def fetch(step, first_buf):
        for k in range(PER_STEP):
            _issue_copies(lists, step * PER_STEP + k, yt.at[first_buf + k], False, ys_hbm,
                          sems.at[first_buf + k])

    @pl.when(s == 0)
    def _():
        yt[...] = jnp.zeros_like(yt)
        fetch(0, 0)

    @pl.when(s + 1 < ns)
    def _():
        fetch(s + 1, nxt)

    for k in range(PER_STEP):
        _wait_copies(lists, s * PER_STEP + k, ys_hbm, yt.at[cur + k], sems.at[cur + k])

    col = lax.broadcasted_iota(jnp.int32, (MOE_TILE, SORT_ROWS), 1)
    for k in range(PER_STEP):
        rs = slice(k * MOE_TILE, (k + 1) * MOE_TILE)
        cm = cm_ref[rs, :]
        lp0 = cm[:, 0:1].astype(jnp.int32)
        lp1 = cm[:, 1:2].astype(jnp.int32)
        w = jnp.where(col == lp0, cm[:, 2:3], jnp.where(col == lp1, cm[:, 3:4], 0.0)).astype(BF16)
        y = jnp.dot(w, yt[cur + k], preferred_element_type=F32)
        o_ref[rs, :] = _layer_norm(ALPHA * x_ref[rs, :] + y, g3_ref[...], b3_ref[...])


def _combine(plan, route, x2, ys, g3, b3):
    T, D = x2.shape
    rows = PER_STEP * MOE_TILE
    grid_spec = pltpu.PrefetchScalarGridSpec(
        num_scalar_prefetch=6,
        grid=(T // rows,),
        in_specs=[pl.BlockSpec((rows, LANES), lambda t, *_: (t, 0)),
                  pl.BlockSpec((rows, D), lambda t, *_: (t, 0)),
                  pl.BlockSpec(memory_space=pl.ANY),
                  pl.BlockSpec((1, D), lambda t, *_: (0, 0)),
                  pl.BlockSpec((1, D), lambda t, *_: (0, 0))],
        out_specs=pl.BlockSpec((rows, D), lambda t, *_: (t, 0)),
        scratch_shapes=[pltpu.VMEM((2 * PER_STEP, SORT_ROWS, D), BF16),
                        pltpu.SemaphoreType.DMA((2 * PER_STEP,))],
    )
    return pl.pallas_call(
        _combine_kernel,
        grid_spec=grid_spec,
        out_shape=jax.ShapeDtypeStruct((T, D), F32),
        compiler_params=pltpu.CompilerParams(
            dimension_semantics=("arbitrary",), vmem_limit_bytes=VMEM_LIMIT),
        name="moe_combine",
    )(*_copy_lists(plan), route, x2, ys, g3, b3)


def _moe(x2, x2b, route, lpt, meta, wg, wu, wd, g3, b3):
    plan = _plan(meta, x2.shape[0])
    xs = _dispatch(plan, lpt, x2b)
    ys = _gmm(plan, xs, wg, wu, wd)
    return _combine(plan, route, x2, ys, g3, b3)


def _row(v):
    return v.reshape(1, -1).astype(F32)


def kernel(x, mem, w_in, b_in, w_dw, b_dw, g_conv_norm, b_conv_norm, attn_sinks, w_out, g_ln1, b_ln1,
           w_mq, w_mkv, w_mo, g_ln2, b_ln2, w_group, b_group, w_router, b_router, w_gate, w_up, w_down,
           g_ln3, b_ln3):
    B, S, D = x.shape
    for l in range(DEPTH):
        w_dw_p = jnp.zeros((CONV_HALO, CONV_CH), F32).at[:CONV_WIDTH].set(w_dw[l])
        x1 = _mixer(x, attn_sinks[l].astype(F32), w_in[l], _row(b_in[l]), w_dw_p,
                    _row(b_dw[l]), _row(g_conv_norm[l]), _row(b_conv_norm[l]),
                    w_out[l], _row(g_ln1[l]), _row(b_ln1[l]))

        kvm = _memkv(mem.reshape(B * MEM_LEN, D), w_mkv[l]).reshape(B, MEM_LEN, 2 * D)

        wr = jnp.concatenate(
            [w_group[l], jnp.transpose(w_router[l], (1, 0, 2)).reshape(D, N_EXPERTS)], axis=1)
        wr = jnp.pad(wr, ((0, 0), (0, LANES - wr.shape[1])))
        br = jnp.pad(jnp.concatenate([b_group[l], b_router[l].reshape(-1)]), (0, LANES - N_GROUPS - N_EXPERTS))
        x2, x2b, route, lpt, meta = _memattn(x1, w_mq[l], kvm, w_mo[l],
                                 _row(g_ln2[l]), _row(b_ln2[l]), wr.astype(F32), _row(br))

        T = B * S
        y = _moe(x2.reshape(T, D), x2b.reshape(T, D), route.reshape(T, LANES), lpt, meta,
                 w_gate[l].astype(BF16), w_up[l].astype(BF16), w_down[l].astype(BF16),
                 _row(g_ln3[l]), _row(b_ln3[l]))
        x = y.reshape(B, S, D)
    return x
```

```python
import functools

import jax
import jax.numpy as jnp
from jax import lax
from jax.experimental import pallas as pl
from jax.experimental.pallas import tpu as pltpu

D_MODEL = 1024
MEM_LEN = 256
CONV_CH = 512
CONV_WIDTH = 31
N_HEADS = 8
N_KV_HEADS = 2
HEAD_DIM = 64
GQ = N_HEADS // N_KV_HEADS
ATTN_W = N_HEADS * HEAD_DIM
KV_W = N_KV_HEADS * HEAD_DIM
WINDOW = 128
D_MIX = CONV_CH + ATTN_W
D_IN = 2 * CONV_CH + ATTN_W + 2 * KV_W
MEM_HEADS = 4
MEM_HEAD_DIM = D_MODEL // MEM_HEADS
N_GROUPS = 4
EXPERTS_PER_GROUP = 4
N_EXPERTS = N_GROUPS * EXPERTS_PER_GROUP
D_EXPERT = D_MODEL // 2
DEPTH = 1
ALPHA = (2.0 * DEPTH) ** 0.25
LN_EPS = 1e-5

LANES = 128
SUBLANES = 8
CONV_ROWS = 128
LN_ROWS = 64
MEM_TILE = 1024
MASK_VALUE = -1e30
CONV_HALO = 32
SEQ_TILE = 512
MOE_TILE = 512
CHUNK = 16
REGION_ALIGN = 512
GMM_ROWS = 2 * REGION_ALIGN
TOP_K = 2
SORT_ROWS = -(-(MOE_TILE * TOP_K + N_EXPERTS * (CHUNK - 1)) // 256) * 256
BIG = 4
MAX_BIG = SORT_ROWS // (BIG * CHUNK)
MAX_SMALL = N_EXPERTS * (BIG - 1)
PER_STEP = 2
WAIT_GROUP = 8
ROUTE_OFF = N_GROUPS
VMEM_LIMIT = 56 * 1024 * 1024

BF16 = jnp.bfloat16
F32 = jnp.float32


def _layer_norm(x, g, b):
    mu = jnp.mean(x, axis=-1, keepdims=True)
    xc = x - mu
    var = jnp.mean(xc * xc, axis=-1, keepdims=True)
    return xc * lax.rsqrt(var + LN_EPS) * g + b


def _cast_bf16(dst_ref, src_ref):
    rows = 256
    for r0 in range(0, src_ref.shape[0], rows):
        dst_ref[r0:r0 + rows, :] = src_ref[r0:r0 + rows, :].astype(BF16)


def _dot_nt(a, b):
    return lax.dot_general(a, b, (((1,), (1,)), ((), ())), preferred_element_type=F32)


def _mixer_kernel(sinks_ref, x_ref, w_in_ref, b_in_ref, w_dw_ref, b_dw_ref, g_cn_ref, b_cn_ref,
                  w_out_ref, g1_ref, b1_ref, o_ref, w_in_b, w_out_b, hbuf, hshift, cbuf, qbuf, kbuf, vbuf, ymix):
    i = pl.program_id(1)
    ts = SEQ_TILE

    @pl.when(jnp.logical_and(pl.program_id(0) == 0, i == 0))
    def _():
        _cast_bf16(w_in_b, w_in_ref)
        _cast_bf16(w_out_b, w_out_ref)

    @pl.when(i == 0)
    def _():
        hbuf[0:CONV_HALO, :] = jnp.zeros((CONV_HALO, CONV_CH), F32)
        kbuf[:, 0:WINDOW, :] = jnp.zeros((2 * N_KV_HEADS, WINDOW, KV_W), BF16)
        vbuf[:, 0:WINDOW, :] = jnp.zeros((2, WINDOW, KV_W), BF16)

    x = x_ref[0]
    u = jnp.dot(x.astype(BF16), w_in_b[...], preferred_element_type=F32) + b_in_ref[...]
    a = u[:, 0:CONV_CH]
    gate = u[:, CONV_CH:2 * CONV_CH]
    hbuf[CONV_HALO:CONV_HALO + ts, :] = a * jax.nn.sigmoid(gate)
    qbuf[...] = (u[:, 2 * CONV_CH:2 * CONV_CH + ATTN_W] * (HEAD_DIM ** -0.5)).astype(BF16)
    kf = u[:, 2 * CONV_CH + ATTN_W:2 * CONV_CH + ATTN_W + KV_W]
    vf = u[:, 2 * CONV_CH + ATTN_W + KV_W:D_IN]
    kr = pltpu.roll(kf, HEAD_DIM, axis=1)
    vr = pltpu.roll(vf, HEAD_DIM, axis=1)
    lo = lax.broadcasted_iota(jnp.int32, (ts, KV_W), 1) < HEAD_DIM
    rows = slice(WINDOW, WINDOW + ts)
    kbuf[0, rows, :] = jnp.where(lo, kf, 0.0).astype(BF16)
    kbuf[1, rows, :] = jnp.where(lo, 0.0, kr).astype(BF16)
    kbuf[2, rows, :] = jnp.where(lo, kr, 0.0).astype(BF16)
    kbuf[3, rows, :] = jnp.where(lo, 0.0, kf).astype(BF16)
    vbuf[0, rows, :] = vf.astype(BF16)
    vbuf[1, rows, :] = vr.astype(BF16)

    base = CONV_HALO - (CONV_WIDTH - 1)
    n_shift = ts + CONV_HALO - SUBLANES
    for b in range(1, SUBLANES):
        hshift[b - 1, 0:n_shift, :] = hbuf[b:b + n_shift, :]
    rc = CONV_ROWS

    def conv_chunk(c):
        r0 = c * rc
        for l in range(CONV_CH // LANES):
            ls = slice(l * LANES, (l + 1) * LANES)
            acc = jnp.zeros((rc, LANES), F32)
            for j in range(CONV_WIDTH):
                a8, b = divmod(j + base, SUBLANES)
                rs = slice(r0 + SUBLANES * a8, r0 + SUBLANES * a8 + rc)
                tap = hbuf[rs, ls] if b == 0 else hshift[b - 1, rs, ls]
                acc = acc + tap * w_dw_ref[j:j + 1, ls]
            cbuf[r0:r0 + rc, ls] = acc
        for r1 in range(r0, r0 + rc, LN_ROWS):
            rs = slice(r1, r1 + LN_ROWS)
            y = _layer_norm(cbuf[rs, :] + b_dw_ref[...], g_cn_ref[...], b_cn_ref[...])
            y = y * jax.nn.sigmoid(y)
            ymix[rs, 0:CONV_CH] = y.astype(BF16)

    qi = lax.broadcasted_iota(jnp.int32, (2 * WINDOW, 2 * WINDOW), 0) % WINDOW
    kj = lax.broadcasted_iota(jnp.int32, (2 * WINDOW, 2 * WINDOW), 1)
    dist = qi + WINDOW - kj
    band = (dist >= 0) & (dist < WINDOW)
    top = lax.broadcasted_iota(jnp.int32, (2 * WINDOW, 1), 0) < WINDOW
    lo_out = lax.broadcasted_iota(jnp.int32, (WINDOW, 2 * HEAD_DIM), 1) < HEAD_DIM

    def attn_block(jb):
        r0 = jb * WINDOW
        valid = band & jnp.logical_or(i != 0, kj >= WINDOW) if jb == 0 else band
        for kvh in range(N_KV_HEADS):
            h0 = kvh * GQ
            c0 = h0 * HEAD_DIM
            qs = jnp.concatenate([qbuf[r0:r0 + WINDOW, c0:c0 + 2 * HEAD_DIM],
                                  qbuf[r0:r0 + WINDOW, c0 + 2 * HEAD_DIM:c0 + 4 * HEAD_DIM]], axis=0)
            pv = []
            for par in range(2):
                kk = kbuf[2 * kvh + par, r0:r0 + 2 * WINDOW, :]
                vv = vbuf[(kvh + par) % 2, r0:r0 + 2 * WINDOW, :]
                s = jnp.where(valid, _dot_nt(qs, kk), MASK_VALUE)
                sink = jnp.where(top, sinks_ref[h0 + par], sinks_ref[h0 + 2 + par])
                m = jnp.maximum(jnp.max(s, axis=-1, keepdims=True), sink)
                p = jnp.exp(s - m)
                denom = jnp.sum(p, axis=-1, keepdims=True) + jnp.exp(sink - m)
                pv.append(jnp.dot(p.astype(BF16), vv, preferred_element_type=F32) / denom)
            for pair in range(2):
                rs = slice(pair * WINDOW, (pair + 1) * WINDOW)
                o = jnp.where(lo_out, pv[0][rs], pv[1][rs])
                cs = CONV_CH + c0 + pair * 2 * HEAD_DIM
                ymix[r0:r0 + WINDOW, cs:cs + 2 * HEAD_DIM] = o.astype(BF16)

    assert ts // rc == ts // WINDOW
    for c in range(ts // rc):
        attn_block(c)
        conv_chunk(c)

    mix = jnp.dot(ymix[...], w_out_b[...], preferred_element_type=F32)
    o_ref[0] = _layer_norm(ALPHA * x + mix, g1_ref[...], b1_ref[...])

    hbuf[0:CONV_HALO, :] = hbuf[ts:ts + CONV_HALO, :]
    kbuf[:, 0:WINDOW, :] = kbuf[:, ts:ts + WINDOW, :]
    vbuf[:, 0:WINDOW, :] = vbuf[:, ts:ts + WINDOW, :]


def _const_spec(shape):
    nd = len(shape)
    return pl.BlockSpec(shape, lambda *_: (0,) * nd)


def _resident_spec(shape):
    nd = len(shape)
    return pl.BlockSpec(shape, lambda *_: (0,) * nd, pipeline_mode=pl.Buffered(1))


def _mixer(x, sinks, w_in, b_in, w_dw, b_dw, g_cn, b_cn, w_out, g1, b1):
    B, S, D = x.shape
    ts = SEQ_TILE
    tile = pl.BlockSpec((1, ts, D), lambda b, i: (b, i, 0))
    return pl.pallas_call(
        _mixer_kernel,
        grid=(B, S // ts),
        in_specs=[
            pl.BlockSpec(memory_space=pltpu.SMEM),
            tile,
            _resident_spec((D, D_IN)), _const_spec((1, D_IN)),
            _const_spec((CONV_HALO, CONV_CH)), _const_spec((1, CONV_CH)),
            _const_spec((1, CONV_CH)), _const_spec((1, CONV_CH)),
            _resident_spec((D_MIX, D)), _const_spec((1, D)), _const_spec((1, D)),
        ],
        out_specs=tile,
        out_shape=jax.ShapeDtypeStruct((B, S, D), F32),
        scratch_shapes=[
            pltpu.VMEM((D, D_IN), BF16),
            pltpu.VMEM((D_MIX, D), BF16),
            pltpu.VMEM((CONV_HALO + ts, CONV_CH), F32),
            pltpu.VMEM((SUBLANES - 1, CONV_HALO + ts, CONV_CH), F32),
            pltpu.VMEM((ts, CONV_CH), F32),
            pltpu.VMEM((ts, ATTN_W), BF16),
            pltpu.VMEM((2 * N_KV_HEADS, WINDOW + ts, KV_W), BF16),
            pltpu.VMEM((2, WINDOW + ts, KV_W), BF16),
            pltpu.VMEM((ts, D_MIX), BF16),
        ],
        compiler_params=pltpu.CompilerParams(
            dimension_semantics=("arbitrary", "arbitrary"), vmem_limit_bytes=VMEM_LIMIT),
        name="mixer",
    )(sinks, x, w_in, b_in, w_dw, b_dw, g_cn, b_cn, w_out, g1, b1)


def _memkv_kernel(mem_ref, w_ref, o_ref):
    o_ref[...] = jnp.dot(mem_ref[...].astype(BF16), w_ref[...].astype(BF16),
                         preferred_element_type=F32).astype(BF16)


def _memkv(mem2d, w_mkv):
    M, D = mem2d.shape
    N = w_mkv.shape[1]
    tn = 512
    return pl.pallas_call(
        _memkv_kernel,
        grid=(N // tn,),
        in_specs=[pl.BlockSpec((M, D), lambda j: (0, 0)), pl.BlockSpec((D, tn), lambda j: (0, j))],
        out_specs=pl.BlockSpec((M, tn), lambda j: (0, j)),
        out_shape=jax.ShapeDtypeStruct((M, N), BF16),
        compiler_params=pltpu.CompilerParams(dimension_semantics=("arbitrary",)),
        name="memkv",
    )(mem2d, w_mkv)


def _first_max(rows):
    best = rows[0]
    for r in rows[1:]:
        best = jnp.maximum(best, r)
    idx = jnp.full(best.shape, len(rows) - 1, jnp.int32)
    for k in range(len(rows) - 2, -1, -1):
        idx = jnp.where(rows[k] == best, k, idx)
    return best, idx


def _route_plan(logits_t):
    tile = logits_t.shape[1]
    row = lambda k: logits_t[k:k + 1, :]
    gmax, g_idx = _first_max([row(g) for g in range(N_GROUPS)])
    gsum = jnp.exp(row(0) - gmax)
    for g in range(1, N_GROUPS):
        gsum = gsum + jnp.exp(row(g) - gmax)
    g_p = 1.0 / gsum
    rl = []
    for e in range(EXPERTS_PER_GROUP):
        v = row(ROUTE_OFF + (N_GROUPS - 1) * EXPERTS_PER_GROUP + e)
        for g in range(N_GROUPS - 2, -1, -1):
            v = jnp.where(g_idx == g, row(ROUTE_OFF + g * EXPERTS_PER_GROUP + e), v)
        rl.append(v)
    m1, i1 = _first_max(rl)
    m2, i2 = _first_max([jnp.where(i1 == e, MASK_VALUE, rl[e]) for e in range(EXPERTS_PER_GROUP)])
    ex = jnp.exp(m2 - m1)
    w1 = 1.0 / (1.0 + ex)
    w2 = ex * w1
    e1 = g_idx * EXPERTS_PER_GROUP + i1
    e2 = g_idx * EXPERTS_PER_GROUP + i2

    eid = lax.broadcasted_iota(jnp.int32, (N_EXPERTS, tile), 0)
    hit1 = eid == e1
    hit2 = eid == e2
    oh = jnp.where(jnp.logical_or(hit1, hit2), 1.0, 0.0)
    r = lax.broadcasted_iota(jnp.int32, (tile, tile), 0)
    c = lax.broadcasted_iota(jnp.int32, (tile, tile), 1)
    tri = jnp.where(r <= c, 1.0, 0.0).astype(BF16)
    csum = jnp.dot(oh.astype(BF16), tri, preferred_element_type=F32)
    counts = jnp.broadcast_to(csum[:, tile - 1:tile], (N_EXPERTS, tile)).astype(jnp.int32)
    nch = jnp.right_shift(counts + (CHUNK - 1), CHUNK.bit_length() - 1)
    er = lax.broadcasted_iota(jnp.int32, (N_EXPERTS, N_EXPERTS), 0)
    ec = lax.broadcasted_iota(jnp.int32, (N_EXPERTS, N_EXPERTS), 1)
    lower = jnp.where(ec < er, 1.0, 0.0).astype(BF16)
    off = jnp.dot(lower, nch.astype(F32).astype(BF16), preferred_element_type=F32) * CHUNK
    pos = off + csum - oh
    lp1 = jnp.sum(jnp.where(hit1, pos, 0.0), axis=0, keepdims=True)
    lp2 = jnp.sum(jnp.where(hit2, pos, 0.0), axis=0, keepdims=True)
    zero = jnp.zeros_like(lp1)
    route_t = jnp.concatenate([lp1, lp2, g_p * w1, g_p * w2, zero, zero, zero, zero], axis=0)
    meta = jnp.concatenate([nch[:, 0:LANES], off[:, 0:LANES].astype(jnp.int32)], axis=0)
    return route_t, meta


def _memattn_kernel(x_ref, wq_ref, k_ref, v_ref, wo_ref, g2_ref, b2_ref, wr2_ref, wrh_ref, br_ref,
                    o_ref, ob_ref, route_ref, routet_ref, meta_ref, wq_b, wo_b):
    @pl.when(jnp.logical_and(pl.program_id(0) == 0, pl.program_id(1) == 0))
    def _():
        _cast_bf16(wq_b, wq_ref)
        _cast_bf16(wo_b, wo_ref)

    def rows_logits(rs):
        x = x_ref[0, rs, :]
        q = jnp.dot(x.astype(BF16), wq_b[...], preferred_element_type=F32)
        q = (q * (MEM_HEAD_DIM ** -0.5)).astype(BF16)
        outs = []
        for h in range(MEM_HEADS):
            sl = slice(h * MEM_HEAD_DIM, (h + 1) * MEM_HEAD_DIM)
            s = _dot_nt(q[:, sl], k_ref[0, :, sl])
            m = jnp.max(s, axis=-1, keepdims=True)
            p = jnp.exp(s - m)
            denom = jnp.sum(p, axis=-1, keepdims=True)
            o = jnp.dot(p.astype(BF16), v_ref[0, :, sl], preferred_element_type=F32)
            outs.append((o / denom).astype(BF16))
        o = jnp.dot(jnp.concatenate(outs, axis=-1), wo_b[...], preferred_element_type=F32)
        x2 = _layer_norm(ALPHA * x + o, g2_ref[...], b2_ref[...])
        o_ref[0, rs, :] = x2
        x2h = x2.astype(BF16)
        ob_ref[0, rs, :] = x2h
        x2l = (x2 - x2h.astype(F32)).astype(BF16)
        hh = jnp.dot(x2h, wr2_ref[...], preferred_element_type=F32)
        return (hh[:, 0:LANES] + hh[:, LANES:2 * LANES]
                + jnp.dot(x2l, wrh_ref[...], preferred_element_type=F32) + br_ref[...])

    ts = x_ref.shape[1]
    groups = [slice(r0, r0 + MOE_TILE) for r0 in range(0, ts, MOE_TILE)]
    logits = [rows_logits(rs) for rs in groups]
    pad = jnp.zeros((LANES - SUBLANES, MOE_TILE), F32)
    for k, rs in enumerate(groups):
        route_t, meta = _route_plan(jnp.transpose(logits[k]))
        routet_ref[k] = route_t
        route_ref[0, rs, :] = jnp.transpose(jnp.concatenate([route_t, pad], axis=0))
        meta_ref[k] = meta


def _memattn(x1, wq, kvm, wo, g2, b2, wr, br):
    B, S, D = x1.shape
    ts = MEM_TILE
    per = ts // MOE_TILE
    nt = S // ts
    c = wr * (2.0 ** 16 + 1.0)
    w_high = c - (c - wr)
    wrh = w_high.astype(BF16)
    wr2 = jnp.concatenate([wrh, (wr - w_high).astype(BF16)], axis=1)
    tile = pl.BlockSpec((1, ts, D), lambda b, i: (b, i, 0))
    kspec = pl.BlockSpec((1, MEM_LEN, D), lambda b, i: (b, 0, 0))
    vspec = pl.BlockSpec((1, MEM_LEN, D), lambda b, i: (b, 0, 1))
    return pl.pallas_call(
        _memattn_kernel,
        grid=(B, nt),
        in_specs=[tile, _resident_spec((D, D)), kspec, vspec, _resident_spec((D, D)),
                  _const_spec((1, D)), _const_spec((1, D)),
                  _const_spec((D, 2 * LANES)), _const_spec((D, LANES)), _const_spec((1, LANES))],
        out_specs=[tile, tile, pl.BlockSpec((1, ts, LANES), lambda b, i: (b, i, 0)),
                   pl.BlockSpec((per, SUBLANES, MOE_TILE), lambda b, i: (b * nt + i, 0, 0)),
                   pl.BlockSpec((per, 2 * N_EXPERTS, LANES), lambda b, i: (b * nt + i, 0, 0))],
        out_shape=[jax.ShapeDtypeStruct((B, S, D), F32),
                   jax.ShapeDtypeStruct((B, S, D), BF16),
                   jax.ShapeDtypeStruct((B, S, LANES), F32),
                   jax.ShapeDtypeStruct((B * nt * per, SUBLANES, MOE_TILE), F32),
                   jax.ShapeDtypeStruct((B * nt * per, 2 * N_EXPERTS, LANES), jnp.int32)],
        scratch_shapes=[pltpu.VMEM((D, D), BF16), pltpu.VMEM((D, D), BF16)],
        compiler_params=pltpu.CompilerParams(
            dimension_semantics=("arbitrary", "arbitrary"), vmem_limit_bytes=VMEM_LIMIT),
        name="memattn",
    )(x1, wq, kvm, kvm, wo, g2, b2, wr2, wrh, br)


def _gmm_blocks(n_tokens):
    rows = (n_tokens * TOP_K + (n_tokens // MOE_TILE) * N_EXPERTS * (CHUNK - 1)
            + N_EXPERTS * (REGION_ALIGN - CHUNK))
    return -(-rows // GMM_ROWS)


def _plan(meta, T):
    nch = meta[:, :N_EXPERTS, 0]
    n16 = nch * CHUNK
    n_e = jnp.sum(n16, axis=0)
    reg = (n_e + REGION_ALIGN - 1) // REGION_ALIGN * REGION_ALIGN
    gend = jnp.cumsum(reg)
    gbase = gend - reg
    dst = gbase[None, :] + jnp.cumsum(n16, axis=0) - n16
    half_row = jnp.arange(2 * _gmm_blocks(T), dtype=jnp.int32) * REGION_ALIGN
    half_expert = jnp.minimum(jnp.sum(half_row[:, None] >= gend[None, :], axis=1), N_EXPERTS - 1)
    half_used = half_row < gend[-1]
    n_used = ((gend[-1] + GMM_ROWS - 1) // GMM_ROWS).astype(jnp.int32).reshape(1)
    src = (jnp.cumsum(nch, axis=1) - nch) * CHUNK
    n_big = nch // BIG

    def copy_list(count, src0, dst0, rows, length):
        cum = jnp.cumsum(count, axis=1)
        first = (cum - count)[:, None, :]
        k = jnp.arange(length, dtype=jnp.int32)[None, :, None]
        mine = (k >= first) & (k < cum[:, None, :])
        step = (k - first) * rows
        pick = lambda base: jnp.sum(jnp.where(mine, base[:, None, :] + step, 0), axis=2)
        return pick(src0), pick(dst0), cum[:, -1]

    big_src, big_dst, big_n = copy_list(n_big, src, dst, BIG * CHUNK, MAX_BIG)
    rest = n_big * (BIG * CHUNK)
    small_src, small_dst, small_n = copy_list(nch - n_big * BIG, src + rest, dst + rest, CHUNK, MAX_SMALL)
    i32 = lambda a: a.astype(jnp.int32)
    fill_start = jnp.concatenate([gbase + n_e, gend[-1:]])
    fill_n = jnp.concatenate([reg - n_e, _gmm_blocks(T) * GMM_ROWS - gend[-1:]]) // CHUNK
    return dict(big_src=i32(big_src).reshape(-1), big_dst=i32(big_dst).reshape(-1), big_n=i32(big_n),
                small_src=i32(small_src).reshape(-1), small_dst=i32(small_dst).reshape(-1), small_n=i32(small_n),
                fill_start=i32(fill_start), fill_n=i32(fill_n),
                fill_tot=i32(jnp.sum(fill_n)).reshape(1), half_expert=i32(half_expert),
                half_used=i32(half_used), n_used=n_used)


def _rows_copy(src_ref, src_row, dst_ref, dst_row, rows, sem):
    return pltpu.make_async_copy(
        src_ref.at[pl.ds(pl.multiple_of(src_row, CHUNK), rows), :],
        dst_ref.at[pl.ds(pl.multiple_of(dst_row, CHUNK), rows), :], sem)


def _chunk_copy(src_ref, src_row, dst_ref, dst_row, sem):
    return _rows_copy(src_ref, src_row, dst_ref, dst_row, CHUNK, sem)


def _issue_copies(lists, tile, tile_ref, tile_is_src, hbm_ref, sem):
    big_src, big_dst, big_n, small_src, small_dst, small_n = lists
    for src_l, dst_l, n_l, length, rows in ((big_src, big_dst, big_n, MAX_BIG, BIG * CHUNK),
                                            (small_src, small_dst, small_n, MAX_SMALL, CHUNK)):
        def issue(k, carry, src_l=src_l, dst_l=dst_l, length=length, rows=rows):
            local, remote = src_l[tile * length + k], dst_l[tile * length + k]
            if tile_is_src:
                _rows_copy(tile_ref, local, hbm_ref, remote, rows, sem).start()
            else:
                _rows_copy(hbm_ref, remote, tile_ref, local, rows, sem).start()
            return carry

        lax.fori_loop(0, n_l[tile], issue, 0)


def _wait_rows(n, rows, src_ref, dst_ref, sem):
    def body(c, carry):
        pltpu.make_async_copy(src_ref.at[pl.ds(0, rows), :], dst_ref.at[pl.ds(0, rows), :], sem).wait()
        return carry

    lax.fori_loop(0, n, body, 0)


def _wait_copies(lists, tile, src_ref, dst_ref, sem, extra_chunks=0):
    _wait_rows(lists[2][tile], BIG * CHUNK, src_ref, dst_ref, sem)
    n = lists[5][tile] + extra_chunks
    _wait_rows(n // WAIT_GROUP, WAIT_GROUP * CHUNK, src_ref, dst_ref, sem)
    _wait_rows(n % WAIT_GROUP, CHUNK, src_ref, dst_ref, sem)


def _dispatch_kernel(bs_ref, bd_ref, bn_ref, ss_ref, sd_ref, sn_ref, fstart_ref, fn_ref, ftot_ref,
                     lp_ref, x_ref, wg_ref, wu_ref, wd_ref, xs_hbm, wgb_ref, wub_ref, wdb_ref, xt, zbuf, sems):
    lists = (bs_ref, bd_ref, bn_ref, ss_ref, sd_ref, sn_ref)
    s = pl.program_id(0)
    ns = pl.num_programs(0)
    cur = (s % 2) * PER_STEP
    prv = PER_STEP - cur

    @pl.when(s >= 2)
    def _():
        for k in range(PER_STEP):
            _wait_copies(lists, (s - 2) * PER_STEP + k, xt.at[cur + k], xs_hbm, sems.at[cur + k])

    r = lax.broadcasted_iota(jnp.int32, (SORT_ROWS, MOE_TILE), 0)
    for k in range(PER_STEP):
        lp = lp_ref[k].astype(jnp.int32)
        hit = jnp.logical_or(lp[0:1, :] == r, lp[1:2, :] == r)
        p = jnp.where(hit, 1.0, 0.0).astype(BF16)
        x = x_ref[k * MOE_TILE:(k + 1) * MOE_TILE, :]
        xt[cur + k] = jnp.dot(p, x, preferred_element_type=F32).astype(BF16)

    for k in range(PER_STEP):
        _issue_copies(lists, s * PER_STEP + k, xt.at[cur + k], True, xs_hbm, sems.at[cur + k])

    _cast_bf16(wgb_ref.at[0], wg_ref.at[0])
    _cast_bf16(wub_ref.at[0], wu_ref.at[0])
    _cast_bf16(wdb_ref.at[0], wd_ref.at[0])

    @pl.when(s == ns - 1)
    def _():
        zbuf[...] = jnp.zeros_like(zbuf)
        sem = sems.at[cur]

        def per_range(e, carry):
            def issue(c, carry2):
                _chunk_copy(zbuf, 0, xs_hbm, fstart_ref[e] + c * CHUNK, sem).start()
                return carry2

            return lax.fori_loop(0, fn_ref[e], issue, carry)

        lax.fori_loop(0, N_EXPERTS + 1, per_range, 0)
        for k in range(PER_STEP):
            _wait_copies(lists, (s - 1) * PER_STEP + k, xt.at[prv + k], xs_hbm, sems.at[prv + k])
            _wait_copies(lists, s * PER_STEP + k, xt.at[cur + k], xs_hbm, sems.at[cur + k],
                         extra_chunks=ftot_ref[0] if k == 0 else 0)


def _copy_lists(plan):
    return tuple(plan[k] for k in ("big_src", "big_dst", "big_n", "small_src", "small_dst", "small_n"))


def _dispatch(plan, lpt, x2b, wg, wu, wd):
    T, D = x2b.shape
    rows = PER_STEP * MOE_TILE
    assert T // rows >= 2
    assert T // rows == N_EXPERTS
    wspec = lambda shape: pl.BlockSpec((1,) + shape, lambda t, *_: (t, 0, 0))
    grid_spec = pltpu.PrefetchScalarGridSpec(
        num_scalar_prefetch=9,
        grid=(T // rows,),
        in_specs=[pl.BlockSpec((PER_STEP, SUBLANES, MOE_TILE), lambda t, *_: (t, 0, 0)),
                  pl.BlockSpec((rows, D), lambda t, *_: (t, 0)),
                  wspec((D, D_EXPERT)), wspec((D, D_EXPERT)), wspec((D_EXPERT, D))],
        out_specs=[pl.BlockSpec(memory_space=pl.ANY),
                   wspec((D, D_EXPERT)), wspec((D, D_EXPERT)), wspec((D_EXPERT, D))],
        scratch_shapes=[pltpu.VMEM((2 * PER_STEP, SORT_ROWS, D), BF16), pltpu.VMEM((CHUNK, D), BF16),
                        pltpu.SemaphoreType.DMA((2 * PER_STEP,))],
    )
    return pl.pallas_call(
        _dispatch_kernel,
        grid_spec=grid_spec,
        out_shape=[jax.ShapeDtypeStruct((_gmm_blocks(T) * GMM_ROWS, D), BF16),
                   jax.ShapeDtypeStruct(wg.shape, BF16), jax.ShapeDtypeStruct(wu.shape, BF16),
                   jax.ShapeDtypeStruct(wd.shape, BF16)],
        compiler_params=pltpu.CompilerParams(
            dimension_semantics=("arbitrary",), vmem_limit_bytes=VMEM_LIMIT),
        name="moe_dispatch",
    )(*_copy_lists(plan), plan["fill_start"], plan["fill_n"], plan["fill_tot"], lpt, x2b, wg, wu, wd)


def _expert_ffn(xb, wg_b, wu_b, wd_b):
    g = jnp.dot(xb, wg_b[...], preferred_element_type=F32)
    u = jnp.dot(xb, wu_b[...], preferred_element_type=F32)
    h = (g * jax.nn.sigmoid(g)) * u
    return jnp.dot(h.astype(BF16), wd_b[...], preferred_element_type=F32).astype(BF16)


def _gmm_kernel(he_ref, hu_ref, nu_ref, x_ref, wga_ref, wua_ref, wda_ref, wgb_ref, wub_ref, wdb_ref, o_ref):
    b = pl.program_id(0)
    half = REGION_ALIGN
    same = he_ref[2 * b] == he_ref[2 * b + 1]
    used0 = hu_ref[2 * b] != 0
    used1 = hu_ref[2 * b + 1] != 0
    set_a = (wga_ref.at[0], wua_ref.at[0], wda_ref.at[0])
    set_b = (wgb_ref.at[0], wub_ref.at[0], wdb_ref.at[0])

    @pl.when(jnp.logical_and(used1, same))
    def _():
        o_ref[...] = _expert_ffn(x_ref[...], *set_a)

    @pl.when(jnp.logical_and(used0, jnp.logical_not(jnp.logical_and(used1, same))))
    def _():
        o_ref[0:half, :] = _expert_ffn(x_ref[0:half, :], *set_a)

    @pl.when(jnp.logical_and(used1, jnp.logical_not(same)))
    def _():
        o_ref[half:GMM_ROWS, :] = _expert_ffn(x_ref[half:GMM_ROWS, :], *set_b)

    @pl.when(jnp.logical_not(used0))
    def _():
        o_ref[0:half, :] = jnp.zeros((half, o_ref.shape[1]), o_ref.dtype)

    @pl.when(jnp.logical_not(used1))
    def _():
        o_ref[half:GMM_ROWS, :] = jnp.zeros((GMM_ROWS - half, o_ref.shape[1]), o_ref.dtype)


def _gmm(plan, xs, wg, wu, wd):
    R, D = xs.shape
    rows = pl.BlockSpec((GMM_ROWS, D), lambda b, he, hu, nu: (jnp.minimum(b, nu[0] - 1), 0))

    def wspec(shape, h):
        return pl.BlockSpec((1,) + shape, lambda b, he, hu, nu: (he[2 * b + h], 0, 0))

    grid_spec = pltpu.PrefetchScalarGridSpec(
        num_scalar_prefetch=3,
        grid=(R // GMM_ROWS,),
        in_specs=[rows,
                  wspec((D, D_EXPERT), 0), wspec((D, D_EXPERT), 0), wspec((D_EXPERT, D), 0),
                  wspec((D, D_EXPERT), 1), wspec((D, D_EXPERT), 1), wspec((D_EXPERT, D), 1)],
        out_specs=pl.BlockSpec((GMM_ROWS, D), lambda b, he, hu, nu: (b, 0)),
    )
    return pl.pallas_call(
        _gmm_kernel,
        grid_spec=grid_spec,
        out_shape=jax.ShapeDtypeStruct((R, D), BF16),
        compiler_params=pltpu.CompilerParams(
            dimension_semantics=("arbitrary",), vmem_limit_bytes=VMEM_LIMIT),
        name="moe_gmm",
    )(plan["half_expert"], plan["half_used"], plan["n_used"], xs, wg, wu, wd, wg, wu, wd)


def _combine_kernel(bs_ref, bd_ref, bn_ref, ss_ref, sd_ref, sn_ref, cm_ref, x_ref, ys_hbm, g3_ref, b3_ref,
                    o_ref, yt, sems):
    lists = (bs_ref, bd_ref, bn_ref, ss_ref, sd_ref, sn_ref)
    s = pl.program_id(0)
    ns = pl.num_programs(0)
    cur = (s % 2) * PER_STEP
    nxt = PER_STEP - cur

    def fetch(step, first_buf):
        for k in range(PER_STEP):
            _issue_copies(lists, step * PER_STEP + k, yt.at[first_buf + k], False, ys_hbm,
                          sems.at[first_buf + k])

    @pl.when(s == 0)
    def _():
        yt[...] = jnp.zeros_like(yt)
        fetch(0, 0)

    @pl.when(s + 1 < ns)
    def _():
        fetch(s + 1, nxt)

    for k in range(PER_STEP):
        _wait_copies(lists, s * PER_STEP + k, ys_hbm, yt.at[cur + k], sems.at[cur + k])

    col = lax.broadcasted_iota(jnp.int32, (MOE_TILE, SORT_ROWS), 1)
    for k in range(PER_STEP):
        rs = slice(k * MOE_TILE, (k + 1) * MOE_TILE)
        cm = cm_ref[rs, :]
        lp0 = cm[:, 0:1].astype(jnp.int32)
        lp1 = cm[:, 1:2].astype(jnp.int32)
        w = jnp.where(col == lp0, cm[:, 2:3], jnp.where(col == lp1, cm[:, 3:4], 0.0)).astype(BF16)
        y = jnp.dot(w, yt[cur + k], preferred_element_type=F32)
        o_ref[rs, :] = _layer_norm(ALPHA * x_ref[rs, :] + y, g3_ref[...], b3_ref[...])


def _combine(plan, route, x2, ys, g3, b3):
    T, D = x2.shape
    rows = PER_STEP * MOE_TILE
    grid_spec = pltpu.PrefetchScalarGridSpec(
        num_scalar_prefetch=6,
        grid=(T // rows,),
        in_specs=[pl.BlockSpec((rows, LANES), lambda t, *_: (t, 0)),
                  pl.BlockSpec((rows, D), lambda t, *_: (t, 0)),
                  pl.BlockSpec(memory_space=pl.ANY),
                  pl.BlockSpec((1, D), lambda t, *_: (0, 0)),
                  pl.BlockSpec((1, D), lambda t, *_: (0, 0))],
        out_specs=pl.BlockSpec((rows, D), lambda t, *_: (t, 0)),
        scratch_shapes=[pltpu.VMEM((2 * PER_STEP, SORT_ROWS, D), BF16),
                        pltpu.SemaphoreType.DMA((2 * PER_STEP,))],
    )
    return pl.pallas_call(
        _combine_kernel,
        grid_spec=grid_spec,
        out_shape=jax.ShapeDtypeStruct((T, D), F32),
        compiler_params=pltpu.CompilerParams(
            dimension_semantics=("arbitrary",), vmem_limit_bytes=VMEM_LIMIT),
        name="moe_combine",
    )(*_copy_lists(plan), route, x2, ys, g3, b3)


def _moe(x2, x2b, route, lpt, meta, wg, wu, wd, g3, b3):
    plan = _plan(meta, x2.shape[0])
    xs, wg_b, wu_b, wd_b = _dispatch(plan, lpt, x2b, wg, wu, wd)
    ys = _gmm(plan, xs, wg_b, wu_b, wd_b)
    return _combine(plan, route, x2, ys, g3, b3)


def _row(v):
    return v.reshape(1, -1).astype(F32)


def kernel(x, mem, w_in, b_in, w_dw, b_dw, g_conv_norm, b_conv_norm, attn_sinks, w_out, g_ln1, b_ln1,
           w_mq, w_mkv, w_mo, g_ln2, b_ln2, w_group, b_group, w_router, b_router, w_gate, w_up, w_down,
           g_ln3, b_ln3):
    B, S, D = x.shape
    for l in range(DEPTH):
        w_dw_p = jnp.zeros((CONV_HALO, CONV_CH), F32).at[:CONV_WIDTH].set(w_dw[l])
        x1 = _mixer(x, attn_sinks[l].astype(F32), w_in[l], _row(b_in[l]), w_dw_p,
                    _row(b_dw[l]), _row(g_conv_norm[l]), _row(b_conv_norm[l]),
                    w_out[l], _row(g_ln1[l]), _row(b_ln1[l]))

        kvm = _memkv(mem.reshape(B * MEM_LEN, D), w_mkv[l]).reshape(B, MEM_LEN, 2 * D)

        wr = jnp.concatenate(
            [w_group[l], jnp.transpose(w_router[l], (1, 0, 2)).reshape(D, N_EXPERTS)], axis=1)
        wr = jnp.pad(wr, ((0, 0), (0, LANES - wr.shape[1])))
        br = jnp.pad(jnp.concatenate([b_group[l], b_router[l].reshape(-1)]), (0, LANES - N_GROUPS - N_EXPERTS))
        x2, x2b, route, lpt, meta = _memattn(x1, w_mq[l], kvm, w_mo[l],
                                 _row(g_ln2[l]), _row(b_ln2[l]), wr.astype(F32), _row(br))

        T = B * S
        y = _moe(x2.reshape(T, D), x2b.reshape(T, D), route.reshape(T, LANES), lpt, meta,
                 w_gate[l], w_up[l], w_down[l],
                 _row(g_ln3[l]), _row(b_ln3[l]))
        x = y.reshape(B, S, D)
    return x
```

```python
import functools

import jax
import jax.numpy as jnp
from jax import lax
from jax.experimental import pallas as pl
from jax.experimental.pallas import tpu as pltpu

D_MODEL = 1024
MEM_LEN = 256
CONV_CH = 512
CONV_WIDTH = 31
N_HEADS = 8
N_KV_HEADS = 2
HEAD_DIM = 64
GQ = N_HEADS // N_KV_HEADS
ATTN_W = N_HEADS * HEAD_DIM
KV_W = N_KV_HEADS * HEAD_DIM
WINDOW = 128
D_MIX = CONV_CH + ATTN_W
D_IN = 2 * CONV_CH + ATTN_W + 2 * KV_W
MEM_HEADS = 4
MEM_HEAD_DIM = D_MODEL // MEM_HEADS
N_GROUPS = 4
EXPERTS_PER_GROUP = 4
N_EXPERTS = N_GROUPS * EXPERTS_PER_GROUP
D_EXPERT = D_MODEL // 2
DEPTH = 1
ALPHA = (2.0 * DEPTH) ** 0.25
LN_EPS = 1e-5

LANES = 128
SUBLANES = 8
CONV_ROWS = 128
LN_ROWS = 64
MEM_TILE = 1024
MASK_VALUE = -1e30
CONV_HALO = 32
SEQ_TILE = 512
MOE_TILE = 512
CHUNK = 16
REGION_ALIGN = 512
GMM_ROWS = 2 * REGION_ALIGN
TOP_K = 2
SORT_ROWS = -(-(MOE_TILE * TOP_K + N_EXPERTS * (CHUNK - 1)) // 256) * 256
BIG = 4
MAX_BIG = SORT_ROWS // (BIG * CHUNK)
MAX_SMALL = N_EXPERTS * (BIG - 1)
PER_STEP = 2
WAIT_GROUP = 8
ROUTE_OFF = N_GROUPS
VMEM_LIMIT = 56 * 1024 * 1024

BF16 = jnp.bfloat16
F32 = jnp.float32


def _layer_norm(x, g, b):
    mu = jnp.mean(x, axis=-1, keepdims=True)
    xc = x - mu
    var = jnp.mean(xc * xc, axis=-1, keepdims=True)
    return xc * lax.rsqrt(var + LN_EPS) * g + b


def _cast_bf16(dst_ref, src_ref):
    rows = 256
    for r0 in range(0, src_ref.shape[0], rows):
        dst_ref[r0:r0 + rows, :] = src_ref[r0:r0 + rows, :].astype(BF16)


def _dot_nt(a, b):
    return lax.dot_general(a, b, (((1,), (1,)), ((), ())), preferred_element_type=F32)


def _mixer_kernel(sinks_ref, x_ref, w_in_ref, b_in_ref, w_dw_ref, b_dw_ref, g_cn_ref, b_cn_ref,
                  w_out_ref, g1_ref, b1_ref, we_ref, o_ref, web_ref, w_in_b, w_out_b, hbuf, hshift, cbuf, qbuf,
                  kbuf, vbuf, ymix):
    i = pl.program_id(1)
    ts = SEQ_TILE
    web_ref[...] = we_ref[...].astype(BF16)

    @pl.when(jnp.logical_and(pl.program_id(0) == 0, i == 0))
    def _():
        _cast_bf16(w_in_b, w_in_ref)
        _cast_bf16(w_out_b, w_out_ref)

    @pl.when(i == 0)
    def _():
        hbuf[0:CONV_HALO, :] = jnp.zeros((CONV_HALO, CONV_CH), F32)
        kbuf[:, 0:WINDOW, :] = jnp.zeros((2 * N_KV_HEADS, WINDOW, KV_W), BF16)
        vbuf[:, 0:WINDOW, :] = jnp.zeros((2, WINDOW, KV_W), BF16)

    x = x_ref[0]
    u = jnp.dot(x.astype(BF16), w_in_b[...], preferred_element_type=F32) + b_in_ref[...]
    a = u[:, 0:CONV_CH]
    gate = u[:, CONV_CH:2 * CONV_CH]
    hbuf[CONV_HALO:CONV_HALO + ts, :] = a * jax.nn.sigmoid(gate)
    qbuf[...] = (u[:, 2 * CONV_CH:2 * CONV_CH + ATTN_W] * (HEAD_DIM ** -0.5)).astype(BF16)
    kf = u[:, 2 * CONV_CH + ATTN_W:2 * CONV_CH + ATTN_W + KV_W]
    vf = u[:, 2 * CONV_CH + ATTN_W + KV_W:D_IN]
    kr = pltpu.roll(kf, HEAD_DIM, axis=1)
    vr = pltpu.roll(vf, HEAD_DIM, axis=1)
    lo = lax.broadcasted_iota(jnp.int32, (ts, KV_W), 1) < HEAD_DIM
    rows = slice(WINDOW, WINDOW + ts)
    kbuf[0, rows, :] = jnp.where(lo, kf, 0.0).astype(BF16)
    kbuf[1, rows, :] = jnp.where(lo, 0.0, kr).astype(BF16)
    kbuf[2, rows, :] = jnp.where(lo, kr, 0.0).astype(BF16)
    kbuf[3, rows, :] = jnp.where(lo, 0.0, kf).astype(BF16)
    vbuf[0, rows, :] = vf.astype(BF16)
    vbuf[1, rows, :] = vr.astype(BF16)

    base = CONV_HALO - (CONV_WIDTH - 1)
    n_shift = ts + CONV_HALO - SUBLANES
    for b in range(1, SUBLANES):
        hshift[b - 1, 0:n_shift, :] = hbuf[b:b + n_shift, :]
    rc = CONV_ROWS

    def conv_chunk(c):
        r0 = c * rc
        for l in range(CONV_CH // LANES):
            ls = slice(l * LANES, (l + 1) * LANES)
            acc = jnp.zeros((rc, LANES), F32)
            for j in range(CONV_WIDTH):
                a8, b = divmod(j + base, SUBLANES)
                rs = slice(r0 + SUBLANES * a8, r0 + SUBLANES * a8 + rc)
                tap = hbuf[rs, ls] if b == 0 else hshift[b - 1, rs, ls]
                acc = acc + tap * w_dw_ref[j:j + 1, ls]
            cbuf[r0:r0 + rc, ls] = acc
        for r1 in range(r0, r0 + rc, LN_ROWS):
            rs = slice(r1, r1 + LN_ROWS)
            y = _layer_norm(cbuf[rs, :] + b_dw_ref[...], g_cn_ref[...], b_cn_ref[...])
            y = y * jax.nn.sigmoid(y)
            ymix[rs, 0:CONV_CH] = y.astype(BF16)

    qi = lax.broadcasted_iota(jnp.int32, (2 * WINDOW, 2 * WINDOW), 0) % WINDOW
    kj = lax.broadcasted_iota(jnp.int32, (2 * WINDOW, 2 * WINDOW), 1)
    dist = qi + WINDOW - kj
    band = (dist >= 0) & (dist < WINDOW)
    top = lax.broadcasted_iota(jnp.int32, (2 * WINDOW, 1), 0) < WINDOW
    lo_out = lax.broadcasted_iota(jnp.int32, (WINDOW, 2 * HEAD_DIM), 1) < HEAD_DIM

    def attn_block(jb):
        r0 = jb * WINDOW
        valid = band & jnp.logical_or(i != 0, kj >= WINDOW) if jb == 0 else band
        for kvh in range(N_KV_HEADS):
            h0 = kvh * GQ
            c0 = h0 * HEAD_DIM
            qs = jnp.concatenate([qbuf[r0:r0 + WINDOW, c0:c0 + 2 * HEAD_DIM],
                                  qbuf[r0:r0 + WINDOW, c0 + 2 * HEAD_DIM:c0 + 4 * HEAD_DIM]], axis=0)
            pv = []
            for par in range(2):
                kk = kbuf[2 * kvh + par, r0:r0 + 2 * WINDOW, :]
                vv = vbuf[(kvh + par) % 2, r0:r0 + 2 * WINDOW, :]
                s = jnp.where(valid, _dot_nt(qs, kk), MASK_VALUE)
                sink = jnp.where(top, sinks_ref[h0 + par], sinks_ref[h0 + 2 + par])
                m = jnp.maximum(jnp.max(s, axis=-1, keepdims=True), sink)
                p = jnp.exp(s - m)
                denom = jnp.sum(p, axis=-1, keepdims=True) + jnp.exp(sink - m)
                pv.append(jnp.dot(p.astype(BF16), vv, preferred_element_type=F32) / denom)
            for pair in range(2):
                rs = slice(pair * WINDOW, (pair + 1) * WINDOW)
                o = jnp.where(lo_out, pv[0][rs], pv[1][rs])
                cs = CONV_CH + c0 + pair * 2 * HEAD_DIM
                ymix[r0:r0 + WINDOW, cs:cs + 2 * HEAD_DIM] = o.astype(BF16)

    assert ts // rc == ts // WINDOW
    for c in range(ts // rc):
        attn_block(c)
        conv_chunk(c)

    mix = jnp.dot(ymix[...], w_out_b[...], preferred_element_type=F32)
    o_ref[0] = _layer_norm(ALPHA * x + mix, g1_ref[...], b1_ref[...])

    hbuf[0:CONV_HALO, :] = hbuf[ts:ts + CONV_HALO, :]
    kbuf[:, 0:WINDOW, :] = kbuf[:, ts:ts + WINDOW, :]
    vbuf[:, 0:WINDOW, :] = vbuf[:, ts:ts + WINDOW, :]


def _const_spec(shape):
    nd = len(shape)
    return pl.BlockSpec(shape, lambda *_: (0,) * nd)


def _resident_spec(shape):
    nd = len(shape)
    return pl.BlockSpec(shape, lambda *_: (0,) * nd, pipeline_mode=pl.Buffered(1))


def _mixer(x, sinks, w_in, b_in, w_dw, b_dw, g_cn, b_cn, w_out, g1, b1, w_expert):
    B, S, D = x.shape
    ts = SEQ_TILE
    tile = pl.BlockSpec((1, ts, D), lambda b, i: (b, i, 0))
    nt = S // ts
    n_e, we_rows, we_cols = w_expert.shape
    per = (B * nt) // n_e
    assert per * n_e == B * nt and we_rows % per == 0
    wes = pl.BlockSpec((1, we_rows // per, we_cols), lambda b, i: ((b * nt + i) // per, (b * nt + i) % per, 0))
    return pl.pallas_call(
        _mixer_kernel,
        grid=(B, S // ts),
        in_specs=[
            pl.BlockSpec(memory_space=pltpu.SMEM),
            tile,
            _resident_spec((D, D_IN)), _const_spec((1, D_IN)),
            _const_spec((CONV_HALO, CONV_CH)), _const_spec((1, CONV_CH)),
            _const_spec((1, CONV_CH)), _const_spec((1, CONV_CH)),
            _resident_spec((D_MIX, D)), _const_spec((1, D)), _const_spec((1, D)),
            wes,
        ],
        out_specs=[tile, wes],
        out_shape=[jax.ShapeDtypeStruct((B, S, D), F32), jax.ShapeDtypeStruct(w_expert.shape, BF16)],
        scratch_shapes=[
            pltpu.VMEM((D, D_IN), BF16),
            pltpu.VMEM((D_MIX, D), BF16),
            pltpu.VMEM((CONV_HALO + ts, CONV_CH), F32),
            pltpu.VMEM((SUBLANES - 1, CONV_HALO + ts, CONV_CH), F32),
            pltpu.VMEM((ts, CONV_CH), F32),
            pltpu.VMEM((ts, ATTN_W), BF16),
            pltpu.VMEM((2 * N_KV_HEADS, WINDOW + ts, KV_W), BF16),
            pltpu.VMEM((2, WINDOW + ts, KV_W), BF16),
            pltpu.VMEM((ts, D_MIX), BF16),
        ],
        compiler_params=pltpu.CompilerParams(
            dimension_semantics=("arbitrary", "arbitrary"), vmem_limit_bytes=VMEM_LIMIT),
        name="mixer",
    )(sinks, x, w_in, b_in, w_dw, b_dw, g_cn, b_cn, w_out, g1, b1, w_expert)


def _memkv_kernel(mem_ref, w_ref, o_ref):
    o_ref[...] = jnp.dot(mem_ref[...].astype(BF16), w_ref[...].astype(BF16),
                         preferred_element_type=F32).astype(BF16)


def _memkv(mem2d, w_mkv):
    M, D = mem2d.shape
    N = w_mkv.shape[1]
    tn = 512
    return pl.pallas_call(
        _memkv_kernel,
        grid=(N // tn,),
        in_specs=[pl.BlockSpec((M, D), lambda j: (0, 0)), pl.BlockSpec((D, tn), lambda j: (0, j))],
        out_specs=pl.BlockSpec((M, tn), lambda j: (0, j)),
        out_shape=jax.ShapeDtypeStruct((M, N), BF16),
        compiler_params=pltpu.CompilerParams(dimension_semantics=("arbitrary",)),
        name="memkv",
    )(mem2d, w_mkv)


def _first_max(rows):
    best = rows[0]
    for r in rows[1:]:
        best = jnp.maximum(best, r)
    idx = jnp.full(best.shape, len(rows) - 1, jnp.int32)
    for k in range(len(rows) - 2, -1, -1):
        idx = jnp.where(rows[k] == best, k, idx)
    return best, idx


def _route_plan(logits_t):
    tile = logits_t.shape[1]
    row = lambda k: logits_t[k:k + 1, :]
    gmax, g_idx = _first_max([row(g) for g in range(N_GROUPS)])
    gsum = jnp.exp(row(0) - gmax)
    for g in range(1, N_GROUPS):
        gsum = gsum + jnp.exp(row(g) - gmax)
    g_p = 1.0 / gsum
    rl = []
    for e in range(EXPERTS_PER_GROUP):
        v = row(ROUTE_OFF + (N_GROUPS - 1) * EXPERTS_PER_GROUP + e)
        for g in range(N_GROUPS - 2, -1, -1):
            v = jnp.where(g_idx == g, row(ROUTE_OFF + g * EXPERTS_PER_GROUP + e), v)
        rl.append(v)
    m1, i1 = _first_max(rl)
    m2, i2 = _first_max([jnp.where(i1 == e, MASK_VALUE, rl[e]) for e in range(EXPERTS_PER_GROUP)])
    ex = jnp.exp(m2 - m1)
    w1 = 1.0 / (1.0 + ex)
    w2 = ex * w1
    e1 = g_idx * EXPERTS_PER_GROUP + i1
    e2 = g_idx * EXPERTS_PER_GROUP + i2

    eid = lax.broadcasted_iota(jnp.int32, (N_EXPERTS, tile), 0)
    hit1 = eid == e1
    hit2 = eid == e2
    oh = jnp.where(jnp.logical_or(hit1, hit2), 1.0, 0.0)
    r = lax.broadcasted_iota(jnp.int32, (tile, tile), 0)
    c = lax.broadcasted_iota(jnp.int32, (tile, tile), 1)
    tri = jnp.where(r <= c, 1.0, 0.0).astype(BF16)
    csum = jnp.dot(oh.astype(BF16), tri, preferred_element_type=F32)
    counts = jnp.broadcast_to(csum[:, tile - 1:tile], (N_EXPERTS, tile)).astype(jnp.int32)
    nch = jnp.right_shift(counts + (CHUNK - 1), CHUNK.bit_length() - 1)
    er = lax.broadcasted_iota(jnp.int32, (N_EXPERTS, N_EXPERTS), 0)
    ec = lax.broadcasted_iota(jnp.int32, (N_EXPERTS, N_EXPERTS), 1)
    lower = jnp.where(ec < er, 1.0, 0.0).astype(BF16)
    off = jnp.dot(lower, nch.astype(F32).astype(BF16), preferred_element_type=F32) * CHUNK
    pos = off + csum - oh
    lp1 = jnp.sum(jnp.where(hit1, pos, 0.0), axis=0, keepdims=True)
    lp2 = jnp.sum(jnp.where(hit2, pos, 0.0), axis=0, keepdims=True)
    zero = jnp.zeros_like(lp1)
    route_t = jnp.concatenate([lp1, lp2, g_p * w1, g_p * w2, zero, zero, zero, zero], axis=0)
    meta = jnp.concatenate([nch[:, 0:LANES], off[:, 0:LANES].astype(jnp.int32)], axis=0)
    return route_t, meta


def _memattn_kernel(x_ref, wq_ref, k_ref, v_ref, wo_ref, g2_ref, b2_ref, wr2_ref, wrh_ref, br_ref, we_ref,
                    o_ref, ob_ref, route_ref, routet_ref, meta_ref, web_ref, wq_b, wo_b):
    web_ref[...] = we_ref[...].astype(BF16)

    @pl.when(jnp.logical_and(pl.program_id(0) == 0, pl.program_id(1) == 0))
    def _():
        _cast_bf16(wq_b, wq_ref)
        _cast_bf16(wo_b, wo_ref)

    def rows_logits(rs):
        x = x_ref[0, rs, :]
        q = jnp.dot(x.astype(BF16), wq_b[...], preferred_element_type=F32)
        q = (q * (MEM_HEAD_DIM ** -0.5)).astype(BF16)
        outs = []
        for h in range(MEM_HEADS):
            sl = slice(h * MEM_HEAD_DIM, (h + 1) * MEM_HEAD_DIM)
            s = _dot_nt(q[:, sl], k_ref[0, :, sl])
            m = jnp.max(s, axis=-1, keepdims=True)
            p = jnp.exp(s - m)
            denom = jnp.sum(p, axis=-1, keepdims=True)
            o = jnp.dot(p.astype(BF16), v_ref[0, :, sl], preferred_element_type=F32)
            outs.append((o / denom).astype(BF16))
        o = jnp.dot(jnp.concatenate(outs, axis=-1), wo_b[...], preferred_element_type=F32)
        x2 = _layer_norm(ALPHA * x + o, g2_ref[...], b2_ref[...])
        o_ref[0, rs, :] = x2
        x2h = x2.astype(BF16)
        ob_ref[0, rs, :] = x2h
        x2l = (x2 - x2h.astype(F32)).astype(BF16)
        hh = jnp.dot(x2h, wr2_ref[...], preferred_element_type=F32)
        return (hh[:, 0:LANES] + hh[:, LANES:2 * LANES]
                + jnp.dot(x2l, wrh_ref[...], preferred_element_type=F32) + br_ref[...])

    ts = x_ref.shape[1]
    groups = [slice(r0, r0 + MOE_TILE) for r0 in range(0, ts, MOE_TILE)]
    logits = [rows_logits(rs) for rs in groups]
    pad = jnp.zeros((LANES - SUBLANES, MOE_TILE), F32)
    for k, rs in enumerate(groups):
        route_t, meta = _route_plan(jnp.transpose(logits[k]))
        routet_ref[k] = route_t
        route_ref[0, rs, :] = jnp.transpose(jnp.concatenate([route_t, pad], axis=0))
        meta_ref[k] = meta


def _memattn(x1, wq, kvm, wo, g2, b2, wr, br, w_expert):
    B, S, D = x1.shape
    ts = MEM_TILE
    assert B * (S // ts) == w_expert.shape[0]
    wes = pl.BlockSpec((1,) + w_expert.shape[1:], lambda b, i: (b * (S // ts) + i, 0, 0))
    per = ts // MOE_TILE
    nt = S // ts
    c = wr * (2.0 ** 16 + 1.0)
    w_high = c - (c - wr)
    wrh = w_high.astype(BF16)
    wr2 = jnp.concatenate([wrh, (wr - w_high).astype(BF16)], axis=1)
    tile = pl.BlockSpec((1, ts, D), lambda b, i: (b, i, 0))
    kspec = pl.BlockSpec((1, MEM_LEN, D), lambda b, i: (b, 0, 0))
    vspec = pl.BlockSpec((1, MEM_LEN, D), lambda b, i: (b, 0, 1))
    return pl.pallas_call(
        _memattn_kernel,
        grid=(B, nt),
        in_specs=[tile, _resident_spec((D, D)), kspec, vspec, _resident_spec((D, D)),
                  _const_spec((1, D)), _const_spec((1, D)),
                  _const_spec((D, 2 * LANES)), _const_spec((D, LANES)), _const_spec((1, LANES)), wes],
        out_specs=[tile, tile, pl.BlockSpec((1, ts, LANES), lambda b, i: (b, i, 0)),
                   pl.BlockSpec((per, SUBLANES, MOE_TILE), lambda b, i: (b * nt + i, 0, 0)),
                   pl.BlockSpec((per, 2 * N_EXPERTS, LANES), lambda b, i: (b * nt + i, 0, 0)), wes],
        out_shape=[jax.ShapeDtypeStruct((B, S, D), F32),
                   jax.ShapeDtypeStruct((B, S, D), BF16),
                   jax.ShapeDtypeStruct((B, S, LANES), F32),
                   jax.ShapeDtypeStruct((B * nt * per, SUBLANES, MOE_TILE), F32),
                   jax.ShapeDtypeStruct((B * nt * per, 2 * N_EXPERTS, LANES), jnp.int32),
                   jax.ShapeDtypeStruct(w_expert.shape, BF16)],
        scratch_shapes=[pltpu.VMEM((D, D), BF16), pltpu.VMEM((D, D), BF16)],
        compiler_params=pltpu.CompilerParams(
            dimension_semantics=("arbitrary", "arbitrary"), vmem_limit_bytes=VMEM_LIMIT),
        name="memattn",
    )(x1, wq, kvm, kvm, wo, g2, b2, wr2, wrh, br, w_expert)


def _gmm_blocks(n_tokens):
    rows = (n_tokens * TOP_K + (n_tokens // MOE_TILE) * N_EXPERTS * (CHUNK - 1)
            + N_EXPERTS * (REGION_ALIGN - CHUNK))
    return -(-rows // GMM_ROWS)


def _plan(meta, T):
    nch = meta[:, :N_EXPERTS, 0]
    n16 = nch * CHUNK
    n_e = jnp.sum(n16, axis=0)
    reg = (n_e + REGION_ALIGN - 1) // REGION_ALIGN * REGION_ALIGN
    gend = jnp.cumsum(reg)
    gbase = gend - reg
    dst = gbase[None, :] + jnp.cumsum(n16, axis=0) - n16
    half_row = jnp.arange(2 * _gmm_blocks(T), dtype=jnp.int32) * REGION_ALIGN
    half_expert = jnp.minimum(jnp.sum(half_row[:, None] >= gend[None, :], axis=1), N_EXPERTS - 1)
    half_used = half_row < gend[-1]
    n_used = ((gend[-1] + GMM_ROWS - 1) // GMM_ROWS).astype(jnp.int32).reshape(1)
    src = (jnp.cumsum(nch, axis=1) - nch) * CHUNK
    n_big = nch // BIG

    def copy_list(count, src0, dst0, rows, length):
        cum = jnp.cumsum(count, axis=1)
        first = (cum - count)[:, None, :]
        k = jnp.arange(length, dtype=jnp.int32)[None, :, None]
        mine = (k >= first) & (k < cum[:, None, :])
        step = (k - first) * rows
        pick = lambda base: jnp.sum(jnp.where(mine, base[:, None, :] + step, 0), axis=2)
        return pick(src0), pick(dst0), cum[:, -1]

    big_src, big_dst, big_n = copy_list(n_big, src, dst, BIG * CHUNK, MAX_BIG)
    rest = n_big * (BIG * CHUNK)
    small_src, small_dst, small_n = copy_list(nch - n_big * BIG, src + rest, dst + rest, CHUNK, MAX_SMALL)
    i32 = lambda a: a.astype(jnp.int32)
    fill_start = jnp.concatenate([gbase + n_e, gend[-1:]])
    fill_n = jnp.concatenate([reg - n_e, _gmm_blocks(T) * GMM_ROWS - gend[-1:]]) // CHUNK
    return dict(big_src=i32(big_src).reshape(-1), big_dst=i32(big_dst).reshape(-1), big_n=i32(big_n),
                small_src=i32(small_src).reshape(-1), small_dst=i32(small_dst).reshape(-1), small_n=i32(small_n),
                fill_start=i32(fill_start), fill_n=i32(fill_n),
                fill_tot=i32(jnp.sum(fill_n)).reshape(1), half_expert=i32(half_expert),
                half_used=i32(half_used), n_used=n_used)


def _rows_copy(src_ref, src_row, dst_ref, dst_row, rows, sem):
    return pltpu.make_async_copy(
        src_ref.at[pl.ds(pl.multiple_of(src_row, CHUNK), rows), :],
        dst_ref.at[pl.ds(pl.multiple_of(dst_row, CHUNK), rows), :], sem)


def _chunk_copy(src_ref, src_row, dst_ref, dst_row, sem):
    return _rows_copy(src_ref, src_row, dst_ref, dst_row, CHUNK, sem)


def _issue_copies(lists, tile, tile_ref, tile_is_src, hbm_ref, sem):
    big_src, big_dst, big_n, small_src, small_dst, small_n = lists
    for src_l, dst_l, n_l, length, rows in ((big_src, big_dst, big_n, MAX_BIG, BIG * CHUNK),
                                            (small_src, small_dst, small_n, MAX_SMALL, CHUNK)):
        def issue(k, carry, src_l=src_l, dst_l=dst_l, length=length, rows=rows):
            local, remote = src_l[tile * length + k], dst_l[tile * length + k]
            if tile_is_src:
                _rows_copy(tile_ref, local, hbm_ref, remote, rows, sem).start()
            else:
                _rows_copy(hbm_ref, remote, tile_ref, local, rows, sem).start()
            return carry

        lax.fori_loop(0, n_l[tile], issue, 0)


def _wait_rows(n, rows, src_ref, dst_ref, sem):
    def body(c, carry):
        pltpu.make_async_copy(src_ref.at[pl.ds(0, rows), :], dst_ref.at[pl.ds(0, rows), :], sem).wait()
        return carry

    lax.fori_loop(0, n, body, 0)


def _wait_copies(lists, tile, src_ref, dst_ref, sem, extra_chunks=0):
    _wait_rows(lists[2][tile], BIG * CHUNK, src_ref, dst_ref, sem)
    n = lists[5][tile] + extra_chunks
    _wait_rows(n // WAIT_GROUP, WAIT_GROUP * CHUNK, src_ref, dst_ref, sem)
    _wait_rows(n % WAIT_GROUP, CHUNK, src_ref, dst_ref, sem)


def _dispatch_kernel(bs_ref, bd_ref, bn_ref, ss_ref, sd_ref, sn_ref, fstart_ref, fn_ref, ftot_ref,
                     lp_ref, x_ref, we_ref, xs_hbm, web_ref, xt, zbuf, sems):
    lists = (bs_ref, bd_ref, bn_ref, ss_ref, sd_ref, sn_ref)
    s = pl.program_id(0)
    ns = pl.num_programs(0)
    cur = (s % 2) * PER_STEP
    prv = PER_STEP - cur

    @pl.when(s >= 2)
    def _():
        for k in range(PER_STEP):
            _wait_copies(lists, (s - 2) * PER_STEP + k, xt.at[cur + k], xs_hbm, sems.at[cur + k])

    r = lax.broadcasted_iota(jnp.int32, (SORT_ROWS, MOE_TILE), 0)
    for k in range(PER_STEP):
        lp = lp_ref[k].astype(jnp.int32)
        hit = jnp.logical_or(lp[0:1, :] == r, lp[1:2, :] == r)
        p = jnp.where(hit, 1.0, 0.0).astype(BF16)
        x = x_ref[k * MOE_TILE:(k + 1) * MOE_TILE, :]
        xt[cur + k] = jnp.dot(p, x, preferred_element_type=F32).astype(BF16)

    for k in range(PER_STEP):
        _issue_copies(lists, s * PER_STEP + k, xt.at[cur + k], True, xs_hbm, sems.at[cur + k])

    _cast_bf16(web_ref.at[0], we_ref.at[0])

    @pl.when(s == ns - 1)
    def _():
        zbuf[...] = jnp.zeros_like(zbuf)
        sem = sems.at[cur]

        def per_range(e, carry):
            def issue(c, carry2):
                _chunk_copy(zbuf, 0, xs_hbm, fstart_ref[e] + c * CHUNK, sem).start()
                return carry2

            return lax.fori_loop(0, fn_ref[e], issue, carry)

        lax.fori_loop(0, N_EXPERTS + 1, per_range, 0)
        for k in range(PER_STEP):
            _wait_copies(lists, (s - 1) * PER_STEP + k, xt.at[prv + k], xs_hbm, sems.at[prv + k])
            _wait_copies(lists, s * PER_STEP + k, xt.at[cur + k], xs_hbm, sems.at[cur + k],
                         extra_chunks=ftot_ref[0] if k == 0 else 0)


def _copy_lists(plan):
    return tuple(plan[k] for k in ("big_src", "big_dst", "big_n", "small_src", "small_dst", "small_n"))


def _dispatch(plan, lpt, x2b, w_expert):
    T, D = x2b.shape
    rows = PER_STEP * MOE_TILE
    assert T // rows >= 2
    assert T // rows == N_EXPERTS
    wes = pl.BlockSpec((1,) + w_expert.shape[1:], lambda t, *_: (t, 0, 0))
    grid_spec = pltpu.PrefetchScalarGridSpec(
        num_scalar_prefetch=9,
        grid=(T // rows,),
        in_specs=[pl.BlockSpec((PER_STEP, SUBLANES, MOE_TILE), lambda t, *_: (t, 0, 0)),
                  pl.BlockSpec((rows, D), lambda t, *_: (t, 0)), wes],
        out_specs=[pl.BlockSpec(memory_space=pl.ANY), wes],
        scratch_shapes=[pltpu.VMEM((2 * PER_STEP, SORT_ROWS, D), BF16), pltpu.VMEM((CHUNK, D), BF16),
                        pltpu.SemaphoreType.DMA((2 * PER_STEP,))],
    )
    return pl.pallas_call(
        _dispatch_kernel,
        grid_spec=grid_spec,
        out_shape=[jax.ShapeDtypeStruct((_gmm_blocks(T) * GMM_ROWS, D), BF16),
                   jax.ShapeDtypeStruct(w_expert.shape, BF16)],
        compiler_params=pltpu.CompilerParams(
            dimension_semantics=("arbitrary",), vmem_limit_bytes=VMEM_LIMIT),
        name="moe_dispatch",
    )(*_copy_lists(plan), plan["fill_start"], plan["fill_n"], plan["fill_tot"], lpt, x2b, w_expert)


def _expert_ffn(xb, wg_b, wu_b, wd_b):
    g = jnp.dot(xb, wg_b[...], preferred_element_type=F32)
    u = jnp.dot(xb, wu_b[...], preferred_element_type=F32)
    h = (g * jax.nn.sigmoid(g)) * u
    return jnp.dot(h.astype(BF16), wd_b[...], preferred_element_type=F32).astype(BF16)


def _gmm_kernel(he_ref, hu_ref, nu_ref, x_ref, wga_ref, wua_ref, wda_ref, wgb_ref, wub_ref, wdb_ref, o_ref):
    b = pl.program_id(0)
    half = REGION_ALIGN
    same = he_ref[2 * b] == he_ref[2 * b + 1]
    used0 = hu_ref[2 * b] != 0
    used1 = hu_ref[2 * b + 1] != 0
    set_a = (wga_ref.at[0], wua_ref.at[0], wda_ref.at[0])
    set_b = (wgb_ref.at[0], wub_ref.at[0], wdb_ref.at[0])

    @pl.when(jnp.logical_and(used1, same))
    def _():
        o_ref[...] = _expert_ffn(x_ref[...], *set_a)

    @pl.when(jnp.logical_and(used0, jnp.logical_not(jnp.logical_and(used1, same))))
    def _():
        o_ref[0:half, :] = _expert_ffn(x_ref[0:half, :], *set_a)

    @pl.when(jnp.logical_and(used1, jnp.logical_not(same)))
    def _():
        o_ref[half:GMM_ROWS, :] = _expert_ffn(x_ref[half:GMM_ROWS, :], *set_b)

    @pl.when(jnp.logical_not(used0))
    def _():
        o_ref[0:half, :] = jnp.zeros((half, o_ref.shape[1]), o_ref.dtype)

    @pl.when(jnp.logical_not(used1))
    def _():
        o_ref[half:GMM_ROWS, :] = jnp.zeros((GMM_ROWS - half, o_ref.shape[1]), o_ref.dtype)


def _gmm(plan, xs, wg, wu, wd):
    R, D = xs.shape
    rows = pl.BlockSpec((GMM_ROWS, D), lambda b, he, hu, nu: (jnp.minimum(b, nu[0] - 1), 0))

    def wspec(shape, h):
        return pl.BlockSpec((1,) + shape, lambda b, he, hu, nu: (he[2 * b + h], 0, 0))

    grid_spec = pltpu.PrefetchScalarGridSpec(
        num_scalar_prefetch=3,
        grid=(R // GMM_ROWS,),
        in_specs=[rows,
                  wspec((D, D_EXPERT), 0), wspec((D, D_EXPERT), 0), wspec((D_EXPERT, D), 0),
                  wspec((D, D_EXPERT), 1), wspec((D, D_EXPERT), 1), wspec((D_EXPERT, D), 1)],
        out_specs=pl.BlockSpec((GMM_ROWS, D), lambda b, he, hu, nu: (b, 0)),
    )
    return pl.pallas_call(
        _gmm_kernel,
        grid_spec=grid_spec,
        out_shape=jax.ShapeDtypeStruct((R, D), BF16),
        compiler_params=pltpu.CompilerParams(
            dimension_semantics=("arbitrary",), vmem_limit_bytes=VMEM_LIMIT),
        name="moe_gmm",
    )(plan["half_expert"], plan["half_used"], plan["n_used"], xs, wg, wu, wd, wg, wu, wd)


def _combine_kernel(bs_ref, bd_ref, bn_ref, ss_ref, sd_ref, sn_ref, cm_ref, x_ref, ys_hbm, g3_ref, b3_ref,
                    o_ref, yt, sems):
    lists = (bs_ref, bd_ref, bn_ref, ss_ref, sd_ref, sn_ref)
    s = pl.program_id(0)
    ns = pl.num_programs(0)
    cur = (s % 2) * PER_STEP
    nxt = PER_STEP - cur

    def fetch(step, first_buf):
        for k in range(PER_STEP):
            _issue_copies(lists, step * PER_STEP + k, yt.at[first_buf + k], False, ys_hbm,
                          sems.at[first_buf + k])

    @pl.when(s == 0)
    def _():
        yt[...] = jnp.zeros_like(yt)
        fetch(0, 0)

    @pl.when(s + 1 < ns)
    def _():
        fetch(s + 1, nxt)

    for k in range(PER_STEP):
        _wait_copies(lists, s * PER_STEP + k, ys_hbm, yt.at[cur + k], sems.at[cur + k])

    col = lax.broadcasted_iota(jnp.int32, (MOE_TILE, SORT_ROWS), 1)
    for k in range(PER_STEP):
        rs = slice(k * MOE_TILE, (k + 1) * MOE_TILE)
        cm = cm_ref[rs, :]
        lp0 = cm[:, 0:1].astype(jnp.int32)
        lp1 = cm[:, 1:2].astype(jnp.int32)
        w = jnp.where(col == lp0, cm[:, 2:3], jnp.where(col == lp1, cm[:, 3:4], 0.0)).astype(BF16)
        y = jnp.dot(w, yt[cur + k], preferred_element_type=F32)
        o_ref[rs, :] = _layer_norm(ALPHA * x_ref[rs, :] + y, g3_ref[...], b3_ref[...])


def _combine(plan, route, x2, ys, g3, b3):
    T, D = x2.shape
    rows = PER_STEP * MOE_TILE
    grid_spec = pltpu.PrefetchScalarGridSpec(
        num_scalar_prefetch=6,
        grid=(T // rows,),
        in_specs=[pl.BlockSpec((rows, LANES), lambda t, *_: (t, 0)),
                  pl.BlockSpec((rows, D), lambda t, *_: (t, 0)),
                  pl.BlockSpec(memory_space=pl.ANY),
                  pl.BlockSpec((1, D), lambda t, *_: (0, 0)),
                  pl.BlockSpec((1, D), lambda t, *_: (0, 0))],
        out_specs=pl.BlockSpec((rows, D), lambda t, *_: (t, 0)),
        scratch_shapes=[pltpu.VMEM((2 * PER_STEP, SORT_ROWS, D), BF16),
                        pltpu.SemaphoreType.DMA((2 * PER_STEP,))],
    )
    return pl.pallas_call(
        _combine_kernel,
        grid_spec=grid_spec,
        out_shape=jax.ShapeDtypeStruct((T, D), F32),
        compiler_params=pltpu.CompilerParams(
            dimension_semantics=("arbitrary",), vmem_limit_bytes=VMEM_LIMIT),
        name="moe_combine",
    )(*_copy_lists(plan), route, x2, ys, g3, b3)


def _moe(x2, x2b, route, lpt, meta, wg, wu_b, wd_b, g3, b3):
    plan = _plan(meta, x2.shape[0])
    xs, wg_b = _dispatch(plan, lpt, x2b, wg)
    ys = _gmm(plan, xs, wg_b, wu_b, wd_b)
    return _combine(plan, route, x2, ys, g3, b3)


def _row(v):
    return v.reshape(1, -1).astype(F32)


def kernel(x, mem, w_in, b_in, w_dw, b_dw, g_conv_norm, b_conv_norm, attn_sinks, w_out, g_ln1, b_ln1,
           w_mq, w_mkv, w_mo, g_ln2, b_ln2, w_group, b_group, w_router, b_router, w_gate, w_up, w_down,
           g_ln3, b_ln3):
    B, S, D = x.shape
    for l in range(DEPTH):
        w_dw_p = jnp.zeros((CONV_HALO, CONV_CH), F32).at[:CONV_WIDTH].set(w_dw[l])
        x1, wd_b = _mixer(x, attn_sinks[l].astype(F32), w_in[l], _row(b_in[l]), w_dw_p,
                          _row(b_dw[l]), _row(g_conv_norm[l]), _row(b_conv_norm[l]),
                          w_out[l], _row(g_ln1[l]), _row(b_ln1[l]), w_down[l])

        kvm = _memkv(mem.reshape(B * MEM_LEN, D), w_mkv[l]).reshape(B, MEM_LEN, 2 * D)

        wr = jnp.concatenate(
            [w_group[l], jnp.transpose(w_router[l], (1, 0, 2)).reshape(D, N_EXPERTS)], axis=1)
        wr = jnp.pad(wr, ((0, 0), (0, LANES - wr.shape[1])))
        br = jnp.pad(jnp.concatenate([b_group[l], b_router[l].reshape(-1)]), (0, LANES - N_GROUPS - N_EXPERTS))
        x2, x2b, route, lpt, meta, wu_b = _memattn(x1, w_mq[l], kvm, w_mo[l], _row(g_ln2[l]), _row(b_ln2[l]),
                                                   wr.astype(F32), _row(br), w_up[l])

        T = B * S
        y = _moe(x2.reshape(T, D), x2b.reshape(T, D), route.reshape(T, LANES), lpt, meta,
                 w_gate[l], wu_b, wd_b,
                 _row(g_ln3[l]), _row(b_ln3[l]))
        x = y.reshape(B, S, D)
    return x
```

```python
import functools

import jax
import jax.numpy as jnp
from jax import lax
from jax.experimental import pallas as pl
from jax.experimental.pallas import tpu as pltpu

D_MODEL = 1024
MEM_LEN = 256
CONV_CH = 512
CONV_WIDTH = 31
N_HEADS = 8
N_KV_HEADS = 2
HEAD_DIM = 64
GQ = N_HEADS // N_KV_HEADS
ATTN_W = N_HEADS * HEAD_DIM
KV_W = N_KV_HEADS * HEAD_DIM
WINDOW = 128
D_MIX = CONV_CH + ATTN_W
D_IN = 2 * CONV_CH + ATTN_W + 2 * KV_W
MEM_HEADS = 4
MEM_HEAD_DIM = D_MODEL // MEM_HEADS
N_GROUPS = 4
EXPERTS_PER_GROUP = 4
N_EXPERTS = N_GROUPS * EXPERTS_PER_GROUP
D_EXPERT = D_MODEL // 2
DEPTH = 1
ALPHA = (2.0 * DEPTH) ** 0.25
LN_EPS = 1e-5

LANES = 128
SUBLANES = 8
CONV_ROWS = 128
LN_ROWS = 64
MEM_TILE = 1024
MASK_VALUE = -1e30
CONV_HALO = 32
SEQ_TILE = 512
MOE_TILE = 512
CHUNK = 16
REGION_ALIGN = 512
GMM_ROWS = 2 * REGION_ALIGN
TOP_K = 2
SORT_ROWS = -(-(MOE_TILE * TOP_K + N_EXPERTS * (CHUNK - 1)) // 256) * 256
BIG = 4
MAX_BIG = SORT_ROWS // (BIG * CHUNK)
MAX_SMALL = N_EXPERTS * (BIG - 1)
PER_STEP = 2
WAIT_GROUP = 8
ROUTE_OFF = N_GROUPS
VMEM_LIMIT = 56 * 1024 * 1024

BF16 = jnp.bfloat16
F32 = jnp.float32


def _layer_norm(x, g, b):
    mu = jnp.mean(x, axis=-1, keepdims=True)
    xc = x - mu
    var = jnp.mean(xc * xc, axis=-1, keepdims=True)
    return xc * lax.rsqrt(var + LN_EPS) * g + b


def _cast_bf16(dst_ref, src_ref):
    rows = 256
    for r0 in range(0, src_ref.shape[0], rows):
        dst_ref[r0:r0 + rows, :] = src_ref[r0:r0 + rows, :].astype(BF16)


def _dot_nt(a, b):
    return lax.dot_general(a, b, (((1,), (1,)), ((), ())), preferred_element_type=F32)


def _mixer_kernel(sinks_ref, x_ref, w_in_ref, b_in_ref, w_dw_ref, b_dw_ref, g_cn_ref, b_cn_ref,
                  w_out_ref, g1_ref, b1_ref, we_ref, o_ref, web_ref, w_in_b, w_out_b, hbuf, hshift, cbuf, qbuf,
                  kbuf, vbuf, ymix):
    i = pl.program_id(1)
    ts = SEQ_TILE
    web_ref[...] = we_ref[...].astype(BF16)

    @pl.when(jnp.logical_and(pl.program_id(0) == 0, i == 0))
    def _():
        _cast_bf16(w_in_b, w_in_ref)
        _cast_bf16(w_out_b, w_out_ref)

    @pl.when(i == 0)
    def _():
        hbuf[0:CONV_HALO, :] = jnp.zeros((CONV_HALO, CONV_CH), F32)
        kbuf[:, 0:WINDOW, :] = jnp.zeros((2 * N_KV_HEADS, WINDOW, KV_W), BF16)
        vbuf[:, 0:WINDOW, :] = jnp.zeros((2, WINDOW, KV_W), BF16)

    x = x_ref[0]
    u = jnp.dot(x.astype(BF16), w_in_b[...], preferred_element_type=F32) + b_in_ref[...]
    a = u[:, 0:CONV_CH]
    gate = u[:, CONV_CH:2 * CONV_CH]
    hbuf[CONV_HALO:CONV_HALO + ts, :] = a * jax.nn.sigmoid(gate)
    qbuf[...] = (u[:, 2 * CONV_CH:2 * CONV_CH + ATTN_W] * (HEAD_DIM ** -0.5)).astype(BF16)
    kf = u[:, 2 * CONV_CH + ATTN_W:2 * CONV_CH + ATTN_W + KV_W]
    vf = u[:, 2 * CONV_CH + ATTN_W + KV_W:D_IN]
    kr = pltpu.roll(kf, HEAD_DIM, axis=1)
    vr = pltpu.roll(vf, HEAD_DIM, axis=1)
    lo = lax.broadcasted_iota(jnp.int32, (ts, KV_W), 1) < HEAD_DIM
    rows = slice(WINDOW, WINDOW + ts)
    kbuf[0, rows, :] = jnp.where(lo, kf, 0.0).astype(BF16)
    kbuf[1, rows, :] = jnp.where(lo, 0.0, kr).astype(BF16)
    kbuf[2, rows, :] = jnp.where(lo, kr, 0.0).astype(BF16)
    kbuf[3, rows, :] = jnp.where(lo, 0.0, kf).astype(BF16)
    vbuf[0, rows, :] = vf.astype(BF16)
    vbuf[1, rows, :] = vr.astype(BF16)

    base = CONV_HALO - (CONV_WIDTH - 1)
    n_shift = ts + CONV_HALO - SUBLANES
    for b in range(1, SUBLANES):
        hshift[b - 1, 0:n_shift, :] = hbuf[b:b + n_shift, :]
    rc = CONV_ROWS

    def conv_chunk(c):
        r0 = c * rc
        for l in range(CONV_CH // LANES):
            ls = slice(l * LANES, (l + 1) * LANES)
            acc = jnp.zeros((rc, LANES), F32)
            for j in range(CONV_WIDTH):
                a8, b = divmod(j + base, SUBLANES)
                rs = slice(r0 + SUBLANES * a8, r0 + SUBLANES * a8 + rc)
                tap = hbuf[rs, ls] if b == 0 else hshift[b - 1, rs, ls]
                acc = acc + tap * w_dw_ref[j:j + 1, ls]
            cbuf[r0:r0 + rc, ls] = acc
        for r1 in range(r0, r0 + rc, LN_ROWS):
            rs = slice(r1, r1 + LN_ROWS)
            y = _layer_norm(cbuf[rs, :] + b_dw_ref[...], g_cn_ref[...], b_cn_ref[...])
            y = y * jax.nn.sigmoid(y)
            ymix[rs, 0:CONV_CH] = y.astype(BF16)

    qi = lax.broadcasted_iota(jnp.int32, (2 * WINDOW, 2 * WINDOW), 0) % WINDOW
    kj = lax.broadcasted_iota(jnp.int32, (2 * WINDOW, 2 * WINDOW), 1)
    dist = qi + WINDOW - kj
    band = (dist >= 0) & (dist < WINDOW)
    top = lax.broadcasted_iota(jnp.int32, (2 * WINDOW, 1), 0) < WINDOW
    lo_out = lax.broadcasted_iota(jnp.int32, (WINDOW, 2 * HEAD_DIM), 1) < HEAD_DIM

    def attn_block(jb):
        r0 = jb * WINDOW
        valid = band & jnp.logical_or(i != 0, kj >= WINDOW) if jb == 0 else band
        for kvh in range(N_KV_HEADS):
            h0 = kvh * GQ
            c0 = h0 * HEAD_DIM
            qs = jnp.concatenate([qbuf[r0:r0 + WINDOW, c0:c0 + 2 * HEAD_DIM],
                                  qbuf[r0:r0 + WINDOW, c0 + 2 * HEAD_DIM:c0 + 4 * HEAD_DIM]], axis=0)
            pv = []
            for par in range(2):
                kk = kbuf[2 * kvh + par, r0:r0 + 2 * WINDOW, :]
                vv = vbuf[(kvh + par) % 2, r0:r0 + 2 * WINDOW, :]
                s = jnp.where(valid, _dot_nt(qs, kk), MASK_VALUE)
                sink = jnp.where(top, sinks_ref[h0 + par], sinks_ref[h0 + 2 + par])
                m = jnp.maximum(jnp.max(s, axis=-1, keepdims=True), sink)
                p = jnp.exp(s - m)
                denom = jnp.sum(p, axis=-1, keepdims=True) + jnp.exp(sink - m)
                pv.append(jnp.dot(p.astype(BF16), vv, preferred_element_type=F32) / denom)
            for pair in range(2):
                rs = slice(pair * WINDOW, (pair + 1) * WINDOW)
                o = jnp.where(lo_out, pv[0][rs], pv[1][rs])
                cs = CONV_CH + c0 + pair * 2 * HEAD_DIM
                ymix[r0:r0 + WINDOW, cs:cs + 2 * HEAD_DIM] = o.astype(BF16)

    assert ts // rc == ts // WINDOW
    for c in range(ts // rc):
        attn_block(c)
        conv_chunk(c)

    mix = jnp.dot(ymix[...], w_out_b[...], preferred_element_type=F32)
    o_ref[0] = _layer_norm(ALPHA * x + mix, g1_ref[...], b1_ref[...])

    hbuf[0:CONV_HALO, :] = hbuf[ts:ts + CONV_HALO, :]
    kbuf[:, 0:WINDOW, :] = kbuf[:, ts:ts + WINDOW, :]
    vbuf[:, 0:WINDOW, :] = vbuf[:, ts:ts + WINDOW, :]


def _const_spec(shape):
    nd = len(shape)
    return pl.BlockSpec(shape, lambda *_: (0,) * nd)


def _resident_spec(shape):
    nd = len(shape)
    return pl.BlockSpec(shape, lambda *_: (0,) * nd, pipeline_mode=pl.Buffered(1))


def _mixer(x, sinks, w_in, b_in, w_dw, b_dw, g_cn, b_cn, w_out, g1, b1, w_expert):
    B, S, D = x.shape
    ts = SEQ_TILE
    tile = pl.BlockSpec((1, ts, D), lambda b, i: (b, i, 0))
    nt = S // ts
    n_e, we_rows, we_cols = w_expert.shape
    per = (B * nt) // n_e
    assert per * n_e == B * nt and we_rows % per == 0
    wes = pl.BlockSpec((1, we_rows // per, we_cols), lambda b, i: ((b * nt + i) // per, (b * nt + i) % per, 0))
    return pl.pallas_call(
        _mixer_kernel,
        grid=(B, S // ts),
        in_specs=[
            pl.BlockSpec(memory_space=pltpu.SMEM),
            tile,
            _resident_spec((D, D_IN)), _const_spec((1, D_IN)),
            _const_spec((CONV_HALO, CONV_CH)), _const_spec((1, CONV_CH)),
            _const_spec((1, CONV_CH)), _const_spec((1, CONV_CH)),
            _resident_spec((D_MIX, D)), _const_spec((1, D)), _const_spec((1, D)),
            wes,
        ],
        out_specs=[tile, wes],
        out_shape=[jax.ShapeDtypeStruct((B, S, D), F32), jax.ShapeDtypeStruct(w_expert.shape, BF16)],
        scratch_shapes=[
            pltpu.VMEM((D, D_IN), BF16),
            pltpu.VMEM((D_MIX, D), BF16),
            pltpu.VMEM((CONV_HALO + ts, CONV_CH), F32),
            pltpu.VMEM((SUBLANES - 1, CONV_HALO + ts, CONV_CH), F32),
            pltpu.VMEM((ts, CONV_CH), F32),
            pltpu.VMEM((ts, ATTN_W), BF16),
            pltpu.VMEM((2 * N_KV_HEADS, WINDOW + ts, KV_W), BF16),
            pltpu.VMEM((2, WINDOW + ts, KV_W), BF16),
            pltpu.VMEM((ts, D_MIX), BF16),
        ],
        compiler_params=pltpu.CompilerParams(
            dimension_semantics=("arbitrary", "arbitrary"), vmem_limit_bytes=VMEM_LIMIT),
        name="mixer",
    )(sinks, x, w_in, b_in, w_dw, b_dw, g_cn, b_cn, w_out, g1, b1, w_expert)


def _memkv_kernel(mem_ref, w_ref, o_ref):
    o_ref[...] = jnp.dot(mem_ref[...].astype(BF16), w_ref[...].astype(BF16),
                         preferred_element_type=F32).astype(BF16)


def _memkv(mem2d, w_mkv):
    M, D = mem2d.shape
    N = w_mkv.shape[1]
    tn = 512
    return pl.pallas_call(
        _memkv_kernel,
        grid=(N // tn,),
        in_specs=[pl.BlockSpec((M, D), lambda j: (0, 0)), pl.BlockSpec((D, tn), lambda j: (0, j))],
        out_specs=pl.BlockSpec((M, tn), lambda j: (0, j)),
        out_shape=jax.ShapeDtypeStruct((M, N), BF16),
        compiler_params=pltpu.CompilerParams(dimension_semantics=("arbitrary",)),
        name="memkv",
    )(mem2d, w_mkv)


def _first_max(rows):
    best = rows[0]
    for r in rows[1:]:
        best = jnp.maximum(best, r)
    idx = jnp.full(best.shape, len(rows) - 1, jnp.int32)
    for k in range(len(rows) - 2, -1, -1):
        idx = jnp.where(rows[k] == best, k, idx)
    return best, idx


def _route_plan(logits_t):
    tile = logits_t.shape[1]
    row = lambda k: logits_t[k:k + 1, :]
    gmax, g_idx = _first_max([row(g) for g in range(N_GROUPS)])
    gsum = jnp.exp(row(0) - gmax)
    for g in range(1, N_GROUPS):
        gsum = gsum + jnp.exp(row(g) - gmax)
    g_p = 1.0 / gsum
    rl = []
    for e in range(EXPERTS_PER_GROUP):
        v = row(ROUTE_OFF + (N_GROUPS - 1) * EXPERTS_PER_GROUP + e)
        for g in range(N_GROUPS - 2, -1, -1):
            v = jnp.where(g_idx == g, row(ROUTE_OFF + g * EXPERTS_PER_GROUP + e), v)
        rl.append(v)
    m1, i1 = _first_max(rl)
    m2, i2 = _first_max([jnp.where(i1 == e, MASK_VALUE, rl[e]) for e in range(EXPERTS_PER_GROUP)])
    ex = jnp.exp(m2 - m1)
    w1 = 1.0 / (1.0 + ex)
    w2 = ex * w1
    e1 = g_idx * EXPERTS_PER_GROUP + i1
    e2 = g_idx * EXPERTS_PER_GROUP + i2

    eid = lax.broadcasted_iota(jnp.int32, (N_EXPERTS, tile), 0)
    hit1 = eid == e1
    hit2 = eid == e2
    oh = jnp.where(jnp.logical_or(hit1, hit2), 1.0, 0.0)
    r = lax.broadcasted_iota(jnp.int32, (tile, tile), 0)
    c = lax.broadcasted_iota(jnp.int32, (tile, tile), 1)
    tri = jnp.where(r <= c, 1.0, 0.0).astype(BF16)
    csum = jnp.dot(oh.astype(BF16), tri, preferred_element_type=F32)
    counts = jnp.broadcast_to(csum[:, tile - 1:tile], (N_EXPERTS, tile)).astype(jnp.int32)
    nch = jnp.right_shift(counts + (CHUNK - 1), CHUNK.bit_length() - 1)
    er = lax.broadcasted_iota(jnp.int32, (N_EXPERTS, N_EXPERTS), 0)
    ec = lax.broadcasted_iota(jnp.int32, (N_EXPERTS, N_EXPERTS), 1)
    lower = jnp.where(ec < er, 1.0, 0.0).astype(BF16)
    off = jnp.dot(lower, nch.astype(F32).astype(BF16), preferred_element_type=F32) * CHUNK
    pos = off + csum - oh
    lp1 = jnp.sum(jnp.where(hit1, pos, 0.0), axis=0, keepdims=True)
    lp2 = jnp.sum(jnp.where(hit2, pos, 0.0), axis=0, keepdims=True)
    zero = jnp.zeros_like(lp1)
    route_t = jnp.concatenate([lp1, lp2, g_p * w1, g_p * w2, zero, zero, zero, zero], axis=0)
    meta = jnp.concatenate([nch[:, 0:LANES], off[:, 0:LANES].astype(jnp.int32)], axis=0)
    return route_t, meta


def _memattn_kernel(x_ref, wq_ref, k_ref, v_ref, wo_ref, g2_ref, b2_ref, wr2_ref, wrh_ref, br_ref, we_ref,
                    o_ref, ob_ref, route_ref, routet_ref, meta_ref, web_ref, wq_b, wo_b):
    web_ref[...] = we_ref[...].astype(BF16)

    @pl.when(jnp.logical_and(pl.program_id(0) == 0, pl.program_id(1) == 0))
    def _():
        _cast_bf16(wq_b, wq_ref)
        _cast_bf16(wo_b, wo_ref)

    def rows_logits(rs):
        x = x_ref[0, rs, :]
        q = jnp.dot(x.astype(BF16), wq_b[...], preferred_element_type=F32)
        q = (q * (MEM_HEAD_DIM ** -0.5)).astype(BF16)
        outs = []
        for h in range(MEM_HEADS):
            sl = slice(h * MEM_HEAD_DIM, (h + 1) * MEM_HEAD_DIM)
            s = _dot_nt(q[:, sl], k_ref[0, :, sl])
            m = jnp.max(s, axis=-1, keepdims=True)
            p = jnp.exp(s - m)
            denom = jnp.sum(p, axis=-1, keepdims=True)
            o = jnp.dot(p.astype(BF16), v_ref[0, :, sl], preferred_element_type=F32)
            outs.append((o / denom).astype(BF16))
        o = jnp.dot(jnp.concatenate(outs, axis=-1), wo_b[...], preferred_element_type=F32)
        x2 = _layer_norm(ALPHA * x + o, g2_ref[...], b2_ref[...])
        o_ref[0, rs, :] = x2
        x2h = x2.astype(BF16)
        ob_ref[0, rs, :] = x2h
        x2l = (x2 - x2h.astype(F32)).astype(BF16)
        hh = jnp.dot(x2h, wr2_ref[...], preferred_element_type=F32)
        return (hh[:, 0:LANES] + hh[:, LANES:2 * LANES]
                + jnp.dot(x2l, wrh_ref[...], preferred_element_type=F32) + br_ref[...])

    ts = x_ref.shape[1]
    groups = [slice(r0, r0 + MOE_TILE) for r0 in range(0, ts, MOE_TILE)]
    logits = rows_logits(slice(0, ts))
    pad = jnp.zeros((LANES - SUBLANES, MOE_TILE), F32)
    for k, rs in enumerate(groups):
        route_t, meta = _route_plan(jnp.transpose(logits[rs, :]))
        routet_ref[k] = route_t
        route_ref[0, rs, :] = jnp.transpose(jnp.concatenate([route_t, pad], axis=0))
        meta_ref[k] = meta


def _memattn(x1, wq, kvm, wo, g2, b2, wr, br, w_expert):
    B, S, D = x1.shape
    ts = MEM_TILE
    assert B * (S // ts) == w_expert.shape[0]
    wes = pl.BlockSpec((1,) + w_expert.shape[1:], lambda b, i: (b * (S // ts) + i, 0, 0))
    per = ts // MOE_TILE
    nt = S // ts
    c = wr * (2.0 ** 16 + 1.0)
    w_high = c - (c - wr)
    wrh = w_high.astype(BF16)
    wr2 = jnp.concatenate([wrh, (wr - w_high).astype(BF16)], axis=1)
    tile = pl.BlockSpec((1, ts, D), lambda b, i: (b, i, 0))
    kspec = pl.BlockSpec((1, MEM_LEN, D), lambda b, i: (b, 0, 0))
    vspec = pl.BlockSpec((1, MEM_LEN, D), lambda b, i: (b, 0, 1))
    return pl.pallas_call(
        _memattn_kernel,
        grid=(B, nt),
        in_specs=[tile, _resident_spec((D, D)), kspec, vspec, _resident_spec((D, D)),
                  _const_spec((1, D)), _const_spec((1, D)),
                  _const_spec((D, 2 * LANES)), _const_spec((D, LANES)), _const_spec((1, LANES)), wes],
        out_specs=[tile, tile, pl.BlockSpec((1, ts, LANES), lambda b, i: (b, i, 0)),
                   pl.BlockSpec((per, SUBLANES, MOE_TILE), lambda b, i: (b * nt + i, 0, 0)),
                   pl.BlockSpec((per, 2 * N_EXPERTS, LANES), lambda b, i: (b * nt + i, 0, 0)), wes],
        out_shape=[jax.ShapeDtypeStruct((B, S, D), F32),
                   jax.ShapeDtypeStruct((B, S, D), BF16),
                   jax.ShapeDtypeStruct((B, S, LANES), F32),
                   jax.ShapeDtypeStruct((B * nt * per, SUBLANES, MOE_TILE), F32),
                   jax.ShapeDtypeStruct((B * nt * per, 2 * N_EXPERTS, LANES), jnp.int32),
                   jax.ShapeDtypeStruct(w_expert.shape, BF16)],
        scratch_shapes=[pltpu.VMEM((D, D), BF16), pltpu.VMEM((D, D), BF16)],
        compiler_params=pltpu.CompilerParams(
            dimension_semantics=("arbitrary", "arbitrary"), vmem_limit_bytes=VMEM_LIMIT),
        name="memattn",
    )(x1, wq, kvm, kvm, wo, g2, b2, wr2, wrh, br, w_expert)


def _gmm_blocks(n_tokens):
    rows = (n_tokens * TOP_K + (n_tokens // MOE_TILE) * N_EXPERTS * (CHUNK - 1)
            + N_EXPERTS * (REGION_ALIGN - CHUNK))
    return -(-rows // GMM_ROWS)


def _plan(meta, T):
    nch = meta[:, :N_EXPERTS, 0]
    n16 = nch * CHUNK
    n_e = jnp.sum(n16, axis=0)
    reg = (n_e + REGION_ALIGN - 1) // REGION_ALIGN * REGION_ALIGN
    gend = jnp.cumsum(reg)
    gbase = gend - reg
    dst = gbase[None, :] + jnp.cumsum(n16, axis=0) - n16
    half_row = jnp.arange(2 * _gmm_blocks(T), dtype=jnp.int32) * REGION_ALIGN
    half_expert = jnp.minimum(jnp.sum(half_row[:, None] >= gend[None, :], axis=1), N_EXPERTS - 1)
    half_used = half_row < gend[-1]
    n_used = ((gend[-1] + GMM_ROWS - 1) // GMM_ROWS).astype(jnp.int32).reshape(1)
    src = (jnp.cumsum(nch, axis=1) - nch) * CHUNK
    n_big = nch // BIG

    def copy_list(count, src0, dst0, rows, length):
        cum = jnp.cumsum(count, axis=1)
        first = (cum - count)[:, None, :]
        k = jnp.arange(length, dtype=jnp.int32)[None, :, None]
        mine = (k >= first) & (k < cum[:, None, :])
        step = (k - first) * rows
        pick = lambda base: jnp.sum(jnp.where(mine, base[:, None, :] + step, 0), axis=2)
        return pick(src0), pick(dst0), cum[:, -1]

    big_src, big_dst, big_n = copy_list(n_big, src, dst, BIG * CHUNK, MAX_BIG)
    rest = n_big * (BIG * CHUNK)
    small_src, small_dst, small_n = copy_list(nch - n_big * BIG, src + rest, dst + rest, CHUNK, MAX_SMALL)
    i32 = lambda a: a.astype(jnp.int32)
    fill_start = jnp.concatenate([gbase + n_e, gend[-1:]])
    fill_n = jnp.concatenate([reg - n_e, _gmm_blocks(T) * GMM_ROWS - gend[-1:]]) // CHUNK
    return dict(big_src=i32(big_src).reshape(-1), big_dst=i32(big_dst).reshape(-1), big_n=i32(big_n),
                small_src=i32(small_src).reshape(-1), small_dst=i32(small_dst).reshape(-1), small_n=i32(small_n),
                fill_start=i32(fill_start), fill_n=i32(fill_n),
                fill_tot=i32(jnp.sum(fill_n)).reshape(1), half_expert=i32(half_expert),
                half_used=i32(half_used), n_used=n_used)


def _rows_copy(src_ref, src_row, dst_ref, dst_row, rows, sem):
    return pltpu.make_async_copy(
        src_ref.at[pl.ds(pl.multiple_of(src_row, CHUNK), rows), :],
        dst_ref.at[pl.ds(pl.multiple_of(dst_row, CHUNK), rows), :], sem)


def _chunk_copy(src_ref, src_row, dst_ref, dst_row, sem):
    return _rows_copy(src_ref, src_row, dst_ref, dst_row, CHUNK, sem)


def _issue_copies(lists, tile, tile_ref, tile_is_src, hbm_ref, sem):
    big_src, big_dst, big_n, small_src, small_dst, small_n = lists
    for src_l, dst_l, n_l, length, rows in ((big_src, big_dst, big_n, MAX_BIG, BIG * CHUNK),
                                            (small_src, small_dst, small_n, MAX_SMALL, CHUNK)):
        def issue(k, carry, src_l=src_l, dst_l=dst_l, length=length, rows=rows):
            local, remote = src_l[tile * length + k], dst_l[tile * length + k]
            if tile_is_src:
                _rows_copy(tile_ref, local, hbm_ref, remote, rows, sem).start()
            else:
                _rows_copy(hbm_ref, remote, tile_ref, local, rows, sem).start()
            return carry

        lax.fori_loop(0, n_l[tile], issue, 0)


def _wait_rows(n, rows, src_ref, dst_ref, sem):
    def body(c, carry):
        pltpu.make_async_copy(src_ref.at[pl.ds(0, rows), :], dst_ref.at[pl.ds(0, rows), :], sem).wait()
        return carry

    lax.fori_loop(0, n, body, 0)


def _wait_copies(lists, tile, src_ref, dst_ref, sem, extra_chunks=0):
    _wait_rows(lists[2][tile], BIG * CHUNK, src_ref, dst_ref, sem)
    n = lists[5][tile] + extra_chunks
    _wait_rows(n // WAIT_GROUP, WAIT_GROUP * CHUNK, src_ref, dst_ref, sem)
    _wait_rows(n % WAIT_GROUP, CHUNK, src_ref, dst_ref, sem)


def _dispatch_kernel(bs_ref, bd_ref, bn_ref, ss_ref, sd_ref, sn_ref, fstart_ref, fn_ref, ftot_ref,
                     lp_ref, x_ref, we_ref, xs_hbm, web_ref, xt, zbuf, sems):
    lists = (bs_ref, bd_ref, bn_ref, ss_ref, sd_ref, sn_ref)
    s = pl.program_id(0)
    ns = pl.num_programs(0)
    cur = (s % 2) * PER_STEP
    prv = PER_STEP - cur

    @pl.when(s >= 2)
    def _():
        for k in range(PER_STEP):
            _wait_copies(lists, (s - 2) * PER_STEP + k, xt.at[cur + k], xs_hbm, sems.at[cur + k])

    r = lax.broadcasted_iota(jnp.int32, (SORT_ROWS, MOE_TILE), 0)
    for k in range(PER_STEP):
        lp = lp_ref[k].astype(jnp.int32)
        hit = jnp.logical_or(lp[0:1, :] == r, lp[1:2, :] == r)
        p = jnp.where(hit, 1.0, 0.0).astype(BF16)
        x = x_ref[k * MOE_TILE:(k + 1) * MOE_TILE, :]
        xt[cur + k] = jnp.dot(p, x, preferred_element_type=F32).astype(BF16)

    for k in range(PER_STEP):
        _issue_copies(lists, s * PER_STEP + k, xt.at[cur + k], True, xs_hbm, sems.at[cur + k])

    _cast_bf16(web_ref.at[0], we_ref.at[0])

    @pl.when(s == ns - 1)
    def _():
        zbuf[...] = jnp.zeros_like(zbuf)
        sem = sems.at[cur]

        def per_range(e, carry):
            def issue(c, carry2):
                _chunk_copy(zbuf, 0, xs_hbm, fstart_ref[e] + c * CHUNK, sem).start()
                return carry2

            return lax.fori_loop(0, fn_ref[e], issue, carry)

        lax.fori_loop(0, N_EXPERTS + 1, per_range, 0)
        for k in range(PER_STEP):
            _wait_copies(lists, (s - 1) * PER_STEP + k, xt.at[prv + k], xs_hbm, sems.at[prv + k])
            _wait_copies(lists, s * PER_STEP + k, xt.at[cur + k], xs_hbm, sems.at[cur + k],
                         extra_chunks=ftot_ref[0] if k == 0 else 0)


def _copy_lists(plan):
    return tuple(plan[k] for k in ("big_src", "big_dst", "big_n", "small_src", "small_dst", "small_n"))


def _dispatch(plan, lpt, x2b, w_expert):
    T, D = x2b.shape
    rows = PER_STEP * MOE_TILE
    assert T // rows >= 2
    assert T // rows == N_EXPERTS
    wes = pl.BlockSpec((1,) + w_expert.shape[1:], lambda t, *_: (t, 0, 0))
    grid_spec = pltpu.PrefetchScalarGridSpec(
        num_scalar_prefetch=9,
        grid=(T // rows,),
        in_specs=[pl.BlockSpec((PER_STEP, SUBLANES, MOE_TILE), lambda t, *_: (t, 0, 0)),
                  pl.BlockSpec((rows, D), lambda t, *_: (t, 0)), wes],
        out_specs=[pl.BlockSpec(memory_space=pl.ANY), wes],
        scratch_shapes=[pltpu.VMEM((2 * PER_STEP, SORT_ROWS, D), BF16), pltpu.VMEM((CHUNK, D), BF16),
                        pltpu.SemaphoreType.DMA((2 * PER_STEP,))],
    )
    return pl.pallas_call(
        _dispatch_kernel,
        grid_spec=grid_spec,
        out_shape=[jax.ShapeDtypeStruct((_gmm_blocks(T) * GMM_ROWS, D), BF16),
                   jax.ShapeDtypeStruct(w_expert.shape, BF16)],
        compiler_params=pltpu.CompilerParams(
            dimension_semantics=("arbitrary",), vmem_limit_bytes=VMEM_LIMIT),
        name="moe_dispatch",
    )(*_copy_lists(plan), plan["fill_start"], plan["fill_n"], plan["fill_tot"], lpt, x2b, w_expert)


def _expert_ffn(xb, wg_b, wu_b, wd_b):
    g = jnp.dot(xb, wg_b[...], preferred_element_type=F32)
    u = jnp.dot(xb, wu_b[...], preferred_element_type=F32)
    h = (g * jax.nn.sigmoid(g)) * u
    return jnp.dot(h.astype(BF16), wd_b[...], preferred_element_type=F32).astype(BF16)


def _gmm_kernel(he_ref, hu_ref, nu_ref, x_ref, wga_ref, wua_ref, wda_ref, wgb_ref, wub_ref, wdb_ref, o_ref):
    b = pl.program_id(0)
    half = REGION_ALIGN
    same = he_ref[2 * b] == he_ref[2 * b + 1]
    used0 = hu_ref[2 * b] != 0
    used1 = hu_ref[2 * b + 1] != 0
    set_a = (wga_ref.at[0], wua_ref.at[0], wda_ref.at[0])
    set_b = (wgb_ref.at[0], wub_ref.at[0], wdb_ref.at[0])

    @pl.when(jnp.logical_and(used1, same))
    def _():
        o_ref[...] = _expert_ffn(x_ref[...], *set_a)

    @pl.when(jnp.logical_and(used0, jnp.logical_not(jnp.logical_and(used1, same))))
    def _():
        o_ref[0:half, :] = _expert_ffn(x_ref[0:half, :], *set_a)

    @pl.when(jnp.logical_and(used1, jnp.logical_not(same)))
    def _():
        o_ref[half:GMM_ROWS, :] = _expert_ffn(x_ref[half:GMM_ROWS, :], *set_b)

    @pl.when(jnp.logical_not(used0))
    def _():
        o_ref[0:half, :] = jnp.zeros((half, o_ref.shape[1]), o_ref.dtype)

    @pl.when(jnp.logical_not(used1))
    def _():
        o_ref[half:GMM_ROWS, :] = jnp.zeros((GMM_ROWS - half, o_ref.shape[1]), o_ref.dtype)


def _gmm(plan, xs, wg, wu, wd):
    R, D = xs.shape
    rows = pl.BlockSpec((GMM_ROWS, D), lambda b, he, hu, nu: (jnp.minimum(b, nu[0] - 1), 0))

    def wspec(shape, h):
        return pl.BlockSpec((1,) + shape, lambda b, he, hu, nu: (he[2 * b + h], 0, 0))

    grid_spec = pltpu.PrefetchScalarGridSpec(
        num_scalar_prefetch=3,
        grid=(R // GMM_ROWS,),
        in_specs=[rows,
                  wspec((D, D_EXPERT), 0), wspec((D, D_EXPERT), 0), wspec((D_EXPERT, D), 0),
                  wspec((D, D_EXPERT), 1), wspec((D, D_EXPERT), 1), wspec((D_EXPERT, D), 1)],
        out_specs=pl.BlockSpec((GMM_ROWS, D), lambda b, he, hu, nu: (b, 0)),
    )
    return pl.pallas_call(
        _gmm_kernel,
        grid_spec=grid_spec,
        out_shape=jax.ShapeDtypeStruct((R, D), BF16),
        compiler_params=pltpu.CompilerParams(
            dimension_semantics=("arbitrary",), vmem_limit_bytes=VMEM_LIMIT),
        name="moe_gmm",
    )(plan["half_expert"], plan["half_used"], plan["n_used"], xs, wg, wu, wd, wg, wu, wd)


def _combine_kernel(bs_ref, bd_ref, bn_ref, ss_ref, sd_ref, sn_ref, cm_ref, x_ref, ys_hbm, g3_ref, b3_ref,
                    o_ref, yt, sems):
    lists = (bs_ref, bd_ref, bn_ref, ss_ref, sd_ref, sn_ref)
    s = pl.program_id(0)
    ns = pl.num_programs(0)
    cur = (s % 2) * PER_STEP
    nxt = PER_STEP - cur

    def fetch(step, first_buf):
        for k in range(PER_STEP):
            _issue_copies(lists, step * PER_STEP + k, yt.at[first_buf + k], False, ys_hbm,
                          sems.at[first_buf + k])

    @pl.when(s == 0)
    def _():
        yt[...] = jnp.zeros_like(yt)
        fetch(0, 0)

    @pl.when(s + 1 < ns)
    def _():
        fetch(s + 1, nxt)

    for k in range(PER_STEP):
        _wait_copies(lists, s * PER_STEP + k, ys_hbm, yt.at[cur + k], sems.at[cur + k])

    col = lax.broadcasted_iota(jnp.int32, (MOE_TILE, SORT_ROWS), 1)
    for k in range(PER_STEP):
        rs = slice(k * MOE_TILE, (k + 1) * MOE_TILE)
        cm = cm_ref[rs, :]
        lp0 = cm[:, 0:1].astype(jnp.int32)
        lp1 = cm[:, 1:2].astype(jnp.int32)
        w = jnp.where(col == lp0, cm[:, 2:3], jnp.where(col == lp1, cm[:, 3:4], 0.0)).astype(BF16)
        y = jnp.dot(w, yt[cur + k], preferred_element_type=F32)
        o_ref[rs, :] = _layer_norm(ALPHA * x_ref[rs, :] + y, g3_ref[...], b3_ref[...])


def _combine(plan, route, x2, ys, g3, b3):
    T, D = x2.shape
    rows = PER_STEP * MOE_TILE
    grid_spec = pltpu.PrefetchScalarGridSpec(
        num_scalar_prefetch=6,
        grid=(T // rows,),
        in_specs=[pl.BlockSpec((rows, LANES), lambda t, *_: (t, 0)),
                  pl.BlockSpec((rows, D), lambda t, *_: (t, 0)),
                  pl.BlockSpec(memory_space=pl.ANY),
                  pl.BlockSpec((1, D), lambda t, *_: (0, 0)),
                  pl.BlockSpec((1, D), lambda t, *_: (0, 0))],
        out_specs=pl.BlockSpec((rows, D), lambda t, *_: (t, 0)),
        scratch_shapes=[pltpu.VMEM((2 * PER_STEP, SORT_ROWS, D), BF16),
                        pltpu.SemaphoreType.DMA((2 * PER_STEP,))],
    )
    return pl.pallas_call(
        _combine_kernel,
        grid_spec=grid_spec,
        out_shape=jax.ShapeDtypeStruct((T, D), F32),
        compiler_params=pltpu.CompilerParams(
            dimension_semantics=("arbitrary",), vmem_limit_bytes=VMEM_LIMIT),
        name="moe_combine",
    )(*_copy_lists(plan), route, x2, ys, g3, b3)


def _moe(x2, x2b, route, lpt, meta, wg, wu_b, wd_b, g3, b3):
    plan = _plan(meta, x2.shape[0])
    xs, wg_b = _dispatch(plan, lpt, x2b, wg)
    ys = _gmm(plan, xs, wg_b, wu_b, wd_b)
    return _combine(plan, route, x2, ys, g3, b3)


def _row(v):
    return v.reshape(1, -1).astype(F32)


def kernel(x, mem, w_in, b_in, w_dw, b_dw, g_conv_norm, b_conv_norm, attn_sinks, w_out, g_ln1, b_ln1,
           w_mq, w_mkv, w_mo, g_ln2, b_ln2, w_group, b_group, w_router, b_router, w_gate, w_up, w_down,
           g_ln3, b_ln3):
    B, S, D = x.shape
    for l in range(DEPTH):
        w_dw_p = jnp.zeros((CONV_HALO, CONV_CH), F32).at[:CONV_WIDTH].set(w_dw[l])
        x1, wd_b = _mixer(x, attn_sinks[l].astype(F32), w_in[l], _row(b_in[l]), w_dw_p,
                          _row(b_dw[l]), _row(g_conv_norm[l]), _row(b_conv_norm[l]),
                          w_out[l], _row(g_ln1[l]), _row(b_ln1[l]), w_down[l])

        kvm = _memkv(mem.reshape(B * MEM_LEN, D), w_mkv[l]).reshape(B, MEM_LEN, 2 * D)

        wr = jnp.concatenate(
            [w_group[l], jnp.transpose(w_router[l], (1, 0, 2)).reshape(D, N_EXPERTS)], axis=1)
        wr = jnp.pad(wr, ((0, 0), (0, LANES - wr.shape[1])))
        br = jnp.pad(jnp.concatenate([b_group[l], b_router[l].reshape(-1)]), (0, LANES - N_GROUPS - N_EXPERTS))
        x2, x2b, route, lpt, meta, wu_b = _memattn(x1, w_mq[l], kvm, w_mo[l], _row(g_ln2[l]), _row(b_ln2[l]),
                                                   wr.astype(F32), _row(br), w_up[l])

        T = B * S
        y = _moe(x2.reshape(T, D), x2b.reshape(T, D), route.reshape(T, LANES), lpt, meta,
                 w_gate[l], wu_b, wd_b,
                 _row(g_ln3[l]), _row(b_ln3[l]))
        x = y.reshape(B, S, D)
    return x
```

```python
import functools

import jax
import jax.numpy as jnp
from jax import lax
from jax.experimental import pallas as pl
from jax.experimental.pallas import tpu as pltpu

D_MODEL = 1024
MEM_LEN = 256
CONV_CH = 512
CONV_WIDTH = 31
N_HEADS = 8
N_KV_HEADS = 2
HEAD_DIM = 64
GQ = N_HEADS // N_KV_HEADS
ATTN_W = N_HEADS * HEAD_DIM
KV_W = N_KV_HEADS * HEAD_DIM
WINDOW = 128
D_MIX = CONV_CH + ATTN_W
D_IN = 2 * CONV_CH + ATTN_W + 2 * KV_W
MEM_HEADS = 4
MEM_HEAD_DIM = D_MODEL // MEM_HEADS
N_GROUPS = 4
EXPERTS_PER_GROUP = 4
N_EXPERTS = N_GROUPS * EXPERTS_PER_GROUP
D_EXPERT = D_MODEL // 2
DEPTH = 1
ALPHA = (2.0 * DEPTH) ** 0.25
LN_EPS = 1e-5

LANES = 128
SUBLANES = 8
CONV_ROWS = 128
LN_ROWS = 64
MEM_TILE = 1024
MASK_VALUE = -1e30
CONV_HALO = 32
SEQ_TILE = 512
MOE_TILE = 512
CHUNK = 16
REGION_ALIGN = 512
GMM_ROWS = 2 * REGION_ALIGN
TOP_K = 2
SORT_ROWS = -(-(MOE_TILE * TOP_K + N_EXPERTS * (CHUNK - 1)) // 256) * 256
BIG = 4
MAX_BIG = SORT_ROWS // (BIG * CHUNK)
MAX_SMALL = N_EXPERTS * (BIG - 1)
PER_STEP = 2
WAIT_GROUP = 8
N_SIDE = 4
ROUTE_OFF = N_GROUPS
VMEM_LIMIT = 56 * 1024 * 1024

BF16 = jnp.bfloat16
F32 = jnp.float32


def _layer_norm(x, g, b):
    mu = jnp.mean(x, axis=-1, keepdims=True)
    xc = x - mu
    var = jnp.mean(xc * xc, axis=-1, keepdims=True)
    return xc * lax.rsqrt(var + LN_EPS) * g + b


def _cast_bf16(dst_ref, src_ref):
    rows = 256
    for r0 in range(0, src_ref.shape[0], rows):
        dst_ref[r0:r0 + rows, :] = src_ref[r0:r0 + rows, :].astype(BF16)


def _dot_nt(a, b):
    return lax.dot_general(a, b, (((1,), (1,)), ((), ())), preferred_element_type=F32)


def _mixer_kernel(sinks_ref, x_ref, w_in_ref, b_in_ref, w_dw_ref, b_dw_ref, g_cn_ref, b_cn_ref,
                  *refs):
    side_in, ymix, side_out = refs[:N_SIDE], refs[N_SIDE], refs[N_SIDE + 1:2 * N_SIDE + 1]
    w_in_b, hbuf, hshift, cbuf, qbuf, kbuf, vbuf = refs[2 * N_SIDE + 1:]
    i = pl.program_id(1)
    ts = SEQ_TILE
    for src, dst in zip(side_in, side_out):
        dst[...] = src[...].astype(BF16)

    @pl.when(jnp.logical_and(pl.program_id(0) == 0, i == 0))
    def _():
        _cast_bf16(w_in_b, w_in_ref)

    @pl.when(i == 0)
    def _():
        hbuf[0:CONV_HALO, :] = jnp.zeros((CONV_HALO, CONV_CH), F32)
        kbuf[:, 0:WINDOW, :] = jnp.zeros((2 * N_KV_HEADS, WINDOW, KV_W), BF16)
        vbuf[:, 0:WINDOW, :] = jnp.zeros((2, WINDOW, KV_W), BF16)

    x = x_ref[0]
    u = jnp.dot(x.astype(BF16), w_in_b[...], preferred_element_type=F32) + b_in_ref[...]
    a = u[:, 0:CONV_CH]
    gate = u[:, CONV_CH:2 * CONV_CH]
    hbuf[CONV_HALO:CONV_HALO + ts, :] = a * jax.nn.sigmoid(gate)
    qbuf[...] = (u[:, 2 * CONV_CH:2 * CONV_CH + ATTN_W] * (HEAD_DIM ** -0.5)).astype(BF16)
    kf = u[:, 2 * CONV_CH + ATTN_W:2 * CONV_CH + ATTN_W + KV_W]
    vf = u[:, 2 * CONV_CH + ATTN_W + KV_W:D_IN]
    kr = pltpu.roll(kf, HEAD_DIM, axis=1)
    vr = pltpu.roll(vf, HEAD_DIM, axis=1)
    lo = lax.broadcasted_iota(jnp.int32, (ts, KV_W), 1) < HEAD_DIM
    rows = slice(WINDOW, WINDOW + ts)
    kbuf[0, rows, :] = jnp.where(lo, kf, 0.0).astype(BF16)
    kbuf[1, rows, :] = jnp.where(lo, 0.0, kr).astype(BF16)
    kbuf[2, rows, :] = jnp.where(lo, kr, 0.0).astype(BF16)
    kbuf[3, rows, :] = jnp.where(lo, 0.0, kf).astype(BF16)
    vbuf[0, rows, :] = vf.astype(BF16)
    vbuf[1, rows, :] = vr.astype(BF16)

    base = CONV_HALO - (CONV_WIDTH - 1)
    n_shift = ts + CONV_HALO - SUBLANES
    for b in range(1, SUBLANES):
        hshift[b - 1, 0:n_shift, :] = hbuf[b:b + n_shift, :]
    rc = CONV_ROWS

    def conv_chunk(c):
        r0 = c * rc
        for l in range(CONV_CH // LANES):
            ls = slice(l * LANES, (l + 1) * LANES)
            acc = jnp.zeros((rc, LANES), F32)
            for j in range(CONV_WIDTH):
                a8, b = divmod(j + base, SUBLANES)
                rs = slice(r0 + SUBLANES * a8, r0 + SUBLANES * a8 + rc)
                tap = hbuf[rs, ls] if b == 0 else hshift[b - 1, rs, ls]
                acc = acc + tap * w_dw_ref[j:j + 1, ls]
            cbuf[r0:r0 + rc, ls] = acc
        for r1 in range(r0, r0 + rc, LN_ROWS):
            rs = slice(r1, r1 + LN_ROWS)
            y = _layer_norm(cbuf[rs, :] + b_dw_ref[...], g_cn_ref[...], b_cn_ref[...])
            y = y * jax.nn.sigmoid(y)
            ymix[0, rs, 0:CONV_CH] = y.astype(BF16)

    qi = lax.broadcasted_iota(jnp.int32, (2 * WINDOW, 2 * WINDOW), 0) % WINDOW
    kj = lax.broadcasted_iota(jnp.int32, (2 * WINDOW, 2 * WINDOW), 1)
    dist = qi + WINDOW - kj
    band = (dist >= 0) & (dist < WINDOW)
    top = lax.broadcasted_iota(jnp.int32, (2 * WINDOW, 1), 0) < WINDOW
    lo_out = lax.broadcasted_iota(jnp.int32, (WINDOW, 2 * HEAD_DIM), 1) < HEAD_DIM

    def attn_block(jb):
        r0 = jb * WINDOW
        valid = band & jnp.logical_or(i != 0, kj >= WINDOW) if jb == 0 else band
        for kvh in range(N_KV_HEADS):
            h0 = kvh * GQ
            c0 = h0 * HEAD_DIM
            qs = jnp.concatenate([qbuf[r0:r0 + WINDOW, c0:c0 + 2 * HEAD_DIM],
                                  qbuf[r0:r0 + WINDOW, c0 + 2 * HEAD_DIM:c0 + 4 * HEAD_DIM]], axis=0)
            pv = []
            for par in range(2):
                kk = kbuf[2 * kvh + par, r0:r0 + 2 * WINDOW, :]
                vv = vbuf[(kvh + par) % 2, r0:r0 + 2 * WINDOW, :]
                s = jnp.where(valid, _dot_nt(qs, kk), MASK_VALUE)
                sink = jnp.where(top, sinks_ref[h0 + par], sinks_ref[h0 + 2 + par])
                m = jnp.maximum(jnp.max(s, axis=-1, keepdims=True), sink)
                p = jnp.exp(s - m)
                denom = jnp.sum(p, axis=-1, keepdims=True) + jnp.exp(sink - m)
                pv.append(jnp.dot(p.astype(BF16), vv, preferred_element_type=F32) / denom)
            for pair in range(2):
                rs = slice(pair * WINDOW, (pair + 1) * WINDOW)
                o = jnp.where(lo_out, pv[0][rs], pv[1][rs])
                cs = CONV_CH + c0 + pair * 2 * HEAD_DIM
                ymix[0, r0:r0 + WINDOW, cs:cs + 2 * HEAD_DIM] = o.astype(BF16)

    assert ts // rc == ts // WINDOW
    for c in range(ts // rc):
        attn_block(c)
        conv_chunk(c)

    hbuf[0:CONV_HALO, :] = hbuf[ts:ts + CONV_HALO, :]
    kbuf[:, 0:WINDOW, :] = kbuf[:, ts:ts + WINDOW, :]
    vbuf[:, 0:WINDOW, :] = vbuf[:, ts:ts + WINDOW, :]


def _const_spec(shape):
    nd = len(shape)
    return pl.BlockSpec(shape, lambda *_: (0,) * nd)


def _resident_spec(shape):
    nd = len(shape)
    return pl.BlockSpec(shape, lambda *_: (0,) * nd, pipeline_mode=pl.Buffered(1))


def _row_slices(w, n_steps, step_of):
    rows = w.shape[-2]
    lead = w.shape[0] if w.ndim == 3 else 1
    per = n_steps // lead
    assert per * lead == n_steps and rows % per == 0 and (rows // per) % 16 == 0
    if w.ndim == 3:
        return pl.BlockSpec((1, rows // per, w.shape[-1]), lambda *g: (step_of(*g) // per, step_of(*g) % per, 0))
    return pl.BlockSpec((rows // per, w.shape[-1]), lambda *g: (step_of(*g), 0))


def _mixer(x, sinks, w_in, b_in, w_dw, b_dw, g_cn, b_cn, side):
    B, S, D = x.shape
    ts = SEQ_TILE
    tile = pl.BlockSpec((1, ts, D), lambda b, i: (b, i, 0))
    nt = S // ts
    assert len(side) == N_SIDE
    side_specs = [_row_slices(w, B * nt, lambda b, i: b * nt + i) for w in side]
    return pl.pallas_call(
        _mixer_kernel,
        grid=(B, S // ts),
        in_specs=[
            pl.BlockSpec(memory_space=pltpu.SMEM),
            tile,
            _resident_spec((D, D_IN)), _const_spec((1, D_IN)),
            _const_spec((CONV_HALO, CONV_CH)), _const_spec((1, CONV_CH)),
            _const_spec((1, CONV_CH)), _const_spec((1, CONV_CH)),
        ] + side_specs,
        out_specs=[pl.BlockSpec((1, ts, D_MIX), lambda b, i: (b, i, 0))] + side_specs,
        out_shape=[jax.ShapeDtypeStruct((B, S, D_MIX), BF16)]
        + [jax.ShapeDtypeStruct(w.shape, BF16) for w in side],
        scratch_shapes=[
            pltpu.VMEM((D, D_IN), BF16),
            pltpu.VMEM((CONV_HALO + ts, CONV_CH), F32),
            pltpu.VMEM((SUBLANES - 1, CONV_HALO + ts, CONV_CH), F32),
            pltpu.VMEM((ts, CONV_CH), F32),
            pltpu.VMEM((ts, ATTN_W), BF16),
            pltpu.VMEM((2 * N_KV_HEADS, WINDOW + ts, KV_W), BF16),
            pltpu.VMEM((2, WINDOW + ts, KV_W), BF16),
        ],
        compiler_params=pltpu.CompilerParams(
            dimension_semantics=("arbitrary", "arbitrary"), vmem_limit_bytes=VMEM_LIMIT),
        name="mixer",
    )(sinks, x, w_in, b_in, w_dw, b_dw, g_cn, b_cn, *side)


def _memkv_kernel(mem_ref, w_ref, o_ref):
    o_ref[...] = jnp.dot(mem_ref[...].astype(BF16), w_ref[...].astype(BF16),
                         preferred_element_type=F32).astype(BF16)


def _memkv(mem2d, w_mkv):
    M, D = mem2d.shape
    N = w_mkv.shape[1]
    tn = 512
    return pl.pallas_call(
        _memkv_kernel,
        grid=(N // tn,),
        in_specs=[pl.BlockSpec((M, D), lambda j: (0, 0)), pl.BlockSpec((D, tn), lambda j: (0, j))],
        out_specs=pl.BlockSpec((M, tn), lambda j: (0, j)),
        out_shape=jax.ShapeDtypeStruct((M, N), BF16),
        compiler_params=pltpu.CompilerParams(dimension_semantics=("arbitrary",)),
        name="memkv",
    )(mem2d, w_mkv)


def _first_max(rows):
    best = rows[0]
    for r in rows[1:]:
        best = jnp.maximum(best, r)
    idx = jnp.full(best.shape, len(rows) - 1, jnp.int32)
    for k in range(len(rows) - 2, -1, -1):
        idx = jnp.where(rows[k] == best, k, idx)
    return best, idx


def _route_plan(logits_t):
    tile = logits_t.shape[1]
    row = lambda k: logits_t[k:k + 1, :]
    gmax, g_idx = _first_max([row(g) for g in range(N_GROUPS)])
    gsum = jnp.exp(row(0) - gmax)
    for g in range(1, N_GROUPS):
        gsum = gsum + jnp.exp(row(g) - gmax)
    g_p = 1.0 / gsum
    rl = []
    for e in range(EXPERTS_PER_GROUP):
        v = row(ROUTE_OFF + (N_GROUPS - 1) * EXPERTS_PER_GROUP + e)
        for g in range(N_GROUPS - 2, -1, -1):
            v = jnp.where(g_idx == g, row(ROUTE_OFF + g * EXPERTS_PER_GROUP + e), v)
        rl.append(v)
    m1, i1 = _first_max(rl)
    m2, i2 = _first_max([jnp.where(i1 == e, MASK_VALUE, rl[e]) for e in range(EXPERTS_PER_GROUP)])
    ex = jnp.exp(m2 - m1)
    w1 = 1.0 / (1.0 + ex)
    w2 = ex * w1
    e1 = g_idx * EXPERTS_PER_GROUP + i1
    e2 = g_idx * EXPERTS_PER_GROUP + i2

    eid = lax.broadcasted_iota(jnp.int32, (N_EXPERTS, tile), 0)
    hit1 = eid == e1
    hit2 = eid == e2
    oh = jnp.where(jnp.logical_or(hit1, hit2), 1.0, 0.0)
    r = lax.broadcasted_iota(jnp.int32, (tile, tile), 0)
    c = lax.broadcasted_iota(jnp.int32, (tile, tile), 1)
    tri = jnp.where(r <= c, 1.0, 0.0).astype(BF16)
    csum = jnp.dot(oh.astype(BF16), tri, preferred_element_type=F32)
    counts = jnp.broadcast_to(csum[:, tile - 1:tile], (N_EXPERTS, tile)).astype(jnp.int32)
    nch = jnp.right_shift(counts + (CHUNK - 1), CHUNK.bit_length() - 1)
    er = lax.broadcasted_iota(jnp.int32, (N_EXPERTS, N_EXPERTS), 0)
    ec = lax.broadcasted_iota(jnp.int32, (N_EXPERTS, N_EXPERTS), 1)
    lower = jnp.where(ec < er, 1.0, 0.0).astype(BF16)
    off = jnp.dot(lower, nch.astype(F32).astype(BF16), preferred_element_type=F32) * CHUNK
    pos = off + csum - oh
    lp1 = jnp.sum(jnp.where(hit1, pos, 0.0), axis=0, keepdims=True)
    lp2 = jnp.sum(jnp.where(hit2, pos, 0.0), axis=0, keepdims=True)
    zero = jnp.zeros_like(lp1)
    route_t = jnp.concatenate([lp1, lp2, g_p * w1, g_p * w2, zero, zero, zero, zero], axis=0)
    meta = jnp.concatenate([nch[:, 0:LANES], off[:, 0:LANES].astype(jnp.int32)], axis=0)
    return route_t, meta


def _memattn_kernel(x_ref, ymix_ref, w_out_ref, g1_ref, b1_ref, wq_ref, k_ref, v_ref, wo_ref, g2_ref, b2_ref,
                    wr2_ref, wrh_ref, br_ref, we_ref,
                    o_ref, ob_ref, route_ref, routet_ref, meta_ref, web_ref):
    web_ref[...] = we_ref[...].astype(BF16)
    w_out_b, wq_b, wo_b = w_out_ref, wq_ref, wo_ref

    def rows_logits(rs):
        mix = jnp.dot(ymix_ref[0, rs, :], w_out_b[...], preferred_element_type=F32)
        x = _layer_norm(ALPHA * x_ref[0, rs, :] + mix, g1_ref[...], b1_ref[...])
        q = jnp.dot(x.astype(BF16), wq_b[...], preferred_element_type=F32)
        q = (q * (MEM_HEAD_DIM ** -0.5)).astype(BF16)
        outs = []
        for h in range(MEM_HEADS):
            sl = slice(h * MEM_HEAD_DIM, (h + 1) * MEM_HEAD_DIM)
            s = _dot_nt(q[:, sl], k_ref[0, :, sl])
            m = jnp.max(s, axis=-1, keepdims=True)
            p = jnp.exp(s - m)
            denom = jnp.sum(p, axis=-1, keepdims=True)
            o = jnp.dot(p.astype(BF16), v_ref[0, :, sl], preferred_element_type=F32)
            outs.append((o / denom).astype(BF16))
        o = jnp.dot(jnp.concatenate(outs, axis=-1), wo_b[...], preferred_element_type=F32)
        x2 = _layer_norm(ALPHA * x + o, g2_ref[...], b2_ref[...])
        o_ref[0, rs, :] = x2
        x2h = x2.astype(BF16)
        ob_ref[0, rs, :] = x2h
        x2l = (x2 - x2h.astype(F32)).astype(BF16)
        hh = jnp.dot(x2h, wr2_ref[...], preferred_element_type=F32)
        return (hh[:, 0:LANES] + hh[:, LANES:2 * LANES]
                + jnp.dot(x2l, wrh_ref[...], preferred_element_type=F32) + br_ref[...])

    ts = x_ref.shape[1]
    groups = [slice(r0, r0 + MOE_TILE) for r0 in range(0, ts, MOE_TILE)]
    logits = rows_logits(slice(0, ts))
    pad = jnp.zeros((LANES - SUBLANES, MOE_TILE), F32)
    for k, rs in enumerate(groups):
        route_t, meta = _route_plan(jnp.transpose(logits[rs, :]))
        routet_ref[k] = route_t
        route_ref[0, rs, :] = jnp.transpose(jnp.concatenate([route_t, pad], axis=0))
        meta_ref[k] = meta


def _memattn(x, ymix, w_out, g1, b1, wq, kvm, wo, g2, b2, wr, br, w_expert):
    B, S, D = x.shape
    ts = MEM_TILE
    assert B * (S // ts) == w_expert.shape[0]
    wes = pl.BlockSpec((1,) + w_expert.shape[1:], lambda b, i: (b * (S // ts) + i, 0, 0))
    per = ts // MOE_TILE
    nt = S // ts
    c = wr * (2.0 ** 16 + 1.0)
    w_high = c - (c - wr)
    wrh = w_high.astype(BF16)
    wr2 = jnp.concatenate([wrh, (wr - w_high).astype(BF16)], axis=1)
    tile = pl.BlockSpec((1, ts, D), lambda b, i: (b, i, 0))
    kspec = pl.BlockSpec((1, MEM_LEN, D), lambda b, i: (b, 0, 0))
    vspec = pl.BlockSpec((1, MEM_LEN, D), lambda b, i: (b, 0, 1))
    return pl.pallas_call(
        _memattn_kernel,
        grid=(B, nt),
        in_specs=[tile, pl.BlockSpec((1, ts, D_MIX), lambda b, i: (b, i, 0)), _resident_spec((D_MIX, D)),
                  _const_spec((1, D)), _const_spec((1, D)),
                  _resident_spec((D, D)), kspec, vspec, _resident_spec((D, D)),
                  _const_spec((1, D)), _const_spec((1, D)),
                  _const_spec((D, 2 * LANES)), _const_spec((D, LANES)), _const_spec((1, LANES)), wes],
        out_specs=[tile, tile, pl.BlockSpec((1, ts, LANES), lambda b, i: (b, i, 0)),
                   pl.BlockSpec((per, SUBLANES, MOE_TILE), lambda b, i: (b * nt + i, 0, 0)),
                   pl.BlockSpec((per, 2 * N_EXPERTS, LANES), lambda b, i: (b * nt + i, 0, 0)), wes],
        out_shape=[jax.ShapeDtypeStruct((B, S, D), F32),
                   jax.ShapeDtypeStruct((B, S, D), BF16),
                   jax.ShapeDtypeStruct((B, S, LANES), F32),
                   jax.ShapeDtypeStruct((B * nt * per, SUBLANES, MOE_TILE), F32),
                   jax.ShapeDtypeStruct((B * nt * per, 2 * N_EXPERTS, LANES), jnp.int32),
                   jax.ShapeDtypeStruct(w_expert.shape, BF16)],
        compiler_params=pltpu.CompilerParams(
            dimension_semantics=("arbitrary", "arbitrary"), vmem_limit_bytes=VMEM_LIMIT),
        name="memattn",
    )(x, ymix, w_out, g1, b1, wq, kvm, kvm, wo, g2, b2, wr2, wrh, br, w_expert)


def _gmm_blocks(n_tokens):
    rows = (n_tokens * TOP_K + (n_tokens // MOE_TILE) * N_EXPERTS * (CHUNK - 1)
            + N_EXPERTS * (REGION_ALIGN - CHUNK))
    return -(-rows // GMM_ROWS)


def _plan(meta, T):
    nch = meta[:, :N_EXPERTS, 0]
    n16 = nch * CHUNK
    n_e = jnp.sum(n16, axis=0)
    reg = (n_e + REGION_ALIGN - 1) // REGION_ALIGN * REGION_ALIGN
    gend = jnp.cumsum(reg)
    gbase = gend - reg
    dst = gbase[None, :] + jnp.cumsum(n16, axis=0) - n16
    half_row = jnp.arange(2 * _gmm_blocks(T), dtype=jnp.int32) * REGION_ALIGN
    half_expert = jnp.minimum(jnp.sum(half_row[:, None] >= gend[None, :], axis=1), N_EXPERTS - 1)
    half_used = half_row < gend[-1]
    n_used = ((gend[-1] + GMM_ROWS - 1) // GMM_ROWS).astype(jnp.int32).reshape(1)
    src = (jnp.cumsum(nch, axis=1) - nch) * CHUNK
    n_big = nch // BIG

    def copy_list(count, src0, dst0, rows, length):
        cum = jnp.cumsum(count, axis=1)
        first = (cum - count)[:, None, :]
        k = jnp.arange(length, dtype=jnp.int32)[None, :, None]
        mine = (k >= first) & (k < cum[:, None, :])
        step = (k - first) * rows
        pick = lambda base: jnp.sum(jnp.where(mine, base[:, None, :] + step, 0), axis=2)
        return pick(src0), pick(dst0), cum[:, -1]

    big_src, big_dst, big_n = copy_list(n_big, src, dst, BIG * CHUNK, MAX_BIG)
    rest = n_big * (BIG * CHUNK)
    small_src, small_dst, small_n = copy_list(nch - n_big * BIG, src + rest, dst + rest, CHUNK, MAX_SMALL)
    i32 = lambda a: a.astype(jnp.int32)
    fill_start = jnp.concatenate([gbase + n_e, gend[-1:]])
    fill_n = jnp.concatenate([reg - n_e, _gmm_blocks(T) * GMM_ROWS - gend[-1:]]) // CHUNK
    return dict(big_src=i32(big_src).reshape(-1), big_dst=i32(big_dst).reshape(-1), big_n=i32(big_n),
                small_src=i32(small_src).reshape(-1), small_dst=i32(small_dst).reshape(-1), small_n=i32(small_n),
                fill_start=i32(fill_start), fill_n=i32(fill_n),
                fill_tot=i32(jnp.sum(fill_n)).reshape(1), half_expert=i32(half_expert),
                half_used=i32(half_used), n_used=n_used)


def _rows_copy(src_ref, src_row, dst_ref, dst_row, rows, sem):
    return pltpu.make_async_copy(
        src_ref.at[pl.ds(pl.multiple_of(src_row, CHUNK), rows), :],
        dst_ref.at[pl.ds(pl.multiple_of(dst_row, CHUNK), rows), :], sem)


def _chunk_copy(src_ref, src_row, dst_ref, dst_row, sem):
    return _rows_copy(src_ref, src_row, dst_ref, dst_row, CHUNK, sem)


def _issue_copies(lists, tile, tile_ref, tile_is_src, hbm_ref, sem):
    big_src, big_dst, big_n, small_src, small_dst, small_n = lists
    for src_l, dst_l, n_l, length, rows in ((big_src, big_dst, big_n, MAX_BIG, BIG * CHUNK),
                                            (small_src, small_dst, small_n, MAX_SMALL, CHUNK)):
        def issue(k, carry, src_l=src_l, dst_l=dst_l, length=length, rows=rows):
            local, remote = src_l[tile * length + k], dst_l[tile * length + k]
            if tile_is_src:
                _rows_copy(tile_ref, local, hbm_ref, remote, rows, sem).start()
            else:
                _rows_copy(hbm_ref, remote, tile_ref, local, rows, sem).start()
            return carry

        lax.fori_loop(0, n_l[tile], issue, 0)


def _wait_rows(n, rows, src_ref, dst_ref, sem):
    def body(c, carry):
        pltpu.make_async_copy(src_ref.at[pl.ds(0, rows), :], dst_ref.at[pl.ds(0, rows), :], sem).wait()
        return carry

    lax.fori_loop(0, n, body, 0)


def _wait_copies(lists, tile, src_ref, dst_ref, sem, extra_chunks=0):
    _wait_rows(lists[2][tile], BIG * CHUNK, src_ref, dst_ref, sem)
    n = lists[5][tile] + extra_chunks
    _wait_rows(n // WAIT_GROUP, WAIT_GROUP * CHUNK, src_ref, dst_ref, sem)
    _wait_rows(n % WAIT_GROUP, CHUNK, src_ref, dst_ref, sem)


def _dispatch_kernel(bs_ref, bd_ref, bn_ref, ss_ref, sd_ref, sn_ref, fstart_ref, fn_ref, ftot_ref,
                     lp_ref, x_ref, we_ref, xs_hbm, web_ref, xt, zbuf, sems):
    lists = (bs_ref, bd_ref, bn_ref, ss_ref, sd_ref, sn_ref)
    s = pl.program_id(0)
    ns = pl.num_programs(0)
    cur = (s % 2) * PER_STEP
    prv = PER_STEP - cur

    @pl.when(s >= 2)
    def _():
        for k in range(PER_STEP):
            _wait_copies(lists, (s - 2) * PER_STEP + k, xt.at[cur + k], xs_hbm, sems.at[cur + k])

    r = lax.broadcasted_iota(jnp.int32, (SORT_ROWS, MOE_TILE), 0)
    for k in range(PER_STEP):
        lp = lp_ref[k].astype(jnp.int32)
        hit = jnp.logical_or(lp[0:1, :] == r, lp[1:2, :] == r)
        p = jnp.where(hit, 1.0, 0.0).astype(BF16)
        x = x_ref[k * MOE_TILE:(k + 1) * MOE_TILE, :]
        xt[cur + k] = jnp.dot(p, x, preferred_element_type=F32).astype(BF16)

    for k in range(PER_STEP):
        _issue_copies(lists, s * PER_STEP + k, xt.at[cur + k], True, xs_hbm, sems.at[cur + k])

    _cast_bf16(web_ref.at[0], we_ref.at[0])

    @pl.when(s == ns - 1)
    def _():
        zbuf[...] = jnp.zeros_like(zbuf)
        sem = sems.at[cur]

        def per_range(e, carry):
            def issue(c, carry2):
                _chunk_copy(zbuf, 0, xs_hbm, fstart_ref[e] + c * CHUNK, sem).start()
                return carry2

            return lax.fori_loop(0, fn_ref[e], issue, carry)

        lax.fori_loop(0, N_EXPERTS + 1, per_range, 0)
        for k in range(PER_STEP):
            _wait_copies(lists, (s - 1) * PER_STEP + k, xt.at[prv + k], xs_hbm, sems.at[prv + k])
            _wait_copies(lists, s * PER_STEP + k, xt.at[cur + k], xs_hbm, sems.at[cur + k],
                         extra_chunks=ftot_ref[0] if k == 0 else 0)


def _copy_lists(plan):
    return tuple(plan[k] for k in ("big_src", "big_dst", "big_n", "small_src", "small_dst", "small_n"))


def _dispatch(plan, lpt, x2b, w_expert):
    T, D = x2b.shape
    rows = PER_STEP * MOE_TILE
    assert T // rows >= 2
    assert T // rows == N_EXPERTS
    wes = pl.BlockSpec((1,) + w_expert.shape[1:], lambda t, *_: (t, 0, 0))
    grid_spec = pltpu.PrefetchScalarGridSpec(
        num_scalar_prefetch=9,
        grid=(T // rows,),
        in_specs=[pl.BlockSpec((PER_STEP, SUBLANES, MOE_TILE), lambda t, *_: (t, 0, 0)),
                  pl.BlockSpec((rows, D), lambda t, *_: (t, 0)), wes],
        out_specs=[pl.BlockSpec(memory_space=pl.ANY), wes],
        scratch_shapes=[pltpu.VMEM((2 * PER_STEP, SORT_ROWS, D), BF16), pltpu.VMEM((CHUNK, D), BF16),
                        pltpu.SemaphoreType.DMA((2 * PER_STEP,))],
    )
    return pl.pallas_call(
        _dispatch_kernel,
        grid_spec=grid_spec,
        out_shape=[jax.ShapeDtypeStruct((_gmm_blocks(T) * GMM_ROWS, D), BF16),
                   jax.ShapeDtypeStruct(w_expert.shape, BF16)],
        compiler_params=pltpu.CompilerParams(
            dimension_semantics=("arbitrary",), vmem_limit_bytes=VMEM_LIMIT),
        name="moe_dispatch",
    )(*_copy_lists(plan), plan["fill_start"], plan["fill_n"], plan["fill_tot"], lpt, x2b, w_expert)


def _expert_ffn(xb, wg_b, wu_b, wd_b):
    g = jnp.dot(xb, wg_b[...], preferred_element_type=F32)
    u = jnp.dot(xb, wu_b[...], preferred_element_type=F32)
    h = (g * jax.nn.sigmoid(g)) * u
    return jnp.dot(h.astype(BF16), wd_b[...], preferred_element_type=F32).astype(BF16)


def _gmm_kernel(he_ref, hu_ref, nu_ref, x_ref, wga_ref, wua_ref, wda_ref, wgb_ref, wub_ref, wdb_ref, o_ref):
    b = pl.program_id(0)
    half = REGION_ALIGN
    same = he_ref[2 * b] == he_ref[2 * b + 1]
    used0 = hu_ref[2 * b] != 0
    used1 = hu_ref[2 * b + 1] != 0
    set_a = (wga_ref.at[0], wua_ref.at[0], wda_ref.at[0])
    set_b = (wgb_ref.at[0], wub_ref.at[0], wdb_ref.at[0])

    @pl.when(jnp.logical_and(used1, same))
    def _():
        o_ref[...] = _expert_ffn(x_ref[...], *set_a)

    @pl.when(jnp.logical_and(used0, jnp.logical_not(jnp.logical_and(used1, same))))
    def _():
        o_ref[0:half, :] = _expert_ffn(x_ref[0:half, :], *set_a)

    @pl.when(jnp.logical_and(used1, jnp.logical_not(same)))
    def _():
        o_ref[half:GMM_ROWS, :] = _expert_ffn(x_ref[half:GMM_ROWS, :], *set_b)

    @pl.when(jnp.logical_not(used0))
    def _():
        o_ref[0:half, :] = jnp.zeros((half, o_ref.shape[1]), o_ref.dtype)

    @pl.when(jnp.logical_not(used1))
    def _():
        o_ref[half:GMM_ROWS, :] = jnp.zeros((GMM_ROWS - half, o_ref.shape[1]), o_ref.dtype)


def _gmm(plan, xs, wg, wu, wd):
    R, D = xs.shape
    rows = pl.BlockSpec((GMM_ROWS, D), lambda b, he, hu, nu: (jnp.minimum(b, nu[0] - 1), 0))

    def wspec(shape, h):
        return pl.BlockSpec((1,) + shape, lambda b, he, hu, nu: (he[2 * b + h], 0, 0))

    grid_spec = pltpu.PrefetchScalarGridSpec(
        num_scalar_prefetch=3,
        grid=(R // GMM_ROWS,),
        in_specs=[rows,
                  wspec((D, D_EXPERT), 0), wspec((D, D_EXPERT), 0), wspec((D_EXPERT, D), 0),
                  wspec((D, D_EXPERT), 1), wspec((D, D_EXPERT), 1), wspec((D_EXPERT, D), 1)],
        out_specs=pl.BlockSpec((GMM_ROWS, D), lambda b, he, hu, nu: (b, 0)),
    )
    return pl.pallas_call(
        _gmm_kernel,
        grid_spec=grid_spec,
        out_shape=jax.ShapeDtypeStruct((R, D), BF16),
        compiler_params=pltpu.CompilerParams(
            dimension_semantics=("arbitrary",), vmem_limit_bytes=VMEM_LIMIT),
        name="moe_gmm",
    )(plan["half_expert"], plan["half_used"], plan["n_used"], xs, wg, wu, wd, wg, wu, wd)


def _combine_kernel(bs_ref, bd_ref, bn_ref, ss_ref, sd_ref, sn_ref, cm_ref, x_ref, ys_hbm, g3_ref, b3_ref,
                    o_ref, yt, sems):
    lists = (bs_ref, bd_ref, bn_ref, ss_ref, sd_ref, sn_ref)
    s = pl.program_id(0)
    ns = pl.num_programs(0)
    cur = (s % 2) * PER_STEP
    nxt = PER_STEP - cur

    def fetch(step, first_buf):
        for k in range(PER_STEP):
            _issue_copies(lists, step * PER_STEP + k, yt.at[first_buf + k], False, ys_hbm,
                          sems.at[first_buf + k])

    @pl.when(s == 0)
    def _():
        yt[...] = jnp.zeros_like(yt)
        fetch(0, 0)

    @pl.when(s + 1 < ns)
    def _():
        fetch(s + 1, nxt)

    for k in range(PER_STEP):
        _wait_copies(lists, s * PER_STEP + k, ys_hbm, yt.at[cur + k], sems.at[cur + k])

    col = lax.broadcasted_iota(jnp.int32, (MOE_TILE, SORT_ROWS), 1)
    for k in range(PER_STEP):
        rs = slice(k * MOE_TILE, (k + 1) * MOE_TILE)
        cm = cm_ref[rs, :]
        lp0 = cm[:, 0:1].astype(jnp.int32)
        lp1 = cm[:, 1:2].astype(jnp.int32)
        w = jnp.where(col == lp0, cm[:, 2:3], jnp.where(col == lp1, cm[:, 3:4], 0.0)).astype(BF16)
        y = jnp.dot(w, yt[cur + k], preferred_element_type=F32)
        o_ref[rs, :] = _layer_norm(ALPHA * x_ref[rs, :] + y, g3_ref[...], b3_ref[...])


def _combine(plan, route, x2, ys, g3, b3):
    T, D = x2.shape
    rows = PER_STEP * MOE_TILE
    grid_spec = pltpu.PrefetchScalarGridSpec(
        num_scalar_prefetch=6,
        grid=(T // rows,),
        in_specs=[pl.BlockSpec((rows, LANES), lambda t, *_: (t, 0)),
                  pl.BlockSpec((rows, D), lambda t, *_: (t, 0)),
                  pl.BlockSpec(memory_space=pl.ANY),
                  pl.BlockSpec((1, D), lambda t, *_: (0, 0)),
                  pl.BlockSpec((1, D), lambda t, *_: (0, 0))],
        out_specs=pl.BlockSpec((rows, D), lambda t, *_: (t, 0)),
        scratch_shapes=[pltpu.VMEM((2 * PER_STEP, SORT_ROWS, D), BF16),
                        pltpu.SemaphoreType.DMA((2 * PER_STEP,))],
    )
    return pl.pallas_call(
        _combine_kernel,
        grid_spec=grid_spec,
        out_shape=jax.ShapeDtypeStruct((T, D), F32),
        compiler_params=pltpu.CompilerParams(
            dimension_semantics=("arbitrary",), vmem_limit_bytes=VMEM_LIMIT),
        name="moe_combine",
    )(*_copy_lists(plan), route, x2, ys, g3, b3)


def _moe(x2, x2b, route, lpt, meta, wg, wu_b, wd_b, g3, b3):
    plan = _plan(meta, x2.shape[0])
    xs, wg_b = _dispatch(plan, lpt, x2b, wg)
    ys = _gmm(plan, xs, wg_b, wu_b, wd_b)
    return _combine(plan, route, x2, ys, g3, b3)


def _row(v):
    return v.reshape(1, -1).astype(F32)


def kernel(x, mem, w_in, b_in, w_dw, b_dw, g_conv_norm, b_conv_norm, attn_sinks, w_out, g_ln1, b_ln1,
           w_mq, w_mkv, w_mo, g_ln2, b_ln2, w_group, b_group, w_router, b_router, w_gate, w_up, w_down,
           g_ln3, b_ln3):
    B, S, D = x.shape
    for l in range(DEPTH):
        w_dw_p = jnp.zeros((CONV_HALO, CONV_CH), F32).at[:CONV_WIDTH].set(w_dw[l])
        ymix, wd_b, w_out_b, w_mq_b, w_mo_b = _mixer(
            x, attn_sinks[l].astype(F32), w_in[l], _row(b_in[l]), w_dw_p, _row(b_dw[l]),
            _row(g_conv_norm[l]), _row(b_conv_norm[l]), [w_down[l], w_out[l], w_mq[l], w_mo[l]])

        kvm = _memkv(mem.reshape(B * MEM_LEN, D), w_mkv[l]).reshape(B, MEM_LEN, 2 * D)

        wr = jnp.concatenate(
            [w_group[l], jnp.transpose(w_router[l], (1, 0, 2)).reshape(D, N_EXPERTS)], axis=1)
        wr = jnp.pad(wr, ((0, 0), (0, LANES - wr.shape[1])))
        br = jnp.pad(jnp.concatenate([b_group[l], b_router[l].reshape(-1)]), (0, LANES - N_GROUPS - N_EXPERTS))
        x2, x2b, route, lpt, meta, wu_b = _memattn(x, ymix, w_out_b, _row(g_ln1[l]), _row(b_ln1[l]),
                                                   w_mq_b, kvm, w_mo_b, _row(g_ln2[l]), _row(b_ln2[l]),
                                                   wr.astype(F32), _row(br), w_up[l])

        T = B * S
        y = _moe(x2.reshape(T, D), x2b.reshape(T, D), route.reshape(T, LANES), lpt, meta,
                 w_gate[l], wu_b, wd_b,
                 _row(g_ln3[l]), _row(b_ln3[l]))
        x = y.reshape(B, S, D)
    return x
```

```python
import functools

import jax
import jax.numpy as jnp
from jax import lax
from jax.experimental import pallas as pl
from jax.experimental.pallas import tpu as pltpu

D_MODEL = 1024
MEM_LEN = 256
CONV_CH = 512
CONV_WIDTH = 31
N_HEADS = 8
N_KV_HEADS = 2
HEAD_DIM = 64
GQ = N_HEADS // N_KV_HEADS
ATTN_W = N_HEADS * HEAD_DIM
KV_W = N_KV_HEADS * HEAD_DIM
WINDOW = 128
D_MIX = CONV_CH + ATTN_W
D_IN = 2 * CONV_CH + ATTN_W + 2 * KV_W
MEM_HEADS = 4
MEM_HEAD_DIM = D_MODEL // MEM_HEADS
N_GROUPS = 4
EXPERTS_PER_GROUP = 4
N_EXPERTS = N_GROUPS * EXPERTS_PER_GROUP
D_EXPERT = D_MODEL // 2
DEPTH = 1
ALPHA = (2.0 * DEPTH) ** 0.25
LN_EPS = 1e-5

LANES = 128
SUBLANES = 8
CONV_ROWS = 128
LN_ROWS = 64
MEM_TILE = 1024
MASK_VALUE = -1e30
CONV_HALO = 32
SEQ_TILE = 512
MOE_TILE = 512
CHUNK = 16
REGION_ALIGN = 512
GMM_ROWS = 2 * REGION_ALIGN
TOP_K = 2
SORT_ROWS = -(-(MOE_TILE * TOP_K + N_EXPERTS * (CHUNK - 1)) // 256) * 256
BIG = 4
MAX_BIG = SORT_ROWS // (BIG * CHUNK)
MAX_SMALL = N_EXPERTS * (BIG - 1)
PER_STEP = 2
WAIT_GROUP = 8
ROUTE_OFF = N_GROUPS
VMEM_LIMIT = 56 * 1024 * 1024

BF16 = jnp.bfloat16
F32 = jnp.float32


def _layer_norm(x, g, b):
    mu = jnp.mean(x, axis=-1, keepdims=True)
    xc = x - mu
    var = jnp.mean(xc * xc, axis=-1, keepdims=True)
    return xc * lax.rsqrt(var + LN_EPS) * g + b


def _cast_bf16(dst_ref, src_ref):
    rows = 256
    for r0 in range(0, src_ref.shape[0], rows):
        dst_ref[r0:r0 + rows, :] = src_ref[r0:r0 + rows, :].astype(BF16)


def _dot_nt(a, b):
    return lax.dot_general(a, b, (((1,), (1,)), ((), ())), preferred_element_type=F32)


def _mixer_kernel(sinks_ref, x_ref, w_in_ref, b_in_ref, w_dw_ref, b_dw_ref, g_cn_ref, b_cn_ref,
                  w_out_ref, g1_ref, b1_ref, we_ref, o_ref, web_ref, w_in_b, w_out_b, hbuf, hshift, cbuf, qbuf,
                  kbuf, vbuf, ymix):
    i = pl.program_id(1)
    ts = SEQ_TILE
    web_ref[...] = we_ref[...].astype(BF16)

    @pl.when(jnp.logical_and(pl.program_id(0) == 0, i == 0))
    def _():
        _cast_bf16(w_in_b, w_in_ref)
        _cast_bf16(w_out_b, w_out_ref)

    @pl.when(i == 0)
    def _():
        hbuf[0:CONV_HALO, :] = jnp.zeros((CONV_HALO, CONV_CH), F32)
        kbuf[:, 0:WINDOW, :] = jnp.zeros((2 * N_KV_HEADS, WINDOW, KV_W), BF16)
        vbuf[:, 0:WINDOW, :] = jnp.zeros((2, WINDOW, KV_W), BF16)

    x = x_ref[0]
    u = jnp.dot(x.astype(BF16), w_in_b[...], preferred_element_type=F32) + b_in_ref[...]
    a = u[:, 0:CONV_CH]
    gate = u[:, CONV_CH:2 * CONV_CH]
    hbuf[CONV_HALO:CONV_HALO + ts, :] = a * jax.nn.sigmoid(gate)
    qbuf[...] = (u[:, 2 * CONV_CH:2 * CONV_CH + ATTN_W] * (HEAD_DIM ** -0.5)).astype(BF16)
    kf = u[:, 2 * CONV_CH + ATTN_W:2 * CONV_CH + ATTN_W + KV_W]
    vf = u[:, 2 * CONV_CH + ATTN_W + KV_W:D_IN]
    kr = pltpu.roll(kf, HEAD_DIM, axis=1)
    vr = pltpu.roll(vf, HEAD_DIM, axis=1)
    lo = lax.broadcasted_iota(jnp.int32, (ts, KV_W), 1) < HEAD_DIM
    rows = slice(WINDOW, WINDOW + ts)
    kbuf[0, rows, :] = jnp.where(lo, kf, 0.0).astype(BF16)
    kbuf[1, rows, :] = jnp.where(lo, 0.0, kr).astype(BF16)
    kbuf[2, rows, :] = jnp.where(lo, kr, 0.0).astype(BF16)
    kbuf[3, rows, :] = jnp.where(lo, 0.0, kf).astype(BF16)
    vbuf[0, rows, :] = vf.astype(BF16)
    vbuf[1, rows, :] = vr.astype(BF16)

    base = CONV_HALO - (CONV_WIDTH - 1)
    n_shift = ts + CONV_HALO - SUBLANES
    for b in range(1, SUBLANES):
        hshift[b - 1, 0:n_shift, :] = hbuf[b:b + n_shift, :]
    rc = CONV_ROWS

    def conv_chunk(c):
        r0 = c * rc
        for l in range(CONV_CH // LANES):
            ls = slice(l * LANES, (l + 1) * LANES)
            acc = jnp.zeros((rc, LANES), F32)
            for j in range(CONV_WIDTH):
                a8, b = divmod(j + base, SUBLANES)
                rs = slice(r0 + SUBLANES * a8, r0 + SUBLANES * a8 + rc)
                tap = hbuf[rs, ls] if b == 0 else hshift[b - 1, rs, ls]
                acc = acc + tap * w_dw_ref[j:j + 1, ls]
            cbuf[r0:r0 + rc, ls] = acc
        for r1 in range(r0, r0 + rc, LN_ROWS):
            rs = slice(r1, r1 + LN_ROWS)
            y = _layer_norm(cbuf[rs, :] + b_dw_ref[...], g_cn_ref[...], b_cn_ref[...])
            y = y * jax.nn.sigmoid(y)
            ymix[rs, 0:CONV_CH] = y.astype(BF16)

    qi = lax.broadcasted_iota(jnp.int32, (2 * WINDOW, WINDOW), 0) % WINDOW
    kj = lax.broadcasted_iota(jnp.int32, (2 * WINDOW, WINDOW), 1)
    own = kj <= qi
    top = lax.broadcasted_iota(jnp.int32, (2 * WINDOW, 1), 0) < WINDOW
    lo_out = lax.broadcasted_iota(jnp.int32, (WINDOW, 2 * HEAD_DIM), 1) < HEAD_DIM

    def attn_block(jb):
        r0 = jb * WINDOW
        prev_ok = jnp.logical_not(own) & (i != 0) if jb == 0 else jnp.logical_not(own)
        for kvh in range(N_KV_HEADS):
            h0 = kvh * GQ
            c0 = h0 * HEAD_DIM
            qs = jnp.concatenate([qbuf[r0:r0 + WINDOW, c0:c0 + 2 * HEAD_DIM],
                                  qbuf[r0:r0 + WINDOW, c0 + 2 * HEAD_DIM:c0 + 4 * HEAD_DIM]], axis=0)
            pv = []
            for par in range(2):
                kk = kbuf[2 * kvh + par, r0:r0 + 2 * WINDOW, :]
                vv = vbuf[(kvh + par) % 2, r0:r0 + 2 * WINDOW, :]
                s2 = _dot_nt(qs, kk)
                s = jnp.where(own, s2[:, WINDOW:], jnp.where(prev_ok, s2[:, :WINDOW], MASK_VALUE))
                sink = jnp.where(top, sinks_ref[h0 + par], sinks_ref[h0 + 2 + par])
                m = jnp.maximum(jnp.max(s, axis=-1, keepdims=True), sink)
                p = jnp.exp(s - m)
                denom = jnp.sum(p, axis=-1, keepdims=True) + jnp.exp(sink - m)
                zero = jnp.zeros((), BF16)
                pb = p.astype(BF16)
                p2 = jnp.concatenate([jnp.where(own, zero, pb), jnp.where(own, pb, zero)], axis=1)
                pv.append(jnp.dot(p2, vv, preferred_element_type=F32) / denom)
            for pair in range(2):
                rs = slice(pair * WINDOW, (pair + 1) * WINDOW)
                o = jnp.where(lo_out, pv[0][rs], pv[1][rs])
                cs = CONV_CH + c0 + pair * 2 * HEAD_DIM
                ymix[r0:r0 + WINDOW, cs:cs + 2 * HEAD_DIM] = o.astype(BF16)

    assert ts // rc == ts // WINDOW
    for c in range(ts // rc):
        attn_block(c)
        conv_chunk(c)

    mix = jnp.dot(ymix[...], w_out_b[...], preferred_element_type=F32)
    o_ref[0] = _layer_norm(ALPHA * x + mix, g1_ref[...], b1_ref[...])

    hbuf[0:CONV_HALO, :] = hbuf[ts:ts + CONV_HALO, :]
    kbuf[:, 0:WINDOW, :] = kbuf[:, ts:ts + WINDOW, :]
    vbuf[:, 0:WINDOW, :] = vbuf[:, ts:ts + WINDOW, :]


def _const_spec(shape):
    nd = len(shape)
    return pl.BlockSpec(shape, lambda *_: (0,) * nd)


def _resident_spec(shape):
    nd = len(shape)
    return pl.BlockSpec(shape, lambda *_: (0,) * nd, pipeline_mode=pl.Buffered(1))


def _mixer(x, sinks, w_in, b_in, w_dw, b_dw, g_cn, b_cn, w_out, g1, b1, w_expert):
    B, S, D = x.shape
    ts = SEQ_TILE
    tile = pl.BlockSpec((1, ts, D), lambda b, i: (b, i, 0))
    nt = S // ts
    n_e, we_rows, we_cols = w_expert.shape
    per = (B * nt) // n_e
    assert per * n_e == B * nt and we_rows % per == 0
    wes = pl.BlockSpec((1, we_rows // per, we_cols), lambda b, i: ((b * nt + i) // per, (b * nt + i) % per, 0))
    return pl.pallas_call(
        _mixer_kernel,
        grid=(B, S // ts),
        in_specs=[
            pl.BlockSpec(memory_space=pltpu.SMEM),
            tile,
            _resident_spec((D, D_IN)), _const_spec((1, D_IN)),
            _const_spec((CONV_HALO, CONV_CH)), _const_spec((1, CONV_CH)),
            _const_spec((1, CONV_CH)), _const_spec((1, CONV_CH)),
            _resident_spec((D_MIX, D)), _const_spec((1, D)), _const_spec((1, D)),
            wes,
        ],
        out_specs=[tile, wes],
        out_shape=[jax.ShapeDtypeStruct((B, S, D), F32), jax.ShapeDtypeStruct(w_expert.shape, BF16)],
        scratch_shapes=[
            pltpu.VMEM((D, D_IN), BF16),
            pltpu.VMEM((D_MIX, D), BF16),
            pltpu.VMEM((CONV_HALO + ts, CONV_CH), F32),
            pltpu.VMEM((SUBLANES - 1, CONV_HALO + ts, CONV_CH), F32),
            pltpu.VMEM((ts, CONV_CH), F32),
            pltpu.VMEM((ts, ATTN_W), BF16),
            pltpu.VMEM((2 * N_KV_HEADS, WINDOW + ts, KV_W), BF16),
            pltpu.VMEM((2, WINDOW + ts, KV_W), BF16),
            pltpu.VMEM((ts, D_MIX), BF16),
        ],
        compiler_params=pltpu.CompilerParams(
            dimension_semantics=("arbitrary", "arbitrary"), vmem_limit_bytes=VMEM_LIMIT),
        name="mixer",
    )(sinks, x, w_in, b_in, w_dw, b_dw, g_cn, b_cn, w_out, g1, b1, w_expert)


def _memkv_kernel(mem_ref, w_ref, o_ref):
    o_ref[...] = jnp.dot(mem_ref[...].astype(BF16), w_ref[...].astype(BF16),
                         preferred_element_type=F32).astype(BF16)


def _memkv(mem2d, w_mkv):
    M, D = mem2d.shape
    N = w_mkv.shape[1]
    tn = 512
    return pl.pallas_call(
        _memkv_kernel,
        grid=(N // tn,),
        in_specs=[pl.BlockSpec((M, D), lambda j: (0, 0)), pl.BlockSpec((D, tn), lambda j: (0, j))],
        out_specs=pl.BlockSpec((M, tn), lambda j: (0, j)),
        out_shape=jax.ShapeDtypeStruct((M, N), BF16),
        compiler_params=pltpu.CompilerParams(dimension_semantics=("arbitrary",)),
        name="memkv",
    )(mem2d, w_mkv)


def _first_max(rows):
    best = rows[0]
    for r in rows[1:]:
        best = jnp.maximum(best, r)
    idx = jnp.full(best.shape, len(rows) - 1, jnp.int32)
    for k in range(len(rows) - 2, -1, -1):
        idx = jnp.where(rows[k] == best, k, idx)
    return best, idx


def _route_plan(logits_t):
    tile = logits_t.shape[1]
    row = lambda k: logits_t[k:k + 1, :]
    gmax, g_idx = _first_max([row(g) for g in range(N_GROUPS)])
    gsum = jnp.exp(row(0) - gmax)
    for g in range(1, N_GROUPS):
        gsum = gsum + jnp.exp(row(g) - gmax)
    g_p = 1.0 / gsum
    rl = []
    for e in range(EXPERTS_PER_GROUP):
        v = row(ROUTE_OFF + (N_GROUPS - 1) * EXPERTS_PER_GROUP + e)
        for g in range(N_GROUPS - 2, -1, -1):
            v = jnp.where(g_idx == g, row(ROUTE_OFF + g * EXPERTS_PER_GROUP + e), v)
        rl.append(v)
    m1, i1 = _first_max(rl)
    m2, i2 = _first_max([jnp.where(i1 == e, MASK_VALUE, rl[e]) for e in range(EXPERTS_PER_GROUP)])
    ex = jnp.exp(m2 - m1)
    w1 = 1.0 / (1.0 + ex)
    w2 = ex * w1
    e1 = g_idx * EXPERTS_PER_GROUP + i1
    e2 = g_idx * EXPERTS_PER_GROUP + i2

    eid = lax.broadcasted_iota(jnp.int32, (N_EXPERTS, tile), 0)
    hit1 = eid == e1
    hit2 = eid == e2
    oh = jnp.where(jnp.logical_or(hit1, hit2), 1.0, 0.0)
    r = lax.broadcasted_iota(jnp.int32, (tile, tile), 0)
    c = lax.broadcasted_iota(jnp.int32, (tile, tile), 1)
    tri = jnp.where(r <= c, 1.0, 0.0).astype(BF16)
    csum = jnp.dot(oh.astype(BF16), tri, preferred_element_type=F32)
    counts = jnp.broadcast_to(csum[:, tile - 1:tile], (N_EXPERTS, tile)).astype(jnp.int32)
    nch = jnp.right_shift(counts + (CHUNK - 1), CHUNK.bit_length() - 1)
    er = lax.broadcasted_iota(jnp.int32, (N_EXPERTS, N_EXPERTS), 0)
    ec = lax.broadcasted_iota(jnp.int32, (N_EXPERTS, N_EXPERTS), 1)
    lower = jnp.where(ec < er, 1.0, 0.0).astype(BF16)
    off = jnp.dot(lower, nch.astype(F32).astype(BF16), preferred_element_type=F32) * CHUNK
    pos = off + csum - oh
    lp1 = jnp.sum(jnp.where(hit1, pos, 0.0), axis=0, keepdims=True)
    lp2 = jnp.sum(jnp.where(hit2, pos, 0.0), axis=0, keepdims=True)
    zero = jnp.zeros_like(lp1)
    route_t = jnp.concatenate([lp1, lp2, g_p * w1, g_p * w2, zero, zero, zero, zero], axis=0)
    meta = jnp.concatenate([nch[:, 0:LANES], off[:, 0:LANES].astype(jnp.int32)], axis=0)
    return route_t, meta


def _memattn_kernel(x_ref, wq_ref, k_ref, v_ref, wo_ref, g2_ref, b2_ref, wr2_ref, wrh_ref, br_ref, we_ref,
                    o_ref, ob_ref, route_ref, routet_ref, meta_ref, web_ref, wq_b, wo_b):
    web_ref[...] = we_ref[...].astype(BF16)

    @pl.when(jnp.logical_and(pl.program_id(0) == 0, pl.program_id(1) == 0))
    def _():
        _cast_bf16(wq_b, wq_ref)
        _cast_bf16(wo_b, wo_ref)

    def rows_logits(rs):
        x = x_ref[0, rs, :]
        q = jnp.dot(x.astype(BF16), wq_b[...], preferred_element_type=F32)
        q = (q * (MEM_HEAD_DIM ** -0.5)).astype(BF16)
        outs = []
        for h in range(MEM_HEADS):
            sl = slice(h * MEM_HEAD_DIM, (h + 1) * MEM_HEAD_DIM)
            s = _dot_nt(q[:, sl], k_ref[0, :, sl])
            m = jnp.max(s, axis=-1, keepdims=True)
            p = jnp.exp(s - m)
            denom = jnp.sum(p, axis=-1, keepdims=True)
            o = jnp.dot(p.astype(BF16), v_ref[0, :, sl], preferred_element_type=F32)
            outs.append((o / denom).astype(BF16))
        o = jnp.dot(jnp.concatenate(outs, axis=-1), wo_b[...], preferred_element_type=F32)
        x2 = _layer_norm(ALPHA * x + o, g2_ref[...], b2_ref[...])
        o_ref[0, rs, :] = x2
        x2h = x2.astype(BF16)
        ob_ref[0, rs, :] = x2h
        x2l = (x2 - x2h.astype(F32)).astype(BF16)
        hh = jnp.dot(x2h, wr2_ref[...], preferred_element_type=F32)
        return (hh[:, 0:LANES] + hh[:, LANES:2 * LANES]
                + jnp.dot(x2l, wrh_ref[...], preferred_element_type=F32) + br_ref[...])

    ts = x_ref.shape[1]
    groups = [slice(r0, r0 + MOE_TILE) for r0 in range(0, ts, MOE_TILE)]
    logits = rows_logits(slice(0, ts))
    pad = jnp.zeros((LANES - SUBLANES, MOE_TILE), F32)
    for k, rs in enumerate(groups):
        route_t, meta = _route_plan(jnp.transpose(logits[rs, :]))
        routet_ref[k] = route_t
        route_ref[0, rs, :] = jnp.transpose(jnp.concatenate([route_t, pad], axis=0))
        meta_ref[k] = meta


def _memattn(x1, wq, kvm, wo, g2, b2, wr, br, w_expert):
    B, S, D = x1.shape
    ts = MEM_TILE
    assert B * (S // ts) == w_expert.shape[0]
    wes = pl.BlockSpec((1,) + w_expert.shape[1:], lambda b, i: (b * (S // ts) + i, 0, 0))
    per = ts // MOE_TILE
    nt = S // ts
    c = wr * (2.0 ** 16 + 1.0)
    w_high = c - (c - wr)
    wrh = w_high.astype(BF16)
    wr2 = jnp.concatenate([wrh, (wr - w_high).astype(BF16)], axis=1)
    tile = pl.BlockSpec((1, ts, D), lambda b, i: (b, i, 0))
    kspec = pl.BlockSpec((1, MEM_LEN, D), lambda b, i: (b, 0, 0))
    vspec = pl.BlockSpec((1, MEM_LEN, D), lambda b, i: (b, 0, 1))
    return pl.pallas_call(
        _memattn_kernel,
        grid=(B, nt),
        in_specs=[tile, _resident_spec((D, D)), kspec, vspec, _resident_spec((D, D)),
                  _const_spec((1, D)), _const_spec((1, D)),
                  _const_spec((D, 2 * LANES)), _const_spec((D, LANES)), _const_spec((1, LANES)), wes],
        out_specs=[tile, tile, pl.BlockSpec((1, ts, LANES), lambda b, i: (b, i, 0)),
                   pl.BlockSpec((per, SUBLANES, MOE_TILE), lambda b, i: (b * nt + i, 0, 0)),
                   pl.BlockSpec((per, 2 * N_EXPERTS, LANES), lambda b, i: (b * nt + i, 0, 0)), wes],
        out_shape=[jax.ShapeDtypeStruct((B, S, D), F32),
                   jax.ShapeDtypeStruct((B, S, D), BF16),
                   jax.ShapeDtypeStruct((B, S, LANES), F32),
                   jax.ShapeDtypeStruct((B * nt * per, SUBLANES, MOE_TILE), F32),
                   jax.ShapeDtypeStruct((B * nt * per, 2 * N_EXPERTS, LANES), jnp.int32),
                   jax.ShapeDtypeStruct(w_expert.shape, BF16)],
        scratch_shapes=[pltpu.VMEM((D, D), BF16), pltpu.VMEM((D, D), BF16)],
        compiler_params=pltpu.CompilerParams(
            dimension_semantics=("arbitrary", "arbitrary"), vmem_limit_bytes=VMEM_LIMIT),
        name="memattn",
    )(x1, wq, kvm, kvm, wo, g2, b2, wr2, wrh, br, w_expert)


def _gmm_blocks(n_tokens):
    rows = (n_tokens * TOP_K + (n_tokens // MOE_TILE) * N_EXPERTS * (CHUNK - 1)
            + N_EXPERTS * (REGION_ALIGN - CHUNK))
    return -(-rows // GMM_ROWS)


def _plan(meta, T):
    nch = meta[:, :N_EXPERTS, 0]
    n16 = nch * CHUNK
    n_e = jnp.sum(n16, axis=0)
    reg = (n_e + REGION_ALIGN - 1) // REGION_ALIGN * REGION_ALIGN
    gend = jnp.cumsum(reg)
    gbase = gend - reg
    dst = gbase[None, :] + jnp.cumsum(n16, axis=0) - n16
    half_row = jnp.arange(2 * _gmm_blocks(T), dtype=jnp.int32) * REGION_ALIGN
    half_expert = jnp.minimum(jnp.sum(half_row[:, None] >= gend[None, :], axis=1), N_EXPERTS - 1)
    half_used = half_row < gend[-1]
    n_used = ((gend[-1] + GMM_ROWS - 1) // GMM_ROWS).astype(jnp.int32).reshape(1)
    src = (jnp.cumsum(nch, axis=1) - nch) * CHUNK
    n_big = nch // BIG

    def copy_list(count, src0, dst0, rows, length):
        cum = jnp.cumsum(count, axis=1)
        first = (cum - count)[:, None, :]
        k = jnp.arange(length, dtype=jnp.int32)[None, :, None]
        mine = (k >= first) & (k < cum[:, None, :])
        step = (k - first) * rows
        pick = lambda base: jnp.sum(jnp.where(mine, base[:, None, :] + step, 0), axis=2)
        return pick(src0), pick(dst0), cum[:, -1]

    big_src, big_dst, big_n = copy_list(n_big, src, dst, BIG * CHUNK, MAX_BIG)
    rest = n_big * (BIG * CHUNK)
    small_src, small_dst, small_n = copy_list(nch - n_big * BIG, src + rest, dst + rest, CHUNK, MAX_SMALL)
    i32 = lambda a: a.astype(jnp.int32)
    fill_start = jnp.concatenate([gbase + n_e, gend[-1:]])
    fill_n = jnp.concatenate([reg - n_e, _gmm_blocks(T) * GMM_ROWS - gend[-1:]]) // CHUNK
    return dict(big_src=i32(big_src).reshape(-1), big_dst=i32(big_dst).reshape(-1), big_n=i32(big_n),
                small_src=i32(small_src).reshape(-1), small_dst=i32(small_dst).reshape(-1), small_n=i32(small_n),
                fill_start=i32(fill_start), fill_n=i32(fill_n),
                fill_tot=i32(jnp.sum(fill_n)).reshape(1), half_expert=i32(half_expert),
                half_used=i32(half_used), n_used=n_used)


def _rows_copy(src_ref, src_row, dst_ref, dst_row, rows, sem):
    return pltpu.make_async_copy(
        src_ref.at[pl.ds(pl.multiple_of(src_row, CHUNK), rows), :],
        dst_ref.at[pl.ds(pl.multiple_of(dst_row, CHUNK), rows), :], sem)


def _chunk_copy(src_ref, src_row, dst_ref, dst_row, sem):
    return _rows_copy(src_ref, src_row, dst_ref, dst_row, CHUNK, sem)


def _issue_copies(lists, tile, tile_ref, tile_is_src, hbm_ref, sem):
    big_src, big_dst, big_n, small_src, small_dst, small_n = lists
    for src_l, dst_l, n_l, length, rows in ((big_src, big_dst, big_n, MAX_BIG, BIG * CHUNK),
                                            (small_src, small_dst, small_n, MAX_SMALL, CHUNK)):
        def issue(k, carry, src_l=src_l, dst_l=dst_l, length=length, rows=rows):
            local, remote = src_l[tile * length + k], dst_l[tile * length + k]
            if tile_is_src:
                _rows_copy(tile_ref, local, hbm_ref, remote, rows, sem).start()
            else:
                _rows_copy(hbm_ref, remote, tile_ref, local, rows, sem).start()
            return carry

        lax.fori_loop(0, n_l[tile], issue, 0)


def _wait_rows(n, rows, src_ref, dst_ref, sem):
    def body(c, carry):
        pltpu.make_async_copy(src_ref.at[pl.ds(0, rows), :], dst_ref.at[pl.ds(0, rows), :], sem).wait()
        return carry

    lax.fori_loop(0, n, body, 0)


def _wait_copies(lists, tile, src_ref, dst_ref, sem, extra_chunks=0):
    _wait_rows(lists[2][tile], BIG * CHUNK, src_ref, dst_ref, sem)
    n = lists[5][tile] + extra_chunks
    _wait_rows(n // WAIT_GROUP, WAIT_GROUP * CHUNK, src_ref, dst_ref, sem)
    _wait_rows(n % WAIT_GROUP, CHUNK, src_ref, dst_ref, sem)


def _dispatch_kernel(bs_ref, bd_ref, bn_ref, ss_ref, sd_ref, sn_ref, fstart_ref, fn_ref, ftot_ref,
                     lp_ref, x_ref, we_ref, xs_hbm, web_ref, xt, zbuf, sems):
    lists = (bs_ref, bd_ref, bn_ref, ss_ref, sd_ref, sn_ref)
    s = pl.program_id(0)
    ns = pl.num_programs(0)
    cur = (s % 2) * PER_STEP
    prv = PER_STEP - cur

    @pl.when(s >= 2)
    def _():
        for k in range(PER_STEP):
            _wait_copies(lists, (s - 2) * PER_STEP + k, xt.at[cur + k], xs_hbm, sems.at[cur + k])

    r = lax.broadcasted_iota(jnp.int32, (SORT_ROWS, MOE_TILE), 0)
    for k in range(PER_STEP):
        lp = lp_ref[k].astype(jnp.int32)
        hit = jnp.logical_or(lp[0:1, :] == r, lp[1:2, :] == r)
        p = jnp.where(hit, 1.0, 0.0).astype(BF16)
        x = x_ref[k * MOE_TILE:(k + 1) * MOE_TILE, :]
        xt[cur + k] = jnp.dot(p, x, preferred_element_type=F32).astype(BF16)

    for k in range(PER_STEP):
        _issue_copies(lists, s * PER_STEP + k, xt.at[cur + k], True, xs_hbm, sems.at[cur + k])

    _cast_bf16(web_ref.at[0], we_ref.at[0])

    @pl.when(s == ns - 1)
    def _():
        zbuf[...] = jnp.zeros_like(zbuf)
        sem = sems.at[cur]

        def per_range(e, carry):
            def issue(c, carry2):
                _chunk_copy(zbuf, 0, xs_hbm, fstart_ref[e] + c * CHUNK, sem).start()
                return carry2

            return lax.fori_loop(0, fn_ref[e], issue, carry)

        lax.fori_loop(0, N_EXPERTS + 1, per_range, 0)
        for k in range(PER_STEP):
            _wait_copies(lists, (s - 1) * PER_STEP + k, xt.at[prv + k], xs_hbm, sems.at[prv + k])
            _wait_copies(lists, s * PER_STEP + k, xt.at[cur + k], xs_hbm, sems.at[cur + k],
                         extra_chunks=ftot_ref[0] if k == 0 else 0)


def _copy_lists(plan):
    return tuple(plan[k] for k in ("big_src", "big_dst", "big_n", "small_src", "small_dst", "small_n"))


def _dispatch(plan, lpt, x2b, w_expert):
    T, D = x2b.shape
    rows = PER_STEP * MOE_TILE
    assert T // rows >= 2
    assert T // rows == N_EXPERTS
    wes = pl.BlockSpec((1,) + w_expert.shape[1:], lambda t, *_: (t, 0, 0))
    grid_spec = pltpu.PrefetchScalarGridSpec(
        num_scalar_prefetch=9,
        grid=(T // rows,),
        in_specs=[pl.BlockSpec((PER_STEP, SUBLANES, MOE_TILE), lambda t, *_: (t, 0, 0)),
                  pl.BlockSpec((rows, D), lambda t, *_: (t, 0)), wes],
        out_specs=[pl.BlockSpec(memory_space=pl.ANY), wes],
        scratch_shapes=[pltpu.VMEM((2 * PER_STEP, SORT_ROWS, D), BF16), pltpu.VMEM((CHUNK, D), BF16),
                        pltpu.SemaphoreType.DMA((2 * PER_STEP,))],
    )
    return pl.pallas_call(
        _dispatch_kernel,
        grid_spec=grid_spec,
        out_shape=[jax.ShapeDtypeStruct((_gmm_blocks(T) * GMM_ROWS, D), BF16),
                   jax.ShapeDtypeStruct(w_expert.shape, BF16)],
        compiler_params=pltpu.CompilerParams(
            dimension_semantics=("arbitrary",), vmem_limit_bytes=VMEM_LIMIT),
        name="moe_dispatch",
    )(*_copy_lists(plan), plan["fill_start"], plan["fill_n"], plan["fill_tot"], lpt, x2b, w_expert)


def _expert_ffn(xb, wg_b, wu_b, wd_b):
    g = jnp.dot(xb, wg_b[...], preferred_element_type=F32)
    u = jnp.dot(xb, wu_b[...], preferred_element_type=F32)
    h = (g * jax.nn.sigmoid(g)) * u
    return jnp.dot(h.astype(BF16), wd_b[...], preferred_element_type=F32).astype(BF16)


def _gmm_kernel(he_ref, hu_ref, nu_ref, x_ref, wga_ref, wua_ref, wda_ref, wgb_ref, wub_ref, wdb_ref, o_ref):
    b = pl.program_id(0)
    half = REGION_ALIGN
    same = he_ref[2 * b] == he_ref[2 * b + 1]
    used0 = hu_ref[2 * b] != 0
    used1 = hu_ref[2 * b + 1] != 0
    set_a = (wga_ref.at[0], wua_ref.at[0], wda_ref.at[0])
    set_b = (wgb_ref.at[0], wub_ref.at[0], wdb_ref.at[0])

    @pl.when(jnp.logical_and(used1, same))
    def _():
        o_ref[...] = _expert_ffn(x_ref[...], *set_a)

    @pl.when(jnp.logical_and(used0, jnp.logical_not(jnp.logical_and(used1, same))))
    def _():
        o_ref[0:half, :] = _expert_ffn(x_ref[0:half, :], *set_a)

    @pl.when(jnp.logical_and(used1, jnp.logical_not(same)))
    def _():
        o_ref[half:GMM_ROWS, :] = _expert_ffn(x_ref[half:GMM_ROWS, :], *set_b)

    @pl.when(jnp.logical_not(used0))
    def _():
        o_ref[0:half, :] = jnp.zeros((half, o_ref.shape[1]), o_ref.dtype)

    @pl.when(jnp.logical_not(used1))
    def _():
        o_ref[half:GMM_ROWS, :] = jnp.zeros((GMM_ROWS - half, o_ref.shape[1]), o_ref.dtype)


def _gmm(plan, xs, wg, wu, wd):
    R, D = xs.shape
    rows = pl.BlockSpec((GMM_ROWS, D), lambda b, he, hu, nu: (jnp.minimum(b, nu[0] - 1), 0))

    def wspec(shape, h):
        return pl.BlockSpec((1,) + shape, lambda b, he, hu, nu: (he[2 * b + h], 0, 0))

    grid_spec = pltpu.PrefetchScalarGridSpec(
        num_scalar_prefetch=3,
        grid=(R // GMM_ROWS,),
        in_specs=[rows,
                  wspec((D, D_EXPERT), 0), wspec((D, D_EXPERT), 0), wspec((D_EXPERT, D), 0),
                  wspec((D, D_EXPERT), 1), wspec((D, D_EXPERT), 1), wspec((D_EXPERT, D), 1)],
        out_specs=pl.BlockSpec((GMM_ROWS, D), lambda b, he, hu, nu: (b, 0)),
    )
    return pl.pallas_call(
        _gmm_kernel,
        grid_spec=grid_spec,
        out_shape=jax.ShapeDtypeStruct((R, D), BF16),
        compiler_params=pltpu.CompilerParams(
            dimension_semantics=("arbitrary",), vmem_limit_bytes=VMEM_LIMIT),
        name="moe_gmm",
    )(plan["half_expert"], plan["half_used"], plan["n_used"], xs, wg, wu, wd, wg, wu, wd)


def _combine_kernel(bs_ref, bd_ref, bn_ref, ss_ref, sd_ref, sn_ref, cm_ref, x_ref, ys_hbm, g3_ref, b3_ref,
                    o_ref, yt, sems):
    lists = (bs_ref, bd_ref, bn_ref, ss_ref, sd_ref, sn_ref)
    s = pl.program_id(0)
    ns = pl.num_programs(0)
    cur = (s % 2) * PER_STEP
    nxt = PER_STEP - cur

    def fetch(step, first_buf):
        for k in range(PER_STEP):
            _issue_copies(lists, step * PER_STEP + k, yt.at[first_buf + k], False, ys_hbm,
                          sems.at[first_buf + k])

    @pl.when(s == 0)
    def _():
        yt[...] = jnp.zeros_like(yt)
        fetch(0, 0)

    @pl.when(s + 1 < ns)
    def _():
        fetch(s + 1, nxt)

    for k in range(PER_STEP):
        _wait_copies(lists, s * PER_STEP + k, ys_hbm, yt.at[cur + k], sems.at[cur + k])

    col = lax.broadcasted_iota(jnp.int32, (MOE_TILE, SORT_ROWS), 1)
    for k in range(PER_STEP):
        rs = slice(k * MOE_TILE, (k + 1) * MOE_TILE)
        cm = cm_ref[rs, :]
        lp0 = cm[:, 0:1].astype(jnp.int32)
        lp1 = cm[:, 1:2].astype(jnp.int32)
        w = jnp.where(col == lp0, cm[:, 2:3], jnp.where(col == lp1, cm[:, 3:4], 0.0)).astype(BF16)
        y = jnp.dot(w, yt[cur + k], preferred_element_type=F32)
        o_ref[rs, :] = _layer_norm(ALPHA * x_ref[rs, :] + y, g3_ref[...], b3_ref[...])


def _combine(plan, route, x2, ys, g3, b3):
    T, D = x2.shape
    rows = PER_STEP * MOE_TILE
    grid_spec = pltpu.PrefetchScalarGridSpec(
        num_scalar_prefetch=6,
        grid=(T // rows,),
        in_specs=[pl.BlockSpec((rows, LANES), lambda t, *_: (t, 0)),
                  pl.BlockSpec((rows, D), lambda t, *_: (t, 0)),
                  pl.BlockSpec(memory_space=pl.ANY),
                  pl.BlockSpec((1, D), lambda t, *_: (0, 0)),
                  pl.BlockSpec((1, D), lambda t, *_: (0, 0))],
        out_specs=pl.BlockSpec((rows, D), lambda t, *_: (t, 0)),
        scratch_shapes=[pltpu.VMEM((2 * PER_STEP, SORT_ROWS, D), BF16),
                        pltpu.SemaphoreType.DMA((2 * PER_STEP,))],
    )
    return pl.pallas_call(
        _combine_kernel,
        grid_spec=grid_spec,
        out_shape=jax.ShapeDtypeStruct((T, D), F32),
        compiler_params=pltpu.CompilerParams(
            dimension_semantics=("arbitrary",), vmem_limit_bytes=VMEM_LIMIT),
        name="moe_combine",
    )(*_copy_lists(plan), route, x2, ys, g3, b3)


def _moe(x2, x2b, route, lpt, meta, wg, wu_b, wd_b, g3, b3):
    plan = _plan(meta, x2.shape[0])
    xs, wg_b = _dispatch(plan, lpt, x2b, wg)
    ys = _gmm(plan, xs, wg_b, wu_b, wd_b)
    return _combine(plan, route, x2, ys, g3, b3)


def _row(v):
    return v.reshape(1, -1).astype(F32)


def kernel(x, mem, w_in, b_in, w_dw, b_dw, g_conv_norm, b_conv_norm, attn_sinks, w_out, g_ln1, b_ln1,
           w_mq, w_mkv, w_mo, g_ln2, b_ln2, w_group, b_group, w_router, b_router, w_gate, w_up, w_down,
           g_ln3, b_ln3):
    B, S, D = x.shape
    for l in range(DEPTH):
        w_dw_p = jnp.zeros((CONV_HALO, CONV_CH), F32).at[:CONV_WIDTH].set(w_dw[l])
        x1, wd_b = _mixer(x, attn_sinks[l].astype(F32), w_in[l], _row(b_in[l]), w_dw_p,
                          _row(b_dw[l]), _row(g_conv_norm[l]), _row(b_conv_norm[l]),
                          w_out[l], _row(g_ln1[l]), _row(b_ln1[l]), w_down[l])

        kvm = _memkv(mem.reshape(B * MEM_LEN, D), w_mkv[l]).reshape(B, MEM_LEN, 2 * D)

        wr = jnp.concatenate(
            [w_group[l], jnp.transpose(w_router[l], (1, 0, 2)).reshape(D, N_EXPERTS)], axis=1)
        wr = jnp.pad(wr, ((0, 0), (0, LANES - wr.shape[1])))
        br = jnp.pad(jnp.concatenate([b_group[l], b_router[l].reshape(-1)]), (0, LANES - N_GROUPS - N_EXPERTS))
        x2, x2b, route, lpt, meta, wu_b = _memattn(x1, w_mq[l], kvm, w_mo[l], _row(g_ln2[l]), _row(b_ln2[l]),
                                                   wr.astype(F32), _row(br), w_up[l])

        T = B * S
        y = _moe(x2.reshape(T, D), x2b.reshape(T, D), route.reshape(T, LANES), lpt, meta,
                 w_gate[l], wu_b, wd_b,
                 _row(g_ln3[l]), _row(b_ln3[l]))
        x = y.reshape(B, S, D)
    return x
```

```python
import functools

import jax
import jax.numpy as jnp
from jax import lax
from jax.experimental import pallas as pl
from jax.experimental.pallas import tpu as pltpu

D_MODEL = 1024
MEM_LEN = 256
CONV_CH = 512
CONV_WIDTH = 31
N_HEADS = 8
N_KV_HEADS = 2
HEAD_DIM = 64
GQ = N_HEADS // N_KV_HEADS
ATTN_W = N_HEADS * HEAD_DIM
KV_W = N_KV_HEADS * HEAD_DIM
WINDOW = 128
D_MIX = CONV_CH + ATTN_W
D_IN = 2 * CONV_CH + ATTN_W + 2 * KV_W
MEM_HEADS = 4
MEM_HEAD_DIM = D_MODEL // MEM_HEADS
N_GROUPS = 4
EXPERTS_PER_GROUP = 4
N_EXPERTS = N_GROUPS * EXPERTS_PER_GROUP
D_EXPERT = D_MODEL // 2
DEPTH = 1
ALPHA = (2.0 * DEPTH) ** 0.25
LN_EPS = 1e-5

LANES = 128
SUBLANES = 8
CONV_ROWS = 128
LN_ROWS = 64
MEM_TILE = 1024
MASK_VALUE = -1e30
CONV_HALO = 32
SEQ_TILE = 512
MOE_TILE = 512
CHUNK = 16
REGION_ALIGN = 512
GMM_ROWS = 2 * REGION_ALIGN
TOP_K = 2
SORT_ROWS = -(-(MOE_TILE * TOP_K + N_EXPERTS * (CHUNK - 1)) // 256) * 256
BIG = 4
MAX_BIG = SORT_ROWS // (BIG * CHUNK)
MAX_SMALL = N_EXPERTS * (BIG - 1)
PER_STEP = 2
WAIT_GROUP = 8
ROUTE_OFF = N_GROUPS
VMEM_LIMIT = 56 * 1024 * 1024

BF16 = jnp.bfloat16
F32 = jnp.float32


def _layer_norm(x, g, b):
    mu = jnp.mean(x, axis=-1, keepdims=True)
    xc = x - mu
    var = jnp.mean(xc * xc, axis=-1, keepdims=True)
    return xc * lax.rsqrt(var + LN_EPS) * g + b


def _cast_bf16(dst_ref, src_ref):
    rows = 256
    for r0 in range(0, src_ref.shape[0], rows):
        dst_ref[r0:r0 + rows, :] = src_ref[r0:r0 + rows, :].astype(BF16)


def _dot_nt(a, b):
    return lax.dot_general(a, b, (((1,), (1,)), ((), ())), preferred_element_type=F32)


def _mixer_kernel(sinks_ref, x_ref, w_in_ref, b_in_ref, w_dw_ref, conv_vec_ref, w_out_ref, ln_ref, we_ref,
                  o_ref, web_ref, w_in_b, w_out_b, hbuf, hshift, cbuf, qbuf, kbuf, vbuf, ymix):
    b_dw_ref, g_cn_ref, b_cn_ref = (conv_vec_ref.at[k:k + 1, :] for k in range(3))
    g1_ref, b1_ref = ln_ref.at[0:1, :], ln_ref.at[1:2, :]
    i = pl.program_id(1)
    ts = SEQ_TILE
    web_ref[...] = we_ref[...].astype(BF16)

    @pl.when(jnp.logical_and(pl.program_id(0) == 0, i == 0))
    def _():
        _cast_bf16(w_in_b, w_in_ref)
        _cast_bf16(w_out_b, w_out_ref)

    @pl.when(i == 0)
    def _():
        hbuf[0:CONV_HALO, :] = jnp.zeros((CONV_HALO, CONV_CH), F32)
        kbuf[:, 0:WINDOW, :] = jnp.zeros((2 * N_KV_HEADS, WINDOW, KV_W), BF16)
        vbuf[:, 0:WINDOW, :] = jnp.zeros((2, WINDOW, KV_W), BF16)

    x = x_ref[0]
    u = jnp.dot(x.astype(BF16), w_in_b[...], preferred_element_type=F32) + b_in_ref[...]
    a = u[:, 0:CONV_CH]
    gate = u[:, CONV_CH:2 * CONV_CH]
    hbuf[CONV_HALO:CONV_HALO + ts, :] = a * jax.nn.sigmoid(gate)
    qbuf[...] = (u[:, 2 * CONV_CH:2 * CONV_CH + ATTN_W] * (HEAD_DIM ** -0.5)).astype(BF16)
    kf = u[:, 2 * CONV_CH + ATTN_W:2 * CONV_CH + ATTN_W + KV_W]
    vf = u[:, 2 * CONV_CH + ATTN_W + KV_W:D_IN]
    kr = pltpu.roll(kf, HEAD_DIM, axis=1)
    vr = pltpu.roll(vf, HEAD_DIM, axis=1)
    lo = lax.broadcasted_iota(jnp.int32, (ts, KV_W), 1) < HEAD_DIM
    rows = slice(WINDOW, WINDOW + ts)
    kbuf[0, rows, :] = jnp.where(lo, kf, 0.0).astype(BF16)
    kbuf[1, rows, :] = jnp.where(lo, 0.0, kr).astype(BF16)
    kbuf[2, rows, :] = jnp.where(lo, kr, 0.0).astype(BF16)
    kbuf[3, rows, :] = jnp.where(lo, 0.0, kf).astype(BF16)
    vbuf[0, rows, :] = vf.astype(BF16)
    vbuf[1, rows, :] = vr.astype(BF16)

    base = CONV_HALO - (CONV_WIDTH - 1)
    n_shift = ts + CONV_HALO - SUBLANES
    for b in range(1, SUBLANES):
        hshift[b - 1, 0:n_shift, :] = hbuf[b:b + n_shift, :]
    rc = CONV_ROWS

    def conv_chunk(c):
        r0 = c * rc
        for l in range(CONV_CH // LANES):
            ls = slice(l * LANES, (l + 1) * LANES)
            acc = jnp.zeros((rc, LANES), F32)
            for j in range(CONV_WIDTH):
                a8, b = divmod(j + base, SUBLANES)
                rs = slice(r0 + SUBLANES * a8, r0 + SUBLANES * a8 + rc)
                tap = hbuf[rs, ls] if b == 0 else hshift[b - 1, rs, ls]
                acc = acc + tap * w_dw_ref[j:j + 1, ls]
            cbuf[r0:r0 + rc, ls] = acc
        for r1 in range(r0, r0 + rc, LN_ROWS):
            rs = slice(r1, r1 + LN_ROWS)
            y = _layer_norm(cbuf[rs, :] + b_dw_ref[...], g_cn_ref[...], b_cn_ref[...])
            y = y * jax.nn.sigmoid(y)
            ymix[rs, 0:CONV_CH] = y.astype(BF16)

    qi = lax.broadcasted_iota(jnp.int32, (2 * WINDOW, WINDOW), 0) % WINDOW
    kj = lax.broadcasted_iota(jnp.int32, (2 * WINDOW, WINDOW), 1)
    own = kj <= qi
    top = lax.broadcasted_iota(jnp.int32, (2 * WINDOW, 1), 0) < WINDOW
    lo_out = lax.broadcasted_iota(jnp.int32, (WINDOW, 2 * HEAD_DIM), 1) < HEAD_DIM

    def attn_block(jb):
        r0 = jb * WINDOW
        prev_ok = jnp.logical_not(own) & (i != 0) if jb == 0 else jnp.logical_not(own)
        for kvh in range(N_KV_HEADS):
            h0 = kvh * GQ
            c0 = h0 * HEAD_DIM
            qs = jnp.concatenate([qbuf[r0:r0 + WINDOW, c0:c0 + 2 * HEAD_DIM],
                                  qbuf[r0:r0 + WINDOW, c0 + 2 * HEAD_DIM:c0 + 4 * HEAD_DIM]], axis=0)
            pv = []
            for par in range(2):
                kk = kbuf[2 * kvh + par, r0:r0 + 2 * WINDOW, :]
                vv = vbuf[(kvh + par) % 2, r0:r0 + 2 * WINDOW, :]
                s2 = _dot_nt(qs, kk)
                s = jnp.where(own, s2[:, WINDOW:], jnp.where(prev_ok, s2[:, :WINDOW], MASK_VALUE))
                sink = jnp.where(top, sinks_ref[h0 + par], sinks_ref[h0 + 2 + par])
                m = jnp.maximum(jnp.max(s, axis=-1, keepdims=True), sink)
                p = jnp.exp(s - m)
                denom = jnp.sum(p, axis=-1, keepdims=True) + jnp.exp(sink - m)
                zero = jnp.zeros((), BF16)
                pb = p.astype(BF16)
                p2 = jnp.concatenate([jnp.where(own, zero, pb), jnp.where(own, pb, zero)], axis=1)
                pv.append(jnp.dot(p2, vv, preferred_element_type=F32) / denom)
            for pair in range(2):
                rs = slice(pair * WINDOW, (pair + 1) * WINDOW)
                o = jnp.where(lo_out, pv[0][rs], pv[1][rs])
                cs = CONV_CH + c0 + pair * 2 * HEAD_DIM
                ymix[r0:r0 + WINDOW, cs:cs + 2 * HEAD_DIM] = o.astype(BF16)

    assert ts // rc == ts // WINDOW
    for c in range(ts // rc):
        attn_block(c)
        conv_chunk(c)

    mix = jnp.dot(ymix[...], w_out_b[...], preferred_element_type=F32)
    o_ref[0] = _layer_norm(ALPHA * x + mix, g1_ref[...], b1_ref[...])

    hbuf[0:CONV_HALO, :] = hbuf[ts:ts + CONV_HALO, :]
    kbuf[:, 0:WINDOW, :] = kbuf[:, ts:ts + WINDOW, :]
    vbuf[:, 0:WINDOW, :] = vbuf[:, ts:ts + WINDOW, :]


def _const_spec(shape):
    nd = len(shape)
    return pl.BlockSpec(shape, lambda *_: (0,) * nd)


def _resident_spec(shape):
    nd = len(shape)
    return pl.BlockSpec(shape, lambda *_: (0,) * nd, pipeline_mode=pl.Buffered(1))


def _mixer(x, sinks, w_in, b_in, w_dw, conv_vec, w_out, ln_vec, w_expert):
    B, S, D = x.shape
    ts = SEQ_TILE
    tile = pl.BlockSpec((1, ts, D), lambda b, i: (b, i, 0))
    nt = S // ts
    n_e, we_rows, we_cols = w_expert.shape
    per = (B * nt) // n_e
    assert per * n_e == B * nt and we_rows % per == 0
    wes = pl.BlockSpec((1, we_rows // per, we_cols), lambda b, i: ((b * nt + i) // per, (b * nt + i) % per, 0))
    return pl.pallas_call(
        _mixer_kernel,
        grid=(B, S // ts),
        in_specs=[
            pl.BlockSpec(memory_space=pltpu.SMEM),
            tile,
            _resident_spec((D, D_IN)), _const_spec((1, D_IN)),
            _const_spec((CONV_HALO, CONV_CH)), _const_spec((SUBLANES, CONV_CH)),
            _resident_spec((D_MIX, D)), _const_spec((SUBLANES, D)),
            wes,
        ],
        out_specs=[tile, wes],
        out_shape=[jax.ShapeDtypeStruct((B, S, D), F32), jax.ShapeDtypeStruct(w_expert.shape, BF16)],
        scratch_shapes=[
            pltpu.VMEM((D, D_IN), BF16),
            pltpu.VMEM((D_MIX, D), BF16),
            pltpu.VMEM((CONV_HALO + ts, CONV_CH), F32),
            pltpu.VMEM((SUBLANES - 1, CONV_HALO + ts, CONV_CH), F32),
            pltpu.VMEM((ts, CONV_CH), F32),
            pltpu.VMEM((ts, ATTN_W), BF16),
            pltpu.VMEM((2 * N_KV_HEADS, WINDOW + ts, KV_W), BF16),
            pltpu.VMEM((2, WINDOW + ts, KV_W), BF16),
            pltpu.VMEM((ts, D_MIX), BF16),
        ],
        compiler_params=pltpu.CompilerParams(
            dimension_semantics=("arbitrary", "arbitrary"), vmem_limit_bytes=VMEM_LIMIT),
        name="mixer",
    )(sinks, x, w_in, b_in, w_dw, conv_vec, w_out, ln_vec, w_expert)


def _memkv_kernel(mem_ref, w_ref, o_ref):
    o_ref[...] = jnp.dot(mem_ref[...].astype(BF16), w_ref[...].astype(BF16),
                         preferred_element_type=F32).astype(BF16)


def _memkv(mem2d, w_mkv):
    M, D = mem2d.shape
    N = w_mkv.shape[1]
    tn = 512
    return pl.pallas_call(
        _memkv_kernel,
        grid=(N // tn,),
        in_specs=[pl.BlockSpec((M, D), lambda j: (0, 0)), pl.BlockSpec((D, tn), lambda j: (0, j))],
        out_specs=pl.BlockSpec((M, tn), lambda j: (0, j)),
        out_shape=jax.ShapeDtypeStruct((M, N), BF16),
        compiler_params=pltpu.CompilerParams(dimension_semantics=("arbitrary",)),
        name="memkv",
    )(mem2d, w_mkv)


def _first_max(rows):
    best = rows[0]
    for r in rows[1:]:
        best = jnp.maximum(best, r)
    idx = jnp.full(best.shape, len(rows) - 1, jnp.int32)
    for k in range(len(rows) - 2, -1, -1):
        idx = jnp.where(rows[k] == best, k, idx)
    return best, idx


def _route_plan(logits_t):
    tile = logits_t.shape[1]
    row = lambda k: logits_t[k:k + 1, :]
    gmax, g_idx = _first_max([row(g) for g in range(N_GROUPS)])
    gsum = jnp.exp(row(0) - gmax)
    for g in range(1, N_GROUPS):
        gsum = gsum + jnp.exp(row(g) - gmax)
    g_p = 1.0 / gsum
    rl = []
    for e in range(EXPERTS_PER_GROUP):
        v = row(ROUTE_OFF + (N_GROUPS - 1) * EXPERTS_PER_GROUP + e)
        for g in range(N_GROUPS - 2, -1, -1):
            v = jnp.where(g_idx == g, row(ROUTE_OFF + g * EXPERTS_PER_GROUP + e), v)
        rl.append(v)
    m1, i1 = _first_max(rl)
    m2, i2 = _first_max([jnp.where(i1 == e, MASK_VALUE, rl[e]) for e in range(EXPERTS_PER_GROUP)])
    ex = jnp.exp(m2 - m1)
    w1 = 1.0 / (1.0 + ex)
    w2 = ex * w1
    e1 = g_idx * EXPERTS_PER_GROUP + i1
    e2 = g_idx * EXPERTS_PER_GROUP + i2

    eid = lax.broadcasted_iota(jnp.int32, (N_EXPERTS, tile), 0)
    hit1 = eid == e1
    hit2 = eid == e2
    oh = jnp.where(jnp.logical_or(hit1, hit2), 1.0, 0.0)
    r = lax.broadcasted_iota(jnp.int32, (tile, tile), 0)
    c = lax.broadcasted_iota(jnp.int32, (tile, tile), 1)
    tri = jnp.where(r <= c, 1.0, 0.0).astype(BF16)
    csum = jnp.dot(oh.astype(BF16), tri, preferred_element_type=F32)
    counts = jnp.broadcast_to(csum[:, tile - 1:tile], (N_EXPERTS, tile)).astype(jnp.int32)
    nch = jnp.right_shift(counts + (CHUNK - 1), CHUNK.bit_length() - 1)
    er = lax.broadcasted_iota(jnp.int32, (N_EXPERTS, N_EXPERTS), 0)
    ec = lax.broadcasted_iota(jnp.int32, (N_EXPERTS, N_EXPERTS), 1)
    lower = jnp.where(ec < er, 1.0, 0.0).astype(BF16)
    off = jnp.dot(lower, nch.astype(F32).astype(BF16), preferred_element_type=F32) * CHUNK
    pos = off + csum - oh
    lp1 = jnp.sum(jnp.where(hit1, pos, 0.0), axis=0, keepdims=True)
    lp2 = jnp.sum(jnp.where(hit2, pos, 0.0), axis=0, keepdims=True)
    zero = jnp.zeros_like(lp1)
    route_t = jnp.concatenate([lp1, lp2, g_p * w1, g_p * w2, zero, zero, zero, zero], axis=0)
    meta = jnp.concatenate([nch[:, 0:LANES], off[:, 0:LANES].astype(jnp.int32)], axis=0)
    return route_t, meta


def _memattn_kernel(x_ref, wq_ref, k_ref, v_ref, wo_ref, ln_ref, wr2_ref, wrh_ref, br_ref, we_ref,
                    o_ref, ob_ref, route_ref, routet_ref, meta_ref, web_ref, wq_b, wo_b):
    g2_ref, b2_ref = ln_ref.at[2:3, :], ln_ref.at[3:4, :]
    web_ref[...] = we_ref[...].astype(BF16)

    @pl.when(jnp.logical_and(pl.program_id(0) == 0, pl.program_id(1) == 0))
    def _():
        _cast_bf16(wq_b, wq_ref)
        _cast_bf16(wo_b, wo_ref)

    def rows_logits(rs):
        x = x_ref[0, rs, :]
        q = jnp.dot(x.astype(BF16), wq_b[...], preferred_element_type=F32)
        q = (q * (MEM_HEAD_DIM ** -0.5)).astype(BF16)
        outs = []
        for h in range(MEM_HEADS):
            sl = slice(h * MEM_HEAD_DIM, (h + 1) * MEM_HEAD_DIM)
            s = _dot_nt(q[:, sl], k_ref[0, :, sl])
            m = jnp.max(s, axis=-1, keepdims=True)
            p = jnp.exp(s - m)
            denom = jnp.sum(p, axis=-1, keepdims=True)
            o = jnp.dot(p.astype(BF16), v_ref[0, :, sl], preferred_element_type=F32)
            outs.append((o / denom).astype(BF16))
        o = jnp.dot(jnp.concatenate(outs, axis=-1), wo_b[...], preferred_element_type=F32)
        x2 = _layer_norm(ALPHA * x + o, g2_ref[...], b2_ref[...])
        o_ref[0, rs, :] = x2
        x2h = x2.astype(BF16)
        ob_ref[0, rs, :] = x2h
        x2l = (x2 - x2h.astype(F32)).astype(BF16)
        hh = jnp.dot(x2h, wr2_ref[...], preferred_element_type=F32)
        return (hh[:, 0:LANES] + hh[:, LANES:2 * LANES]
                + jnp.dot(x2l, wrh_ref[...], preferred_element_type=F32) + br_ref[...])

    ts = x_ref.shape[1]
    groups = [slice(r0, r0 + MOE_TILE) for r0 in range(0, ts, MOE_TILE)]
    logits = rows_logits(slice(0, ts))
    pad = jnp.zeros((LANES - SUBLANES, MOE_TILE), F32)
    for k, rs in enumerate(groups):
        route_t, meta = _route_plan(jnp.transpose(logits[rs, :]))
        routet_ref[k] = route_t
        route_ref[0, rs, :] = jnp.transpose(jnp.concatenate([route_t, pad], axis=0))
        meta_ref[k] = meta


def _memattn(x1, wq, kvm, wo, ln_vec, wr, br, w_expert):
    B, S, D = x1.shape
    ts = MEM_TILE
    assert B * (S // ts) == w_expert.shape[0]
    wes = pl.BlockSpec((1,) + w_expert.shape[1:], lambda b, i: (b * (S // ts) + i, 0, 0))
    per = ts // MOE_TILE
    nt = S // ts
    c = wr * (2.0 ** 16 + 1.0)
    w_high = c - (c - wr)
    wrh = w_high.astype(BF16)
    wr2 = jnp.concatenate([wrh, (wr - w_high).astype(BF16)], axis=1)
    tile = pl.BlockSpec((1, ts, D), lambda b, i: (b, i, 0))
    kspec = pl.BlockSpec((1, MEM_LEN, D), lambda b, i: (b, 0, 0))
    vspec = pl.BlockSpec((1, MEM_LEN, D), lambda b, i: (b, 0, 1))
    return pl.pallas_call(
        _memattn_kernel,
        grid=(B, nt),
        in_specs=[tile, _resident_spec((D, D)), kspec, vspec, _resident_spec((D, D)),
                  _const_spec((SUBLANES, D)),
                  _const_spec((D, 2 * LANES)), _const_spec((D, LANES)), _const_spec((1, LANES)), wes],
        out_specs=[tile, tile, pl.BlockSpec((1, ts, LANES), lambda b, i: (b, i, 0)),
                   pl.BlockSpec((per, SUBLANES, MOE_TILE), lambda b, i: (b * nt + i, 0, 0)),
                   pl.BlockSpec((per, 2 * N_EXPERTS, LANES), lambda b, i: (b * nt + i, 0, 0)), wes],
        out_shape=[jax.ShapeDtypeStruct((B, S, D), F32),
                   jax.ShapeDtypeStruct((B, S, D), BF16),
                   jax.ShapeDtypeStruct((B, S, LANES), F32),
                   jax.ShapeDtypeStruct((B * nt * per, SUBLANES, MOE_TILE), F32),
                   jax.ShapeDtypeStruct((B * nt * per, 2 * N_EXPERTS, LANES), jnp.int32),
                   jax.ShapeDtypeStruct(w_expert.shape, BF16)],
        scratch_shapes=[pltpu.VMEM((D, D), BF16), pltpu.VMEM((D, D), BF16)],
        compiler_params=pltpu.CompilerParams(
            dimension_semantics=("arbitrary", "arbitrary"), vmem_limit_bytes=VMEM_LIMIT),
        name="memattn",
    )(x1, wq, kvm, kvm, wo, ln_vec, wr2, wrh, br, w_expert)


def _gmm_blocks(n_tokens):
    rows = (n_tokens * TOP_K + (n_tokens // MOE_TILE) * N_EXPERTS * (CHUNK - 1)
            + N_EXPERTS * (REGION_ALIGN - CHUNK))
    return -(-rows // GMM_ROWS)


def _plan(meta, T):
    nch = meta[:, :N_EXPERTS, 0]
    n16 = nch * CHUNK
    n_e = jnp.sum(n16, axis=0)
    reg = (n_e + REGION_ALIGN - 1) // REGION_ALIGN * REGION_ALIGN
    gend = jnp.cumsum(reg)
    gbase = gend - reg
    dst = gbase[None, :] + jnp.cumsum(n16, axis=0) - n16
    half_row = jnp.arange(2 * _gmm_blocks(T), dtype=jnp.int32) * REGION_ALIGN
    half_expert = jnp.minimum(jnp.sum(half_row[:, None] >= gend[None, :], axis=1), N_EXPERTS - 1)
    half_used = half_row < gend[-1]
    n_used = ((gend[-1] + GMM_ROWS - 1) // GMM_ROWS).astype(jnp.int32).reshape(1)
    src = (jnp.cumsum(nch, axis=1) - nch) * CHUNK
    n_big = nch // BIG

    def copy_list(count, src0, dst0, rows, length):
        cum = jnp.cumsum(count, axis=1)
        first = (cum - count)[:, None, :]
        k = jnp.arange(length, dtype=jnp.int32)[None, :, None]
        mine = (k >= first) & (k < cum[:, None, :])
        step = (k - first) * rows
        pick = lambda base: jnp.sum(jnp.where(mine, base[:, None, :] + step, 0), axis=2)
        return pick(src0), pick(dst0), cum[:, -1]

    big_src, big_dst, big_n = copy_list(n_big, src, dst, BIG * CHUNK, MAX_BIG)
    rest = n_big * (BIG * CHUNK)
    small_src, small_dst, small_n = copy_list(nch - n_big * BIG, src + rest, dst + rest, CHUNK, MAX_SMALL)
    i32 = lambda a: a.astype(jnp.int32)
    fill_start = jnp.concatenate([gbase + n_e, gend[-1:]])
    fill_n = jnp.concatenate([reg - n_e, _gmm_blocks(T) * GMM_ROWS - gend[-1:]]) // CHUNK
    return dict(big_src=i32(big_src).reshape(-1), big_dst=i32(big_dst).reshape(-1), big_n=i32(big_n),
                small_src=i32(small_src).reshape(-1), small_dst=i32(small_dst).reshape(-1), small_n=i32(small_n),
                fill_start=i32(fill_start), fill_n=i32(fill_n),
                fill_tot=i32(jnp.sum(fill_n)).reshape(1), half_expert=i32(half_expert),
                half_used=i32(half_used), n_used=n_used)


def _rows_copy(src_ref, src_row, dst_ref, dst_row, rows, sem):
    return pltpu.make_async_copy(
        src_ref.at[pl.ds(pl.multiple_of(src_row, CHUNK), rows), :],
        dst_ref.at[pl.ds(pl.multiple_of(dst_row, CHUNK), rows), :], sem)


def _chunk_copy(src_ref, src_row, dst_ref, dst_row, sem):
    return _rows_copy(src_ref, src_row, dst_ref, dst_row, CHUNK, sem)


def _issue_copies(lists, tile, tile_ref, tile_is_src, hbm_ref, sem):
    big_src, big_dst, big_n, small_src, small_dst, small_n = lists
    for src_l, dst_l, n_l, length, rows in ((big_src, big_dst, big_n, MAX_BIG, BIG * CHUNK),
                                            (small_src, small_dst, small_n, MAX_SMALL, CHUNK)):
        def issue(k, carry, src_l=src_l, dst_l=dst_l, length=length, rows=rows):
            local, remote = src_l[tile * length + k], dst_l[tile * length + k]
            if tile_is_src:
                _rows_copy(tile_ref, local, hbm_ref, remote, rows, sem).start()
            else:
                _rows_copy(hbm_ref, remote, tile_ref, local, rows, sem).start()
            return carry

        lax.fori_loop(0, n_l[tile], issue, 0)


def _wait_rows(n, rows, src_ref, dst_ref, sem):
    def body(c, carry):
        pltpu.make_async_copy(src_ref.at[pl.ds(0, rows), :], dst_ref.at[pl.ds(0, rows), :], sem).wait()
        return carry

    lax.fori_loop(0, n, body, 0)


def _wait_copies(lists, tile, src_ref, dst_ref, sem, extra_chunks=0):
    _wait_rows(lists[2][tile], BIG * CHUNK, src_ref, dst_ref, sem)
    n = lists[5][tile] + extra_chunks
    _wait_rows(n // WAIT_GROUP, WAIT_GROUP * CHUNK, src_ref, dst_ref, sem)
    _wait_rows(n % WAIT_GROUP, CHUNK, src_ref, dst_ref, sem)


def _dispatch_kernel(bs_ref, bd_ref, bn_ref, ss_ref, sd_ref, sn_ref, fstart_ref, fn_ref, ftot_ref,
                     lp_ref, x_ref, we_ref, xs_hbm, web_ref, xt, zbuf, sems):
    lists = (bs_ref, bd_ref, bn_ref, ss_ref, sd_ref, sn_ref)
    s = pl.program_id(0)
    ns = pl.num_programs(0)
    cur = (s % 2) * PER_STEP
    prv = PER_STEP - cur

    @pl.when(s >= 2)
    def _():
        for k in range(PER_STEP):
            _wait_copies(lists, (s - 2) * PER_STEP + k, xt.at[cur + k], xs_hbm, sems.at[cur + k])

    r = lax.broadcasted_iota(jnp.int32, (SORT_ROWS, MOE_TILE), 0)
    for k in range(PER_STEP):
        lp = lp_ref[k].astype(jnp.int32)
        hit = jnp.logical_or(lp[0:1, :] == r, lp[1:2, :] == r)
        p = jnp.where(hit, 1.0, 0.0).astype(BF16)
        x = x_ref[k * MOE_TILE:(k + 1) * MOE_TILE, :]
        xt[cur + k] = jnp.dot(p, x, preferred_element_type=F32).astype(BF16)

    for k in range(PER_STEP):
        _issue_copies(lists, s * PER_STEP + k, xt.at[cur + k], True, xs_hbm, sems.at[cur + k])

    _cast_bf16(web_ref.at[0], we_ref.at[0])

    @pl.when(s == ns - 1)
    def _():
        zbuf[...] = jnp.zeros_like(zbuf)
        sem = sems.at[cur]

        def per_range(e, carry):
            def issue(c, carry2):
                _chunk_copy(zbuf, 0, xs_hbm, fstart_ref[e] + c * CHUNK, sem).start()
                return carry2

            return lax.fori_loop(0, fn_ref[e], issue, carry)

        lax.fori_loop(0, N_EXPERTS + 1, per_range, 0)
        for k in range(PER_STEP):
            _wait_copies(lists, (s - 1) * PER_STEP + k, xt.at[prv + k], xs_hbm, sems.at[prv + k])
            _wait_copies(lists, s * PER_STEP + k, xt.at[cur + k], xs_hbm, sems.at[cur + k],
                         extra_chunks=ftot_ref[0] if k == 0 else 0)


def _copy_lists(plan):
    return tuple(plan[k] for k in ("big_src", "big_dst", "big_n", "small_src", "small_dst", "small_n"))


def _dispatch(plan, lpt, x2b, w_expert):
    T, D = x2b.shape
    rows = PER_STEP * MOE_TILE
    assert T // rows >= 2
    assert T // rows == N_EXPERTS
    wes = pl.BlockSpec((1,) + w_expert.shape[1:], lambda t, *_: (t, 0, 0))
    grid_spec = pltpu.PrefetchScalarGridSpec(
        num_scalar_prefetch=9,
        grid=(T // rows,),
        in_specs=[pl.BlockSpec((PER_STEP, SUBLANES, MOE_TILE), lambda t, *_: (t, 0, 0)),
                  pl.BlockSpec((rows, D), lambda t, *_: (t, 0)), wes],
        out_specs=[pl.BlockSpec(memory_space=pl.ANY), wes],
        scratch_shapes=[pltpu.VMEM((2 * PER_STEP, SORT_ROWS, D), BF16), pltpu.VMEM((CHUNK, D), BF16),
                        pltpu.SemaphoreType.DMA((2 * PER_STEP,))],
    )
    return pl.pallas_call(
        _dispatch_kernel,
        grid_spec=grid_spec,
        out_shape=[jax.ShapeDtypeStruct((_gmm_blocks(T) * GMM_ROWS, D), BF16),
                   jax.ShapeDtypeStruct(w_expert.shape, BF16)],
        compiler_params=pltpu.CompilerParams(
            dimension_semantics=("arbitrary",), vmem_limit_bytes=VMEM_LIMIT),
        name="moe_dispatch",
    )(*_copy_lists(plan), plan["fill_start"], plan["fill_n"], plan["fill_tot"], lpt, x2b, w_expert)


def _expert_ffn(xb, wg_b, wu_b, wd_b):
    g = jnp.dot(xb, wg_b[...], preferred_element_type=F32)
    u = jnp.dot(xb, wu_b[...], preferred_element_type=F32)
    h = (g * jax.nn.sigmoid(g)) * u
    return jnp.dot(h.astype(BF16), wd_b[...], preferred_element_type=F32).astype(BF16)


def _gmm_kernel(he_ref, hu_ref, nu_ref, x_ref, wga_ref, wua_ref, wda_ref, wgb_ref, wub_ref, wdb_ref, o_ref):
    b = pl.program_id(0)
    half = REGION_ALIGN
    same = he_ref[2 * b] == he_ref[2 * b + 1]
    used0 = hu_ref[2 * b] != 0
    used1 = hu_ref[2 * b + 1] != 0
    set_a = (wga_ref.at[0], wua_ref.at[0], wda_ref.at[0])
    set_b = (wgb_ref.at[0], wub_ref.at[0], wdb_ref.at[0])

    @pl.when(jnp.logical_and(used1, same))
    def _():
        o_ref[...] = _expert_ffn(x_ref[...], *set_a)

    @pl.when(jnp.logical_and(used0, jnp.logical_not(jnp.logical_and(used1, same))))
    def _():
        o_ref[0:half, :] = _expert_ffn(x_ref[0:half, :], *set_a)

    @pl.when(jnp.logical_and(used1, jnp.logical_not(same)))
    def _():
        o_ref[half:GMM_ROWS, :] = _expert_ffn(x_ref[half:GMM_ROWS, :], *set_b)

    @pl.when(jnp.logical_not(used0))
    def _():
        o_ref[0:half, :] = jnp.zeros((half, o_ref.shape[1]), o_ref.dtype)

    @pl.when(jnp.logical_not(used1))
    def _():
        o_ref[half:GMM_ROWS, :] = jnp.zeros((GMM_ROWS - half, o_ref.shape[1]), o_ref.dtype)


def _gmm(plan, xs, wg, wu, wd):
    R, D = xs.shape
    rows = pl.BlockSpec((GMM_ROWS, D), lambda b, he, hu, nu: (jnp.minimum(b, nu[0] - 1), 0))

    def wspec(shape, h):
        return pl.BlockSpec((1,) + shape, lambda b, he, hu, nu: (he[2 * b + h], 0, 0))

    grid_spec = pltpu.PrefetchScalarGridSpec(
        num_scalar_prefetch=3,
        grid=(R // GMM_ROWS,),
        in_specs=[rows,
                  wspec((D, D_EXPERT), 0), wspec((D, D_EXPERT), 0), wspec((D_EXPERT, D), 0),
                  wspec((D, D_EXPERT), 1), wspec((D, D_EXPERT), 1), wspec((D_EXPERT, D), 1)],
        out_specs=pl.BlockSpec((GMM_ROWS, D), lambda b, he, hu, nu: (b, 0)),
    )
    return pl.pallas_call(
        _gmm_kernel,
        grid_spec=grid_spec,
        out_shape=jax.ShapeDtypeStruct((R, D), BF16),
        compiler_params=pltpu.CompilerParams(
            dimension_semantics=("arbitrary",), vmem_limit_bytes=VMEM_LIMIT),
        name="moe_gmm",
    )(plan["half_expert"], plan["half_used"], plan["n_used"], xs, wg, wu, wd, wg, wu, wd)


def _combine_kernel(bs_ref, bd_ref, bn_ref, ss_ref, sd_ref, sn_ref, cm_ref, x_ref, ys_hbm, ln_ref,
                    o_ref, yt, sems):
    lists = (bs_ref, bd_ref, bn_ref, ss_ref, sd_ref, sn_ref)
    g3_ref, b3_ref = ln_ref.at[4:5, :], ln_ref.at[5:6, :]
    s = pl.program_id(0)
    ns = pl.num_programs(0)
    cur = (s % 2) * PER_STEP
    nxt = PER_STEP - cur

    def fetch(step, first_buf):
        for k in range(PER_STEP):
            _issue_copies(lists, step * PER_STEP + k, yt.at[first_buf + k], False, ys_hbm,
                          sems.at[first_buf + k])

    @pl.when(s == 0)
    def _():
        yt[...] = jnp.zeros_like(yt)
        fetch(0, 0)

    @pl.when(s + 1 < ns)
    def _():
        fetch(s + 1, nxt)

    for k in range(PER_STEP):
        _wait_copies(lists, s * PER_STEP + k, ys_hbm, yt.at[cur + k], sems.at[cur + k])

    col = lax.broadcasted_iota(jnp.int32, (MOE_TILE, SORT_ROWS), 1)
    for k in range(PER_STEP):
        rs = slice(k * MOE_TILE, (k + 1) * MOE_TILE)
        cm = cm_ref[rs, :]
        lp0 = cm[:, 0:1].astype(jnp.int32)
        lp1 = cm[:, 1:2].astype(jnp.int32)
        w = jnp.where(col == lp0, cm[:, 2:3], jnp.where(col == lp1, cm[:, 3:4], 0.0)).astype(BF16)
        y = jnp.dot(w, yt[cur + k], preferred_element_type=F32)
        o_ref[rs, :] = _layer_norm(ALPHA * x_ref[rs, :] + y, g3_ref[...], b3_ref[...])


def _combine(plan, route, x2, ys, ln_vec):
    T, D = x2.shape
    rows = PER_STEP * MOE_TILE
    grid_spec = pltpu.PrefetchScalarGridSpec(
        num_scalar_prefetch=6,
        grid=(T // rows,),
        in_specs=[pl.BlockSpec((rows, LANES), lambda t, *_: (t, 0)),
                  pl.BlockSpec((rows, D), lambda t, *_: (t, 0)),
                  pl.BlockSpec(memory_space=pl.ANY),
                  pl.BlockSpec((SUBLANES, D), lambda t, *_: (0, 0))],
        out_specs=pl.BlockSpec((rows, D), lambda t, *_: (t, 0)),
        scratch_shapes=[pltpu.VMEM((2 * PER_STEP, SORT_ROWS, D), BF16),
                        pltpu.SemaphoreType.DMA((2 * PER_STEP,))],
    )
    return pl.pallas_call(
        _combine_kernel,
        grid_spec=grid_spec,
        out_shape=jax.ShapeDtypeStruct((T, D), F32),
        compiler_params=pltpu.CompilerParams(
            dimension_semantics=("arbitrary",), vmem_limit_bytes=VMEM_LIMIT),
        name="moe_combine",
    )(*_copy_lists(plan), route, x2, ys, ln_vec)


def _moe(x2, x2b, route, lpt, meta, wg, wu_b, wd_b, ln_vec):
    plan = _plan(meta, x2.shape[0])
    xs, wg_b = _dispatch(plan, lpt, x2b, wg)
    ys = _gmm(plan, xs, wg_b, wu_b, wd_b)
    return _combine(plan, route, x2, ys, ln_vec)


def _row(v):
    return v.reshape(1, -1).astype(F32)


def _pack_rows(vectors):
    rows = jnp.stack([v.astype(F32) for v in vectors])
    return jnp.pad(rows, ((0, SUBLANES - len(vectors)), (0, 0)))


def kernel(x, mem, w_in, b_in, w_dw, b_dw, g_conv_norm, b_conv_norm, attn_sinks, w_out, g_ln1, b_ln1,
           w_mq, w_mkv, w_mo, g_ln2, b_ln2, w_group, b_group, w_router, b_router, w_gate, w_up, w_down,
           g_ln3, b_ln3):
    B, S, D = x.shape
    for l in range(DEPTH):
        w_dw_p = jnp.zeros((CONV_HALO, CONV_CH), F32).at[:CONV_WIDTH].set(w_dw[l])
        ln_vec = _pack_rows([g_ln1[l], b_ln1[l], g_ln2[l], b_ln2[l], g_ln3[l], b_ln3[l]])
        conv_vec = _pack_rows([b_dw[l], g_conv_norm[l], b_conv_norm[l]])
        x1, wd_b = _mixer(x, attn_sinks[l].astype(F32), w_in[l], _row(b_in[l]), w_dw_p, conv_vec,
                          w_out[l], ln_vec, w_down[l])

        kvm = _memkv(mem.reshape(B * MEM_LEN, D), w_mkv[l]).reshape(B, MEM_LEN, 2 * D)

        wr = jnp.concatenate(
            [w_group[l], jnp.transpose(w_router[l], (1, 0, 2)).reshape(D, N_EXPERTS)], axis=1)
        wr = jnp.pad(wr, ((0, 0), (0, LANES - wr.shape[1])))
        br = jnp.pad(jnp.concatenate([b_group[l], b_router[l].reshape(-1)]), (0, LANES - N_GROUPS - N_EXPERTS))
        x2, x2b, route, lpt, meta, wu_b = _memattn(x1, w_mq[l], kvm, w_mo[l], ln_vec,
                                                   wr.astype(F32), _row(br), w_up[l])

        T = B * S
        y = _moe(x2.reshape(T, D), x2b.reshape(T, D), route.reshape(T, LANES), lpt, meta,
                 w_gate[l], wu_b, wd_b, ln_vec)
        x = y.reshape(B, S, D)
    return x
```

```python
import functools

import jax
import jax.numpy as jnp
from jax import lax
from jax.experimental import pallas as pl
from jax.experimental.pallas import tpu as pltpu

D_MODEL = 1024
MEM_LEN = 256
CONV_CH = 512
CONV_WIDTH = 31
N_HEADS = 8
N_KV_HEADS = 2
HEAD_DIM = 64
GQ = N_HEADS // N_KV_HEADS
ATTN_W = N_HEADS * HEAD_DIM
KV_W = N_KV_HEADS * HEAD_DIM
WINDOW = 128
D_MIX = CONV_CH + ATTN_W
D_IN = 2 * CONV_CH + ATTN_W + 2 * KV_W
MEM_HEADS = 4
MEM_HEAD_DIM = D_MODEL // MEM_HEADS
N_GROUPS = 4
EXPERTS_PER_GROUP = 4
N_EXPERTS = N_GROUPS * EXPERTS_PER_GROUP
D_EXPERT = D_MODEL // 2
DEPTH = 1
ALPHA = (2.0 * DEPTH) ** 0.25
LN_EPS = 1e-5

LANES = 128
SUBLANES = 8
CONV_ROWS = 128
LN_ROWS = 64
MEM_TILE = 1024
MASK_VALUE = -1e30
CONV_HALO = 32
SEQ_TILE = 512
MOE_TILE = 512
CHUNK = 16
REGION_ALIGN = 512
GMM_ROWS = 2 * REGION_ALIGN
TOP_K = 2
SORT_ROWS = -(-(MOE_TILE * TOP_K + N_EXPERTS * (CHUNK - 1)) // 256) * 256
BIG = 4
MAX_BIG = SORT_ROWS // (BIG * CHUNK)
MAX_SMALL = N_EXPERTS * (BIG - 1)
PER_STEP = 2
WAIT_GROUP = 8
ROUTE_OFF = N_GROUPS
VMEM_LIMIT = 56 * 1024 * 1024

BF16 = jnp.bfloat16
F32 = jnp.float32


def _layer_norm(x, g, b):
    mu = jnp.mean(x, axis=-1, keepdims=True)
    xc = x - mu
    var = jnp.mean(xc * xc, axis=-1, keepdims=True)
    return xc * lax.rsqrt(var + LN_EPS) * g + b


def _cast_bf16(dst_ref, src_ref):
    rows = 256
    for r0 in range(0, src_ref.shape[0], rows):
        dst_ref[r0:r0 + rows, :] = src_ref[r0:r0 + rows, :].astype(BF16)


def _dot_nt(a, b):
    return lax.dot_general(a, b, (((1,), (1,)), ((), ())), preferred_element_type=F32)


def _mixer_kernel(sinks_ref, x_ref, xp_ref, w_in_ref, b_in_ref, w_dw_ref, b_dw_ref, g_cn_ref, b_cn_ref,
                  w_out_ref, g1_ref, b1_ref, we_ref, o_ref, web_ref, w_in_b, w_out_b, hbuf, hshift, cbuf, qbuf,
                  kbuf, vbuf, ymix, mixbuf, *, tiles_per_seq):
    t = pl.program_id(0)
    i = jnp.minimum(t, pl.num_programs(0) - 2) % tiles_per_seq
    ts = SEQ_TILE
    web_ref[...] = we_ref[...].astype(BF16)

    @pl.when(t == 0)
    def _():
        _cast_bf16(w_in_b, w_in_ref)
        _cast_bf16(w_out_b, w_out_ref)
        mixbuf[...] = jnp.zeros_like(mixbuf)

    @pl.when(i == 0)
    def _():
        hbuf[0:CONV_HALO, :] = jnp.zeros((CONV_HALO, CONV_CH), F32)
        kbuf[:, 0:WINDOW, :] = jnp.zeros((2 * N_KV_HEADS, WINDOW, KV_W), BF16)
        vbuf[:, 0:WINDOW, :] = jnp.zeros((2, WINDOW, KV_W), BF16)

    x = x_ref[0]
    u = jnp.dot(x.astype(BF16), w_in_b[...], preferred_element_type=F32) + b_in_ref[...]
    o_ref[0] = _layer_norm(ALPHA * xp_ref[0] + mixbuf[...], g1_ref[...], b1_ref[...])
    a = u[:, 0:CONV_CH]
    gate = u[:, CONV_CH:2 * CONV_CH]
    hbuf[CONV_HALO:CONV_HALO + ts, :] = a * jax.nn.sigmoid(gate)
    qbuf[...] = (u[:, 2 * CONV_CH:2 * CONV_CH + ATTN_W] * (HEAD_DIM ** -0.5)).astype(BF16)
    kf = u[:, 2 * CONV_CH + ATTN_W:2 * CONV_CH + ATTN_W + KV_W]
    vf = u[:, 2 * CONV_CH + ATTN_W + KV_W:D_IN]
    kr = pltpu.roll(kf, HEAD_DIM, axis=1)
    vr = pltpu.roll(vf, HEAD_DIM, axis=1)
    lo = lax.broadcasted_iota(jnp.int32, (ts, KV_W), 1) < HEAD_DIM
    rows = slice(WINDOW, WINDOW + ts)
    kbuf[0, rows, :] = jnp.where(lo, kf, 0.0).astype(BF16)
    kbuf[1, rows, :] = jnp.where(lo, 0.0, kr).astype(BF16)
    kbuf[2, rows, :] = jnp.where(lo, kr, 0.0).astype(BF16)
    kbuf[3, rows, :] = jnp.where(lo, 0.0, kf).astype(BF16)
    vbuf[0, rows, :] = vf.astype(BF16)
    vbuf[1, rows, :] = vr.astype(BF16)

    base = CONV_HALO - (CONV_WIDTH - 1)
    n_shift = ts + CONV_HALO - SUBLANES
    for b in range(1, SUBLANES):
        hshift[b - 1, 0:n_shift, :] = hbuf[b:b + n_shift, :]
    rc = CONV_ROWS

    def conv_chunk(c):
        r0 = c * rc
        for l in range(CONV_CH // LANES):
            ls = slice(l * LANES, (l + 1) * LANES)
            acc = jnp.zeros((rc, LANES), F32)
            for j in range(CONV_WIDTH):
                a8, b = divmod(j + base, SUBLANES)
                rs = slice(r0 + SUBLANES * a8, r0 + SUBLANES * a8 + rc)
                tap = hbuf[rs, ls] if b == 0 else hshift[b - 1, rs, ls]
                acc = acc + tap * w_dw_ref[j:j + 1, ls]
            cbuf[r0:r0 + rc, ls] = acc
        for r1 in range(r0, r0 + rc, LN_ROWS):
            rs = slice(r1, r1 + LN_ROWS)
            y = _layer_norm(cbuf[rs, :] + b_dw_ref[...], g_cn_ref[...], b_cn_ref[...])
            y = y * jax.nn.sigmoid(y)
            ymix[rs, 0:CONV_CH] = y.astype(BF16)

    qi = lax.broadcasted_iota(jnp.int32, (2 * WINDOW, WINDOW), 0) % WINDOW
    kj = lax.broadcasted_iota(jnp.int32, (2 * WINDOW, WINDOW), 1)
    own = kj <= qi
    top = lax.broadcasted_iota(jnp.int32, (2 * WINDOW, 1), 0) < WINDOW
    lo_out = lax.broadcasted_iota(jnp.int32, (WINDOW, 2 * HEAD_DIM), 1) < HEAD_DIM

    def attn_block(jb):
        r0 = jb * WINDOW
        prev_ok = jnp.logical_not(own) & (i != 0) if jb == 0 else jnp.logical_not(own)
        for kvh in range(N_KV_HEADS):
            h0 = kvh * GQ
            c0 = h0 * HEAD_DIM
            qs = jnp.concatenate([qbuf[r0:r0 + WINDOW, c0:c0 + 2 * HEAD_DIM],
                                  qbuf[r0:r0 + WINDOW, c0 + 2 * HEAD_DIM:c0 + 4 * HEAD_DIM]], axis=0)
            pv = []
            for par in range(2):
                kk = kbuf[2 * kvh + par, r0:r0 + 2 * WINDOW, :]
                vv = vbuf[(kvh + par) % 2, r0:r0 + 2 * WINDOW, :]
                s2 = _dot_nt(qs, kk)
                s = jnp.where(own, s2[:, WINDOW:], jnp.where(prev_ok, s2[:, :WINDOW], MASK_VALUE))
                sink = jnp.where(top, sinks_ref[h0 + par], sinks_ref[h0 + 2 + par])
                m = jnp.maximum(jnp.max(s, axis=-1, keepdims=True), sink)
                p = jnp.exp(s - m)
                denom = jnp.sum(p, axis=-1, keepdims=True) + jnp.exp(sink - m)
                zero = jnp.zeros((), BF16)
                pb = p.astype(BF16)
                p2 = jnp.concatenate([jnp.where(own, zero, pb), jnp.where(own, pb, zero)], axis=1)
                pv.append(jnp.dot(p2, vv, preferred_element_type=F32) / denom)
            for pair in range(2):
                rs = slice(pair * WINDOW, (pair + 1) * WINDOW)
                o = jnp.where(lo_out, pv[0][rs], pv[1][rs])
                cs = CONV_CH + c0 + pair * 2 * HEAD_DIM
                ymix[r0:r0 + WINDOW, cs:cs + 2 * HEAD_DIM] = o.astype(BF16)

    assert ts // rc == ts // WINDOW
    for c in range(ts // rc):
        attn_block(c)
        conv_chunk(c)

    mixbuf[...] = jnp.dot(ymix[...], w_out_b[...], preferred_element_type=F32)

    hbuf[0:CONV_HALO, :] = hbuf[ts:ts + CONV_HALO, :]
    kbuf[:, 0:WINDOW, :] = kbuf[:, ts:ts + WINDOW, :]
    vbuf[:, 0:WINDOW, :] = vbuf[:, ts:ts + WINDOW, :]


def _const_spec(shape):
    nd = len(shape)
    return pl.BlockSpec(shape, lambda *_: (0,) * nd)


def _resident_spec(shape):
    nd = len(shape)
    return pl.BlockSpec(shape, lambda *_: (0,) * nd, pipeline_mode=pl.Buffered(1))


def _mixer(x, sinks, w_in, b_in, w_dw, b_dw, g_cn, b_cn, w_out, g1, b1, w_expert):
    B, S, D = x.shape
    ts = SEQ_TILE
    nt = S // ts
    n_tiles = B * nt
    cur = pl.BlockSpec((1, ts, D), lambda t: (jnp.minimum(t, n_tiles - 1) // nt, jnp.minimum(t, n_tiles - 1) % nt, 0))
    prev = pl.BlockSpec((1, ts, D), lambda t: (jnp.maximum(t - 1, 0) // nt, jnp.maximum(t - 1, 0) % nt, 0))
    n_e, we_rows, we_cols = w_expert.shape
    per = n_tiles // n_e
    assert per * n_e == n_tiles and we_rows % per == 0

    def we_index(t):
        k = jnp.minimum(t, n_tiles - 1)
        return (k // per, k % per, 0)

    wes = pl.BlockSpec((1, we_rows // per, we_cols), we_index)
    return pl.pallas_call(
        functools.partial(_mixer_kernel, tiles_per_seq=nt),
        grid=(n_tiles + 1,),
        in_specs=[
            pl.BlockSpec(memory_space=pltpu.SMEM),
            cur, prev,
            _resident_spec((D, D_IN)), _const_spec((1, D_IN)),
            _const_spec((CONV_HALO, CONV_CH)), _const_spec((1, CONV_CH)),
            _const_spec((1, CONV_CH)), _const_spec((1, CONV_CH)),
            _resident_spec((D_MIX, D)), _const_spec((1, D)), _const_spec((1, D)),
            wes,
        ],
        out_specs=[prev, wes],
        out_shape=[jax.ShapeDtypeStruct((B, S, D), F32), jax.ShapeDtypeStruct(w_expert.shape, BF16)],
        scratch_shapes=[
            pltpu.VMEM((D, D_IN), BF16),
            pltpu.VMEM((D_MIX, D), BF16),
            pltpu.VMEM((CONV_HALO + ts, CONV_CH), F32),
            pltpu.VMEM((SUBLANES - 1, CONV_HALO + ts, CONV_CH), F32),
            pltpu.VMEM((ts, CONV_CH), F32),
            pltpu.VMEM((ts, ATTN_W), BF16),
            pltpu.VMEM((2 * N_KV_HEADS, WINDOW + ts, KV_W), BF16),
            pltpu.VMEM((2, WINDOW + ts, KV_W), BF16),
            pltpu.VMEM((ts, D_MIX), BF16),
            pltpu.VMEM((ts, D), F32),
        ],
        compiler_params=pltpu.CompilerParams(
            dimension_semantics=("arbitrary",), vmem_limit_bytes=VMEM_LIMIT),
        name="mixer",
    )(sinks, x, x, w_in, b_in, w_dw, b_dw, g_cn, b_cn, w_out, g1, b1, w_expert)


def _memkv_kernel(mem_ref, w_ref, o_ref):
    o_ref[...] = jnp.dot(mem_ref[...].astype(BF16), w_ref[...].astype(BF16),
                         preferred_element_type=F32).astype(BF16)


def _memkv(mem2d, w_mkv):
    M, D = mem2d.shape
    N = w_mkv.shape[1]
    tn = 512
    return pl.pallas_call(
        _memkv_kernel,
        grid=(N // tn,),
        in_specs=[pl.BlockSpec((M, D), lambda j: (0, 0)), pl.BlockSpec((D, tn), lambda j: (0, j))],
        out_specs=pl.BlockSpec((M, tn), lambda j: (0, j)),
        out_shape=jax.ShapeDtypeStruct((M, N), BF16),
        compiler_params=pltpu.CompilerParams(dimension_semantics=("arbitrary",)),
        name="memkv",
    )(mem2d, w_mkv)


def _first_max(rows):
    best = rows[0]
    for r in rows[1:]:
        best = jnp.maximum(best, r)
    idx = jnp.full(best.shape, len(rows) - 1, jnp.int32)
    for k in range(len(rows) - 2, -1, -1):
        idx = jnp.where(rows[k] == best, k, idx)
    return best, idx


def _route_plan(logits_t):
    tile = logits_t.shape[1]
    row = lambda k: logits_t[k:k + 1, :]
    gmax, g_idx = _first_max([row(g) for g in range(N_GROUPS)])
    gsum = jnp.exp(row(0) - gmax)
    for g in range(1, N_GROUPS):
        gsum = gsum + jnp.exp(row(g) - gmax)
    g_p = 1.0 / gsum
    rl = []
    for e in range(EXPERTS_PER_GROUP):
        v = row(ROUTE_OFF + (N_GROUPS - 1) * EXPERTS_PER_GROUP + e)
        for g in range(N_GROUPS - 2, -1, -1):
            v = jnp.where(g_idx == g, row(ROUTE_OFF + g * EXPERTS_PER_GROUP + e), v)
        rl.append(v)
    m1, i1 = _first_max(rl)
    m2, i2 = _first_max([jnp.where(i1 == e, MASK_VALUE, rl[e]) for e in range(EXPERTS_PER_GROUP)])
    ex = jnp.exp(m2 - m1)
    w1 = 1.0 / (1.0 + ex)
    w2 = ex * w1
    e1 = g_idx * EXPERTS_PER_GROUP + i1
    e2 = g_idx * EXPERTS_PER_GROUP + i2

    eid = lax.broadcasted_iota(jnp.int32, (N_EXPERTS, tile), 0)
    hit1 = eid == e1
    hit2 = eid == e2
    oh = jnp.where(jnp.logical_or(hit1, hit2), 1.0, 0.0)
    r = lax.broadcasted_iota(jnp.int32, (tile, tile), 0)
    c = lax.broadcasted_iota(jnp.int32, (tile, tile), 1)
    tri = jnp.where(r <= c, 1.0, 0.0).astype(BF16)
    csum = jnp.dot(oh.astype(BF16), tri, preferred_element_type=F32)
    counts = jnp.broadcast_to(csum[:, tile - 1:tile], (N_EXPERTS, tile)).astype(jnp.int32)
    nch = jnp.right_shift(counts + (CHUNK - 1), CHUNK.bit_length() - 1)
    er = lax.broadcasted_iota(jnp.int32, (N_EXPERTS, N_EXPERTS), 0)
    ec = lax.broadcasted_iota(jnp.int32, (N_EXPERTS, N_EXPERTS), 1)
    lower = jnp.where(ec < er, 1.0, 0.0).astype(BF16)
    off = jnp.dot(lower, nch.astype(F32).astype(BF16), preferred_element_type=F32) * CHUNK
    pos = off + csum - oh
    lp1 = jnp.sum(jnp.where(hit1, pos, 0.0), axis=0, keepdims=True)
    lp2 = jnp.sum(jnp.where(hit2, pos, 0.0), axis=0, keepdims=True)
    zero = jnp.zeros_like(lp1)
    route_t = jnp.concatenate([lp1, lp2, g_p * w1, g_p * w2, zero, zero, zero, zero], axis=0)
    meta = jnp.concatenate([nch[:, 0:LANES], off[:, 0:LANES].astype(jnp.int32)], axis=0)
    return route_t, meta


def _memattn_kernel(x_ref, wq_ref, k_ref, v_ref, wo_ref, g2_ref, b2_ref, wr2_ref, wrh_ref, br_ref, we_ref,
                    o_ref, ob_ref, route_ref, routet_ref, meta_ref, web_ref, wq_b, wo_b):
    web_ref[...] = we_ref[...].astype(BF16)

    @pl.when(jnp.logical_and(pl.program_id(0) == 0, pl.program_id(1) == 0))
    def _():
        _cast_bf16(wq_b, wq_ref)
        _cast_bf16(wo_b, wo_ref)

    def rows_logits(rs):
        x = x_ref[0, rs, :]
        q = jnp.dot(x.astype(BF16), wq_b[...], preferred_element_type=F32)
        q = (q * (MEM_HEAD_DIM ** -0.5)).astype(BF16)
        outs = []
        for h in range(MEM_HEADS):
            sl = slice(h * MEM_HEAD_DIM, (h + 1) * MEM_HEAD_DIM)
            s = _dot_nt(q[:, sl], k_ref[0, :, sl])
            m = jnp.max(s, axis=-1, keepdims=True)
            p = jnp.exp(s - m)
            denom = jnp.sum(p, axis=-1, keepdims=True)
            o = jnp.dot(p.astype(BF16), v_ref[0, :, sl], preferred_element_type=F32)
            outs.append((o / denom).astype(BF16))
        o = jnp.dot(jnp.concatenate(outs, axis=-1), wo_b[...], preferred_element_type=F32)
        x2 = _layer_norm(ALPHA * x + o, g2_ref[...], b2_ref[...])
        o_ref[0, rs, :] = x2
        x2h = x2.astype(BF16)
        ob_ref[0, rs, :] = x2h
        x2l = (x2 - x2h.astype(F32)).astype(BF16)
        hh = jnp.dot(x2h, wr2_ref[...], preferred_element_type=F32)
        return (hh[:, 0:LANES] + hh[:, LANES:2 * LANES]
                + jnp.dot(x2l, wrh_ref[...], preferred_element_type=F32) + br_ref[...])

    ts = x_ref.shape[1]
    groups = [slice(r0, r0 + MOE_TILE) for r0 in range(0, ts, MOE_TILE)]
    logits = rows_logits(slice(0, ts))
    pad = jnp.zeros((LANES - SUBLANES, MOE_TILE), F32)
    for k, rs in enumerate(groups):
        route_t, meta = _route_plan(jnp.transpose(logits[rs, :]))
        routet_ref[k] = route_t
        route_ref[0, rs, :] = jnp.transpose(jnp.concatenate([route_t, pad], axis=0))
        meta_ref[k] = meta


def _memattn(x1, wq, kvm, wo, g2, b2, wr, br, w_expert):
    B, S, D = x1.shape
    ts = MEM_TILE
    assert B * (S // ts) == w_expert.shape[0]
    wes = pl.BlockSpec((1,) + w_expert.shape[1:], lambda b, i: (b * (S // ts) + i, 0, 0))
    per = ts // MOE_TILE
    nt = S // ts
    c = wr * (2.0 ** 16 + 1.0)
    w_high = c - (c - wr)
    wrh = w_high.astype(BF16)
    wr2 = jnp.concatenate([wrh, (wr - w_high).astype(BF16)], axis=1)
    tile = pl.BlockSpec((1, ts, D), lambda b, i: (b, i, 0))
    kspec = pl.BlockSpec((1, MEM_LEN, D), lambda b, i: (b, 0, 0))
    vspec = pl.BlockSpec((1, MEM_LEN, D), lambda b, i: (b, 0, 1))
    return pl.pallas_call(
        _memattn_kernel,
        grid=(B, nt),
        in_specs=[tile, _resident_spec((D, D)), kspec, vspec, _resident_spec((D, D)),
                  _const_spec((1, D)), _const_spec((1, D)),
                  _const_spec((D, 2 * LANES)), _const_spec((D, LANES)), _const_spec((1, LANES)), wes],
        out_specs=[tile, tile, pl.BlockSpec((1, ts, LANES), lambda b, i: (b, i, 0)),
                   pl.BlockSpec((per, SUBLANES, MOE_TILE), lambda b, i: (b * nt + i, 0, 0)),
                   pl.BlockSpec((per, 2 * N_EXPERTS, LANES), lambda b, i: (b * nt + i, 0, 0)), wes],
        out_shape=[jax.ShapeDtypeStruct((B, S, D), F32),
                   jax.ShapeDtypeStruct((B, S, D), BF16),
                   jax.ShapeDtypeStruct((B, S, LANES), F32),
                   jax.ShapeDtypeStruct((B * nt * per, SUBLANES, MOE_TILE), F32),
                   jax.ShapeDtypeStruct((B * nt * per, 2 * N_EXPERTS, LANES), jnp.int32),
                   jax.ShapeDtypeStruct(w_expert.shape, BF16)],
        scratch_shapes=[pltpu.VMEM((D, D), BF16), pltpu.VMEM((D, D), BF16)],
        compiler_params=pltpu.CompilerParams(
            dimension_semantics=("arbitrary", "arbitrary"), vmem_limit_bytes=VMEM_LIMIT),
        name="memattn",
    )(x1, wq, kvm, kvm, wo, g2, b2, wr2, wrh, br, w_expert)


def _gmm_blocks(n_tokens):
    rows = (n_tokens * TOP_K + (n_tokens // MOE_TILE) * N_EXPERTS * (CHUNK - 1)
            + N_EXPERTS * (REGION_ALIGN - CHUNK))
    return -(-rows // GMM_ROWS)


def _plan(meta, T):
    nch = meta[:, :N_EXPERTS, 0]
    n16 = nch * CHUNK
    n_e = jnp.sum(n16, axis=0)
    reg = (n_e + REGION_ALIGN - 1) // REGION_ALIGN * REGION_ALIGN
    gend = jnp.cumsum(reg)
    gbase = gend - reg
    dst = gbase[None, :] + jnp.cumsum(n16, axis=0) - n16
    half_row = jnp.arange(2 * _gmm_blocks(T), dtype=jnp.int32) * REGION_ALIGN
    half_expert = jnp.minimum(jnp.sum(half_row[:, None] >= gend[None, :], axis=1), N_EXPERTS - 1)
    half_used = half_row < gend[-1]
    n_used = ((gend[-1] + GMM_ROWS - 1) // GMM_ROWS).astype(jnp.int32).reshape(1)
    src = (jnp.cumsum(nch, axis=1) - nch) * CHUNK
    n_big = nch // BIG

    def copy_list(count, src0, dst0, rows, length):
        cum = jnp.cumsum(count, axis=1)
        first = (cum - count)[:, None, :]
        k = jnp.arange(length, dtype=jnp.int32)[None, :, None]
        mine = (k >= first) & (k < cum[:, None, :])
        step = (k - first) * rows
        pick = lambda base: jnp.sum(jnp.where(mine, base[:, None, :] + step, 0), axis=2)
        return pick(src0), pick(dst0), cum[:, -1]

    big_src, big_dst, big_n = copy_list(n_big, src, dst, BIG * CHUNK, MAX_BIG)
    rest = n_big * (BIG * CHUNK)
    small_src, small_dst, small_n = copy_list(nch - n_big * BIG, src + rest, dst + rest, CHUNK, MAX_SMALL)
    i32 = lambda a: a.astype(jnp.int32)
    fill_start = jnp.concatenate([gbase + n_e, gend[-1:]])
    fill_n = jnp.concatenate([reg - n_e, _gmm_blocks(T) * GMM_ROWS - gend[-1:]]) // CHUNK
    return dict(big_src=i32(big_src).reshape(-1), big_dst=i32(big_dst).reshape(-1), big_n=i32(big_n),
                small_src=i32(small_src).reshape(-1), small_dst=i32(small_dst).reshape(-1), small_n=i32(small_n),
                fill_start=i32(fill_start), fill_n=i32(fill_n),
                fill_tot=i32(jnp.sum(fill_n)).reshape(1), half_expert=i32(half_expert),
                half_used=i32(half_used), n_used=n_used)


def _rows_copy(src_ref, src_row, dst_ref, dst_row, rows, sem):
    return pltpu.make_async_copy(
        src_ref.at[pl.ds(pl.multiple_of(src_row, CHUNK), rows), :],
        dst_ref.at[pl.ds(pl.multiple_of(dst_row, CHUNK), rows), :], sem)


def _chunk_copy(src_ref, src_row, dst_ref, dst_row, sem):
    return _rows_copy(src_ref, src_row, dst_ref, dst_row, CHUNK, sem)


def _issue_copies(lists, tile, tile_ref, tile_is_src, hbm_ref, sem):
    big_src, big_dst, big_n, small_src, small_dst, small_n = lists
    for src_l, dst_l, n_l, length, rows in ((big_src, big_dst, big_n, MAX_BIG, BIG * CHUNK),
                                            (small_src, small_dst, small_n, MAX_SMALL, CHUNK)):
        def issue(k, carry, src_l=src_l, dst_l=dst_l, length=length, rows=rows):
            local, remote = src_l[tile * length + k], dst_l[tile * length + k]
            if tile_is_src:
                _rows_copy(tile_ref, local, hbm_ref, remote, rows, sem).start()
            else:
                _rows_copy(hbm_ref, remote, tile_ref, local, rows, sem).start()
            return carry

        lax.fori_loop(0, n_l[tile], issue, 0)


def _wait_rows(n, rows, src_ref, dst_ref, sem):
    def body(c, carry):
        pltpu.make_async_copy(src_ref.at[pl.ds(0, rows), :], dst_ref.at[pl.ds(0, rows), :], sem).wait()
        return carry

    lax.fori_loop(0, n, body, 0)


def _wait_copies(lists, tile, src_ref, dst_ref, sem, extra_chunks=0):
    _wait_rows(lists[2][tile], BIG * CHUNK, src_ref, dst_ref, sem)
    n = lists[5][tile] + extra_chunks
    _wait_rows(n // WAIT_GROUP, WAIT_GROUP * CHUNK, src_ref, dst_ref, sem)
    _wait_rows(n % WAIT_GROUP, CHUNK, src_ref, dst_ref, sem)


def _dispatch_kernel(bs_ref, bd_ref, bn_ref, ss_ref, sd_ref, sn_ref, fstart_ref, fn_ref, ftot_ref,
                     lp_ref, x_ref, we_ref, xs_hbm, web_ref, xt, zbuf, sems):
    lists = (bs_ref, bd_ref, bn_ref, ss_ref, sd_ref, sn_ref)
    s = pl.program_id(0)
    ns = pl.num_programs(0)
    cur = (s % 2) * PER_STEP
    prv = PER_STEP - cur

    @pl.when(s >= 2)
    def _():
        for k in range(PER_STEP):
            _wait_copies(lists, (s - 2) * PER_STEP + k, xt.at[cur + k], xs_hbm, sems.at[cur + k])

    r = lax.broadcasted_iota(jnp.int32, (SORT_ROWS, MOE_TILE), 0)
    for k in range(PER_STEP):
        lp = lp_ref[k].astype(jnp.int32)
        hit = jnp.logical_or(lp[0:1, :] == r, lp[1:2, :] == r)
        p = jnp.where(hit, 1.0, 0.0).astype(BF16)
        x = x_ref[k * MOE_TILE:(k + 1) * MOE_TILE, :]
        xt[cur + k] = jnp.dot(p, x, preferred_element_type=F32).astype(BF16)

    for k in range(PER_STEP):
        _issue_copies(lists, s * PER_STEP + k, xt.at[cur + k], True, xs_hbm, sems.at[cur + k])

    _cast_bf16(web_ref.at[0], we_ref.at[0])

    @pl.when(s == ns - 1)
    def _():
        zbuf[...] = jnp.zeros_like(zbuf)
        sem = sems.at[cur]

        def per_range(e, carry):
            def issue(c, carry2):
                _chunk_copy(zbuf, 0, xs_hbm, fstart_ref[e] + c * CHUNK, sem).start()
                return carry2

            return lax.fori_loop(0, fn_ref[e], issue, carry)

        lax.fori_loop(0, N_EXPERTS + 1, per_range, 0)
        for k in range(PER_STEP):
            _wait_copies(lists, (s - 1) * PER_STEP + k, xt.at[prv + k], xs_hbm, sems.at[prv + k])
            _wait_copies(lists, s * PER_STEP + k, xt.at[cur + k], xs_hbm, sems.at[cur + k],
                         extra_chunks=ftot_ref[0] if k == 0 else 0)


def _copy_lists(plan):
    return tuple(plan[k] for k in ("big_src", "big_dst", "big_n", "small_src", "small_dst", "small_n"))


def _dispatch(plan, lpt, x2b, w_expert):
    T, D = x2b.shape
    rows = PER_STEP * MOE_TILE
    assert T // rows >= 2
    assert T // rows == N_EXPERTS
    wes = pl.BlockSpec((1,) + w_expert.shape[1:], lambda t, *_: (t, 0, 0))
    grid_spec = pltpu.PrefetchScalarGridSpec(
        num_scalar_prefetch=9,
        grid=(T // rows,),
        in_specs=[pl.BlockSpec((PER_STEP, SUBLANES, MOE_TILE), lambda t, *_: (t, 0, 0)),
                  pl.BlockSpec((rows, D), lambda t, *_: (t, 0)), wes],
        out_specs=[pl.BlockSpec(memory_space=pl.ANY), wes],
        scratch_shapes=[pltpu.VMEM((2 * PER_STEP, SORT_ROWS, D), BF16), pltpu.VMEM((CHUNK, D), BF16),
                        pltpu.SemaphoreType.DMA((2 * PER_STEP,))],
    )
    return pl.pallas_call(
        _dispatch_kernel,
        grid_spec=grid_spec,
        out_shape=[jax.ShapeDtypeStruct((_gmm_blocks(T) * GMM_ROWS, D), BF16),
                   jax.ShapeDtypeStruct(w_expert.shape, BF16)],
        compiler_params=pltpu.CompilerParams(
            dimension_semantics=("arbitrary",), vmem_limit_bytes=VMEM_LIMIT),
        name="moe_dispatch",
    )(*_copy_lists(plan), plan["fill_start"], plan["fill_n"], plan["fill_tot"], lpt, x2b, w_expert)


def _expert_ffn(xb, wg_b, wu_b, wd_b):
    g = jnp.dot(xb, wg_b[...], preferred_element_type=F32)
    u = jnp.dot(xb, wu_b[...], preferred_element_type=F32)
    h = (g * jax.nn.sigmoid(g)) * u
    return jnp.dot(h.astype(BF16), wd_b[...], preferred_element_type=F32).astype(BF16)


def _gmm_kernel(he_ref, hu_ref, nu_ref, x_ref, wga_ref, wua_ref, wda_ref, wgb_ref, wub_ref, wdb_ref, o_ref):
    b = pl.program_id(0)
    half = REGION_ALIGN
    same = he_ref[2 * b] == he_ref[2 * b + 1]
    used0 = hu_ref[2 * b] != 0
    used1 = hu_ref[2 * b + 1] != 0
    set_a = (wga_ref.at[0], wua_ref.at[0], wda_ref.at[0])
    set_b = (wgb_ref.at[0], wub_ref.at[0], wdb_ref.at[0])

    @pl.when(jnp.logical_and(used1, same))
    def _():
        o_ref[...] = _expert_ffn(x_ref[...], *set_a)

    @pl.when(jnp.logical_and(used0, jnp.logical_not(jnp.logical_and(used1, same))))
    def _():
        o_ref[0:half, :] = _expert_ffn(x_ref[0:half, :], *set_a)

    @pl.when(jnp.logical_and(used1, jnp.logical_not(same)))
    def _():
        o_ref[half:GMM_ROWS, :] = _expert_ffn(x_ref[half:GMM_ROWS, :], *set_b)

    @pl.when(jnp.logical_not(used0))
    def _():
        o_ref[0:half, :] = jnp.zeros((half, o_ref.shape[1]), o_ref.dtype)

    @pl.when(jnp.logical_not(used1))
    def _():
        o_ref[half:GMM_ROWS, :] = jnp.zeros((GMM_ROWS - half, o_ref.shape[1]), o_ref.dtype)


def _gmm(plan, xs, wg, wu, wd):
    R, D = xs.shape
    rows = pl.BlockSpec((GMM_ROWS, D), lambda b, he, hu, nu: (jnp.minimum(b, nu[0] - 1), 0))

    def wspec(shape, h):
        return pl.BlockSpec((1,) + shape, lambda b, he, hu, nu: (he[2 * b + h], 0, 0))

    grid_spec = pltpu.PrefetchScalarGridSpec(
        num_scalar_prefetch=3,
        grid=(R // GMM_ROWS,),
        in_specs=[rows,
                  wspec((D, D_EXPERT), 0), wspec((D, D_EXPERT), 0), wspec((D_EXPERT, D), 0),
                  wspec((D, D_EXPERT), 1), wspec((D, D_EXPERT), 1), wspec((D_EXPERT, D), 1)],
        out_specs=pl.BlockSpec((GMM_ROWS, D), lambda b, he, hu, nu: (b, 0)),
    )
    return pl.pallas_call(
        _gmm_kernel,
        grid_spec=grid_spec,
        out_shape=jax.ShapeDtypeStruct((R, D), BF16),
        compiler_params=pltpu.CompilerParams(
            dimension_semantics=("arbitrary",), vmem_limit_bytes=VMEM_LIMIT),
        name="moe_gmm",
    )(plan["half_expert"], plan["half_used"], plan["n_used"], xs, wg, wu, wd, wg, wu, wd)


def _combine_kernel(bs_ref, bd_ref, bn_ref, ss_ref, sd_ref, sn_ref, cm_ref, x_ref, ys_hbm, g3_ref, b3_ref,
                    o_ref, yt, sems):
    lists = (bs_ref, bd_ref, bn_ref, ss_ref, sd_ref, sn_ref)
    s = pl.program_id(0)
    ns = pl.num_programs(0)
    cur = (s % 2) * PER_STEP
    nxt = PER_STEP - cur

    def fetch(step, first_buf):
        for k in range(PER_STEP):
            _issue_copies(lists, step * PER_STEP + k, yt.at[first_buf + k], False, ys_hbm,
                          sems.at[first_buf + k])

    @pl.when(s == 0)
    def _():
        yt[...] = jnp.zeros_like(yt)
        fetch(0, 0)

    @pl.when(s + 1 < ns)
    def _():
        fetch(s + 1, nxt)

    for k in range(PER_STEP):
        _wait_copies(lists, s * PER_STEP + k, ys_hbm, yt.at[cur + k], sems.at[cur + k])

    col = lax.broadcasted_iota(jnp.int32, (MOE_TILE, SORT_ROWS), 1)
    for k in range(PER_STEP):
        rs = slice(k * MOE_TILE, (k + 1) * MOE_TILE)
        cm = cm_ref[rs, :]
        lp0 = cm[:, 0:1].astype(jnp.int32)
        lp1 = cm[:, 1:2].astype(jnp.int32)
        w = jnp.where(col == lp0, cm[:, 2:3], jnp.where(col == lp1, cm[:, 3:4], 0.0)).astype(BF16)
        y = jnp.dot(w, yt[cur + k], preferred_element_type=F32)
        o_ref[rs, :] = _layer_norm(ALPHA * x_ref[rs, :] + y, g3_ref[...], b3_ref[...])


def _combine(plan, route, x2, ys, g3, b3):
    T, D = x2.shape
    rows = PER_STEP * MOE_TILE
    grid_spec = pltpu.PrefetchScalarGridSpec(
        num_scalar_prefetch=6,
        grid=(T // rows,),
        in_specs=[pl.BlockSpec((rows, LANES), lambda t, *_: (t, 0)),
                  pl.BlockSpec((rows, D), lambda t, *_: (t, 0)),
                  pl.BlockSpec(memory_space=pl.ANY),
                  pl.BlockSpec((1, D), lambda t, *_: (0, 0)),
                  pl.BlockSpec((1, D), lambda t, *_: (0, 0))],
        out_specs=pl.BlockSpec((rows, D), lambda t, *_: (t, 0)),
        scratch_shapes=[pltpu.VMEM((2 * PER_STEP, SORT_ROWS, D), BF16),
                        pltpu.SemaphoreType.DMA((2 * PER_STEP,))],
    )
    return pl.pallas_call(
        _combine_kernel,
        grid_spec=grid_spec,
        out_shape=jax.ShapeDtypeStruct((T, D), F32),
        compiler_params=pltpu.CompilerParams(
            dimension_semantics=("arbitrary",), vmem_limit_bytes=VMEM_LIMIT),
        name="moe_combine",
    )(*_copy_lists(plan), route, x2, ys, g3, b3)


def _moe(x2, x2b, route, lpt, meta, wg, wu_b, wd_b, g3, b3):
    plan = _plan(meta, x2.shape[0])
    xs, wg_b = _dispatch(plan, lpt, x2b, wg)
    ys = _gmm(plan, xs, wg_b, wu_b, wd_b)
    return _combine(plan, route, x2, ys, g3, b3)


def _row(v):
    return v.reshape(1, -1).astype(F32)


def kernel(x, mem, w_in, b_in, w_dw, b_dw, g_conv_norm, b_conv_norm, attn_sinks, w_out, g_ln1, b_ln1,
           w_mq, w_mkv, w_mo, g_ln2, b_ln2, w_group, b_group, w_router, b_router, w_gate, w_up, w_down,
           g_ln3, b_ln3):
    B, S, D = x.shape
    for l in range(DEPTH):
        w_dw_p = jnp.zeros((CONV_HALO, CONV_CH), F32).at[:CONV_WIDTH].set(w_dw[l])
        x1, wd_b = _mixer(x, attn_sinks[l].astype(F32), w_in[l], _row(b_in[l]), w_dw_p,
                          _row(b_dw[l]), _row(g_conv_norm[l]), _row(b_conv_norm[l]),
                          w_out[l], _row(g_ln1[l]), _row(b_ln1[l]), w_down[l])

        kvm = _memkv(mem.reshape(B * MEM_LEN, D), w_mkv[l]).reshape(B, MEM_LEN, 2 * D)

        wr = jnp.concatenate(
            [w_group[l], jnp.transpose(w_router[l], (1, 0, 2)).reshape(D, N_EXPERTS)], axis=1)
        wr = jnp.pad(wr, ((0, 0), (0, LANES - wr.shape[1])))
        br = jnp.pad(jnp.concatenate([b_group[l], b_router[l].reshape(-1)]), (0, LANES - N_GROUPS - N_EXPERTS))
        x2, x2b, route, lpt, meta, wu_b = _memattn(x1, w_mq[l], kvm, w_mo[l], _row(g_ln2[l]), _row(b_ln2[l]),
                                                   wr.astype(F32), _row(br), w_up[l])

        T = B * S
        y = _moe(x2.reshape(T, D), x2b.reshape(T, D), route.reshape(T, LANES), lpt, meta,
                 w_gate[l], wu_b, wd_b,
                 _row(g_ln3[l]), _row(b_ln3[l]))
        x = y.reshape(B, S, D)
    return x
```

```python
import jax
import jax.numpy as jnp
from jax import lax
from jax.experimental import pallas as pl
from jax.experimental.pallas import tpu as pltpu

D_MODEL = 1024
MEM_LEN = 256
CONV_CH = 512
CONV_WIDTH = 31
N_HEADS = 8
N_KV_HEADS = 2
HEAD_DIM = 64
GQ = N_HEADS // N_KV_HEADS
ATTN_W = N_HEADS * HEAD_DIM
KV_W = N_KV_HEADS * HEAD_DIM
WINDOW = 128
D_MIX = CONV_CH + ATTN_W
D_IN = 2 * CONV_CH + ATTN_W + 2 * KV_W
MEM_HEADS = 4
MEM_HEAD_DIM = D_MODEL // MEM_HEADS
N_GROUPS = 4
EXPERTS_PER_GROUP = 4
N_EXPERTS = N_GROUPS * EXPERTS_PER_GROUP
D_EXPERT = D_MODEL // 2
DEPTH = 1
ALPHA = (2.0 * DEPTH) ** 0.25
LN_EPS = 1e-5

LANES = 128
SUBLANES = 8
CONV_ROWS = 128
LN_ROWS = 64
MEMKV_COLS = 1024
MEM_TILE = 1024
LOG2E = 1.4426950408889634
MASK_VALUE = -1e30
CONV_HALO = 32
SEQ_TILE = 512
MOE_TILE = 512
CHUNK = 16
REGION_ALIGN = 512
GMM_ROWS = 2 * REGION_ALIGN
TOP_K = 2
SORT_ROWS = -(-(MOE_TILE * TOP_K + N_EXPERTS * (CHUNK - 1)) // 256) * 256
BIG = 4
MAX_BIG = SORT_ROWS // (BIG * CHUNK)
MAX_SMALL = N_EXPERTS * (BIG - 1)
PER_STEP = 2
WAIT_GROUP = 8
ROUTE_OFF = N_GROUPS
VMEM_LIMIT = 56 * 1024 * 1024

BF16 = jnp.bfloat16
F32 = jnp.float32


def _layer_norm(x, g, b):
    mu = jnp.mean(x, axis=-1, keepdims=True)
    xc = x - mu
    var = jnp.mean(xc * xc, axis=-1, keepdims=True)
    return xc * lax.rsqrt(var + LN_EPS) * g + b


def _cast_bf16(dst_ref, src_ref):
    rows = 256
    for r0 in range(0, src_ref.shape[0], rows):
        dst_ref[r0:r0 + rows, :] = src_ref[r0:r0 + rows, :].astype(BF16)


def _dot_nt(a, b):
    return lax.dot_general(a, b, (((1,), (1,)), ((), ())), preferred_element_type=F32)


def _mixer_kernel(sinks_ref, x_ref, w_in_ref, b_in_ref, w_dw_ref, b_dw_ref, g_cn_ref, b_cn_ref,
                  w_out_ref, g1_ref, b1_ref, we_ref, o_ref, web_ref, w_in_b, w_out_b, hbuf, hshift, cbuf, qbuf,
                  kbuf, vbuf, ymix):
    i = pl.program_id(1)
    ts = SEQ_TILE
    web_ref[...] = we_ref[...].astype(BF16)

    @pl.when(jnp.logical_and(pl.program_id(0) == 0, i == 0))
    def _():
        _cast_bf16(w_in_b, w_in_ref)
        _cast_bf16(w_out_b, w_out_ref)

    @pl.when(i == 0)
    def _():
        hbuf[0:CONV_HALO, :] = jnp.zeros((CONV_HALO, CONV_CH), F32)
        kbuf[:, 0:WINDOW, :] = jnp.zeros((2 * N_KV_HEADS, WINDOW, KV_W), BF16)
        vbuf[:, 0:WINDOW, :] = jnp.zeros((2, WINDOW, KV_W), BF16)

    x = x_ref[0]
    u = jnp.dot(x.astype(BF16), w_in_b[...], preferred_element_type=F32) + b_in_ref[...]
    a = u[:, 0:CONV_CH]
    gate = u[:, CONV_CH:2 * CONV_CH]
    hbuf[CONV_HALO:CONV_HALO + ts, :] = a * jax.nn.sigmoid(gate)
    qbuf[...] = (u[:, 2 * CONV_CH:2 * CONV_CH + ATTN_W] * (HEAD_DIM ** -0.5 * LOG2E)).astype(BF16)
    kf = u[:, 2 * CONV_CH + ATTN_W:2 * CONV_CH + ATTN_W + KV_W]
    vf = u[:, 2 * CONV_CH + ATTN_W + KV_W:D_IN]
    kr = pltpu.roll(kf, HEAD_DIM, axis=1)
    vr = pltpu.roll(vf, HEAD_DIM, axis=1)
    lo = lax.broadcasted_iota(jnp.int32, (ts, KV_W), 1) < HEAD_DIM
    rows = slice(WINDOW, WINDOW + ts)
    kbuf[0, rows, :] = jnp.where(lo, kf, 0.0).astype(BF16)
    kbuf[1, rows, :] = jnp.where(lo, 0.0, kr).astype(BF16)
    kbuf[2, rows, :] = jnp.where(lo, kr, 0.0).astype(BF16)
    kbuf[3, rows, :] = jnp.where(lo, 0.0, kf).astype(BF16)
    vbuf[0, rows, :] = vf.astype(BF16)
    vbuf[1, rows, :] = vr.astype(BF16)

    base = CONV_HALO - (CONV_WIDTH - 1)
    n_shift = ts + CONV_HALO - SUBLANES
    for b in range(1, SUBLANES):
        hshift[b - 1, 0:n_shift, :] = hbuf[b:b + n_shift, :]
    rc = CONV_ROWS

    def conv_chunk(c):
        r0 = c * rc
        for l in range(CONV_CH // LANES):
            ls = slice(l * LANES, (l + 1) * LANES)
            acc = jnp.zeros((rc, LANES), F32)
            for j in range(CONV_WIDTH):
                a8, b = divmod(j + base, SUBLANES)
                rs = slice(r0 + SUBLANES * a8, r0 + SUBLANES * a8 + rc)
                tap = hbuf[rs, ls] if b == 0 else hshift[b - 1, rs, ls]
                acc = acc + tap * w_dw_ref[j:j + 1, ls]
            cbuf[r0:r0 + rc, ls] = acc
        for r1 in range(r0, r0 + rc, LN_ROWS):
            rs = slice(r1, r1 + LN_ROWS)
            y = _layer_norm(cbuf[rs, :] + b_dw_ref[...], g_cn_ref[...], b_cn_ref[...])
            y = y * jax.nn.sigmoid(y)
            ymix[rs, 0:CONV_CH] = y.astype(BF16)

    qi = lax.broadcasted_iota(jnp.int32, (2 * WINDOW, WINDOW), 0) % WINDOW
    kj = lax.broadcasted_iota(jnp.int32, (2 * WINDOW, WINDOW), 1)
    own = kj <= qi
    top = lax.broadcasted_iota(jnp.int32, (2 * WINDOW, 1), 0) < WINDOW
    lo_out = lax.broadcasted_iota(jnp.int32, (WINDOW, 2 * HEAD_DIM), 1) < HEAD_DIM

    def attn_block(jb):
        r0 = jb * WINDOW
        prev_ok = jnp.logical_not(own) & (i != 0) if jb == 0 else jnp.logical_not(own)
        for kvh in range(N_KV_HEADS):
            h0 = kvh * GQ
            c0 = h0 * HEAD_DIM
            qs = jnp.concatenate([qbuf[r0:r0 + WINDOW, c0:c0 + 2 * HEAD_DIM],
                                  qbuf[r0:r0 + WINDOW, c0 + 2 * HEAD_DIM:c0 + 4 * HEAD_DIM]], axis=0)
            pv = []
            for par in range(2):
                kk = kbuf[2 * kvh + par, r0:r0 + 2 * WINDOW, :]
                vv = vbuf[(kvh + par) % 2, r0:r0 + 2 * WINDOW, :]
                s2 = _dot_nt(qs, kk)
                prev_part = jnp.where(prev_ok, s2[:, :WINDOW], MASK_VALUE) if jb == 0 else s2[:, :WINDOW]
                s = jnp.where(own, s2[:, WINDOW:], prev_part)
                sink = jnp.where(top, sinks_ref[h0 + par] * LOG2E, sinks_ref[h0 + 2 + par] * LOG2E)
                m = jnp.maximum(jnp.max(s, axis=-1, keepdims=True), sink)
                p = jnp.exp2(s - m)
                denom = jnp.sum(p, axis=-1, keepdims=True) + jnp.exp2(sink - m)
                zero = jnp.zeros((), BF16)
                pb = p.astype(BF16)
                p2 = jnp.concatenate([jnp.where(own, zero, pb), jnp.where(own, pb, zero)], axis=1)
                pv.append(jnp.dot(p2, vv, preferred_element_type=F32) / denom)
            for pair in range(2):
                rs = slice(pair * WINDOW, (pair + 1) * WINDOW)
                o = jnp.where(lo_out, pv[0][rs], pv[1][rs])
                cs = CONV_CH + c0 + pair * 2 * HEAD_DIM
                ymix[r0:r0 + WINDOW, cs:cs + 2 * HEAD_DIM] = o.astype(BF16)

    assert ts // rc == ts // WINDOW
    for c in range(ts // rc):
        attn_block(c)
        conv_chunk(c)

    mix = jnp.dot(ymix[...], w_out_b[...], preferred_element_type=F32)
    o_ref[0] = _layer_norm(ALPHA * x + mix, g1_ref[...], b1_ref[...])

    hbuf[0:CONV_HALO, :] = hbuf[ts:ts + CONV_HALO, :]
    kbuf[:, 0:WINDOW, :] = kbuf[:, ts:ts + WINDOW, :]
    vbuf[:, 0:WINDOW, :] = vbuf[:, ts:ts + WINDOW, :]


def _const_spec(shape):
    nd = len(shape)
    return pl.BlockSpec(shape, lambda *_: (0,) * nd)


def _resident_spec(shape):
    nd = len(shape)
    return pl.BlockSpec(shape, lambda *_: (0,) * nd, pipeline_mode=pl.Buffered(1))


def _mixer(x, sinks, w_in, b_in, w_dw, b_dw, g_cn, b_cn, w_out, g1, b1, w_expert):
    B, S, D = x.shape
    ts = SEQ_TILE
    tile = pl.BlockSpec((1, ts, D), lambda b, i: (b, i, 0))
    nt = S // ts
    n_e, we_rows, we_cols = w_expert.shape
    per = (B * nt) // n_e
    assert per * n_e == B * nt and we_rows % per == 0
    wes = pl.BlockSpec((1, we_rows // per, we_cols), lambda b, i: ((b * nt + i) // per, (b * nt + i) % per, 0))
    return pl.pallas_call(
        _mixer_kernel,
        grid=(B, S // ts),
        in_specs=[
            pl.BlockSpec(memory_space=pltpu.SMEM),
            tile,
            _resident_spec((D, D_IN)), _const_spec((1, D_IN)),
            _const_spec((CONV_HALO, CONV_CH)), _const_spec((1, CONV_CH)),
            _const_spec((1, CONV_CH)), _const_spec((1, CONV_CH)),
            _resident_spec((D_MIX, D)), _const_spec((1, D)), _const_spec((1, D)),
            wes,
        ],
        out_specs=[tile, wes],
        out_shape=[jax.ShapeDtypeStruct((B, S, D), F32), jax.ShapeDtypeStruct(w_expert.shape, BF16)],
        scratch_shapes=[
            pltpu.VMEM((D, D_IN), BF16),
            pltpu.VMEM((D_MIX, D), BF16),
            pltpu.VMEM((CONV_HALO + ts, CONV_CH), F32),
            pltpu.VMEM((SUBLANES - 1, CONV_HALO + ts, CONV_CH), F32),
            pltpu.VMEM((ts, CONV_CH), F32),
            pltpu.VMEM((ts, ATTN_W), BF16),
            pltpu.VMEM((2 * N_KV_HEADS, WINDOW + ts, KV_W), BF16),
            pltpu.VMEM((2, WINDOW + ts, KV_W), BF16),
            pltpu.VMEM((ts, D_MIX), BF16),
        ],
        compiler_params=pltpu.CompilerParams(
            dimension_semantics=("arbitrary", "arbitrary"), vmem_limit_bytes=VMEM_LIMIT),
        name="mixer",
    )(sinks, x, w_in, b_in, w_dw, b_dw, g_cn, b_cn, w_out, g1, b1, w_expert)


def _memkv_kernel(mem_ref, w_ref, o_ref):
    o_ref[...] = jnp.dot(mem_ref[...].astype(BF16), w_ref[...].astype(BF16),
                         preferred_element_type=F32).astype(BF16)


def _memkv(mem2d, w_mkv):
    M, D = mem2d.shape
    N = w_mkv.shape[1]
    tn = MEMKV_COLS
    return pl.pallas_call(
        _memkv_kernel,
        grid=(N // tn,),
        in_specs=[pl.BlockSpec((M, D), lambda j: (0, 0)), pl.BlockSpec((D, tn), lambda j: (0, j))],
        out_specs=pl.BlockSpec((M, tn), lambda j: (0, j)),
        out_shape=jax.ShapeDtypeStruct((M, N), BF16),
        compiler_params=pltpu.CompilerParams(dimension_semantics=("arbitrary",)),
        name="memkv",
    )(mem2d, w_mkv)


def _first_max(rows):
    best = rows[0]
    for r in rows[1:]:
        best = jnp.maximum(best, r)
    idx = jnp.full(best.shape, len(rows) - 1, jnp.int32)
    for k in range(len(rows) - 2, -1, -1):
        idx = jnp.where(rows[k] == best, k, idx)
    return best, idx


def _route_plan(logits_t):
    tile = logits_t.shape[1]
    row = lambda k: logits_t[k:k + 1, :]
    gmax, g_idx = _first_max([row(g) for g in range(N_GROUPS)])
    gsum = jnp.exp(row(0) - gmax)
    for g in range(1, N_GROUPS):
        gsum = gsum + jnp.exp(row(g) - gmax)
    g_p = 1.0 / gsum
    rl = []
    for e in range(EXPERTS_PER_GROUP):
        v = row(ROUTE_OFF + (N_GROUPS - 1) * EXPERTS_PER_GROUP + e)
        for g in range(N_GROUPS - 2, -1, -1):
            v = jnp.where(g_idx == g, row(ROUTE_OFF + g * EXPERTS_PER_GROUP + e), v)
        rl.append(v)
    m1, i1 = _first_max(rl)
    m2, i2 = _first_max([jnp.where(i1 == e, MASK_VALUE, rl[e]) for e in range(EXPERTS_PER_GROUP)])
    ex = jnp.exp(m2 - m1)
    w1 = 1.0 / (1.0 + ex)
    w2 = ex * w1
    e1 = g_idx * EXPERTS_PER_GROUP + i1
    e2 = g_idx * EXPERTS_PER_GROUP + i2

    eid = lax.broadcasted_iota(jnp.int32, (N_EXPERTS, tile), 0)
    hit1 = eid == e1
    hit2 = eid == e2
    oh = jnp.where(jnp.logical_or(hit1, hit2), 1.0, 0.0)
    r = lax.broadcasted_iota(jnp.int32, (tile, tile), 0)
    c = lax.broadcasted_iota(jnp.int32, (tile, tile), 1)
    tri = jnp.where(r <= c, 1.0, 0.0).astype(BF16)
    csum = jnp.dot(oh.astype(BF16), tri, preferred_element_type=F32)
    counts = jnp.broadcast_to(csum[:, tile - 1:tile], (N_EXPERTS, tile)).astype(jnp.int32)
    nch = jnp.right_shift(counts + (CHUNK - 1), CHUNK.bit_length() - 1)
    er = lax.broadcasted_iota(jnp.int32, (N_EXPERTS, N_EXPERTS), 0)
    ec = lax.broadcasted_iota(jnp.int32, (N_EXPERTS, N_EXPERTS), 1)
    lower = jnp.where(ec < er, 1.0, 0.0).astype(BF16)
    off = jnp.dot(lower, nch.astype(F32).astype(BF16), preferred_element_type=F32) * CHUNK
    pos = off + csum - oh
    lp1 = jnp.sum(jnp.where(hit1, pos, 0.0), axis=0, keepdims=True)
    lp2 = jnp.sum(jnp.where(hit2, pos, 0.0), axis=0, keepdims=True)
    zero = jnp.zeros_like(lp1)
    route_t = jnp.concatenate([lp1, lp2, g_p * w1, g_p * w2, zero, zero, zero, zero], axis=0)
    meta = jnp.concatenate([nch[:, 0:LANES], off[:, 0:LANES].astype(jnp.int32)], axis=0)
    return route_t, meta


def _memattn_kernel(x_ref, wq_ref, k_ref, v_ref, wo_ref, g2_ref, b2_ref, wr2_ref, wrh_ref, br_ref, we_ref,
                    o_ref, ob_ref, route_ref, routet_ref, meta_ref, web_ref, wq_b, wo_b):
    web_ref[...] = we_ref[...].astype(BF16)

    @pl.when(jnp.logical_and(pl.program_id(0) == 0, pl.program_id(1) == 0))
    def _():
        _cast_bf16(wq_b, wq_ref)
        _cast_bf16(wo_b, wo_ref)

    def rows_logits(rs):
        x = x_ref[0, rs, :]
        q = jnp.dot(x.astype(BF16), wq_b[...], preferred_element_type=F32)
        q = (q * (MEM_HEAD_DIM ** -0.5)).astype(BF16)
        outs = []
        for h in range(MEM_HEADS):
            sl = slice(h * MEM_HEAD_DIM, (h + 1) * MEM_HEAD_DIM)
            s = _dot_nt(q[:, sl], k_ref[0, :, sl])
            m = jnp.max(s, axis=-1, keepdims=True)
            p = jnp.exp(s - m)
            denom = jnp.sum(p, axis=-1, keepdims=True)
            o = jnp.dot(p.astype(BF16), v_ref[0, :, sl], preferred_element_type=F32)
            outs.append((o / denom).astype(BF16))
        o = jnp.dot(jnp.concatenate(outs, axis=-1), wo_b[...], preferred_element_type=F32)
        x2 = _layer_norm(ALPHA * x + o, g2_ref[...], b2_ref[...])
        o_ref[0, rs, :] = x2
        x2h = x2.astype(BF16)
        ob_ref[0, rs, :] = x2h
        x2l = (x2 - x2h.astype(F32)).astype(BF16)
        hh = jnp.dot(x2h, wr2_ref[...], preferred_element_type=F32)
        return (hh[:, 0:LANES] + hh[:, LANES:2 * LANES]
                + jnp.dot(x2l, wrh_ref[...], preferred_element_type=F32) + br_ref[...])

    ts = x_ref.shape[1]
    groups = [slice(r0, r0 + MOE_TILE) for r0 in range(0, ts, MOE_TILE)]
    logits = rows_logits(slice(0, ts))
    pad = jnp.zeros((LANES - SUBLANES, MOE_TILE), F32)
    for k, rs in enumerate(groups):
        route_t, meta = _route_plan(jnp.transpose(logits[rs, :]))
        routet_ref[k] = route_t
        route_ref[0, rs, :] = jnp.transpose(jnp.concatenate([route_t, pad], axis=0))
        meta_ref[k] = meta


def _memattn(x1, wq, kvm, wo, g2, b2, wr, br, w_expert):
    B, S, D = x1.shape
    ts = MEM_TILE
    assert B * (S // ts) == w_expert.shape[0]
    wes = pl.BlockSpec((1,) + w_expert.shape[1:], lambda b, i: (b * (S // ts) + i, 0, 0))
    per = ts // MOE_TILE
    nt = S // ts
    c = wr * (2.0 ** 16 + 1.0)
    w_high = c - (c - wr)
    wrh = w_high.astype(BF16)
    wr2 = jnp.concatenate([wrh, (wr - w_high).astype(BF16)], axis=1)
    tile = pl.BlockSpec((1, ts, D), lambda b, i: (b, i, 0))
    kspec = pl.BlockSpec((1, MEM_LEN, D), lambda b, i: (b, 0, 0))
    vspec = pl.BlockSpec((1, MEM_LEN, D), lambda b, i: (b, 0, 1))
    return pl.pallas_call(
        _memattn_kernel,
        grid=(B, nt),
        in_specs=[tile, _resident_spec((D, D)), kspec, vspec, _resident_spec((D, D)),
                  _const_spec((1, D)), _const_spec((1, D)),
                  _const_spec((D, 2 * LANES)), _const_spec((D, LANES)), _const_spec((1, LANES)), wes],
        out_specs=[tile, tile, pl.BlockSpec((1, ts, LANES), lambda b, i: (b, i, 0)),
                   pl.BlockSpec((per, SUBLANES, MOE_TILE), lambda b, i: (b * nt + i, 0, 0)),
                   pl.BlockSpec((per, 2 * N_EXPERTS, LANES), lambda b, i: (b * nt + i, 0, 0)), wes],
        out_shape=[jax.ShapeDtypeStruct((B, S, D), F32),
                   jax.ShapeDtypeStruct((B, S, D), BF16),
                   jax.ShapeDtypeStruct((B, S, LANES), F32),
                   jax.ShapeDtypeStruct((B * nt * per, SUBLANES, MOE_TILE), F32),
                   jax.ShapeDtypeStruct((B * nt * per, 2 * N_EXPERTS, LANES), jnp.int32),
                   jax.ShapeDtypeStruct(w_expert.shape, BF16)],
        scratch_shapes=[pltpu.VMEM((D, D), BF16), pltpu.VMEM((D, D), BF16)],
        compiler_params=pltpu.CompilerParams(
            dimension_semantics=("arbitrary", "arbitrary"), vmem_limit_bytes=VMEM_LIMIT),
        name="memattn",
    )(x1, wq, kvm, kvm, wo, g2, b2, wr2, wrh, br, w_expert)


def _gmm_blocks(n_tokens):
    rows = (n_tokens * TOP_K + (n_tokens // MOE_TILE) * N_EXPERTS * (CHUNK - 1)
            + N_EXPERTS * (REGION_ALIGN - CHUNK))
    return -(-rows // GMM_ROWS)


def _plan(meta, T):
    nch = meta[:, :N_EXPERTS, 0]
    n16 = nch * CHUNK
    n_e = jnp.sum(n16, axis=0)
    reg = (n_e + REGION_ALIGN - 1) // REGION_ALIGN * REGION_ALIGN
    gend = jnp.cumsum(reg)
    gbase = gend - reg
    dst = gbase[None, :] + jnp.cumsum(n16, axis=0) - n16
    half_row = jnp.arange(2 * _gmm_blocks(T), dtype=jnp.int32) * REGION_ALIGN
    half_expert = jnp.minimum(jnp.sum(half_row[:, None] >= gend[None, :], axis=1), N_EXPERTS - 1)
    half_used = half_row < gend[-1]
    n_used = ((gend[-1] + GMM_ROWS - 1) // GMM_ROWS).astype(jnp.int32).reshape(1)
    src = (jnp.cumsum(nch, axis=1) - nch) * CHUNK
    n_big = nch // BIG

    def copy_list(count, src0, dst0, rows, length):
        cum = jnp.cumsum(count, axis=1)
        first = (cum - count)[:, None, :]
        k = jnp.arange(length, dtype=jnp.int32)[None, :, None]
        mine = (k >= first) & (k < cum[:, None, :])
        step = (k - first) * rows
        pick = lambda base: jnp.sum(jnp.where(mine, base[:, None, :] + step, 0), axis=2)
        return pick(src0), pick(dst0), cum[:, -1]

    big_src, big_dst, big_n = copy_list(n_big, src, dst, BIG * CHUNK, MAX_BIG)
    rest = n_big * (BIG * CHUNK)
    small_src, small_dst, small_n = copy_list(nch - n_big * BIG, src + rest, dst + rest, CHUNK, MAX_SMALL)
    i32 = lambda a: a.astype(jnp.int32)
    fill_start = jnp.concatenate([gbase + n_e, gend[-1:]])
    fill_n = jnp.concatenate([reg - n_e, _gmm_blocks(T) * GMM_ROWS - gend[-1:]]) // CHUNK
    return dict(big_src=i32(big_src).reshape(-1), big_dst=i32(big_dst).reshape(-1), big_n=i32(big_n),
                small_src=i32(small_src).reshape(-1), small_dst=i32(small_dst).reshape(-1), small_n=i32(small_n),
                fill_start=i32(fill_start), fill_n=i32(fill_n),
                fill_tot=i32(jnp.sum(fill_n)).reshape(1), half_expert=i32(half_expert),
                half_used=i32(half_used), n_used=n_used)


def _rows_copy(src_ref, src_row, dst_ref, dst_row, rows, sem):
    return pltpu.make_async_copy(
        src_ref.at[pl.ds(pl.multiple_of(src_row, CHUNK), rows), :],
        dst_ref.at[pl.ds(pl.multiple_of(dst_row, CHUNK), rows), :], sem)


def _chunk_copy(src_ref, src_row, dst_ref, dst_row, sem):
    return _rows_copy(src_ref, src_row, dst_ref, dst_row, CHUNK, sem)


def _issue_copies(lists, tile, tile_ref, tile_is_src, hbm_ref, sem):
    big_src, big_dst, big_n, small_src, small_dst, small_n = lists
    for src_l, dst_l, n_l, length, rows in ((big_src, big_dst, big_n, MAX_BIG, BIG * CHUNK),
                                            (small_src, small_dst, small_n, MAX_SMALL, CHUNK)):
        def issue(k, carry, src_l=src_l, dst_l=dst_l, length=length, rows=rows):
            local, remote = src_l[tile * length + k], dst_l[tile * length + k]
            if tile_is_src:
                _rows_copy(tile_ref, local, hbm_ref, remote, rows, sem).start()
            else:
                _rows_copy(hbm_ref, remote, tile_ref, local, rows, sem).start()
            return carry

        lax.fori_loop(0, n_l[tile], issue, 0)


def _wait_rows(n, rows, src_ref, dst_ref, sem):
    def body(c, carry):
        pltpu.make_async_copy(src_ref.at[pl.ds(0, rows), :], dst_ref.at[pl.ds(0, rows), :], sem).wait()
        return carry

    lax.fori_loop(0, n, body, 0)


def _wait_copies(lists, tile, src_ref, dst_ref, sem, extra_chunks=0):
    _wait_rows(lists[2][tile], BIG * CHUNK, src_ref, dst_ref, sem)
    n = lists[5][tile] + extra_chunks
    _wait_rows(n // WAIT_GROUP, WAIT_GROUP * CHUNK, src_ref, dst_ref, sem)
    _wait_rows(n % WAIT_GROUP, CHUNK, src_ref, dst_ref, sem)


def _dispatch_kernel(bs_ref, bd_ref, bn_ref, ss_ref, sd_ref, sn_ref, fstart_ref, fn_ref, ftot_ref,
                     lp_ref, x_ref, we_ref, xs_hbm, web_ref, xt, zbuf, sems):
    lists = (bs_ref, bd_ref, bn_ref, ss_ref, sd_ref, sn_ref)
    s = pl.program_id(0)
    ns = pl.num_programs(0)
    cur = (s % 2) * PER_STEP
    prv = PER_STEP - cur

    @pl.when(s >= 2)
    def _():
        for k in range(PER_STEP):
            _wait_copies(lists, (s - 2) * PER_STEP + k, xt.at[cur + k], xs_hbm, sems.at[cur + k])

    r = lax.broadcasted_iota(jnp.int32, (SORT_ROWS, MOE_TILE), 0)
    for k in range(PER_STEP):
        lp = lp_ref[k].astype(jnp.int32)
        hit = jnp.logical_or(lp[0:1, :] == r, lp[1:2, :] == r)
        p = jnp.where(hit, 1.0, 0.0).astype(BF16)
        x = x_ref[k * MOE_TILE:(k + 1) * MOE_TILE, :]
        xt[cur + k] = jnp.dot(p, x, preferred_element_type=F32).astype(BF16)

    for k in range(PER_STEP):
        _issue_copies(lists, s * PER_STEP + k, xt.at[cur + k], True, xs_hbm, sems.at[cur + k])

    _cast_bf16(web_ref.at[0], we_ref.at[0])

    @pl.when(s == ns - 1)
    def _():
        zbuf[...] = jnp.zeros_like(zbuf)
        sem = sems.at[cur]

        def per_range(e, carry):
            def issue(c, carry2):
                _chunk_copy(zbuf, 0, xs_hbm, fstart_ref[e] + c * CHUNK, sem).start()
                return carry2

            return lax.fori_loop(0, fn_ref[e], issue, carry)

        lax.fori_loop(0, N_EXPERTS + 1, per_range, 0)
        for k in range(PER_STEP):
            _wait_copies(lists, (s - 1) * PER_STEP + k, xt.at[prv + k], xs_hbm, sems.at[prv + k])
            _wait_copies(lists, s * PER_STEP + k, xt.at[cur + k], xs_hbm, sems.at[cur + k],
                         extra_chunks=ftot_ref[0] if k == 0 else 0)


def _copy_lists(plan):
    return tuple(plan[k] for k in ("big_src", "big_dst", "big_n", "small_src", "small_dst", "small_n"))


def _dispatch(plan, lpt, x2b, w_expert):
    T, D = x2b.shape
    rows = PER_STEP * MOE_TILE
    assert T // rows >= 2
    assert T // rows == N_EXPERTS
    wes = pl.BlockSpec((1,) + w_expert.shape[1:], lambda t, *_: (t, 0, 0))
    grid_spec = pltpu.PrefetchScalarGridSpec(
        num_scalar_prefetch=9,
        grid=(T // rows,),
        in_specs=[pl.BlockSpec((PER_STEP, SUBLANES, MOE_TILE), lambda t, *_: (t, 0, 0)),
                  pl.BlockSpec((rows, D), lambda t, *_: (t, 0)), wes],
        out_specs=[pl.BlockSpec(memory_space=pl.ANY), wes],
        scratch_shapes=[pltpu.VMEM((2 * PER_STEP, SORT_ROWS, D), BF16), pltpu.VMEM((CHUNK, D), BF16),
                        pltpu.SemaphoreType.DMA((2 * PER_STEP,))],
    )
    return pl.pallas_call(
        _dispatch_kernel,
        grid_spec=grid_spec,
        out_shape=[jax.ShapeDtypeStruct((_gmm_blocks(T) * GMM_ROWS, D), BF16),
                   jax.ShapeDtypeStruct(w_expert.shape, BF16)],
        compiler_params=pltpu.CompilerParams(
            dimension_semantics=("arbitrary",), vmem_limit_bytes=VMEM_LIMIT),
        name="moe_dispatch",
    )(*_copy_lists(plan), plan["fill_start"], plan["fill_n"], plan["fill_tot"], lpt, x2b, w_expert)


def _expert_ffn(xb, wg_b, wu_b, wd_b):
    g = jnp.dot(xb, wg_b[...], preferred_element_type=F32)
    u = jnp.dot(xb, wu_b[...], preferred_element_type=F32)
    h = (g * jax.nn.sigmoid(g)) * u
    return jnp.dot(h.astype(BF16), wd_b[...], preferred_element_type=F32).astype(BF16)


def _gmm_kernel(he_ref, hu_ref, nu_ref, x_ref, wga_ref, wua_ref, wda_ref, wgb_ref, wub_ref, wdb_ref, o_ref):
    b = pl.program_id(0)
    half = REGION_ALIGN
    same = he_ref[2 * b] == he_ref[2 * b + 1]
    used0 = hu_ref[2 * b] != 0
    used1 = hu_ref[2 * b + 1] != 0
    set_a = (wga_ref.at[0], wua_ref.at[0], wda_ref.at[0])
    set_b = (wgb_ref.at[0], wub_ref.at[0], wdb_ref.at[0])

    @pl.when(jnp.logical_and(used1, same))
    def _():
        o_ref[...] = _expert_ffn(x_ref[...], *set_a)

    @pl.when(jnp.logical_and(used0, jnp.logical_not(jnp.logical_and(used1, same))))
    def _():
        o_ref[0:half, :] = _expert_ffn(x_ref[0:half, :], *set_a)

    @pl.when(jnp.logical_and(used1, jnp.logical_not(same)))
    def _():
        o_ref[half:GMM_ROWS, :] = _expert_ffn(x_ref[half:GMM_ROWS, :], *set_b)

    @pl.when(jnp.logical_not(used0))
    def _():
        o_ref[0:half, :] = jnp.zeros((half, o_ref.shape[1]), o_ref.dtype)

    @pl.when(jnp.logical_not(used1))
    def _():
        o_ref[half:GMM_ROWS, :] = jnp.zeros((GMM_ROWS - half, o_ref.shape[1]), o_ref.dtype)


def _gmm(plan, xs, wg, wu, wd):
    R, D = xs.shape
    rows = pl.BlockSpec((GMM_ROWS, D), lambda b, he, hu, nu: (jnp.minimum(b, nu[0] - 1), 0))

    def wspec(shape, h):
        return pl.BlockSpec((1,) + shape, lambda b, he, hu, nu: (he[2 * b + h], 0, 0))

    grid_spec = pltpu.PrefetchScalarGridSpec(
        num_scalar_prefetch=3,
        grid=(R // GMM_ROWS,),
        in_specs=[rows,
                  wspec((D, D_EXPERT), 0), wspec((D, D_EXPERT), 0), wspec((D_EXPERT, D), 0),
                  wspec((D, D_EXPERT), 1), wspec((D, D_EXPERT), 1), wspec((D_EXPERT, D), 1)],
        out_specs=pl.BlockSpec((GMM_ROWS, D), lambda b, he, hu, nu: (b, 0)),
    )
    return pl.pallas_call(
        _gmm_kernel,
        grid_spec=grid_spec,
        out_shape=jax.ShapeDtypeStruct((R, D), BF16),
        compiler_params=pltpu.CompilerParams(
            dimension_semantics=("arbitrary",), vmem_limit_bytes=VMEM_LIMIT),
        name="moe_gmm",
    )(plan["half_expert"], plan["half_used"], plan["n_used"], xs, wg, wu, wd, wg, wu, wd)


def _combine_kernel(bs_ref, bd_ref, bn_ref, ss_ref, sd_ref, sn_ref, cm_ref, x_ref, ys_hbm, g3_ref, b3_ref,
                    o_ref, yt, sems):
    lists = (bs_ref, bd_ref, bn_ref, ss_ref, sd_ref, sn_ref)
    s = pl.program_id(0)
    ns = pl.num_programs(0)
    cur = (s % 2) * PER_STEP
    nxt = PER_STEP - cur

    def fetch(step, first_buf):
        for k in range(PER_STEP):
            _issue_copies(lists, step * PER_STEP + k, yt.at[first_buf + k], False, ys_hbm,
                          sems.at[first_buf + k])

    @pl.when(s == 0)
    def _():
        yt[...] = jnp.zeros_like(yt)
        fetch(0, 0)

    @pl.when(s + 1 < ns)
    def _():
        fetch(s + 1, nxt)

    for k in range(PER_STEP):
        _wait_copies(lists, s * PER_STEP + k, ys_hbm, yt.at[cur + k], sems.at[cur + k])

    col = lax.broadcasted_iota(jnp.int32, (MOE_TILE, SORT_ROWS), 1)
    for k in range(PER_STEP):
        rs = slice(k * MOE_TILE, (k + 1) * MOE_TILE)
        cm = cm_ref[rs, :]
        lp0 = cm[:, 0:1].astype(jnp.int32)
        lp1 = cm[:, 1:2].astype(jnp.int32)
        w = jnp.where(col == lp0, cm[:, 2:3], jnp.where(col == lp1, cm[:, 3:4], 0.0)).astype(BF16)
        y = jnp.dot(w, yt[cur + k], preferred_element_type=F32)
        o_ref[rs, :] = _layer_norm(ALPHA * x_ref[rs, :] + y, g3_ref[...], b3_ref[...])


def _combine(plan, route, x2, ys, g3, b3):
    T, D = x2.shape
    rows = PER_STEP * MOE_TILE
    grid_spec = pltpu.PrefetchScalarGridSpec(
        num_scalar_prefetch=6,
        grid=(T // rows,),
        in_specs=[pl.BlockSpec((rows, LANES), lambda t, *_: (t, 0)),
                  pl.BlockSpec((rows, D), lambda t, *_: (t, 0)),
                  pl.BlockSpec(memory_space=pl.ANY),
                  pl.BlockSpec((1, D), lambda t, *_: (0, 0)),
                  pl.BlockSpec((1, D), lambda t, *_: (0, 0))],
        out_specs=pl.BlockSpec((rows, D), lambda t, *_: (t, 0)),
        scratch_shapes=[pltpu.VMEM((2 * PER_STEP, SORT_ROWS, D), BF16),
                        pltpu.SemaphoreType.DMA((2 * PER_STEP,))],
    )
    return pl.pallas_call(
        _combine_kernel,
        grid_spec=grid_spec,
        out_shape=jax.ShapeDtypeStruct((T, D), F32),
        compiler_params=pltpu.CompilerParams(
            dimension_semantics=("arbitrary",), vmem_limit_bytes=VMEM_LIMIT),
        name="moe_combine",
    )(*_copy_lists(plan), route, x2, ys, g3, b3)


def _moe(x2, x2b, route, lpt, meta, wg, wu_b, wd_b, g3, b3):
    plan = _plan(meta, x2.shape[0])
    xs, wg_b = _dispatch(plan, lpt, x2b, wg)
    ys = _gmm(plan, xs, wg_b, wu_b, wd_b)
    return _combine(plan, route, x2, ys, g3, b3)


def _row(v):
    return v.reshape(1, -1).astype(F32)


def kernel(x, mem, w_in, b_in, w_dw, b_dw, g_conv_norm, b_conv_norm, attn_sinks, w_out, g_ln1, b_ln1,
           w_mq, w_mkv, w_mo, g_ln2, b_ln2, w_group, b_group, w_router, b_router, w_gate, w_up, w_down,
           g_ln3, b_ln3):
    B, S, D = x.shape
    for l in range(DEPTH):
        w_dw_p = jnp.zeros((CONV_HALO, CONV_CH), F32).at[:CONV_WIDTH].set(w_dw[l])
        x1, wd_b = _mixer(x, attn_sinks[l].astype(F32), w_in[l], _row(b_in[l]), w_dw_p,
                          _row(b_dw[l]), _row(g_conv_norm[l]), _row(b_conv_norm[l]),
                          w_out[l], _row(g_ln1[l]), _row(b_ln1[l]), w_down[l])

        kvm = _memkv(mem.reshape(B * MEM_LEN, D), w_mkv[l]).reshape(B, MEM_LEN, 2 * D)

        wr = jnp.concatenate(
            [w_group[l], jnp.transpose(w_router[l], (1, 0, 2)).reshape(D, N_EXPERTS)], axis=1)
        wr = jnp.pad(wr, ((0, 0), (0, LANES - wr.shape[1])))
        br = jnp.pad(jnp.concatenate([b_group[l], b_router[l].reshape(-1)]), (0, LANES - N_GROUPS - N_EXPERTS))
        x2, x2b, route, lpt, meta, wu_b = _memattn(x1, w_mq[l], kvm, w_mo[l], _row(g_ln2[l]), _row(b_ln2[l]),
                                                   wr.astype(F32), _row(br), w_up[l])

        T = B * S
        y = _moe(x2.reshape(T, D), x2b.reshape(T, D), route.reshape(T, LANES), lpt, meta,
                 w_gate[l], wu_b, wd_b,
                 _row(g_ln3[l]), _row(b_ln3[l]))
        x = y.reshape(B, S, D)
    return x
```

```python
import jax
import jax.numpy as jnp
from jax import lax
from jax.experimental import pallas as pl
from jax.experimental.pallas import tpu as pltpu

D_MODEL = 1024
MEM_LEN = 256
CONV_CH = 512
CONV_WIDTH = 31
N_HEADS = 8
N_KV_HEADS = 2
HEAD_DIM = 64
GQ = N_HEADS // N_KV_HEADS
ATTN_W = N_HEADS * HEAD_DIM
KV_W = N_KV_HEADS * HEAD_DIM
WINDOW = 128
D_MIX = CONV_CH + ATTN_W
D_IN = 2 * CONV_CH + ATTN_W + 2 * KV_W
MEM_HEADS = 4
MEM_HEAD_DIM = D_MODEL // MEM_HEADS
N_GROUPS = 4
EXPERTS_PER_GROUP = 4
N_EXPERTS = N_GROUPS * EXPERTS_PER_GROUP
D_EXPERT = D_MODEL // 2
DEPTH = 1
ALPHA = (2.0 * DEPTH) ** 0.25
LN_EPS = 1e-5

LANES = 128
SUBLANES = 8
CONV_ROWS = 128
LN_ROWS = 64
MEMKV_COLS = 1024
MEM_TILE = 1024
LOG2E = 1.4426950408889634
MASK_VALUE = -1e30
CONV_HALO = 32
SEQ_TILE = 512
MOE_TILE = 512
CHUNK = 16
REGION_ALIGN = 512
GMM_ROWS = 2 * REGION_ALIGN
TOP_K = 2
SORT_ROWS = -(-(MOE_TILE * TOP_K + N_EXPERTS * (CHUNK - 1)) // 256) * 256
BIG = 4
MAX_BIG = SORT_ROWS // (BIG * CHUNK)
MAX_SMALL = N_EXPERTS * (BIG - 1)
PER_STEP = 2
WAIT_GROUP = 8
ROUTE_OFF = N_GROUPS
VMEM_LIMIT = 56 * 1024 * 1024

BF16 = jnp.bfloat16
F32 = jnp.float32


def _layer_norm(x, g, b):
    mu = jnp.mean(x, axis=-1, keepdims=True)
    xc = x - mu
    var = jnp.mean(xc * xc, axis=-1, keepdims=True)
    return xc * lax.rsqrt(var + LN_EPS) * g + b


def _cast_bf16(dst_ref, src_ref):
    rows = 256
    for r0 in range(0, src_ref.shape[0], rows):
        dst_ref[r0:r0 + rows, :] = src_ref[r0:r0 + rows, :].astype(BF16)


def _dot_nt(a, b):
    return lax.dot_general(a, b, (((1,), (1,)), ((), ())), preferred_element_type=F32)


def _mixer_kernel(sinks_ref, x_ref, w_in_ref, b_in_ref, w_dw_ref, b_dw_ref, g_cn_ref, b_cn_ref,
                  w_out_ref, g1_ref, b1_ref, we_ref, o_ref, web_ref, w_in_b, w_out_b, hbuf, hshift, cbuf, qbuf,
                  kbuf, vbuf, ymix):
    i = pl.program_id(1)
    ts = SEQ_TILE
    web_ref[...] = we_ref[...].astype(BF16)

    @pl.when(jnp.logical_and(pl.program_id(0) == 0, i == 0))
    def _():
        _cast_bf16(w_in_b, w_in_ref)
        _cast_bf16(w_out_b, w_out_ref)

    @pl.when(i == 0)
    def _():
        hbuf[0:CONV_HALO, :] = jnp.zeros((CONV_HALO, CONV_CH), F32)
        kbuf[:, 0:WINDOW, :] = jnp.zeros((2 * N_KV_HEADS, WINDOW, KV_W), BF16)
        vbuf[:, 0:WINDOW, :] = jnp.zeros((2, WINDOW, KV_W), BF16)

    x = x_ref[0]
    u = jnp.dot(x.astype(BF16), w_in_b[...], preferred_element_type=F32) + b_in_ref[...]
    a = u[:, 0:CONV_CH]
    gate = u[:, CONV_CH:2 * CONV_CH]
    hbuf[CONV_HALO:CONV_HALO + ts, :] = a * jax.nn.sigmoid(gate)
    qbuf[...] = (u[:, 2 * CONV_CH:2 * CONV_CH + ATTN_W] * (HEAD_DIM ** -0.5 * LOG2E)).astype(BF16)
    kf = u[:, 2 * CONV_CH + ATTN_W:2 * CONV_CH + ATTN_W + KV_W]
    vf = u[:, 2 * CONV_CH + ATTN_W + KV_W:D_IN]
    kr = pltpu.roll(kf, HEAD_DIM, axis=1)
    vr = pltpu.roll(vf, HEAD_DIM, axis=1)
    lo = lax.broadcasted_iota(jnp.int32, (ts, KV_W), 1) < HEAD_DIM
    rows = slice(WINDOW, WINDOW + ts)
    kbuf[0, rows, :] = jnp.where(lo, kf, 0.0).astype(BF16)
    kbuf[1, rows, :] = jnp.where(lo, 0.0, kr).astype(BF16)
    kbuf[2, rows, :] = jnp.where(lo, kr, 0.0).astype(BF16)
    kbuf[3, rows, :] = jnp.where(lo, 0.0, kf).astype(BF16)
    vbuf[0, rows, :] = vf.astype(BF16)
    vbuf[1, rows, :] = vr.astype(BF16)

    base = CONV_HALO - (CONV_WIDTH - 1)
    n_shift = ts + CONV_HALO - SUBLANES
    for b in range(1, SUBLANES):
        hshift[b - 1, 0:n_shift, :] = hbuf[b:b + n_shift, :]
    rc = CONV_ROWS

    def conv_chunk(c):
        r0 = c * rc
        for l in range(CONV_CH // LANES):
            ls = slice(l * LANES, (l + 1) * LANES)
            acc = jnp.zeros((rc, LANES), F32)
            for j in range(CONV_WIDTH):
                a8, b = divmod(j + base, SUBLANES)
                rs = slice(r0 + SUBLANES * a8, r0 + SUBLANES * a8 + rc)
                tap = hbuf[rs, ls] if b == 0 else hshift[b - 1, rs, ls]
                acc = acc + tap * w_dw_ref[j:j + 1, ls]
            cbuf[r0:r0 + rc, ls] = acc
        for r1 in range(r0, r0 + rc, LN_ROWS):
            rs = slice(r1, r1 + LN_ROWS)
            y = _layer_norm(cbuf[rs, :] + b_dw_ref[...], g_cn_ref[...], b_cn_ref[...])
            y = y * jax.nn.sigmoid(y)
            ymix[rs, 0:CONV_CH] = y.astype(BF16)

    qi = lax.broadcasted_iota(jnp.int32, (2 * WINDOW, WINDOW), 0) % WINDOW
    kj = lax.broadcasted_iota(jnp.int32, (2 * WINDOW, WINDOW), 1)
    own = kj <= qi
    top = lax.broadcasted_iota(jnp.int32, (2 * WINDOW, 1), 0) < WINDOW
    lo_out = lax.broadcasted_iota(jnp.int32, (WINDOW, 2 * HEAD_DIM), 1) < HEAD_DIM

    def attn_block(jb):
        r0 = jb * WINDOW
        prev_ok = jnp.logical_not(own) & (i != 0) if jb == 0 else jnp.logical_not(own)
        for kvh in range(N_KV_HEADS):
            h0 = kvh * GQ
            c0 = h0 * HEAD_DIM
            qs = jnp.concatenate([qbuf[r0:r0 + WINDOW, c0:c0 + 2 * HEAD_DIM],
                                  qbuf[r0:r0 + WINDOW, c0 + 2 * HEAD_DIM:c0 + 4 * HEAD_DIM]], axis=0)
            pv = []
            for par in range(2):
                kk = kbuf[2 * kvh + par, r0:r0 + 2 * WINDOW, :]
                vv = vbuf[(kvh + par) % 2, r0:r0 + 2 * WINDOW, :]
                s2 = _dot_nt(qs, kk)
                prev_part = jnp.where(prev_ok, s2[:, :WINDOW], MASK_VALUE) if jb == 0 else s2[:, :WINDOW]
                s = jnp.where(own, s2[:, WINDOW:], prev_part)
                sink = jnp.where(top, sinks_ref[h0 + par] * LOG2E, sinks_ref[h0 + 2 + par] * LOG2E)
                m = jnp.maximum(jnp.max(s, axis=-1, keepdims=True), sink)
                p = jnp.exp2(s - m)
                denom = jnp.sum(p, axis=-1, keepdims=True) + jnp.exp2(sink - m)
                zero = jnp.zeros((), BF16)
                pb = p.astype(BF16)
                p2 = jnp.concatenate([jnp.where(own, zero, pb), jnp.where(own, pb, zero)], axis=1)
                pv.append(jnp.dot(p2, vv, preferred_element_type=F32) / denom)
            for pair in range(2):
                rs = slice(pair * WINDOW, (pair + 1) * WINDOW)
                o = jnp.where(lo_out, pv[0][rs], pv[1][rs])
                cs = CONV_CH + c0 + pair * 2 * HEAD_DIM
                ymix[r0:r0 + WINDOW, cs:cs + 2 * HEAD_DIM] = o.astype(BF16)

    assert ts // rc == ts // WINDOW
    for c in range(ts // rc):
        attn_block(c)
        conv_chunk(c)

    mix = jnp.dot(ymix[...], w_out_b[...], preferred_element_type=F32)
    o_ref[0] = _layer_norm(ALPHA * x + mix, g1_ref[...], b1_ref[...])

    hbuf[0:CONV_HALO, :] = hbuf[ts:ts + CONV_HALO, :]
    kbuf[:, 0:WINDOW, :] = kbuf[:, ts:ts + WINDOW, :]
    vbuf[:, 0:WINDOW, :] = vbuf[:, ts:ts + WINDOW, :]


def _const_spec(shape):
    nd = len(shape)
    return pl.BlockSpec(shape, lambda *_: (0,) * nd)


def _resident_spec(shape):
    nd = len(shape)
    return pl.BlockSpec(shape, lambda *_: (0,) * nd, pipeline_mode=pl.Buffered(1))


def _mixer(x, sinks, w_in, b_in, w_dw, b_dw, g_cn, b_cn, w_out, g1, b1, w_expert):
    B, S, D = x.shape
    ts = SEQ_TILE
    tile = pl.BlockSpec((1, ts, D), lambda b, i: (b, i, 0))
    nt = S // ts
    n_e, we_rows, we_cols = w_expert.shape
    per = (B * nt) // n_e
    assert per * n_e == B * nt and we_rows % per == 0
    wes = pl.BlockSpec((1, we_rows // per, we_cols), lambda b, i: ((b * nt + i) // per, (b * nt + i) % per, 0))
    return pl.pallas_call(
        _mixer_kernel,
        grid=(B, S // ts),
        in_specs=[
            pl.BlockSpec(memory_space=pltpu.SMEM),
            tile,
            _resident_spec((D, D_IN)), _const_spec((1, D_IN)),
            _const_spec((CONV_HALO, CONV_CH)), _const_spec((1, CONV_CH)),
            _const_spec((1, CONV_CH)), _const_spec((1, CONV_CH)),
            _resident_spec((D_MIX, D)), _const_spec((1, D)), _const_spec((1, D)),
            wes,
        ],
        out_specs=[tile, wes],
        out_shape=[jax.ShapeDtypeStruct((B, S, D), F32), jax.ShapeDtypeStruct(w_expert.shape, BF16)],
        scratch_shapes=[
            pltpu.VMEM((D, D_IN), BF16),
            pltpu.VMEM((D_MIX, D), BF16),
            pltpu.VMEM((CONV_HALO + ts, CONV_CH), F32),
            pltpu.VMEM((SUBLANES - 1, CONV_HALO + ts, CONV_CH), F32),
            pltpu.VMEM((ts, CONV_CH), F32),
            pltpu.VMEM((ts, ATTN_W), BF16),
            pltpu.VMEM((2 * N_KV_HEADS, WINDOW + ts, KV_W), BF16),
            pltpu.VMEM((2, WINDOW + ts, KV_W), BF16),
            pltpu.VMEM((ts, D_MIX), BF16),
        ],
        compiler_params=pltpu.CompilerParams(
            dimension_semantics=("arbitrary", "arbitrary"), vmem_limit_bytes=VMEM_LIMIT),
        name="mixer",
    )(sinks, x, w_in, b_in, w_dw, b_dw, g_cn, b_cn, w_out, g1, b1, w_expert)


def _memkv_kernel(mem_ref, w_ref, o_ref):
    o_ref[...] = jnp.dot(mem_ref[...].astype(BF16), w_ref[...].astype(BF16),
                         preferred_element_type=F32).astype(BF16)


def _memkv(mem2d, w_mkv):
    M, D = mem2d.shape
    N = w_mkv.shape[1]
    tn = MEMKV_COLS
    return pl.pallas_call(
        _memkv_kernel,
        grid=(N // tn,),
        in_specs=[pl.BlockSpec((M, D), lambda j: (0, 0)), pl.BlockSpec((D, tn), lambda j: (0, j))],
        out_specs=pl.BlockSpec((M, tn), lambda j: (0, j)),
        out_shape=jax.ShapeDtypeStruct((M, N), BF16),
        compiler_params=pltpu.CompilerParams(dimension_semantics=("arbitrary",)),
        name="memkv",
    )(mem2d, w_mkv)


def _first_max(rows):
    best = rows[0]
    for r in rows[1:]:
        best = jnp.maximum(best, r)
    idx = jnp.full(best.shape, len(rows) - 1, jnp.int32)
    for k in range(len(rows) - 2, -1, -1):
        idx = jnp.where(rows[k] == best, k, idx)
    return best, idx


def _route_plan(logits_t):
    tile = logits_t.shape[1]
    row = lambda k: logits_t[k:k + 1, :]
    gmax, g_idx = _first_max([row(g) for g in range(N_GROUPS)])
    gsum = jnp.exp(row(0) - gmax)
    for g in range(1, N_GROUPS):
        gsum = gsum + jnp.exp(row(g) - gmax)
    g_p = 1.0 / gsum
    rl = []
    for e in range(EXPERTS_PER_GROUP):
        v = row(ROUTE_OFF + (N_GROUPS - 1) * EXPERTS_PER_GROUP + e)
        for g in range(N_GROUPS - 2, -1, -1):
            v = jnp.where(g_idx == g, row(ROUTE_OFF + g * EXPERTS_PER_GROUP + e), v)
        rl.append(v)
    m1, i1 = _first_max(rl)
    m2, i2 = _first_max([jnp.where(i1 == e, MASK_VALUE, rl[e]) for e in range(EXPERTS_PER_GROUP)])
    ex = jnp.exp(m2 - m1)
    w1 = 1.0 / (1.0 + ex)
    w2 = ex * w1
    e1 = g_idx * EXPERTS_PER_GROUP + i1
    e2 = g_idx * EXPERTS_PER_GROUP + i2

    eid = lax.broadcasted_iota(jnp.int32, (N_EXPERTS, tile), 0)
    hit1 = eid == e1
    hit2 = eid == e2
    oh = jnp.where(jnp.logical_or(hit1, hit2), 1.0, 0.0)
    r = lax.broadcasted_iota(jnp.int32, (tile, tile), 0)
    c = lax.broadcasted_iota(jnp.int32, (tile, tile), 1)
    tri = jnp.where(r <= c, 1.0, 0.0).astype(BF16)
    csum = jnp.dot(oh.astype(BF16), tri, preferred_element_type=F32)
    counts = jnp.broadcast_to(csum[:, tile - 1:tile], (N_EXPERTS, tile)).astype(jnp.int32)
    nch = jnp.right_shift(counts + (CHUNK - 1), CHUNK.bit_length() - 1)
    er = lax.broadcasted_iota(jnp.int32, (N_EXPERTS, N_EXPERTS), 0)
    ec = lax.broadcasted_iota(jnp.int32, (N_EXPERTS, N_EXPERTS), 1)
    lower = jnp.where(ec < er, 1.0, 0.0).astype(BF16)
    off = jnp.dot(lower, nch.astype(F32).astype(BF16), preferred_element_type=F32) * CHUNK
    pos = off + csum - oh
    lp1 = jnp.sum(jnp.where(hit1, pos, 0.0), axis=0, keepdims=True)
    lp2 = jnp.sum(jnp.where(hit2, pos, 0.0), axis=0, keepdims=True)
    zero = jnp.zeros_like(lp1)
    route_t = jnp.concatenate([lp1, lp2, g_p * w1, g_p * w2, zero, zero, zero, zero], axis=0)
    meta = jnp.concatenate([nch[:, 0:LANES], off[:, 0:LANES].astype(jnp.int32)], axis=0)
    return route_t, meta


def _memattn_kernel(x_ref, wq_ref, k_ref, v_ref, wo_ref, g2_ref, b2_ref, wr2_ref, wrh_ref, br_ref, we0_ref,
                    we1_ref, o_ref, ob_ref, route_ref, routet_ref, meta_ref, web0_ref, web1_ref, wq_b, wo_b):
    web0_ref[...] = we0_ref[...].astype(BF16)
    web1_ref[...] = we1_ref[...].astype(BF16)

    @pl.when(jnp.logical_and(pl.program_id(0) == 0, pl.program_id(1) == 0))
    def _():
        _cast_bf16(wq_b, wq_ref)
        _cast_bf16(wo_b, wo_ref)

    def rows_logits(rs):
        x = x_ref[0, rs, :]
        q = jnp.dot(x.astype(BF16), wq_b[...], preferred_element_type=F32)
        q = (q * (MEM_HEAD_DIM ** -0.5)).astype(BF16)
        outs = []
        for h in range(MEM_HEADS):
            sl = slice(h * MEM_HEAD_DIM, (h + 1) * MEM_HEAD_DIM)
            s = _dot_nt(q[:, sl], k_ref[0, :, sl])
            m = jnp.max(s, axis=-1, keepdims=True)
            p = jnp.exp(s - m)
            denom = jnp.sum(p, axis=-1, keepdims=True)
            o = jnp.dot(p.astype(BF16), v_ref[0, :, sl], preferred_element_type=F32)
            outs.append((o / denom).astype(BF16))
        o = jnp.dot(jnp.concatenate(outs, axis=-1), wo_b[...], preferred_element_type=F32)
        x2 = _layer_norm(ALPHA * x + o, g2_ref[...], b2_ref[...])
        o_ref[0, rs, :] = x2
        x2h = x2.astype(BF16)
        ob_ref[0, rs, :] = x2h
        x2l = (x2 - x2h.astype(F32)).astype(BF16)
        hh = jnp.dot(x2h, wr2_ref[...], preferred_element_type=F32)
        return (hh[:, 0:LANES] + hh[:, LANES:2 * LANES]
                + jnp.dot(x2l, wrh_ref[...], preferred_element_type=F32) + br_ref[...])

    ts = x_ref.shape[1]
    groups = [slice(r0, r0 + MOE_TILE) for r0 in range(0, ts, MOE_TILE)]
    logits = rows_logits(slice(0, ts))
    pad = jnp.zeros((LANES - SUBLANES, MOE_TILE), F32)
    for k, rs in enumerate(groups):
        route_t, meta = _route_plan(jnp.transpose(logits[rs, :]))
        routet_ref[k] = route_t
        route_ref[0, rs, :] = jnp.transpose(jnp.concatenate([route_t, pad], axis=0))
        meta_ref[k] = meta


def _memattn(x1, wq, kvm, wo, g2, b2, wr, br, w_experts):
    B, S, D = x1.shape
    ts = MEM_TILE
    assert all(B * (S // ts) == w.shape[0] for w in w_experts) and len(w_experts) == 2
    wes = [pl.BlockSpec((1,) + w.shape[1:], lambda b, i: (b * (S // ts) + i, 0, 0)) for w in w_experts]
    per = ts // MOE_TILE
    nt = S // ts
    c = wr * (2.0 ** 16 + 1.0)
    w_high = c - (c - wr)
    wrh = w_high.astype(BF16)
    wr2 = jnp.concatenate([wrh, (wr - w_high).astype(BF16)], axis=1)
    tile = pl.BlockSpec((1, ts, D), lambda b, i: (b, i, 0))
    kspec = pl.BlockSpec((1, MEM_LEN, D), lambda b, i: (b, 0, 0))
    vspec = pl.BlockSpec((1, MEM_LEN, D), lambda b, i: (b, 0, 1))
    return pl.pallas_call(
        _memattn_kernel,
        grid=(B, nt),
        in_specs=[tile, _resident_spec((D, D)), kspec, vspec, _resident_spec((D, D)),
                  _const_spec((1, D)), _const_spec((1, D)),
                  _const_spec((D, 2 * LANES)), _const_spec((D, LANES)), _const_spec((1, LANES))] + wes,
        out_specs=[tile, tile, pl.BlockSpec((1, ts, LANES), lambda b, i: (b, i, 0)),
                   pl.BlockSpec((per, SUBLANES, MOE_TILE), lambda b, i: (b * nt + i, 0, 0)),
                   pl.BlockSpec((per, 2 * N_EXPERTS, LANES), lambda b, i: (b * nt + i, 0, 0))] + wes,
        out_shape=[jax.ShapeDtypeStruct((B, S, D), F32),
                   jax.ShapeDtypeStruct((B, S, D), BF16),
                   jax.ShapeDtypeStruct((B, S, LANES), F32),
                   jax.ShapeDtypeStruct((B * nt * per, SUBLANES, MOE_TILE), F32),
                   jax.ShapeDtypeStruct((B * nt * per, 2 * N_EXPERTS, LANES), jnp.int32)]
        + [jax.ShapeDtypeStruct(w.shape, BF16) for w in w_experts],
        scratch_shapes=[pltpu.VMEM((D, D), BF16), pltpu.VMEM((D, D), BF16)],
        compiler_params=pltpu.CompilerParams(
            dimension_semantics=("arbitrary", "arbitrary"), vmem_limit_bytes=VMEM_LIMIT),
        name="memattn",
    )(x1, wq, kvm, kvm, wo, g2, b2, wr2, wrh, br, *w_experts)


def _gmm_blocks(n_tokens):
    rows = (n_tokens * TOP_K + (n_tokens // MOE_TILE) * N_EXPERTS * (CHUNK - 1)
            + N_EXPERTS * (REGION_ALIGN - CHUNK))
    return -(-rows // GMM_ROWS)


def _plan(meta, T):
    nch = meta[:, :N_EXPERTS, 0]
    n16 = nch * CHUNK
    n_e = jnp.sum(n16, axis=0)
    reg = (n_e + REGION_ALIGN - 1) // REGION_ALIGN * REGION_ALIGN
    gend = jnp.cumsum(reg)
    gbase = gend - reg
    dst = gbase[None, :] + jnp.cumsum(n16, axis=0) - n16
    half_row = jnp.arange(2 * _gmm_blocks(T), dtype=jnp.int32) * REGION_ALIGN
    half_expert = jnp.minimum(jnp.sum(half_row[:, None] >= gend[None, :], axis=1), N_EXPERTS - 1)
    half_used = half_row < gend[-1]
    n_used = ((gend[-1] + GMM_ROWS - 1) // GMM_ROWS).astype(jnp.int32).reshape(1)
    src = (jnp.cumsum(nch, axis=1) - nch) * CHUNK
    n_big = nch // BIG

    def copy_list(count, src0, dst0, rows, length):
        cum = jnp.cumsum(count, axis=1)
        first = (cum - count)[:, None, :]
        k = jnp.arange(length, dtype=jnp.int32)[None, :, None]
        mine = (k >= first) & (k < cum[:, None, :])
        step = (k - first) * rows
        pick = lambda base: jnp.sum(jnp.where(mine, base[:, None, :] + step, 0), axis=2)
        return pick(src0), pick(dst0), cum[:, -1]

    big_src, big_dst, big_n = copy_list(n_big, src, dst, BIG * CHUNK, MAX_BIG)
    rest = n_big * (BIG * CHUNK)
    small_src, small_dst, small_n = copy_list(nch - n_big * BIG, src + rest, dst + rest, CHUNK, MAX_SMALL)
    i32 = lambda a: a.astype(jnp.int32)
    fill_start = jnp.concatenate([gbase + n_e, gend[-1:]])
    fill_n = jnp.concatenate([reg - n_e, _gmm_blocks(T) * GMM_ROWS - gend[-1:]]) // CHUNK
    return dict(big_src=i32(big_src).reshape(-1), big_dst=i32(big_dst).reshape(-1), big_n=i32(big_n),
                small_src=i32(small_src).reshape(-1), small_dst=i32(small_dst).reshape(-1), small_n=i32(small_n),
                fill_start=i32(fill_start), fill_n=i32(fill_n),
                fill_tot=i32(jnp.sum(fill_n)).reshape(1), half_expert=i32(half_expert),
                half_used=i32(half_used), n_used=n_used)


def _rows_copy(src_ref, src_row, dst_ref, dst_row, rows, sem):
    return pltpu.make_async_copy(
        src_ref.at[pl.ds(pl.multiple_of(src_row, CHUNK), rows), :],
        dst_ref.at[pl.ds(pl.multiple_of(dst_row, CHUNK), rows), :], sem)


def _chunk_copy(src_ref, src_row, dst_ref, dst_row, sem):
    return _rows_copy(src_ref, src_row, dst_ref, dst_row, CHUNK, sem)


def _issue_copies(lists, tile, tile_ref, tile_is_src, hbm_ref, sem):
    big_src, big_dst, big_n, small_src, small_dst, small_n = lists
    for src_l, dst_l, n_l, length, rows in ((big_src, big_dst, big_n, MAX_BIG, BIG * CHUNK),
                                            (small_src, small_dst, small_n, MAX_SMALL, CHUNK)):
        def issue(k, carry, src_l=src_l, dst_l=dst_l, length=length, rows=rows):
            local, remote = src_l[tile * length + k], dst_l[tile * length + k]
            if tile_is_src:
                _rows_copy(tile_ref, local, hbm_ref, remote, rows, sem).start()
            else:
                _rows_copy(hbm_ref, remote, tile_ref, local, rows, sem).start()
            return carry

        lax.fori_loop(0, n_l[tile], issue, 0)


def _wait_rows(n, rows, src_ref, dst_ref, sem):
    def body(c, carry):
        pltpu.make_async_copy(src_ref.at[pl.ds(0, rows), :], dst_ref.at[pl.ds(0, rows), :], sem).wait()
        return carry

    lax.fori_loop(0, n, body, 0)


def _wait_copies(lists, tile, src_ref, dst_ref, sem, extra_chunks=0):
    _wait_rows(lists[2][tile], BIG * CHUNK, src_ref, dst_ref, sem)
    n = lists[5][tile] + extra_chunks
    _wait_rows(n // WAIT_GROUP, WAIT_GROUP * CHUNK, src_ref, dst_ref, sem)
    _wait_rows(n % WAIT_GROUP, CHUNK, src_ref, dst_ref, sem)


def _dispatch_kernel(bs_ref, bd_ref, bn_ref, ss_ref, sd_ref, sn_ref, fstart_ref, fn_ref, ftot_ref,
                     lp_ref, x_ref, xs_hbm, xt, zbuf, sems):
    lists = (bs_ref, bd_ref, bn_ref, ss_ref, sd_ref, sn_ref)
    s = pl.program_id(0)
    ns = pl.num_programs(0)
    cur = (s % 2) * PER_STEP
    prv = PER_STEP - cur

    @pl.when(s >= 2)
    def _():
        for k in range(PER_STEP):
            _wait_copies(lists, (s - 2) * PER_STEP + k, xt.at[cur + k], xs_hbm, sems.at[cur + k])

    r = lax.broadcasted_iota(jnp.int32, (SORT_ROWS, MOE_TILE), 0)
    for k in range(PER_STEP):
        lp = lp_ref[k].astype(jnp.int32)
        hit = jnp.logical_or(lp[0:1, :] == r, lp[1:2, :] == r)
        p = jnp.where(hit, 1.0, 0.0).astype(BF16)
        x = x_ref[k * MOE_TILE:(k + 1) * MOE_TILE, :]
        xt[cur + k] = jnp.dot(p, x, preferred_element_type=F32).astype(BF16)

    for k in range(PER_STEP):
        _issue_copies(lists, s * PER_STEP + k, xt.at[cur + k], True, xs_hbm, sems.at[cur + k])

    @pl.when(s == ns - 1)
    def _():
        zbuf[...] = jnp.zeros_like(zbuf)
        sem = sems.at[cur]

        def per_range(e, carry):
            def issue(c, carry2):
                _chunk_copy(zbuf, 0, xs_hbm, fstart_ref[e] + c * CHUNK, sem).start()
                return carry2

            return lax.fori_loop(0, fn_ref[e], issue, carry)

        lax.fori_loop(0, N_EXPERTS + 1, per_range, 0)
        for k in range(PER_STEP):
            _wait_copies(lists, (s - 1) * PER_STEP + k, xt.at[prv + k], xs_hbm, sems.at[prv + k])
            _wait_copies(lists, s * PER_STEP + k, xt.at[cur + k], xs_hbm, sems.at[cur + k],
                         extra_chunks=ftot_ref[0] if k == 0 else 0)


def _copy_lists(plan):
    return tuple(plan[k] for k in ("big_src", "big_dst", "big_n", "small_src", "small_dst", "small_n"))


def _dispatch(plan, lpt, x2b):
    T, D = x2b.shape
    rows = PER_STEP * MOE_TILE
    assert T // rows >= 2
    grid_spec = pltpu.PrefetchScalarGridSpec(
        num_scalar_prefetch=9,
        grid=(T // rows,),
        in_specs=[pl.BlockSpec((PER_STEP, SUBLANES, MOE_TILE), lambda t, *_: (t, 0, 0)),
                  pl.BlockSpec((rows, D), lambda t, *_: (t, 0))],
        out_specs=pl.BlockSpec(memory_space=pl.ANY),
        scratch_shapes=[pltpu.VMEM((2 * PER_STEP, SORT_ROWS, D), BF16), pltpu.VMEM((CHUNK, D), BF16),
                        pltpu.SemaphoreType.DMA((2 * PER_STEP,))],
    )
    return pl.pallas_call(
        _dispatch_kernel,
        grid_spec=grid_spec,
        out_shape=jax.ShapeDtypeStruct((_gmm_blocks(T) * GMM_ROWS, D), BF16),
        compiler_params=pltpu.CompilerParams(
            dimension_semantics=("arbitrary",), vmem_limit_bytes=VMEM_LIMIT),
        name="moe_dispatch",
    )(*_copy_lists(plan), plan["fill_start"], plan["fill_n"], plan["fill_tot"], lpt, x2b)


def _expert_ffn(xb, wg_b, wu_b, wd_b):
    g = jnp.dot(xb, wg_b[...], preferred_element_type=F32)
    u = jnp.dot(xb, wu_b[...], preferred_element_type=F32)
    h = (g * jax.nn.sigmoid(g)) * u
    return jnp.dot(h.astype(BF16), wd_b[...], preferred_element_type=F32).astype(BF16)


def _gmm_kernel(he_ref, hu_ref, nu_ref, x_ref, wga_ref, wua_ref, wda_ref, wgb_ref, wub_ref, wdb_ref, o_ref):
    b = pl.program_id(0)
    half = REGION_ALIGN
    same = he_ref[2 * b] == he_ref[2 * b + 1]
    used0 = hu_ref[2 * b] != 0
    used1 = hu_ref[2 * b + 1] != 0
    set_a = (wga_ref.at[0], wua_ref.at[0], wda_ref.at[0])
    set_b = (wgb_ref.at[0], wub_ref.at[0], wdb_ref.at[0])

    @pl.when(jnp.logical_and(used1, same))
    def _():
        o_ref[...] = _expert_ffn(x_ref[...], *set_a)

    @pl.when(jnp.logical_and(used0, jnp.logical_not(jnp.logical_and(used1, same))))
    def _():
        o_ref[0:half, :] = _expert_ffn(x_ref[0:half, :], *set_a)

    @pl.when(jnp.logical_and(used1, jnp.logical_not(same)))
    def _():
        o_ref[half:GMM_ROWS, :] = _expert_ffn(x_ref[half:GMM_ROWS, :], *set_b)

    @pl.when(jnp.logical_not(used0))
    def _():
        o_ref[0:half, :] = jnp.zeros((half, o_ref.shape[1]), o_ref.dtype)

    @pl.when(jnp.logical_not(used1))
    def _():
        o_ref[half:GMM_ROWS, :] = jnp.zeros((GMM_ROWS - half, o_ref.shape[1]), o_ref.dtype)


def _gmm(plan, xs, wg, wu, wd):
    R, D = xs.shape
    rows = pl.BlockSpec((GMM_ROWS, D), lambda b, he, hu, nu: (jnp.minimum(b, nu[0] - 1), 0))

    def wspec(shape, h):
        return pl.BlockSpec((1,) + shape, lambda b, he, hu, nu: (he[2 * b + h], 0, 0))

    grid_spec = pltpu.PrefetchScalarGridSpec(
        num_scalar_prefetch=3,
        grid=(R // GMM_ROWS,),
        in_specs=[rows,
                  wspec((D, D_EXPERT), 0), wspec((D, D_EXPERT), 0), wspec((D_EXPERT, D), 0),
                  wspec((D, D_EXPERT), 1), wspec((D, D_EXPERT), 1), wspec((D_EXPERT, D), 1)],
        out_specs=pl.BlockSpec((GMM_ROWS, D), lambda b, he, hu, nu: (b, 0)),
    )
    return pl.pallas_call(
        _gmm_kernel,
        grid_spec=grid_spec,
        out_shape=jax.ShapeDtypeStruct((R, D), BF16),
        compiler_params=pltpu.CompilerParams(
            dimension_semantics=("arbitrary",), vmem_limit_bytes=VMEM_LIMIT),
        name="moe_gmm",
    )(plan["half_expert"], plan["half_used"], plan["n_used"], xs, wg, wu, wd, wg, wu, wd)


def _combine_kernel(bs_ref, bd_ref, bn_ref, ss_ref, sd_ref, sn_ref, cm_ref, x_ref, ys_hbm, g3_ref, b3_ref,
                    o_ref, yt, sems):
    lists = (bs_ref, bd_ref, bn_ref, ss_ref, sd_ref, sn_ref)
    s = pl.program_id(0)
    ns = pl.num_programs(0)
    cur = (s % 2) * PER_STEP
    nxt = PER_STEP - cur

    def fetch(step, first_buf):
        for k in range(PER_STEP):
            _issue_copies(lists, step * PER_STEP + k, yt.at[first_buf + k], False, ys_hbm,
                          sems.at[first_buf + k])

    @pl.when(s == 0)
    def _():
        yt[...] = jnp.zeros_like(yt)
        fetch(0, 0)

    @pl.when(s + 1 < ns)
    def _():
        fetch(s + 1, nxt)

    for k in range(PER_STEP):
        _wait_copies(lists, s * PER_STEP + k, ys_hbm, yt.at[cur + k], sems.at[cur + k])

    col = lax.broadcasted_iota(jnp.int32, (MOE_TILE, SORT_ROWS), 1)
    for k in range(PER_STEP):
        rs = slice(k * MOE_TILE, (k + 1) * MOE_TILE)
        cm = cm_ref[rs, :]
        lp0 = cm[:, 0:1].astype(jnp.int32)
        lp1 = cm[:, 1:2].astype(jnp.int32)
        w = jnp.where(col == lp0, cm[:, 2:3], jnp.where(col == lp1, cm[:, 3:4], 0.0)).astype(BF16)
        y = jnp.dot(w, yt[cur + k], preferred_element_type=F32)
        o_ref[rs, :] = _layer_norm(ALPHA * x_ref[rs, :] + y, g3_ref[...], b3_ref[...])


def _combine(plan, route, x2, ys, g3, b3):
    T, D = x2.shape
    rows = PER_STEP * MOE_TILE
    grid_spec = pltpu.PrefetchScalarGridSpec(
        num_scalar_prefetch=6,
        grid=(T // rows,),
        in_specs=[pl.BlockSpec((rows, LANES), lambda t, *_: (t, 0)),
                  pl.BlockSpec((rows, D), lambda t, *_: (t, 0)),
                  pl.BlockSpec(memory_space=pl.ANY),
                  pl.BlockSpec((1, D), lambda t, *_: (0, 0)),
                  pl.BlockSpec((1, D), lambda t, *_: (0, 0))],
        out_specs=pl.BlockSpec((rows, D), lambda t, *_: (t, 0)),
        scratch_shapes=[pltpu.VMEM((2 * PER_STEP, SORT_ROWS, D), BF16),
                        pltpu.SemaphoreType.DMA((2 * PER_STEP,))],
    )
    return pl.pallas_call(
        _combine_kernel,
        grid_spec=grid_spec,
        out_shape=jax.ShapeDtypeStruct((T, D), F32),
        compiler_params=pltpu.CompilerParams(
            dimension_semantics=("arbitrary",), vmem_limit_bytes=VMEM_LIMIT),
        name="moe_combine",
    )(*_copy_lists(plan), route, x2, ys, g3, b3)


def _moe(x2, x2b, route, lpt, meta, wg_b, wu_b, wd_b, g3, b3):
    plan = _plan(meta, x2.shape[0])
    xs = _dispatch(plan, lpt, x2b)
    ys = _gmm(plan, xs, wg_b, wu_b, wd_b)
    return _combine(plan, route, x2, ys, g3, b3)


def _row(v):
    return v.reshape(1, -1).astype(F32)


def kernel(x, mem, w_in, b_in, w_dw, b_dw, g_conv_norm, b_conv_norm, attn_sinks, w_out, g_ln1, b_ln1,
           w_mq, w_mkv, w_mo, g_ln2, b_ln2, w_group, b_group, w_router, b_router, w_gate, w_up, w_down,
           g_ln3, b_ln3):
    B, S, D = x.shape
    for l in range(DEPTH):
        w_dw_p = jnp.zeros((CONV_HALO, CONV_CH), F32).at[:CONV_WIDTH].set(w_dw[l])
        x1, wd_b = _mixer(x, attn_sinks[l].astype(F32), w_in[l], _row(b_in[l]), w_dw_p,
                          _row(b_dw[l]), _row(g_conv_norm[l]), _row(b_conv_norm[l]),
                          w_out[l], _row(g_ln1[l]), _row(b_ln1[l]), w_down[l])

        kvm = _memkv(mem.reshape(B * MEM_LEN, D), w_mkv[l]).reshape(B, MEM_LEN, 2 * D)

        wr = jnp.concatenate(
            [w_group[l], jnp.transpose(w_router[l], (1, 0, 2)).reshape(D, N_EXPERTS)], axis=1)
        wr = jnp.pad(wr, ((0, 0), (0, LANES - wr.shape[1])))
        br = jnp.pad(jnp.concatenate([b_group[l], b_router[l].reshape(-1)]), (0, LANES - N_GROUPS - N_EXPERTS))
        x2, x2b, route, lpt, meta, wg_b, wu_b = _memattn(x1, w_mq[l], kvm, w_mo[l], _row(g_ln2[l]),
                                                         _row(b_ln2[l]), wr.astype(F32), _row(br),
                                                         [w_gate[l], w_up[l]])

        T = B * S
        y = _moe(x2.reshape(T, D), x2b.reshape(T, D), route.reshape(T, LANES), lpt, meta,
                 wg_b, wu_b, wd_b,
                 _row(g_ln3[l]), _row(b_ln3[l]))
        x = y.reshape(B, S, D)
    return x
```

```python
import jax
import jax.numpy as jnp
from jax import lax
from jax.experimental import pallas as pl
from jax.experimental.pallas import tpu as pltpu

D_MODEL = 1024
MEM_LEN = 256
CONV_CH = 512
CONV_WIDTH = 31
N_HEADS = 8
N_KV_HEADS = 2
HEAD_DIM = 64
GQ = N_HEADS // N_KV_HEADS
ATTN_W = N_HEADS * HEAD_DIM
KV_W = N_KV_HEADS * HEAD_DIM
WINDOW = 128
D_MIX = CONV_CH + ATTN_W
D_IN = 2 * CONV_CH + ATTN_W + 2 * KV_W
MEM_HEADS = 4
MEM_HEAD_DIM = D_MODEL // MEM_HEADS
N_GROUPS = 4
EXPERTS_PER_GROUP = 4
N_EXPERTS = N_GROUPS * EXPERTS_PER_GROUP
D_EXPERT = D_MODEL // 2
DEPTH = 1
ALPHA = (2.0 * DEPTH) ** 0.25
LN_EPS = 1e-5

LANES = 128
SUBLANES = 8
CONV_ROWS = 128
LN_ROWS = 64
MEMKV_COLS = 1024
MEM_TILE = 1024
LOG2E = 1.4426950408889634
MASK_VALUE = -1e30
CONV_HALO = 32
SEQ_TILE = 512
MOE_TILE = 512
CHUNK = 16
REGION_ALIGN = 512
GMM_ROWS = 2 * REGION_ALIGN
TOP_K = 2
SORT_ROWS = -(-(MOE_TILE * TOP_K + N_EXPERTS * (CHUNK - 1)) // 256) * 256
BIG = 4
MAX_BIG = SORT_ROWS // (BIG * CHUNK)
MAX_SMALL = N_EXPERTS * (BIG - 1)
PER_STEP = 2
WAIT_GROUP = 8
ROUTE_OFF = N_GROUPS
VMEM_LIMIT = 56 * 1024 * 1024

BF16 = jnp.bfloat16
F32 = jnp.float32


def _layer_norm(x, g, b):
    mu = jnp.mean(x, axis=-1, keepdims=True)
    xc = x - mu
    var = jnp.mean(xc * xc, axis=-1, keepdims=True)
    return xc * lax.rsqrt(var + LN_EPS) * g + b


def _cast_bf16(dst_ref, src_ref):
    rows = 256
    for r0 in range(0, src_ref.shape[0], rows):
        dst_ref[r0:r0 + rows, :] = src_ref[r0:r0 + rows, :].astype(BF16)


def _dot_nt(a, b):
    return lax.dot_general(a, b, (((1,), (1,)), ((), ())), preferred_element_type=F32)


def _mixer_kernel(sinks_ref, x_ref, w_in_ref, b_in_ref, w_dw_ref, b_dw_ref, g_cn_ref, b_cn_ref,
                  w_out_ref, g1_ref, b1_ref, we_ref, o_ref, web_ref, w_in_b, w_out_b, hbuf, hshift, cbuf, qbuf,
                  kbuf, vbuf, ymix):
    i = pl.program_id(1)
    ts = SEQ_TILE

    @pl.when(jnp.logical_and(pl.program_id(0) == 0, i == 0))
    def _():
        _cast_bf16(w_in_b, w_in_ref)
        _cast_bf16(w_out_b, w_out_ref)

    @pl.when(i == 0)
    def _():
        hbuf[0:CONV_HALO, :] = jnp.zeros((CONV_HALO, CONV_CH), F32)
        kbuf[:, 0:WINDOW, :] = jnp.zeros((2 * N_KV_HEADS, WINDOW, KV_W), BF16)
        vbuf[:, 0:WINDOW, :] = jnp.zeros((2, WINDOW, KV_W), BF16)

    web_ref[...] = we_ref[...].astype(BF16)

    x = x_ref[0]
    u = jnp.dot(x.astype(BF16), w_in_b[...], preferred_element_type=F32) + b_in_ref[...]
    a = u[:, 0:CONV_CH]
    gate = u[:, CONV_CH:2 * CONV_CH]
    hbuf[CONV_HALO:CONV_HALO + ts, :] = a * jax.nn.sigmoid(gate)
    qbuf[...] = (u[:, 2 * CONV_CH:2 * CONV_CH + ATTN_W] * (HEAD_DIM ** -0.5 * LOG2E)).astype(BF16)
    kf = u[:, 2 * CONV_CH + ATTN_W:2 * CONV_CH + ATTN_W + KV_W]
    vf = u[:, 2 * CONV_CH + ATTN_W + KV_W:D_IN]
    kr = pltpu.roll(kf, HEAD_DIM, axis=1)
    vr = pltpu.roll(vf, HEAD_DIM, axis=1)
    lo = lax.broadcasted_iota(jnp.int32, (ts, KV_W), 1) < HEAD_DIM
    rows = slice(WINDOW, WINDOW + ts)
    kbuf[0, rows, :] = jnp.where(lo, kf, 0.0).astype(BF16)
    kbuf[1, rows, :] = jnp.where(lo, 0.0, kr).astype(BF16)
    kbuf[2, rows, :] = jnp.where(lo, kr, 0.0).astype(BF16)
    kbuf[3, rows, :] = jnp.where(lo, 0.0, kf).astype(BF16)
    vbuf[0, rows, :] = vf.astype(BF16)
    vbuf[1, rows, :] = vr.astype(BF16)

    base = CONV_HALO - (CONV_WIDTH - 1)
    n_shift = ts + CONV_HALO - SUBLANES
    for b in range(1, SUBLANES):
        hshift[b - 1, 0:n_shift, :] = hbuf[b:b + n_shift, :]
    rc = CONV_ROWS

    def conv_chunk(c):
        r0 = c * rc
        for l in range(CONV_CH // LANES):
            ls = slice(l * LANES, (l + 1) * LANES)
            acc = jnp.zeros((rc, LANES), F32)
            for j in range(CONV_WIDTH):
                a8, b = divmod(j + base, SUBLANES)
                rs = slice(r0 + SUBLANES * a8, r0 + SUBLANES * a8 + rc)
                tap = hbuf[rs, ls] if b == 0 else hshift[b - 1, rs, ls]
                acc = acc + tap * w_dw_ref[j:j + 1, ls]
            cbuf[r0:r0 + rc, ls] = acc
        for r1 in range(r0, r0 + rc, LN_ROWS):
            rs = slice(r1, r1 + LN_ROWS)
            y = _layer_norm(cbuf[rs, :] + b_dw_ref[...], g_cn_ref[...], b_cn_ref[...])
            y = y * jax.nn.sigmoid(y)
            ymix[rs, 0:CONV_CH] = y.astype(BF16)

    qi = lax.broadcasted_iota(jnp.int32, (2 * WINDOW, WINDOW), 0) % WINDOW
    kj = lax.broadcasted_iota(jnp.int32, (2 * WINDOW, WINDOW), 1)
    own = kj <= qi
    top = lax.broadcasted_iota(jnp.int32, (2 * WINDOW, 1), 0) < WINDOW
    lo_out = lax.broadcasted_iota(jnp.int32, (WINDOW, 2 * HEAD_DIM), 1) < HEAD_DIM

    def attn_block(jb):
        r0 = jb * WINDOW
        prev_ok = jnp.logical_not(own) & (i != 0) if jb == 0 else jnp.logical_not(own)
        for kvh in range(N_KV_HEADS):
            h0 = kvh * GQ
            c0 = h0 * HEAD_DIM
            qs = jnp.concatenate([qbuf[r0:r0 + WINDOW, c0:c0 + 2 * HEAD_DIM],
                                  qbuf[r0:r0 + WINDOW, c0 + 2 * HEAD_DIM:c0 + 4 * HEAD_DIM]], axis=0)
            pv = []
            for par in range(2):
                kk = kbuf[2 * kvh + par, r0:r0 + 2 * WINDOW, :]
                vv = vbuf[(kvh + par) % 2, r0:r0 + 2 * WINDOW, :]
                s2 = _dot_nt(qs, kk)
                prev_part = jnp.where(prev_ok, s2[:, :WINDOW], MASK_VALUE) if jb == 0 else s2[:, :WINDOW]
                s = jnp.where(own, s2[:, WINDOW:], prev_part)
                sink = jnp.where(top, sinks_ref[h0 + par] * LOG2E, sinks_ref[h0 + 2 + par] * LOG2E)
                m = jnp.maximum(jnp.max(s, axis=-1, keepdims=True), sink)
                p = jnp.exp2(s - m)
                denom = jnp.sum(p, axis=-1, keepdims=True) + jnp.exp2(sink - m)
                zero = jnp.zeros((), BF16)
                pb = p.astype(BF16)
                p2 = jnp.concatenate([jnp.where(own, zero, pb), jnp.where(own, pb, zero)], axis=1)
                pv.append(jnp.dot(p2, vv, preferred_element_type=F32) / denom)
            for pair in range(2):
                rs = slice(pair * WINDOW, (pair + 1) * WINDOW)
                o = jnp.where(lo_out, pv[0][rs], pv[1][rs])
                cs = CONV_CH + c0 + pair * 2 * HEAD_DIM
                ymix[r0:r0 + WINDOW, cs:cs + 2 * HEAD_DIM] = o.astype(BF16)

    assert ts // rc == ts // WINDOW
    for c in range(ts // rc):
        attn_block(c)
        conv_chunk(c)

    mix = jnp.dot(ymix[...], w_out_b[...], preferred_element_type=F32)
    o_ref[0] = _layer_norm(ALPHA * x + mix, g1_ref[...], b1_ref[...])

    hbuf[0:CONV_HALO, :] = hbuf[ts:ts + CONV_HALO, :]
    kbuf[:, 0:WINDOW, :] = kbuf[:, ts:ts + WINDOW, :]
    vbuf[:, 0:WINDOW, :] = vbuf[:, ts:ts + WINDOW, :]


def _const_spec(shape):
    nd = len(shape)
    return pl.BlockSpec(shape, lambda *_: (0,) * nd)


def _resident_spec(shape):
    nd = len(shape)
    return pl.BlockSpec(shape, lambda *_: (0,) * nd, pipeline_mode=pl.Buffered(1))


def _mixer(x, sinks, w_in, b_in, w_dw, b_dw, g_cn, b_cn, w_out, g1, b1, w_expert):
    B, S, D = x.shape
    ts = SEQ_TILE
    tile = pl.BlockSpec((1, ts, D), lambda b, i: (b, i, 0))
    nt = S // ts
    n_e, we_rows, we_cols = w_expert.shape
    per = (B * nt) // n_e
    assert per * n_e == B * nt and we_rows % per == 0
    wes = pl.BlockSpec((1, we_rows // per, we_cols), lambda b, i: ((b * nt + i) // per, (b * nt + i) % per, 0))
    return pl.pallas_call(
        _mixer_kernel,
        grid=(B, S // ts),
        in_specs=[
            pl.BlockSpec(memory_space=pltpu.SMEM),
            tile,
            _resident_spec((D, D_IN)), _const_spec((1, D_IN)),
            _const_spec((CONV_HALO, CONV_CH)), _const_spec((1, CONV_CH)),
            _const_spec((1, CONV_CH)), _const_spec((1, CONV_CH)),
            _resident_spec((D_MIX, D)), _const_spec((1, D)), _const_spec((1, D)),
            wes,
        ],
        out_specs=[tile, wes],
        out_shape=[jax.ShapeDtypeStruct((B, S, D), F32), jax.ShapeDtypeStruct(w_expert.shape, BF16)],
        scratch_shapes=[
            pltpu.VMEM((D, D_IN), BF16),
            pltpu.VMEM((D_MIX, D), BF16),
            pltpu.VMEM((CONV_HALO + ts, CONV_CH), F32),
            pltpu.VMEM((SUBLANES - 1, CONV_HALO + ts, CONV_CH), F32),
            pltpu.VMEM((ts, CONV_CH), F32),
            pltpu.VMEM((ts, ATTN_W), BF16),
            pltpu.VMEM((2 * N_KV_HEADS, WINDOW + ts, KV_W), BF16),
            pltpu.VMEM((2, WINDOW + ts, KV_W), BF16),
            pltpu.VMEM((ts, D_MIX), BF16),
        ],
        compiler_params=pltpu.CompilerParams(
            dimension_semantics=("arbitrary", "arbitrary"), vmem_limit_bytes=VMEM_LIMIT),
        name="mixer",
    )(sinks, x, w_in, b_in, w_dw, b_dw, g_cn, b_cn, w_out, g1, b1, w_expert)


def _memkv_kernel(mem_ref, w_ref, o_ref):
    o_ref[...] = jnp.dot(mem_ref[...].astype(BF16), w_ref[...].astype(BF16),
                         preferred_element_type=F32).astype(BF16)


def _memkv(mem2d, w_mkv):
    M, D = mem2d.shape
    N = w_mkv.shape[1]
    tn = MEMKV_COLS
    return pl.pallas_call(
        _memkv_kernel,
        grid=(N // tn,),
        in_specs=[pl.BlockSpec((M, D), lambda j: (0, 0)), pl.BlockSpec((D, tn), lambda j: (0, j))],
        out_specs=pl.BlockSpec((M, tn), lambda j: (0, j)),
        out_shape=jax.ShapeDtypeStruct((M, N), BF16),
        compiler_params=pltpu.CompilerParams(dimension_semantics=("arbitrary",)),
        name="memkv",
    )(mem2d, w_mkv)


def _first_max(rows):
    best = rows[0]
    for r in rows[1:]:
        best = jnp.maximum(best, r)
    idx = jnp.full(best.shape, len(rows) - 1, jnp.int32)
    for k in range(len(rows) - 2, -1, -1):
        idx = jnp.where(rows[k] == best, k, idx)
    return best, idx


def _route_plan(logits_t):
    tile = logits_t.shape[1]
    row = lambda k: logits_t[k:k + 1, :]
    gmax, g_idx = _first_max([row(g) for g in range(N_GROUPS)])
    gsum = jnp.exp(row(0) - gmax)
    for g in range(1, N_GROUPS):
        gsum = gsum + jnp.exp(row(g) - gmax)
    g_p = 1.0 / gsum
    rl = []
    for e in range(EXPERTS_PER_GROUP):
        v = row(ROUTE_OFF + (N_GROUPS - 1) * EXPERTS_PER_GROUP + e)
        for g in range(N_GROUPS - 2, -1, -1):
            v = jnp.where(g_idx == g, row(ROUTE_OFF + g * EXPERTS_PER_GROUP + e), v)
        rl.append(v)
    m1, i1 = _first_max(rl)
    m2, i2 = _first_max([jnp.where(i1 == e, MASK_VALUE, rl[e]) for e in range(EXPERTS_PER_GROUP)])
    ex = jnp.exp(m2 - m1)
    w1 = 1.0 / (1.0 + ex)
    w2 = ex * w1
    e1 = g_idx * EXPERTS_PER_GROUP + i1
    e2 = g_idx * EXPERTS_PER_GROUP + i2

    eid = lax.broadcasted_iota(jnp.int32, (N_EXPERTS, tile), 0)
    hit1 = eid == e1
    hit2 = eid == e2
    oh = jnp.where(jnp.logical_or(hit1, hit2), 1.0, 0.0)
    r = lax.broadcasted_iota(jnp.int32, (tile, tile), 0)
    c = lax.broadcasted_iota(jnp.int32, (tile, tile), 1)
    tri = jnp.where(r <= c, 1.0, 0.0).astype(BF16)
    csum = jnp.dot(oh.astype(BF16), tri, preferred_element_type=F32)
    counts = jnp.broadcast_to(csum[:, tile - 1:tile], (N_EXPERTS, tile)).astype(jnp.int32)
    nch = jnp.right_shift(counts + (CHUNK - 1), CHUNK.bit_length() - 1)
    er = lax.broadcasted_iota(jnp.int32, (N_EXPERTS, N_EXPERTS), 0)
    ec = lax.broadcasted_iota(jnp.int32, (N_EXPERTS, N_EXPERTS), 1)
    lower = jnp.where(ec < er, 1.0, 0.0).astype(BF16)
    off = jnp.dot(lower, nch.astype(F32).astype(BF16), preferred_element_type=F32) * CHUNK
    pos = off + csum - oh
    lp1 = jnp.sum(jnp.where(hit1, pos, 0.0), axis=0, keepdims=True)
    lp2 = jnp.sum(jnp.where(hit2, pos, 0.0), axis=0, keepdims=True)
    zero = jnp.zeros_like(lp1)
    route_t = jnp.concatenate([lp1, lp2, g_p * w1, g_p * w2, zero, zero, zero, zero], axis=0)
    meta = jnp.concatenate([nch[:, 0:LANES], off[:, 0:LANES].astype(jnp.int32)], axis=0)
    return route_t, meta


def _memattn_kernel(x_ref, wq_ref, k_ref, v_ref, wo_ref, g2_ref, b2_ref, wr2_ref, wrh_ref, br_ref, we_ref,
                    o_ref, ob_ref, route_ref, routet_ref, meta_ref, web_ref, wq_b, wo_b):
    @pl.when(jnp.logical_and(pl.program_id(0) == 0, pl.program_id(1) == 0))
    def _():
        _cast_bf16(wq_b, wq_ref)
        _cast_bf16(wo_b, wo_ref)

    def rows_logits(rs):
        x = x_ref[0, rs, :]
        q = jnp.dot(x.astype(BF16), wq_b[...], preferred_element_type=F32)
        q = (q * (MEM_HEAD_DIM ** -0.5)).astype(BF16)
        outs = []
        for h in range(MEM_HEADS):
            sl = slice(h * MEM_HEAD_DIM, (h + 1) * MEM_HEAD_DIM)
            s = _dot_nt(q[:, sl], k_ref[0, :, sl])
            m = jnp.max(s, axis=-1, keepdims=True)
            p = jnp.exp(s - m)
            denom = jnp.sum(p, axis=-1, keepdims=True)
            o = jnp.dot(p.astype(BF16), v_ref[0, :, sl], preferred_element_type=F32)
            outs.append((o / denom).astype(BF16))
        o = jnp.dot(jnp.concatenate(outs, axis=-1), wo_b[...], preferred_element_type=F32)
        x2 = _layer_norm(ALPHA * x + o, g2_ref[...], b2_ref[...])
        o_ref[0, rs, :] = x2
        x2h = x2.astype(BF16)
        ob_ref[0, rs, :] = x2h
        x2l = (x2 - x2h.astype(F32)).astype(BF16)
        hh = jnp.dot(x2h, wr2_ref[...], preferred_element_type=F32)
        return (hh[:, 0:LANES] + hh[:, LANES:2 * LANES]
                + jnp.dot(x2l, wrh_ref[...], preferred_element_type=F32) + br_ref[...])

    ts = x_ref.shape[1]
    web_ref[...] = we_ref[...].astype(BF16)
    groups = [slice(r0, r0 + MOE_TILE) for r0 in range(0, ts, MOE_TILE)]
    logits = rows_logits(slice(0, ts))
    pad = jnp.zeros((LANES - SUBLANES, MOE_TILE), F32)
    for k, rs in enumerate(groups):
        route_t, meta = _route_plan(jnp.transpose(logits[rs, :]))
        routet_ref[k] = route_t
        route_ref[0, rs, :] = jnp.transpose(jnp.concatenate([route_t, pad], axis=0))
        meta_ref[k] = meta


def _memattn(x1, wq, kvm, wo, g2, b2, wr, br, w_expert):
    B, S, D = x1.shape
    ts = MEM_TILE
    assert B * (S // ts) == w_expert.shape[0]
    wes = pl.BlockSpec((1,) + w_expert.shape[1:], lambda b, i: (b * (S // ts) + i, 0, 0))
    per = ts // MOE_TILE
    nt = S // ts
    c = wr * (2.0 ** 16 + 1.0)
    w_high = c - (c - wr)
    wrh = w_high.astype(BF16)
    wr2 = jnp.concatenate([wrh, (wr - w_high).astype(BF16)], axis=1)
    tile = pl.BlockSpec((1, ts, D), lambda b, i: (b, i, 0))
    kspec = pl.BlockSpec((1, MEM_LEN, D), lambda b, i: (b, 0, 0))
    vspec = pl.BlockSpec((1, MEM_LEN, D), lambda b, i: (b, 0, 1))
    return pl.pallas_call(
        _memattn_kernel,
        grid=(B, nt),
        in_specs=[tile, _resident_spec((D, D)), kspec, vspec, _resident_spec((D, D)),
                  _const_spec((1, D)), _const_spec((1, D)),
                  _const_spec((D, 2 * LANES)), _const_spec((D, LANES)), _const_spec((1, LANES)), wes],
        out_specs=[tile, tile, pl.BlockSpec((1, ts, LANES), lambda b, i: (b, i, 0)),
                   pl.BlockSpec((per, SUBLANES, MOE_TILE), lambda b, i: (b * nt + i, 0, 0)),
                   pl.BlockSpec((per, 2 * N_EXPERTS, LANES), lambda b, i: (b * nt + i, 0, 0)), wes],
        out_shape=[jax.ShapeDtypeStruct((B, S, D), F32),
                   jax.ShapeDtypeStruct((B, S, D), BF16),
                   jax.ShapeDtypeStruct((B, S, LANES), F32),
                   jax.ShapeDtypeStruct((B * nt * per, SUBLANES, MOE_TILE), F32),
                   jax.ShapeDtypeStruct((B * nt * per, 2 * N_EXPERTS, LANES), jnp.int32),
                   jax.ShapeDtypeStruct(w_expert.shape, BF16)],
        scratch_shapes=[pltpu.VMEM((D, D), BF16), pltpu.VMEM((D, D), BF16)],
        compiler_params=pltpu.CompilerParams(
            dimension_semantics=("arbitrary", "arbitrary"), vmem_limit_bytes=VMEM_LIMIT),
        name="memattn",
    )(x1, wq, kvm, kvm, wo, g2, b2, wr2, wrh, br, w_expert)


def _gmm_blocks(n_tokens):
    rows = (n_tokens * TOP_K + (n_tokens // MOE_TILE) * N_EXPERTS * (CHUNK - 1)
            + N_EXPERTS * (REGION_ALIGN - CHUNK))
    return -(-rows // GMM_ROWS)


def _plan(meta, T):
    nch = meta[:, :N_EXPERTS, 0]
    n16 = nch * CHUNK
    n_e = jnp.sum(n16, axis=0)
    reg = (n_e + REGION_ALIGN - 1) // REGION_ALIGN * REGION_ALIGN
    gend = jnp.cumsum(reg)
    gbase = gend - reg
    dst = gbase[None, :] + jnp.cumsum(n16, axis=0) - n16
    half_row = jnp.arange(2 * _gmm_blocks(T), dtype=jnp.int32) * REGION_ALIGN
    half_expert = jnp.minimum(jnp.sum(half_row[:, None] >= gend[None, :], axis=1), N_EXPERTS - 1)
    half_used = half_row < gend[-1]
    n_used = ((gend[-1] + GMM_ROWS - 1) // GMM_ROWS).astype(jnp.int32).reshape(1)
    src = (jnp.cumsum(nch, axis=1) - nch) * CHUNK
    n_big = nch // BIG

    def copy_list(count, src0, dst0, rows, length):
        cum = jnp.cumsum(count, axis=1)
        first = (cum - count)[:, None, :]
        k = jnp.arange(length, dtype=jnp.int32)[None, :, None]
        mine = (k >= first) & (k < cum[:, None, :])
        step = (k - first) * rows
        pick = lambda base: jnp.sum(jnp.where(mine, base[:, None, :] + step, 0), axis=2)
        return pick(src0), pick(dst0), cum[:, -1]

    big_src, big_dst, big_n = copy_list(n_big, src, dst, BIG * CHUNK, MAX_BIG)
    rest = n_big * (BIG * CHUNK)
    small_src, small_dst, small_n = copy_list(nch - n_big * BIG, src + rest, dst + rest, CHUNK, MAX_SMALL)
    i32 = lambda a: a.astype(jnp.int32)
    fill_start = jnp.concatenate([gbase + n_e, gend[-1:]])
    fill_n = jnp.concatenate([reg - n_e, _gmm_blocks(T) * GMM_ROWS - gend[-1:]]) // CHUNK
    return dict(big_src=i32(big_src).reshape(-1), big_dst=i32(big_dst).reshape(-1), big_n=i32(big_n),
                small_src=i32(small_src).reshape(-1), small_dst=i32(small_dst).reshape(-1), small_n=i32(small_n),
                fill_start=i32(fill_start), fill_n=i32(fill_n),
                fill_tot=i32(jnp.sum(fill_n)).reshape(1), half_expert=i32(half_expert),
                half_used=i32(half_used), n_used=n_used)


def _rows_copy(src_ref, src_row, dst_ref, dst_row, rows, sem):
    return pltpu.make_async_copy(
        src_ref.at[pl.ds(pl.multiple_of(src_row, CHUNK), rows), :],
        dst_ref.at[pl.ds(pl.multiple_of(dst_row, CHUNK), rows), :], sem)


def _chunk_copy(src_ref, src_row, dst_ref, dst_row, sem):
    return _rows_copy(src_ref, src_row, dst_ref, dst_row, CHUNK, sem)


def _issue_copies(lists, tile, tile_ref, tile_is_src, hbm_ref, sem):
    big_src, big_dst, big_n, small_src, small_dst, small_n = lists
    for src_l, dst_l, n_l, length, rows in ((big_src, big_dst, big_n, MAX_BIG, BIG * CHUNK),
                                            (small_src, small_dst, small_n, MAX_SMALL, CHUNK)):
        def issue(k, carry, src_l=src_l, dst_l=dst_l, length=length, rows=rows):
            local, remote = src_l[tile * length + k], dst_l[tile * length + k]
            if tile_is_src:
                _rows_copy(tile_ref, local, hbm_ref, remote, rows, sem).start()
            else:
                _rows_copy(hbm_ref, remote, tile_ref, local, rows, sem).start()
            return carry

        lax.fori_loop(0, n_l[tile], issue, 0)


def _wait_rows(n, rows, src_ref, dst_ref, sem):
    def body(c, carry):
        pltpu.make_async_copy(src_ref.at[pl.ds(0, rows), :], dst_ref.at[pl.ds(0, rows), :], sem).wait()
        return carry

    lax.fori_loop(0, n, body, 0)


def _wait_copies(lists, tile, src_ref, dst_ref, sem, extra_chunks=0):
    _wait_rows(lists[2][tile], BIG * CHUNK, src_ref, dst_ref, sem)
    n = lists[5][tile] + extra_chunks
    _wait_rows(n // WAIT_GROUP, WAIT_GROUP * CHUNK, src_ref, dst_ref, sem)
    _wait_rows(n % WAIT_GROUP, CHUNK, src_ref, dst_ref, sem)


def _dispatch_kernel(bs_ref, bd_ref, bn_ref, ss_ref, sd_ref, sn_ref, fstart_ref, fn_ref, ftot_ref,
                     lp_ref, x_ref, we_ref, xs_hbm, web_ref, xt, zbuf, sems):
    lists = (bs_ref, bd_ref, bn_ref, ss_ref, sd_ref, sn_ref)
    s = pl.program_id(0)
    ns = pl.num_programs(0)
    cur = (s % 2) * PER_STEP
    prv = PER_STEP - cur

    @pl.when(s >= 2)
    def _():
        for k in range(PER_STEP):
            _wait_copies(lists, (s - 2) * PER_STEP + k, xt.at[cur + k], xs_hbm, sems.at[cur + k])

    r = lax.broadcasted_iota(jnp.int32, (SORT_ROWS, MOE_TILE), 0)
    for k in range(PER_STEP):
        lp = lp_ref[k].astype(jnp.int32)
        hit = jnp.logical_or(lp[0:1, :] == r, lp[1:2, :] == r)
        p = jnp.where(hit, 1.0, 0.0).astype(BF16)
        x = x_ref[k * MOE_TILE:(k + 1) * MOE_TILE, :]
        xt[cur + k] = jnp.dot(p, x, preferred_element_type=F32).astype(BF16)

    _cast_bf16(web_ref.at[0], we_ref.at[0])

    for k in range(PER_STEP):
        _issue_copies(lists, s * PER_STEP + k, xt.at[cur + k], True, xs_hbm, sems.at[cur + k])

    @pl.when(s == ns - 1)
    def _():
        zbuf[...] = jnp.zeros_like(zbuf)
        sem = sems.at[cur]

        def per_range(e, carry):
            def issue(c, carry2):
                _chunk_copy(zbuf, 0, xs_hbm, fstart_ref[e] + c * CHUNK, sem).start()
                return carry2

            return lax.fori_loop(0, fn_ref[e], issue, carry)

        lax.fori_loop(0, N_EXPERTS + 1, per_range, 0)
        for k in range(PER_STEP):
            _wait_copies(lists, (s - 1) * PER_STEP + k, xt.at[prv + k], xs_hbm, sems.at[prv + k])
            _wait_copies(lists, s * PER_STEP + k, xt.at[cur + k], xs_hbm, sems.at[cur + k],
                         extra_chunks=ftot_ref[0] if k == 0 else 0)


def _copy_lists(plan):
    return tuple(plan[k] for k in ("big_src", "big_dst", "big_n", "small_src", "small_dst", "small_n"))


def _dispatch(plan, lpt, x2b, w_expert):
    T, D = x2b.shape
    rows = PER_STEP * MOE_TILE
    assert T // rows >= 2
    assert T // rows == N_EXPERTS
    wes = pl.BlockSpec((1,) + w_expert.shape[1:], lambda t, *_: (t, 0, 0))
    grid_spec = pltpu.PrefetchScalarGridSpec(
        num_scalar_prefetch=9,
        grid=(T // rows,),
        in_specs=[pl.BlockSpec((PER_STEP, SUBLANES, MOE_TILE), lambda t, *_: (t, 0, 0)),
                  pl.BlockSpec((rows, D), lambda t, *_: (t, 0)), wes],
        out_specs=[pl.BlockSpec(memory_space=pl.ANY), wes],
        scratch_shapes=[pltpu.VMEM((2 * PER_STEP, SORT_ROWS, D), BF16), pltpu.VMEM((CHUNK, D), BF16),
                        pltpu.SemaphoreType.DMA((2 * PER_STEP,))],
    )
    return pl.pallas_call(
        _dispatch_kernel,
        grid_spec=grid_spec,
        out_shape=[jax.ShapeDtypeStruct((_gmm_blocks(T) * GMM_ROWS, D), BF16),
                   jax.ShapeDtypeStruct(w_expert.shape, BF16)],
        compiler_params=pltpu.CompilerParams(
            dimension_semantics=("arbitrary",), vmem_limit_bytes=VMEM_LIMIT),
        name="moe_dispatch",
    )(*_copy_lists(plan), plan["fill_start"], plan["fill_n"], plan["fill_tot"], lpt, x2b, w_expert)


def _expert_ffn(xb, wg_b, wu_b, wd_b):
    g = jnp.dot(xb, wg_b[...], preferred_element_type=F32)
    u = jnp.dot(xb, wu_b[...], preferred_element_type=F32)
    h = (g * jax.nn.sigmoid(g)) * u
    return jnp.dot(h.astype(BF16), wd_b[...], preferred_element_type=F32).astype(BF16)


def _gmm_kernel(he_ref, hu_ref, nu_ref, x_ref, wga_ref, wua_ref, wda_ref, wgb_ref, wub_ref, wdb_ref, o_ref):
    b = pl.program_id(0)
    half = REGION_ALIGN
    same = he_ref[2 * b] == he_ref[2 * b + 1]
    used0 = hu_ref[2 * b] != 0
    used1 = hu_ref[2 * b + 1] != 0
    set_a = (wga_ref.at[0], wua_ref.at[0], wda_ref.at[0])
    set_b = (wgb_ref.at[0], wub_ref.at[0], wdb_ref.at[0])

    @pl.when(jnp.logical_and(used1, same))
    def _():
        o_ref[...] = _expert_ffn(x_ref[...], *set_a)

    @pl.when(jnp.logical_and(used0, jnp.logical_not(jnp.logical_and(used1, same))))
    def _():
        o_ref[0:half, :] = _expert_ffn(x_ref[0:half, :], *set_a)

    @pl.when(jnp.logical_and(used1, jnp.logical_not(same)))
    def _():
        o_ref[half:GMM_ROWS, :] = _expert_ffn(x_ref[half:GMM_ROWS, :], *set_b)

    @pl.when(jnp.logical_not(used0))
    def _():
        o_ref[0:half, :] = jnp.zeros((half, o_ref.shape[1]), o_ref.dtype)

    @pl.when(jnp.logical_not(used1))
    def _():
        o_ref[half:GMM_ROWS, :] = jnp.zeros((GMM_ROWS - half, o_ref.shape[1]), o_ref.dtype)


def _gmm(plan, xs, wg, wu, wd):
    R, D = xs.shape
    rows = pl.BlockSpec((GMM_ROWS, D), lambda b, he, hu, nu: (jnp.minimum(b, nu[0] - 1), 0))

    def wspec(shape, h):
        return pl.BlockSpec((1,) + shape, lambda b, he, hu, nu: (he[2 * b + h], 0, 0))

    grid_spec = pltpu.PrefetchScalarGridSpec(
        num_scalar_prefetch=3,
        grid=(R // GMM_ROWS,),
        in_specs=[rows,
                  wspec((D, D_EXPERT), 0), wspec((D, D_EXPERT), 0), wspec((D_EXPERT, D), 0),
                  wspec((D, D_EXPERT), 1), wspec((D, D_EXPERT), 1), wspec((D_EXPERT, D), 1)],
        out_specs=pl.BlockSpec((GMM_ROWS, D), lambda b, he, hu, nu: (b, 0)),
    )
    return pl.pallas_call(
        _gmm_kernel,
        grid_spec=grid_spec,
        out_shape=jax.ShapeDtypeStruct((R, D), BF16),
        compiler_params=pltpu.CompilerParams(
            dimension_semantics=("arbitrary",), vmem_limit_bytes=VMEM_LIMIT),
        name="moe_gmm",
    )(plan["half_expert"], plan["half_used"], plan["n_used"], xs, wg, wu, wd, wg, wu, wd)


def _combine_kernel(bs_ref, bd_ref, bn_ref, ss_ref, sd_ref, sn_ref, cm_ref, x_ref, ys_hbm, g3_ref, b3_ref,
                    o_ref, yt, sems):
    lists = (bs_ref, bd_ref, bn_ref, ss_ref, sd_ref, sn_ref)
    s = pl.program_id(0)
    ns = pl.num_programs(0)
    cur = (s % 2) * PER_STEP
    nxt = PER_STEP - cur

    def fetch(step, first_buf):
        for k in range(PER_STEP):
            _issue_copies(lists, step * PER_STEP + k, yt.at[first_buf + k], False, ys_hbm,
                          sems.at[first_buf + k])

    @pl.when(s == 0)
    def _():
        yt[...] = jnp.zeros_like(yt)
        fetch(0, 0)

    @pl.when(s + 1 < ns)
    def _():
        fetch(s + 1, nxt)

    for k in range(PER_STEP):
        _wait_copies(lists, s * PER_STEP + k, ys_hbm, yt.at[cur + k], sems.at[cur + k])

    col = lax.broadcasted_iota(jnp.int32, (MOE_TILE, SORT_ROWS), 1)
    for k in range(PER_STEP):
        rs = slice(k * MOE_TILE, (k + 1) * MOE_TILE)
        cm = cm_ref[rs, :]
        lp0 = cm[:, 0:1].astype(jnp.int32)
        lp1 = cm[:, 1:2].astype(jnp.int32)
        w = jnp.where(col == lp0, cm[:, 2:3], jnp.where(col == lp1, cm[:, 3:4], 0.0)).astype(BF16)
        y = jnp.dot(w, yt[cur + k], preferred_element_type=F32)
        o_ref[rs, :] = _layer_norm(ALPHA * x_ref[rs, :] + y, g3_ref[...], b3_ref[...])


def _combine(plan, route, x2, ys, g3, b3):
    T, D = x2.shape
    rows = PER_STEP * MOE_TILE
    grid_spec = pltpu.PrefetchScalarGridSpec(
        num_scalar_prefetch=6,
        grid=(T // rows,),
        in_specs=[pl.BlockSpec((rows, LANES), lambda t, *_: (t, 0)),
                  pl.BlockSpec((rows, D), lambda t, *_: (t, 0)),
                  pl.BlockSpec(memory_space=pl.ANY),
                  pl.BlockSpec((1, D), lambda t, *_: (0, 0)),
                  pl.BlockSpec((1, D), lambda t, *_: (0, 0))],
        out_specs=pl.BlockSpec((rows, D), lambda t, *_: (t, 0)),
        scratch_shapes=[pltpu.VMEM((2 * PER_STEP, SORT_ROWS, D), BF16),
                        pltpu.SemaphoreType.DMA((2 * PER_STEP,))],
    )
    return pl.pallas_call(
        _combine_kernel,
        grid_spec=grid_spec,
        out_shape=jax.ShapeDtypeStruct((T, D), F32),
        compiler_params=pltpu.CompilerParams(
            dimension_semantics=("arbitrary",), vmem_limit_bytes=VMEM_LIMIT),
        name="moe_combine",
    )(*_copy_lists(plan), route, x2, ys, g3, b3)


def _moe(x2, x2b, route, lpt, meta, wg, wu_b, wd_b, g3, b3):
    plan = _plan(meta, x2.shape[0])
    xs, wg_b = _dispatch(plan, lpt, x2b, wg)
    ys = _gmm(plan, xs, wg_b, wu_b, wd_b)
    return _combine(plan, route, x2, ys, g3, b3)


def _row(v):
    return v.reshape(1, -1).astype(F32)


def kernel(x, mem, w_in, b_in, w_dw, b_dw, g_conv_norm, b_conv_norm, attn_sinks, w_out, g_ln1, b_ln1,
           w_mq, w_mkv, w_mo, g_ln2, b_ln2, w_group, b_group, w_router, b_router, w_gate, w_up, w_down,
           g_ln3, b_ln3):
    B, S, D = x.shape
    for l in range(DEPTH):
        w_dw_p = jnp.zeros((CONV_HALO, CONV_CH), F32).at[:CONV_WIDTH].set(w_dw[l])
        x1, wd_b = _mixer(x, attn_sinks[l].astype(F32), w_in[l], _row(b_in[l]), w_dw_p,
                          _row(b_dw[l]), _row(g_conv_norm[l]), _row(b_conv_norm[l]),
                          w_out[l], _row(g_ln1[l]), _row(b_ln1[l]), w_down[l])

        kvm = _memkv(mem.reshape(B * MEM_LEN, D), w_mkv[l]).reshape(B, MEM_LEN, 2 * D)

        wr = jnp.concatenate(
            [w_group[l], jnp.transpose(w_router[l], (1, 0, 2)).reshape(D, N_EXPERTS)], axis=1)
        wr = jnp.pad(wr, ((0, 0), (0, LANES - wr.shape[1])))
        br = jnp.pad(jnp.concatenate([b_group[l], b_router[l].reshape(-1)]), (0, LANES - N_GROUPS - N_EXPERTS))
        x2, x2b, route, lpt, meta, wu_b = _memattn(x1, w_mq[l], kvm, w_mo[l], _row(g_ln2[l]), _row(b_ln2[l]),
                                                   wr.astype(F32), _row(br), w_up[l])

        T = B * S
        y = _moe(x2.reshape(T, D), x2b.reshape(T, D), route.reshape(T, LANES), lpt, meta,
                 w_gate[l], wu_b, wd_b,
                 _row(g_ln3[l]), _row(b_ln3[l]))
        x = y.reshape(B, S, D)
    return x
```

```python
import jax
import jax.numpy as jnp
from jax import lax
from jax.experimental import pallas as pl
from jax.experimental.pallas import tpu as pltpu

D_MODEL = 1024
MEM_LEN = 256
CONV_CH = 512
CONV_WIDTH = 31
N_HEADS = 8
N_KV_HEADS = 2
HEAD_DIM = 64
GQ = N_HEADS // N_KV_HEADS
ATTN_W = N_HEADS * HEAD_DIM
KV_W = N_KV_HEADS * HEAD_DIM
WINDOW = 128
D_MIX = CONV_CH + ATTN_W
D_IN = 2 * CONV_CH + ATTN_W + 2 * KV_W
MEM_HEADS = 4
MEM_HEAD_DIM = D_MODEL // MEM_HEADS
N_GROUPS = 4
EXPERTS_PER_GROUP = 4
N_EXPERTS = N_GROUPS * EXPERTS_PER_GROUP
D_EXPERT = D_MODEL // 2
DEPTH = 1
ALPHA = (2.0 * DEPTH) ** 0.25
LN_EPS = 1e-5

LANES = 128
SUBLANES = 8
CONV_ROWS = 128
LN_ROWS = 64
MEMKV_COLS = 1024
MEM_TILE = 1024
LOG2E = 1.4426950408889634
MASK_VALUE = -1e30
CONV_HALO = 32
SEQ_TILE = 512
MOE_TILE = 512
CHUNK = 16
REGION_ALIGN = 512
GMM_ROWS = 2 * REGION_ALIGN
TOP_K = 2
SORT_ROWS = -(-(MOE_TILE * TOP_K + N_EXPERTS * (CHUNK - 1)) // 256) * 256
BIG = 4
MAX_BIG = SORT_ROWS // (BIG * CHUNK)
MAX_SMALL = N_EXPERTS * (BIG - 1)
LIST_STRIDE = LANES
assert max(MAX_BIG, MAX_SMALL) <= LIST_STRIDE
PER_STEP = 2
WAIT_GROUP = 8
ROUTE_OFF = N_GROUPS
VMEM_LIMIT = 56 * 1024 * 1024

BF16 = jnp.bfloat16
F32 = jnp.float32


def _layer_norm(x, g, b):
    mu = jnp.mean(x, axis=-1, keepdims=True)
    xc = x - mu
    var = jnp.mean(xc * xc, axis=-1, keepdims=True)
    return xc * lax.rsqrt(var + LN_EPS) * g + b


def _cast_bf16(dst_ref, src_ref):
    rows = 256
    for r0 in range(0, src_ref.shape[0], rows):
        dst_ref[r0:r0 + rows, :] = src_ref[r0:r0 + rows, :].astype(BF16)


def _dot_nt(a, b):
    return lax.dot_general(a, b, (((1,), (1,)), ((), ())), preferred_element_type=F32)


def _mixer_kernel(sinks_ref, x_ref, w_in_ref, b_in_ref, w_dw_ref, b_dw_ref, g_cn_ref, b_cn_ref,
                  w_out_ref, g1_ref, b1_ref, we_ref, o_ref, web_ref, w_in_b, w_out_b, hbuf, hshift, cbuf, qbuf,
                  kbuf, vbuf, ymix):
    i = pl.program_id(1)
    ts = SEQ_TILE
    web_ref[...] = we_ref[...].astype(BF16)

    @pl.when(jnp.logical_and(pl.program_id(0) == 0, i == 0))
    def _():
        _cast_bf16(w_in_b, w_in_ref)
        _cast_bf16(w_out_b, w_out_ref)

    @pl.when(i == 0)
    def _():
        hbuf[0:CONV_HALO, :] = jnp.zeros((CONV_HALO, CONV_CH), F32)
        kbuf[:, 0:WINDOW, :] = jnp.zeros((2 * N_KV_HEADS, WINDOW, KV_W), BF16)
        vbuf[:, 0:WINDOW, :] = jnp.zeros((2, WINDOW, KV_W), BF16)

    x = x_ref[0]
    u = jnp.dot(x.astype(BF16), w_in_b[...], preferred_element_type=F32) + b_in_ref[...]
    a = u[:, 0:CONV_CH]
    gate = u[:, CONV_CH:2 * CONV_CH]
    hbuf[CONV_HALO:CONV_HALO + ts, :] = a * jax.nn.sigmoid(gate)
    qbuf[...] = (u[:, 2 * CONV_CH:2 * CONV_CH + ATTN_W] * (HEAD_DIM ** -0.5 * LOG2E)).astype(BF16)
    kf = u[:, 2 * CONV_CH + ATTN_W:2 * CONV_CH + ATTN_W + KV_W]
    vf = u[:, 2 * CONV_CH + ATTN_W + KV_W:D_IN]
    kr = pltpu.roll(kf, HEAD_DIM, axis=1)
    vr = pltpu.roll(vf, HEAD_DIM, axis=1)
    lo = lax.broadcasted_iota(jnp.int32, (ts, KV_W), 1) < HEAD_DIM
    rows = slice(WINDOW, WINDOW + ts)
    kbuf[0, rows, :] = jnp.where(lo, kf, 0.0).astype(BF16)
    kbuf[1, rows, :] = jnp.where(lo, 0.0, kr).astype(BF16)
    kbuf[2, rows, :] = jnp.where(lo, kr, 0.0).astype(BF16)
    kbuf[3, rows, :] = jnp.where(lo, 0.0, kf).astype(BF16)
    vbuf[0, rows, :] = vf.astype(BF16)
    vbuf[1, rows, :] = vr.astype(BF16)

    base = CONV_HALO - (CONV_WIDTH - 1)
    n_shift = ts + CONV_HALO - SUBLANES
    for b in range(1, SUBLANES):
        hshift[b - 1, 0:n_shift, :] = hbuf[b:b + n_shift, :]
    rc = CONV_ROWS

    def conv_chunk(c):
        r0 = c * rc
        for l in range(CONV_CH // LANES):
            ls = slice(l * LANES, (l + 1) * LANES)
            acc = jnp.zeros((rc, LANES), F32)
            for j in range(CONV_WIDTH):
                a8, b = divmod(j + base, SUBLANES)
                rs = slice(r0 + SUBLANES * a8, r0 + SUBLANES * a8 + rc)
                tap = hbuf[rs, ls] if b == 0 else hshift[b - 1, rs, ls]
                acc = acc + tap * w_dw_ref[j:j + 1, ls]
            cbuf[r0:r0 + rc, ls] = acc
        for r1 in range(r0, r0 + rc, LN_ROWS):
            rs = slice(r1, r1 + LN_ROWS)
            y = _layer_norm(cbuf[rs, :] + b_dw_ref[...], g_cn_ref[...], b_cn_ref[...])
            y = y * jax.nn.sigmoid(y)
            ymix[rs, 0:CONV_CH] = y.astype(BF16)

    qi = lax.broadcasted_iota(jnp.int32, (2 * WINDOW, WINDOW), 0) % WINDOW
    kj = lax.broadcasted_iota(jnp.int32, (2 * WINDOW, WINDOW), 1)
    own = kj <= qi
    top = lax.broadcasted_iota(jnp.int32, (2 * WINDOW, 1), 0) < WINDOW
    lo_out = lax.broadcasted_iota(jnp.int32, (WINDOW, 2 * HEAD_DIM), 1) < HEAD_DIM

    def attn_block(jb):
        r0 = jb * WINDOW
        prev_ok = jnp.logical_not(own) & (i != 0) if jb == 0 else jnp.logical_not(own)
        for kvh in range(N_KV_HEADS):
            h0 = kvh * GQ
            c0 = h0 * HEAD_DIM
            qs = jnp.concatenate([qbuf[r0:r0 + WINDOW, c0:c0 + 2 * HEAD_DIM],
                                  qbuf[r0:r0 + WINDOW, c0 + 2 * HEAD_DIM:c0 + 4 * HEAD_DIM]], axis=0)
            pv = []
            for par in range(2):
                kk = kbuf[2 * kvh + par, r0:r0 + 2 * WINDOW, :]
                vv = vbuf[(kvh + par) % 2, r0:r0 + 2 * WINDOW, :]
                s2 = _dot_nt(qs, kk)
                prev_part = jnp.where(prev_ok, s2[:, :WINDOW], MASK_VALUE) if jb == 0 else s2[:, :WINDOW]
                s = jnp.where(own, s2[:, WINDOW:], prev_part)
                sink = jnp.where(top, sinks_ref[h0 + par] * LOG2E, sinks_ref[h0 + 2 + par] * LOG2E)
                m = jnp.maximum(jnp.max(s, axis=-1, keepdims=True), sink)
                p = jnp.exp2(s - m)
                denom = jnp.sum(p, axis=-1, keepdims=True) + jnp.exp2(sink - m)
                zero = jnp.zeros((), BF16)
                pb = p.astype(BF16)
                p2 = jnp.concatenate([jnp.where(own, zero, pb), jnp.where(own, pb, zero)], axis=1)
                pv.append(jnp.dot(p2, vv, preferred_element_type=F32) / denom)
            for pair in range(2):
                rs = slice(pair * WINDOW, (pair + 1) * WINDOW)
                o = jnp.where(lo_out, pv[0][rs], pv[1][rs])
                cs = CONV_CH + c0 + pair * 2 * HEAD_DIM
                ymix[r0:r0 + WINDOW, cs:cs + 2 * HEAD_DIM] = o.astype(BF16)

    assert ts // rc == ts // WINDOW
    for c in range(ts // rc):
        attn_block(c)
        conv_chunk(c)

    mix = jnp.dot(ymix[...], w_out_b[...], preferred_element_type=F32)
    o_ref[0] = _layer_norm(ALPHA * x + mix, g1_ref[...], b1_ref[...])

    hbuf[0:CONV_HALO, :] = hbuf[ts:ts + CONV_HALO, :]
    kbuf[:, 0:WINDOW, :] = kbuf[:, ts:ts + WINDOW, :]
    vbuf[:, 0:WINDOW, :] = vbuf[:, ts:ts + WINDOW, :]


def _const_spec(shape):
    nd = len(shape)
    return pl.BlockSpec(shape, lambda *_: (0,) * nd)


def _resident_spec(shape):
    nd = len(shape)
    return pl.BlockSpec(shape, lambda *_: (0,) * nd, pipeline_mode=pl.Buffered(1))


def _mixer(x, sinks, w_in, b_in, w_dw, b_dw, g_cn, b_cn, w_out, g1, b1, w_expert):
    B, S, D = x.shape
    ts = SEQ_TILE
    tile = pl.BlockSpec((1, ts, D), lambda b, i: (b, i, 0))
    nt = S // ts
    n_e, we_rows, we_cols = w_expert.shape
    per = (B * nt) // n_e
    assert per * n_e == B * nt and we_rows % per == 0
    wes = pl.BlockSpec((1, we_rows // per, we_cols), lambda b, i: ((b * nt + i) // per, (b * nt + i) % per, 0))
    return pl.pallas_call(
        _mixer_kernel,
        grid=(B, S // ts),
        in_specs=[
            pl.BlockSpec(memory_space=pltpu.SMEM),
            tile,
            _resident_spec((D, D_IN)), _const_spec((1, D_IN)),
            _const_spec((CONV_HALO, CONV_CH)), _const_spec((1, CONV_CH)),
            _const_spec((1, CONV_CH)), _const_spec((1, CONV_CH)),
            _resident_spec((D_MIX, D)), _const_spec((1, D)), _const_spec((1, D)),
            wes,
        ],
        out_specs=[tile, wes],
        out_shape=[jax.ShapeDtypeStruct((B, S, D), F32), jax.ShapeDtypeStruct(w_expert.shape, BF16)],
        scratch_shapes=[
            pltpu.VMEM((D, D_IN), BF16),
            pltpu.VMEM((D_MIX, D), BF16),
            pltpu.VMEM((CONV_HALO + ts, CONV_CH), F32),
            pltpu.VMEM((SUBLANES - 1, CONV_HALO + ts, CONV_CH), F32),
            pltpu.VMEM((ts, CONV_CH), F32),
            pltpu.VMEM((ts, ATTN_W), BF16),
            pltpu.VMEM((2 * N_KV_HEADS, WINDOW + ts, KV_W), BF16),
            pltpu.VMEM((2, WINDOW + ts, KV_W), BF16),
            pltpu.VMEM((ts, D_MIX), BF16),
        ],
        compiler_params=pltpu.CompilerParams(
            dimension_semantics=("arbitrary", "arbitrary"), vmem_limit_bytes=VMEM_LIMIT),
        name="mixer",
    )(sinks, x, w_in, b_in, w_dw, b_dw, g_cn, b_cn, w_out, g1, b1, w_expert)


def _memkv_kernel(mem_ref, w_ref, o_ref):
    o_ref[...] = jnp.dot(mem_ref[...].astype(BF16), w_ref[...].astype(BF16),
                         preferred_element_type=F32).astype(BF16)


def _memkv(mem2d, w_mkv):
    M, D = mem2d.shape
    N = w_mkv.shape[1]
    tn = MEMKV_COLS
    return pl.pallas_call(
        _memkv_kernel,
        grid=(N // tn,),
        in_specs=[pl.BlockSpec((M, D), lambda j: (0, 0)), pl.BlockSpec((D, tn), lambda j: (0, j))],
        out_specs=pl.BlockSpec((M, tn), lambda j: (0, j)),
        out_shape=jax.ShapeDtypeStruct((M, N), BF16),
        compiler_params=pltpu.CompilerParams(dimension_semantics=("arbitrary",)),
        name="memkv",
    )(mem2d, w_mkv)


def _first_max(rows):
    best = rows[0]
    for r in rows[1:]:
        best = jnp.maximum(best, r)
    idx = jnp.full(best.shape, len(rows) - 1, jnp.int32)
    for k in range(len(rows) - 2, -1, -1):
        idx = jnp.where(rows[k] == best, k, idx)
    return best, idx


def _route_plan(logits_t):
    tile = logits_t.shape[1]
    row = lambda k: logits_t[k:k + 1, :]
    gmax, g_idx = _first_max([row(g) for g in range(N_GROUPS)])
    gsum = jnp.exp(row(0) - gmax)
    for g in range(1, N_GROUPS):
        gsum = gsum + jnp.exp(row(g) - gmax)
    g_p = 1.0 / gsum
    rl = []
    for e in range(EXPERTS_PER_GROUP):
        v = row(ROUTE_OFF + (N_GROUPS - 1) * EXPERTS_PER_GROUP + e)
        for g in range(N_GROUPS - 2, -1, -1):
            v = jnp.where(g_idx == g, row(ROUTE_OFF + g * EXPERTS_PER_GROUP + e), v)
        rl.append(v)
    m1, i1 = _first_max(rl)
    m2, i2 = _first_max([jnp.where(i1 == e, MASK_VALUE, rl[e]) for e in range(EXPERTS_PER_GROUP)])
    ex = jnp.exp(m2 - m1)
    w1 = 1.0 / (1.0 + ex)
    w2 = ex * w1
    e1 = g_idx * EXPERTS_PER_GROUP + i1
    e2 = g_idx * EXPERTS_PER_GROUP + i2

    eid = lax.broadcasted_iota(jnp.int32, (N_EXPERTS, tile), 0)
    hit1 = eid == e1
    hit2 = eid == e2
    oh = jnp.where(jnp.logical_or(hit1, hit2), 1.0, 0.0)
    r = lax.broadcasted_iota(jnp.int32, (tile, tile), 0)
    c = lax.broadcasted_iota(jnp.int32, (tile, tile), 1)
    tri = jnp.where(r <= c, 1.0, 0.0).astype(BF16)
    csum = jnp.dot(oh.astype(BF16), tri, preferred_element_type=F32)
    counts = jnp.broadcast_to(csum[:, tile - 1:tile], (N_EXPERTS, tile)).astype(jnp.int32)
    nch = jnp.right_shift(counts + (CHUNK - 1), CHUNK.bit_length() - 1)
    er = lax.broadcasted_iota(jnp.int32, (N_EXPERTS, N_EXPERTS), 0)
    ec = lax.broadcasted_iota(jnp.int32, (N_EXPERTS, N_EXPERTS), 1)
    lower = jnp.where(ec < er, 1.0, 0.0).astype(BF16)
    off = jnp.dot(lower, nch.astype(F32).astype(BF16), preferred_element_type=F32) * CHUNK
    pos = off + csum - oh
    lp1 = jnp.sum(jnp.where(hit1, pos, 0.0), axis=0, keepdims=True)
    lp2 = jnp.sum(jnp.where(hit2, pos, 0.0), axis=0, keepdims=True)
    zero = jnp.zeros_like(lp1)
    route_t = jnp.concatenate([lp1, lp2, g_p * w1, g_p * w2, zero, zero, zero, zero], axis=0)
    meta = jnp.concatenate([nch[:, 0:LANES], off[:, 0:LANES].astype(jnp.int32)], axis=0)
    return route_t, meta


def _memattn_kernel(x_ref, wq_ref, k_ref, v_ref, wo_ref, g2_ref, b2_ref, wr2_ref, wrh_ref, br_ref, we_ref,
                    o_ref, ob_ref, route_ref, routet_ref, meta_ref, web_ref, wq_b, wo_b):
    web_ref[...] = we_ref[...].astype(BF16)

    @pl.when(jnp.logical_and(pl.program_id(0) == 0, pl.program_id(1) == 0))
    def _():
        _cast_bf16(wq_b, wq_ref)
        _cast_bf16(wo_b, wo_ref)

    def rows_logits(rs):
        x = x_ref[0, rs, :]
        q = jnp.dot(x.astype(BF16), wq_b[...], preferred_element_type=F32)
        q = (q * (MEM_HEAD_DIM ** -0.5)).astype(BF16)
        outs = []
        for h in range(MEM_HEADS):
            sl = slice(h * MEM_HEAD_DIM, (h + 1) * MEM_HEAD_DIM)
            s = _dot_nt(q[:, sl], k_ref[0, :, sl])
            m = jnp.max(s, axis=-1, keepdims=True)
            p = jnp.exp(s - m)
            denom = jnp.sum(p, axis=-1, keepdims=True)
            o = jnp.dot(p.astype(BF16), v_ref[0, :, sl], preferred_element_type=F32)
            outs.append((o / denom).astype(BF16))
        o = jnp.dot(jnp.concatenate(outs, axis=-1), wo_b[...], preferred_element_type=F32)
        x2 = _layer_norm(ALPHA * x + o, g2_ref[...], b2_ref[...])
        o_ref[0, rs, :] = x2
        x2h = x2.astype(BF16)
        ob_ref[0, rs, :] = x2h
        x2l = (x2 - x2h.astype(F32)).astype(BF16)
        hh = jnp.dot(x2h, wr2_ref[...], preferred_element_type=F32)
        return (hh[:, 0:LANES] + hh[:, LANES:2 * LANES]
                + jnp.dot(x2l, wrh_ref[...], preferred_element_type=F32) + br_ref[...])

    ts = x_ref.shape[1]
    groups = [slice(r0, r0 + MOE_TILE) for r0 in range(0, ts, MOE_TILE)]
    logits = rows_logits(slice(0, ts))
    pad = jnp.zeros((LANES - SUBLANES, MOE_TILE), F32)
    for k, rs in enumerate(groups):
        route_t, meta = _route_plan(jnp.transpose(logits[rs, :]))
        routet_ref[k] = route_t
        route_ref[0, rs, :] = jnp.transpose(jnp.concatenate([route_t, pad], axis=0))
        meta_ref[k] = meta


def _memattn(x1, wq, kvm, wo, g2, b2, wr, br, w_expert):
    B, S, D = x1.shape
    ts = MEM_TILE
    assert B * (S // ts) == w_expert.shape[0]
    wes = pl.BlockSpec((1,) + w_expert.shape[1:], lambda b, i: (b * (S // ts) + i, 0, 0))
    per = ts // MOE_TILE
    nt = S // ts
    c = wr * (2.0 ** 16 + 1.0)
    w_high = c - (c - wr)
    wrh = w_high.astype(BF16)
    wr2 = jnp.concatenate([wrh, (wr - w_high).astype(BF16)], axis=1)
    tile = pl.BlockSpec((1, ts, D), lambda b, i: (b, i, 0))
    kspec = pl.BlockSpec((1, MEM_LEN, D), lambda b, i: (b, 0, 0))
    vspec = pl.BlockSpec((1, MEM_LEN, D), lambda b, i: (b, 0, 1))
    return pl.pallas_call(
        _memattn_kernel,
        grid=(B, nt),
        in_specs=[tile, _resident_spec((D, D)), kspec, vspec, _resident_spec((D, D)),
                  _const_spec((1, D)), _const_spec((1, D)),
                  _const_spec((D, 2 * LANES)), _const_spec((D, LANES)), _const_spec((1, LANES)), wes],
        out_specs=[tile, tile, pl.BlockSpec((1, ts, LANES), lambda b, i: (b, i, 0)),
                   pl.BlockSpec((per, SUBLANES, MOE_TILE), lambda b, i: (b * nt + i, 0, 0)),
                   pl.BlockSpec((per, 2 * N_EXPERTS, LANES), lambda b, i: (b * nt + i, 0, 0)), wes],
        out_shape=[jax.ShapeDtypeStruct((B, S, D), F32),
                   jax.ShapeDtypeStruct((B, S, D), BF16),
                   jax.ShapeDtypeStruct((B, S, LANES), F32),
                   jax.ShapeDtypeStruct((B * nt * per, SUBLANES, MOE_TILE), F32),
                   jax.ShapeDtypeStruct((B * nt * per, 2 * N_EXPERTS, LANES), jnp.int32),
                   jax.ShapeDtypeStruct(w_expert.shape, BF16)],
        scratch_shapes=[pltpu.VMEM((D, D), BF16), pltpu.VMEM((D, D), BF16)],
        compiler_params=pltpu.CompilerParams(
            dimension_semantics=("arbitrary", "arbitrary"), vmem_limit_bytes=VMEM_LIMIT),
        name="memattn",
    )(x1, wq, kvm, kvm, wo, g2, b2, wr2, wrh, br, w_expert)


def _gmm_blocks(n_tokens):
    rows = (n_tokens * TOP_K + (n_tokens // MOE_TILE) * N_EXPERTS * (CHUNK - 1)
            + N_EXPERTS * (REGION_ALIGN - CHUNK))
    return -(-rows // GMM_ROWS)


def _plan(meta, T):
    nch = meta[:, :N_EXPERTS, 0]
    n16 = nch * CHUNK
    n_e = jnp.sum(n16, axis=0)
    reg = (n_e + REGION_ALIGN - 1) // REGION_ALIGN * REGION_ALIGN
    gend = jnp.cumsum(reg)
    gbase = gend - reg
    dst = gbase[None, :] + jnp.cumsum(n16, axis=0) - n16
    half_row = jnp.arange(2 * _gmm_blocks(T), dtype=jnp.int32) * REGION_ALIGN
    half_expert = jnp.minimum(jnp.sum(half_row[:, None] >= gend[None, :], axis=1), N_EXPERTS - 1)
    half_used = half_row < gend[-1]
    n_used = ((gend[-1] + GMM_ROWS - 1) // GMM_ROWS).astype(jnp.int32).reshape(1)
    src = (jnp.cumsum(nch, axis=1) - nch) * CHUNK
    n_big = nch // BIG

    def copy_list(count, src0, dst0, rows, length):
        cum = jnp.cumsum(count, axis=1)
        first = (cum - count)[:, None, :]
        k = jnp.arange(length, dtype=jnp.int32)[None, :, None]
        mine = (k >= first) & (k < cum[:, None, :])
        step = (k - first) * rows
        pick = lambda base: jnp.sum(jnp.where(mine, base[:, None, :] + step, 0), axis=2)
        return pick(src0), pick(dst0), cum[:, -1]

    big_src, big_dst, big_n = copy_list(n_big, src, dst, BIG * CHUNK, LIST_STRIDE)
    rest = n_big * (BIG * CHUNK)
    small_src, small_dst, small_n = copy_list(nch - n_big * BIG, src + rest, dst + rest, CHUNK, LIST_STRIDE)
    i32 = lambda a: a.astype(jnp.int32)
    fill_start = jnp.concatenate([gbase + n_e, gend[-1:]])
    fill_n = jnp.concatenate([reg - n_e, _gmm_blocks(T) * GMM_ROWS - gend[-1:]]) // CHUNK
    return dict(big_src=i32(big_src).reshape(-1), big_dst=i32(big_dst).reshape(-1), big_n=i32(big_n),
                small_src=i32(small_src).reshape(-1), small_dst=i32(small_dst).reshape(-1), small_n=i32(small_n),
                fill_start=i32(fill_start), fill_n=i32(fill_n),
                fill_tot=i32(jnp.sum(fill_n)).reshape(1), half_expert=i32(half_expert),
                half_used=i32(half_used), n_used=n_used)


def _rows_copy(src_ref, src_row, dst_ref, dst_row, rows, sem):
    return pltpu.make_async_copy(
        src_ref.at[pl.ds(pl.multiple_of(src_row, CHUNK), rows), :],
        dst_ref.at[pl.ds(pl.multiple_of(dst_row, CHUNK), rows), :], sem)


def _chunk_copy(src_ref, src_row, dst_ref, dst_row, sem):
    return _rows_copy(src_ref, src_row, dst_ref, dst_row, CHUNK, sem)


def _issue_copies(lists, tile, tile_ref, tile_is_src, hbm_ref, sem):
    big_src, big_dst, big_n, small_src, small_dst, small_n = lists
    for src_l, dst_l, n_l, rows in ((big_src, big_dst, big_n, BIG * CHUNK),
                                    (small_src, small_dst, small_n, CHUNK)):
        def issue(k, carry, src_l=src_l, dst_l=dst_l, rows=rows):
            local, remote = src_l[tile * LIST_STRIDE + k], dst_l[tile * LIST_STRIDE + k]
            if tile_is_src:
                _rows_copy(tile_ref, local, hbm_ref, remote, rows, sem).start()
            else:
                _rows_copy(hbm_ref, remote, tile_ref, local, rows, sem).start()
            return carry

        lax.fori_loop(0, n_l[tile], issue, 0)


def _wait_rows(n, rows, src_ref, dst_ref, sem):
    def body(c, carry):
        pltpu.make_async_copy(src_ref.at[pl.ds(0, rows), :], dst_ref.at[pl.ds(0, rows), :], sem).wait()
        return carry

    lax.fori_loop(0, n, body, 0)


def _wait_copies(lists, tile, src_ref, dst_ref, sem, extra_chunks=0):
    _wait_rows(lists[2][tile], BIG * CHUNK, src_ref, dst_ref, sem)
    n = lists[5][tile] + extra_chunks
    _wait_rows(n // WAIT_GROUP, WAIT_GROUP * CHUNK, src_ref, dst_ref, sem)
    _wait_rows(n % WAIT_GROUP, CHUNK, src_ref, dst_ref, sem)


def _dispatch_kernel(bs_ref, bd_ref, bn_ref, ss_ref, sd_ref, sn_ref, fstart_ref, fn_ref, ftot_ref,
                     lp_ref, x_ref, we_ref, xs_hbm, web_ref, xt, zbuf, sems):
    lists = (bs_ref, bd_ref, bn_ref, ss_ref, sd_ref, sn_ref)
    s = pl.program_id(0)
    ns = pl.num_programs(0)
    cur = (s % 2) * PER_STEP
    prv = PER_STEP - cur

    @pl.when(s >= 2)
    def _():
        for k in range(PER_STEP):
            _wait_copies(lists, (s - 2) * PER_STEP + k, xt.at[cur + k], xs_hbm, sems.at[cur + k])

    r = lax.broadcasted_iota(jnp.int32, (SORT_ROWS, MOE_TILE), 0)
    for k in range(PER_STEP):
        lp = lp_ref[k].astype(jnp.int32)
        hit = jnp.logical_or(lp[0:1, :] == r, lp[1:2, :] == r)
        p = jnp.where(hit, 1.0, 0.0).astype(BF16)
        x = x_ref[k * MOE_TILE:(k + 1) * MOE_TILE, :]
        xt[cur + k] = jnp.dot(p, x, preferred_element_type=F32).astype(BF16)

    for k in range(PER_STEP):
        _issue_copies(lists, s * PER_STEP + k, xt.at[cur + k], True, xs_hbm, sems.at[cur + k])

    _cast_bf16(web_ref.at[0], we_ref.at[0])

    @pl.when(s == ns - 1)
    def _():
        zbuf[...] = jnp.zeros_like(zbuf)
        sem = sems.at[cur]

        def per_range(e, carry):
            def issue(c, carry2):
                _chunk_copy(zbuf, 0, xs_hbm, fstart_ref[e] + c * CHUNK, sem).start()
                return carry2

            return lax.fori_loop(0, fn_ref[e], issue, carry)

        lax.fori_loop(0, N_EXPERTS + 1, per_range, 0)
        for k in range(PER_STEP):
            _wait_copies(lists, (s - 1) * PER_STEP + k, xt.at[prv + k], xs_hbm, sems.at[prv + k])
            _wait_copies(lists, s * PER_STEP + k, xt.at[cur + k], xs_hbm, sems.at[cur + k],
                         extra_chunks=ftot_ref[0] if k == 0 else 0)


def _copy_lists(plan):
    return tuple(plan[k] for k in ("big_src", "big_dst", "big_n", "small_src", "small_dst", "small_n"))


def _dispatch(plan, lpt, x2b, w_expert):
    T, D = x2b.shape
    rows = PER_STEP * MOE_TILE
    assert T // rows >= 2
    assert T // rows == N_EXPERTS
    wes = pl.BlockSpec((1,) + w_expert.shape[1:], lambda t, *_: (t, 0, 0))
    grid_spec = pltpu.PrefetchScalarGridSpec(
        num_scalar_prefetch=9,
        grid=(T // rows,),
        in_specs=[pl.BlockSpec((PER_STEP, SUBLANES, MOE_TILE), lambda t, *_: (t, 0, 0)),
                  pl.BlockSpec((rows, D), lambda t, *_: (t, 0)), wes],
        out_specs=[pl.BlockSpec(memory_space=pl.ANY), wes],
        scratch_shapes=[pltpu.VMEM((2 * PER_STEP, SORT_ROWS, D), BF16), pltpu.VMEM((CHUNK, D), BF16),
                        pltpu.SemaphoreType.DMA((2 * PER_STEP,))],
    )
    return pl.pallas_call(
        _dispatch_kernel,
        grid_spec=grid_spec,
        out_shape=[jax.ShapeDtypeStruct((_gmm_blocks(T) * GMM_ROWS, D), BF16),
                   jax.ShapeDtypeStruct(w_expert.shape, BF16)],
        compiler_params=pltpu.CompilerParams(
            dimension_semantics=("arbitrary",), vmem_limit_bytes=VMEM_LIMIT),
        name="moe_dispatch",
    )(*_copy_lists(plan), plan["fill_start"], plan["fill_n"], plan["fill_tot"], lpt, x2b, w_expert)


def _expert_ffn(xb, wg_b, wu_b, wd_b):
    g = jnp.dot(xb, wg_b[...], preferred_element_type=F32)
    u = jnp.dot(xb, wu_b[...], preferred_element_type=F32)
    h = (g * jax.nn.sigmoid(g)) * u
    return jnp.dot(h.astype(BF16), wd_b[...], preferred_element_type=F32).astype(BF16)


def _gmm_kernel(he_ref, hu_ref, nu_ref, x_ref, wga_ref, wua_ref, wda_ref, wgb_ref, wub_ref, wdb_ref, o_ref):
    b = pl.program_id(0)
    half = REGION_ALIGN
    same = he_ref[2 * b] == he_ref[2 * b + 1]
    used0 = hu_ref[2 * b] != 0
    used1 = hu_ref[2 * b + 1] != 0
    set_a = (wga_ref.at[0], wua_ref.at[0], wda_ref.at[0])
    set_b = (wgb_ref.at[0], wub_ref.at[0], wdb_ref.at[0])

    @pl.when(jnp.logical_and(used1, same))
    def _():
        o_ref[...] = _expert_ffn(x_ref[...], *set_a)

    @pl.when(jnp.logical_and(used0, jnp.logical_not(jnp.logical_and(used1, same))))
    def _():
        o_ref[0:half, :] = _expert_ffn(x_ref[0:half, :], *set_a)

    @pl.when(jnp.logical_and(used1, jnp.logical_not(same)))
    def _():
        o_ref[half:GMM_ROWS, :] = _expert_ffn(x_ref[half:GMM_ROWS, :], *set_b)

    @pl.when(jnp.logical_not(used0))
    def _():
        o_ref[0:half, :] = jnp.zeros((half, o_ref.shape[1]), o_ref.dtype)

    @pl.when(jnp.logical_not(used1))
    def _():
        o_ref[half:GMM_ROWS, :] = jnp.zeros((GMM_ROWS - half, o_ref.shape[1]), o_ref.dtype)


def _gmm(plan, xs, wg, wu, wd):
    R, D = xs.shape
    rows = pl.BlockSpec((GMM_ROWS, D), lambda b, he, hu, nu: (jnp.minimum(b, nu[0] - 1), 0))

    def wspec(shape, h):
        return pl.BlockSpec((1,) + shape, lambda b, he, hu, nu: (he[2 * b + h], 0, 0))

    grid_spec = pltpu.PrefetchScalarGridSpec(
        num_scalar_prefetch=3,
        grid=(R // GMM_ROWS,),
        in_specs=[rows,
                  wspec((D, D_EXPERT), 0), wspec((D, D_EXPERT), 0), wspec((D_EXPERT, D), 0),
                  wspec((D, D_EXPERT), 1), wspec((D, D_EXPERT), 1), wspec((D_EXPERT, D), 1)],
        out_specs=pl.BlockSpec((GMM_ROWS, D), lambda b, he, hu, nu: (b, 0)),
    )
    return pl.pallas_call(
        _gmm_kernel,
        grid_spec=grid_spec,
        out_shape=jax.ShapeDtypeStruct((R, D), BF16),
        compiler_params=pltpu.CompilerParams(
            dimension_semantics=("arbitrary",), vmem_limit_bytes=VMEM_LIMIT),
        name="moe_gmm",
    )(plan["half_expert"], plan["half_used"], plan["n_used"], xs, wg, wu, wd, wg, wu, wd)


def _combine_kernel(bs_ref, bd_ref, bn_ref, ss_ref, sd_ref, sn_ref, cm_ref, x_ref, ys_hbm, g3_ref, b3_ref,
                    o_ref, yt, sems):
    lists = (bs_ref, bd_ref, bn_ref, ss_ref, sd_ref, sn_ref)
    s = pl.program_id(0)
    ns = pl.num_programs(0)
    cur = (s % 2) * PER_STEP
    nxt = PER_STEP - cur

    def fetch(step, first_buf):
        for k in range(PER_STEP):
            _issue_copies(lists, step * PER_STEP + k, yt.at[first_buf + k], False, ys_hbm,
                          sems.at[first_buf + k])

    @pl.when(s == 0)
    def _():
        yt[...] = jnp.zeros_like(yt)
        fetch(0, 0)

    @pl.when(s + 1 < ns)
    def _():
        fetch(s + 1, nxt)

    for k in range(PER_STEP):
        _wait_copies(lists, s * PER_STEP + k, ys_hbm, yt.at[cur + k], sems.at[cur + k])

    col = lax.broadcasted_iota(jnp.int32, (MOE_TILE, SORT_ROWS), 1)
    for k in range(PER_STEP):
        rs = slice(k * MOE_TILE, (k + 1) * MOE_TILE)
        cm = cm_ref[rs, :]
        lp0 = cm[:, 0:1].astype(jnp.int32)
        lp1 = cm[:, 1:2].astype(jnp.int32)
        w = jnp.where(col == lp0, cm[:, 2:3], jnp.where(col == lp1, cm[:, 3:4], 0.0)).astype(BF16)
        y = jnp.dot(w, yt[cur + k], preferred_element_type=F32)
        o_ref[rs, :] = _layer_norm(ALPHA * x_ref[rs, :] + y, g3_ref[...], b3_ref[...])


def _combine(plan, route, x2, ys, g3, b3):
    T, D = x2.shape
    rows = PER_STEP * MOE_TILE
    grid_spec = pltpu.PrefetchScalarGridSpec(
        num_scalar_prefetch=6,
        grid=(T // rows,),
        in_specs=[pl.BlockSpec((rows, LANES), lambda t, *_: (t, 0)),
                  pl.BlockSpec((rows, D), lambda t, *_: (t, 0)),
                  pl.BlockSpec(memory_space=pl.ANY),
                  pl.BlockSpec((1, D), lambda t, *_: (0, 0)),
                  pl.BlockSpec((1, D), lambda t, *_: (0, 0))],
        out_specs=pl.BlockSpec((rows, D), lambda t, *_: (t, 0)),
        scratch_shapes=[pltpu.VMEM((2 * PER_STEP, SORT_ROWS, D), BF16),
                        pltpu.SemaphoreType.DMA((2 * PER_STEP,))],
    )
    return pl.pallas_call(
        _combine_kernel,
        grid_spec=grid_spec,
        out_shape=jax.ShapeDtypeStruct((T, D), F32),
        compiler_params=pltpu.CompilerParams(
            dimension_semantics=("arbitrary",), vmem_limit_bytes=VMEM_LIMIT),
        name="moe_combine",
    )(*_copy_lists(plan), route, x2, ys, g3, b3)


def _moe(x2, x2b, route, lpt, meta, wg, wu_b, wd_b, g3, b3):
    plan = _plan(meta, x2.shape[0])
    xs, wg_b = _dispatch(plan, lpt, x2b, wg)
    ys = _gmm(plan, xs, wg_b, wu_b, wd_b)
    return _combine(plan, route, x2, ys, g3, b3)


def _row(v):
    return v.reshape(1, -1).astype(F32)


def kernel(x, mem, w_in, b_in, w_dw, b_dw, g_conv_norm, b_conv_norm, attn_sinks, w_out, g_ln1, b_ln1,
           w_mq, w_mkv, w_mo, g_ln2, b_ln2, w_group, b_group, w_router, b_router, w_gate, w_up, w_down,
           g_ln3, b_ln3):
    B, S, D = x.shape
    for l in range(DEPTH):
        w_dw_p = jnp.zeros((CONV_HALO, CONV_CH), F32).at[:CONV_WIDTH].set(w_dw[l])
        x1, wd_b = _mixer(x, attn_sinks[l].astype(F32), w_in[l], _row(b_in[l]), w_dw_p,
                          _row(b_dw[l]), _row(g_conv_norm[l]), _row(b_conv_norm[l]),
                          w_out[l], _row(g_ln1[l]), _row(b_ln1[l]), w_down[l])

        kvm = _memkv(mem.reshape(B * MEM_LEN, D), w_mkv[l]).reshape(B, MEM_LEN, 2 * D)

        wr = jnp.concatenate(
            [w_group[l], jnp.transpose(w_router[l], (1, 0, 2)).reshape(D, N_EXPERTS)], axis=1)
        wr = jnp.pad(wr, ((0, 0), (0, LANES - wr.shape[1])))
        br = jnp.pad(jnp.concatenate([b_group[l], b_router[l].reshape(-1)]), (0, LANES - N_GROUPS - N_EXPERTS))
        x2, x2b, route, lpt, meta, wu_b = _memattn(x1, w_mq[l], kvm, w_mo[l], _row(g_ln2[l]), _row(b_ln2[l]),
                                                   wr.astype(F32), _row(br), w_up[l])

        T = B * S
        y = _moe(x2.reshape(T, D), x2b.reshape(T, D), route.reshape(T, LANES), lpt, meta,
                 w_gate[l], wu_b, wd_b,
                 _row(g_ln3[l]), _row(b_ln3[l]))
        x = y.reshape(B, S, D)
    return x
```

```python
import jax
import jax.numpy as jnp
from jax import lax
from jax.experimental import pallas as pl
from jax.experimental.pallas import tpu as pltpu

D_MODEL = 1024
MEM_LEN = 256
CONV_CH = 512
CONV_WIDTH = 31
N_HEADS = 8
N_KV_HEADS = 2
HEAD_DIM = 64
GQ = N_HEADS // N_KV_HEADS
ATTN_W = N_HEADS * HEAD_DIM
KV_W = N_KV_HEADS * HEAD_DIM
WINDOW = 128
D_MIX = CONV_CH + ATTN_W
D_IN = 2 * CONV_CH + ATTN_W + 2 * KV_W
MEM_HEADS = 4
MEM_HEAD_DIM = D_MODEL // MEM_HEADS
N_GROUPS = 4
EXPERTS_PER_GROUP = 4
N_EXPERTS = N_GROUPS * EXPERTS_PER_GROUP
D_EXPERT = D_MODEL // 2
DEPTH = 1
ALPHA = (2.0 * DEPTH) ** 0.25
LN_EPS = 1e-5

LANES = 128
SUBLANES = 8
CONV_ROWS = 128
LN_ROWS = 64
MEMKV_COLS = 1024
MEM_TILE = 1024
LOG2E = 1.4426950408889634
MASK_VALUE = -1e30
CONV_HALO = 32
SEQ_TILE = 512
MOE_TILE = 512
CHUNK = 16
REGION_ALIGN = 512
GMM_ROWS = 2 * REGION_ALIGN
TOP_K = 2
SORT_ROWS = -(-(MOE_TILE * TOP_K + N_EXPERTS * (CHUNK - 1)) // 256) * 256
BIG = 4
MAX_BIG = SORT_ROWS // (BIG * CHUNK)
MAX_SMALL = N_EXPERTS * (BIG - 1)
PER_STEP = 2
WAIT_GROUP = 8
ROUTE_OFF = N_GROUPS
VMEM_LIMIT = 56 * 1024 * 1024

BF16 = jnp.bfloat16
F32 = jnp.float32


def _layer_norm(x, g, b):
    mu = jnp.mean(x, axis=-1, keepdims=True)
    xc = x - mu
    var = jnp.mean(xc * xc, axis=-1, keepdims=True)
    return xc * lax.rsqrt(var + LN_EPS) * g + b


def _cast_bf16(dst_ref, src_ref):
    rows = 256
    for r0 in range(0, src_ref.shape[0], rows):
        dst_ref[r0:r0 + rows, :] = src_ref[r0:r0 + rows, :].astype(BF16)


def _dot_nt(a, b):
    return lax.dot_general(a, b, (((1,), (1,)), ((), ())), preferred_element_type=F32)


def _mixer_kernel(sinks_ref, x_ref, w_in_ref, b_in_ref, w_dw_ref, b_dw_ref, g_cn_ref, b_cn_ref,
                  w_out_ref, g1_ref, b1_ref, we_ref, o_ref, web_ref, w_in_b, w_out_b, hbuf, hshift, cbuf, qbuf,
                  kbuf, vbuf, ymix):
    i = pl.program_id(1)
    ts = SEQ_TILE
    web_ref[...] = we_ref[...].astype(BF16)

    @pl.when(jnp.logical_and(pl.program_id(0) == 0, i == 0))
    def _():
        _cast_bf16(w_in_b, w_in_ref)
        _cast_bf16(w_out_b, w_out_ref)

    @pl.when(i == 0)
    def _():
        hbuf[0:CONV_HALO, :] = jnp.zeros((CONV_HALO, CONV_CH), F32)
        kbuf[:, 0:WINDOW, :] = jnp.zeros((2 * N_KV_HEADS, WINDOW, KV_W), BF16)
        vbuf[:, 0:WINDOW, :] = jnp.zeros((2, WINDOW, KV_W), BF16)

    x = x_ref[0]
    u = jnp.dot(x.astype(BF16), w_in_b[...], preferred_element_type=F32) + b_in_ref[...]
    a = u[:, 0:CONV_CH]
    gate = u[:, CONV_CH:2 * CONV_CH]
    hbuf[CONV_HALO:CONV_HALO + ts, :] = a * jax.nn.sigmoid(gate)
    qbuf[...] = (u[:, 2 * CONV_CH:2 * CONV_CH + ATTN_W] * (HEAD_DIM ** -0.5 * LOG2E)).astype(BF16)
    kf = u[:, 2 * CONV_CH + ATTN_W:2 * CONV_CH + ATTN_W + KV_W]
    vf = u[:, 2 * CONV_CH + ATTN_W + KV_W:D_IN]
    kr = pltpu.roll(kf, HEAD_DIM, axis=1)
    vr = pltpu.roll(vf, HEAD_DIM, axis=1)
    lo = lax.broadcasted_iota(jnp.int32, (ts, KV_W), 1) < HEAD_DIM
    rows = slice(WINDOW, WINDOW + ts)
    kbuf[0, rows, :] = jnp.where(lo, kf, 0.0).astype(BF16)
    kbuf[1, rows, :] = jnp.where(lo, 0.0, kr).astype(BF16)
    kbuf[2, rows, :] = jnp.where(lo, kr, 0.0).astype(BF16)
    kbuf[3, rows, :] = jnp.where(lo, 0.0, kf).astype(BF16)
    vbuf[0, rows, :] = vf.astype(BF16)
    vbuf[1, rows, :] = vr.astype(BF16)

    base = CONV_HALO - (CONV_WIDTH - 1)
    n_shift = ts + CONV_HALO - SUBLANES
    for b in range(1, SUBLANES):
        hshift[b - 1, 0:n_shift, :] = hbuf[b:b + n_shift, :]
    rc = CONV_ROWS

    def conv_chunk(c):
        r0 = c * rc
        for l in range(CONV_CH // LANES):
            ls = slice(l * LANES, (l + 1) * LANES)
            acc = jnp.zeros((rc, LANES), F32)
            for j in range(CONV_WIDTH):
                a8, b = divmod(j + base, SUBLANES)
                rs = slice(r0 + SUBLANES * a8, r0 + SUBLANES * a8 + rc)
                tap = hbuf[rs, ls] if b == 0 else hshift[b - 1, rs, ls]
                acc = acc + tap * w_dw_ref[j:j + 1, ls]
            cbuf[r0:r0 + rc, ls] = acc
        for r1 in range(r0, r0 + rc, LN_ROWS):
            rs = slice(r1, r1 + LN_ROWS)
            y = _layer_norm(cbuf[rs, :] + b_dw_ref[...], g_cn_ref[...], b_cn_ref[...])
            y = y * jax.nn.sigmoid(y)
            ymix[rs, 0:CONV_CH] = y.astype(BF16)

    qi = lax.broadcasted_iota(jnp.int32, (2 * WINDOW, WINDOW), 0) % WINDOW
    kj = lax.broadcasted_iota(jnp.int32, (2 * WINDOW, WINDOW), 1)
    own = kj <= qi
    top = lax.broadcasted_iota(jnp.int32, (2 * WINDOW, 1), 0) < WINDOW
    lo_out = lax.broadcasted_iota(jnp.int32, (WINDOW, 2 * HEAD_DIM), 1) < HEAD_DIM

    def attn_block(jb):
        r0 = jb * WINDOW
        prev_ok = jnp.logical_not(own) & (i != 0) if jb == 0 else jnp.logical_not(own)
        for kvh in range(N_KV_HEADS):
            h0 = kvh * GQ
            c0 = h0 * HEAD_DIM
            qs = jnp.concatenate([qbuf[r0:r0 + WINDOW, c0:c0 + 2 * HEAD_DIM],
                                  qbuf[r0:r0 + WINDOW, c0 + 2 * HEAD_DIM:c0 + 4 * HEAD_DIM]], axis=0)
            pv = []
            for par in range(2):
                kk = kbuf[2 * kvh + par, r0:r0 + 2 * WINDOW, :]
                vv = vbuf[(kvh + par) % 2, r0:r0 + 2 * WINDOW, :]
                s2 = _dot_nt(qs, kk)
                prev_part = jnp.where(prev_ok, s2[:, :WINDOW], MASK_VALUE) if jb == 0 else s2[:, :WINDOW]
                s = jnp.where(own, s2[:, WINDOW:], prev_part)
                sink = jnp.where(top, sinks_ref[h0 + par] * LOG2E, sinks_ref[h0 + 2 + par] * LOG2E)
                m = jnp.maximum(jnp.max(s, axis=-1, keepdims=True), sink)
                p = jnp.exp2(s - m)
                denom = jnp.sum(p, axis=-1, keepdims=True) + jnp.exp2(sink - m)
                zero = jnp.zeros((), BF16)
                pb = p.astype(BF16)
                p2 = jnp.concatenate([jnp.where(own, zero, pb), jnp.where(own, pb, zero)], axis=1)
                pv.append(jnp.dot(p2, vv, preferred_element_type=F32) / denom)
            for pair in range(2):
                rs = slice(pair * WINDOW, (pair + 1) * WINDOW)
                o = jnp.where(lo_out, pv[0][rs], pv[1][rs])
                cs = CONV_CH + c0 + pair * 2 * HEAD_DIM
                ymix[r0:r0 + WINDOW, cs:cs + 2 * HEAD_DIM] = o.astype(BF16)

    assert ts // rc == ts // WINDOW
    for c in range(ts // rc):
        attn_block(c)
        conv_chunk(c)

    mix = jnp.dot(ymix[...], w_out_b[...], preferred_element_type=F32)
    o_ref[0] = _layer_norm(ALPHA * x + mix, g1_ref[...], b1_ref[...])

    hbuf[0:CONV_HALO, :] = hbuf[ts:ts + CONV_HALO, :]
    kbuf[:, 0:WINDOW, :] = kbuf[:, ts:ts + WINDOW, :]
    vbuf[:, 0:WINDOW, :] = vbuf[:, ts:ts + WINDOW, :]


def _const_spec(shape):
    nd = len(shape)
    return pl.BlockSpec(shape, lambda *_: (0,) * nd)


def _resident_spec(shape):
    nd = len(shape)
    return pl.BlockSpec(shape, lambda *_: (0,) * nd, pipeline_mode=pl.Buffered(1))


def _mixer(x, sinks, w_in, b_in, w_dw, b_dw, g_cn, b_cn, w_out, g1, b1, w_expert):
    B, S, D = x.shape
    ts = SEQ_TILE
    tile = pl.BlockSpec((1, ts, D), lambda b, i: (b, i, 0))
    nt = S // ts
    n_e, we_rows, we_cols = w_expert.shape
    per = (B * nt) // n_e
    assert per * n_e == B * nt and we_rows % per == 0
    wes = pl.BlockSpec((1, we_rows // per, we_cols), lambda b, i: ((b * nt + i) // per, (b * nt + i) % per, 0))
    return pl.pallas_call(
        _mixer_kernel,
        grid=(B, S // ts),
        in_specs=[
            pl.BlockSpec(memory_space=pltpu.SMEM),
            tile,
            _resident_spec((D, D_IN)), _const_spec((1, D_IN)),
            _const_spec((CONV_HALO, CONV_CH)), _const_spec((1, CONV_CH)),
            _const_spec((1, CONV_CH)), _const_spec((1, CONV_CH)),
            _resident_spec((D_MIX, D)), _const_spec((1, D)), _const_spec((1, D)),
            wes,
        ],
        out_specs=[tile, wes],
        out_shape=[jax.ShapeDtypeStruct((B, S, D), F32), jax.ShapeDtypeStruct(w_expert.shape, BF16)],
        scratch_shapes=[
            pltpu.VMEM((D, D_IN), BF16),
            pltpu.VMEM((D_MIX, D), BF16),
            pltpu.VMEM((CONV_HALO + ts, CONV_CH), F32),
            pltpu.VMEM((SUBLANES - 1, CONV_HALO + ts, CONV_CH), F32),
            pltpu.VMEM((ts, CONV_CH), F32),
            pltpu.VMEM((ts, ATTN_W), BF16),
            pltpu.VMEM((2 * N_KV_HEADS, WINDOW + ts, KV_W), BF16),
            pltpu.VMEM((2, WINDOW + ts, KV_W), BF16),
            pltpu.VMEM((ts, D_MIX), BF16),
        ],
        compiler_params=pltpu.CompilerParams(
            dimension_semantics=("arbitrary", "arbitrary"), vmem_limit_bytes=VMEM_LIMIT),
        name="mixer",
    )(sinks, x, w_in, b_in, w_dw, b_dw, g_cn, b_cn, w_out, g1, b1, w_expert)


def _memkv_kernel(mem_ref, w_ref, o_ref):
    o_ref[...] = jnp.dot(mem_ref[...].astype(BF16), w_ref[...].astype(BF16),
                         preferred_element_type=F32).astype(BF16)


def _memkv(mem2d, w_mkv):
    M, D = mem2d.shape
    N = w_mkv.shape[1]
    tn = MEMKV_COLS
    return pl.pallas_call(
        _memkv_kernel,
        grid=(N // tn,),
        in_specs=[pl.BlockSpec((M, D), lambda j: (0, 0)), pl.BlockSpec((D, tn), lambda j: (0, j))],
        out_specs=pl.BlockSpec((M, tn), lambda j: (0, j)),
        out_shape=jax.ShapeDtypeStruct((M, N), BF16),
        compiler_params=pltpu.CompilerParams(dimension_semantics=("arbitrary",)),
        name="memkv",
    )(mem2d, w_mkv)


def _first_max(rows):
    best = rows[0]
    for r in rows[1:]:
        best = jnp.maximum(best, r)
    idx = jnp.full(best.shape, len(rows) - 1, jnp.int32)
    for k in range(len(rows) - 2, -1, -1):
        idx = jnp.where(rows[k] == best, k, idx)
    return best, idx


def _route_plan(logits_t):
    tile = logits_t.shape[1]
    row = lambda k: logits_t[k:k + 1, :]
    gmax, g_idx = _first_max([row(g) for g in range(N_GROUPS)])
    gsum = jnp.exp(row(0) - gmax)
    for g in range(1, N_GROUPS):
        gsum = gsum + jnp.exp(row(g) - gmax)
    g_p = 1.0 / gsum
    rl = []
    for e in range(EXPERTS_PER_GROUP):
        v = row(ROUTE_OFF + (N_GROUPS - 1) * EXPERTS_PER_GROUP + e)
        for g in range(N_GROUPS - 2, -1, -1):
            v = jnp.where(g_idx == g, row(ROUTE_OFF + g * EXPERTS_PER_GROUP + e), v)
        rl.append(v)
    m1, i1 = _first_max(rl)
    m2, i2 = _first_max([jnp.where(i1 == e, MASK_VALUE, rl[e]) for e in range(EXPERTS_PER_GROUP)])
    ex = jnp.exp(m2 - m1)
    w1 = 1.0 / (1.0 + ex)
    w2 = ex * w1
    e1 = g_idx * EXPERTS_PER_GROUP + i1
    e2 = g_idx * EXPERTS_PER_GROUP + i2

    eid = lax.broadcasted_iota(jnp.int32, (N_EXPERTS, tile), 0)
    hit1 = eid == e1
    hit2 = eid == e2
    oh = jnp.where(jnp.logical_or(hit1, hit2), 1.0, 0.0)
    r = lax.broadcasted_iota(jnp.int32, (tile, tile), 0)
    c = lax.broadcasted_iota(jnp.int32, (tile, tile), 1)
    tri = jnp.where(r <= c, 1.0, 0.0).astype(BF16)
    csum = jnp.dot(oh.astype(BF16), tri, preferred_element_type=F32)
    counts = jnp.broadcast_to(csum[:, tile - 1:tile], (N_EXPERTS, tile)).astype(jnp.int32)
    nch = jnp.right_shift(counts + (CHUNK - 1), CHUNK.bit_length() - 1)
    er = lax.broadcasted_iota(jnp.int32, (N_EXPERTS, N_EXPERTS), 0)
    ec = lax.broadcasted_iota(jnp.int32, (N_EXPERTS, N_EXPERTS), 1)
    lower = jnp.where(ec < er, 1.0, 0.0).astype(BF16)
    off = jnp.dot(lower, nch.astype(F32).astype(BF16), preferred_element_type=F32) * CHUNK
    pos = off + csum - oh
    lp1 = jnp.sum(jnp.where(hit1, pos, 0.0), axis=0, keepdims=True)
    lp2 = jnp.sum(jnp.where(hit2, pos, 0.0), axis=0, keepdims=True)
    zero = jnp.zeros_like(lp1)
    route_t = jnp.concatenate([lp1, lp2, g_p * w1, g_p * w2, zero, zero, zero, zero], axis=0)
    meta = jnp.concatenate([nch[:, 0:LANES], off[:, 0:LANES].astype(jnp.int32)], axis=0)
    return route_t, meta


def _memattn_kernel(x_ref, wq_ref, k_ref, v_ref, wo_ref, g2_ref, b2_ref, wr2_ref, wrh_ref, br_ref, we_ref,
                    o_ref, ob_ref, route_ref, routet_ref, meta_ref, web_ref, wq_b, wo_b):
    web_ref[...] = we_ref[...].astype(BF16)

    @pl.when(jnp.logical_and(pl.program_id(0) == 0, pl.program_id(1) == 0))
    def _():
        _cast_bf16(wq_b, wq_ref)
        _cast_bf16(wo_b, wo_ref)

    def rows_logits(rs):
        x = x_ref[0, rs, :]
        q = jnp.dot(x.astype(BF16), wq_b[...], preferred_element_type=F32)
        q = (q * (MEM_HEAD_DIM ** -0.5 * LOG2E)).astype(BF16)
        outs = []
        for h in range(MEM_HEADS):
            sl = slice(h * MEM_HEAD_DIM, (h + 1) * MEM_HEAD_DIM)
            s = _dot_nt(q[:, sl], k_ref[0, :, sl])
            m = jnp.max(s, axis=-1, keepdims=True)
            p = jnp.exp2(s - m)
            denom = jnp.sum(p, axis=-1, keepdims=True)
            o = jnp.dot(p.astype(BF16), v_ref[0, :, sl], preferred_element_type=F32)
            outs.append((o / denom).astype(BF16))
        o = jnp.dot(jnp.concatenate(outs, axis=-1), wo_b[...], preferred_element_type=F32)
        x2 = _layer_norm(ALPHA * x + o, g2_ref[...], b2_ref[...])
        o_ref[0, rs, :] = x2
        x2h = x2.astype(BF16)
        ob_ref[0, rs, :] = x2h
        x2l = (x2 - x2h.astype(F32)).astype(BF16)
        hh = jnp.dot(x2h, wr2_ref[...], preferred_element_type=F32)
        return (hh[:, 0:LANES] + hh[:, LANES:2 * LANES]
                + jnp.dot(x2l, wrh_ref[...], preferred_element_type=F32) + br_ref[...])

    ts = x_ref.shape[1]
    groups = [slice(r0, r0 + MOE_TILE) for r0 in range(0, ts, MOE_TILE)]
    logits = rows_logits(slice(0, ts))
    pad = jnp.zeros((LANES - SUBLANES, MOE_TILE), F32)
    for k, rs in enumerate(groups):
        route_t, meta = _route_plan(jnp.transpose(logits[rs, :]))
        routet_ref[k] = route_t
        route_ref[0, rs, :] = jnp.transpose(jnp.concatenate([route_t, pad], axis=0))
        meta_ref[k] = meta


def _memattn(x1, wq, kvm, wo, g2, b2, wr, br, w_expert):
    B, S, D = x1.shape
    ts = MEM_TILE
    assert B * (S // ts) == w_expert.shape[0]
    wes = pl.BlockSpec((1,) + w_expert.shape[1:], lambda b, i: (b * (S // ts) + i, 0, 0))
    per = ts // MOE_TILE
    nt = S // ts
    c = wr * (2.0 ** 16 + 1.0)
    w_high = c - (c - wr)
    wrh = w_high.astype(BF16)
    wr2 = jnp.concatenate([wrh, (wr - w_high).astype(BF16)], axis=1)
    tile = pl.BlockSpec((1, ts, D), lambda b, i: (b, i, 0))
    kspec = pl.BlockSpec((1, MEM_LEN, D), lambda b, i: (b, 0, 0))
    vspec = pl.BlockSpec((1, MEM_LEN, D), lambda b, i: (b, 0, 1))
    return pl.pallas_call(
        _memattn_kernel,
        grid=(B, nt),
        in_specs=[tile, _resident_spec((D, D)), kspec, vspec, _resident_spec((D, D)),
                  _const_spec((1, D)), _const_spec((1, D)),
                  _const_spec((D, 2 * LANES)), _const_spec((D, LANES)), _const_spec((1, LANES)), wes],
        out_specs=[tile, tile, pl.BlockSpec((1, ts, LANES), lambda b, i: (b, i, 0)),
                   pl.BlockSpec((per, SUBLANES, MOE_TILE), lambda b, i: (b * nt + i, 0, 0)),
                   pl.BlockSpec((per, 2 * N_EXPERTS, LANES), lambda b, i: (b * nt + i, 0, 0)), wes],
        out_shape=[jax.ShapeDtypeStruct((B, S, D), F32),
                   jax.ShapeDtypeStruct((B, S, D), BF16),
                   jax.ShapeDtypeStruct((B, S, LANES), F32),
                   jax.ShapeDtypeStruct((B * nt * per, SUBLANES, MOE_TILE), F32),
                   jax.ShapeDtypeStruct((B * nt * per, 2 * N_EXPERTS, LANES), jnp.int32),
                   jax.ShapeDtypeStruct(w_expert.shape, BF16)],
        scratch_shapes=[pltpu.VMEM((D, D), BF16), pltpu.VMEM((D, D), BF16)],
        compiler_params=pltpu.CompilerParams(
            dimension_semantics=("arbitrary", "arbitrary"), vmem_limit_bytes=VMEM_LIMIT),
        name="memattn",
    )(x1, wq, kvm, kvm, wo, g2, b2, wr2, wrh, br, w_expert)


def _gmm_blocks(n_tokens):
    rows = (n_tokens * TOP_K + (n_tokens // MOE_TILE) * N_EXPERTS * (CHUNK - 1)
            + N_EXPERTS * (REGION_ALIGN - CHUNK))
    return -(-rows // GMM_ROWS)


def _plan(meta, T):
    nch = meta[:, :N_EXPERTS, 0]
    n16 = nch * CHUNK
    n_e = jnp.sum(n16, axis=0)
    reg = (n_e + REGION_ALIGN - 1) // REGION_ALIGN * REGION_ALIGN
    gend = jnp.cumsum(reg)
    gbase = gend - reg
    dst = gbase[None, :] + jnp.cumsum(n16, axis=0) - n16
    half_row = jnp.arange(2 * _gmm_blocks(T), dtype=jnp.int32) * REGION_ALIGN
    half_expert = jnp.minimum(jnp.sum(half_row[:, None] >= gend[None, :], axis=1), N_EXPERTS - 1)
    half_used = half_row < gend[-1]
    n_used = ((gend[-1] + GMM_ROWS - 1) // GMM_ROWS).astype(jnp.int32).reshape(1)
    src = (jnp.cumsum(nch, axis=1) - nch) * CHUNK
    n_big = nch // BIG

    def copy_list(count, src0, dst0, rows, length):
        cum = jnp.cumsum(count, axis=1)
        first = (cum - count)[:, None, :]
        k = jnp.arange(length, dtype=jnp.int32)[None, :, None]
        mine = (k >= first) & (k < cum[:, None, :])
        step = (k - first) * rows
        pick = lambda base: jnp.sum(jnp.where(mine, base[:, None, :] + step, 0), axis=2)
        return pick(src0), pick(dst0), cum[:, -1]

    big_src, big_dst, big_n = copy_list(n_big, src, dst, BIG * CHUNK, MAX_BIG)
    rest = n_big * (BIG * CHUNK)
    small_src, small_dst, small_n = copy_list(nch - n_big * BIG, src + rest, dst + rest, CHUNK, MAX_SMALL)
    i32 = lambda a: a.astype(jnp.int32)
    fill_start = jnp.concatenate([gbase + n_e, gend[-1:]])
    fill_n = jnp.concatenate([reg - n_e, _gmm_blocks(T) * GMM_ROWS - gend[-1:]]) // CHUNK
    return dict(big_src=i32(big_src).reshape(-1), big_dst=i32(big_dst).reshape(-1), big_n=i32(big_n),
                small_src=i32(small_src).reshape(-1), small_dst=i32(small_dst).reshape(-1), small_n=i32(small_n),
                fill_start=i32(fill_start), fill_n=i32(fill_n),
                fill_tot=i32(jnp.sum(fill_n)).reshape(1), half_expert=i32(half_expert),
                half_used=i32(half_used), n_used=n_used)


def _rows_copy(src_ref, src_row, dst_ref, dst_row, rows, sem):
    return pltpu.make_async_copy(
        src_ref.at[pl.ds(pl.multiple_of(src_row, CHUNK), rows), :],
        dst_ref.at[pl.ds(pl.multiple_of(dst_row, CHUNK), rows), :], sem)


def _chunk_copy(src_ref, src_row, dst_ref, dst_row, sem):
    return _rows_copy(src_ref, src_row, dst_ref, dst_row, CHUNK, sem)


def _issue_copies(lists, tile, tile_ref, tile_is_src, hbm_ref, sem):
    big_src, big_dst, big_n, small_src, small_dst, small_n = lists
    for src_l, dst_l, n_l, length, rows in ((big_src, big_dst, big_n, MAX_BIG, BIG * CHUNK),
                                            (small_src, small_dst, small_n, MAX_SMALL, CHUNK)):
        def issue(k, carry, src_l=src_l, dst_l=dst_l, length=length, rows=rows):
            local, remote = src_l[tile * length + k], dst_l[tile * length + k]
            if tile_is_src:
                _rows_copy(tile_ref, local, hbm_ref, remote, rows, sem).start()
            else:
                _rows_copy(hbm_ref, remote, tile_ref, local, rows, sem).start()
            return carry

        lax.fori_loop(0, n_l[tile], issue, 0)


def _wait_rows(n, rows, src_ref, dst_ref, sem):
    def body(c, carry):
        pltpu.make_async_copy(src_ref.at[pl.ds(0, rows), :], dst_ref.at[pl.ds(0, rows), :], sem).wait()
        return carry

    lax.fori_loop(0, n, body, 0)


def _wait_copies(lists, tile, src_ref, dst_ref, sem, extra_chunks=0):
    _wait_rows(lists[2][tile], BIG * CHUNK, src_ref, dst_ref, sem)
    n = lists[5][tile] + extra_chunks
    _wait_rows(n // WAIT_GROUP, WAIT_GROUP * CHUNK, src_ref, dst_ref, sem)
    _wait_rows(n % WAIT_GROUP, CHUNK, src_ref, dst_ref, sem)


def _dispatch_kernel(bs_ref, bd_ref, bn_ref, ss_ref, sd_ref, sn_ref, fstart_ref, fn_ref, ftot_ref,
                     lp_ref, x_ref, we_ref, xs_hbm, web_ref, xt, zbuf, sems):
    lists = (bs_ref, bd_ref, bn_ref, ss_ref, sd_ref, sn_ref)
    s = pl.program_id(0)
    ns = pl.num_programs(0)
    cur = (s % 2) * PER_STEP
    prv = PER_STEP - cur

    @pl.when(s >= 2)
    def _():
        for k in range(PER_STEP):
            _wait_copies(lists, (s - 2) * PER_STEP + k, xt.at[cur + k], xs_hbm, sems.at[cur + k])

    r = lax.broadcasted_iota(jnp.int32, (SORT_ROWS, MOE_TILE), 0)
    for k in range(PER_STEP):
        lp = lp_ref[k].astype(jnp.int32)
        hit = jnp.logical_or(lp[0:1, :] == r, lp[1:2, :] == r)
        p = jnp.where(hit, 1.0, 0.0).astype(BF16)
        x = x_ref[k * MOE_TILE:(k + 1) * MOE_TILE, :]
        xt[cur + k] = jnp.dot(p, x, preferred_element_type=F32).astype(BF16)

    for k in range(PER_STEP):
        _issue_copies(lists, s * PER_STEP + k, xt.at[cur + k], True, xs_hbm, sems.at[cur + k])

    _cast_bf16(web_ref.at[0], we_ref.at[0])

    @pl.when(s == ns - 1)
    def _():
        zbuf[...] = jnp.zeros_like(zbuf)
        sem = sems.at[cur]

        def per_range(e, carry):
            def issue(c, carry2):
                _chunk_copy(zbuf, 0, xs_hbm, fstart_ref[e] + c * CHUNK, sem).start()
                return carry2

            return lax.fori_loop(0, fn_ref[e], issue, carry)

        lax.fori_loop(0, N_EXPERTS + 1, per_range, 0)
        for k in range(PER_STEP):
            _wait_copies(lists, (s - 1) * PER_STEP + k, xt.at[prv + k], xs_hbm, sems.at[prv + k])
            _wait_copies(lists, s * PER_STEP + k, xt.at[cur + k], xs_hbm, sems.at[cur + k],
                         extra_chunks=ftot_ref[0] if k == 0 else 0)


def _copy_lists(plan):
    return tuple(plan[k] for k in ("big_src", "big_dst", "big_n", "small_src", "small_dst", "small_n"))


def _dispatch(plan, lpt, x2b, w_expert):
    T, D = x2b.shape
    rows = PER_STEP * MOE_TILE
    assert T // rows >= 2
    assert T // rows == N_EXPERTS
    wes = pl.BlockSpec((1,) + w_expert.shape[1:], lambda t, *_: (t, 0, 0))
    grid_spec = pltpu.PrefetchScalarGridSpec(
        num_scalar_prefetch=9,
        grid=(T // rows,),
        in_specs=[pl.BlockSpec((PER_STEP, SUBLANES, MOE_TILE), lambda t, *_: (t, 0, 0)),
                  pl.BlockSpec((rows, D), lambda t, *_: (t, 0)), wes],
        out_specs=[pl.BlockSpec(memory_space=pl.ANY), wes],
        scratch_shapes=[pltpu.VMEM((2 * PER_STEP, SORT_ROWS, D), BF16), pltpu.VMEM((CHUNK, D), BF16),
                        pltpu.SemaphoreType.DMA((2 * PER_STEP,))],
    )
    return pl.pallas_call(
        _dispatch_kernel,
        grid_spec=grid_spec,
        out_shape=[jax.ShapeDtypeStruct((_gmm_blocks(T) * GMM_ROWS, D), BF16),
                   jax.ShapeDtypeStruct(w_expert.shape, BF16)],
        compiler_params=pltpu.CompilerParams(
            dimension_semantics=("arbitrary",), vmem_limit_bytes=VMEM_LIMIT),
        name="moe_dispatch",
    )(*_copy_lists(plan), plan["fill_start"], plan["fill_n"], plan["fill_tot"], lpt, x2b, w_expert)


def _expert_ffn(xb, wg_b, wu_b, wd_b):
    g = jnp.dot(xb, wg_b[...], preferred_element_type=F32)
    u = jnp.dot(xb, wu_b[...], preferred_element_type=F32)
    h = (g * jax.nn.sigmoid(g)) * u
    return jnp.dot(h.astype(BF16), wd_b[...], preferred_element_type=F32).astype(BF16)


def _gmm_kernel(he_ref, hu_ref, nu_ref, x_ref, wga_ref, wua_ref, wda_ref, wgb_ref, wub_ref, wdb_ref, o_ref):
    b = pl.program_id(0)
    half = REGION_ALIGN
    same = he_ref[2 * b] == he_ref[2 * b + 1]
    used0 = hu_ref[2 * b] != 0
    used1 = hu_ref[2 * b + 1] != 0
    set_a = (wga_ref.at[0], wua_ref.at[0], wda_ref.at[0])
    set_b = (wgb_ref.at[0], wub_ref.at[0], wdb_ref.at[0])

    @pl.when(jnp.logical_and(used1, same))
    def _():
        o_ref[...] = _expert_ffn(x_ref[...], *set_a)

    @pl.when(jnp.logical_and(used0, jnp.logical_not(jnp.logical_and(used1, same))))
    def _():
        o_ref[0:half, :] = _expert_ffn(x_ref[0:half, :], *set_a)

    @pl.when(jnp.logical_and(used1, jnp.logical_not(same)))
    def _():
        o_ref[half:GMM_ROWS, :] = _expert_ffn(x_ref[half:GMM_ROWS, :], *set_b)

    @pl.when(jnp.logical_not(used0))
    def _():
        o_ref[0:half, :] = jnp.zeros((half, o_ref.shape[1]), o_ref.dtype)

    @pl.when(jnp.logical_not(used1))
    def _():
        o_ref[half:GMM_ROWS, :] = jnp.zeros((GMM_ROWS - half, o_ref.shape[1]), o_ref.dtype)


def _gmm(plan, xs, wg, wu, wd):
    R, D = xs.shape
    rows = pl.BlockSpec((GMM_ROWS, D), lambda b, he, hu, nu: (jnp.minimum(b, nu[0] - 1), 0))

    def wspec(shape, h):
        return pl.BlockSpec((1,) + shape, lambda b, he, hu, nu: (he[2 * b + h], 0, 0))

    grid_spec = pltpu.PrefetchScalarGridSpec(
        num_scalar_prefetch=3,
        grid=(R // GMM_ROWS,),
        in_specs=[rows,
                  wspec((D, D_EXPERT), 0), wspec((D, D_EXPERT), 0), wspec((D_EXPERT, D), 0),
                  wspec((D, D_EXPERT), 1), wspec((D, D_EXPERT), 1), wspec((D_EXPERT, D), 1)],
        out_specs=pl.BlockSpec((GMM_ROWS, D), lambda b, he, hu, nu: (b, 0)),
    )
    return pl.pallas_call(
        _gmm_kernel,
        grid_spec=grid_spec,
        out_shape=jax.ShapeDtypeStruct((R, D), BF16),
        compiler_params=pltpu.CompilerParams(
            dimension_semantics=("arbitrary",), vmem_limit_bytes=VMEM_LIMIT),
        name="moe_gmm",
    )(plan["half_expert"], plan["half_used"], plan["n_used"], xs, wg, wu, wd, wg, wu, wd)


def _combine_kernel(bs_ref, bd_ref, bn_ref, ss_ref, sd_ref, sn_ref, cm_ref, x_ref, ys_hbm, g3_ref, b3_ref,
                    o_ref, yt, sems):
    lists = (bs_ref, bd_ref, bn_ref, ss_ref, sd_ref, sn_ref)
    s = pl.program_id(0)
    ns = pl.num_programs(0)
    cur = (s % 2) * PER_STEP
    nxt = PER_STEP - cur

    def fetch(step, first_buf):
        for k in range(PER_STEP):
            _issue_copies(lists, step * PER_STEP + k, yt.at[first_buf + k], False, ys_hbm,
                          sems.at[first_buf + k])

    @pl.when(s == 0)
    def _():
        yt[...] = jnp.zeros_like(yt)
        fetch(0, 0)

    @pl.when(s + 1 < ns)
    def _():
        fetch(s + 1, nxt)

    for k in range(PER_STEP):
        _wait_copies(lists, s * PER_STEP + k, ys_hbm, yt.at[cur + k], sems.at[cur + k])

    col = lax.broadcasted_iota(jnp.int32, (MOE_TILE, SORT_ROWS), 1)
    for k in range(PER_STEP):
        rs = slice(k * MOE_TILE, (k + 1) * MOE_TILE)
        cm = cm_ref[rs, :]
        lp0 = cm[:, 0:1].astype(jnp.int32)
        lp1 = cm[:, 1:2].astype(jnp.int32)
        w = jnp.where(col == lp0, cm[:, 2:3], jnp.where(col == lp1, cm[:, 3:4], 0.0)).astype(BF16)
        y = jnp.dot(w, yt[cur + k], preferred_element_type=F32)
        o_ref[rs, :] = _layer_norm(ALPHA * x_ref[rs, :] + y, g3_ref[...], b3_ref[...])


def _combine(plan, route, x2, ys, g3, b3):
    T, D = x2.shape
    rows = PER_STEP * MOE_TILE
    grid_spec = pltpu.PrefetchScalarGridSpec(
        num_scalar_prefetch=6,
        grid=(T // rows,),
        in_specs=[pl.BlockSpec((rows, LANES), lambda t, *_: (t, 0)),
                  pl.BlockSpec((rows, D), lambda t, *_: (t, 0)),
                  pl.BlockSpec(memory_space=pl.ANY),
                  pl.BlockSpec((1, D), lambda t, *_: (0, 0)),
                  pl.BlockSpec((1, D), lambda t, *_: (0, 0))],
        out_specs=pl.BlockSpec((rows, D), lambda t, *_: (t, 0)),
        scratch_shapes=[pltpu.VMEM((2 * PER_STEP, SORT_ROWS, D), BF16),
                        pltpu.SemaphoreType.DMA((2 * PER_STEP,))],
    )
    return pl.pallas_call(
        _combine_kernel,
        grid_spec=grid_spec,
        out_shape=jax.ShapeDtypeStruct((T, D), F32),
        compiler_params=pltpu.CompilerParams(
            dimension_semantics=("arbitrary",), vmem_limit_bytes=VMEM_LIMIT),
        name="moe_combine",
    )(*_copy_lists(plan), route, x2, ys, g3, b3)


def _moe(x2, x2b, route, lpt, meta, wg, wu_b, wd_b, g3, b3):
    plan = _plan(meta, x2.shape[0])
    xs, wg_b = _dispatch(plan, lpt, x2b, wg)
    ys = _gmm(plan, xs, wg_b, wu_b, wd_b)
    return _combine(plan, route, x2, ys, g3, b3)


def _row(v):
    return v.reshape(1, -1).astype(F32)


def kernel(x, mem, w_in, b_in, w_dw, b_dw, g_conv_norm, b_conv_norm, attn_sinks, w_out, g_ln1, b_ln1,
           w_mq, w_mkv, w_mo, g_ln2, b_ln2, w_group, b_group, w_router, b_router, w_gate, w_up, w_down,
           g_ln3, b_ln3):
    B, S, D = x.shape
    for l in range(DEPTH):
        w_dw_p = jnp.zeros((CONV_HALO, CONV_CH), F32).at[:CONV_WIDTH].set(w_dw[l])
        x1, wd_b = _mixer(x, attn_sinks[l].astype(F32), w_in[l], _row(b_in[l]), w_dw_p,
                          _row(b_dw[l]), _row(g_conv_norm[l]), _row(b_conv_norm[l]),
                          w_out[l], _row(g_ln1[l]), _row(b_ln1[l]), w_down[l])

        kvm = _memkv(mem.reshape(B * MEM_LEN, D), w_mkv[l]).reshape(B, MEM_LEN, 2 * D)

        wr = jnp.concatenate(
            [w_group[l], jnp.transpose(w_router[l], (1, 0, 2)).reshape(D, N_EXPERTS)], axis=1)
        wr = jnp.pad(wr, ((0, 0), (0, LANES - wr.shape[1])))
        br = jnp.pad(jnp.concatenate([b_group[l], b_router[l].reshape(-1)]), (0, LANES - N_GROUPS - N_EXPERTS))
        x2, x2b, route, lpt, meta, wu_b = _memattn(x1, w_mq[l], kvm, w_mo[l], _row(g_ln2[l]), _row(b_ln2[l]),
                                                   wr.astype(F32), _row(br), w_up[l])

        T = B * S
        y = _moe(x2.reshape(T, D), x2b.reshape(T, D), route.reshape(T, LANES), lpt, meta,
                 w_gate[l], wu_b, wd_b,
                 _row(g_ln3[l]), _row(b_ln3[l]))
        x = y.reshape(B, S, D)
    return x
```

```python
import jax
import jax.numpy as jnp
from jax import lax
from jax.experimental import pallas as pl
from jax.experimental.pallas import tpu as pltpu

D_MODEL = 1024
MEM_LEN = 256
CONV_CH = 512
CONV_WIDTH = 31
N_HEADS = 8
N_KV_HEADS = 2
HEAD_DIM = 64
GQ = N_HEADS // N_KV_HEADS
ATTN_W = N_HEADS * HEAD_DIM
KV_W = N_KV_HEADS * HEAD_DIM
WINDOW = 128
D_MIX = CONV_CH + ATTN_W
D_IN = 2 * CONV_CH + ATTN_W + 2 * KV_W
MEM_HEADS = 4
MEM_HEAD_DIM = D_MODEL // MEM_HEADS
N_GROUPS = 4
EXPERTS_PER_GROUP = 4
N_EXPERTS = N_GROUPS * EXPERTS_PER_GROUP
D_EXPERT = D_MODEL // 2
DEPTH = 1
ALPHA = (2.0 * DEPTH) ** 0.25
LN_EPS = 1e-5

LANES = 128
SUBLANES = 8
CONV_ROWS = 128
LN_ROWS = 64
MEMKV_COLS = 1024
MEM_TILE = 1024
LOG2E = 1.4426950408889634
MASK_VALUE = -1e30
CONV_HALO = 32
SEQ_TILE = 512
MOE_TILE = 512
CHUNK = 16
REGION_ALIGN = 512
GMM_ROWS = 2 * REGION_ALIGN
TOP_K = 2
SORT_ROWS = -(-(MOE_TILE * TOP_K + N_EXPERTS * (CHUNK - 1)) // 256) * 256
BIG = 4
MAX_BIG = SORT_ROWS // (BIG * CHUNK)
MAX_SMALL = N_EXPERTS * (BIG - 1)
PER_STEP = 2
WAIT_GROUP = 8
ROUTE_OFF = N_GROUPS
VMEM_LIMIT = 56 * 1024 * 1024

BF16 = jnp.bfloat16
F32 = jnp.float32


def _layer_norm(x, g, b):
    mu = jnp.mean(x, axis=-1, keepdims=True)
    xc = x - mu
    var = jnp.mean(xc * xc, axis=-1, keepdims=True)
    return xc * lax.rsqrt(var + LN_EPS) * g + b


def _cast_bf16(dst_ref, src_ref):
    rows = 256
    for r0 in range(0, src_ref.shape[0], rows):
        dst_ref[r0:r0 + rows, :] = src_ref[r0:r0 + rows, :].astype(BF16)


def _dot_nt(a, b):
    return lax.dot_general(a, b, (((1,), (1,)), ((), ())), preferred_element_type=F32)


def _mixer_kernel(sinks_ref, x_ref, w_in_ref, b_in_ref, w_dw_ref, b_dw_ref, g_cn_ref, b_cn_ref,
                  w_out_ref, g1_ref, b1_ref, we_ref, o_ref, web_ref, w_in_b, w_out_b, hbuf, hshift, cbuf, qbuf,
                  kbuf, vbuf, ymix):
    i = pl.program_id(1)
    ts = SEQ_TILE
    web_ref[...] = we_ref[...].astype(BF16)

    @pl.when(jnp.logical_and(pl.program_id(0) == 0, i == 0))
    def _():
        _cast_bf16(w_in_b, w_in_ref)
        _cast_bf16(w_out_b, w_out_ref)

    @pl.when(i == 0)
    def _():
        hbuf[0:CONV_HALO, :] = jnp.zeros((CONV_HALO, CONV_CH), F32)
        kbuf[:, 0:WINDOW, :] = jnp.zeros((2 * N_KV_HEADS, WINDOW, KV_W), BF16)
        vbuf[:, 0:WINDOW, :] = jnp.zeros((2, WINDOW, KV_W), BF16)

    x = x_ref[0]
    u = jnp.dot(x.astype(BF16), w_in_b[...], preferred_element_type=F32) + b_in_ref[...]
    a = u[:, 0:CONV_CH]
    gate = u[:, CONV_CH:2 * CONV_CH]
    hbuf[CONV_HALO:CONV_HALO + ts, :] = a * jax.nn.sigmoid(gate)
    qbuf[...] = (u[:, 2 * CONV_CH:2 * CONV_CH + ATTN_W] * (HEAD_DIM ** -0.5 * LOG2E)).astype(BF16)
    kf = u[:, 2 * CONV_CH + ATTN_W:2 * CONV_CH + ATTN_W + KV_W]
    vf = u[:, 2 * CONV_CH + ATTN_W + KV_W:D_IN]
    kr = pltpu.roll(kf, HEAD_DIM, axis=1)
    vr = pltpu.roll(vf, HEAD_DIM, axis=1)
    lo = lax.broadcasted_iota(jnp.int32, (ts, KV_W), 1) < HEAD_DIM
    rows = slice(WINDOW, WINDOW + ts)
    kbuf[0, rows, :] = jnp.where(lo, kf, 0.0).astype(BF16)
    kbuf[1, rows, :] = jnp.where(lo, 0.0, kr).astype(BF16)
    kbuf[2, rows, :] = jnp.where(lo, kr, 0.0).astype(BF16)
    kbuf[3, rows, :] = jnp.where(lo, 0.0, kf).astype(BF16)
    vbuf[0, rows, :] = vf.astype(BF16)
    vbuf[1, rows, :] = vr.astype(BF16)

    base = CONV_HALO - (CONV_WIDTH - 1)
    n_shift = ts + CONV_HALO - SUBLANES
    for b in range(1, SUBLANES):
        hshift[b - 1, 0:n_shift, :] = hbuf[b:b + n_shift, :]
    rc = CONV_ROWS

    def conv_chunk(c):
        r0 = c * rc
        for l in range(CONV_CH // LANES):
            ls = slice(l * LANES, (l + 1) * LANES)
            acc = jnp.zeros((rc, LANES), F32)
            for j in range(CONV_WIDTH):
                a8, b = divmod(j + base, SUBLANES)
                rs = slice(r0 + SUBLANES * a8, r0 + SUBLANES * a8 + rc)
                tap = hbuf[rs, ls] if b == 0 else hshift[b - 1, rs, ls]
                acc = acc + tap * w_dw_ref[j:j + 1, ls]
            cbuf[r0:r0 + rc, ls] = acc
        for r1 in range(r0, r0 + rc, LN_ROWS):
            rs = slice(r1, r1 + LN_ROWS)
            y = _layer_norm(cbuf[rs, :] + b_dw_ref[...], g_cn_ref[...], b_cn_ref[...])
            y = y * jax.nn.sigmoid(y)
            ymix[rs, 0:CONV_CH] = y.astype(BF16)

    qi = lax.broadcasted_iota(jnp.int32, (2 * WINDOW, WINDOW), 0) % WINDOW
    kj = lax.broadcasted_iota(jnp.int32, (2 * WINDOW, WINDOW), 1)
    own = kj <= qi
    top = lax.broadcasted_iota(jnp.int32, (2 * WINDOW, 1), 0) < WINDOW
    lo_out = lax.broadcasted_iota(jnp.int32, (WINDOW, 2 * HEAD_DIM), 1) < HEAD_DIM

    def attn_block(jb):
        r0 = jb * WINDOW
        prev_ok = jnp.logical_not(own) & (i != 0) if jb == 0 else jnp.logical_not(own)
        for kvh in range(N_KV_HEADS):
            h0 = kvh * GQ
            c0 = h0 * HEAD_DIM
            qs = jnp.concatenate([qbuf[r0:r0 + WINDOW, c0:c0 + 2 * HEAD_DIM],
                                  qbuf[r0:r0 + WINDOW, c0 + 2 * HEAD_DIM:c0 + 4 * HEAD_DIM]], axis=0)
            pv = []
            for par in range(2):
                kk = kbuf[2 * kvh + par, r0:r0 + 2 * WINDOW, :]
                vv = vbuf[(kvh + par) % 2, r0:r0 + 2 * WINDOW, :]
                s2 = _dot_nt(qs, kk)
                prev_part = jnp.where(prev_ok, s2[:, :WINDOW], MASK_VALUE) if jb == 0 else s2[:, :WINDOW]
                s = jnp.where(own, s2[:, WINDOW:], prev_part)
                sink = jnp.where(top, sinks_ref[h0 + par] * LOG2E, sinks_ref[h0 + 2 + par] * LOG2E)
                m = jnp.maximum(jnp.max(s, axis=-1, keepdims=True), sink)
                p = jnp.exp2(s - m)
                denom = jnp.sum(p, axis=-1, keepdims=True) + jnp.exp2(sink - m)
                zero = jnp.zeros((), BF16)
                pb = p.astype(BF16)
                p2 = jnp.concatenate([jnp.where(own, zero, pb), jnp.where(own, pb, zero)], axis=1)
                pv.append(jnp.dot(p2, vv, preferred_element_type=F32) / denom)
            for pair in range(2):
                rs = slice(pair * WINDOW, (pair + 1) * WINDOW)
                o = jnp.where(lo_out, pv[0][rs], pv[1][rs])
                cs = CONV_CH + c0 + pair * 2 * HEAD_DIM
                ymix[r0:r0 + WINDOW, cs:cs + 2 * HEAD_DIM] = o.astype(BF16)

    assert ts // rc == ts // WINDOW
    for c in range(ts // rc):
        attn_block(c)
        conv_chunk(c)

    mix = jnp.dot(ymix[...], w_out_b[...], preferred_element_type=F32)
    o_ref[0] = _layer_norm(ALPHA * x + mix, g1_ref[...], b1_ref[...])

    hbuf[0:CONV_HALO, :] = hbuf[ts:ts + CONV_HALO, :]
    kbuf[:, 0:WINDOW, :] = kbuf[:, ts:ts + WINDOW, :]
    vbuf[:, 0:WINDOW, :] = vbuf[:, ts:ts + WINDOW, :]


def _const_spec(shape):
    nd = len(shape)
    return pl.BlockSpec(shape, lambda *_: (0,) * nd)


def _resident_spec(shape):
    nd = len(shape)
    return pl.BlockSpec(shape, lambda *_: (0,) * nd, pipeline_mode=pl.Buffered(1))


def _mixer(x, sinks, w_in, b_in, w_dw, b_dw, g_cn, b_cn, w_out, g1, b1, w_expert):
    B, S, D = x.shape
    ts = SEQ_TILE
    tile = pl.BlockSpec((1, ts, D), lambda b, i: (b, i, 0))
    nt = S // ts
    n_e, we_rows, we_cols = w_expert.shape
    per = (B * nt) // n_e
    assert per * n_e == B * nt and we_rows % per == 0
    wes = pl.BlockSpec((1, we_rows // per, we_cols), lambda b, i: ((b * nt + i) // per, (b * nt + i) % per, 0))
    return pl.pallas_call(
        _mixer_kernel,
        grid=(B, S // ts),
        in_specs=[
            pl.BlockSpec(memory_space=pltpu.SMEM),
            tile,
            _resident_spec((D, D_IN)), _const_spec((1, D_IN)),
            _const_spec((CONV_HALO, CONV_CH)), _const_spec((1, CONV_CH)),
            _const_spec((1, CONV_CH)), _const_spec((1, CONV_CH)),
            _resident_spec((D_MIX, D)), _const_spec((1, D)), _const_spec((1, D)),
            wes,
        ],
        out_specs=[tile, wes],
        out_shape=[jax.ShapeDtypeStruct((B, S, D), F32), jax.ShapeDtypeStruct(w_expert.shape, BF16)],
        scratch_shapes=[
            pltpu.VMEM((D, D_IN), BF16),
            pltpu.VMEM((D_MIX, D), BF16),
            pltpu.VMEM((CONV_HALO + ts, CONV_CH), F32),
            pltpu.VMEM((SUBLANES - 1, CONV_HALO + ts, CONV_CH), F32),
            pltpu.VMEM((ts, CONV_CH), F32),
            pltpu.VMEM((ts, ATTN_W), BF16),
            pltpu.VMEM((2 * N_KV_HEADS, WINDOW + ts, KV_W), BF16),
            pltpu.VMEM((2, WINDOW + ts, KV_W), BF16),
            pltpu.VMEM((ts, D_MIX), BF16),
        ],
        compiler_params=pltpu.CompilerParams(
            dimension_semantics=("arbitrary", "arbitrary"), vmem_limit_bytes=VMEM_LIMIT),
        name="mixer",
    )(sinks, x, w_in, b_in, w_dw, b_dw, g_cn, b_cn, w_out, g1, b1, w_expert)


def _memkv_kernel(mem_ref, w_ref, o_ref):
    o_ref[...] = jnp.dot(mem_ref[...].astype(BF16), w_ref[...].astype(BF16),
                         preferred_element_type=F32).astype(BF16)


def _memkv(mem2d, w_mkv):
    M, D = mem2d.shape
    N = w_mkv.shape[1]
    tn = MEMKV_COLS
    return pl.pallas_call(
        _memkv_kernel,
        grid=(N // tn,),
        in_specs=[pl.BlockSpec((M, D), lambda j: (0, 0)), pl.BlockSpec((D, tn), lambda j: (0, j))],
        out_specs=pl.BlockSpec((M, tn), lambda j: (0, j)),
        out_shape=jax.ShapeDtypeStruct((M, N), BF16),
        compiler_params=pltpu.CompilerParams(dimension_semantics=("arbitrary",)),
        name="memkv",
    )(mem2d, w_mkv)


def _first_max(rows):
    best = rows[0]
    for r in rows[1:]:
        best = jnp.maximum(best, r)
    idx = jnp.full(best.shape, len(rows) - 1, jnp.int32)
    for k in range(len(rows) - 2, -1, -1):
        idx = jnp.where(rows[k] == best, k, idx)
    return best, idx


def _route_plan(logits_t):
    tile = logits_t.shape[1]
    row = lambda k: logits_t[k:k + 1, :]
    gmax, g_idx = _first_max([row(g) for g in range(N_GROUPS)])
    gsum = jnp.exp(row(0) - gmax)
    for g in range(1, N_GROUPS):
        gsum = gsum + jnp.exp(row(g) - gmax)
    g_p = 1.0 / gsum
    rl = []
    for e in range(EXPERTS_PER_GROUP):
        v = row(ROUTE_OFF + (N_GROUPS - 1) * EXPERTS_PER_GROUP + e)
        for g in range(N_GROUPS - 2, -1, -1):
            v = jnp.where(g_idx == g, row(ROUTE_OFF + g * EXPERTS_PER_GROUP + e), v)
        rl.append(v)
    m1, i1 = _first_max(rl)
    m2, i2 = _first_max([jnp.where(i1 == e, MASK_VALUE, rl[e]) for e in range(EXPERTS_PER_GROUP)])
    ex = jnp.exp(m2 - m1)
    w1 = 1.0 / (1.0 + ex)
    w2 = ex * w1
    e1 = g_idx * EXPERTS_PER_GROUP + i1
    e2 = g_idx * EXPERTS_PER_GROUP + i2

    eid = lax.broadcasted_iota(jnp.int32, (N_EXPERTS, tile), 0)
    hit1 = eid == e1
    hit2 = eid == e2
    oh = jnp.where(jnp.logical_or(hit1, hit2), 1.0, 0.0)
    r = lax.broadcasted_iota(jnp.int32, (tile, tile), 0)
    c = lax.broadcasted_iota(jnp.int32, (tile, tile), 1)
    tri = jnp.where(r <= c, 1.0, 0.0).astype(BF16)
    csum = jnp.dot(oh.astype(BF16), tri, preferred_element_type=F32)
    counts = jnp.broadcast_to(csum[:, tile - 1:tile], (N_EXPERTS, tile)).astype(jnp.int32)
    nch = jnp.right_shift(counts + (CHUNK - 1), CHUNK.bit_length() - 1)
    er = lax.broadcasted_iota(jnp.int32, (N_EXPERTS, N_EXPERTS), 0)
    ec = lax.broadcasted_iota(jnp.int32, (N_EXPERTS, N_EXPERTS), 1)
    lower = jnp.where(ec < er, 1.0, 0.0).astype(BF16)
    off = jnp.dot(lower, nch.astype(F32).astype(BF16), preferred_element_type=F32) * CHUNK
    pos = off + csum - oh
    lp1 = jnp.sum(jnp.where(hit1, pos, 0.0), axis=0, keepdims=True)
    lp2 = jnp.sum(jnp.where(hit2, pos, 0.0), axis=0, keepdims=True)
    zero = jnp.zeros_like(lp1)
    route_t = jnp.concatenate([lp1, lp2, g_p * w1, g_p * w2, zero, zero, zero, zero], axis=0)
    meta = jnp.concatenate([nch[:, 0:LANES], off[:, 0:LANES].astype(jnp.int32)], axis=0)
    return route_t, meta


def _memattn_kernel(x_ref, wq_ref, k_ref, v_ref, wo_ref, g2_ref, b2_ref, wr2_ref, wrh_ref, br_ref, we_ref,
                    o_ref, ob_ref, route_ref, routet_ref, meta_ref, web_ref, wq_b, wo_b):
    web_ref[...] = we_ref[...].astype(BF16)

    @pl.when(jnp.logical_and(pl.program_id(0) == 0, pl.program_id(1) == 0))
    def _():
        _cast_bf16(wq_b, wq_ref)
        _cast_bf16(wo_b, wo_ref)

    def rows_logits(rs):
        x = x_ref[0, rs, :]
        q = jnp.dot(x.astype(BF16), wq_b[...], preferred_element_type=F32)
        q = (q * (MEM_HEAD_DIM ** -0.5)).astype(BF16)
        outs = []
        for h in range(MEM_HEADS):
            sl = slice(h * MEM_HEAD_DIM, (h + 1) * MEM_HEAD_DIM)
            s = _dot_nt(q[:, sl], k_ref[0, :, sl])
            m = jnp.max(s, axis=-1, keepdims=True)
            p = jnp.exp(s - m)
            denom = jnp.sum(p, axis=-1, keepdims=True)
            o = jnp.dot(p.astype(BF16), v_ref[0, :, sl], preferred_element_type=F32)
            outs.append((o / denom).astype(BF16))
        o = jnp.dot(jnp.concatenate(outs, axis=-1), wo_b[...], preferred_element_type=F32)
        x2 = _layer_norm(ALPHA * x + o, g2_ref[...], b2_ref[...])
        o_ref[0, rs, :] = x2
        x2h = x2.astype(BF16)
        ob_ref[0, rs, :] = x2h
        x2l = (x2 - x2h.astype(F32)).astype(BF16)
        hh = jnp.dot(x2h, wr2_ref[...], preferred_element_type=F32)
        return (hh[:, 0:LANES] + hh[:, LANES:2 * LANES]
                + jnp.dot(x2l, wrh_ref[...], preferred_element_type=F32) + br_ref[...])

    ts = x_ref.shape[1]
    groups = [slice(r0, r0 + MOE_TILE) for r0 in range(0, ts, MOE_TILE)]
    logits = rows_logits(slice(0, ts))
    pad = jnp.zeros((LANES - SUBLANES, MOE_TILE), F32)
    for k, rs in enumerate(groups):
        route_t, meta = _route_plan(jnp.transpose(logits[rs, :]))
        routet_ref[k] = route_t
        route_ref[0, rs, :] = jnp.transpose(jnp.concatenate([route_t, pad], axis=0))
        meta_ref[k] = meta


def _memattn(x1, wq, kvm, wo, g2, b2, wr, br, w_expert):
    B, S, D = x1.shape
    ts = MEM_TILE
    assert B * (S // ts) == w_expert.shape[0]
    wes = pl.BlockSpec((1,) + w_expert.shape[1:], lambda b, i: (b * (S // ts) + i, 0, 0))
    per = ts // MOE_TILE
    nt = S // ts
    c = wr * (2.0 ** 16 + 1.0)
    w_high = c - (c - wr)
    wrh = w_high.astype(BF16)
    wr2 = jnp.concatenate([wrh, (wr - w_high).astype(BF16)], axis=1)
    tile = pl.BlockSpec((1, ts, D), lambda b, i: (b, i, 0))
    kspec = pl.BlockSpec((1, MEM_LEN, D), lambda b, i: (b, 0, 0))
    vspec = pl.BlockSpec((1, MEM_LEN, D), lambda b, i: (b, 0, 1))
    return pl.pallas_call(
        _memattn_kernel,
        grid=(B, nt),
        in_specs=[tile, _resident_spec((D, D)), kspec, vspec, _resident_spec((D, D)),
                  _const_spec((1, D)), _const_spec((1, D)),
                  _const_spec((D, 2 * LANES)), _const_spec((D, LANES)), _const_spec((1, LANES)), wes],
        out_specs=[tile, tile, pl.BlockSpec((1, ts, LANES), lambda b, i: (b, i, 0)),
                   pl.BlockSpec((per, SUBLANES, MOE_TILE), lambda b, i: (b * nt + i, 0, 0)),
                   pl.BlockSpec((per, 2 * N_EXPERTS, LANES), lambda b, i: (b * nt + i, 0, 0)), wes],
        out_shape=[jax.ShapeDtypeStruct((B, S, D), F32),
                   jax.ShapeDtypeStruct((B, S, D), BF16),
                   jax.ShapeDtypeStruct((B, S, LANES), F32),
                   jax.ShapeDtypeStruct((B * nt * per, SUBLANES, MOE_TILE), F32),
                   jax.ShapeDtypeStruct((B * nt * per, 2 * N_EXPERTS, LANES), jnp.int32),
                   jax.ShapeDtypeStruct(w_expert.shape, BF16)],
        scratch_shapes=[pltpu.VMEM((D, D), BF16), pltpu.VMEM((D, D), BF16)],
        compiler_params=pltpu.CompilerParams(
            dimension_semantics=("arbitrary", "arbitrary"), vmem_limit_bytes=VMEM_LIMIT),
        name="memattn",
    )(x1, wq, kvm, kvm, wo, g2, b2, wr2, wrh, br, w_expert)


def _gmm_blocks(n_tokens):
    rows = (n_tokens * TOP_K + (n_tokens // MOE_TILE) * N_EXPERTS * (CHUNK - 1)
            + N_EXPERTS * (REGION_ALIGN - CHUNK))
    return -(-rows // GMM_ROWS)


def _plan(meta, T):
    nch = meta[:, :N_EXPERTS, 0]
    n16 = nch * CHUNK
    n_e = jnp.sum(n16, axis=0)
    reg = (n_e + REGION_ALIGN - 1) // REGION_ALIGN * REGION_ALIGN
    gend = jnp.cumsum(reg)
    gbase = gend - reg
    dst = gbase[None, :] + jnp.cumsum(n16, axis=0) - n16
    half_row = jnp.arange(2 * _gmm_blocks(T), dtype=jnp.int32) * REGION_ALIGN
    half_expert = jnp.minimum(jnp.sum(half_row[:, None] >= gend[None, :], axis=1), N_EXPERTS - 1)
    half_used = half_row < gend[-1]
    n_used = ((gend[-1] + GMM_ROWS - 1) // GMM_ROWS).astype(jnp.int32).reshape(1)
    src = (jnp.cumsum(nch, axis=1) - nch) * CHUNK
    n_big = nch // BIG

    def copy_list(count, src0, dst0, rows, length):
        cum = jnp.cumsum(count, axis=1)
        first = (cum - count)[:, None, :]
        k = jnp.arange(length, dtype=jnp.int32)[None, :, None]
        mine = (k >= first) & (k < cum[:, None, :])
        step = (k - first) * rows
        pick = lambda base: jnp.sum(jnp.where(mine, base[:, None, :] + step, 0), axis=2)
        return pick(src0), pick(dst0), cum[:, -1]

    big_src, big_dst, big_n = copy_list(n_big, src, dst, BIG * CHUNK, MAX_BIG)
    rest = n_big * (BIG * CHUNK)
    small_src, small_dst, small_n = copy_list(nch - n_big * BIG, src + rest, dst + rest, CHUNK, MAX_SMALL)
    i32 = lambda a: a.astype(jnp.int32)
    fill_start = jnp.concatenate([gbase + n_e, gend[-1:]])
    fill_n = jnp.concatenate([reg - n_e, _gmm_blocks(T) * GMM_ROWS - gend[-1:]]) // CHUNK
    return dict(big_src=i32(big_src).reshape(-1), big_dst=i32(big_dst).reshape(-1), big_n=i32(big_n),
                small_src=i32(small_src).reshape(-1), small_dst=i32(small_dst).reshape(-1), small_n=i32(small_n),
                fill_start=i32(fill_start), fill_n=i32(fill_n),
                fill_tot=i32(jnp.sum(fill_n)).reshape(1), half_expert=i32(half_expert),
                half_used=i32(half_used), n_used=n_used)


def _rows_copy(src_ref, src_row, dst_ref, dst_row, rows, sem):
    return pltpu.make_async_copy(
        src_ref.at[pl.ds(pl.multiple_of(src_row, CHUNK), rows), :],
        dst_ref.at[pl.ds(pl.multiple_of(dst_row, CHUNK), rows), :], sem)


def _chunk_copy(src_ref, src_row, dst_ref, dst_row, sem):
    return _rows_copy(src_ref, src_row, dst_ref, dst_row, CHUNK, sem)


def _issue_copies(lists, tile, tile_ref, tile_is_src, hbm_ref, sem):
    big_src, big_dst, big_n, small_src, small_dst, small_n = lists
    for src_l, dst_l, n_l, length, rows in ((big_src, big_dst, big_n, MAX_BIG, BIG * CHUNK),
                                            (small_src, small_dst, small_n, MAX_SMALL, CHUNK)):
        def issue(k, priority, src_l=src_l, dst_l=dst_l, length=length, rows=rows):
            local, remote = src_l[tile * length + k], dst_l[tile * length + k]
            if tile_is_src:
                _rows_copy(tile_ref, local, hbm_ref, remote, rows, sem).start(priority=priority)
            else:
                _rows_copy(hbm_ref, remote, tile_ref, local, rows, sem).start(priority=priority)

        n = n_l[tile]

        def pair(j, carry, issue=issue):
            issue(2 * j, 0)
            issue(2 * j + 1, 1)
            return carry

        lax.fori_loop(0, lax.shift_right_logical(n, 1), pair, 0)

        @pl.when((n & 1) == 1)
        def _(issue=issue, n=n):
            issue(n - 1, 0)


def _wait_rows(n, rows, src_ref, dst_ref, sem):
    def body(c, carry):
        pltpu.make_async_copy(src_ref.at[pl.ds(0, rows), :], dst_ref.at[pl.ds(0, rows), :], sem).wait()
        return carry

    lax.fori_loop(0, n, body, 0)


def _wait_copies(lists, tile, src_ref, dst_ref, sem, extra_chunks=0):
    _wait_rows(lists[2][tile], BIG * CHUNK, src_ref, dst_ref, sem)
    n = lists[5][tile] + extra_chunks
    _wait_rows(n // WAIT_GROUP, WAIT_GROUP * CHUNK, src_ref, dst_ref, sem)
    _wait_rows(n % WAIT_GROUP, CHUNK, src_ref, dst_ref, sem)


def _dispatch_kernel(bs_ref, bd_ref, bn_ref, ss_ref, sd_ref, sn_ref, fstart_ref, fn_ref, ftot_ref,
                     lp_ref, x_ref, we_ref, xs_hbm, web_ref, xt, zbuf, sems):
    lists = (bs_ref, bd_ref, bn_ref, ss_ref, sd_ref, sn_ref)
    s = pl.program_id(0)
    ns = pl.num_programs(0)
    cur = (s % 2) * PER_STEP
    prv = PER_STEP - cur

    @pl.when(s >= 2)
    def _():
        for k in range(PER_STEP):
            _wait_copies(lists, (s - 2) * PER_STEP + k, xt.at[cur + k], xs_hbm, sems.at[cur + k])

    r = lax.broadcasted_iota(jnp.int32, (SORT_ROWS, MOE_TILE), 0)
    for k in range(PER_STEP):
        lp = lp_ref[k].astype(jnp.int32)
        hit = jnp.logical_or(lp[0:1, :] == r, lp[1:2, :] == r)
        p = jnp.where(hit, 1.0, 0.0).astype(BF16)
        x = x_ref[k * MOE_TILE:(k + 1) * MOE_TILE, :]
        xt[cur + k] = jnp.dot(p, x, preferred_element_type=F32).astype(BF16)

    for k in range(PER_STEP):
        _issue_copies(lists, s * PER_STEP + k, xt.at[cur + k], True, xs_hbm, sems.at[cur + k])

    _cast_bf16(web_ref.at[0], we_ref.at[0])

    @pl.when(s == ns - 1)
    def _():
        zbuf[...] = jnp.zeros_like(zbuf)
        sem = sems.at[cur]

        def per_range(e, carry):
            def issue(c, carry2):
                _chunk_copy(zbuf, 0, xs_hbm, fstart_ref[e] + c * CHUNK, sem).start()
                return carry2

            return lax.fori_loop(0, fn_ref[e], issue, carry)

        lax.fori_loop(0, N_EXPERTS + 1, per_range, 0)
        for k in range(PER_STEP):
            _wait_copies(lists, (s - 1) * PER_STEP + k, xt.at[prv + k], xs_hbm, sems.at[prv + k])
            _wait_copies(lists, s * PER_STEP + k, xt.at[cur + k], xs_hbm, sems.at[cur + k],
                         extra_chunks=ftot_ref[0] if k == 0 else 0)


def _copy_lists(plan):
    return tuple(plan[k] for k in ("big_src", "big_dst", "big_n", "small_src", "small_dst", "small_n"))


def _dispatch(plan, lpt, x2b, w_expert):
    T, D = x2b.shape
    rows = PER_STEP * MOE_TILE
    assert T // rows >= 2
    assert T // rows == N_EXPERTS
    wes = pl.BlockSpec((1,) + w_expert.shape[1:], lambda t, *_: (t, 0, 0))
    grid_spec = pltpu.PrefetchScalarGridSpec(
        num_scalar_prefetch=9,
        grid=(T // rows,),
        in_specs=[pl.BlockSpec((PER_STEP, SUBLANES, MOE_TILE), lambda t, *_: (t, 0, 0)),
                  pl.BlockSpec((rows, D), lambda t, *_: (t, 0)), wes],
        out_specs=[pl.BlockSpec(memory_space=pl.ANY), wes],
        scratch_shapes=[pltpu.VMEM((2 * PER_STEP, SORT_ROWS, D), BF16), pltpu.VMEM((CHUNK, D), BF16),
                        pltpu.SemaphoreType.DMA((2 * PER_STEP,))],
    )
    return pl.pallas_call(
        _dispatch_kernel,
        grid_spec=grid_spec,
        out_shape=[jax.ShapeDtypeStruct((_gmm_blocks(T) * GMM_ROWS, D), BF16),
                   jax.ShapeDtypeStruct(w_expert.shape, BF16)],
        compiler_params=pltpu.CompilerParams(
            dimension_semantics=("arbitrary",), vmem_limit_bytes=VMEM_LIMIT),
        name="moe_dispatch",
    )(*_copy_lists(plan), plan["fill_start"], plan["fill_n"], plan["fill_tot"], lpt, x2b, w_expert)


def _expert_ffn(xb, wg_b, wu_b, wd_b):
    g = jnp.dot(xb, wg_b[...], preferred_element_type=F32)
    u = jnp.dot(xb, wu_b[...], preferred_element_type=F32)
    h = (g * jax.nn.sigmoid(g)) * u
    return jnp.dot(h.astype(BF16), wd_b[...], preferred_element_type=F32).astype(BF16)


def _gmm_kernel(he_ref, hu_ref, nu_ref, x_ref, wga_ref, wua_ref, wda_ref, wgb_ref, wub_ref, wdb_ref, o_ref):
    b = pl.program_id(0)
    half = REGION_ALIGN
    same = he_ref[2 * b] == he_ref[2 * b + 1]
    used0 = hu_ref[2 * b] != 0
    used1 = hu_ref[2 * b + 1] != 0
    set_a = (wga_ref.at[0], wua_ref.at[0], wda_ref.at[0])
    set_b = (wgb_ref.at[0], wub_ref.at[0], wdb_ref.at[0])

    @pl.when(jnp.logical_and(used1, same))
    def _():
        o_ref[...] = _expert_ffn(x_ref[...], *set_a)

    @pl.when(jnp.logical_and(used0, jnp.logical_not(jnp.logical_and(used1, same))))
    def _():
        o_ref[0:half, :] = _expert_ffn(x_ref[0:half, :], *set_a)

    @pl.when(jnp.logical_and(used1, jnp.logical_not(same)))
    def _():
        o_ref[half:GMM_ROWS, :] = _expert_ffn(x_ref[half:GMM_ROWS, :], *set_b)

    @pl.when(jnp.logical_not(used0))
    def _():
        o_ref[0:half, :] = jnp.zeros((half, o_ref.shape[1]), o_ref.dtype)

    @pl.when(jnp.logical_not(used1))
    def _():
        o_ref[half:GMM_ROWS, :] = jnp.zeros((GMM_ROWS - half, o_ref.shape[1]), o_ref.dtype)


def _gmm(plan, xs, wg, wu, wd):
    R, D = xs.shape
    rows = pl.BlockSpec((GMM_ROWS, D), lambda b, he, hu, nu: (jnp.minimum(b, nu[0] - 1), 0))

    def wspec(shape, h):
        return pl.BlockSpec((1,) + shape, lambda b, he, hu, nu: (he[2 * b + h], 0, 0))

    grid_spec = pltpu.PrefetchScalarGridSpec(
        num_scalar_prefetch=3,
        grid=(R // GMM_ROWS,),
        in_specs=[rows,
                  wspec((D, D_EXPERT), 0), wspec((D, D_EXPERT), 0), wspec((D_EXPERT, D), 0),
                  wspec((D, D_EXPERT), 1), wspec((D, D_EXPERT), 1), wspec((D_EXPERT, D), 1)],
        out_specs=pl.BlockSpec((GMM_ROWS, D), lambda b, he, hu, nu: (b, 0)),
    )
    return pl.pallas_call(
        _gmm_kernel,
        grid_spec=grid_spec,
        out_shape=jax.ShapeDtypeStruct((R, D), BF16),
        compiler_params=pltpu.CompilerParams(
            dimension_semantics=("arbitrary",), vmem_limit_bytes=VMEM_LIMIT),
        name="moe_gmm",
    )(plan["half_expert"], plan["half_used"], plan["n_used"], xs, wg, wu, wd, wg, wu, wd)


def _combine_kernel(bs_ref, bd_ref, bn_ref, ss_ref, sd_ref, sn_ref, cm_ref, x_ref, ys_hbm, g3_ref, b3_ref,
                    o_ref, yt, sems):
    lists = (bs_ref, bd_ref, bn_ref, ss_ref, sd_ref, sn_ref)
    s = pl.program_id(0)
    ns = pl.num_programs(0)
    cur = (s % 2) * PER_STEP
    nxt = PER_STEP - cur

    def fetch(step, first_buf):
        for k in range(PER_STEP):
            _issue_copies(lists, step * PER_STEP + k, yt.at[first_buf + k], False, ys_hbm,
                          sems.at[first_buf + k])

    @pl.when(s == 0)
    def _():
        yt[...] = jnp.zeros_like(yt)
        fetch(0, 0)

    @pl.when(s + 1 < ns)
    def _():
        fetch(s + 1, nxt)

    for k in range(PER_STEP):
        _wait_copies(lists, s * PER_STEP + k, ys_hbm, yt.at[cur + k], sems.at[cur + k])

    col = lax.broadcasted_iota(jnp.int32, (MOE_TILE, SORT_ROWS), 1)
    for k in range(PER_STEP):
        rs = slice(k * MOE_TILE, (k + 1) * MOE_TILE)
        cm = cm_ref[rs, :]
        lp0 = cm[:, 0:1].astype(jnp.int32)
        lp1 = cm[:, 1:2].astype(jnp.int32)
        w = jnp.where(col == lp0, cm[:, 2:3], jnp.where(col == lp1, cm[:, 3:4], 0.0)).astype(BF16)
        y = jnp.dot(w, yt[cur + k], preferred_element_type=F32)
        o_ref[rs, :] = _layer_norm(ALPHA * x_ref[rs, :] + y, g3_ref[...], b3_ref[...])


def _combine(plan, route, x2, ys, g3, b3):
    T, D = x2.shape
    rows = PER_STEP * MOE_TILE
    grid_spec = pltpu.PrefetchScalarGridSpec(
        num_scalar_prefetch=6,
        grid=(T // rows,),
        in_specs=[pl.BlockSpec((rows, LANES), lambda t, *_: (t, 0)),
                  pl.BlockSpec((rows, D), lambda t, *_: (t, 0)),
                  pl.BlockSpec(memory_space=pl.ANY),
                  pl.BlockSpec((1, D), lambda t, *_: (0, 0)),
                  pl.BlockSpec((1, D), lambda t, *_: (0, 0))],
        out_specs=pl.BlockSpec((rows, D), lambda t, *_: (t, 0)),
        scratch_shapes=[pltpu.VMEM((2 * PER_STEP, SORT_ROWS, D), BF16),
                        pltpu.SemaphoreType.DMA((2 * PER_STEP,))],
    )
    return pl.pallas_call(
        _combine_kernel,
        grid_spec=grid_spec,
        out_shape=jax.ShapeDtypeStruct((T, D), F32),
        compiler_params=pltpu.CompilerParams(
            dimension_semantics=("arbitrary",), vmem_limit_bytes=VMEM_LIMIT),
        name="moe_combine",
    )(*_copy_lists(plan), route, x2, ys, g3, b3)


def _moe(x2, x2b, route, lpt, meta, wg, wu_b, wd_b, g3, b3):
    plan = _plan(meta, x2.shape[0])
    xs, wg_b = _dispatch(plan, lpt, x2b, wg)
    ys = _gmm(plan, xs, wg_b, wu_b, wd_b)
    return _combine(plan, route, x2, ys, g3, b3)


def _row(v):
    return v.reshape(1, -1).astype(F32)


def kernel(x, mem, w_in, b_in, w_dw, b_dw, g_conv_norm, b_conv_norm, attn_sinks, w_out, g_ln1, b_ln1,
           w_mq, w_mkv, w_mo, g_ln2, b_ln2, w_group, b_group, w_router, b_router, w_gate, w_up, w_down,
           g_ln3, b_ln3):
    B, S, D = x.shape
    for l in range(DEPTH):
        w_dw_p = jnp.zeros((CONV_HALO, CONV_CH), F32).at[:CONV_WIDTH].set(w_dw[l])
        x1, wd_b = _mixer(x, attn_sinks[l].astype(F32), w_in[l], _row(b_in[l]), w_dw_p,
                          _row(b_dw[l]), _row(g_conv_norm[l]), _row(b_conv_norm[l]),
                          w_out[l], _row(g_ln1[l]), _row(b_ln1[l]), w_down[l])

        kvm = _memkv(mem.reshape(B * MEM_LEN, D), w_mkv[l]).reshape(B, MEM_LEN, 2 * D)

        wr = jnp.concatenate(
            [w_group[l], jnp.transpose(w_router[l], (1, 0, 2)).reshape(D, N_EXPERTS)], axis=1)
        wr = jnp.pad(wr, ((0, 0), (0, LANES - wr.shape[1])))
        br = jnp.pad(jnp.concatenate([b_group[l], b_router[l].reshape(-1)]), (0, LANES - N_GROUPS - N_EXPERTS))
        x2, x2b, route, lpt, meta, wu_b = _memattn(x1, w_mq[l], kvm, w_mo[l], _row(g_ln2[l]), _row(b_ln2[l]),
                                                   wr.astype(F32), _row(br), w_up[l])

        T = B * S
        y = _moe(x2.reshape(T, D), x2b.reshape(T, D), route.reshape(T, LANES), lpt, meta,
                 w_gate[l], wu_b, wd_b,
                 _row(g_ln3[l]), _row(b_ln3[l]))
        x = y.reshape(B, S, D)
    return x
```
